```python
import jax, jax.numpy as jnp
from jax import lax
import numpy as np

D_MODEL = 1024
BATCH = 4
SEQ = 4096
DEPTH = 1

PLE_DIM = 256
EPS = 1e-6
N_HEADS = 8
N_KV_HEADS = 2
HEAD_DIM = 64
ATTN_WIDTH = N_HEADS * HEAD_DIM
KV_WIDTH = N_KV_HEADS * HEAD_DIM
WINDOW = 128
BLOCK = WINDOW
ROPE_THETA = 10000.0
LRU_WIDTH = D_MODEL - ATTN_WIDTH
LRU_BLOCKS = 8
LRU_BLOCK_DIM = LRU_WIDTH // LRU_BLOCKS
CONV_WIDTH = 4
LRU_C = 8.0
MIX_WIDTH = ATTN_WIDTH + LRU_WIDTH
OFF_K = ATTN_WIDTH
OFF_V = OFF_K + KV_WIDTH
OFF_LX = OFF_V + KV_WIDTH
OFF_LG = OFF_LX + LRU_WIDTH
IN_PROJ_WIDTH = OFF_LG + LRU_WIDTH
N_GROUPS = 4
EXPERTS_PER_GROUP = 8
TOP_K = 2
EXPERT_FF = 512
NEG_INF = -1e30

kernel_name = "hymba_swa_rglru_hmoe_layer"


def rmsnorm(x, g):
    xf = x.astype(jnp.float32)
    y = xf * lax.rsqrt(jnp.mean(xf * xf, axis=-1, keepdims=True) + EPS)
    return (y * g.astype(jnp.float32)).astype(x.dtype)


def rope_tables(positions):
    inv_freq = ROPE_THETA ** (-jnp.arange(0, HEAD_DIM, 2, dtype=jnp.float32) / HEAD_DIM)
    ang = positions.astype(jnp.float32)[..., None] * inv_freq
    return jnp.cos(ang)[:, :, None, :], jnp.sin(ang)[:, :, None, :]


def apply_rope(t, cos, sin):
    tf = t.astype(jnp.float32)
    t1, t2 = tf[..., : HEAD_DIM // 2], tf[..., HEAD_DIM // 2:]
    out = jnp.concatenate([t1 * cos - t2 * sin, t2 * cos + t1 * sin], axis=-1)
    return out.astype(t.dtype)


def sliding_window_attention(q, k, v, sinks):
    B, S, H, Dh = q.shape
    nb = S // BLOCK
    grp = H // N_KV_HEADS
    qb = q.reshape(B, nb, BLOCK, N_KV_HEADS, grp, Dh)

    def with_prev_block(t):
        prev = jnp.pad(t, ((0, 0), (BLOCK, 0), (0, 0), (0, 0)))[:, :S]
        prev = prev.reshape(B, nb, BLOCK, N_KV_HEADS, Dh)
        cur = t.reshape(B, nb, BLOCK, N_KV_HEADS, Dh)
        return jnp.concatenate([prev, cur], axis=2)

    kb = with_prev_block(k)
    vb = with_prev_block(v)
    scale = HEAD_DIM ** -0.5
    scores = jnp.einsum('bnqkgd,bnskd->bnkgqs', qb, kb,
                        preferred_element_type=jnp.float32) * scale
    qi = jnp.arange(BLOCK)[:, None]
    kj = jnp.arange(2 * BLOCK)[None, :]
    rel = qi + BLOCK - kj
    band = (rel >= 0) & (rel < WINDOW)
    block_ok = (jnp.arange(nb)[:, None, None] > 0) | (kj >= BLOCK)[None]
    mask = band[None] & block_ok
    scores = jnp.where(mask[None, :, None, None], scores, NEG_INF)
    sink = sinks.astype(jnp.float32).reshape(N_KV_HEADS, grp)[None, None, :, :, None, None]
    sink = jnp.broadcast_to(sink, scores.shape[:-1] + (1,))
    probs = jax.nn.softmax(jnp.concatenate([scores, sink], axis=-1), axis=-1)[..., :-1]
    out = jnp.einsum('bnkgqs,bnskd->bnqkgd', probs.astype(vb.dtype), vb)
    return out.reshape(B, S, H * Dh)


def causal_depthwise_conv(x, w, b):
    y = lax.conv_general_dilated(
        x, w[:, None, :], window_strides=(1,), padding=((CONV_WIDTH - 1, 0),),
        dimension_numbers=('NWC', 'WIO', 'NWC'), feature_group_count=x.shape[-1])
    return y + b


def rg_lru(x, w_a, b_a, w_x, b_x, lam):
    B, S, C = x.shape
    xb = x.reshape(B, S, LRU_BLOCKS, LRU_BLOCK_DIM)
    gate_a = jnp.einsum('bshi,hij->bshj', xb, w_a).reshape(B, S, C) + b_a
    gate_x = jnp.einsum('bshi,hij->bshj', xb, w_x).reshape(B, S, C) + b_x
    r = jax.nn.sigmoid(gate_a.astype(jnp.float32))
    i = jax.nn.sigmoid(gate_x.astype(jnp.float32))
    log_a = -LRU_C * r * jax.nn.softplus(-lam.astype(jnp.float32))
    a = jnp.exp(log_a)
    mult = jnp.sqrt(-jnp.expm1(2.0 * log_a))
    bterm = mult * i * x.astype(jnp.float32)

    def combine(left, right):
        a1, b1 = left
        a2, b2 = right
        return a1 * a2, a2 * b1 + b2

    _, h = lax.associative_scan(combine, (a, bterm), axis=1)
    return h.astype(x.dtype)


def hierarchical_moe(h, w_rg, b_rg, w_re, b_re, w_gate, w_up, w_down):
    B, S, D = h.shape
    T = B * S
    t = h.reshape(T, D)
    g_prob = jax.nn.softmax((t @ w_rg).astype(jnp.float32) + b_rg.astype(jnp.float32), axis=-1)
    g_top_p, g_top = lax.top_k(g_prob, 1)
    e_logits = ((t @ w_re).astype(jnp.float32) + b_re.astype(jnp.float32)).reshape(
        T, N_GROUPS, EXPERTS_PER_GROUP)
    e_sel = jnp.take_along_axis(e_logits, g_top[:, :, None], axis=1)[:, 0]
    e_prob = jax.nn.softmax(e_sel, axis=-1)
    e_top_p, e_top = lax.top_k(e_prob, TOP_K)
    e_top_p = e_top_p / jnp.sum(e_top_p, axis=-1, keepdims=True)
    w_tok = g_top_p * e_top_p
    expert_w = jnp.sum(jax.nn.one_hot(e_top, EXPERTS_PER_GROUP, dtype=jnp.float32)
                       * w_tok[..., None], axis=1)
    combine = (jax.nn.one_hot(g_top[:, 0], N_GROUPS, dtype=jnp.float32)[:, :, None]
               * expert_w[:, None, :])
    out = jnp.zeros((T, D), dtype=h.dtype)
    for g in range(N_GROUPS):
        a = jnp.einsum('td,edf->tef', t, w_gate[g])
        u = jnp.einsum('td,edf->tef', t, w_up[g])
        hid = jax.nn.silu(a) * u * combine[:, g, :, None].astype(t.dtype)
        out = out + jnp.einsum('tef,efd->td', hid, w_down[g])
    return out.reshape(B, S, D)


def setup_inputs(seed: int = 0) -> dict:
    key = jax.random.key(seed)
    ks = jax.random.split(key, 32)
    f32 = jnp.float32

    def nrm(k, shape, scale):
        return jax.random.normal(k, shape, f32) * scale

    def gain(k, shape):
        return 1.0 + 0.01 * jax.random.normal(k, shape, f32)

    L, D, G, E, F = DEPTH, D_MODEL, N_GROUPS, EXPERTS_PER_GROUP, EXPERT_FF
    u = jax.random.uniform(ks[14], (L, LRU_WIDTH), f32, minval=0.9, maxval=0.999)
    s = u ** (1.0 / LRU_C)
    lru_lambda = jnp.log(s) - jnp.log1p(-s)
    positions = (jnp.arange(SEQ, dtype=jnp.int32)[None, :]
                 + jax.random.randint(ks[2], (BATCH, 1), 0, 1024, dtype=jnp.int32))
    return {
        "x": nrm(ks[0], (BATCH, SEQ, D), 1.0),
        "p": nrm(ks[1], (L, BATCH, SEQ, PLE_DIM), 1.0),
        "positions": positions,
        "g_mix": gain(ks[3], (L, D)),
        "w_in": nrm(ks[4], (L, D, IN_PROJ_WIDTH), D ** -0.5),
        "sinks": nrm(ks[5], (L, N_HEADS), 0.5),
        "conv_w": nrm(ks[6], (L, CONV_WIDTH, LRU_WIDTH), CONV_WIDTH ** -0.5),
        "conv_b": nrm(ks[7], (L, LRU_WIDTH), 0.01),
        "lru_wa": nrm(ks[8], (L, LRU_BLOCKS, LRU_BLOCK_DIM, LRU_BLOCK_DIM), LRU_BLOCK_DIM ** -0.5),
        "lru_ba": nrm(ks[9], (L, LRU_WIDTH), 0.01),
        "lru_wx": nrm(ks[10], (L, LRU_BLOCKS, LRU_BLOCK_DIM, LRU_BLOCK_DIM), LRU_BLOCK_DIM ** -0.5),
        "lru_bx": nrm(ks[11], (L, LRU_WIDTH), 0.01),
        "lru_lambda": lru_lambda,
        "g_attn_out": gain(ks[12], (L, ATTN_WIDTH)),
        "g_lru_out": gain(ks[13], (L, LRU_WIDTH)),
        "w_out": nrm(ks[15], (L, MIX_WIDTH, D), MIX_WIDTH ** -0.5),
        "g_ffn": gain(ks[16], (L, D)),
        "w_router_group": nrm(ks[17], (L, D, G), D ** -0.5),
        "b_router_group": nrm(ks[18], (L, G), 0.01),
        "w_router_expert": nrm(ks[19], (L, D, G * E), D ** -0.5),
        "b_router_expert": nrm(ks[20], (L, G * E), 0.01),
        "w_expert_gate": nrm(ks[21], (L, G, E, D, F), D ** -0.5),
        "w_expert_up": nrm(ks[22], (L, G, E, D, F), D ** -0.5),
        "w_expert_down": nrm(ks[23], (L, G, E, F, D), F ** -0.5),
        "g_ple": gain(ks[24], (L, D)),
        "w_ple_gate": nrm(ks[25], (L, D, D), D ** -0.5),
        "w_ple_proj": nrm(ks[26], (L, PLE_DIM, D), PLE_DIM ** -0.5),
        "g_final": gain(ks[27], (D,)),
    }


def reference(x, p, positions, g_mix, w_in, sinks, conv_w, conv_b, lru_wa, lru_ba,
              lru_wx, lru_bx, lru_lambda, g_attn_out, g_lru_out, w_out, g_ffn,
              w_router_group, b_router_group, w_router_expert, b_router_expert,
              w_expert_gate, w_expert_up, w_expert_down, g_ple, w_ple_gate, w_ple_proj,
              g_final):
    B, S, _ = x.shape
    cos, sin = rope_tables(positions)
    for i in range(DEPTH):
        h = rmsnorm(x, g_mix[i])
        proj = h @ w_in[i]
        q = proj[..., :OFF_K].reshape(B, S, N_HEADS, HEAD_DIM)
        k = proj[..., OFF_K:OFF_V].reshape(B, S, N_KV_HEADS, HEAD_DIM)
        v = proj[..., OFF_V:OFF_LX].reshape(B, S, N_KV_HEADS, HEAD_DIM)
        lx = proj[..., OFF_LX:OFF_LG]
        lg = proj[..., OFF_LG:]
        q = apply_rope(q, cos, sin)
        k = apply_rope(k, cos, sin)
        attn = sliding_window_attention(q, k, v, sinks[i])
        lx = causal_depthwise_conv(lx, conv_w[i], conv_b[i])
        lru = rg_lru(lx, lru_wa[i], lru_ba[i], lru_wx[i], lru_bx[i], lru_lambda[i])
        lru = lru * jax.nn.gelu(lg)
        mix = jnp.concatenate([rmsnorm(attn, g_attn_out[i]), rmsnorm(lru, g_lru_out[i])], axis=-1)
        x = x + mix @ w_out[i]
        h2 = rmsnorm(x, g_ffn[i])
        x = x + hierarchical_moe(h2, w_router_group[i], b_router_group[i],
                                 w_router_expert[i], b_router_expert[i],
                                 w_expert_gate[i], w_expert_up[i], w_expert_down[i])
        gate = jax.nn.sigmoid(rmsnorm(x, g_ple[i]) @ w_ple_gate[i])
        x = x + gate * (p[i] @ w_ple_proj[i])
    return rmsnorm(x, g_final)
```

```python
import functools
import math

import jax
import jax.numpy as jnp
from jax import lax
from jax.experimental import pallas as pl
from jax.experimental.pallas import tpu as pltpu

F32 = jnp.float32
BF16 = jnp.bfloat16
I32 = jnp.int32

EPS = 1e-6
N_HEADS = 8
N_KV_HEADS = 2
HEAD_DIM = 64
WINDOW = 128
ROPE_THETA = 10000.0
CONV_WIDTH = 4
LRU_C = 8.0
N_GROUPS = 4
EXPERTS_PER_GROUP = 8
N_EXPERTS = N_GROUPS * EXPERTS_PER_GROUP
NEG_INF = -1e30
LANES = 128
SUBLANES = 8

TM_PROJ = 512
TL_LRU = 256
TM_PLAN = 512
TM_EXPERT = 256
TM_DISPATCH = 512
TM_COMBINE = 256
VMEM_LIMIT = 56 * 1024 * 1024


def _rms(x, g):
    ms = jnp.mean(x * x, axis=-1, keepdims=True)
    return x * lax.rsqrt(ms + EPS) * g


def _cparams(sem):
    return pltpu.CompilerParams(dimension_semantics=sem, vmem_limit_bytes=VMEM_LIMIT)


def _inproj_kernel(x_ref, pos_ref, g_ref, w_ref, q_ref, k_ref, v_ref, lx_ref, lg_ref):
    h = _rms(x_ref[...], g_ref[...]).astype(BF16)
    proj = jnp.dot(h, w_ref[...], preferred_element_type=F32)
    lane = lax.broadcasted_iota(I32, (1, LANES), 1)
    fidx = (lane % (HEAD_DIM // 2)).astype(F32)
    inv_freq = jnp.exp(fidx * (-2.0 / HEAD_DIM * math.log(ROPE_THETA)))
    ang = pos_ref[...].astype(F32) * inv_freq
    cos = jnp.cos(ang)
    sin = jnp.sin(ang)
    first_half = (lane % HEAD_DIM) < (HEAD_DIM // 2)
    sin_signed = jnp.where(first_half, -sin, sin)

    def rope(t):
        partner = jnp.where(first_half, pltpu.roll(t, LANES - HEAD_DIM // 2, 1),
                            pltpu.roll(t, HEAD_DIM // 2, 1))
        return t * cos + partner * sin_signed

    attn_w = N_HEADS * HEAD_DIM
    for c in range(attn_w // LANES):
        q_ref[:, c * LANES:(c + 1) * LANES] = rope(proj[:, c * LANES:(c + 1) * LANES]).astype(BF16)
    kv_w = N_KV_HEADS * HEAD_DIM
    k_ref[...] = rope(proj[:, attn_w:attn_w + kv_w]).astype(BF16)
    v_ref[...] = proj[:, attn_w + kv_w:attn_w + 2 * kv_w].astype(BF16)
    off_lx = attn_w + 2 * kv_w
    lru_w = lx_ref.shape[1]
    lx_ref[...] = proj[:, off_lx:off_lx + lru_w]
    lg_ref[...] = proj[:, off_lx + lru_w:off_lx + 2 * lru_w]


def _inproj(x2, pos2, g_mix, w_in_b, lru_w):
    T, D = x2.shape
    tm = TM_PROJ
    attn_w = N_HEADS * HEAD_DIM
    kv_w = N_KV_HEADS * HEAD_DIM
    row = lambda i: (i, 0)
    const = lambda i: (0, 0)
    return pl.pallas_call(
        _inproj_kernel,
        grid=(T // tm,),
        in_specs=[pl.BlockSpec((tm, D), row), pl.BlockSpec((tm, 1), row),
                  pl.BlockSpec((1, D), const), pl.BlockSpec(w_in_b.shape, const)],
        out_specs=[pl.BlockSpec((tm, attn_w), row), pl.BlockSpec((tm, kv_w), row),
                   pl.BlockSpec((tm, kv_w), row), pl.BlockSpec((tm, lru_w), row),
                   pl.BlockSpec((tm, lru_w), row)],
        out_shape=[jax.ShapeDtypeStruct((T, attn_w), BF16), jax.ShapeDtypeStruct((T, kv_w), BF16),
                   jax.ShapeDtypeStruct((T, kv_w), BF16), jax.ShapeDtypeStruct((T, lru_w), F32),
                   jax.ShapeDtypeStruct((T, lru_w), F32)],
        compiler_params=_cparams(("arbitrary",)),
        name="inproj",
    )(x2, pos2, g_mix, w_in_b)


def _attn_kernel(sinks_ref, q_ref, kc_ref, kp_ref, vc_ref, vp_ref, g_ref, o_ref):
    n = pl.program_id(1)
    blk = WINDOW
    q = q_ref[...]
    kk = jnp.concatenate([kp_ref[...], kc_ref[...]], axis=0)
    vv = jnp.concatenate([vp_ref[...], vc_ref[...]], axis=0)
    qi = lax.broadcasted_iota(I32, (blk, 2 * blk), 0)
    kj = lax.broadcasted_iota(I32, (blk, 2 * blk), 1)
    rel = qi + blk - kj
    mask = (rel >= 0) & (rel < WINDOW) & ((kj >= blk) | (n > 0))
    scale = HEAD_DIM ** -0.5
    grp = N_HEADS // N_KV_HEADS
    outs = []
    for h in range(N_HEADS):
        g = h // grp
        qh = q[:, h * HEAD_DIM:(h + 1) * HEAD_DIM]
        kh = kk[:, g * HEAD_DIM:(g + 1) * HEAD_DIM]
        vh = vv[:, g * HEAD_DIM:(g + 1) * HEAD_DIM]
        s = lax.dot_general(qh, kh, (((1,), (1,)), ((), ())), preferred_element_type=F32) * scale
        s = jnp.where(mask, s, NEG_INF)
        sink = sinks_ref[h]
        m = jnp.maximum(jnp.max(s, axis=-1, keepdims=True), sink)
        e = jnp.exp(s - m)
        den = jnp.sum(e, axis=-1, keepdims=True) + jnp.exp(sink - m)
        p = (e / den).astype(BF16)
        outs.append(jnp.dot(p, vh, preferred_element_type=F32))
    attn = jnp.concatenate(outs, axis=1)
    o_ref[...] = _rms(attn, g_ref[...]).astype(BF16)


def _attention(q, k, v, sinks, g_attn, B, S):
    T, attn_w = q.shape
    kv_w = k.shape[1]
    blk = WINDOW
    nb = S // blk
    cur = lambda b, n: (b * nb + n, 0)
    prev = lambda b, n: (b * nb + jnp.maximum(n - 1, 0), 0)
    return pl.pallas_call(
        _attn_kernel,
        grid=(B, nb),
        in_specs=[pl.BlockSpec(memory_space=pltpu.SMEM),
                  pl.BlockSpec((blk, attn_w), cur),
                  pl.BlockSpec((blk, kv_w), cur), pl.BlockSpec((blk, kv_w), prev),
                  pl.BlockSpec((blk, kv_w), cur), pl.BlockSpec((blk, kv_w), prev),
                  pl.BlockSpec((1, attn_w), lambda b, n: (0, 0))],
        out_specs=pl.BlockSpec((blk, attn_w), cur),
        out_shape=jax.ShapeDtypeStruct((T, attn_w), BF16),
        compiler_params=_cparams(("arbitrary", "arbitrary")),
        name="attn",
    )(sinks, q, k, k, v, v, g_attn)


def _neg_expm1(x):
    series = -x * (1.0 + x * (0.5 + x * (1.0 / 6.0 + x * (1.0 / 24.0 + x * (1.0 / 120.0)))))
    return jnp.where(x > -0.05, series, 1.0 - jnp.exp(x))


def _gelu_tanh(x):
    return 0.5 * x * (1.0 + jnp.tanh(math.sqrt(2.0 / math.pi) * (x + 0.044715 * (x * x * x))))


def _lru_kernel(lx_ref, lg_ref, cw_ref, cb_ref, wa_ref, ba_ref, wx_ref, bx_ref, lam_ref, g_ref,
                o_ref, xpad, a_scr, b_scr, h_scr, h_carry):
    i = pl.program_id(0)
    B, tl, C = lx_ref.shape
    pad = SUBLANES

    @pl.when(i == 0)
    def _():
        xpad[:, 0:pad, :] = jnp.zeros((B, pad, C), F32)
        h_carry[...] = jnp.zeros_like(h_carry)

    @pl.when(i > 0)
    def _():
        xpad[:, 0:pad, :] = xpad[:, tl:tl + pad, :]

    xpad[:, pad:, :] = lx_ref[...]
    y = jnp.zeros((B, tl, C), F32) + cb_ref[...]
    for j in range(CONV_WIDTH):
        y = y + xpad[:, pl.ds(pad - (CONV_WIDTH - 1) + j, tl), :] * cw_ref[j:j + 1, :]
    y2 = y.reshape(B * tl, C)
    yb = y2.astype(BF16)
    gate_a = jnp.dot(yb, wa_ref[...], preferred_element_type=F32) + ba_ref[...]
    gate_x = jnp.dot(yb, wx_ref[...], preferred_element_type=F32) + bx_ref[...]
    r = jax.nn.sigmoid(gate_a)
    ig = jax.nn.sigmoid(gate_x)
    nl = -lam_ref[...]
    softplus = jnp.maximum(nl, 0.0) + jnp.log1p(jnp.exp(-jnp.abs(nl)))
    log_a = (-LRU_C) * r * softplus
    a_all = jnp.exp(log_a)
    b_all = jnp.sqrt(_neg_expm1(2.0 * log_a)) * ig * y2
    ncol = C // LANES
    for c in range(ncol):
        a_scr[c] = a_all[:, c * LANES:(c + 1) * LANES]
        b_scr[c] = b_all[:, c * LANES:(c + 1) * LANES]

    def step(t, hs):
        out = []
        for c in range(ncol):
            a = a_scr[c, pl.ds(t, B, stride=tl), :]
            b = b_scr[c, pl.ds(t, B, stride=tl), :]
            h = a * hs[c] + b
            h_scr[c, pl.ds(t, B, stride=tl), :] = h
            out.append(h)
        return tuple(out)

    hs = lax.fori_loop(0, tl, step, tuple(h_carry[c] for c in range(ncol)), unroll=8)
    for c in range(ncol):
        h_carry[c] = hs[c]
    h_all = jnp.concatenate([h_scr[c] for c in range(ncol)], axis=1)
    lru = h_all * _gelu_tanh(lg_ref[...].reshape(B * tl, C))
    o_ref[...] = _rms(lru, g_ref[...]).astype(BF16).reshape(B, tl, C)


def _lru(lx3, lg3, conv_w, conv_b, wa_bd, ba, wx_bd, bx, lam, g_lru):
    B, S, C = lx3.shape
    tl = TL_LRU
    blk = lambda i: (0, i, 0)
    const = lambda i: (0, 0)
    vec = pl.BlockSpec((1, C), const)
    return pl.pallas_call(
        _lru_kernel,
        grid=(S // tl,),
        in_specs=[pl.BlockSpec((B, tl, C), blk), pl.BlockSpec((B, tl, C), blk),
                  pl.BlockSpec((CONV_WIDTH, C), const), vec,
                  pl.BlockSpec((C, C), const), vec, pl.BlockSpec((C, C), const), vec, vec, vec],
        out_specs=pl.BlockSpec((B, tl, C), blk),
        out_shape=jax.ShapeDtypeStruct((B, S, C), BF16),
        scratch_shapes=[pltpu.VMEM((B, tl + SUBLANES, C), F32),
                        pltpu.VMEM((C // LANES, B * tl, LANES), F32),
                        pltpu.VMEM((C // LANES, B * tl, LANES), F32),
                        pltpu.VMEM((C // LANES, B * tl, LANES), F32),
                        pltpu.VMEM((C // LANES, B, LANES), F32)],
        compiler_params=_cparams(("arbitrary",)),
        name="lru",
    )(lx3, lg3, conv_w, conv_b, wa_bd, ba, wx_bd, bx, lam, g_lru)


def _outproj_kernel(x_ref, ma_ref, mb_ref, wo_ref, g_ref, wr_ref, br_ref,
                    x1_ref, h2_ref, eid_ref, wt_ref):
    half = ma_ref.shape[1]
    x1 = (x_ref[...]
          + jnp.dot(ma_ref[...], wo_ref[0:half, :], preferred_element_type=F32)
          + jnp.dot(mb_ref[...], wo_ref[half:, :], preferred_element_type=F32))
    x1_ref[...] = x1
    h2 = _rms(x1, g_ref[...])
    h2_ref[...] = h2
    logits = jnp.dot(h2, wr_ref[...], preferred_element_type=F32,
                     precision=lax.Precision.HIGHEST) + br_ref[...]
    tm = logits.shape[0]
    lane = lax.broadcasted_iota(I32, (tm, LANES), 1)
    lane_f = lane.astype(F32)
    big = float(LANES)

    def first_argmax(vals):
        m = jnp.max(vals, axis=-1, keepdims=True)
        idx = jnp.min(jnp.where(vals == m, lane_f, big), axis=-1, keepdims=True)
        return m, idx.astype(I32)

    gl = jnp.where(lane < N_GROUPS, logits, NEG_INF)
    gmax, gidx = first_argmax(gl)
    gsum = jnp.sum(jnp.where(lane < N_GROUPS, jnp.exp(gl - gmax), 0.0), axis=-1, keepdims=True)
    g_top_p = 1.0 / gsum
    lo = N_GROUPS + EXPERTS_PER_GROUP * gidx
    el = jnp.where((lane >= lo) & (lane < lo + EXPERTS_PER_GROUP), logits, NEG_INF)
    m1, i1 = first_argmax(el)
    el2 = jnp.where(lane == i1, NEG_INF, el)
    m2, i2 = first_argmax(el2)
    ratio = jnp.exp(m2 - m1)
    w1 = g_top_p / (1.0 + ratio)
    w2 = g_top_p * ratio / (1.0 + ratio)
    eid_ref[...] = jnp.where(lane == 0, i1 - N_GROUPS, jnp.where(lane == 1, i2 - N_GROUPS, 0))
    wt_ref[...] = jnp.where(lane == 0, w1, jnp.where(lane == 1, w2, 0.0))


def _outproj(x2, mix_a, mix_b, w_out_b, g_ffn, w_router, b_router):
    T, D = x2.shape
    tm = TM_PROJ
    half = mix_a.shape[1]
    row = lambda i: (i, 0)
    const = lambda i: (0, 0)
    return pl.pallas_call(
        _outproj_kernel,
        grid=(T // tm,),
        in_specs=[pl.BlockSpec((tm, D), row), pl.BlockSpec((tm, half), row),
                  pl.BlockSpec((tm, half), row), pl.BlockSpec(w_out_b.shape, const),
                  pl.BlockSpec((1, D), const), pl.BlockSpec((D, LANES), const),
                  pl.BlockSpec((1, LANES), const)],
        out_specs=[pl.BlockSpec((tm, D), row), pl.BlockSpec((tm, D), row),
                   pl.BlockSpec((tm, LANES), row), pl.BlockSpec((tm, LANES), row)],
        out_shape=[jax.ShapeDtypeStruct((T, D), F32), jax.ShapeDtypeStruct((T, D), F32),
                   jax.ShapeDtypeStruct((T, LANES), I32), jax.ShapeDtypeStruct((T, LANES), F32)],
        compiler_params=_cparams(("arbitrary",)),
        name="outproj",
    )(x2, mix_a, mix_b, w_out_b, g_ffn, w_router, b_router)


def _lane_cumsum(x):
    lane = lax.broadcasted_iota(I32, x.shape, 1)
    shift = 1
    while shift < LANES:
        x = x + jnp.where(lane >= shift, pltpu.roll(x, shift, 1), 0)
        shift *= 2
    return x


def _plan_kernel(eid_ref, pos_ref, te_ref, nv_ref, rank_scr, cnt_scr):
    phase = pl.program_id(0)
    i = pl.program_id(1)
    tm = eid_ref.shape[0]
    lane = lax.broadcasted_iota(I32, (tm, LANES), 1)
    eid = eid_ref[...]
    f1 = eid[:, 0:1]
    f2 = eid[:, 1:2]
    sel1 = lane == f1
    sel2 = lane == f2

    @pl.when((phase == 0) & (i == 0))
    def _():
        cnt_scr[...] = jnp.zeros_like(cnt_scr)

    @pl.when(phase == 0)
    def _():
        onehot = (sel1 | sel2).astype(BF16)
        r = lax.broadcasted_iota(I32, (tm, tm), 0)
        c = lax.broadcasted_iota(I32, (tm, tm), 1)
        strict_lower = (r > c).astype(BF16)
        before = jnp.dot(strict_lower, onehot, preferred_element_type=F32) + cnt_scr[0:1, :]
        r1 = jnp.sum(jnp.where(sel1, before, 0.0), axis=-1, keepdims=True)
        r2 = jnp.sum(jnp.where(sel2, before, 0.0), axis=-1, keepdims=True)
        rank = jnp.where(lane == 0, r1, jnp.where(lane == 1, r2, 0.0)).astype(I32)
        rank_scr[pl.ds(pl.multiple_of(i * tm, tm), tm), :] = rank
        cnt_scr[...] = cnt_scr[...] + jnp.sum(onehot.astype(F32), axis=0, keepdims=True)

    @pl.when(phase == 1)
    def _():
        shift = int(math.log2(TM_EXPERT))
        cnt = cnt_scr[...].astype(I32)
        padded = ((cnt + (TM_EXPERT - 1)) >> shift) << shift
        ends = _lane_cumsum(padded)
        starts = (ends - padded).astype(F32)[0:1, :]
        o1 = jnp.sum(jnp.where(sel1, starts, 0.0), axis=-1, keepdims=True).astype(I32)
        o2 = jnp.sum(jnp.where(sel2, starts, 0.0), axis=-1, keepdims=True).astype(I32)
        rank = rank_scr[pl.ds(pl.multiple_of(i * tm, tm), tm), :]
        pos_ref[...] = rank + jnp.where(lane == 0, o1, jnp.where(lane == 1, o2, 0))
        nt = te_ref.shape[0]
        end_tile = (ends >> shift)[0:1, :]
        tl_lane = lax.broadcasted_iota(I32, (nt, LANES), 1)
        tile = lax.broadcasted_iota(I32, (nt, LANES), 0)
        done = jnp.where((tl_lane < N_EXPERTS) & (end_tile <= tile), 1.0, 0.0)
        te = jnp.sum(done, axis=-1, keepdims=True).astype(I32)
        te_ref[...] = jnp.broadcast_to(jnp.minimum(te, N_EXPERTS - 1), (nt, LANES))
        n_valid = jnp.sum(jnp.where(tl_lane[0:1, :] == N_EXPERTS - 1, end_tile, 0).astype(F32),
                          axis=-1, keepdims=True).astype(I32)
        nv_ref[...] = jnp.broadcast_to(n_valid, nv_ref.shape)


def _plan(eid, n_tiles):
    T = eid.shape[0]
    tm = TM_PLAN
    const = lambda p, i: (0, 0)
    return pl.pallas_call(
        _plan_kernel,
        grid=(2, T // tm),
        in_specs=[pl.BlockSpec((tm, LANES), lambda p, i: (i, 0))],
        out_specs=[pl.BlockSpec((tm, LANES), lambda p, i: (i * p, 0)),
                   pl.BlockSpec((n_tiles, LANES), const), pl.BlockSpec((SUBLANES, LANES), const)],
        out_shape=[jax.ShapeDtypeStruct((T, LANES), I32), jax.ShapeDtypeStruct((n_tiles, LANES), I32),
                   jax.ShapeDtypeStruct((SUBLANES, LANES), I32)],
        scratch_shapes=[pltpu.VMEM((T, LANES), I32), pltpu.VMEM((SUBLANES, LANES), F32)],
        compiler_params=_cparams(("arbitrary", "arbitrary")),
        name="plan",
    )(eid)


def _dispatch_kernel(pos_ref, h_ref, xs_ref, sem):
    tm = h_ref.shape[0]

    def row_copy(t, p):
        return pltpu.make_async_copy(h_ref.at[pl.ds(t, 1)], xs_ref.at[pl.ds(p, 1)], sem)

    def issue(t, c):
        row_copy(t, pos_ref[2 * t]).start()
        row_copy(t, pos_ref[2 * t + 1]).start()
        return c

    lax.fori_loop(0, tm, issue, 0)

    def drain(t, c):
        row_copy(0, 0).wait()
        row_copy(0, 0).wait()
        return c

    lax.fori_loop(0, tm, drain, 0)


def _dispatch(pos_flat, h2, n_rows):
    T, D = h2.shape
    tm = TM_DISPATCH
    return pl.pallas_call(
        _dispatch_kernel,
        grid=(T // tm,),
        in_specs=[pl.BlockSpec((2 * tm,), lambda i: (i,), memory_space=pltpu.SMEM),
                  pl.BlockSpec((tm, D), lambda i: (i, 0))],
        out_specs=pl.BlockSpec(memory_space=pl.ANY),
        out_shape=jax.ShapeDtypeStruct((n_rows, D), F32),
        scratch_shapes=[pltpu.SemaphoreType.DMA(())],
        compiler_params=_cparams(("arbitrary",)),
        name="dispatch",
    )(pos_flat, h2)


def _experts_kernel(te_ref, nv_ref, xs_ref, wg_ref, wu_ref, wd_ref, ys_ref, wg_b, wu_b, wd_b):
    j = pl.program_id(0)
    new_expert = (j == 0) | (te_ref[j] != te_ref[jnp.maximum(j - 1, 0)])

    @pl.when(new_expert)
    def _():
        wg_b[...] = wg_ref[0].astype(BF16)
        wu_b[...] = wu_ref[0].astype(BF16)
        wd_b[...] = wd_ref[0].astype(BF16)

    @pl.when(j < nv_ref[0])
    def _():
        x = xs_ref[...].astype(BF16)
        a = jnp.dot(x, wg_b[...], preferred_element_type=F32)
        u = jnp.dot(x, wu_b[...], preferred_element_type=F32)
        hid = (a * jax.nn.sigmoid(a) * u).astype(BF16)
        ys_ref[...] = jnp.dot(hid, wd_b[...], preferred_element_type=F32)


def _experts(tile_expert, n_valid, xs, wg, wu, wd):
    R, D = xs.shape
    E, _, Fh = wg.shape
    tm = TM_EXPERT
    n_tiles = R // tm
    rows = lambda j, te, nv: (jnp.minimum(j, nv[0] - 1), 0)
    wsel = lambda j, te, nv: (te[j], 0, 0)
    return pl.pallas_call(
        _experts_kernel,
        grid_spec=pltpu.PrefetchScalarGridSpec(
            num_scalar_prefetch=2,
            grid=(n_tiles,),
            in_specs=[pl.BlockSpec((tm, D), rows),
                      pl.BlockSpec((1, D, Fh), wsel), pl.BlockSpec((1, D, Fh), wsel),
                      pl.BlockSpec((1, Fh, D), wsel)],
            out_specs=pl.BlockSpec((tm, D), rows),
            scratch_shapes=[pltpu.VMEM((D, Fh), BF16), pltpu.VMEM((D, Fh), BF16),
                            pltpu.VMEM((Fh, D), BF16)]),
        out_shape=jax.ShapeDtypeStruct((R, D), F32),
        compiler_params=_cparams(("arbitrary",)),
        name="experts",
    )(tile_expert, n_valid, xs, wg, wu, wd)


def _combine_kernel(pos_ref, x1_ref, wt_ref, p_ref, gp_ref, wpg_ref, wpp_ref, gf_ref, ys_ref,
                    o_ref, buf, sem):
    tm = x1_ref.shape[0]

    def row_copy(k, t, p):
        return pltpu.make_async_copy(ys_ref.at[pl.ds(p, 1)], buf.at[k, pl.ds(t, 1)], sem)

    def issue(t, c):
        row_copy(0, t, pos_ref[2 * t]).start()
        row_copy(1, t, pos_ref[2 * t + 1]).start()
        return c

    lax.fori_loop(0, tm, issue, 0)

    def drain(t, c):
        row_copy(0, 0, 0).wait()
        row_copy(1, 0, 0).wait()
        return c

    lax.fori_loop(0, tm, drain, 0)
    wt = wt_ref[...]
    x2 = x1_ref[...] + wt[:, 0:1] * buf[0] + wt[:, 1:2] * buf[1]
    hp = _rms(x2, gp_ref[...]).astype(BF16)
    gate = jax.nn.sigmoid(jnp.dot(hp, wpg_ref[...], preferred_element_type=F32))
    proj = jnp.dot(p_ref[...].astype(BF16), wpp_ref[...], preferred_element_type=F32)
    x3 = x2 + gate * proj
    o_ref[...] = _rms(x3, gf_ref[...])


def _combine(pos_flat, x1, wts, p2, g_ple, w_ple_gate_b, w_ple_proj_b, g_final, ys):
    T, D = x1.shape
    P = p2.shape[1]
    tm = TM_COMBINE
    row = lambda i: (i, 0)
    const = lambda i: (0, 0)
    return pl.pallas_call(
        _combine_kernel,
        grid=(T // tm,),
        in_specs=[pl.BlockSpec((2 * tm,), lambda i: (i,), memory_space=pltpu.SMEM),
                  pl.BlockSpec((tm, D), row), pl.BlockSpec((tm, LANES), row),
                  pl.BlockSpec((tm, P), row), pl.BlockSpec((1, D), const),
                  pl.BlockSpec((D, D), const), pl.BlockSpec((P, D), const),
                  pl.BlockSpec((1, D), const), pl.BlockSpec(memory_space=pl.ANY)],
        out_specs=pl.BlockSpec((tm, D), row),
        out_shape=jax.ShapeDtypeStruct((T, D), F32),
        scratch_shapes=[pltpu.VMEM((2, tm, D), F32), pltpu.SemaphoreType.DMA(())],
        compiler_params=_cparams(("arbitrary",)),
        name="combine",
    )(pos_flat, x1, wts, p2, g_ple, w_ple_gate_b, w_ple_proj_b, g_final, ys)


def _block_diag(w):
    nb, d, _ = w.shape
    eye = jnp.eye(nb, dtype=w.dtype)
    return (eye[:, None, :, None] * w[:, :, None, :]).reshape(nb * d, nb * d)


def kernel(x, p, positions, g_mix, w_in, sinks, conv_w, conv_b, lru_wa, lru_ba, lru_wx, lru_bx,
           lru_lambda, g_attn_out, g_lru_out, w_out, g_ffn, w_router_group, b_router_group,
           w_router_expert, b_router_expert, w_expert_gate, w_expert_up, w_expert_down, g_ple,
           w_ple_gate, w_ple_proj, g_final):
    B, S, D = x.shape
    assert w_in.shape[0] == 1, "single-layer block only"
    T = B * S
    lru_w = conv_w.shape[-1]
    n_rows = 2 * T + N_EXPERTS * TM_EXPERT
    n_tiles = n_rows // TM_EXPERT
    pos2 = positions.reshape(T, 1).astype(I32)
    x2 = x.reshape(T, D)
    q, k, v, lx, lg = _inproj(x2, pos2, g_mix[0][None], w_in[0].astype(BF16), lru_w)
    mix_a = _attention(q, k, v, sinks[0], g_attn_out[0][None], B, S)
    mix_b = _lru(lx.reshape(B, S, lru_w), lg.reshape(B, S, lru_w), conv_w[0], conv_b[0][None],
                 _block_diag(lru_wa[0]).astype(BF16), lru_ba[0][None],
                 _block_diag(lru_wx[0]).astype(BF16), lru_bx[0][None],
                 lru_lambda[0][None], g_lru_out[0][None]).reshape(T, lru_w)
    n_router = N_GROUPS + N_EXPERTS
    w_router = jnp.pad(jnp.concatenate([w_router_group[0], w_router_expert[0]], axis=1),
                       ((0, 0), (0, LANES - n_router)))
    b_router = jnp.pad(jnp.concatenate([b_router_group[0], b_router_expert[0]]),
                       (0, LANES - n_router))[None]
    x1, h2, eid, wts = _outproj(x2, mix_a, mix_b, w_out[0].astype(BF16), g_ffn[0][None],
                                w_router, b_router)
    pos, tile_expert, n_valid = _plan(eid, n_tiles)
    pos_flat = pos[:, :2].reshape(2 * T)
    xs = _dispatch(pos_flat, h2, n_rows)
    Fh = w_expert_gate.shape[-1]
    ys = _experts(tile_expert[:, 0], n_valid[0, :1], xs,
                  w_expert_gate[0].reshape(N_EXPERTS, D, Fh),
                  w_expert_up[0].reshape(N_EXPERTS, D, Fh),
                  w_expert_down[0].reshape(N_EXPERTS, Fh, D))
    out = _combine(pos_flat, x1, wts, p[0].reshape(T, -1), g_ple[0][None],
                   w_ple_gate[0].astype(BF16), w_ple_proj[0].astype(BF16), g_final[None], ys)
    return out.reshape(B, S, D)
```

```python
import functools
import math

import jax
import jax.numpy as jnp
from jax import lax
from jax.experimental import pallas as pl
from jax.experimental.pallas import tpu as pltpu

F32 = jnp.float32
BF16 = jnp.bfloat16
I32 = jnp.int32

EPS = 1e-6
N_HEADS = 8
N_KV_HEADS = 2
HEAD_DIM = 64
WINDOW = 128
ROPE_THETA = 10000.0
CONV_WIDTH = 4
LRU_C = 8.0
N_GROUPS = 4
EXPERTS_PER_GROUP = 8
N_EXPERTS = N_GROUPS * EXPERTS_PER_GROUP
NEG_INF = -1e30
LANES = 128
SUBLANES = 8

TM_PROJ = 512
TL_LRU = 256
TM_PLAN = 512
TM_EXPERT = 256
TM_DISPATCH = 512
TM_COMBINE = 256
VMEM_LIMIT = 56 * 1024 * 1024


def _rms(x, g):
    ms = jnp.mean(x * x, axis=-1, keepdims=True)
    return x * lax.rsqrt(ms + EPS) * g


def _cparams(sem):
    return pltpu.CompilerParams(dimension_semantics=sem, vmem_limit_bytes=VMEM_LIMIT)


def _inproj_kernel(x_ref, pos_ref, g_ref, w_ref, q_ref, k_ref, v_ref, lx_ref, lg_ref):
    h = _rms(x_ref[...], g_ref[...]).astype(BF16)
    proj = jnp.dot(h, w_ref[...], preferred_element_type=F32)
    lane = lax.broadcasted_iota(I32, (1, LANES), 1)
    fidx = (lane % (HEAD_DIM // 2)).astype(F32)
    inv_freq = jnp.exp(fidx * (-2.0 / HEAD_DIM * math.log(ROPE_THETA)))
    ang = pos_ref[...].astype(F32) * inv_freq
    cos = jnp.cos(ang)
    sin = jnp.sin(ang)
    first_half = (lane % HEAD_DIM) < (HEAD_DIM // 2)
    sin_signed = jnp.where(first_half, -sin, sin)

    def rope(t):
        partner = jnp.where(first_half, pltpu.roll(t, LANES - HEAD_DIM // 2, 1),
                            pltpu.roll(t, HEAD_DIM // 2, 1))
        return t * cos + partner * sin_signed

    attn_w = N_HEADS * HEAD_DIM
    for c in range(attn_w // LANES):
        q_ref[:, c * LANES:(c + 1) * LANES] = rope(proj[:, c * LANES:(c + 1) * LANES]).astype(BF16)
    kv_w = N_KV_HEADS * HEAD_DIM
    k_ref[...] = rope(proj[:, attn_w:attn_w + kv_w]).astype(BF16)
    v_ref[...] = proj[:, attn_w + kv_w:attn_w + 2 * kv_w].astype(BF16)
    off_lx = attn_w + 2 * kv_w
    lru_w = lx_ref.shape[1]
    lx_ref[...] = proj[:, off_lx:off_lx + lru_w]
    lg_ref[...] = proj[:, off_lx + lru_w:off_lx + 2 * lru_w]


def _inproj(x2, pos2, g_mix, w_in_b, lru_w):
    T, D = x2.shape
    tm = TM_PROJ
    attn_w = N_HEADS * HEAD_DIM
    kv_w = N_KV_HEADS * HEAD_DIM
    row = lambda i: (i, 0)
    const = lambda i: (0, 0)
    return pl.pallas_call(
        _inproj_kernel,
        grid=(T // tm,),
        in_specs=[pl.BlockSpec((tm, D), row), pl.BlockSpec((tm, 1), row),
                  pl.BlockSpec((1, D), const), pl.BlockSpec(w_in_b.shape, const)],
        out_specs=[pl.BlockSpec((tm, attn_w), row), pl.BlockSpec((tm, kv_w), row),
                   pl.BlockSpec((tm, kv_w), row), pl.BlockSpec((tm, lru_w), row),
                   pl.BlockSpec((tm, lru_w), row)],
        out_shape=[jax.ShapeDtypeStruct((T, attn_w), BF16), jax.ShapeDtypeStruct((T, kv_w), BF16),
                   jax.ShapeDtypeStruct((T, kv_w), BF16), jax.ShapeDtypeStruct((T, lru_w), F32),
                   jax.ShapeDtypeStruct((T, lru_w), F32)],
        compiler_params=_cparams(("arbitrary",)),
        name="inproj",
    )(x2, pos2, g_mix, w_in_b)


def _attn_kernel(sinks_ref, q_ref, kc_ref, kp_ref, vc_ref, vp_ref, g_ref, o_ref):
    n = pl.program_id(1)
    blk = WINDOW
    q = q_ref[...]
    kk = jnp.concatenate([kp_ref[...], kc_ref[...]], axis=0)
    vv = jnp.concatenate([vp_ref[...], vc_ref[...]], axis=0)
    qi = lax.broadcasted_iota(I32, (blk, 2 * blk), 0)
    kj = lax.broadcasted_iota(I32, (blk, 2 * blk), 1)
    rel = qi + blk - kj
    mask = (rel >= 0) & (rel < WINDOW) & ((kj >= blk) | (n > 0))
    scale = HEAD_DIM ** -0.5
    grp = N_HEADS // N_KV_HEADS
    outs = []
    for h in range(N_HEADS):
        g = h // grp
        qh = q[:, h * HEAD_DIM:(h + 1) * HEAD_DIM]
        kh = kk[:, g * HEAD_DIM:(g + 1) * HEAD_DIM]
        vh = vv[:, g * HEAD_DIM:(g + 1) * HEAD_DIM]
        s = lax.dot_general(qh, kh, (((1,), (1,)), ((), ())), preferred_element_type=F32) * scale
        s = jnp.where(mask, s, NEG_INF)
        sink = sinks_ref[h]
        m = jnp.maximum(jnp.max(s, axis=-1, keepdims=True), sink)
        e = jnp.exp(s - m)
        den = jnp.sum(e, axis=-1, keepdims=True) + jnp.exp(sink - m)
        p = (e / den).astype(BF16)
        outs.append(jnp.dot(p, vh, preferred_element_type=F32))
    attn = jnp.concatenate(outs, axis=1)
    o_ref[...] = _rms(attn, g_ref[...]).astype(BF16)


def _attention(q, k, v, sinks, g_attn, B, S):
    T, attn_w = q.shape
    kv_w = k.shape[1]
    blk = WINDOW
    nb = S // blk
    cur = lambda b, n: (b * nb + n, 0)
    prev = lambda b, n: (b * nb + jnp.maximum(n - 1, 0), 0)
    return pl.pallas_call(
        _attn_kernel,
        grid=(B, nb),
        in_specs=[pl.BlockSpec(memory_space=pltpu.SMEM),
                  pl.BlockSpec((blk, attn_w), cur),
                  pl.BlockSpec((blk, kv_w), cur), pl.BlockSpec((blk, kv_w), prev),
                  pl.BlockSpec((blk, kv_w), cur), pl.BlockSpec((blk, kv_w), prev),
                  pl.BlockSpec((1, attn_w), lambda b, n: (0, 0))],
        out_specs=pl.BlockSpec((blk, attn_w), cur),
        out_shape=jax.ShapeDtypeStruct((T, attn_w), BF16),
        compiler_params=_cparams(("arbitrary", "arbitrary")),
        name="attn",
    )(sinks, q, k, k, v, v, g_attn)


def _neg_expm1(x):
    series = -x * (1.0 + x * (0.5 + x * (1.0 / 6.0 + x * (1.0 / 24.0 + x * (1.0 / 120.0)))))
    return jnp.where(x > -0.05, series, 1.0 - jnp.exp(x))


def _gelu_tanh(x):
    return 0.5 * x * (1.0 + jnp.tanh(math.sqrt(2.0 / math.pi) * (x + 0.044715 * (x * x * x))))


def _lru_kernel(lx_ref, lg_ref, cw_ref, cb_ref, wa_ref, ba_ref, wx_ref, bx_ref, lam_ref, g_ref,
                o_ref, xpad, a_scr, b_scr, h_scr, h_carry):
    i = pl.program_id(0)
    B, tl, C = lx_ref.shape
    pad = SUBLANES

    @pl.when(i == 0)
    def _():
        xpad[:, 0:pad, :] = jnp.zeros((B, pad, C), F32)
        h_carry[...] = jnp.zeros_like(h_carry)

    @pl.when(i > 0)
    def _():
        xpad[:, 0:pad, :] = xpad[:, tl:tl + pad, :]

    xpad[:, pad:, :] = lx_ref[...]
    y = jnp.zeros((B, tl, C), F32) + cb_ref[...]
    for j in range(CONV_WIDTH):
        y = y + xpad[:, pl.ds(pad - (CONV_WIDTH - 1) + j, tl), :] * cw_ref[j:j + 1, :]
    y2 = y.reshape(B * tl, C)
    yb = y2.astype(BF16)
    gate_a = jnp.dot(yb, wa_ref[...], preferred_element_type=F32) + ba_ref[...]
    gate_x = jnp.dot(yb, wx_ref[...], preferred_element_type=F32) + bx_ref[...]
    r = jax.nn.sigmoid(gate_a)
    ig = jax.nn.sigmoid(gate_x)
    nl = -lam_ref[...]
    softplus = jnp.maximum(nl, 0.0) + jnp.log1p(jnp.exp(-jnp.abs(nl)))
    log_a = (-LRU_C) * r * softplus
    a_all = jnp.exp(log_a)
    b_all = jnp.sqrt(_neg_expm1(2.0 * log_a)) * ig * y2
    ncol = C // LANES
    for c in range(ncol):
        a_scr[c] = a_all[:, c * LANES:(c + 1) * LANES]
        b_scr[c] = b_all[:, c * LANES:(c + 1) * LANES]

    def step(t, hs):
        out = []
        for c in range(ncol):
            a = a_scr[c, pl.ds(t, B, stride=tl), :]
            b = b_scr[c, pl.ds(t, B, stride=tl), :]
            h = a * hs[c] + b
            h_scr[c, pl.ds(t, B, stride=tl), :] = h
            out.append(h)
        return tuple(out)

    hs = lax.fori_loop(0, tl, step, tuple(h_carry[c] for c in range(ncol)), unroll=8)
    for c in range(ncol):
        h_carry[c] = hs[c]
    h_all = jnp.concatenate([h_scr[c] for c in range(ncol)], axis=1)
    lru = h_all * _gelu_tanh(lg_ref[...].reshape(B * tl, C))
    o_ref[...] = _rms(lru, g_ref[...]).astype(BF16).reshape(B, tl, C)


def _lru(lx3, lg3, conv_w, conv_b, wa_bd, ba, wx_bd, bx, lam, g_lru):
    B, S, C = lx3.shape
    tl = TL_LRU
    blk = lambda i: (0, i, 0)
    const = lambda i: (0, 0)
    vec = pl.BlockSpec((1, C), const)
    return pl.pallas_call(
        _lru_kernel,
        grid=(S // tl,),
        in_specs=[pl.BlockSpec((B, tl, C), blk), pl.BlockSpec((B, tl, C), blk),
                  pl.BlockSpec((CONV_WIDTH, C), const), vec,
                  pl.BlockSpec((C, C), const), vec, pl.BlockSpec((C, C), const), vec, vec, vec],
        out_specs=pl.BlockSpec((B, tl, C), blk),
        out_shape=jax.ShapeDtypeStruct((B, S, C), BF16),
        scratch_shapes=[pltpu.VMEM((B, tl + SUBLANES, C), F32),
                        pltpu.VMEM((C // LANES, B * tl, LANES), F32),
                        pltpu.VMEM((C // LANES, B * tl, LANES), F32),
                        pltpu.VMEM((C // LANES, B * tl, LANES), F32),
                        pltpu.VMEM((C // LANES, B, LANES), F32)],
        compiler_params=_cparams(("arbitrary",)),
        name="lru",
    )(lx3, lg3, conv_w, conv_b, wa_bd, ba, wx_bd, bx, lam, g_lru)


def _outproj_kernel(x_ref, ma_ref, mb_ref, wo_ref, g_ref, wr_ref, br_ref,
                    x1_ref, h2_ref, eid_ref, wt_ref):
    half = ma_ref.shape[1]
    x1 = (x_ref[...]
          + jnp.dot(ma_ref[...], wo_ref[0:half, :], preferred_element_type=F32)
          + jnp.dot(mb_ref[...], wo_ref[half:, :], preferred_element_type=F32))
    x1_ref[...] = x1
    h2 = _rms(x1, g_ref[...])
    h2_ref[...] = h2
    logits = jnp.dot(h2, wr_ref[...], preferred_element_type=F32,
                     precision=lax.Precision.HIGHEST) + br_ref[...]
    tm = logits.shape[0]
    lane = lax.broadcasted_iota(I32, (tm, LANES), 1)
    lane_f = lane.astype(F32)
    big = float(LANES)

    def first_argmax(vals):
        m = jnp.max(vals, axis=-1, keepdims=True)
        idx = jnp.min(jnp.where(vals == m, lane_f, big), axis=-1, keepdims=True)
        return m, idx.astype(I32)

    gl = jnp.where(lane < N_GROUPS, logits, NEG_INF)
    gmax, gidx = first_argmax(gl)
    gsum = jnp.sum(jnp.where(lane < N_GROUPS, jnp.exp(gl - gmax), 0.0), axis=-1, keepdims=True)
    g_top_p = 1.0 / gsum
    lo = N_GROUPS + EXPERTS_PER_GROUP * gidx
    el = jnp.where((lane >= lo) & (lane < lo + EXPERTS_PER_GROUP), logits, NEG_INF)
    m1, i1 = first_argmax(el)
    el2 = jnp.where(lane == i1, NEG_INF, el)
    m2, i2 = first_argmax(el2)
    ratio = jnp.exp(m2 - m1)
    w1 = g_top_p / (1.0 + ratio)
    w2 = g_top_p * ratio / (1.0 + ratio)
    eid_ref[...] = jnp.where(lane == 0, i1 - N_GROUPS, jnp.where(lane == 1, i2 - N_GROUPS, 0))
    wt_ref[...] = jnp.where(lane == 0, w1, jnp.where(lane == 1, w2, 0.0))


def _outproj(x2, mix_a, mix_b, w_out_b, g_ffn, w_router, b_router):
    T, D = x2.shape
    tm = TM_PROJ
    half = mix_a.shape[1]
    row = lambda i: (i, 0)
    const = lambda i: (0, 0)
    return pl.pallas_call(
        _outproj_kernel,
        grid=(T // tm,),
        in_specs=[pl.BlockSpec((tm, D), row), pl.BlockSpec((tm, half), row),
                  pl.BlockSpec((tm, half), row), pl.BlockSpec(w_out_b.shape, const),
                  pl.BlockSpec((1, D), const), pl.BlockSpec((D, LANES), const),
                  pl.BlockSpec((1, LANES), const)],
        out_specs=[pl.BlockSpec((tm, D), row), pl.BlockSpec((tm, D), row),
                   pl.BlockSpec((tm, LANES), row), pl.BlockSpec((tm, LANES), row)],
        out_shape=[jax.ShapeDtypeStruct((T, D), F32), jax.ShapeDtypeStruct((T, D), F32),
                   jax.ShapeDtypeStruct((T, LANES), I32), jax.ShapeDtypeStruct((T, LANES), F32)],
        compiler_params=_cparams(("arbitrary",)),
        name="outproj",
    )(x2, mix_a, mix_b, w_out_b, g_ffn, w_router, b_router)


def _lane_cumsum(x):
    lane = lax.broadcasted_iota(I32, x.shape, 1)
    shift = 1
    while shift < LANES:
        x = x + jnp.where(lane >= shift, pltpu.roll(x, shift, 1), 0)
        shift *= 2
    return x


def _plan_kernel(eid_ref, pos_ref, te_ref, nv_ref, meta_ref, rank_scr, cnt_scr):
    phase = pl.program_id(0)
    i = pl.program_id(1)
    tm = eid_ref.shape[0]
    lane = lax.broadcasted_iota(I32, (tm, LANES), 1)
    eid = eid_ref[...]
    f1 = eid[:, 0:1]
    f2 = eid[:, 1:2]
    sel1 = lane == f1
    sel2 = lane == f2

    @pl.when((phase == 0) & (i == 0))
    def _():
        cnt_scr[...] = jnp.zeros_like(cnt_scr)

    @pl.when(phase == 0)
    def _():
        onehot = (sel1 | sel2).astype(BF16)
        r = lax.broadcasted_iota(I32, (tm, tm), 0)
        c = lax.broadcasted_iota(I32, (tm, tm), 1)
        strict_lower = (r > c).astype(BF16)
        before = jnp.dot(strict_lower, onehot, preferred_element_type=F32) + cnt_scr[0:1, :]
        r1 = jnp.sum(jnp.where(sel1, before, 0.0), axis=-1, keepdims=True)
        r2 = jnp.sum(jnp.where(sel2, before, 0.0), axis=-1, keepdims=True)
        rank = jnp.where(lane == 0, r1, jnp.where(lane == 1, r2, 0.0)).astype(I32)
        rank_scr[pl.ds(pl.multiple_of(i * tm, tm), tm), :] = rank
        cnt_scr[...] = cnt_scr[...] + jnp.sum(onehot.astype(F32), axis=0, keepdims=True)

    @pl.when(phase == 1)
    def _():
        shift = int(math.log2(TM_EXPERT))
        cnt = cnt_scr[...].astype(I32)
        padded = ((cnt + (TM_EXPERT - 1)) >> shift) << shift
        ends = _lane_cumsum(padded)
        starts = (ends - padded).astype(F32)[0:1, :]
        o1 = jnp.sum(jnp.where(sel1, starts, 0.0), axis=-1, keepdims=True).astype(I32)
        o2 = jnp.sum(jnp.where(sel2, starts, 0.0), axis=-1, keepdims=True).astype(I32)
        rank = rank_scr[pl.ds(pl.multiple_of(i * tm, tm), tm), :]
        pos_ref[...] = rank + jnp.where(lane == 0, o1, jnp.where(lane == 1, o2, 0))
        nt = te_ref.shape[0]
        end_tile = (ends >> shift)[0:1, :]
        tl_lane = lax.broadcasted_iota(I32, (nt, LANES), 1)
        tile = lax.broadcasted_iota(I32, (nt, LANES), 0)
        done = jnp.where((tl_lane < N_EXPERTS) & (end_tile <= tile), 1.0, 0.0)
        te = jnp.sum(done, axis=-1, keepdims=True).astype(I32)
        te_ref[...] = jnp.broadcast_to(jnp.minimum(te, N_EXPERTS - 1), (nt, LANES))
        n_valid = jnp.sum(jnp.where(tl_lane[0:1, :] == N_EXPERTS - 1, end_tile, 0).astype(F32),
                          axis=-1, keepdims=True).astype(I32)
        nv_ref[...] = jnp.broadcast_to(n_valid, nv_ref.shape)
        row = lax.broadcasted_iota(I32, cnt.shape, 0)
        meta_ref[...] = jnp.where(row == 0, ends - padded + cnt, jnp.where(row == 1, padded - cnt, 0))


def _plan(eid, n_tiles):
    T = eid.shape[0]
    tm = TM_PLAN
    const = lambda p, i: (0, 0)
    return pl.pallas_call(
        _plan_kernel,
        grid=(2, T // tm),
        in_specs=[pl.BlockSpec((tm, LANES), lambda p, i: (i, 0))],
        out_specs=[pl.BlockSpec((tm, LANES), lambda p, i: (i * p, 0)),
                   pl.BlockSpec((n_tiles, LANES), const), pl.BlockSpec((SUBLANES, LANES), const),
                   pl.BlockSpec((SUBLANES, LANES), const)],
        out_shape=[jax.ShapeDtypeStruct((T, LANES), I32), jax.ShapeDtypeStruct((n_tiles, LANES), I32),
                   jax.ShapeDtypeStruct((SUBLANES, LANES), I32),
                   jax.ShapeDtypeStruct((SUBLANES, LANES), I32)],
        scratch_shapes=[pltpu.VMEM((T, LANES), I32), pltpu.VMEM((SUBLANES, LANES), F32)],
        compiler_params=_cparams(("arbitrary", "arbitrary")),
        name="plan",
    )(eid)


_PAD_BITS = tuple(1 << b for b in reversed(range(int(math.log2(TM_EXPERT // SUBLANES)))))


def _dispatch_kernel(pos_ref, seg_end_ref, seg_pad_ref, nv_ref, h_ref, xs_ref, zeros, sem, zsem):
    i = pl.program_id(0)
    tm = h_ref.shape[0]

    def issue(t, c):
        for k in range(2):
            pltpu.make_async_copy(h_ref.at[pl.ds(t, 1)], xs_ref.at[pl.ds(pos_ref[2 * t + k], 1)],
                                  sem).start()
        return c

    lax.fori_loop(0, tm, issue, 0, unroll=8)
    for k in range(2):
        pltpu.make_async_copy(h_ref, xs_ref.at[pl.ds(0, tm)], sem).wait()

    @pl.when(i == pl.num_programs(0) - 1)
    def _():
        zeros[...] = jnp.zeros_like(zeros)
        n_tiles = xs_ref.shape[0] // TM_EXPERT
        first_tail = n_tiles - N_EXPERTS

        def row_copy(e, r):
            end = seg_end_ref[e]
            n_single = jnp.minimum((-end) & (SUBLANES - 1), seg_pad_ref[e])
            return r < n_single, pltpu.make_async_copy(zeros.at[pl.ds(0, 1)],
                                                       xs_ref.at[pl.ds(end + r, 1)], zsem)

        def chunk_copy(e, bit):
            end = seg_end_ref[e]
            n_single = jnp.minimum((-end) & (SUBLANES - 1), seg_pad_ref[e])
            n = (seg_pad_ref[e] - n_single) // SUBLANES
            dst = pl.multiple_of(end + n_single + SUBLANES * (n & ~(2 * bit - 1)), SUBLANES)
            rows = SUBLANES * bit
            return (n & bit) != 0, pltpu.make_async_copy(zeros.at[pl.ds(0, rows)],
                                                         xs_ref.at[pl.ds(dst, rows)], zsem)

        def tail_copy(j):
            return j >= nv_ref[0], pltpu.make_async_copy(
                zeros, xs_ref.at[pl.ds(pl.multiple_of(j * TM_EXPERT, TM_EXPERT), TM_EXPERT)], zsem)

        def for_each_fill(act):
            def per_expert(e, c):
                for r in range(SUBLANES - 1):
                    pred, cp = row_copy(e, r)
                    pl.when(pred)(functools.partial(act, cp))
                for bit in _PAD_BITS:
                    pred, cp = chunk_copy(e, bit)
                    pl.when(pred)(functools.partial(act, cp))
                return c

            lax.fori_loop(0, N_EXPERTS, per_expert, 0)

            def per_tile(j, c):
                pred, cp = tail_copy(j)
                pl.when(pred)(functools.partial(act, cp))
                return c

            lax.fori_loop(first_tail, n_tiles, per_tile, 0)

        for_each_fill(lambda cp: cp.start())
        for_each_fill(lambda cp: cp.wait())


def _dispatch(pos_flat, seg_end, seg_pad, n_valid, h2, n_rows):
    T, D = h2.shape
    tm = TM_DISPATCH
    smem = pl.BlockSpec(memory_space=pltpu.SMEM)
    return pl.pallas_call(
        _dispatch_kernel,
        grid=(T // tm,),
        in_specs=[pl.BlockSpec((2 * tm,), lambda i: (i,), memory_space=pltpu.SMEM),
                  smem, smem, smem, pl.BlockSpec((tm, D), lambda i: (i, 0))],
        out_specs=pl.BlockSpec(memory_space=pl.ANY),
        out_shape=jax.ShapeDtypeStruct((n_rows, D), F32),
        scratch_shapes=[pltpu.VMEM((TM_EXPERT, D), F32), pltpu.SemaphoreType.DMA(()),
                        pltpu.SemaphoreType.DMA(())],
        compiler_params=_cparams(("arbitrary",)),
        name="dispatch",
    )(pos_flat, seg_end, seg_pad, n_valid, h2)


def _experts_kernel(te_ref, nv_ref, xs_ref, wg_ref, wu_ref, wd_ref, ys_ref, wg_b, wu_b, wd_b):
    j = pl.program_id(0)
    new_expert = (j == 0) | (te_ref[j] != te_ref[jnp.maximum(j - 1, 0)])

    @pl.when(new_expert)
    def _():
        wg_b[...] = wg_ref[0].astype(BF16)
        wu_b[...] = wu_ref[0].astype(BF16)
        wd_b[...] = wd_ref[0].astype(BF16)

    @pl.when(j < nv_ref[0])
    def _():
        x = xs_ref[...].astype(BF16)
        a = jnp.dot(x, wg_b[...], preferred_element_type=F32)
        u = jnp.dot(x, wu_b[...], preferred_element_type=F32)
        hid = (a * jax.nn.sigmoid(a) * u).astype(BF16)
        ys_ref[...] = jnp.dot(hid, wd_b[...], preferred_element_type=F32)

    @pl.when(j >= nv_ref[0])
    def _():
        ys_ref[...] = jnp.zeros_like(ys_ref)


def _experts(tile_expert, n_valid, xs, wg, wu, wd):
    R, D = xs.shape
    E, _, Fh = wg.shape
    tm = TM_EXPERT
    n_tiles = R // tm
    rows = lambda j, te, nv: (j, 0)
    wsel = lambda j, te, nv: (te[j], 0, 0)
    return pl.pallas_call(
        _experts_kernel,
        grid_spec=pltpu.PrefetchScalarGridSpec(
            num_scalar_prefetch=2,
            grid=(n_tiles,),
            in_specs=[pl.BlockSpec((tm, D), rows),
                      pl.BlockSpec((1, D, Fh), wsel), pl.BlockSpec((1, D, Fh), wsel),
                      pl.BlockSpec((1, Fh, D), wsel)],
            out_specs=pl.BlockSpec((tm, D), rows),
            scratch_shapes=[pltpu.VMEM((D, Fh), BF16), pltpu.VMEM((D, Fh), BF16),
                            pltpu.VMEM((Fh, D), BF16)]),
        out_shape=jax.ShapeDtypeStruct((R, D), F32),
        compiler_params=_cparams(("arbitrary",)),
        name="experts",
    )(tile_expert, n_valid, xs, wg, wu, wd)


def _combine_kernel(pos_ref, pos_next_ref, x1_ref, wt_ref, p_ref, gp_ref, wpg_ref, wpp_ref, gf_ref,
                    ys_ref, o_ref, buf, sem):
    i = pl.program_id(0)
    n = pl.num_programs(0)
    tm = x1_ref.shape[0]
    slot = i % 2

    def gather(idx_ref, s):
        def issue(t, c):
            for k in range(2):
                pltpu.make_async_copy(ys_ref.at[pl.ds(idx_ref[2 * t + k], 1)],
                                      buf.at[s, k, pl.ds(t, 1)], sem.at[s]).start()
            return c

        lax.fori_loop(0, tm, issue, 0, unroll=8)

    @pl.when(i == 0)
    def _():
        gather(pos_ref, 0)

    @pl.when(i + 1 < n)
    def _():
        gather(pos_next_ref, 1 - slot)

    for k in range(2):
        pltpu.make_async_copy(ys_ref.at[pl.ds(0, tm)], buf.at[slot, k], sem.at[slot]).wait()
    wt = wt_ref[...]
    x2 = x1_ref[...] + wt[:, 0:1] * buf[slot, 0] + wt[:, 1:2] * buf[slot, 1]
    hp = _rms(x2, gp_ref[...]).astype(BF16)
    gate = jax.nn.sigmoid(jnp.dot(hp, wpg_ref[...], preferred_element_type=F32))
    proj = jnp.dot(p_ref[...].astype(BF16), wpp_ref[...], preferred_element_type=F32)
    x3 = x2 + gate * proj
    o_ref[...] = _rms(x3, gf_ref[...])


def _combine(pos_flat, x1, wts, p2, g_ple, w_ple_gate_b, w_ple_proj_b, g_final, ys):
    T, D = x1.shape
    P = p2.shape[1]
    tm = TM_COMBINE
    n = T // tm
    row = lambda i: (i, 0)
    const = lambda i: (0, 0)
    return pl.pallas_call(
        _combine_kernel,
        grid=(n,),
        in_specs=[pl.BlockSpec((2 * tm,), lambda i: (i,), memory_space=pltpu.SMEM),
                  pl.BlockSpec((2 * tm,), lambda i: (jnp.minimum(i + 1, n - 1),),
                               memory_space=pltpu.SMEM),
                  pl.BlockSpec((tm, D), row), pl.BlockSpec((tm, LANES), row),
                  pl.BlockSpec((tm, P), row), pl.BlockSpec((1, D), const),
                  pl.BlockSpec((D, D), const), pl.BlockSpec((P, D), const),
                  pl.BlockSpec((1, D), const), pl.BlockSpec(memory_space=pl.ANY)],
        out_specs=pl.BlockSpec((tm, D), row),
        out_shape=jax.ShapeDtypeStruct((T, D), F32),
        scratch_shapes=[pltpu.VMEM((2, 2, tm, D), F32), pltpu.SemaphoreType.DMA((2,))],
        compiler_params=_cparams(("arbitrary",)),
        name="combine",
    )(pos_flat, pos_flat, x1, wts, p2, g_ple, w_ple_gate_b, w_ple_proj_b, g_final, ys)


def _block_diag(w):
    nb, d, _ = w.shape
    eye = jnp.eye(nb, dtype=w.dtype)
    return (eye[:, None, :, None] * w[:, :, None, :]).reshape(nb * d, nb * d)


def kernel(x, p, positions, g_mix, w_in, sinks, conv_w, conv_b, lru_wa, lru_ba, lru_wx, lru_bx,
           lru_lambda, g_attn_out, g_lru_out, w_out, g_ffn, w_router_group, b_router_group,
           w_router_expert, b_router_expert, w_expert_gate, w_expert_up, w_expert_down, g_ple,
           w_ple_gate, w_ple_proj, g_final):
    B, S, D = x.shape
    assert w_in.shape[0] == 1, "single-layer block only"
    T = B * S
    lru_w = conv_w.shape[-1]
    n_rows = 2 * T + N_EXPERTS * TM_EXPERT
    n_tiles = n_rows // TM_EXPERT
    pos2 = positions.reshape(T, 1).astype(I32)
    x2 = x.reshape(T, D)
    q, k, v, lx, lg = _inproj(x2, pos2, g_mix[0][None], w_in[0].astype(BF16), lru_w)
    mix_a = _attention(q, k, v, sinks[0], g_attn_out[0][None], B, S)
    mix_b = _lru(lx.reshape(B, S, lru_w), lg.reshape(B, S, lru_w), conv_w[0], conv_b[0][None],
                 _block_diag(lru_wa[0]).astype(BF16), lru_ba[0][None],
                 _block_diag(lru_wx[0]).astype(BF16), lru_bx[0][None],
                 lru_lambda[0][None], g_lru_out[0][None]).reshape(T, lru_w)
    n_router = N_GROUPS + N_EXPERTS
    w_router = jnp.pad(jnp.concatenate([w_router_group[0], w_router_expert[0]], axis=1),
                       ((0, 0), (0, LANES - n_router)))
    b_router = jnp.pad(jnp.concatenate([b_router_group[0], b_router_expert[0]]),
                       (0, LANES - n_router))[None]
    x1, h2, eid, wts = _outproj(x2, mix_a, mix_b, w_out[0].astype(BF16), g_ffn[0][None],
                                w_router, b_router)
    pos, tile_expert, n_valid, meta = _plan(eid, n_tiles)
    pos_flat = pos[:, :2].reshape(2 * T)
    n_valid = n_valid[0, :1]
    xs = _dispatch(pos_flat, meta[0], meta[1], n_valid, h2, n_rows)
    Fh = w_expert_gate.shape[-1]
    ys = _experts(tile_expert[:, 0], n_valid, xs,
                  w_expert_gate[0].reshape(N_EXPERTS, D, Fh),
                  w_expert_up[0].reshape(N_EXPERTS, D, Fh),
                  w_expert_down[0].reshape(N_EXPERTS, Fh, D))
    out = _combine(pos_flat, x1, wts, p[0].reshape(T, -1), g_ple[0][None],
                   w_ple_gate[0].astype(BF16), w_ple_proj[0].astype(BF16), g_final[None], ys)
    return out.reshape(B, S, D)
```

```python
import functools
import math

import jax
import jax.numpy as jnp
from jax import lax
from jax.experimental import pallas as pl
from jax.experimental.pallas import tpu as pltpu

F32 = jnp.float32
BF16 = jnp.bfloat16
I32 = jnp.int32

EPS = 1e-6
N_HEADS = 8
N_KV_HEADS = 2
HEAD_DIM = 64
WINDOW = 128
ROPE_THETA = 10000.0
CONV_WIDTH = 4
LRU_C = 8.0
N_GROUPS = 4
EXPERTS_PER_GROUP = 8
N_EXPERTS = N_GROUPS * EXPERTS_PER_GROUP
NEG_INF = -1e30
LANES = 128
SUBLANES = 8

TM_PROJ = 512
TQ_ATTN = 512
TL_LRU = 256
TM_PLAN = 1024
TM_EXPERT = 256
TM_DISPATCH = 512
TM_COMBINE = 256
VMEM_LIMIT = 56 * 1024 * 1024


def _rms(x, g):
    ms = jnp.mean(x * x, axis=-1, keepdims=True)
    return x * lax.rsqrt(ms + EPS) * g


def _cparams(sem):
    return pltpu.CompilerParams(dimension_semantics=sem, vmem_limit_bytes=VMEM_LIMIT)


def _inproj_kernel(x_ref, pos_ref, g_ref, w_ref, q_ref, k_ref, v_ref, lx_ref, lg_ref):
    h = _rms(x_ref[...], g_ref[...]).astype(BF16)
    proj = jnp.dot(h, w_ref[...], preferred_element_type=F32)
    lane = lax.broadcasted_iota(I32, (1, LANES), 1)
    fidx = (lane % (HEAD_DIM // 2)).astype(F32)
    inv_freq = jnp.exp(fidx * (-2.0 / HEAD_DIM * math.log(ROPE_THETA)))
    ang = pos_ref[...].astype(F32) * inv_freq
    cos = jnp.cos(ang)
    sin = jnp.sin(ang)
    first_half = (lane % HEAD_DIM) < (HEAD_DIM // 2)
    sin_signed = jnp.where(first_half, -sin, sin)

    def rope(t):
        partner = jnp.where(first_half, pltpu.roll(t, LANES - HEAD_DIM // 2, 1),
                            pltpu.roll(t, HEAD_DIM // 2, 1))
        return t * cos + partner * sin_signed

    attn_w = N_HEADS * HEAD_DIM
    scale = HEAD_DIM ** -0.5
    assert math.frexp(scale)[0] == 0.5, "score scale must be a power of two to fold into q exactly"
    for c in range(attn_w // LANES):
        q_ref[:, c * LANES:(c + 1) * LANES] = (
            rope(proj[:, c * LANES:(c + 1) * LANES]) * scale).astype(BF16)
    kv_w = N_KV_HEADS * HEAD_DIM
    assert kv_w == LANES
    k = rope(proj[:, attn_w:attn_w + kv_w])
    v = proj[:, attn_w + kv_w:attn_w + 2 * kv_w]
    k_ref[:, :kv_w] = k.astype(BF16)
    k_ref[:, kv_w:] = pltpu.roll(k, HEAD_DIM, 1).astype(BF16)
    v_ref[:, :kv_w] = v.astype(BF16)
    v_ref[:, kv_w:] = pltpu.roll(v, HEAD_DIM, 1).astype(BF16)
    off_lx = attn_w + 2 * kv_w
    lru_w = lx_ref.shape[1]
    lx_ref[...] = proj[:, off_lx:off_lx + lru_w]
    lg_ref[...] = proj[:, off_lx + lru_w:off_lx + 2 * lru_w]


def _inproj(x2, pos2, g_mix, w_in_b, lru_w):
    T, D = x2.shape
    tm = TM_PROJ
    attn_w = N_HEADS * HEAD_DIM
    kv_w = N_KV_HEADS * HEAD_DIM
    row = lambda i: (i, 0)
    const = lambda i: (0, 0)
    return pl.pallas_call(
        _inproj_kernel,
        grid=(T // tm,),
        in_specs=[pl.BlockSpec((tm, D), row), pl.BlockSpec((tm, 1), row),
                  pl.BlockSpec((1, D), const), pl.BlockSpec(w_in_b.shape, const)],
        out_specs=[pl.BlockSpec((tm, attn_w), row), pl.BlockSpec((tm, 2 * kv_w), row),
                   pl.BlockSpec((tm, 2 * kv_w), row), pl.BlockSpec((tm, lru_w), row),
                   pl.BlockSpec((tm, lru_w), row)],
        out_shape=[jax.ShapeDtypeStruct((T, attn_w), BF16),
                   jax.ShapeDtypeStruct((T, 2 * kv_w), BF16),
                   jax.ShapeDtypeStruct((T, 2 * kv_w), BF16), jax.ShapeDtypeStruct((T, lru_w), F32),
                   jax.ShapeDtypeStruct((T, lru_w), F32)],
        compiler_params=_cparams(("arbitrary",)),
        name="inproj",
    )(x2, pos2, g_mix, w_in_b)


def _attn_kernel(sinks_ref, q_ref, k_ref, v_ref, kp_ref, vp_ref, g_ref, o_ref):
    n = pl.program_id(1)
    blk = WINDOW
    hd = HEAD_DIM
    nqb = q_ref.shape[0] // blk
    n_pairs = N_HEADS // 2
    pairs_per_group = n_pairs // N_KV_HEADS
    lane = lax.broadcasted_iota(I32, (1, LANES), 1)
    first = lane < hd
    qi = lax.broadcasted_iota(I32, (blk, 2 * blk), 0)
    kj = lax.broadcasted_iota(I32, (blk, 2 * blk), 1)
    rel = qi + blk - kj
    band = (rel >= 0) & (rel < WINDOW)
    bd_row = lax.broadcasted_iota(I32, (4 * blk, LANES), 0)
    bd_lane = lax.broadcasted_iota(I32, (4 * blk, LANES), 1)
    ones_bd = ((bd_row < 2 * blk) == (bd_lane < hd)).astype(BF16)
    zero = jnp.zeros((), BF16)

    def block_diag(t):
        same, swapped = t[:, :LANES], t[:, LANES:]
        out = []
        for g in range(N_KV_HEADS):
            top = jnp.where(first, same if g == 0 else swapped, zero)
            bot = jnp.where(first, zero, swapped if g == 0 else same)
            out.append(jnp.concatenate([top, bot], axis=0))
        return out

    for j in range(nqb):
        rows = slice(j * blk, (j + 1) * blk)
        if j == 0:
            k_prev, v_prev = kp_ref[...], vp_ref[...]
            mask = band & ((kj >= blk) | (n > 0))
        else:
            prev_rows = slice((j - 1) * blk, j * blk)
            k_prev, v_prev = k_ref[prev_rows, :], v_ref[prev_rows, :]
            mask = band
        k_bd = block_diag(jnp.concatenate([k_prev, k_ref[rows, :]], axis=0))
        v_bd = block_diag(jnp.concatenate([v_prev, v_ref[rows, :]], axis=0))
        outs = []
        for p in range(n_pairs):
            g = p // pairs_per_group
            qp = q_ref[rows, p * LANES:(p + 1) * LANES]
            s = lax.dot_general(qp, k_bd[g], (((1,), (1,)), ((), ())), preferred_element_type=F32)
            es, ms = [], []
            for a in range(2):
                sa = jnp.where(mask, s[:, a * 2 * blk:(a + 1) * 2 * blk], NEG_INF)
                m = jnp.maximum(jnp.max(sa, axis=-1, keepdims=True), sinks_ref[2 * p + a])
                es.append(jnp.exp(sa - m))
                ms.append(m)
            e = jnp.concatenate(es, axis=1).astype(BF16)
            od = jnp.dot(e, jnp.concatenate([v_bd[g], ones_bd], axis=1), preferred_element_type=F32)
            sink_term = jnp.where(first, jnp.exp(sinks_ref[2 * p] - ms[0]),
                                  jnp.exp(sinks_ref[2 * p + 1] - ms[1]))
            outs.append(od[:, :LANES] / (od[:, LANES:] + sink_term))
        attn = jnp.concatenate(outs, axis=1)
        o_ref[rows, :] = _rms(attn, g_ref[...]).astype(BF16)


def _attention(q, k, v, sinks, g_attn, B, S):
    T, attn_w = q.shape
    kv_w = k.shape[1]
    blk = WINDOW
    tq = TQ_ATTN
    nt = S // tq
    cur = lambda b, n: (b * nt + n, 0)
    prev = lambda b, n: (b * (S // blk) + jnp.maximum(n * (tq // blk) - 1, 0), 0)
    return pl.pallas_call(
        _attn_kernel,
        grid=(B, nt),
        in_specs=[pl.BlockSpec(memory_space=pltpu.SMEM),
                  pl.BlockSpec((tq, attn_w), cur),
                  pl.BlockSpec((tq, kv_w), cur), pl.BlockSpec((tq, kv_w), cur),
                  pl.BlockSpec((blk, kv_w), prev), pl.BlockSpec((blk, kv_w), prev),
                  pl.BlockSpec((1, attn_w), lambda b, n: (0, 0))],
        out_specs=pl.BlockSpec((tq, attn_w), cur),
        out_shape=jax.ShapeDtypeStruct((T, attn_w), BF16),
        compiler_params=_cparams(("arbitrary", "arbitrary")),
        name="attn",
    )(sinks, q, k, v, k, v, g_attn)


def _neg_expm1(x, exp_x):
    series = -x * (1.0 + x * (0.5 + x * (1.0 / 6.0 + x * (1.0 / 24.0))))
    return jnp.where(x > -0.02, series, 1.0 - exp_x)


def _gelu_tanh(x):
    return 0.5 * x * (1.0 + jnp.tanh(math.sqrt(2.0 / math.pi) * (x + 0.044715 * (x * x * x))))


def _lru_kernel(lx_ref, lg_ref, cw_ref, cb_ref, wa_ref, ba_ref, wx_ref, bx_ref, lam_ref, g_ref,
                o_ref, xpad, a_scr, b_scr, h_scr, h_carry):
    i = pl.program_id(0)
    B, tl, C = lx_ref.shape
    pad = SUBLANES

    @pl.when(i == 0)
    def _():
        xpad[:, 0:pad, :] = jnp.zeros((B, pad, C), F32)
        h_carry[...] = jnp.zeros_like(h_carry)

    @pl.when(i > 0)
    def _():
        xpad[:, 0:pad, :] = xpad[:, tl:tl + pad, :]

    xpad[:, pad:, :] = lx_ref[...]
    y = jnp.zeros((B, tl, C), F32) + cb_ref[...]
    for j in range(CONV_WIDTH):
        y = y + xpad[:, pl.ds(pad - (CONV_WIDTH - 1) + j, tl), :] * cw_ref[j:j + 1, :]
    y2 = y.reshape(B * tl, C)
    yb = y2.astype(BF16)
    gate_a = jnp.dot(yb, wa_ref[...], preferred_element_type=F32) + ba_ref[...]
    gate_x = jnp.dot(yb, wx_ref[...], preferred_element_type=F32) + bx_ref[...]
    r = jax.nn.sigmoid(gate_a)
    ig = jax.nn.sigmoid(gate_x)
    nl = -lam_ref[...]
    softplus = jnp.maximum(nl, 0.0) + jnp.log1p(jnp.exp(-jnp.abs(nl)))
    log_a = (-LRU_C) * r * softplus
    a_all = jnp.exp(log_a)
    b_all = jnp.sqrt(_neg_expm1(2.0 * log_a, a_all * a_all)) * ig * y2
    ncol = C // LANES
    for c in range(ncol):
        a_scr[c] = a_all[:, c * LANES:(c + 1) * LANES]
        b_scr[c] = b_all[:, c * LANES:(c + 1) * LANES]

    def step(t, hs):
        out = []
        for c in range(ncol):
            a = a_scr[c, pl.ds(t, B, stride=tl), :]
            b = b_scr[c, pl.ds(t, B, stride=tl), :]
            h = a * hs[c] + b
            h_scr[c, pl.ds(t, B, stride=tl), :] = h
            out.append(h)
        return tuple(out)

    hs = lax.fori_loop(0, tl, step, tuple(h_carry[c] for c in range(ncol)), unroll=8)
    for c in range(ncol):
        h_carry[c] = hs[c]
    h_all = jnp.concatenate([h_scr[c] for c in range(ncol)], axis=1)
    lru = h_all * _gelu_tanh(lg_ref[...].reshape(B * tl, C))
    o_ref[...] = _rms(lru, g_ref[...]).astype(BF16).reshape(B, tl, C)


def _lru(lx3, lg3, conv_w, conv_b, wa_bd, ba, wx_bd, bx, lam, g_lru):
    B, S, C = lx3.shape
    tl = TL_LRU
    blk = lambda i: (0, i, 0)
    const = lambda i: (0, 0)
    vec = pl.BlockSpec((1, C), const)
    return pl.pallas_call(
        _lru_kernel,
        grid=(S // tl,),
        in_specs=[pl.BlockSpec((B, tl, C), blk), pl.BlockSpec((B, tl, C), blk),
                  pl.BlockSpec((CONV_WIDTH, C), const), vec,
                  pl.BlockSpec((C, C), const), vec, pl.BlockSpec((C, C), const), vec, vec, vec],
        out_specs=pl.BlockSpec((B, tl, C), blk),
        out_shape=jax.ShapeDtypeStruct((B, S, C), BF16),
        scratch_shapes=[pltpu.VMEM((B, tl + SUBLANES, C), F32),
                        pltpu.VMEM((C // LANES, B * tl, LANES), F32),
                        pltpu.VMEM((C // LANES, B * tl, LANES), F32),
                        pltpu.VMEM((C // LANES, B * tl, LANES), F32),
                        pltpu.VMEM((C // LANES, B, LANES), F32)],
        compiler_params=_cparams(("arbitrary",)),
        name="lru",
    )(lx3, lg3, conv_w, conv_b, wa_bd, ba, wx_bd, bx, lam, g_lru)


def _outproj_kernel(x_ref, ma_ref, mb_ref, wo_ref, g_ref, wr_ref, br_ref,
                    x1_ref, h2_ref, eid_ref, wt_ref):
    half = ma_ref.shape[1]
    x1 = (x_ref[...]
          + jnp.dot(ma_ref[...], wo_ref[0:half, :], preferred_element_type=F32)
          + jnp.dot(mb_ref[...], wo_ref[half:, :], preferred_element_type=F32))
    x1_ref[...] = x1
    h2 = _rms(x1, g_ref[...])
    h2_ref[...] = h2
    h_hi = h2.astype(BF16)
    h_lo = (h2 - h_hi.astype(F32)).astype(BF16)
    hw = jnp.dot(h_hi, wr_ref[...], preferred_element_type=F32)
    logits = (hw[:, :LANES] + hw[:, LANES:]
              + jnp.dot(h_lo, wr_ref[:, :LANES], preferred_element_type=F32) + br_ref[...])
    tm = logits.shape[0]
    lane = lax.broadcasted_iota(I32, (tm, LANES), 1)
    lane_f = lane.astype(F32)
    big = float(LANES)

    def first_argmax(vals):
        m = jnp.max(vals, axis=-1, keepdims=True)
        idx = jnp.min(jnp.where(vals == m, lane_f, big), axis=-1, keepdims=True)
        return m, idx.astype(I32)

    gl = jnp.where(lane < N_GROUPS, logits, NEG_INF)
    gmax, gidx = first_argmax(gl)
    gsum = jnp.sum(jnp.where(lane < N_GROUPS, jnp.exp(gl - gmax), 0.0), axis=-1, keepdims=True)
    g_top_p = 1.0 / gsum
    lo = N_GROUPS + EXPERTS_PER_GROUP * gidx
    el = jnp.where((lane >= lo) & (lane < lo + EXPERTS_PER_GROUP), logits, NEG_INF)
    m1, i1 = first_argmax(el)
    el2 = jnp.where(lane == i1, NEG_INF, el)
    m2, i2 = first_argmax(el2)
    ratio = jnp.exp(m2 - m1)
    w1 = g_top_p / (1.0 + ratio)
    w2 = g_top_p * ratio / (1.0 + ratio)
    eid_ref[...] = jnp.where(lane == 0, i1 - N_GROUPS, jnp.where(lane == 1, i2 - N_GROUPS, 0))
    wt_ref[...] = jnp.where(lane == 0, w1, jnp.where(lane == 1, w2, 0.0))


def _outproj(x2, mix_a, mix_b, w_out_b, g_ffn, w_router, b_router):
    T, D = x2.shape
    tm = TM_PROJ
    half = mix_a.shape[1]
    row = lambda i: (i, 0)
    const = lambda i: (0, 0)
    return pl.pallas_call(
        _outproj_kernel,
        grid=(T // tm,),
        in_specs=[pl.BlockSpec((tm, D), row), pl.BlockSpec((tm, half), row),
                  pl.BlockSpec((tm, half), row), pl.BlockSpec(w_out_b.shape, const),
                  pl.BlockSpec((1, D), const), pl.BlockSpec((D, 2 * LANES), const),
                  pl.BlockSpec((1, LANES), const)],
        out_specs=[pl.BlockSpec((tm, D), row), pl.BlockSpec((tm, D), row),
                   pl.BlockSpec((tm, LANES), row), pl.BlockSpec((tm, LANES), row)],
        out_shape=[jax.ShapeDtypeStruct((T, D), F32), jax.ShapeDtypeStruct((T, D), F32),
                   jax.ShapeDtypeStruct((T, LANES), I32), jax.ShapeDtypeStruct((T, LANES), F32)],
        compiler_params=_cparams(("arbitrary",)),
        name="outproj",
    )(x2, mix_a, mix_b, w_out_b, g_ffn, w_router, b_router)


def _lane_cumsum(x):
    lane = lax.broadcasted_iota(I32, x.shape, 1)
    shift = 1
    while shift < LANES:
        x = x + jnp.where(lane >= shift, pltpu.roll(x, shift, 1), 0)
        shift *= 2
    return x


def _plan_kernel(eid_ref, pos_ref, te_ref, nv_ref, meta_ref, rank_scr, cnt_scr, start_scr):
    phase = pl.program_id(0)
    i = pl.program_id(1)
    tm = eid_ref.shape[0]
    lane = lax.broadcasted_iota(I32, (tm, LANES), 1)
    eid = eid_ref[...]
    f1 = eid[:, 0:1]
    f2 = eid[:, 1:2]
    sel1 = lane == f1
    sel2 = lane == f2

    @pl.when((phase == 0) & (i == 0))
    def _():
        cnt_scr[...] = jnp.zeros_like(cnt_scr)

    @pl.when(phase == 0)
    def _():
        onehot = (sel1 | sel2).astype(BF16)
        r = lax.broadcasted_iota(I32, (tm, tm), 0)
        c = lax.broadcasted_iota(I32, (tm, tm), 1)
        strict_lower = (r > c).astype(BF16)
        before = jnp.dot(strict_lower, onehot, preferred_element_type=F32) + cnt_scr[0:1, :]
        r1 = jnp.sum(jnp.where(sel1, before, 0.0), axis=-1, keepdims=True)
        r2 = jnp.sum(jnp.where(sel2, before, 0.0), axis=-1, keepdims=True)
        rank = jnp.where(lane == 0, r1, jnp.where(lane == 1, r2, 0.0)).astype(I32)
        rank_scr[pl.ds(pl.multiple_of(i * tm, tm), tm), :] = rank
        cnt_scr[...] = cnt_scr[...] + jnp.sum(onehot.astype(F32), axis=0, keepdims=True)

    @pl.when(phase == 1)
    def _():
        starts = start_scr[0:1, :]
        o1 = jnp.sum(jnp.where(sel1, starts, 0.0), axis=-1, keepdims=True).astype(I32)
        o2 = jnp.sum(jnp.where(sel2, starts, 0.0), axis=-1, keepdims=True).astype(I32)
        rank = rank_scr[pl.ds(pl.multiple_of(i * tm, tm), tm), :]
        pos_ref[...] = rank + jnp.where(lane == 0, o1, jnp.where(lane == 1, o2, 0))

    @pl.when((phase == 0) & (i == pl.num_programs(1) - 1))
    def _():
        shift = int(math.log2(TM_EXPERT))
        cnt = cnt_scr[...].astype(I32)
        padded = ((cnt + (TM_EXPERT - 1)) >> shift) << shift
        ends = _lane_cumsum(padded)
        start_scr[...] = (ends - padded).astype(F32)
        nt = te_ref.shape[0]
        end_tile = (ends >> shift)[0:1, :]
        tl_lane = lax.broadcasted_iota(I32, (nt, LANES), 1)
        tile = lax.broadcasted_iota(I32, (nt, LANES), 0)
        done = jnp.where((tl_lane < N_EXPERTS) & (end_tile <= tile), 1.0, 0.0)
        te = jnp.sum(done, axis=-1, keepdims=True).astype(I32)
        te_ref[...] = jnp.broadcast_to(jnp.minimum(te, N_EXPERTS - 1), (nt, LANES))
        n_valid = jnp.sum(jnp.where(tl_lane[0:1, :] == N_EXPERTS - 1, end_tile, 0).astype(F32),
                          axis=-1, keepdims=True).astype(I32)
        nv_ref[...] = jnp.broadcast_to(n_valid, nv_ref.shape)
        row = lax.broadcasted_iota(I32, cnt.shape, 0)
        meta_ref[...] = jnp.where(row == 0, ends - padded + cnt, jnp.where(row == 1, padded - cnt, 0))


def _plan(eid, n_tiles):
    T = eid.shape[0]
    tm = TM_PLAN
    const = lambda p, i: (0, 0)
    return pl.pallas_call(
        _plan_kernel,
        grid=(2, T // tm),
        in_specs=[pl.BlockSpec((tm, LANES), lambda p, i: (i, 0))],
        out_specs=[pl.BlockSpec((tm, LANES), lambda p, i: (i * p, 0)),
                   pl.BlockSpec((n_tiles, LANES), const), pl.BlockSpec((SUBLANES, LANES), const),
                   pl.BlockSpec((SUBLANES, LANES), const)],
        out_shape=[jax.ShapeDtypeStruct((T, LANES), I32), jax.ShapeDtypeStruct((n_tiles, LANES), I32),
                   jax.ShapeDtypeStruct((SUBLANES, LANES), I32),
                   jax.ShapeDtypeStruct((SUBLANES, LANES), I32)],
        scratch_shapes=[pltpu.VMEM((T, LANES), I32), pltpu.VMEM((SUBLANES, LANES), F32),
                        pltpu.VMEM((SUBLANES, LANES), F32)],
        compiler_params=_cparams(("arbitrary", "arbitrary")),
        name="plan",
    )(eid)


_PAD_BITS = tuple(1 << b for b in reversed(range(int(math.log2(TM_EXPERT // SUBLANES)))))


def _dispatch_kernel(pos_ref, seg_end_ref, seg_pad_ref, nv_ref, h_ref, xs_ref, zeros, sem, zsem):
    i = pl.program_id(0)
    tm = h_ref.shape[0]

    def issue(t, c):
        for k in range(2):
            pltpu.make_async_copy(h_ref.at[pl.ds(t, 1)], xs_ref.at[pl.ds(pos_ref[2 * t + k], 1)],
                                  sem).start()
        return c

    lax.fori_loop(0, tm, issue, 0, unroll=8)
    for k in range(2):
        pltpu.make_async_copy(h_ref, xs_ref.at[pl.ds(0, tm)], sem).wait()

    @pl.when(i == pl.num_programs(0) - 1)
    def _():
        zeros[...] = jnp.zeros_like(zeros)
        n_tiles = xs_ref.shape[0] // TM_EXPERT
        first_tail = n_tiles - N_EXPERTS

        def row_copy(e, r):
            end = seg_end_ref[e]
            n_single = jnp.minimum((-end) & (SUBLANES - 1), seg_pad_ref[e])
            return r < n_single, pltpu.make_async_copy(zeros.at[pl.ds(0, 1)],
                                                       xs_ref.at[pl.ds(end + r, 1)], zsem)

        def chunk_copy(e, bit):
            end = seg_end_ref[e]
            n_single = jnp.minimum((-end) & (SUBLANES - 1), seg_pad_ref[e])
            n = (seg_pad_ref[e] - n_single) // SUBLANES
            dst = pl.multiple_of(end + n_single + SUBLANES * (n & ~(2 * bit - 1)), SUBLANES)
            rows = SUBLANES * bit
            return (n & bit) != 0, pltpu.make_async_copy(zeros.at[pl.ds(0, rows)],
                                                         xs_ref.at[pl.ds(dst, rows)], zsem)

        def tail_copy(j):
            return j >= nv_ref[0], pltpu.make_async_copy(
                zeros, xs_ref.at[pl.ds(pl.multiple_of(j * TM_EXPERT, TM_EXPERT), TM_EXPERT)], zsem)

        def for_each_fill(act):
            def per_expert(e, c):
                for r in range(SUBLANES - 1):
                    pred, cp = row_copy(e, r)
                    pl.when(pred)(functools.partial(act, cp))
                for bit in _PAD_BITS:
                    pred, cp = chunk_copy(e, bit)
                    pl.when(pred)(functools.partial(act, cp))
                return c

            lax.fori_loop(0, N_EXPERTS, per_expert, 0)

            def per_tile(j, c):
                pred, cp = tail_copy(j)
                pl.when(pred)(functools.partial(act, cp))
                return c

            lax.fori_loop(first_tail, n_tiles, per_tile, 0)

        for_each_fill(lambda cp: cp.start())
        for_each_fill(lambda cp: cp.wait())


def _dispatch(pos_flat, seg_end, seg_pad, n_valid, h2, n_rows):
    T, D = h2.shape
    tm = TM_DISPATCH
    smem = pl.BlockSpec(memory_space=pltpu.SMEM)
    return pl.pallas_call(
        _dispatch_kernel,
        grid=(T // tm,),
        in_specs=[pl.BlockSpec((2 * tm,), lambda i: (i,), memory_space=pltpu.SMEM),
                  smem, smem, smem, pl.BlockSpec((tm, D), lambda i: (i, 0))],
        out_specs=pl.BlockSpec(memory_space=pl.ANY),
        out_shape=jax.ShapeDtypeStruct((n_rows, D), F32),
        scratch_shapes=[pltpu.VMEM((TM_EXPERT, D), F32), pltpu.SemaphoreType.DMA(()),
                        pltpu.SemaphoreType.DMA(())],
        compiler_params=_cparams(("arbitrary",)),
        name="dispatch",
    )(pos_flat, seg_end, seg_pad, n_valid, h2)


def _experts_kernel(te_ref, nv_ref, xs_ref, wg_ref, wu_ref, wd_ref, ys_ref, wg_b, wu_b, wd_b):
    j = pl.program_id(0)
    new_expert = (j == 0) | (te_ref[j] != te_ref[jnp.maximum(j - 1, 0)])

    @pl.when(new_expert)
    def _():
        wg_b[...] = wg_ref[0].astype(BF16)
        wu_b[...] = wu_ref[0].astype(BF16)
        wd_b[...] = wd_ref[0].astype(BF16)

    @pl.when(j < nv_ref[0])
    def _():
        x = xs_ref[...].astype(BF16)
        a = jnp.dot(x, wg_b[...], preferred_element_type=F32)
        u = jnp.dot(x, wu_b[...], preferred_element_type=F32)
        hid = (a * jax.nn.sigmoid(a) * u).astype(BF16)
        ys_ref[...] = jnp.dot(hid, wd_b[...], preferred_element_type=F32)

    @pl.when(j >= nv_ref[0])
    def _():
        ys_ref[...] = jnp.zeros_like(ys_ref)


def _experts(tile_expert, n_valid, xs, wg, wu, wd):
    R, D = xs.shape
    E, _, Fh = wg.shape
    tm = TM_EXPERT
    n_tiles = R // tm
    rows = lambda j, te, nv: (j, 0)
    wsel = lambda j, te, nv: (te[j], 0, 0)
    return pl.pallas_call(
        _experts_kernel,
        grid_spec=pltpu.PrefetchScalarGridSpec(
            num_scalar_prefetch=2,
            grid=(n_tiles,),
            in_specs=[pl.BlockSpec((tm, D), rows),
                      pl.BlockSpec((1, D, Fh), wsel), pl.BlockSpec((1, D, Fh), wsel),
                      pl.BlockSpec((1, Fh, D), wsel)],
            out_specs=pl.BlockSpec((tm, D), rows),
            scratch_shapes=[pltpu.VMEM((D, Fh), BF16), pltpu.VMEM((D, Fh), BF16),
                            pltpu.VMEM((Fh, D), BF16)]),
        out_shape=jax.ShapeDtypeStruct((R, D), F32),
        compiler_params=_cparams(("arbitrary",)),
        name="experts",
    )(tile_expert, n_valid, xs, wg, wu, wd)


def _combine_kernel(pos_ref, pos_next_ref, x1_ref, wt_ref, p_ref, gp_ref, wpg_ref, wpp_ref, gf_ref,
                    ys_ref, o_ref, buf, sem):
    i = pl.program_id(0)
    n = pl.num_programs(0)
    tm = x1_ref.shape[0]
    slot = i % 2

    def gather(idx_ref, s):
        def issue(t, c):
            for k in range(2):
                pltpu.make_async_copy(ys_ref.at[pl.ds(idx_ref[2 * t + k], 1)],
                                      buf.at[s, k, pl.ds(t, 1)], sem.at[s]).start()
            return c

        lax.fori_loop(0, tm, issue, 0, unroll=8)

    @pl.when(i == 0)
    def _():
        gather(pos_ref, 0)

    @pl.when(i + 1 < n)
    def _():
        gather(pos_next_ref, 1 - slot)

    for k in range(2):
        pltpu.make_async_copy(ys_ref.at[pl.ds(0, tm)], buf.at[slot, k], sem.at[slot]).wait()
    wt = wt_ref[...]
    x2 = x1_ref[...] + wt[:, 0:1] * buf[slot, 0] + wt[:, 1:2] * buf[slot, 1]
    hp = _rms(x2, gp_ref[...]).astype(BF16)
    gate = jax.nn.sigmoid(jnp.dot(hp, wpg_ref[...], preferred_element_type=F32))
    proj = jnp.dot(p_ref[...].astype(BF16), wpp_ref[...], preferred_element_type=F32)
    x3 = x2 + gate * proj
    o_ref[...] = _rms(x3, gf_ref[...])


def _combine(pos_flat, x1, wts, p2, g_ple, w_ple_gate_b, w_ple_proj_b, g_final, ys):
    T, D = x1.shape
    P = p2.shape[1]
    tm = TM_COMBINE
    n = T // tm
    row = lambda i: (i, 0)
    const = lambda i: (0, 0)
    return pl.pallas_call(
        _combine_kernel,
        grid=(n,),
        in_specs=[pl.BlockSpec((2 * tm,), lambda i: (i,), memory_space=pltpu.SMEM),
                  pl.BlockSpec((2 * tm,), lambda i: (jnp.minimum(i + 1, n - 1),),
                               memory_space=pltpu.SMEM),
                  pl.BlockSpec((tm, D), row), pl.BlockSpec((tm, LANES), row),
                  pl.BlockSpec((tm, P), row), pl.BlockSpec((1, D), const),
                  pl.BlockSpec((D, D), const), pl.BlockSpec((P, D), const),
                  pl.BlockSpec((1, D), const), pl.BlockSpec(memory_space=pl.ANY)],
        out_specs=pl.BlockSpec((tm, D), row),
        out_shape=jax.ShapeDtypeStruct((T, D), F32),
        scratch_shapes=[pltpu.VMEM((2, 2, tm, D), F32), pltpu.SemaphoreType.DMA((2,))],
        compiler_params=_cparams(("arbitrary",)),
        name="combine",
    )(pos_flat, pos_flat, x1, wts, p2, g_ple, w_ple_gate_b, w_ple_proj_b, g_final, ys)


def _block_diag(w):
    nb, d, _ = w.shape
    eye = jnp.eye(nb, dtype=w.dtype)
    return (eye[:, None, :, None] * w[:, :, None, :]).reshape(nb * d, nb * d)


def kernel(x, p, positions, g_mix, w_in, sinks, conv_w, conv_b, lru_wa, lru_ba, lru_wx, lru_bx,
           lru_lambda, g_attn_out, g_lru_out, w_out, g_ffn, w_router_group, b_router_group,
           w_router_expert, b_router_expert, w_expert_gate, w_expert_up, w_expert_down, g_ple,
           w_ple_gate, w_ple_proj, g_final):
    B, S, D = x.shape
    assert w_in.shape[0] == 1, "single-layer block only"
    T = B * S
    lru_w = conv_w.shape[-1]
    n_rows = 2 * T + N_EXPERTS * TM_EXPERT
    n_tiles = n_rows // TM_EXPERT
    pos2 = positions.reshape(T, 1).astype(I32)
    x2 = x.reshape(T, D)
    q, k, v, lx, lg = _inproj(x2, pos2, g_mix[0][None], w_in[0].astype(BF16), lru_w)
    mix_a = _attention(q, k, v, sinks[0], g_attn_out[0][None], B, S)
    mix_b = _lru(lx.reshape(B, S, lru_w), lg.reshape(B, S, lru_w), conv_w[0], conv_b[0][None],
                 _block_diag(lru_wa[0]).astype(BF16), lru_ba[0][None],
                 _block_diag(lru_wx[0]).astype(BF16), lru_bx[0][None],
                 lru_lambda[0][None], g_lru_out[0][None]).reshape(T, lru_w)
    n_router = N_GROUPS + N_EXPERTS
    w_router = jnp.pad(jnp.concatenate([w_router_group[0], w_router_expert[0]], axis=1),
                       ((0, 0), (0, LANES - n_router)))
    b_router = jnp.pad(jnp.concatenate([b_router_group[0], b_router_expert[0]]),
                       (0, LANES - n_router))[None]
    w_router_hi = w_router.astype(BF16)
    w_router_lo = (w_router - w_router_hi.astype(F32)).astype(BF16)
    x1, h2, eid, wts = _outproj(x2, mix_a, mix_b, w_out[0].astype(BF16), g_ffn[0][None],
                                jnp.concatenate([w_router_hi, w_router_lo], axis=1), b_router)
    pos, tile_expert, n_valid, meta = _plan(eid, n_tiles)
    pos_flat = pos[:, :2].reshape(2 * T)
    n_valid = n_valid[0, :1]
    xs = _dispatch(pos_flat, meta[0], meta[1], n_valid, h2, n_rows)
    Fh = w_expert_gate.shape[-1]
    ys = _experts(tile_expert[:, 0], n_valid, xs,
                  w_expert_gate[0].reshape(N_EXPERTS, D, Fh),
                  w_expert_up[0].reshape(N_EXPERTS, D, Fh),
                  w_expert_down[0].reshape(N_EXPERTS, Fh, D))
    out = _combine(pos_flat, x1, wts, p[0].reshape(T, -1), g_ple[0][None],
                   w_ple_gate[0].astype(BF16), w_ple_proj[0].astype(BF16), g_final[None], ys)
    return out.reshape(B, S, D)
```

```python
import functools
import math

import jax
import jax.numpy as jnp
from jax import lax
from jax.experimental import pallas as pl
from jax.experimental.pallas import tpu as pltpu

F32 = jnp.float32
BF16 = jnp.bfloat16
I32 = jnp.int32

EPS = 1e-6
N_HEADS = 8
N_KV_HEADS = 2
HEAD_DIM = 64
WINDOW = 128
ROPE_THETA = 10000.0
CONV_WIDTH = 4
LRU_C = 8.0
N_GROUPS = 4
EXPERTS_PER_GROUP = 8
N_EXPERTS = N_GROUPS * EXPERTS_PER_GROUP
NEG_INF = -1e30
LANES = 128
SUBLANES = 8

TM_PROJ = 512
TQ_ATTN = 512
TL_LRU = 256
TM_PLAN = 1024
TM_EXPERT = 256
TM_DISPATCH = 512
TM_COMBINE = 256
VMEM_LIMIT = 56 * 1024 * 1024


def _rms(x, g):
    ms = jnp.mean(x * x, axis=-1, keepdims=True)
    return x * lax.rsqrt(ms + EPS) * g


def _cparams(sem):
    return pltpu.CompilerParams(dimension_semantics=sem, vmem_limit_bytes=VMEM_LIMIT)


def _inproj_kernel(x_ref, pos_ref, g_ref, w_ref, q_ref, k_ref, v_ref, lx_ref, lg_ref):
    h = _rms(x_ref[...], g_ref[...]).astype(BF16)
    proj = jnp.dot(h, w_ref[...], preferred_element_type=F32)
    lane = lax.broadcasted_iota(I32, (1, LANES), 1)
    fidx = (lane % (HEAD_DIM // 2)).astype(F32)
    inv_freq = jnp.exp(fidx * (-2.0 / HEAD_DIM * math.log(ROPE_THETA)))
    ang = pos_ref[...].astype(F32) * inv_freq
    cos = jnp.cos(ang)
    sin = jnp.sin(ang)
    first_half = (lane % HEAD_DIM) < (HEAD_DIM // 2)
    sin_signed = jnp.where(first_half, -sin, sin)

    def rope(t):
        partner = jnp.where(first_half, pltpu.roll(t, LANES - HEAD_DIM // 2, 1),
                            pltpu.roll(t, HEAD_DIM // 2, 1))
        return t * cos + partner * sin_signed

    attn_w = N_HEADS * HEAD_DIM
    scale = HEAD_DIM ** -0.5
    assert math.frexp(scale)[0] == 0.5, "score scale must be a power of two to fold into q exactly"
    for c in range(attn_w // LANES):
        q_ref[:, c * LANES:(c + 1) * LANES] = (
            rope(proj[:, c * LANES:(c + 1) * LANES]) * scale).astype(BF16)
    kv_w = N_KV_HEADS * HEAD_DIM
    assert kv_w == LANES
    k = rope(proj[:, attn_w:attn_w + kv_w])
    v = proj[:, attn_w + kv_w:attn_w + 2 * kv_w]
    k_ref[:, :kv_w] = k.astype(BF16)
    k_ref[:, kv_w:] = pltpu.roll(k, HEAD_DIM, 1).astype(BF16)
    v_ref[:, :kv_w] = v.astype(BF16)
    v_ref[:, kv_w:] = pltpu.roll(v, HEAD_DIM, 1).astype(BF16)
    off_lx = attn_w + 2 * kv_w
    lru_w = lx_ref.shape[1]
    lx_ref[...] = proj[:, off_lx:off_lx + lru_w]
    lg_ref[...] = proj[:, off_lx + lru_w:off_lx + 2 * lru_w]


def _inproj(x2, pos2, g_mix, w_in_b, lru_w):
    T, D = x2.shape
    tm = TM_PROJ
    attn_w = N_HEADS * HEAD_DIM
    kv_w = N_KV_HEADS * HEAD_DIM
    row = lambda i: (i, 0)
    const = lambda i: (0, 0)
    return pl.pallas_call(
        _inproj_kernel,
        grid=(T // tm,),
        in_specs=[pl.BlockSpec((tm, D), row), pl.BlockSpec((tm, 1), row),
                  pl.BlockSpec((1, D), const), pl.BlockSpec(w_in_b.shape, const)],
        out_specs=[pl.BlockSpec((tm, attn_w), row), pl.BlockSpec((tm, 2 * kv_w), row),
                   pl.BlockSpec((tm, 2 * kv_w), row), pl.BlockSpec((tm, lru_w), row),
                   pl.BlockSpec((tm, lru_w), row)],
        out_shape=[jax.ShapeDtypeStruct((T, attn_w), BF16),
                   jax.ShapeDtypeStruct((T, 2 * kv_w), BF16),
                   jax.ShapeDtypeStruct((T, 2 * kv_w), BF16), jax.ShapeDtypeStruct((T, lru_w), F32),
                   jax.ShapeDtypeStruct((T, lru_w), F32)],
        compiler_params=_cparams(("arbitrary",)),
        name="inproj",
    )(x2, pos2, g_mix, w_in_b)


def _attn_kernel(sinks_ref, q_ref, k_ref, v_ref, kp_ref, vp_ref, g_ref, o_ref):
    n = pl.program_id(1)
    blk = WINDOW
    hd = HEAD_DIM
    nqb = q_ref.shape[0] // blk
    n_pairs = N_HEADS // 2
    pairs_per_group = n_pairs // N_KV_HEADS
    lane = lax.broadcasted_iota(I32, (1, LANES), 1)
    first = lane < hd
    qi = lax.broadcasted_iota(I32, (blk, 2 * blk), 0)
    kj = lax.broadcasted_iota(I32, (blk, 2 * blk), 1)
    rel = qi + blk - kj
    band = (rel >= 0) & (rel < WINDOW)
    bd_row = lax.broadcasted_iota(I32, (4 * blk, LANES), 0)
    bd_lane = lax.broadcasted_iota(I32, (4 * blk, LANES), 1)
    ones_bd = ((bd_row < 2 * blk) == (bd_lane < hd)).astype(BF16)
    zero = jnp.zeros((), BF16)

    def block_diag(t):
        same, swapped = t[:, :LANES], t[:, LANES:]
        out = []
        for g in range(N_KV_HEADS):
            top = jnp.where(first, same if g == 0 else swapped, zero)
            bot = jnp.where(first, zero, swapped if g == 0 else same)
            out.append(jnp.concatenate([top, bot], axis=0))
        return out

    for j in range(nqb):
        rows = slice(j * blk, (j + 1) * blk)
        if j == 0:
            k_prev, v_prev = kp_ref[...], vp_ref[...]
            mask = band & ((kj >= blk) | (n > 0))
        else:
            prev_rows = slice((j - 1) * blk, j * blk)
            k_prev, v_prev = k_ref[prev_rows, :], v_ref[prev_rows, :]
            mask = band
        k_bd = block_diag(jnp.concatenate([k_prev, k_ref[rows, :]], axis=0))
        v_bd = block_diag(jnp.concatenate([v_prev, v_ref[rows, :]], axis=0))
        outs = []
        for p in range(n_pairs):
            g = p // pairs_per_group
            qp = q_ref[rows, p * LANES:(p + 1) * LANES]
            s = lax.dot_general(qp, k_bd[g], (((1,), (1,)), ((), ())), preferred_element_type=F32)
            es, ms = [], []
            for a in range(2):
                sa = jnp.where(mask, s[:, a * 2 * blk:(a + 1) * 2 * blk], NEG_INF)
                m = jnp.maximum(jnp.max(sa, axis=-1, keepdims=True), sinks_ref[2 * p + a])
                es.append(jnp.exp(sa - m))
                ms.append(m)
            e = jnp.concatenate(es, axis=1).astype(BF16)
            od = jnp.dot(e, jnp.concatenate([v_bd[g], ones_bd], axis=1), preferred_element_type=F32)
            sink_term = jnp.where(first, jnp.exp(sinks_ref[2 * p] - ms[0]),
                                  jnp.exp(sinks_ref[2 * p + 1] - ms[1]))
            outs.append(od[:, :LANES] / (od[:, LANES:] + sink_term))
        attn = jnp.concatenate(outs, axis=1)
        o_ref[rows, :] = _rms(attn, g_ref[...]).astype(BF16)


def _attention(q, k, v, sinks, g_attn, B, S):
    T, attn_w = q.shape
    kv_w = k.shape[1]
    blk = WINDOW
    tq = TQ_ATTN
    nt = S // tq
    cur = lambda b, n: (b * nt + n, 0)
    prev = lambda b, n: (b * (S // blk) + jnp.maximum(n * (tq // blk) - 1, 0), 0)
    return pl.pallas_call(
        _attn_kernel,
        grid=(B, nt),
        in_specs=[pl.BlockSpec(memory_space=pltpu.SMEM),
                  pl.BlockSpec((tq, attn_w), cur),
                  pl.BlockSpec((tq, kv_w), cur), pl.BlockSpec((tq, kv_w), cur),
                  pl.BlockSpec((blk, kv_w), prev), pl.BlockSpec((blk, kv_w), prev),
                  pl.BlockSpec((1, attn_w), lambda b, n: (0, 0))],
        out_specs=pl.BlockSpec((tq, attn_w), cur),
        out_shape=jax.ShapeDtypeStruct((T, attn_w), BF16),
        compiler_params=_cparams(("arbitrary", "arbitrary")),
        name="attn",
    )(sinks, q, k, v, k, v, g_attn)


def _neg_expm1(x, exp_x):
    series = -x * (1.0 + x * (0.5 + x * (1.0 / 6.0 + x * (1.0 / 24.0))))
    return jnp.where(x > -0.02, series, 1.0 - exp_x)


def _gelu_tanh(x):
    return 0.5 * x * (1.0 + jnp.tanh(math.sqrt(2.0 / math.pi) * (x + 0.044715 * (x * x * x))))


def _lru_kernel(lx_ref, lg_ref, cw_ref, cb_ref, wa_ref, ba_ref, wx_ref, bx_ref, lam_ref, g_ref,
                o_ref, xpad, a_scr, b_scr, h_scr, h_carry):
    i = pl.program_id(0)
    B, tl, C = lx_ref.shape
    pad = SUBLANES

    @pl.when(i == 0)
    def _():
        xpad[:, 0:pad, :] = jnp.zeros((B, pad, C), F32)
        h_carry[...] = jnp.zeros_like(h_carry)

    @pl.when(i > 0)
    def _():
        xpad[:, 0:pad, :] = xpad[:, tl:tl + pad, :]

    xpad[:, pad:, :] = lx_ref[...]
    y = jnp.zeros((B, tl, C), F32) + cb_ref[...]
    for j in range(CONV_WIDTH):
        y = y + xpad[:, pl.ds(pad - (CONV_WIDTH - 1) + j, tl), :] * cw_ref[j:j + 1, :]
    y2 = y.reshape(B * tl, C)
    yb = y2.astype(BF16)
    gate_a = jnp.dot(yb, wa_ref[...], preferred_element_type=F32) + ba_ref[...]
    gate_x = jnp.dot(yb, wx_ref[...], preferred_element_type=F32) + bx_ref[...]
    r = jax.nn.sigmoid(gate_a)
    ig = jax.nn.sigmoid(gate_x)
    nl = -lam_ref[...]
    softplus = jnp.maximum(nl, 0.0) + jnp.log1p(jnp.exp(-jnp.abs(nl)))
    log_a = (-LRU_C) * r * softplus
    a_all = jnp.exp(log_a)
    b_all = jnp.sqrt(_neg_expm1(2.0 * log_a, a_all * a_all)) * ig * y2
    ncol = C // LANES
    for c in range(ncol):
        a_scr[c] = a_all[:, c * LANES:(c + 1) * LANES]
        b_scr[c] = b_all[:, c * LANES:(c + 1) * LANES]

    def step(t, hs):
        out = []
        for c in range(ncol):
            a = a_scr[c, pl.ds(t, B, stride=tl), :]
            b = b_scr[c, pl.ds(t, B, stride=tl), :]
            h = a * hs[c] + b
            h_scr[c, pl.ds(t, B, stride=tl), :] = h
            out.append(h)
        return tuple(out)

    hs = lax.fori_loop(0, tl, step, tuple(h_carry[c] for c in range(ncol)), unroll=8)
    for c in range(ncol):
        h_carry[c] = hs[c]
    h_all = jnp.concatenate([h_scr[c] for c in range(ncol)], axis=1)
    lru = h_all * _gelu_tanh(lg_ref[...].reshape(B * tl, C))
    o_ref[...] = _rms(lru, g_ref[...]).astype(BF16).reshape(B, tl, C)


def _lru(lx3, lg3, conv_w, conv_b, wa_bd, ba, wx_bd, bx, lam, g_lru):
    B, S, C = lx3.shape
    tl = TL_LRU
    blk = lambda i: (0, i, 0)
    const = lambda i: (0, 0)
    vec = pl.BlockSpec((1, C), const)
    return pl.pallas_call(
        _lru_kernel,
        grid=(S // tl,),
        in_specs=[pl.BlockSpec((B, tl, C), blk), pl.BlockSpec((B, tl, C), blk),
                  pl.BlockSpec((CONV_WIDTH, C), const), vec,
                  pl.BlockSpec((C, C), const), vec, pl.BlockSpec((C, C), const), vec, vec, vec],
        out_specs=pl.BlockSpec((B, tl, C), blk),
        out_shape=jax.ShapeDtypeStruct((B, S, C), BF16),
        scratch_shapes=[pltpu.VMEM((B, tl + SUBLANES, C), F32),
                        pltpu.VMEM((C // LANES, B * tl, LANES), F32),
                        pltpu.VMEM((C // LANES, B * tl, LANES), F32),
                        pltpu.VMEM((C // LANES, B * tl, LANES), F32),
                        pltpu.VMEM((C // LANES, B, LANES), F32)],
        compiler_params=_cparams(("arbitrary",)),
        name="lru",
    )(lx3, lg3, conv_w, conv_b, wa_bd, ba, wx_bd, bx, lam, g_lru)


def _outproj_kernel(x_ref, ma_ref, mb_ref, wo_ref, g_ref, wr_ref, br_ref,
                    x1_ref, h2_ref, eid_ref, wt_ref):
    half = ma_ref.shape[1]
    x1 = (x_ref[...]
          + jnp.dot(ma_ref[...], wo_ref[0:half, :], preferred_element_type=F32)
          + jnp.dot(mb_ref[...], wo_ref[half:, :], preferred_element_type=F32))
    x1_ref[...] = x1
    h2 = _rms(x1, g_ref[...])
    h2_ref[...] = h2.astype(BF16).reshape(h2_ref.shape)
    h_hi = h2.astype(BF16)
    h_lo = (h2 - h_hi.astype(F32)).astype(BF16)
    hw = jnp.dot(h_hi, wr_ref[...], preferred_element_type=F32)
    logits = (hw[:, :LANES] + hw[:, LANES:]
              + jnp.dot(h_lo, wr_ref[:, :LANES], preferred_element_type=F32) + br_ref[...])
    tm = logits.shape[0]
    lane = lax.broadcasted_iota(I32, (tm, LANES), 1)
    lane_f = lane.astype(F32)
    big = float(LANES)

    def first_argmax(vals):
        m = jnp.max(vals, axis=-1, keepdims=True)
        idx = jnp.min(jnp.where(vals == m, lane_f, big), axis=-1, keepdims=True)
        return m, idx.astype(I32)

    gl = jnp.where(lane < N_GROUPS, logits, NEG_INF)
    gmax, gidx = first_argmax(gl)
    gsum = jnp.sum(jnp.where(lane < N_GROUPS, jnp.exp(gl - gmax), 0.0), axis=-1, keepdims=True)
    g_top_p = 1.0 / gsum
    lo = N_GROUPS + EXPERTS_PER_GROUP * gidx
    el = jnp.where((lane >= lo) & (lane < lo + EXPERTS_PER_GROUP), logits, NEG_INF)
    m1, i1 = first_argmax(el)
    el2 = jnp.where(lane == i1, NEG_INF, el)
    m2, i2 = first_argmax(el2)
    ratio = jnp.exp(m2 - m1)
    w1 = g_top_p / (1.0 + ratio)
    w2 = g_top_p * ratio / (1.0 + ratio)
    eid_ref[...] = jnp.where(lane == 0, i1 - N_GROUPS, jnp.where(lane == 1, i2 - N_GROUPS, 0))
    wt_ref[...] = jnp.where(lane == 0, w1, jnp.where(lane == 1, w2, 0.0))


def _outproj(x2, mix_a, mix_b, w_out_b, g_ffn, w_router, b_router):
    T, D = x2.shape
    tm = TM_PROJ
    half = mix_a.shape[1]
    row = lambda i: (i, 0)
    const = lambda i: (0, 0)
    return pl.pallas_call(
        _outproj_kernel,
        grid=(T // tm,),
        in_specs=[pl.BlockSpec((tm, D), row), pl.BlockSpec((tm, half), row),
                  pl.BlockSpec((tm, half), row), pl.BlockSpec(w_out_b.shape, const),
                  pl.BlockSpec((1, D), const), pl.BlockSpec((D, 2 * LANES), const),
                  pl.BlockSpec((1, LANES), const)],
        out_specs=[pl.BlockSpec((tm, D), row),
                   pl.BlockSpec((tm, D // LANES, LANES), lambda i: (i, 0, 0)),
                   pl.BlockSpec((tm, LANES), row), pl.BlockSpec((tm, LANES), row)],
        out_shape=[jax.ShapeDtypeStruct((T, D), F32),
                   jax.ShapeDtypeStruct((T, D // LANES, LANES), BF16),
                   jax.ShapeDtypeStruct((T, LANES), I32), jax.ShapeDtypeStruct((T, LANES), F32)],
        compiler_params=_cparams(("arbitrary",)),
        name="outproj",
    )(x2, mix_a, mix_b, w_out_b, g_ffn, w_router, b_router)


def _lane_cumsum(x):
    lane = lax.broadcasted_iota(I32, x.shape, 1)
    shift = 1
    while shift < LANES:
        x = x + jnp.where(lane >= shift, pltpu.roll(x, shift, 1), 0)
        shift *= 2
    return x


def _plan_kernel(eid_ref, pos_ref, te_ref, nv_ref, meta_ref, rank_scr, cnt_scr, start_scr):
    phase = pl.program_id(0)
    i = pl.program_id(1)
    tm = eid_ref.shape[0]
    lane = lax.broadcasted_iota(I32, (tm, LANES), 1)
    eid = eid_ref[...]
    f1 = eid[:, 0:1]
    f2 = eid[:, 1:2]
    sel1 = lane == f1
    sel2 = lane == f2

    @pl.when((phase == 0) & (i == 0))
    def _():
        cnt_scr[...] = jnp.zeros_like(cnt_scr)

    @pl.when(phase == 0)
    def _():
        onehot = (sel1 | sel2).astype(BF16)
        r = lax.broadcasted_iota(I32, (tm, tm), 0)
        c = lax.broadcasted_iota(I32, (tm, tm), 1)
        strict_lower = (r > c).astype(BF16)
        before = jnp.dot(strict_lower, onehot, preferred_element_type=F32) + cnt_scr[0:1, :]
        r1 = jnp.sum(jnp.where(sel1, before, 0.0), axis=-1, keepdims=True)
        r2 = jnp.sum(jnp.where(sel2, before, 0.0), axis=-1, keepdims=True)
        rank = jnp.where(lane == 0, r1, jnp.where(lane == 1, r2, 0.0)).astype(I32)
        rank_scr[pl.ds(pl.multiple_of(i * tm, tm), tm), :] = rank
        cnt_scr[...] = cnt_scr[...] + jnp.sum(onehot.astype(F32), axis=0, keepdims=True)

    @pl.when(phase == 1)
    def _():
        starts = start_scr[0:1, :]
        o1 = jnp.sum(jnp.where(sel1, starts, 0.0), axis=-1, keepdims=True).astype(I32)
        o2 = jnp.sum(jnp.where(sel2, starts, 0.0), axis=-1, keepdims=True).astype(I32)
        rank = rank_scr[pl.ds(pl.multiple_of(i * tm, tm), tm), :]
        pos_ref[...] = rank + jnp.where(lane == 0, o1, jnp.where(lane == 1, o2, 0))

    @pl.when((phase == 0) & (i == pl.num_programs(1) - 1))
    def _():
        shift = int(math.log2(TM_EXPERT))
        cnt = cnt_scr[...].astype(I32)
        padded = ((cnt + (TM_EXPERT - 1)) >> shift) << shift
        ends = _lane_cumsum(padded)
        start_scr[...] = (ends - padded).astype(F32)
        nt = te_ref.shape[0]
        end_tile = (ends >> shift)[0:1, :]
        tl_lane = lax.broadcasted_iota(I32, (nt, LANES), 1)
        tile = lax.broadcasted_iota(I32, (nt, LANES), 0)
        done = jnp.where((tl_lane < N_EXPERTS) & (end_tile <= tile), 1.0, 0.0)
        te = jnp.sum(done, axis=-1, keepdims=True).astype(I32)
        te_ref[...] = jnp.broadcast_to(jnp.minimum(te, N_EXPERTS - 1), (nt, LANES))
        n_valid = jnp.sum(jnp.where(tl_lane[0:1, :] == N_EXPERTS - 1, end_tile, 0).astype(F32),
                          axis=-1, keepdims=True).astype(I32)
        nv_ref[...] = jnp.broadcast_to(n_valid, nv_ref.shape)
        row = lax.broadcasted_iota(I32, cnt.shape, 0)
        meta_ref[...] = jnp.where(row == 0, ends - padded + cnt, jnp.where(row == 1, padded - cnt, 0))


def _plan(eid, n_tiles):
    T = eid.shape[0]
    tm = TM_PLAN
    const = lambda p, i: (0, 0)
    return pl.pallas_call(
        _plan_kernel,
        grid=(2, T // tm),
        in_specs=[pl.BlockSpec((tm, LANES), lambda p, i: (i, 0))],
        out_specs=[pl.BlockSpec((tm, LANES), lambda p, i: (i * p, 0)),
                   pl.BlockSpec((n_tiles, LANES), const), pl.BlockSpec((SUBLANES, LANES), const),
                   pl.BlockSpec((SUBLANES, LANES), const)],
        out_shape=[jax.ShapeDtypeStruct((T, LANES), I32), jax.ShapeDtypeStruct((n_tiles, LANES), I32),
                   jax.ShapeDtypeStruct((SUBLANES, LANES), I32),
                   jax.ShapeDtypeStruct((SUBLANES, LANES), I32)],
        scratch_shapes=[pltpu.VMEM((T, LANES), I32), pltpu.VMEM((SUBLANES, LANES), F32),
                        pltpu.VMEM((SUBLANES, LANES), F32)],
        compiler_params=_cparams(("arbitrary", "arbitrary")),
        name="plan",
    )(eid)


_PAD_BITS = tuple(1 << b for b in reversed(range(int(math.log2(TM_EXPERT)))))


def _dispatch_kernel(pos_ref, seg_end_ref, seg_pad_ref, nv_ref, h_ref, xs_ref, zeros, sem, zsem):
    i = pl.program_id(0)
    tm = h_ref.shape[0]

    def issue(t, c):
        for k in range(2):
            pltpu.make_async_copy(h_ref.at[pl.ds(t, 1)], xs_ref.at[pl.ds(pos_ref[2 * t + k], 1)],
                                  sem).start()
        return c

    lax.fori_loop(0, tm, issue, 0, unroll=8)
    for k in range(2):
        pltpu.make_async_copy(h_ref, xs_ref.at[pl.ds(0, tm)], sem).wait()

    @pl.when(i == pl.num_programs(0) - 1)
    def _():
        zeros[...] = jnp.zeros_like(zeros)
        n_tiles = xs_ref.shape[0] // TM_EXPERT
        first_tail = n_tiles - N_EXPERTS

        def pad_copy(e, bit):
            n = seg_pad_ref[e]
            dst = seg_end_ref[e] + (n & ~(2 * bit - 1))
            return (n & bit) != 0, pltpu.make_async_copy(zeros.at[pl.ds(0, bit)],
                                                         xs_ref.at[pl.ds(dst, bit)], zsem)

        def tail_copy(j):
            return j >= nv_ref[0], pltpu.make_async_copy(
                zeros, xs_ref.at[pl.ds(j * TM_EXPERT, TM_EXPERT)], zsem)

        def for_each_fill(act):
            def per_expert(e, c):
                for bit in _PAD_BITS:
                    pred, cp = pad_copy(e, bit)
                    pl.when(pred)(functools.partial(act, cp))
                return c

            lax.fori_loop(0, N_EXPERTS, per_expert, 0)

            def per_tile(j, c):
                pred, cp = tail_copy(j)
                pl.when(pred)(functools.partial(act, cp))
                return c

            lax.fori_loop(first_tail, n_tiles, per_tile, 0)

        for_each_fill(lambda cp: cp.start())
        for_each_fill(lambda cp: cp.wait())


def _dispatch(pos_flat, seg_end, seg_pad, n_valid, h2, n_rows):
    T = h2.shape[0]
    slab = h2.shape[1:]
    tm = TM_DISPATCH
    smem = pl.BlockSpec(memory_space=pltpu.SMEM)
    return pl.pallas_call(
        _dispatch_kernel,
        grid=(T // tm,),
        in_specs=[pl.BlockSpec((2 * tm,), lambda i: (i,), memory_space=pltpu.SMEM),
                  smem, smem, smem, pl.BlockSpec((tm,) + slab, lambda i: (i, 0, 0))],
        out_specs=pl.BlockSpec(memory_space=pl.ANY),
        out_shape=jax.ShapeDtypeStruct((n_rows,) + slab, h2.dtype),
        scratch_shapes=[pltpu.VMEM((TM_EXPERT,) + slab, h2.dtype), pltpu.SemaphoreType.DMA(()),
                        pltpu.SemaphoreType.DMA(())],
        compiler_params=_cparams(("arbitrary",)),
        name="dispatch",
    )(pos_flat, seg_end, seg_pad, n_valid, h2)


def _experts_kernel(te_ref, nv_ref, xs_ref, wg_ref, wu_ref, wd_ref, ys_ref, wg_b, wu_b, wd_b):
    j = pl.program_id(0)
    new_expert = (j == 0) | (te_ref[j] != te_ref[jnp.maximum(j - 1, 0)])

    @pl.when(new_expert)
    def _():
        wg_b[...] = wg_ref[0].astype(BF16)
        wu_b[...] = wu_ref[0].astype(BF16)
        wd_b[...] = wd_ref[0].astype(BF16)

    @pl.when(j < nv_ref[0])
    def _():
        x = xs_ref[...].reshape(xs_ref.shape[0], -1)
        a = jnp.dot(x, wg_b[...], preferred_element_type=F32)
        u = jnp.dot(x, wu_b[...], preferred_element_type=F32)
        hid = (a * jax.nn.sigmoid(a) * u).astype(BF16)
        y = jnp.dot(hid, wd_b[...], preferred_element_type=F32)
        ys_ref[...] = y.astype(BF16).reshape(ys_ref.shape)

    @pl.when(j >= nv_ref[0])
    def _():
        ys_ref[...] = jnp.zeros_like(ys_ref)


def _experts(tile_expert, n_valid, xs, wg, wu, wd):
    R = xs.shape[0]
    slab = xs.shape[1:]
    E, D, Fh = wg.shape
    tm = TM_EXPERT
    n_tiles = R // tm
    rows = lambda j, te, nv: (j, 0, 0)
    wsel = lambda j, te, nv: (te[j], 0, 0)
    return pl.pallas_call(
        _experts_kernel,
        grid_spec=pltpu.PrefetchScalarGridSpec(
            num_scalar_prefetch=2,
            grid=(n_tiles,),
            in_specs=[pl.BlockSpec((tm,) + slab, rows),
                      pl.BlockSpec((1, D, Fh), wsel), pl.BlockSpec((1, D, Fh), wsel),
                      pl.BlockSpec((1, Fh, D), wsel)],
            out_specs=pl.BlockSpec((tm,) + slab, rows),
            scratch_shapes=[pltpu.VMEM((D, Fh), BF16), pltpu.VMEM((D, Fh), BF16),
                            pltpu.VMEM((Fh, D), BF16)]),
        out_shape=jax.ShapeDtypeStruct(xs.shape, xs.dtype),
        compiler_params=_cparams(("arbitrary",)),
        name="experts",
    )(tile_expert, n_valid, xs, wg, wu, wd)


def _combine_kernel(pos_ref, pos_next_ref, x1_ref, wt_ref, p_ref, gp_ref, wpg_ref, wpp_ref, gf_ref,
                    ys_ref, o_ref, buf, sem):
    i = pl.program_id(0)
    n = pl.num_programs(0)
    tm = x1_ref.shape[0]
    slot = i % 2

    def gather(idx_ref, s):
        def issue(t, c):
            for k in range(2):
                pltpu.make_async_copy(ys_ref.at[pl.ds(idx_ref[2 * t + k], 1)],
                                      buf.at[s, k, pl.ds(t, 1)], sem.at[s]).start()
            return c

        lax.fori_loop(0, tm, issue, 0, unroll=8)

    @pl.when(i == 0)
    def _():
        gather(pos_ref, 0)

    @pl.when(i + 1 < n)
    def _():
        gather(pos_next_ref, 1 - slot)

    for k in range(2):
        pltpu.make_async_copy(ys_ref.at[pl.ds(0, tm)], buf.at[slot, k], sem.at[slot]).wait()
    wt = wt_ref[...]
    y0 = buf[slot, 0].reshape(tm, -1).astype(F32)
    y1 = buf[slot, 1].reshape(tm, -1).astype(F32)
    x2 = x1_ref[...] + wt[:, 0:1] * y0 + wt[:, 1:2] * y1
    hp = _rms(x2, gp_ref[...]).astype(BF16)
    gate = jax.nn.sigmoid(jnp.dot(hp, wpg_ref[...], preferred_element_type=F32))
    proj = jnp.dot(p_ref[...].astype(BF16), wpp_ref[...], preferred_element_type=F32)
    x3 = x2 + gate * proj
    o_ref[...] = _rms(x3, gf_ref[...])


def _combine(pos_flat, x1, wts, p2, g_ple, w_ple_gate_b, w_ple_proj_b, g_final, ys):
    T, D = x1.shape
    P = p2.shape[1]
    tm = TM_COMBINE
    n = T // tm
    row = lambda i: (i, 0)
    const = lambda i: (0, 0)
    return pl.pallas_call(
        _combine_kernel,
        grid=(n,),
        in_specs=[pl.BlockSpec((2 * tm,), lambda i: (i,), memory_space=pltpu.SMEM),
                  pl.BlockSpec((2 * tm,), lambda i: (jnp.minimum(i + 1, n - 1),),
                               memory_space=pltpu.SMEM),
                  pl.BlockSpec((tm, D), row), pl.BlockSpec((tm, LANES), row),
                  pl.BlockSpec((tm, P), row), pl.BlockSpec((1, D), const),
                  pl.BlockSpec((D, D), const), pl.BlockSpec((P, D), const),
                  pl.BlockSpec((1, D), const), pl.BlockSpec(memory_space=pl.ANY)],
        out_specs=pl.BlockSpec((tm, D), row),
        out_shape=jax.ShapeDtypeStruct((T, D), F32),
        scratch_shapes=[pltpu.VMEM((2, 2, tm) + ys.shape[1:], ys.dtype),
                        pltpu.SemaphoreType.DMA((2,))],
        compiler_params=_cparams(("arbitrary",)),
        name="combine",
    )(pos_flat, pos_flat, x1, wts, p2, g_ple, w_ple_gate_b, w_ple_proj_b, g_final, ys)


def _block_diag(w):
    nb, d, _ = w.shape
    eye = jnp.eye(nb, dtype=w.dtype)
    return (eye[:, None, :, None] * w[:, :, None, :]).reshape(nb * d, nb * d)


def kernel(x, p, positions, g_mix, w_in, sinks, conv_w, conv_b, lru_wa, lru_ba, lru_wx, lru_bx,
           lru_lambda, g_attn_out, g_lru_out, w_out, g_ffn, w_router_group, b_router_group,
           w_router_expert, b_router_expert, w_expert_gate, w_expert_up, w_expert_down, g_ple,
           w_ple_gate, w_ple_proj, g_final):
    B, S, D = x.shape
    assert w_in.shape[0] == 1, "single-layer block only"
    T = B * S
    lru_w = conv_w.shape[-1]
    n_rows = 2 * T + N_EXPERTS * TM_EXPERT
    n_tiles = n_rows // TM_EXPERT
    pos2 = positions.reshape(T, 1).astype(I32)
    x2 = x.reshape(T, D)
    q, k, v, lx, lg = _inproj(x2, pos2, g_mix[0][None], w_in[0].astype(BF16), lru_w)
    mix_a = _attention(q, k, v, sinks[0], g_attn_out[0][None], B, S)
    mix_b = _lru(lx.reshape(B, S, lru_w), lg.reshape(B, S, lru_w), conv_w[0], conv_b[0][None],
                 _block_diag(lru_wa[0]).astype(BF16), lru_ba[0][None],
                 _block_diag(lru_wx[0]).astype(BF16), lru_bx[0][None],
                 lru_lambda[0][None], g_lru_out[0][None]).reshape(T, lru_w)
    n_router = N_GROUPS + N_EXPERTS
    w_router = jnp.pad(jnp.concatenate([w_router_group[0], w_router_expert[0]], axis=1),
                       ((0, 0), (0, LANES - n_router)))
    b_router = jnp.pad(jnp.concatenate([b_router_group[0], b_router_expert[0]]),
                       (0, LANES - n_router))[None]
    w_router_hi = w_router.astype(BF16)
    w_router_lo = (w_router - w_router_hi.astype(F32)).astype(BF16)
    x1, h2, eid, wts = _outproj(x2, mix_a, mix_b, w_out[0].astype(BF16), g_ffn[0][None],
                                jnp.concatenate([w_router_hi, w_router_lo], axis=1), b_router)
    pos, tile_expert, n_valid, meta = _plan(eid, n_tiles)
    pos_flat = pos[:, :2].reshape(2 * T)
    n_valid = n_valid[0, :1]
    xs = _dispatch(pos_flat, meta[0], meta[1], n_valid, h2, n_rows)
    Fh = w_expert_gate.shape[-1]
    ys = _experts(tile_expert[:, 0], n_valid, xs,
                  w_expert_gate[0].reshape(N_EXPERTS, D, Fh),
                  w_expert_up[0].reshape(N_EXPERTS, D, Fh),
                  w_expert_down[0].reshape(N_EXPERTS, Fh, D))
    out = _combine(pos_flat, x1, wts, p[0].reshape(T, -1), g_ple[0][None],
                   w_ple_gate[0].astype(BF16), w_ple_proj[0].astype(BF16), g_final[None], ys)
    return out.reshape(B, S, D)
```

```python
import functools
import math

import jax
import jax.numpy as jnp
from jax import lax
from jax.experimental import pallas as pl
from jax.experimental.pallas import tpu as pltpu

F32 = jnp.float32
BF16 = jnp.bfloat16
I32 = jnp.int32

EPS = 1e-6
N_HEADS = 8
N_KV_HEADS = 2
HEAD_DIM = 64
WINDOW = 128
ROPE_THETA = 10000.0
CONV_WIDTH = 4
LRU_C = 8.0
N_GROUPS = 4
EXPERTS_PER_GROUP = 8
N_EXPERTS = N_GROUPS * EXPERTS_PER_GROUP
NEG_INF = -1e30
LANES = 128
SUBLANES = 8

TM_PROJ = 512
TQ_ATTN = 512
TL_LRU = 256
TM_PLAN = 1024
TM_EXPERT = 512
TM_DISPATCH = 512
TM_COMBINE = 256
VMEM_LIMIT = 56 * 1024 * 1024


def _rms(x, g):
    ms = jnp.mean(x * x, axis=-1, keepdims=True)
    return x * lax.rsqrt(ms + EPS) * g


def _cparams(sem):
    return pltpu.CompilerParams(dimension_semantics=sem, vmem_limit_bytes=VMEM_LIMIT)


def _inproj_kernel(x_ref, pos_ref, g_ref, w_ref, q_ref, k_ref, v_ref, lx_ref, lg_ref):
    h = _rms(x_ref[...], g_ref[...]).astype(BF16)
    proj = jnp.dot(h, w_ref[...], preferred_element_type=F32)
    lane = lax.broadcasted_iota(I32, (1, LANES), 1)
    fidx = (lane % (HEAD_DIM // 2)).astype(F32)
    inv_freq = jnp.exp(fidx * (-2.0 / HEAD_DIM * math.log(ROPE_THETA)))
    ang = pos_ref[...].astype(F32) * inv_freq
    cos = jnp.cos(ang)
    sin = jnp.sin(ang)
    first_half = (lane % HEAD_DIM) < (HEAD_DIM // 2)
    sin_signed = jnp.where(first_half, -sin, sin)

    def rope(t):
        partner = jnp.where(first_half, pltpu.roll(t, LANES - HEAD_DIM // 2, 1),
                            pltpu.roll(t, HEAD_DIM // 2, 1))
        return t * cos + partner * sin_signed

    attn_w = N_HEADS * HEAD_DIM
    scale = HEAD_DIM ** -0.5
    assert math.frexp(scale)[0] == 0.5, "score scale must be a power of two to fold into q exactly"
    for c in range(attn_w // LANES):
        q_ref[:, c * LANES:(c + 1) * LANES] = (
            rope(proj[:, c * LANES:(c + 1) * LANES]) * scale).astype(BF16)
    kv_w = N_KV_HEADS * HEAD_DIM
    assert kv_w == LANES
    k = rope(proj[:, attn_w:attn_w + kv_w])
    v = proj[:, attn_w + kv_w:attn_w + 2 * kv_w]
    k_ref[:, :kv_w] = k.astype(BF16)
    k_ref[:, kv_w:] = pltpu.roll(k, HEAD_DIM, 1).astype(BF16)
    v_ref[:, :kv_w] = v.astype(BF16)
    v_ref[:, kv_w:] = pltpu.roll(v, HEAD_DIM, 1).astype(BF16)
    off_lx = attn_w + 2 * kv_w
    lru_w = lx_ref.shape[1]
    lx_ref[...] = proj[:, off_lx:off_lx + lru_w]
    lg_ref[...] = proj[:, off_lx + lru_w:off_lx + 2 * lru_w]


def _inproj(x2, pos2, g_mix, w_in_b, lru_w):
    T, D = x2.shape
    tm = TM_PROJ
    attn_w = N_HEADS * HEAD_DIM
    kv_w = N_KV_HEADS * HEAD_DIM
    row = lambda i: (i, 0)
    const = lambda i: (0, 0)
    return pl.pallas_call(
        _inproj_kernel,
        grid=(T // tm,),
        in_specs=[pl.BlockSpec((tm, D), row), pl.BlockSpec((tm, 1), row),
                  pl.BlockSpec((1, D), const), pl.BlockSpec(w_in_b.shape, const)],
        out_specs=[pl.BlockSpec((tm, attn_w), row), pl.BlockSpec((tm, 2 * kv_w), row),
                   pl.BlockSpec((tm, 2 * kv_w), row), pl.BlockSpec((tm, lru_w), row),
                   pl.BlockSpec((tm, lru_w), row)],
        out_shape=[jax.ShapeDtypeStruct((T, attn_w), BF16),
                   jax.ShapeDtypeStruct((T, 2 * kv_w), BF16),
                   jax.ShapeDtypeStruct((T, 2 * kv_w), BF16), jax.ShapeDtypeStruct((T, lru_w), F32),
                   jax.ShapeDtypeStruct((T, lru_w), F32)],
        compiler_params=_cparams(("arbitrary",)),
        name="inproj",
    )(x2, pos2, g_mix, w_in_b)


def _attn_kernel(sinks_ref, q_ref, k_ref, v_ref, kp_ref, vp_ref, g_ref, o_ref):
    n = pl.program_id(1)
    blk = WINDOW
    hd = HEAD_DIM
    nqb = q_ref.shape[0] // blk
    n_pairs = N_HEADS // 2
    pairs_per_group = n_pairs // N_KV_HEADS
    lane = lax.broadcasted_iota(I32, (1, LANES), 1)
    first = lane < hd
    qi = lax.broadcasted_iota(I32, (blk, 2 * blk), 0)
    kj = lax.broadcasted_iota(I32, (blk, 2 * blk), 1)
    rel = qi + blk - kj
    band = (rel >= 0) & (rel < WINDOW)
    bd_row = lax.broadcasted_iota(I32, (4 * blk, LANES), 0)
    bd_lane = lax.broadcasted_iota(I32, (4 * blk, LANES), 1)
    ones_bd = ((bd_row < 2 * blk) == (bd_lane < hd)).astype(BF16)
    zero = jnp.zeros((), BF16)

    def block_diag(t):
        same, swapped = t[:, :LANES], t[:, LANES:]
        out = []
        for g in range(N_KV_HEADS):
            top = jnp.where(first, same if g == 0 else swapped, zero)
            bot = jnp.where(first, zero, swapped if g == 0 else same)
            out.append(jnp.concatenate([top, bot], axis=0))
        return out

    for j in range(nqb):
        rows = slice(j * blk, (j + 1) * blk)
        if j == 0:
            k_prev, v_prev = kp_ref[...], vp_ref[...]
            mask = band & ((kj >= blk) | (n > 0))
        else:
            prev_rows = slice((j - 1) * blk, j * blk)
            k_prev, v_prev = k_ref[prev_rows, :], v_ref[prev_rows, :]
            mask = band
        k_bd = block_diag(jnp.concatenate([k_prev, k_ref[rows, :]], axis=0))
        v_bd = block_diag(jnp.concatenate([v_prev, v_ref[rows, :]], axis=0))
        outs = []
        for p in range(n_pairs):
            g = p // pairs_per_group
            qp = q_ref[rows, p * LANES:(p + 1) * LANES]
            s = lax.dot_general(qp, k_bd[g], (((1,), (1,)), ((), ())), preferred_element_type=F32)
            es, ms = [], []
            for a in range(2):
                sa = jnp.where(mask, s[:, a * 2 * blk:(a + 1) * 2 * blk], NEG_INF)
                m = jnp.maximum(jnp.max(sa, axis=-1, keepdims=True), sinks_ref[2 * p + a])
                es.append(jnp.exp(sa - m))
                ms.append(m)
            e = jnp.concatenate(es, axis=1).astype(BF16)
            od = jnp.dot(e, jnp.concatenate([v_bd[g], ones_bd], axis=1), preferred_element_type=F32)
            sink_term = jnp.where(first, jnp.exp(sinks_ref[2 * p] - ms[0]),
                                  jnp.exp(sinks_ref[2 * p + 1] - ms[1]))
            outs.append(od[:, :LANES] / (od[:, LANES:] + sink_term))
        attn = jnp.concatenate(outs, axis=1)
        o_ref[rows, :] = _rms(attn, g_ref[...]).astype(BF16)


def _attention(q, k, v, sinks, g_attn, B, S):
    T, attn_w = q.shape
    kv_w = k.shape[1]
    blk = WINDOW
    tq = TQ_ATTN
    nt = S // tq
    cur = lambda b, n: (b * nt + n, 0)
    prev = lambda b, n: (b * (S // blk) + jnp.maximum(n * (tq // blk) - 1, 0), 0)
    return pl.pallas_call(
        _attn_kernel,
        grid=(B, nt),
        in_specs=[pl.BlockSpec(memory_space=pltpu.SMEM),
                  pl.BlockSpec((tq, attn_w), cur),
                  pl.BlockSpec((tq, kv_w), cur), pl.BlockSpec((tq, kv_w), cur),
                  pl.BlockSpec((blk, kv_w), prev), pl.BlockSpec((blk, kv_w), prev),
                  pl.BlockSpec((1, attn_w), lambda b, n: (0, 0))],
        out_specs=pl.BlockSpec((tq, attn_w), cur),
        out_shape=jax.ShapeDtypeStruct((T, attn_w), BF16),
        compiler_params=_cparams(("arbitrary", "arbitrary")),
        name="attn",
    )(sinks, q, k, v, k, v, g_attn)


def _neg_expm1(x, exp_x):
    series = -x * (1.0 + x * (0.5 + x * (1.0 / 6.0 + x * (1.0 / 24.0))))
    return jnp.where(x > -0.02, series, 1.0 - exp_x)


def _gelu_tanh(x):
    return 0.5 * x * (1.0 + jnp.tanh(math.sqrt(2.0 / math.pi) * (x + 0.044715 * (x * x * x))))


def _lru_kernel(lx_ref, lg_ref, cw_ref, cb_ref, wa_ref, ba_ref, wx_ref, bx_ref, lam_ref, g_ref,
                o_ref, xpad, a_scr, b_scr, h_scr, h_carry):
    i = pl.program_id(0)
    B, tl, C = lx_ref.shape
    pad = SUBLANES

    @pl.when(i == 0)
    def _():
        xpad[:, 0:pad, :] = jnp.zeros((B, pad, C), F32)
        h_carry[...] = jnp.zeros_like(h_carry)

    @pl.when(i > 0)
    def _():
        xpad[:, 0:pad, :] = xpad[:, tl:tl + pad, :]

    xpad[:, pad:, :] = lx_ref[...]
    y = jnp.zeros((B, tl, C), F32) + cb_ref[...]
    for j in range(CONV_WIDTH):
        y = y + xpad[:, pl.ds(pad - (CONV_WIDTH - 1) + j, tl), :] * cw_ref[j:j + 1, :]
    y2 = y.reshape(B * tl, C)
    yb = y2.astype(BF16)
    gate_a = jnp.dot(yb, wa_ref[...], preferred_element_type=F32) + ba_ref[...]
    gate_x = jnp.dot(yb, wx_ref[...], preferred_element_type=F32) + bx_ref[...]
    r = jax.nn.sigmoid(gate_a)
    ig = jax.nn.sigmoid(gate_x)
    nl = -lam_ref[...]
    softplus = jnp.maximum(nl, 0.0) + jnp.log1p(jnp.exp(-jnp.abs(nl)))
    log_a = (-LRU_C) * r * softplus
    a_all = jnp.exp(log_a)
    one_minus_a2 = _neg_expm1(2.0 * log_a, a_all * a_all)
    b_all = one_minus_a2 * lax.rsqrt(jnp.maximum(one_minus_a2, 1e-30)) * ig * y2
    ncol = C // LANES
    for c in range(ncol):
        a_scr[c] = a_all[:, c * LANES:(c + 1) * LANES]
        b_scr[c] = b_all[:, c * LANES:(c + 1) * LANES]

    def step(t, hs):
        out = []
        for c in range(ncol):
            a = a_scr[c, pl.ds(t, B, stride=tl), :]
            b = b_scr[c, pl.ds(t, B, stride=tl), :]
            h = a * hs[c] + b
            h_scr[c, pl.ds(t, B, stride=tl), :] = h
            out.append(h)
        return tuple(out)

    hs = lax.fori_loop(0, tl, step, tuple(h_carry[c] for c in range(ncol)), unroll=8)
    for c in range(ncol):
        h_carry[c] = hs[c]
    h_all = jnp.concatenate([h_scr[c] for c in range(ncol)], axis=1)
    lru = h_all * _gelu_tanh(lg_ref[...].reshape(B * tl, C))
    o_ref[...] = _rms(lru, g_ref[...]).astype(BF16).reshape(B, tl, C)


def _lru(lx3, lg3, conv_w, conv_b, wa_bd, ba, wx_bd, bx, lam, g_lru):
    B, S, C = lx3.shape
    tl = TL_LRU
    blk = lambda i: (0, i, 0)
    const = lambda i: (0, 0)
    vec = pl.BlockSpec((1, C), const)
    return pl.pallas_call(
        _lru_kernel,
        grid=(S // tl,),
        in_specs=[pl.BlockSpec((B, tl, C), blk), pl.BlockSpec((B, tl, C), blk),
                  pl.BlockSpec((CONV_WIDTH, C), const), vec,
                  pl.BlockSpec((C, C), const), vec, pl.BlockSpec((C, C), const), vec, vec, vec],
        out_specs=pl.BlockSpec((B, tl, C), blk),
        out_shape=jax.ShapeDtypeStruct((B, S, C), BF16),
        scratch_shapes=[pltpu.VMEM((B, tl + SUBLANES, C), F32),
                        pltpu.VMEM((C // LANES, B * tl, LANES), F32),
                        pltpu.VMEM((C // LANES, B * tl, LANES), F32),
                        pltpu.VMEM((C // LANES, B * tl, LANES), F32),
                        pltpu.VMEM((C // LANES, B, LANES), F32)],
        compiler_params=_cparams(("arbitrary",)),
        name="lru",
    )(lx3, lg3, conv_w, conv_b, wa_bd, ba, wx_bd, bx, lam, g_lru)


def _outproj_kernel(x_ref, ma_ref, mb_ref, wo_ref, g_ref, wr_ref, br_ref,
                    x1_ref, h2_ref, eid_ref, wt_ref):
    half = ma_ref.shape[1]
    x1 = (x_ref[...]
          + jnp.dot(ma_ref[...], wo_ref[0:half, :], preferred_element_type=F32)
          + jnp.dot(mb_ref[...], wo_ref[half:, :], preferred_element_type=F32))
    x1_ref[...] = x1
    h2 = _rms(x1, g_ref[...])
    h2_ref[...] = h2.astype(BF16).reshape(h2_ref.shape)
    h_hi = h2.astype(BF16)
    h_lo = (h2 - h_hi.astype(F32)).astype(BF16)
    hw = jnp.dot(h_hi, wr_ref[...], preferred_element_type=F32)
    logits = (hw[:, :LANES] + hw[:, LANES:]
              + jnp.dot(h_lo, wr_ref[:, :LANES], preferred_element_type=F32) + br_ref[...])
    tm = logits.shape[0]
    lane = lax.broadcasted_iota(I32, (tm, LANES), 1)
    lane_f = lane.astype(F32)
    big = float(LANES)

    def first_argmax(vals):
        m = jnp.max(vals, axis=-1, keepdims=True)
        idx = jnp.min(jnp.where(vals == m, lane_f, big), axis=-1, keepdims=True)
        return m, idx.astype(I32)

    gl = jnp.where(lane < N_GROUPS, logits, NEG_INF)
    gmax, gidx = first_argmax(gl)
    gsum = jnp.sum(jnp.where(lane < N_GROUPS, jnp.exp(gl - gmax), 0.0), axis=-1, keepdims=True)
    g_top_p = 1.0 / gsum
    lo = N_GROUPS + EXPERTS_PER_GROUP * gidx
    el = jnp.where((lane >= lo) & (lane < lo + EXPERTS_PER_GROUP), logits, NEG_INF)
    m1, i1 = first_argmax(el)
    el2 = jnp.where(lane == i1, NEG_INF, el)
    m2, i2 = first_argmax(el2)
    ratio = jnp.exp(m2 - m1)
    w1 = g_top_p / (1.0 + ratio)
    w2 = g_top_p * ratio / (1.0 + ratio)
    eid_ref[...] = jnp.where(lane == 0, i1 - N_GROUPS, jnp.where(lane == 1, i2 - N_GROUPS, 0))
    wt_ref[...] = jnp.where(lane == 0, w1, jnp.where(lane == 1, w2, 0.0))


def _outproj(x2, mix_a, mix_b, w_out_b, g_ffn, w_router, b_router):
    T, D = x2.shape
    tm = TM_PROJ
    half = mix_a.shape[1]
    row = lambda i: (i, 0)
    const = lambda i: (0, 0)
    return pl.pallas_call(
        _outproj_kernel,
        grid=(T // tm,),
        in_specs=[pl.BlockSpec((tm, D), row), pl.BlockSpec((tm, half), row),
                  pl.BlockSpec((tm, half), row), pl.BlockSpec(w_out_b.shape, const),
                  pl.BlockSpec((1, D), const), pl.BlockSpec((D, 2 * LANES), const),
                  pl.BlockSpec((1, LANES), const)],
        out_specs=[pl.BlockSpec((tm, D), row),
                   pl.BlockSpec((tm, D // LANES, LANES), lambda i: (i, 0, 0)),
                   pl.BlockSpec((tm, LANES), row), pl.BlockSpec((tm, LANES), row)],
        out_shape=[jax.ShapeDtypeStruct((T, D), F32),
                   jax.ShapeDtypeStruct((T, D // LANES, LANES), BF16),
                   jax.ShapeDtypeStruct((T, LANES), I32), jax.ShapeDtypeStruct((T, LANES), F32)],
        compiler_params=_cparams(("arbitrary",)),
        name="outproj",
    )(x2, mix_a, mix_b, w_out_b, g_ffn, w_router, b_router)


def _lane_cumsum(x):
    lane = lax.broadcasted_iota(I32, x.shape, 1)
    shift = 1
    while shift < LANES:
        x = x + jnp.where(lane >= shift, pltpu.roll(x, shift, 1), 0)
        shift *= 2
    return x


def _plan_kernel(eid_ref, pos_ref, te_ref, nv_ref, meta_ref, rank_scr, cnt_scr, start_scr):
    phase = pl.program_id(0)
    i = pl.program_id(1)
    tm = eid_ref.shape[0]
    lane = lax.broadcasted_iota(I32, (tm, LANES), 1)
    eid = eid_ref[...]
    f1 = eid[:, 0:1]
    f2 = eid[:, 1:2]
    sel1 = lane == f1
    sel2 = lane == f2

    @pl.when((phase == 0) & (i == 0))
    def _():
        cnt_scr[...] = jnp.zeros_like(cnt_scr)

    @pl.when(phase == 0)
    def _():
        onehot = (sel1 | sel2).astype(BF16)
        r = lax.broadcasted_iota(I32, (tm, tm), 0)
        c = lax.broadcasted_iota(I32, (tm, tm), 1)
        strict_lower = (r > c).astype(BF16)
        before = jnp.dot(strict_lower, onehot, preferred_element_type=F32) + cnt_scr[0:1, :]
        r1 = jnp.sum(jnp.where(sel1, before, 0.0), axis=-1, keepdims=True)
        r2 = jnp.sum(jnp.where(sel2, before, 0.0), axis=-1, keepdims=True)
        rank = jnp.where(lane == 0, r1, jnp.where(lane == 1, r2, 0.0)).astype(I32)
        rank_scr[pl.ds(pl.multiple_of(i * tm, tm), tm), :] = rank
        cnt_scr[...] = cnt_scr[...] + jnp.sum(onehot.astype(F32), axis=0, keepdims=True)

    @pl.when(phase == 1)
    def _():
        starts = start_scr[0:1, :]
        o1 = jnp.sum(jnp.where(sel1, starts, 0.0), axis=-1, keepdims=True).astype(I32)
        o2 = jnp.sum(jnp.where(sel2, starts, 0.0), axis=-1, keepdims=True).astype(I32)
        rank = rank_scr[pl.ds(pl.multiple_of(i * tm, tm), tm), :]
        pos_ref[...] = rank + jnp.where(lane == 0, o1, jnp.where(lane == 1, o2, 0))

    @pl.when((phase == 0) & (i == pl.num_programs(1) - 1))
    def _():
        shift = int(math.log2(TM_EXPERT))
        cnt = cnt_scr[...].astype(I32)
        padded = ((cnt + (TM_EXPERT - 1)) >> shift) << shift
        ends = _lane_cumsum(padded)
        start_scr[...] = (ends - padded).astype(F32)
        nt = te_ref.shape[0]
        end_tile = (ends >> shift)[0:1, :]
        tl_lane = lax.broadcasted_iota(I32, (nt, LANES), 1)
        tile = lax.broadcasted_iota(I32, (nt, LANES), 0)
        done = jnp.where((tl_lane < N_EXPERTS) & (end_tile <= tile), 1.0, 0.0)
        te = jnp.sum(done, axis=-1, keepdims=True).astype(I32)
        te_ref[...] = jnp.broadcast_to(jnp.minimum(te, N_EXPERTS - 1), (nt, LANES))
        n_valid = jnp.sum(jnp.where(tl_lane[0:1, :] == N_EXPERTS - 1, end_tile, 0).astype(F32),
                          axis=-1, keepdims=True).astype(I32)
        nv_ref[...] = jnp.broadcast_to(n_valid, nv_ref.shape)
        row = lax.broadcasted_iota(I32, cnt.shape, 0)
        meta_ref[...] = jnp.where(row == 0, ends - padded + cnt, jnp.where(row == 1, padded - cnt, 0))


def _plan(eid, n_tiles):
    T = eid.shape[0]
    tm = TM_PLAN
    const = lambda p, i: (0, 0)
    return pl.pallas_call(
        _plan_kernel,
        grid=(2, T // tm),
        in_specs=[pl.BlockSpec((tm, LANES), lambda p, i: (i, 0))],
        out_specs=[pl.BlockSpec((tm, LANES), lambda p, i: (i * p, 0)),
                   pl.BlockSpec((n_tiles, LANES), const), pl.BlockSpec((SUBLANES, LANES), const),
                   pl.BlockSpec((SUBLANES, LANES), const)],
        out_shape=[jax.ShapeDtypeStruct((T, LANES), I32), jax.ShapeDtypeStruct((n_tiles, LANES), I32),
                   jax.ShapeDtypeStruct((SUBLANES, LANES), I32),
                   jax.ShapeDtypeStruct((SUBLANES, LANES), I32)],
        scratch_shapes=[pltpu.VMEM((T, LANES), I32), pltpu.VMEM((SUBLANES, LANES), F32),
                        pltpu.VMEM((SUBLANES, LANES), F32)],
        compiler_params=_cparams(("arbitrary", "arbitrary")),
        name="plan",
    )(eid)


_PAD_BITS = tuple(1 << b for b in reversed(range(int(math.log2(TM_EXPERT)))))


def _dispatch_kernel(pos_ref, seg_end_ref, seg_pad_ref, nv_ref, h_ref, xs_ref, zeros, sem, zsem):
    i = pl.program_id(0)
    tm = h_ref.shape[0]

    def issue(t, c):
        for k in range(2):
            pltpu.make_async_copy(h_ref.at[pl.ds(t, 1)], xs_ref.at[pl.ds(pos_ref[2 * t + k], 1)],
                                  sem).start()
        return c

    lax.fori_loop(0, tm, issue, 0, unroll=8)
    for k in range(2):
        pltpu.make_async_copy(h_ref, xs_ref.at[pl.ds(0, tm)], sem).wait()

    @pl.when(i == pl.num_programs(0) - 1)
    def _():
        zeros[...] = jnp.zeros_like(zeros)
        n_tiles = xs_ref.shape[0] // TM_EXPERT
        first_tail = n_tiles - N_EXPERTS

        def pad_copy(e, bit):
            n = seg_pad_ref[e]
            dst = seg_end_ref[e] + (n & ~(2 * bit - 1))
            return (n & bit) != 0, pltpu.make_async_copy(zeros.at[pl.ds(0, bit)],
                                                         xs_ref.at[pl.ds(dst, bit)], zsem)

        def tail_copy(j):
            return j >= nv_ref[0], pltpu.make_async_copy(
                zeros, xs_ref.at[pl.ds(j * TM_EXPERT, TM_EXPERT)], zsem)

        def for_each_fill(act):
            def per_expert(e, c):
                for bit in _PAD_BITS:
                    pred, cp = pad_copy(e, bit)
                    pl.when(pred)(functools.partial(act, cp))
                return c

            lax.fori_loop(0, N_EXPERTS, per_expert, 0)

            def per_tile(j, c):
                pred, cp = tail_copy(j)
                pl.when(pred)(functools.partial(act, cp))
                return c

            lax.fori_loop(first_tail, n_tiles, per_tile, 0)

        for_each_fill(lambda cp: cp.start())
        for_each_fill(lambda cp: cp.wait())


def _dispatch(pos_flat, seg_end, seg_pad, n_valid, h2, n_rows):
    T = h2.shape[0]
    slab = h2.shape[1:]
    tm = TM_DISPATCH
    smem = pl.BlockSpec(memory_space=pltpu.SMEM)
    return pl.pallas_call(
        _dispatch_kernel,
        grid=(T // tm,),
        in_specs=[pl.BlockSpec((2 * tm,), lambda i: (i,), memory_space=pltpu.SMEM),
                  smem, smem, smem, pl.BlockSpec((tm,) + slab, lambda i: (i, 0, 0))],
        out_specs=pl.BlockSpec(memory_space=pl.ANY),
        out_shape=jax.ShapeDtypeStruct((n_rows,) + slab, h2.dtype),
        scratch_shapes=[pltpu.VMEM((TM_EXPERT,) + slab, h2.dtype), pltpu.SemaphoreType.DMA(()),
                        pltpu.SemaphoreType.DMA(())],
        compiler_params=_cparams(("arbitrary",)),
        name="dispatch",
    )(pos_flat, seg_end, seg_pad, n_valid, h2)


def _experts_kernel(te_ref, nv_ref, xs_ref, wg_ref, wu_ref, wd_ref, ys_ref, wg_b, wu_b, wd_b):
    j = pl.program_id(0)
    new_expert = (j == 0) | (te_ref[j] != te_ref[jnp.maximum(j - 1, 0)])

    @pl.when(new_expert)
    def _():
        wg_b[...] = wg_ref[0].astype(BF16)
        wu_b[...] = wu_ref[0].astype(BF16)
        wd_b[...] = wd_ref[0].astype(BF16)

    @pl.when(j < nv_ref[0])
    def _():
        x = xs_ref[...].reshape(xs_ref.shape[0], -1)
        a = jnp.dot(x, wg_b[...], preferred_element_type=F32)
        u = jnp.dot(x, wu_b[...], preferred_element_type=F32)
        hid = (a * jax.nn.sigmoid(a) * u).astype(BF16)
        y = jnp.dot(hid, wd_b[...], preferred_element_type=F32)
        ys_ref[...] = y.astype(BF16).reshape(ys_ref.shape)

    @pl.when(j >= nv_ref[0])
    def _():
        ys_ref[...] = jnp.zeros_like(ys_ref)


def _experts(tile_expert, n_valid, xs, wg, wu, wd):
    R = xs.shape[0]
    slab = xs.shape[1:]
    E, D, Fh = wg.shape
    tm = TM_EXPERT
    n_tiles = R // tm
    rows = lambda j, te, nv: (j, 0, 0)
    wsel = lambda j, te, nv: (te[j], 0, 0)
    return pl.pallas_call(
        _experts_kernel,
        grid_spec=pltpu.PrefetchScalarGridSpec(
            num_scalar_prefetch=2,
            grid=(n_tiles,),
            in_specs=[pl.BlockSpec((tm,) + slab, rows),
                      pl.BlockSpec((1, D, Fh), wsel), pl.BlockSpec((1, D, Fh), wsel),
                      pl.BlockSpec((1, Fh, D), wsel)],
            out_specs=pl.BlockSpec((tm,) + slab, rows),
            scratch_shapes=[pltpu.VMEM((D, Fh), BF16), pltpu.VMEM((D, Fh), BF16),
                            pltpu.VMEM((Fh, D), BF16)]),
        out_shape=jax.ShapeDtypeStruct(xs.shape, xs.dtype),
        compiler_params=_cparams(("arbitrary",)),
        name="experts",
    )(tile_expert, n_valid, xs, wg, wu, wd)


def _combine_kernel(pos_ref, pos_next_ref, x1_ref, wt_ref, p_ref, gp_ref, wpg_ref, wpp_ref, gf_ref,
                    ys_ref, o_ref, buf_a, buf_b, sem):
    i = pl.program_id(0)
    n = pl.num_programs(0)
    tm = x1_ref.shape[0]

    def gather(idx_ref, buf, s, unroll):
        def issue(t, c):
            for k in range(2):
                pltpu.make_async_copy(ys_ref.at[pl.ds(idx_ref[2 * t + k], 1)],
                                      buf.at[k, pl.ds(t, 1)], sem.at[s]).start()
            return c

        lax.fori_loop(0, tm, issue, 0, unroll=unroll)

    def wait_buf(buf, s):
        for k in range(2):
            pltpu.make_async_copy(ys_ref.at[pl.ds(0, tm)], buf.at[k], sem.at[s]).wait()

    @pl.when(i == 0)
    def _():
        gather(pos_ref, buf_a, 0, 8)

    def step(cur, cur_s, nxt, nxt_s):
        wait_buf(cur, cur_s)
        gather(pos_next_ref, nxt, nxt_s, True)
        wt = wt_ref[...]
        y0 = cur[0].reshape(tm, -1).astype(F32)
        y1 = cur[1].reshape(tm, -1).astype(F32)
        x2 = x1_ref[...] + wt[:, 0:1] * y0 + wt[:, 1:2] * y1
        hp = _rms(x2, gp_ref[...]).astype(BF16)
        gate = jax.nn.sigmoid(jnp.dot(hp, wpg_ref[...], preferred_element_type=F32))
        proj = jnp.dot(p_ref[...].astype(BF16), wpp_ref[...], preferred_element_type=F32)
        x3 = x2 + gate * proj
        o_ref[...] = _rms(x3, gf_ref[...])

        @pl.when(i == n - 1)
        def _():
            wait_buf(nxt, nxt_s)

    pl.when(i % 2 == 0)(functools.partial(step, buf_a, 0, buf_b, 1))
    pl.when(i % 2 == 1)(functools.partial(step, buf_b, 1, buf_a, 0))


def _combine(pos_flat, x1, wts, p2, g_ple, w_ple_gate_b, w_ple_proj_b, g_final, ys):
    T, D = x1.shape
    P = p2.shape[1]
    tm = TM_COMBINE
    n = T // tm
    row = lambda i: (i, 0)
    const = lambda i: (0, 0)
    return pl.pallas_call(
        _combine_kernel,
        grid=(n,),
        in_specs=[pl.BlockSpec((2 * tm,), lambda i: (i,), memory_space=pltpu.SMEM),
                  pl.BlockSpec((2 * tm,), lambda i: (jnp.minimum(i + 1, n - 1),),
                               memory_space=pltpu.SMEM),
                  pl.BlockSpec((tm, D), row), pl.BlockSpec((tm, LANES), row),
                  pl.BlockSpec((tm, P), row), pl.BlockSpec((1, D), const),
                  pl.BlockSpec((D, D), const), pl.BlockSpec((P, D), const),
                  pl.BlockSpec((1, D), const), pl.BlockSpec(memory_space=pl.ANY)],
        out_specs=pl.BlockSpec((tm, D), row),
        out_shape=jax.ShapeDtypeStruct((T, D), F32),
        scratch_shapes=[pltpu.VMEM((2, tm) + ys.shape[1:], ys.dtype),
                        pltpu.VMEM((2, tm) + ys.shape[1:], ys.dtype),
                        pltpu.SemaphoreType.DMA((2,))],
        compiler_params=_cparams(("arbitrary",)),
        name="combine",
    )(pos_flat, pos_flat, x1, wts, p2, g_ple, w_ple_gate_b, w_ple_proj_b, g_final, ys)


def _block_diag(w):
    nb, d, _ = w.shape
    eye = jnp.eye(nb, dtype=w.dtype)
    return (eye[:, None, :, None] * w[:, :, None, :]).reshape(nb * d, nb * d)


def kernel(x, p, positions, g_mix, w_in, sinks, conv_w, conv_b, lru_wa, lru_ba, lru_wx, lru_bx,
           lru_lambda, g_attn_out, g_lru_out, w_out, g_ffn, w_router_group, b_router_group,
           w_router_expert, b_router_expert, w_expert_gate, w_expert_up, w_expert_down, g_ple,
           w_ple_gate, w_ple_proj, g_final):
    B, S, D = x.shape
    assert w_in.shape[0] == 1, "single-layer block only"
    T = B * S
    lru_w = conv_w.shape[-1]
    n_rows = 2 * T + N_EXPERTS * TM_EXPERT
    n_tiles = n_rows // TM_EXPERT
    pos2 = positions.reshape(T, 1).astype(I32)
    x2 = x.reshape(T, D)
    q, k, v, lx, lg = _inproj(x2, pos2, g_mix[0][None], w_in[0].astype(BF16), lru_w)
    mix_a = _attention(q, k, v, sinks[0], g_attn_out[0][None], B, S)
    mix_b = _lru(lx.reshape(B, S, lru_w), lg.reshape(B, S, lru_w), conv_w[0], conv_b[0][None],
                 _block_diag(lru_wa[0]).astype(BF16), lru_ba[0][None],
                 _block_diag(lru_wx[0]).astype(BF16), lru_bx[0][None],
                 lru_lambda[0][None], g_lru_out[0][None]).reshape(T, lru_w)
    n_router = N_GROUPS + N_EXPERTS
    w_router = jnp.pad(jnp.concatenate([w_router_group[0], w_router_expert[0]], axis=1),
                       ((0, 0), (0, LANES - n_router)))
    b_router = jnp.pad(jnp.concatenate([b_router_group[0], b_router_expert[0]]),
                       (0, LANES - n_router))[None]
    w_router_hi = w_router.astype(BF16)
    w_router_lo = (w_router - w_router_hi.astype(F32)).astype(BF16)
    x1, h2, eid, wts = _outproj(x2, mix_a, mix_b, w_out[0].astype(BF16), g_ffn[0][None],
                                jnp.concatenate([w_router_hi, w_router_lo], axis=1), b_router)
    pos, tile_expert, n_valid, meta = _plan(eid, n_tiles)
    pos_flat = pos[:, :2].reshape(2 * T)
    n_valid = n_valid[0, :1]
    xs = _dispatch(pos_flat, meta[0], meta[1], n_valid, h2, n_rows)
    Fh = w_expert_gate.shape[-1]
    ys = _experts(tile_expert[:, 0], n_valid, xs,
                  w_expert_gate[0].reshape(N_EXPERTS, D, Fh),
                  w_expert_up[0].reshape(N_EXPERTS, D, Fh),
                  w_expert_down[0].reshape(N_EXPERTS, Fh, D))
    out = _combine(pos_flat, x1, wts, p[0].reshape(T, -1), g_ple[0][None],
                   w_ple_gate[0].astype(BF16), w_ple_proj[0].astype(BF16), g_final[None], ys)
    return out.reshape(B, S, D)
```

```python
import functools
import math

import jax
import jax.numpy as jnp
from jax import lax
from jax.experimental import pallas as pl
from jax.experimental.pallas import tpu as pltpu

F32 = jnp.float32
BF16 = jnp.bfloat16
I32 = jnp.int32

EPS = 1e-6
N_HEADS = 8
N_KV_HEADS = 2
HEAD_DIM = 64
WINDOW = 128
ROPE_THETA = 10000.0
CONV_WIDTH = 4
LRU_C = 8.0
N_GROUPS = 4
EXPERTS_PER_GROUP = 8
N_EXPERTS = N_GROUPS * EXPERTS_PER_GROUP
NEG_INF = -1e30
LANES = 128
SUBLANES = 8

TM_INPROJ = 512
TM_PROJ = 512
TQ_ATTN = 512
TL_LRU = 256
TM_PLAN = 1024
TM_EXPERT = 512
EXPERT_FF_BLOCK = 256
TM_DISPATCH = 512
TM_COMBINE = 256
VMEM_LIMIT = 56 * 1024 * 1024


def _rms(x, g):
    ms = jnp.mean(x * x, axis=-1, keepdims=True)
    return x * lax.rsqrt(ms + EPS) * g


def _cparams(sem):
    return pltpu.CompilerParams(dimension_semantics=sem, vmem_limit_bytes=VMEM_LIMIT)


def _inproj_kernel(x_ref, pos_ref, g_ref, w_ref, q_ref, k_ref, v_ref, lx_ref, lg_ref):
    h = _rms(x_ref[...], g_ref[...]).astype(BF16)
    proj = jnp.dot(h, w_ref[...], preferred_element_type=F32)
    lane = lax.broadcasted_iota(I32, (1, LANES), 1)
    fidx = (lane % (HEAD_DIM // 2)).astype(F32)
    inv_freq = jnp.exp(fidx * (-2.0 / HEAD_DIM * math.log(ROPE_THETA)))
    ang = pos_ref[...].astype(F32) * inv_freq
    cos = jnp.cos(ang)
    sin = jnp.sin(ang)
    first_half = (lane % HEAD_DIM) < (HEAD_DIM // 2)
    sin_signed = jnp.where(first_half, -sin, sin)

    def rope(t):
        partner = jnp.where(first_half, pltpu.roll(t, LANES - HEAD_DIM // 2, 1),
                            pltpu.roll(t, HEAD_DIM // 2, 1))
        return t * cos + partner * sin_signed

    attn_w = N_HEADS * HEAD_DIM
    scale = HEAD_DIM ** -0.5
    assert math.frexp(scale)[0] == 0.5, "score scale must be a power of two to fold into q exactly"
    for c in range(attn_w // LANES):
        q_ref[:, c * LANES:(c + 1) * LANES] = (
            rope(proj[:, c * LANES:(c + 1) * LANES]) * scale).astype(BF16)
    kv_w = N_KV_HEADS * HEAD_DIM
    assert kv_w == LANES
    k = rope(proj[:, attn_w:attn_w + kv_w])
    v = proj[:, attn_w + kv_w:attn_w + 2 * kv_w]
    k_ref[:, :kv_w] = k.astype(BF16)
    k_ref[:, kv_w:] = pltpu.roll(k, HEAD_DIM, 1).astype(BF16)
    v_ref[:, :kv_w] = v.astype(BF16)
    v_ref[:, kv_w:] = pltpu.roll(v, HEAD_DIM, 1).astype(BF16)
    off_lx = attn_w + 2 * kv_w
    lru_w = lx_ref.shape[1]
    lx_ref[...] = proj[:, off_lx:off_lx + lru_w]
    lg_ref[...] = proj[:, off_lx + lru_w:off_lx + 2 * lru_w]


def _inproj(x2, pos2, g_mix, w_in_b, lru_w):
    T, D = x2.shape
    tm = TM_INPROJ
    attn_w = N_HEADS * HEAD_DIM
    kv_w = N_KV_HEADS * HEAD_DIM
    row = lambda i: (i, 0)
    const = lambda i: (0, 0)
    return pl.pallas_call(
        _inproj_kernel,
        grid=(T // tm,),
        in_specs=[pl.BlockSpec((tm, D), row), pl.BlockSpec((tm, 1), row),
                  pl.BlockSpec((1, D), const), pl.BlockSpec(w_in_b.shape, const)],
        out_specs=[pl.BlockSpec((tm, attn_w), row), pl.BlockSpec((tm, 2 * kv_w), row),
                   pl.BlockSpec((tm, 2 * kv_w), row), pl.BlockSpec((tm, lru_w), row),
                   pl.BlockSpec((tm, lru_w), row)],
        out_shape=[jax.ShapeDtypeStruct((T, attn_w), BF16),
                   jax.ShapeDtypeStruct((T, 2 * kv_w), BF16),
                   jax.ShapeDtypeStruct((T, 2 * kv_w), BF16), jax.ShapeDtypeStruct((T, lru_w), F32),
                   jax.ShapeDtypeStruct((T, lru_w), F32)],
        compiler_params=_cparams(("arbitrary",)),
        name="inproj",
    )(x2, pos2, g_mix, w_in_b)


def _attn_kernel(sinks_ref, q_ref, k_ref, v_ref, kp_ref, vp_ref, g_ref, o_ref):
    n = pl.program_id(1)
    blk = WINDOW
    hd = HEAD_DIM
    nqb = q_ref.shape[0] // blk
    n_pairs = N_HEADS // 2
    pairs_per_group = n_pairs // N_KV_HEADS
    lane = lax.broadcasted_iota(I32, (1, LANES), 1)
    first = lane < hd
    qi = lax.broadcasted_iota(I32, (blk, 2 * blk), 0)
    kj = lax.broadcasted_iota(I32, (blk, 2 * blk), 1)
    rel = qi + blk - kj
    band = (rel >= 0) & (rel < WINDOW)
    bd_row = lax.broadcasted_iota(I32, (4 * blk, LANES), 0)
    bd_lane = lax.broadcasted_iota(I32, (4 * blk, LANES), 1)
    ones_bd = ((bd_row < 2 * blk) == (bd_lane < hd)).astype(BF16)
    zero = jnp.zeros((), BF16)

    def block_diag(t):
        same, swapped = t[:, :LANES], t[:, LANES:]
        out = []
        for g in range(N_KV_HEADS):
            top = jnp.where(first, same if g == 0 else swapped, zero)
            bot = jnp.where(first, zero, swapped if g == 0 else same)
            out.append(jnp.concatenate([top, bot], axis=0))
        return out

    for j in range(nqb):
        rows = slice(j * blk, (j + 1) * blk)
        if j == 0:
            k_prev, v_prev = kp_ref[...], vp_ref[...]
            mask = band & ((kj >= blk) | (n > 0))
        else:
            prev_rows = slice((j - 1) * blk, j * blk)
            k_prev, v_prev = k_ref[prev_rows, :], v_ref[prev_rows, :]
            mask = band
        k_bd = block_diag(jnp.concatenate([k_prev, k_ref[rows, :]], axis=0))
        v_bd = block_diag(jnp.concatenate([v_prev, v_ref[rows, :]], axis=0))
        outs = []
        for p in range(n_pairs):
            g = p // pairs_per_group
            qp = q_ref[rows, p * LANES:(p + 1) * LANES]
            s = lax.dot_general(qp, k_bd[g], (((1,), (1,)), ((), ())), preferred_element_type=F32)
            es, ms = [], []
            for a in range(2):
                sa = jnp.where(mask, s[:, a * 2 * blk:(a + 1) * 2 * blk], NEG_INF)
                m = jnp.maximum(jnp.max(sa, axis=-1, keepdims=True), sinks_ref[2 * p + a])
                es.append(jnp.exp(sa - m))
                ms.append(m)
            e = jnp.concatenate(es, axis=1).astype(BF16)
            od = jnp.dot(e, jnp.concatenate([v_bd[g], ones_bd], axis=1), preferred_element_type=F32)
            sink_term = jnp.where(first, jnp.exp(sinks_ref[2 * p] - ms[0]),
                                  jnp.exp(sinks_ref[2 * p + 1] - ms[1]))
            outs.append(od[:, :LANES] / (od[:, LANES:] + sink_term))
        attn = jnp.concatenate(outs, axis=1)
        o_ref[rows, :] = _rms(attn, g_ref[...]).astype(BF16)


def _attention(q, k, v, sinks, g_attn, B, S):
    T, attn_w = q.shape
    kv_w = k.shape[1]
    blk = WINDOW
    tq = TQ_ATTN
    nt = S // tq
    cur = lambda b, n: (b * nt + n, 0)
    prev = lambda b, n: (b * (S // blk) + jnp.maximum(n * (tq // blk) - 1, 0), 0)
    return pl.pallas_call(
        _attn_kernel,
        grid=(B, nt),
        in_specs=[pl.BlockSpec(memory_space=pltpu.SMEM),
                  pl.BlockSpec((tq, attn_w), cur),
                  pl.BlockSpec((tq, kv_w), cur), pl.BlockSpec((tq, kv_w), cur),
                  pl.BlockSpec((blk, kv_w), prev), pl.BlockSpec((blk, kv_w), prev),
                  pl.BlockSpec((1, attn_w), lambda b, n: (0, 0))],
        out_specs=pl.BlockSpec((tq, attn_w), cur),
        out_shape=jax.ShapeDtypeStruct((T, attn_w), BF16),
        compiler_params=_cparams(("arbitrary", "arbitrary")),
        name="attn",
    )(sinks, q, k, v, k, v, g_attn)


def _neg_expm1(x, exp_x):
    series = -x * (1.0 + x * (0.5 + x * (1.0 / 6.0 + x * (1.0 / 24.0))))
    return jnp.where(x > -0.02, series, 1.0 - exp_x)


def _gelu_tanh(x):
    return 0.5 * x * (1.0 + jnp.tanh(math.sqrt(2.0 / math.pi) * (x + 0.044715 * (x * x * x))))


def _lru_kernel(lx_ref, lg_ref, cw_ref, cb_ref, wa_ref, ba_ref, wx_ref, bx_ref, lam_ref, g_ref,
                o_ref, xpad, a_scr, b_scr, h_scr, h_carry):
    i = pl.program_id(0)
    B, tl, C = lx_ref.shape
    pad = SUBLANES

    @pl.when(i == 0)
    def _():
        xpad[:, 0:pad, :] = jnp.zeros((B, pad, C), F32)
        h_carry[...] = jnp.zeros_like(h_carry)

    @pl.when(i > 0)
    def _():
        xpad[:, 0:pad, :] = xpad[:, tl:tl + pad, :]

    xpad[:, pad:, :] = lx_ref[...]
    y = jnp.zeros((B, tl, C), F32) + cb_ref[...]
    for j in range(CONV_WIDTH):
        y = y + xpad[:, pl.ds(pad - (CONV_WIDTH - 1) + j, tl), :] * cw_ref[j:j + 1, :]
    y2 = y.reshape(B * tl, C)
    yb = y2.astype(BF16)
    gate_a = jnp.dot(yb, wa_ref[...], preferred_element_type=F32) + ba_ref[...]
    gate_x = jnp.dot(yb, wx_ref[...], preferred_element_type=F32) + bx_ref[...]
    r = jax.nn.sigmoid(gate_a)
    ig = jax.nn.sigmoid(gate_x)
    nl = -lam_ref[...]
    softplus = jnp.maximum(nl, 0.0) + jnp.log1p(jnp.exp(-jnp.abs(nl)))
    log_a = (-LRU_C) * r * softplus
    a_all = jnp.exp(log_a)
    one_minus_a2 = _neg_expm1(2.0 * log_a, a_all * a_all)
    b_all = one_minus_a2 * lax.rsqrt(jnp.maximum(one_minus_a2, 1e-30)) * ig * y2
    groups = (B * tl // SUBLANES, SUBLANES, C)
    sub = lax.broadcasted_iota(I32, groups, 1)
    a_cum, b_loc = a_all.reshape(groups), b_all.reshape(groups)
    d = 1
    while d < SUBLANES:
        keep = sub >= d
        a_prev = pltpu.roll(a_cum, d, 1)
        b_prev = pltpu.roll(b_loc, d, 1)
        b_loc = b_loc + jnp.where(keep, a_cum * b_prev, 0.0)
        a_cum = jnp.where(keep, a_cum * a_prev, a_cum)
        d *= 2
    a_scr[...] = a_cum.reshape(B, tl, C)
    b_scr[...] = b_loc.reshape(B, tl, C)

    def carry(g, h_prev):
        rows = pl.ds(pl.multiple_of(g * SUBLANES, SUBLANES), SUBLANES)
        h = b_scr[:, rows, :] + a_scr[:, rows, :] * h_prev
        h_scr[:, rows, :] = h
        return h[:, SUBLANES - 1:SUBLANES, :]

    h_carry[...] = lax.fori_loop(0, tl // SUBLANES, carry, h_carry[...], unroll=4)
    lru = h_scr[...].reshape(B * tl, C) * _gelu_tanh(lg_ref[...].reshape(B * tl, C))
    o_ref[...] = _rms(lru, g_ref[...]).astype(BF16).reshape(B, tl, C)


def _lru(lx3, lg3, conv_w, conv_b, wa_bd, ba, wx_bd, bx, lam, g_lru):
    B, S, C = lx3.shape
    tl = TL_LRU
    blk = lambda i: (0, i, 0)
    const = lambda i: (0, 0)
    vec = pl.BlockSpec((1, C), const)
    return pl.pallas_call(
        _lru_kernel,
        grid=(S // tl,),
        in_specs=[pl.BlockSpec((B, tl, C), blk), pl.BlockSpec((B, tl, C), blk),
                  pl.BlockSpec((CONV_WIDTH, C), const), vec,
                  pl.BlockSpec((C, C), const), vec, pl.BlockSpec((C, C), const), vec, vec, vec],
        out_specs=pl.BlockSpec((B, tl, C), blk),
        out_shape=jax.ShapeDtypeStruct((B, S, C), BF16),
        scratch_shapes=[pltpu.VMEM((B, tl + SUBLANES, C), F32),
                        pltpu.VMEM((B, tl, C), F32), pltpu.VMEM((B, tl, C), F32),
                        pltpu.VMEM((B, tl, C), F32), pltpu.VMEM((B, 1, C), F32)],
        compiler_params=_cparams(("arbitrary",)),
        name="lru",
    )(lx3, lg3, conv_w, conv_b, wa_bd, ba, wx_bd, bx, lam, g_lru)


def _outproj_kernel(x_ref, ma_ref, mb_ref, wo_ref, g_ref, wr_ref, br_ref,
                    x1_ref, h2_ref, eid_ref, wt_ref):
    half = ma_ref.shape[1]
    x1 = (x_ref[...]
          + jnp.dot(ma_ref[...], wo_ref[0:half, :], preferred_element_type=F32)
          + jnp.dot(mb_ref[...], wo_ref[half:, :], preferred_element_type=F32))
    x1_ref[...] = x1
    h2 = _rms(x1, g_ref[...])
    h2_ref[...] = h2.astype(BF16).reshape(h2_ref.shape)
    h_hi = h2.astype(BF16)
    h_lo = (h2 - h_hi.astype(F32)).astype(BF16)
    hw = jnp.dot(h_hi, wr_ref[...], preferred_element_type=F32)
    logits = (hw[:, :LANES] + hw[:, LANES:]
              + jnp.dot(h_lo, wr_ref[:, :LANES], preferred_element_type=F32) + br_ref[...])
    tm = logits.shape[0]
    lane = lax.broadcasted_iota(I32, (tm, LANES), 1)
    lane_f = lane.astype(F32)
    big = float(LANES)

    def first_argmax(vals):
        m = jnp.max(vals, axis=-1, keepdims=True)
        idx = jnp.min(jnp.where(vals == m, lane_f, big), axis=-1, keepdims=True)
        return m, idx.astype(I32)

    gl = jnp.where(lane < N_GROUPS, logits, NEG_INF)
    gmax, gidx = first_argmax(gl)
    gsum = jnp.sum(jnp.where(lane < N_GROUPS, jnp.exp(gl - gmax), 0.0), axis=-1, keepdims=True)
    g_top_p = 1.0 / gsum
    lo = N_GROUPS + EXPERTS_PER_GROUP * gidx
    el = jnp.where((lane >= lo) & (lane < lo + EXPERTS_PER_GROUP), logits, NEG_INF)
    m1, i1 = first_argmax(el)
    el2 = jnp.where(lane == i1, NEG_INF, el)
    m2, i2 = first_argmax(el2)
    ratio = jnp.exp(m2 - m1)
    w1 = g_top_p / (1.0 + ratio)
    w2 = g_top_p * ratio / (1.0 + ratio)
    eid_ref[...] = jnp.where(lane == 0, i1 - N_GROUPS, jnp.where(lane == 1, i2 - N_GROUPS, 0))
    wt_ref[...] = jnp.where(lane == 0, w1, jnp.where(lane == 1, w2, 0.0))


def _outproj(x2, mix_a, mix_b, w_out_b, g_ffn, w_router, b_router):
    T, D = x2.shape
    tm = TM_PROJ
    half = mix_a.shape[1]
    row = lambda i: (i, 0)
    const = lambda i: (0, 0)
    return pl.pallas_call(
        _outproj_kernel,
        grid=(T // tm,),
        in_specs=[pl.BlockSpec((tm, D), row), pl.BlockSpec((tm, half), row),
                  pl.BlockSpec((tm, half), row), pl.BlockSpec(w_out_b.shape, const),
                  pl.BlockSpec((1, D), const), pl.BlockSpec((D, 2 * LANES), const),
                  pl.BlockSpec((1, LANES), const)],
        out_specs=[pl.BlockSpec((tm, D), row),
                   pl.BlockSpec((tm, D // LANES, LANES), lambda i: (i, 0, 0)),
                   pl.BlockSpec((tm, LANES), row), pl.BlockSpec((tm, LANES), row)],
        out_shape=[jax.ShapeDtypeStruct((T, D), F32),
                   jax.ShapeDtypeStruct((T, D // LANES, LANES), BF16),
                   jax.ShapeDtypeStruct((T, LANES), I32), jax.ShapeDtypeStruct((T, LANES), F32)],
        compiler_params=_cparams(("arbitrary",)),
        name="outproj",
    )(x2, mix_a, mix_b, w_out_b, g_ffn, w_router, b_router)


def _lane_cumsum(x):
    lane = lax.broadcasted_iota(I32, x.shape, 1)
    shift = 1
    while shift < LANES:
        x = x + jnp.where(lane >= shift, pltpu.roll(x, shift, 1), 0)
        shift *= 2
    return x


def _plan_kernel(eid_ref, pos_ref, te_ref, nv_ref, meta_ref, rank_scr, cnt_scr, start_scr):
    phase = pl.program_id(0)
    i = pl.program_id(1)
    tm = eid_ref.shape[0]
    lane = lax.broadcasted_iota(I32, (tm, LANES), 1)
    eid = eid_ref[...]
    f1 = eid[:, 0:1]
    f2 = eid[:, 1:2]
    sel1 = lane == f1
    sel2 = lane == f2

    @pl.when((phase == 0) & (i == 0))
    def _():
        cnt_scr[...] = jnp.zeros_like(cnt_scr)

    @pl.when(phase == 0)
    def _():
        onehot = (sel1 | sel2).astype(BF16)
        r = lax.broadcasted_iota(I32, (tm, tm), 0)
        c = lax.broadcasted_iota(I32, (tm, tm), 1)
        strict_lower = (r > c).astype(BF16)
        before = jnp.dot(strict_lower, onehot, preferred_element_type=F32) + cnt_scr[0:1, :]
        r1 = jnp.sum(jnp.where(sel1, before, 0.0), axis=-1, keepdims=True)
        r2 = jnp.sum(jnp.where(sel2, before, 0.0), axis=-1, keepdims=True)
        rank = jnp.where(lane == 0, r1, jnp.where(lane == 1, r2, 0.0)).astype(I32)
        rank_scr[pl.ds(pl.multiple_of(i * tm, tm), tm), :] = rank
        cnt_scr[...] = cnt_scr[...] + jnp.sum(onehot.astype(F32), axis=0, keepdims=True)

    @pl.when(phase == 1)
    def _():
        starts = start_scr[0:1, :]
        o1 = jnp.sum(jnp.where(sel1, starts, 0.0), axis=-1, keepdims=True).astype(I32)
        o2 = jnp.sum(jnp.where(sel2, starts, 0.0), axis=-1, keepdims=True).astype(I32)
        rank = rank_scr[pl.ds(pl.multiple_of(i * tm, tm), tm), :]
        pos_ref[...] = rank + jnp.where(lane == 0, o1, jnp.where(lane == 1, o2, 0))

    @pl.when((phase == 0) & (i == pl.num_programs(1) - 1))
    def _():
        shift = int(math.log2(TM_EXPERT))
        cnt = cnt_scr[...].astype(I32)
        padded = ((cnt + (TM_EXPERT - 1)) >> shift) << shift
        ends = _lane_cumsum(padded)
        start_scr[...] = (ends - padded).astype(F32)
        nt = te_ref.shape[0]
        end_tile = (ends >> shift)[0:1, :]
        tl_lane = lax.broadcasted_iota(I32, (nt, LANES), 1)
        tile = lax.broadcasted_iota(I32, (nt, LANES), 0)
        done = jnp.where((tl_lane < N_EXPERTS) & (end_tile <= tile), 1.0, 0.0)
        te = jnp.sum(done, axis=-1, keepdims=True).astype(I32)
        te_ref[...] = jnp.broadcast_to(jnp.minimum(te, N_EXPERTS - 1), (nt, LANES))
        n_valid = jnp.sum(jnp.where(tl_lane[0:1, :] == N_EXPERTS - 1, end_tile, 0).astype(F32),
                          axis=-1, keepdims=True).astype(I32)
        nv_ref[...] = jnp.broadcast_to(n_valid, nv_ref.shape)
        row = lax.broadcasted_iota(I32, cnt.shape, 0)
        meta_ref[...] = jnp.where(row == 0, ends - padded + cnt, jnp.where(row == 1, padded - cnt, 0))


def _plan(eid, n_tiles):
    T = eid.shape[0]
    tm = TM_PLAN
    const = lambda p, i: (0, 0)
    return pl.pallas_call(
        _plan_kernel,
        grid=(2, T // tm),
        in_specs=[pl.BlockSpec((tm, LANES), lambda p, i: (i, 0))],
        out_specs=[pl.BlockSpec((tm, LANES), lambda p, i: (i * p, 0)),
                   pl.BlockSpec((n_tiles, LANES), const), pl.BlockSpec((SUBLANES, LANES), const),
                   pl.BlockSpec((SUBLANES, LANES), const)],
        out_shape=[jax.ShapeDtypeStruct((T, LANES), I32), jax.ShapeDtypeStruct((n_tiles, LANES), I32),
                   jax.ShapeDtypeStruct((SUBLANES, LANES), I32),
                   jax.ShapeDtypeStruct((SUBLANES, LANES), I32)],
        scratch_shapes=[pltpu.VMEM((T, LANES), I32), pltpu.VMEM((SUBLANES, LANES), F32),
                        pltpu.VMEM((SUBLANES, LANES), F32)],
        compiler_params=_cparams(("arbitrary", "arbitrary")),
        name="plan",
    )(eid)


_PAD_BITS = tuple(1 << b for b in reversed(range(int(math.log2(TM_EXPERT)))))


def _dispatch_kernel(pos_ref, seg_end_ref, seg_pad_ref, nv_ref, h_ref, xs_ref, zeros, sem, zsem):
    i = pl.program_id(0)
    tm = h_ref.shape[0]

    def issue(t, c):
        for k in range(2):
            pltpu.make_async_copy(h_ref.at[pl.ds(t, 1)], xs_ref.at[pl.ds(pos_ref[2 * t + k], 1)],
                                  sem).start()
        return c

    lax.fori_loop(0, tm, issue, 0, unroll=8)
    for k in range(2):
        pltpu.make_async_copy(h_ref, xs_ref.at[pl.ds(0, tm)], sem).wait()

    @pl.when(i == pl.num_programs(0) - 1)
    def _():
        zeros[...] = jnp.zeros_like(zeros)
        n_tiles = xs_ref.shape[0] // TM_EXPERT
        first_tail = n_tiles - N_EXPERTS

        def pad_copy(e, bit):
            n = seg_pad_ref[e]
            dst = seg_end_ref[e] + (n & ~(2 * bit - 1))
            return (n & bit) != 0, pltpu.make_async_copy(zeros.at[pl.ds(0, bit)],
                                                         xs_ref.at[pl.ds(dst, bit)], zsem)

        def tail_copy(j):
            return j >= nv_ref[0], pltpu.make_async_copy(
                zeros, xs_ref.at[pl.ds(j * TM_EXPERT, TM_EXPERT)], zsem)

        def for_each_fill(act):
            def per_expert(e, c):
                for bit in _PAD_BITS:
                    pred, cp = pad_copy(e, bit)
                    pl.when(pred)(functools.partial(act, cp))
                return c

            lax.fori_loop(0, N_EXPERTS, per_expert, 0)

            def per_tile(j, c):
                pred, cp = tail_copy(j)
                pl.when(pred)(functools.partial(act, cp))
                return c

            lax.fori_loop(first_tail, n_tiles, per_tile, 0)

        for_each_fill(lambda cp: cp.start())
        for_each_fill(lambda cp: cp.wait())


def _dispatch(pos_flat, seg_end, seg_pad, n_valid, h2, n_rows):
    T = h2.shape[0]
    slab = h2.shape[1:]
    tm = TM_DISPATCH
    smem = pl.BlockSpec(memory_space=pltpu.SMEM)
    return pl.pallas_call(
        _dispatch_kernel,
        grid=(T // tm,),
        in_specs=[pl.BlockSpec((2 * tm,), lambda i: (i,), memory_space=pltpu.SMEM),
                  smem, smem, smem, pl.BlockSpec((tm,) + slab, lambda i: (i, 0, 0))],
        out_specs=pl.BlockSpec(memory_space=pl.ANY),
        out_shape=jax.ShapeDtypeStruct((n_rows,) + slab, h2.dtype),
        scratch_shapes=[pltpu.VMEM((TM_EXPERT,) + slab, h2.dtype), pltpu.SemaphoreType.DMA(()),
                        pltpu.SemaphoreType.DMA(())],
        compiler_params=_cparams(("arbitrary",)),
        name="dispatch",
    )(pos_flat, seg_end, seg_pad, n_valid, h2)


def _experts_kernel(te_ref, nv_ref, xs_ref, wg_ref, wu_ref, wd_ref, ys_ref, wg_b, wu_b, wd_b):
    j = pl.program_id(0)
    new_expert = (j == 0) | (te_ref[j] != te_ref[jnp.maximum(j - 1, 0)])

    @pl.when(new_expert)
    def _():
        wg_b[...] = wg_ref[0].astype(BF16)
        wu_b[...] = wu_ref[0].astype(BF16)
        wd_b[...] = wd_ref[0].astype(BF16)

    @pl.when(j < nv_ref[0])
    def _():
        x = xs_ref[...].reshape(xs_ref.shape[0], -1)
        y = None
        for lo in range(0, wg_b.shape[1], EXPERT_FF_BLOCK):
            cols = slice(lo, lo + EXPERT_FF_BLOCK)
            a = jnp.dot(x, wg_b[:, cols], preferred_element_type=F32)
            u = jnp.dot(x, wu_b[:, cols], preferred_element_type=F32)
            hid = (a * jax.nn.sigmoid(a) * u).astype(BF16)
            part = jnp.dot(hid, wd_b[cols, :], preferred_element_type=F32)
            y = part if y is None else y + part
        ys_ref[...] = y.astype(BF16).reshape(ys_ref.shape)

    @pl.when(j >= nv_ref[0])
    def _():
        ys_ref[...] = jnp.zeros_like(ys_ref)


def _experts(tile_expert, n_valid, xs, wg, wu, wd):
    R = xs.shape[0]
    slab = xs.shape[1:]
    E, D, Fh = wg.shape
    tm = TM_EXPERT
    n_tiles = R // tm
    rows = lambda j, te, nv: (j, 0, 0)
    wsel = lambda j, te, nv: (te[j], 0, 0)
    return pl.pallas_call(
        _experts_kernel,
        grid_spec=pltpu.PrefetchScalarGridSpec(
            num_scalar_prefetch=2,
            grid=(n_tiles,),
            in_specs=[pl.BlockSpec((tm,) + slab, rows),
                      pl.BlockSpec((1, D, Fh), wsel), pl.BlockSpec((1, D, Fh), wsel),
                      pl.BlockSpec((1, Fh, D), wsel)],
            out_specs=pl.BlockSpec((tm,) + slab, rows),
            scratch_shapes=[pltpu.VMEM((D, Fh), BF16), pltpu.VMEM((D, Fh), BF16),
                            pltpu.VMEM((Fh, D), BF16)]),
        out_shape=jax.ShapeDtypeStruct(xs.shape, xs.dtype),
        compiler_params=_cparams(("arbitrary",)),
        name="experts",
    )(tile_expert, n_valid, xs, wg, wu, wd)


def _combine_kernel(pos_ref, pos_next_ref, x1_ref, wt_ref, p_ref, gp_ref, wpg_ref, wpp_ref, gf_ref,
                    ys_ref, o_ref, buf, sem):
    i = pl.program_id(0)
    n = pl.num_programs(0)
    tm = x1_ref.shape[0]
    slot = i % 2

    def gather(idx_ref, s):
        def issue(t, c):
            for k in range(2):
                pltpu.make_async_copy(ys_ref.at[pl.ds(idx_ref[2 * t + k], 1)],
                                      buf.at[s, k, pl.ds(t, 1)], sem.at[s]).start()
            return c

        lax.fori_loop(0, tm, issue, 0, unroll=8)

    @pl.when(i == 0)
    def _():
        gather(pos_ref, 0)

    @pl.when(i + 1 < n)
    def _():
        gather(pos_next_ref, 1 - slot)

    for k in range(2):
        pltpu.make_async_copy(ys_ref.at[pl.ds(0, tm)], buf.at[slot, k], sem.at[slot]).wait()
    wt = wt_ref[...]
    y0 = buf[slot, 0].reshape(tm, -1).astype(F32)
    y1 = buf[slot, 1].reshape(tm, -1).astype(F32)
    x2 = x1_ref[...] + wt[:, 0:1] * y0 + wt[:, 1:2] * y1
    hp = _rms(x2, gp_ref[...]).astype(BF16)
    gate = jax.nn.sigmoid(jnp.dot(hp, wpg_ref[...], preferred_element_type=F32))
    proj = jnp.dot(p_ref[...].astype(BF16), wpp_ref[...], preferred_element_type=F32)
    x3 = x2 + gate * proj
    o_ref[...] = _rms(x3, gf_ref[...])


def _combine(pos_flat, x1, wts, p2, g_ple, w_ple_gate_b, w_ple_proj_b, g_final, ys):
    T, D = x1.shape
    P = p2.shape[1]
    tm = TM_COMBINE
    n = T // tm
    row = lambda i: (i, 0)
    const = lambda i: (0, 0)
    return pl.pallas_call(
        _combine_kernel,
        grid=(n,),
        in_specs=[pl.BlockSpec((2 * tm,), lambda i: (i,), memory_space=pltpu.SMEM),
                  pl.BlockSpec((2 * tm,), lambda i: (jnp.minimum(i + 1, n - 1),),
                               memory_space=pltpu.SMEM),
                  pl.BlockSpec((tm, D), row), pl.BlockSpec((tm, LANES), row),
                  pl.BlockSpec((tm, P), row), pl.BlockSpec((1, D), const),
                  pl.BlockSpec((D, D), const), pl.BlockSpec((P, D), const),
                  pl.BlockSpec((1, D), const), pl.BlockSpec(memory_space=pl.ANY)],
        out_specs=pl.BlockSpec((tm, D), row),
        out_shape=jax.ShapeDtypeStruct((T, D), F32),
        scratch_shapes=[pltpu.VMEM((2, 2, tm) + ys.shape[1:], ys.dtype),
                        pltpu.SemaphoreType.DMA((2,))],
        compiler_params=_cparams(("arbitrary",)),
        name="combine",
    )(pos_flat, pos_flat, x1, wts, p2, g_ple, w_ple_gate_b, w_ple_proj_b, g_final, ys)


def _block_diag(w):
    nb, d, _ = w.shape
    eye = jnp.eye(nb, dtype=w.dtype)
    return (eye[:, None, :, None] * w[:, :, None, :]).reshape(nb * d, nb * d)


def kernel(x, p, positions, g_mix, w_in, sinks, conv_w, conv_b, lru_wa, lru_ba, lru_wx, lru_bx,
           lru_lambda, g_attn_out, g_lru_out, w_out, g_ffn, w_router_group, b_router_group,
           w_router_expert, b_router_expert, w_expert_gate, w_expert_up, w_expert_down, g_ple,
           w_ple_gate, w_ple_proj, g_final):
    B, S, D = x.shape
    assert w_in.shape[0] == 1, "single-layer block only"
    T = B * S
    lru_w = conv_w.shape[-1]
    n_rows = 2 * T + N_EXPERTS * TM_EXPERT
    n_tiles = n_rows // TM_EXPERT
    pos2 = positions.reshape(T, 1).astype(I32)
    x2 = x.reshape(T, D)
    q, k, v, lx, lg = _inproj(x2, pos2, g_mix[0][None], w_in[0].astype(BF16), lru_w)
    mix_a = _attention(q, k, v, sinks[0], g_attn_out[0][None], B, S)
    mix_b = _lru(lx.reshape(B, S, lru_w), lg.reshape(B, S, lru_w), conv_w[0], conv_b[0][None],
                 _block_diag(lru_wa[0]).astype(BF16), lru_ba[0][None],
                 _block_diag(lru_wx[0]).astype(BF16), lru_bx[0][None],
                 lru_lambda[0][None], g_lru_out[0][None]).reshape(T, lru_w)
    n_router = N_GROUPS + N_EXPERTS
    w_router = jnp.pad(jnp.concatenate([w_router_group[0], w_router_expert[0]], axis=1),
                       ((0, 0), (0, LANES - n_router)))
    b_router = jnp.pad(jnp.concatenate([b_router_group[0], b_router_expert[0]]),
                       (0, LANES - n_router))[None]
    w_router_hi = w_router.astype(BF16)
    w_router_lo = (w_router - w_router_hi.astype(F32)).astype(BF16)
    x1, h2, eid, wts = _outproj(x2, mix_a, mix_b, w_out[0].astype(BF16), g_ffn[0][None],
                                jnp.concatenate([w_router_hi, w_router_lo], axis=1), b_router)
    pos, tile_expert, n_valid, meta = _plan(eid, n_tiles)
    pos_flat = pos[:, :2].reshape(2 * T)
    n_valid = n_valid[0, :1]
    xs = _dispatch(pos_flat, meta[0], meta[1], n_valid, h2, n_rows)
    Fh = w_expert_gate.shape[-1]
    ys = _experts(tile_expert[:, 0], n_valid, xs,
                  w_expert_gate[0].reshape(N_EXPERTS, D, Fh),
                  w_expert_up[0].reshape(N_EXPERTS, D, Fh),
                  w_expert_down[0].reshape(N_EXPERTS, Fh, D))
    out = _combine(pos_flat, x1, wts, p[0].reshape(T, -1), g_ple[0][None],
                   w_ple_gate[0].astype(BF16), w_ple_proj[0].astype(BF16), g_final[None], ys)
    return out.reshape(B, S, D)
```

```python
import functools
import math

import jax
import jax.numpy as jnp
from jax import lax
from jax.experimental import pallas as pl
from jax.experimental.pallas import tpu as pltpu

F32 = jnp.float32
BF16 = jnp.bfloat16
I32 = jnp.int32

EPS = 1e-6
N_HEADS = 8
N_KV_HEADS = 2
HEAD_DIM = 64
WINDOW = 128
ROPE_THETA = 10000.0
CONV_WIDTH = 4
LRU_C = 8.0
N_GROUPS = 4
EXPERTS_PER_GROUP = 8
N_EXPERTS = N_GROUPS * EXPERTS_PER_GROUP
NEG_INF = -1e30
LANES = 128
SUBLANES = 8

TM_INPROJ = 512
TM_PROJ = 512
TQ_ATTN = 512
TL_LRU = 256
TM_PLAN = 1024
TM_EXPERT = 512
EXPERT_FF_BLOCK = 256
TM_DISPATCH = 512
TM_COMBINE = 256
VMEM_LIMIT = 56 * 1024 * 1024


def _rms(x, g):
    ms = jnp.mean(x * x, axis=-1, keepdims=True)
    return x * lax.rsqrt(ms + EPS) * g


def _cparams(sem):
    return pltpu.CompilerParams(dimension_semantics=sem, vmem_limit_bytes=VMEM_LIMIT)


def _inproj_kernel(x_ref, pos_ref, g_ref, w_ref, q_ref, k_ref, v_ref, lx_ref, lg_ref):
    h = _rms(x_ref[...], g_ref[...]).astype(BF16)
    proj = jnp.dot(h, w_ref[...], preferred_element_type=F32)
    lane = lax.broadcasted_iota(I32, (1, LANES), 1)
    fidx = (lane % (HEAD_DIM // 2)).astype(F32)
    inv_freq = jnp.exp(fidx * (-2.0 / HEAD_DIM * math.log(ROPE_THETA)))
    ang = pos_ref[...].astype(F32) * inv_freq
    cos = jnp.cos(ang)
    sin = jnp.sin(ang)
    first_half = (lane % HEAD_DIM) < (HEAD_DIM // 2)
    sin_signed = jnp.where(first_half, -sin, sin)

    def rope(t):
        partner = jnp.where(first_half, pltpu.roll(t, LANES - HEAD_DIM // 2, 1),
                            pltpu.roll(t, HEAD_DIM // 2, 1))
        return t * cos + partner * sin_signed

    attn_w = N_HEADS * HEAD_DIM
    scale = HEAD_DIM ** -0.5
    assert math.frexp(scale)[0] == 0.5, "score scale must be a power of two to fold into q exactly"
    for c in range(attn_w // LANES):
        q_ref[:, c * LANES:(c + 1) * LANES] = (
            rope(proj[:, c * LANES:(c + 1) * LANES]) * scale).astype(BF16)
    kv_w = N_KV_HEADS * HEAD_DIM
    assert kv_w == LANES
    k = rope(proj[:, attn_w:attn_w + kv_w])
    v = proj[:, attn_w + kv_w:attn_w + 2 * kv_w]
    k_ref[:, :kv_w] = k.astype(BF16)
    k_ref[:, kv_w:] = pltpu.roll(k, HEAD_DIM, 1).astype(BF16)
    v_ref[:, :kv_w] = v.astype(BF16)
    v_ref[:, kv_w:] = pltpu.roll(v, HEAD_DIM, 1).astype(BF16)
    off_lx = attn_w + 2 * kv_w
    lru_w = lx_ref.shape[1]
    lx_ref[...] = proj[:, off_lx:off_lx + lru_w]
    lg_ref[...] = proj[:, off_lx + lru_w:off_lx + 2 * lru_w]


def _inproj(x2, pos2, g_mix, w_in_b, lru_w):
    T, D = x2.shape
    tm = TM_INPROJ
    attn_w = N_HEADS * HEAD_DIM
    kv_w = N_KV_HEADS * HEAD_DIM
    row = lambda i: (i, 0)
    const = lambda i: (0, 0)
    return pl.pallas_call(
        _inproj_kernel,
        grid=(T // tm,),
        in_specs=[pl.BlockSpec((tm, D), row), pl.BlockSpec((tm, 1), row),
                  pl.BlockSpec((1, D), const), pl.BlockSpec(w_in_b.shape, const)],
        out_specs=[pl.BlockSpec((tm, attn_w), row), pl.BlockSpec((tm, 2 * kv_w), row),
                   pl.BlockSpec((tm, 2 * kv_w), row), pl.BlockSpec((tm, lru_w), row),
                   pl.BlockSpec((tm, lru_w), row)],
        out_shape=[jax.ShapeDtypeStruct((T, attn_w), BF16),
                   jax.ShapeDtypeStruct((T, 2 * kv_w), BF16),
                   jax.ShapeDtypeStruct((T, 2 * kv_w), BF16), jax.ShapeDtypeStruct((T, lru_w), F32),
                   jax.ShapeDtypeStruct((T, lru_w), F32)],
        compiler_params=_cparams(("arbitrary",)),
        name="inproj",
    )(x2, pos2, g_mix, w_in_b)


def _attn_kernel(sinks_ref, q_ref, k_ref, v_ref, kp_ref, vp_ref, g_ref, o_ref):
    n = pl.program_id(1)
    blk = WINDOW
    hd = HEAD_DIM
    nqb = q_ref.shape[0] // blk
    n_pairs = N_HEADS // 2
    pairs_per_group = n_pairs // N_KV_HEADS
    lane = lax.broadcasted_iota(I32, (1, LANES), 1)
    first = lane < hd
    qi = lax.broadcasted_iota(I32, (blk, 2 * blk), 0)
    kj = lax.broadcasted_iota(I32, (blk, 2 * blk), 1)
    rel = qi + blk - kj
    band = (rel >= 0) & (rel < WINDOW)
    bd_row = lax.broadcasted_iota(I32, (4 * blk, LANES), 0)
    bd_lane = lax.broadcasted_iota(I32, (4 * blk, LANES), 1)
    ones_bd = ((bd_row < 2 * blk) == (bd_lane < hd)).astype(BF16)
    zero = jnp.zeros((), BF16)

    def block_diag(t):
        same, swapped = t[:, :LANES], t[:, LANES:]
        out = []
        for g in range(N_KV_HEADS):
            top = jnp.where(first, same if g == 0 else swapped, zero)
            bot = jnp.where(first, zero, swapped if g == 0 else same)
            out.append(jnp.concatenate([top, bot], axis=0))
        return out

    for j in range(nqb):
        rows = slice(j * blk, (j + 1) * blk)
        if j == 0:
            k_prev, v_prev = kp_ref[...], vp_ref[...]
            mask = band & ((kj >= blk) | (n > 0))
        else:
            prev_rows = slice((j - 1) * blk, j * blk)
            k_prev, v_prev = k_ref[prev_rows, :], v_ref[prev_rows, :]
            mask = band
        k_bd = block_diag(jnp.concatenate([k_prev, k_ref[rows, :]], axis=0))
        v_bd = block_diag(jnp.concatenate([v_prev, v_ref[rows, :]], axis=0))
        outs = []
        for p in range(n_pairs):
            g = p // pairs_per_group
            qp = q_ref[rows, p * LANES:(p + 1) * LANES]
            s = lax.dot_general(qp, k_bd[g], (((1,), (1,)), ((), ())), preferred_element_type=F32)
            es, ms = [], []
            for a in range(2):
                sa = jnp.where(mask, s[:, a * 2 * blk:(a + 1) * 2 * blk], NEG_INF)
                m = jnp.maximum(jnp.max(sa, axis=-1, keepdims=True), sinks_ref[2 * p + a])
                es.append(jnp.exp(sa - m))
                ms.append(m)
            e = jnp.concatenate(es, axis=1).astype(BF16)
            od = jnp.dot(e, jnp.concatenate([v_bd[g], ones_bd], axis=1), preferred_element_type=F32)
            sink_term = jnp.where(first, jnp.exp(sinks_ref[2 * p] - ms[0]),
                                  jnp.exp(sinks_ref[2 * p + 1] - ms[1]))
            outs.append(od[:, :LANES] / (od[:, LANES:] + sink_term))
        attn = jnp.concatenate(outs, axis=1)
        o_ref[rows, :] = _rms(attn, g_ref[...]).astype(BF16)


def _attention(q, k, v, sinks, g_attn, B, S):
    T, attn_w = q.shape
    kv_w = k.shape[1]
    blk = WINDOW
    tq = TQ_ATTN
    nt = S // tq
    cur = lambda b, n: (b * nt + n, 0)
    prev = lambda b, n: (b * (S // blk) + jnp.maximum(n * (tq // blk) - 1, 0), 0)
    return pl.pallas_call(
        _attn_kernel,
        grid=(B, nt),
        in_specs=[pl.BlockSpec(memory_space=pltpu.SMEM),
                  pl.BlockSpec((tq, attn_w), cur),
                  pl.BlockSpec((tq, kv_w), cur), pl.BlockSpec((tq, kv_w), cur),
                  pl.BlockSpec((blk, kv_w), prev), pl.BlockSpec((blk, kv_w), prev),
                  pl.BlockSpec((1, attn_w), lambda b, n: (0, 0))],
        out_specs=pl.BlockSpec((tq, attn_w), cur),
        out_shape=jax.ShapeDtypeStruct((T, attn_w), BF16),
        compiler_params=_cparams(("arbitrary", "arbitrary")),
        name="attn",
    )(sinks, q, k, v, k, v, g_attn)


def _neg_expm1(x, exp_x):
    series = -x * (1.0 + x * (0.5 + x * (1.0 / 6.0 + x * (1.0 / 24.0))))
    return jnp.where(x > -0.02, series, 1.0 - exp_x)


def _gelu_tanh(x):
    return 0.5 * x * (1.0 + jnp.tanh(math.sqrt(2.0 / math.pi) * (x + 0.044715 * (x * x * x))))


def _lru_kernel(lx_ref, lg_ref, cw_ref, cb_ref, wa_ref, ba_ref, wx_ref, bx_ref, lam_ref, g_ref,
                o_ref, xpad, a_scr, b_scr, h_scr, h_carry):
    i = pl.program_id(0)
    B, tl, C = lx_ref.shape
    pad = SUBLANES

    @pl.when(i == 0)
    def _():
        xpad[:, 0:pad, :] = jnp.zeros((B, pad, C), F32)
        h_carry[...] = jnp.zeros_like(h_carry)

    @pl.when(i > 0)
    def _():
        xpad[:, 0:pad, :] = xpad[:, tl:tl + pad, :]

    xpad[:, pad:, :] = lx_ref[...]
    y = jnp.zeros((B, tl, C), F32) + cb_ref[...]
    for j in range(CONV_WIDTH):
        y = y + xpad[:, pl.ds(pad - (CONV_WIDTH - 1) + j, tl), :] * cw_ref[j:j + 1, :]
    y2 = y.reshape(B * tl, C)
    yb = y2.astype(BF16)
    gate_a = jnp.dot(yb, wa_ref[...], preferred_element_type=F32) + ba_ref[...]
    gate_x = jnp.dot(yb, wx_ref[...], preferred_element_type=F32) + bx_ref[...]
    r = jax.nn.sigmoid(gate_a)
    ig = jax.nn.sigmoid(gate_x)
    nl = -lam_ref[...]
    softplus = jnp.maximum(nl, 0.0) + jnp.log1p(jnp.exp(-jnp.abs(nl)))
    log_a = (-LRU_C) * r * softplus
    a_all = jnp.exp(log_a)
    one_minus_a2 = _neg_expm1(2.0 * log_a, a_all * a_all)
    b_all = one_minus_a2 * lax.rsqrt(jnp.maximum(one_minus_a2, 1e-30)) * ig * y2
    groups = (B * tl // SUBLANES, SUBLANES, C)
    sub = lax.broadcasted_iota(I32, groups, 1)
    a_cum, b_loc = a_all.reshape(groups), b_all.reshape(groups)
    d = 1
    while d < SUBLANES:
        keep = sub >= d
        a_prev = pltpu.roll(a_cum, d, 1)
        b_prev = pltpu.roll(b_loc, d, 1)
        b_loc = b_loc + jnp.where(keep, a_cum * b_prev, 0.0)
        a_cum = jnp.where(keep, a_cum * a_prev, a_cum)
        d *= 2
    a_scr[...] = a_cum.reshape(B, tl, C)
    b_scr[...] = b_loc.reshape(B, tl, C)

    def carry(g, h_prev):
        rows = pl.ds(pl.multiple_of(g * SUBLANES, SUBLANES), SUBLANES)
        h = b_scr[:, rows, :] + a_scr[:, rows, :] * h_prev
        h_scr[:, rows, :] = h
        return h[:, SUBLANES - 1:SUBLANES, :]

    h_carry[...] = lax.fori_loop(0, tl // SUBLANES, carry, h_carry[...], unroll=4)
    lru = h_scr[...].reshape(B * tl, C) * _gelu_tanh(lg_ref[...].reshape(B * tl, C))
    o_ref[...] = _rms(lru, g_ref[...]).astype(BF16).reshape(B, tl, C)


def _lru(lx3, lg3, conv_w, conv_b, wa_bd, ba, wx_bd, bx, lam, g_lru):
    B, S, C = lx3.shape
    tl = TL_LRU
    blk = lambda i: (0, i, 0)
    const = lambda i: (0, 0)
    vec = pl.BlockSpec((1, C), const)
    return pl.pallas_call(
        _lru_kernel,
        grid=(S // tl,),
        in_specs=[pl.BlockSpec((B, tl, C), blk), pl.BlockSpec((B, tl, C), blk),
                  pl.BlockSpec((CONV_WIDTH, C), const), vec,
                  pl.BlockSpec((C, C), const), vec, pl.BlockSpec((C, C), const), vec, vec, vec],
        out_specs=pl.BlockSpec((B, tl, C), blk),
        out_shape=jax.ShapeDtypeStruct((B, S, C), BF16),
        scratch_shapes=[pltpu.VMEM((B, tl + SUBLANES, C), F32),
                        pltpu.VMEM((B, tl, C), F32), pltpu.VMEM((B, tl, C), F32),
                        pltpu.VMEM((B, tl, C), F32), pltpu.VMEM((B, 1, C), F32)],
        compiler_params=_cparams(("arbitrary",)),
        name="lru",
    )(lx3, lg3, conv_w, conv_b, wa_bd, ba, wx_bd, bx, lam, g_lru)


def _outproj_kernel(x_ref, ma_ref, mb_ref, wo_ref, g_ref, wr_ref, br_ref,
                    x1_ref, h2_ref, eid_ref, wt_ref):
    half = ma_ref.shape[1]
    x1 = (x_ref[...]
          + jnp.dot(ma_ref[...], wo_ref[0:half, :], preferred_element_type=F32)
          + jnp.dot(mb_ref[...], wo_ref[half:, :], preferred_element_type=F32))
    x1_ref[...] = x1
    h2 = _rms(x1, g_ref[...])
    h2_ref[...] = h2.astype(BF16).reshape(h2_ref.shape)
    h_hi = h2.astype(BF16)
    h_lo = (h2 - h_hi.astype(F32)).astype(BF16)
    hw = jnp.dot(h_hi, wr_ref[...], preferred_element_type=F32)
    logits = (hw[:, :LANES] + hw[:, LANES:]
              + jnp.dot(h_lo, wr_ref[:, :LANES], preferred_element_type=F32) + br_ref[...])
    tm = logits.shape[0]
    lane = lax.broadcasted_iota(I32, (tm, LANES), 1)
    lane_f = lane.astype(F32)
    big = float(LANES)

    def first_argmax(vals):
        m = jnp.max(vals, axis=-1, keepdims=True)
        idx = jnp.min(jnp.where(vals == m, lane_f, big), axis=-1, keepdims=True)
        return m, idx.astype(I32)

    gl = jnp.where(lane < N_GROUPS, logits, NEG_INF)
    gmax, gidx = first_argmax(gl)
    gsum = jnp.sum(jnp.where(lane < N_GROUPS, jnp.exp(gl - gmax), 0.0), axis=-1, keepdims=True)
    g_top_p = 1.0 / gsum
    lo = N_GROUPS + EXPERTS_PER_GROUP * gidx
    el = jnp.where((lane >= lo) & (lane < lo + EXPERTS_PER_GROUP), logits, NEG_INF)
    m1, i1 = first_argmax(el)
    el2 = jnp.where(lane == i1, NEG_INF, el)
    m2, i2 = first_argmax(el2)
    ratio = jnp.exp(m2 - m1)
    w1 = g_top_p / (1.0 + ratio)
    w2 = g_top_p * ratio / (1.0 + ratio)
    eid_ref[...] = jnp.where(lane == 0, i1 - N_GROUPS, jnp.where(lane == 1, i2 - N_GROUPS, 0))
    wt_ref[...] = jnp.where(lane == 0, w1, jnp.where(lane == 1, w2, 0.0))


def _outproj(x2, mix_a, mix_b, w_out_b, g_ffn, w_router, b_router):
    T, D = x2.shape
    tm = TM_PROJ
    half = mix_a.shape[1]
    row = lambda i: (i, 0)
    const = lambda i: (0, 0)
    return pl.pallas_call(
        _outproj_kernel,
        grid=(T // tm,),
        in_specs=[pl.BlockSpec((tm, D), row), pl.BlockSpec((tm, half), row),
                  pl.BlockSpec((tm, half), row), pl.BlockSpec(w_out_b.shape, const),
                  pl.BlockSpec((1, D), const), pl.BlockSpec((D, 2 * LANES), const),
                  pl.BlockSpec((1, LANES), const)],
        out_specs=[pl.BlockSpec((tm, D), row),
                   pl.BlockSpec((tm, D // LANES, LANES), lambda i: (i, 0, 0)),
                   pl.BlockSpec((tm, LANES), row), pl.BlockSpec((tm, LANES), row)],
        out_shape=[jax.ShapeDtypeStruct((T, D), F32),
                   jax.ShapeDtypeStruct((T, D // LANES, LANES), BF16),
                   jax.ShapeDtypeStruct((T, LANES), I32), jax.ShapeDtypeStruct((T, LANES), F32)],
        compiler_params=_cparams(("arbitrary",)),
        name="outproj",
    )(x2, mix_a, mix_b, w_out_b, g_ffn, w_router, b_router)


def _lane_cumsum(x):
    lane = lax.broadcasted_iota(I32, x.shape, 1)
    shift = 1
    while shift < LANES:
        x = x + jnp.where(lane >= shift, pltpu.roll(x, shift, 1), 0)
        shift *= 2
    return x


def _plan_kernel(eid_ref, pos_ref, lrow_ref, te_ref, nv_ref, meta_ref, tcnt_ref, tstart_ref,
                 rank_scr, cnt_scr, start_scr, tcnt_scr, tpre_scr):
    phase = pl.program_id(0)
    i = pl.program_id(1)
    tm = eid_ref.shape[0]
    td = TM_DISPATCH
    nsub = tm // td
    lane = lax.broadcasted_iota(I32, (tm, LANES), 1)
    eid = eid_ref[...]
    f1 = eid[:, 0:1]
    f2 = eid[:, 1:2]
    sel1 = lane == f1
    sel2 = lane == f2

    @pl.when((phase == 0) & (i == 0))
    def _():
        cnt_scr[...] = jnp.zeros_like(cnt_scr)

    @pl.when(phase == 0)
    def _():
        onehot = (sel1 | sel2).astype(BF16)
        r = lax.broadcasted_iota(I32, (tm, tm), 0)
        c = lax.broadcasted_iota(I32, (tm, tm), 1)
        strict_lower = (r > c).astype(BF16)
        within = jnp.dot(strict_lower, onehot, preferred_element_type=F32)
        base = cnt_scr[0:1, :]
        before = within + base
        r1 = jnp.sum(jnp.where(sel1, before, 0.0), axis=-1, keepdims=True)
        r2 = jnp.sum(jnp.where(sel2, before, 0.0), axis=-1, keepdims=True)
        rank = jnp.where(lane == 0, r1, jnp.where(lane == 1, r2, 0.0)).astype(I32)
        rank_scr[pl.ds(pl.multiple_of(i * tm, tm), tm), :] = rank
        onehot_f = onehot.astype(F32)
        row = lax.broadcasted_iota(I32, (tm, LANES), 0)
        srow = lax.broadcasted_iota(I32, (SUBLANES, LANES), 0)
        counts = [jnp.sum(onehot_f[s * td:(s + 1) * td], axis=0, keepdims=True) for s in range(nsub)]
        cnt_mat = jnp.zeros((SUBLANES, LANES), F32)
        seen = jnp.zeros((1, LANES), F32)
        local_before = within
        for s in range(nsub):
            tile = i * nsub + s
            tcnt_scr[pl.ds(tile, 1), :] = counts[s]
            tpre_scr[pl.ds(tile, 1), :] = base + seen
            cnt_mat = jnp.where(srow == s, counts[s], cnt_mat)
            if s > 0:
                local_before = jnp.where(row >= s * td, within - seen, local_before)
            seen = seen + counts[s]
        cnt_i = cnt_mat.astype(I32)
        run_start = (_lane_cumsum(cnt_i) - cnt_i).astype(F32)
        local = local_before + run_start[0:1, :]
        for s in range(1, nsub):
            local = jnp.where(row >= s * td, local_before + run_start[s:s + 1, :], local)
        l1 = jnp.sum(jnp.where(sel1, local, 0.0), axis=-1, keepdims=True)
        l2 = jnp.sum(jnp.where(sel2, local, 0.0), axis=-1, keepdims=True)
        lrow_ref[...] = jnp.where(lane == 0, l1, jnp.where(lane == 1, l2, 0.0)).astype(I32)
        cnt_scr[...] = cnt_scr[...] + seen

    @pl.when(phase == 1)
    def _():
        starts = start_scr[0:1, :]
        o1 = jnp.sum(jnp.where(sel1, starts, 0.0), axis=-1, keepdims=True).astype(I32)
        o2 = jnp.sum(jnp.where(sel2, starts, 0.0), axis=-1, keepdims=True).astype(I32)
        rank = rank_scr[pl.ds(pl.multiple_of(i * tm, tm), tm), :]
        pos_ref[...] = rank + jnp.where(lane == 0, o1, jnp.where(lane == 1, o2, 0))

    @pl.when((phase == 0) & (i == pl.num_programs(1) - 1))
    def _():
        shift = int(math.log2(TM_EXPERT))
        cnt = cnt_scr[...].astype(I32)
        padded = ((cnt + (TM_EXPERT - 1)) >> shift) << shift
        ends = _lane_cumsum(padded)
        start_scr[...] = (ends - padded).astype(F32)
        nt = te_ref.shape[0]
        end_tile = (ends >> shift)[0:1, :]
        tl_lane = lax.broadcasted_iota(I32, (nt, LANES), 1)
        tile = lax.broadcasted_iota(I32, (nt, LANES), 0)
        done = jnp.where((tl_lane < N_EXPERTS) & (end_tile <= tile), 1.0, 0.0)
        te = jnp.sum(done, axis=-1, keepdims=True).astype(I32)
        te_ref[...] = jnp.broadcast_to(jnp.minimum(te, N_EXPERTS - 1), (nt, LANES))
        n_valid = jnp.sum(jnp.where(tl_lane[0:1, :] == N_EXPERTS - 1, end_tile, 0).astype(F32),
                          axis=-1, keepdims=True).astype(I32)
        nv_ref[...] = jnp.broadcast_to(n_valid, nv_ref.shape)
        row = lax.broadcasted_iota(I32, cnt.shape, 0)
        meta_ref[...] = jnp.where(row == 0, ends - padded + cnt, jnp.where(row == 1, padded - cnt, 0))
        tcnt_ref[...] = tcnt_scr[...].astype(I32)
        tstart_ref[...] = (tpre_scr[...] + (ends - padded).astype(F32)[0:1, :]).astype(I32)


def _plan(eid, n_tiles):
    T = eid.shape[0]
    tm = TM_PLAN
    steps = T // tm
    n_dtiles = T // TM_DISPATCH
    const = lambda p, i: (0, 0)
    return pl.pallas_call(
        _plan_kernel,
        grid=(2, steps),
        in_specs=[pl.BlockSpec((tm, LANES), lambda p, i: (i, 0))],
        out_specs=[pl.BlockSpec((tm, LANES), lambda p, i: (i * p, 0)),
                   pl.BlockSpec((tm, LANES), lambda p, i: (i * (1 - p) + (steps - 1) * p, 0)),
                   pl.BlockSpec((n_tiles, LANES), const), pl.BlockSpec((SUBLANES, LANES), const),
                   pl.BlockSpec((SUBLANES, LANES), const),
                   pl.BlockSpec((n_dtiles, LANES), const), pl.BlockSpec((n_dtiles, LANES), const)],
        out_shape=[jax.ShapeDtypeStruct((T, LANES), I32), jax.ShapeDtypeStruct((T, LANES), I32),
                   jax.ShapeDtypeStruct((n_tiles, LANES), I32),
                   jax.ShapeDtypeStruct((SUBLANES, LANES), I32),
                   jax.ShapeDtypeStruct((SUBLANES, LANES), I32),
                   jax.ShapeDtypeStruct((n_dtiles, LANES), I32),
                   jax.ShapeDtypeStruct((n_dtiles, LANES), I32)],
        scratch_shapes=[pltpu.VMEM((T, LANES), I32), pltpu.VMEM((SUBLANES, LANES), F32),
                        pltpu.VMEM((SUBLANES, LANES), F32),
                        pltpu.VMEM((n_dtiles, LANES), F32), pltpu.VMEM((n_dtiles, LANES), F32)],
        compiler_params=_cparams(("arbitrary", "arbitrary")),
        name="plan",
    )(eid)


_PAD_BITS = tuple(1 << b for b in reversed(range(int(math.log2(TM_EXPERT)))))


_RUN_BITS = tuple(1 << b for b in reversed(range(int(math.log2(TM_DISPATCH)) + 1)))


def _dispatch_kernel(tcnt_ref, tstart_ref, seg_end_ref, seg_pad_ref, nv_ref, lrow_ref, h_ref,
                     xs_ref, lbuf, zeros, sem, zsem):
    i = pl.program_id(0)
    n = pl.num_programs(0)
    td = h_ref.shape[0]
    nr = 2 * td
    slot = i % 2

    def wait_slot(s):
        pltpu.make_async_copy(lbuf.at[s], xs_ref.at[pl.ds(0, nr)], sem.at[s]).wait()

    @pl.when(i >= 2)
    def _():
        wait_slot(slot)

    local_row = lrow_ref[...].astype(F32).T
    r = lax.broadcasted_iota(I32, (nr, td), 0).astype(F32)
    onehot = ((r == local_row[0:1, :]) | (r == local_row[1:2, :])).astype(BF16)
    sorted_rows = jnp.dot(onehot, h_ref[...].reshape(td, -1), preferred_element_type=F32)
    lbuf[slot] = sorted_rows.astype(BF16).reshape(lbuf.shape[1:])

    def run(e, lo):
        cnt = tcnt_ref[i * N_EXPERTS + e]
        dst = tstart_ref[i * N_EXPERTS + e]
        for bit in _RUN_BITS:
            off = cnt & ~(2 * bit - 1)

            @pl.when((cnt & bit) != 0)
            def _():
                pltpu.make_async_copy(lbuf.at[slot, pl.ds(lo + off, bit)],
                                      xs_ref.at[pl.ds(dst + off, bit)], sem.at[slot]).start()
        return lo + cnt

    lax.fori_loop(0, N_EXPERTS, run, 0)

    @pl.when(i == n - 1)
    def _():
        wait_slot(slot)

        @pl.when(n >= 2)
        def _():
            wait_slot(1 - slot)

        zeros[...] = jnp.zeros_like(zeros)
        n_tiles = xs_ref.shape[0] // TM_EXPERT
        first_tail = n_tiles - N_EXPERTS

        def pad_copy(e, bit):
            n = seg_pad_ref[e]
            dst = seg_end_ref[e] + (n & ~(2 * bit - 1))
            return (n & bit) != 0, pltpu.make_async_copy(zeros.at[pl.ds(0, bit)],
                                                         xs_ref.at[pl.ds(dst, bit)], zsem)

        def tail_copy(j):
            return j >= nv_ref[0], pltpu.make_async_copy(
                zeros, xs_ref.at[pl.ds(j * TM_EXPERT, TM_EXPERT)], zsem)

        def for_each_fill(act):
            def per_expert(e, c):
                for bit in _PAD_BITS:
                    pred, cp = pad_copy(e, bit)
                    pl.when(pred)(functools.partial(act, cp))
                return c

            lax.fori_loop(0, N_EXPERTS, per_expert, 0)

            def per_tile(j, c):
                pred, cp = tail_copy(j)
                pl.when(pred)(functools.partial(act, cp))
                return c

            lax.fori_loop(first_tail, n_tiles, per_tile, 0)

        for_each_fill(lambda cp: cp.start())
        for_each_fill(lambda cp: cp.wait())


def _dispatch(tile_cnt, tile_start, seg_end, seg_pad, n_valid, lrow, h2, n_rows):
    T = h2.shape[0]
    slab = h2.shape[1:]
    td = TM_DISPATCH
    smem = pl.BlockSpec(memory_space=pltpu.SMEM)
    return pl.pallas_call(
        _dispatch_kernel,
        grid=(T // td,),
        in_specs=[smem, smem, smem, smem, smem,
                  pl.BlockSpec((td, LANES), lambda i: (i, 0)),
                  pl.BlockSpec((td,) + slab, lambda i: (i, 0, 0))],
        out_specs=pl.BlockSpec(memory_space=pl.ANY),
        out_shape=jax.ShapeDtypeStruct((n_rows,) + slab, h2.dtype),
        scratch_shapes=[pltpu.VMEM((2, 2 * td) + slab, h2.dtype),
                        pltpu.VMEM((TM_EXPERT,) + slab, h2.dtype),
                        pltpu.SemaphoreType.DMA((2,)), pltpu.SemaphoreType.DMA(())],
        compiler_params=_cparams(("arbitrary",)),
        name="dispatch",
    )(tile_cnt, tile_start, seg_end, seg_pad, n_valid, lrow, h2)


def _experts_kernel(te_ref, nv_ref, xs_ref, wg_ref, wu_ref, wd_ref, ys_ref, wg_b, wu_b, wd_b):
    j = pl.program_id(0)
    new_expert = (j == 0) | (te_ref[j] != te_ref[jnp.maximum(j - 1, 0)])

    @pl.when(new_expert)
    def _():
        wg_b[...] = wg_ref[0].astype(BF16)
        wu_b[...] = wu_ref[0].astype(BF16)
        wd_b[...] = wd_ref[0].astype(BF16)

    @pl.when(j < nv_ref[0])
    def _():
        x = xs_ref[...].reshape(xs_ref.shape[0], -1)
        y = None
        for lo in range(0, wg_b.shape[1], EXPERT_FF_BLOCK):
            cols = slice(lo, lo + EXPERT_FF_BLOCK)
            a = jnp.dot(x, wg_b[:, cols], preferred_element_type=F32)
            u = jnp.dot(x, wu_b[:, cols], preferred_element_type=F32)
            hid = (a * jax.nn.sigmoid(a) * u).astype(BF16)
            part = jnp.dot(hid, wd_b[cols, :], preferred_element_type=F32)
            y = part if y is None else y + part
        ys_ref[...] = y.astype(BF16).reshape(ys_ref.shape)

    @pl.when(j >= nv_ref[0])
    def _():
        ys_ref[...] = jnp.zeros_like(ys_ref)


def _experts(tile_expert, n_valid, xs, wg, wu, wd):
    R = xs.shape[0]
    slab = xs.shape[1:]
    E, D, Fh = wg.shape
    tm = TM_EXPERT
    n_tiles = R // tm
    rows = lambda j, te, nv: (j, 0, 0)
    wsel = lambda j, te, nv: (te[j], 0, 0)
    return pl.pallas_call(
        _experts_kernel,
        grid_spec=pltpu.PrefetchScalarGridSpec(
            num_scalar_prefetch=2,
            grid=(n_tiles,),
            in_specs=[pl.BlockSpec((tm,) + slab, rows),
                      pl.BlockSpec((1, D, Fh), wsel), pl.BlockSpec((1, D, Fh), wsel),
                      pl.BlockSpec((1, Fh, D), wsel)],
            out_specs=pl.BlockSpec((tm,) + slab, rows),
            scratch_shapes=[pltpu.VMEM((D, Fh), BF16), pltpu.VMEM((D, Fh), BF16),
                            pltpu.VMEM((Fh, D), BF16)]),
        out_shape=jax.ShapeDtypeStruct(xs.shape, xs.dtype),
        compiler_params=_cparams(("arbitrary",)),
        name="experts",
    )(tile_expert, n_valid, xs, wg, wu, wd)


def _combine_kernel(pos_ref, pos_next_ref, x1_ref, wt_ref, p_ref, gp_ref, wpg_ref, wpp_ref, gf_ref,
                    ys_ref, o_ref, buf, sem):
    i = pl.program_id(0)
    n = pl.num_programs(0)
    tm = x1_ref.shape[0]
    slot = i % 2

    def gather(idx_ref, s):
        def issue(t, c):
            for k in range(2):
                pltpu.make_async_copy(ys_ref.at[pl.ds(idx_ref[2 * t + k], 1)],
                                      buf.at[s, k, pl.ds(t, 1)], sem.at[s]).start()
            return c

        lax.fori_loop(0, tm, issue, 0, unroll=8)

    @pl.when(i == 0)
    def _():
        gather(pos_ref, 0)

    @pl.when(i + 1 < n)
    def _():
        gather(pos_next_ref, 1 - slot)

    for k in range(2):
        pltpu.make_async_copy(ys_ref.at[pl.ds(0, tm)], buf.at[slot, k], sem.at[slot]).wait()
    wt = wt_ref[...]
    y0 = buf[slot, 0].reshape(tm, -1).astype(F32)
    y1 = buf[slot, 1].reshape(tm, -1).astype(F32)
    x2 = x1_ref[...] + wt[:, 0:1] * y0 + wt[:, 1:2] * y1
    hp = _rms(x2, gp_ref[...]).astype(BF16)
    gate = jax.nn.sigmoid(jnp.dot(hp, wpg_ref[...], preferred_element_type=F32))
    proj = jnp.dot(p_ref[...].astype(BF16), wpp_ref[...], preferred_element_type=F32)
    x3 = x2 + gate * proj
    o_ref[...] = _rms(x3, gf_ref[...])


def _combine(pos_flat, x1, wts, p2, g_ple, w_ple_gate_b, w_ple_proj_b, g_final, ys):
    T, D = x1.shape
    P = p2.shape[1]
    tm = TM_COMBINE
    n = T // tm
    row = lambda i: (i, 0)
    const = lambda i: (0, 0)
    return pl.pallas_call(
        _combine_kernel,
        grid=(n,),
        in_specs=[pl.BlockSpec((2 * tm,), lambda i: (i,), memory_space=pltpu.SMEM),
                  pl.BlockSpec((2 * tm,), lambda i: (jnp.minimum(i + 1, n - 1),),
                               memory_space=pltpu.SMEM),
                  pl.BlockSpec((tm, D), row), pl.BlockSpec((tm, LANES), row),
                  pl.BlockSpec((tm, P), row), pl.BlockSpec((1, D), const),
                  pl.BlockSpec((D, D), const), pl.BlockSpec((P, D), const),
                  pl.BlockSpec((1, D), const), pl.BlockSpec(memory_space=pl.ANY)],
        out_specs=pl.BlockSpec((tm, D), row),
        out_shape=jax.ShapeDtypeStruct((T, D), F32),
        scratch_shapes=[pltpu.VMEM((2, 2, tm) + ys.shape[1:], ys.dtype),
                        pltpu.SemaphoreType.DMA((2,))],
        compiler_params=_cparams(("arbitrary",)),
        name="combine",
    )(pos_flat, pos_flat, x1, wts, p2, g_ple, w_ple_gate_b, w_ple_proj_b, g_final, ys)


def _block_diag(w):
    nb, d, _ = w.shape
    eye = jnp.eye(nb, dtype=w.dtype)
    return (eye[:, None, :, None] * w[:, :, None, :]).reshape(nb * d, nb * d)


def kernel(x, p, positions, g_mix, w_in, sinks, conv_w, conv_b, lru_wa, lru_ba, lru_wx, lru_bx,
           lru_lambda, g_attn_out, g_lru_out, w_out, g_ffn, w_router_group, b_router_group,
           w_router_expert, b_router_expert, w_expert_gate, w_expert_up, w_expert_down, g_ple,
           w_ple_gate, w_ple_proj, g_final):
    B, S, D = x.shape
    assert w_in.shape[0] == 1, "single-layer block only"
    T = B * S
    lru_w = conv_w.shape[-1]
    n_rows = 2 * T + N_EXPERTS * TM_EXPERT
    n_tiles = n_rows // TM_EXPERT
    pos2 = positions.reshape(T, 1).astype(I32)
    x2 = x.reshape(T, D)
    q, k, v, lx, lg = _inproj(x2, pos2, g_mix[0][None], w_in[0].astype(BF16), lru_w)
    mix_a = _attention(q, k, v, sinks[0], g_attn_out[0][None], B, S)
    mix_b = _lru(lx.reshape(B, S, lru_w), lg.reshape(B, S, lru_w), conv_w[0], conv_b[0][None],
                 _block_diag(lru_wa[0]).astype(BF16), lru_ba[0][None],
                 _block_diag(lru_wx[0]).astype(BF16), lru_bx[0][None],
                 lru_lambda[0][None], g_lru_out[0][None]).reshape(T, lru_w)
    n_router = N_GROUPS + N_EXPERTS
    w_router = jnp.pad(jnp.concatenate([w_router_group[0], w_router_expert[0]], axis=1),
                       ((0, 0), (0, LANES - n_router)))
    b_router = jnp.pad(jnp.concatenate([b_router_group[0], b_router_expert[0]]),
                       (0, LANES - n_router))[None]
    w_router_hi = w_router.astype(BF16)
    w_router_lo = (w_router - w_router_hi.astype(F32)).astype(BF16)
    x1, h2, eid, wts = _outproj(x2, mix_a, mix_b, w_out[0].astype(BF16), g_ffn[0][None],
                                jnp.concatenate([w_router_hi, w_router_lo], axis=1), b_router)
    pos, lrow, tile_expert, n_valid, meta, tile_cnt, tile_start = _plan(eid, n_tiles)
    pos_flat = pos[:, :2].reshape(2 * T)
    n_valid = n_valid[0, :1]
    xs = _dispatch(tile_cnt[:, :N_EXPERTS].reshape(-1), tile_start[:, :N_EXPERTS].reshape(-1),
                   meta[0], meta[1], n_valid, lrow, h2, n_rows)
    Fh = w_expert_gate.shape[-1]
    ys = _experts(tile_expert[:, 0], n_valid, xs,
                  w_expert_gate[0].reshape(N_EXPERTS, D, Fh),
                  w_expert_up[0].reshape(N_EXPERTS, D, Fh),
                  w_expert_down[0].reshape(N_EXPERTS, Fh, D))
    out = _combine(pos_flat, x1, wts, p[0].reshape(T, -1), g_ple[0][None],
                   w_ple_gate[0].astype(BF16), w_ple_proj[0].astype(BF16), g_final[None], ys)
    return out.reshape(B, S, D)
```

```python
import functools
import math

import jax
import jax.numpy as jnp
from jax import lax
from jax.experimental import pallas as pl
from jax.experimental.pallas import tpu as pltpu

F32 = jnp.float32
BF16 = jnp.bfloat16
I32 = jnp.int32

EPS = 1e-6
N_HEADS = 8
N_KV_HEADS = 2
HEAD_DIM = 64
WINDOW = 128
ROPE_THETA = 10000.0
CONV_WIDTH = 4
LRU_C = 8.0
N_GROUPS = 4
EXPERTS_PER_GROUP = 8
N_EXPERTS = N_GROUPS * EXPERTS_PER_GROUP
NEG_INF = -1e30
LANES = 128
SUBLANES = 8

TM_INPROJ = 512
TM_PROJ = 512
TQ_ATTN = 512
TL_LRU = 256
TM_PLAN = 1024
TM_EXPERT = 512
EXPERT_FF_BLOCK = 256
TM_DISPATCH = 512
TM_COMBINE = 256
VMEM_LIMIT = 56 * 1024 * 1024


def _rms(x, g):
    ms = jnp.mean(x * x, axis=-1, keepdims=True)
    return x * lax.rsqrt(ms + EPS) * g


_FETCH_ONCE = dict(pipeline_mode=pl.Buffered(1))


def _cparams(sem):
    return pltpu.CompilerParams(dimension_semantics=sem, vmem_limit_bytes=VMEM_LIMIT)


def _inproj_kernel(x_ref, pos_ref, g_ref, w_ref, q_ref, k_ref, v_ref, lx_ref, lg_ref, w_b):
    @pl.when(pl.program_id(0) == 0)
    def _():
        w_b[...] = w_ref[...].astype(BF16)

    h = _rms(x_ref[...], g_ref[...]).astype(BF16)
    proj = jnp.dot(h, w_b[...], preferred_element_type=F32)
    lane = lax.broadcasted_iota(I32, (1, LANES), 1)
    fidx = (lax.broadcasted_iota(I32, (LANES, 1), 0) % (HEAD_DIM // 2)).astype(F32)
    inv_freq = jnp.exp(fidx * (-2.0 / HEAD_DIM * math.log(ROPE_THETA)))
    pos = pos_ref[0].astype(F32)
    ang = jnp.concatenate([(inv_freq * pos[c:c + 1, :]).T for c in range(pos.shape[0])], axis=0)
    cos = jnp.cos(ang)
    sin = jnp.sin(ang)
    first_half = (lane % HEAD_DIM) < (HEAD_DIM // 2)
    sin_signed = jnp.where(first_half, -sin, sin)

    def rope(t):
        partner = jnp.where(first_half, pltpu.roll(t, LANES - HEAD_DIM // 2, 1),
                            pltpu.roll(t, HEAD_DIM // 2, 1))
        return t * cos + partner * sin_signed

    attn_w = N_HEADS * HEAD_DIM
    scale = HEAD_DIM ** -0.5
    assert math.frexp(scale)[0] == 0.5, "score scale must be a power of two to fold into q exactly"
    for c in range(attn_w // LANES):
        q_ref[:, c * LANES:(c + 1) * LANES] = (
            rope(proj[:, c * LANES:(c + 1) * LANES]) * scale).astype(BF16)
    kv_w = N_KV_HEADS * HEAD_DIM
    assert kv_w == LANES
    k = rope(proj[:, attn_w:attn_w + kv_w])
    v = proj[:, attn_w + kv_w:attn_w + 2 * kv_w]
    k_ref[:, :kv_w] = k.astype(BF16)
    k_ref[:, kv_w:] = pltpu.roll(k, HEAD_DIM, 1).astype(BF16)
    v_ref[:, :kv_w] = v.astype(BF16)
    v_ref[:, kv_w:] = pltpu.roll(v, HEAD_DIM, 1).astype(BF16)
    off_lx = attn_w + 2 * kv_w
    lru_w = lx_ref.shape[1]
    lx_ref[...] = proj[:, off_lx:off_lx + lru_w]
    lg_ref[...] = proj[:, off_lx + lru_w:off_lx + 2 * lru_w]


def _inproj(x2, pos3, g_mix, w_in, lru_w):
    T, D = x2.shape
    tm = TM_INPROJ
    attn_w = N_HEADS * HEAD_DIM
    kv_w = N_KV_HEADS * HEAD_DIM
    row = lambda i: (i, 0)
    const = lambda i: (0, 0)
    return pl.pallas_call(
        _inproj_kernel,
        grid=(T // tm,),
        in_specs=[pl.BlockSpec((tm, D), row),
                  pl.BlockSpec((1, tm // LANES, LANES), lambda i: (i, 0, 0)),
                  pl.BlockSpec((1, D), const), pl.BlockSpec(w_in.shape, const, **_FETCH_ONCE)],
        out_specs=[pl.BlockSpec((tm, attn_w), row), pl.BlockSpec((tm, 2 * kv_w), row),
                   pl.BlockSpec((tm, 2 * kv_w), row), pl.BlockSpec((tm, lru_w), row),
                   pl.BlockSpec((tm, lru_w), row)],
        out_shape=[jax.ShapeDtypeStruct((T, attn_w), BF16),
                   jax.ShapeDtypeStruct((T, 2 * kv_w), BF16),
                   jax.ShapeDtypeStruct((T, 2 * kv_w), BF16), jax.ShapeDtypeStruct((T, lru_w), F32),
                   jax.ShapeDtypeStruct((T, lru_w), F32)],
        scratch_shapes=[pltpu.VMEM(w_in.shape, BF16)],
        compiler_params=_cparams(("arbitrary",)),
        name="inproj",
    )(x2, pos3, g_mix, w_in)


def _attn_kernel(sinks_ref, q_ref, k_ref, v_ref, kp_ref, vp_ref, g_ref, o_ref):
    n = pl.program_id(1)
    blk = WINDOW
    hd = HEAD_DIM
    nqb = q_ref.shape[0] // blk
    n_pairs = N_HEADS // 2
    pairs_per_group = n_pairs // N_KV_HEADS
    lane = lax.broadcasted_iota(I32, (1, LANES), 1)
    first = lane < hd
    qi = lax.broadcasted_iota(I32, (blk, 2 * blk), 0)
    kj = lax.broadcasted_iota(I32, (blk, 2 * blk), 1)
    rel = qi + blk - kj
    band = (rel >= 0) & (rel < WINDOW)
    bd_row = lax.broadcasted_iota(I32, (4 * blk, LANES), 0)
    bd_lane = lax.broadcasted_iota(I32, (4 * blk, LANES), 1)
    ones_bd = ((bd_row < 2 * blk) == (bd_lane < hd)).astype(BF16)
    zero = jnp.zeros((), BF16)

    def block_diag(t):
        same, swapped = t[:, :LANES], t[:, LANES:]
        out = []
        for g in range(N_KV_HEADS):
            top = jnp.where(first, same if g == 0 else swapped, zero)
            bot = jnp.where(first, zero, swapped if g == 0 else same)
            out.append(jnp.concatenate([top, bot], axis=0))
        return out

    for j in range(nqb):
        rows = slice(j * blk, (j + 1) * blk)
        if j == 0:
            k_prev, v_prev = kp_ref[...], vp_ref[...]
            mask = band & ((kj >= blk) | (n > 0))
        else:
            prev_rows = slice((j - 1) * blk, j * blk)
            k_prev, v_prev = k_ref[prev_rows, :], v_ref[prev_rows, :]
            mask = band
        k_bd = block_diag(jnp.concatenate([k_prev, k_ref[rows, :]], axis=0))
        v_bd = block_diag(jnp.concatenate([v_prev, v_ref[rows, :]], axis=0))
        outs = []
        for p in range(n_pairs):
            g = p // pairs_per_group
            qp = q_ref[rows, p * LANES:(p + 1) * LANES]
            s = lax.dot_general(qp, k_bd[g], (((1,), (1,)), ((), ())), preferred_element_type=F32)
            es, ms = [], []
            for a in range(2):
                sa = jnp.where(mask, s[:, a * 2 * blk:(a + 1) * 2 * blk], NEG_INF)
                m = jnp.maximum(jnp.max(sa, axis=-1, keepdims=True), sinks_ref[2 * p + a])
                es.append(jnp.exp(sa - m))
                ms.append(m)
            e = jnp.concatenate(es, axis=1).astype(BF16)
            od = jnp.dot(e, jnp.concatenate([v_bd[g], ones_bd], axis=1), preferred_element_type=F32)
            sink_term = jnp.where(first, jnp.exp(sinks_ref[2 * p] - ms[0]),
                                  jnp.exp(sinks_ref[2 * p + 1] - ms[1]))
            outs.append(od[:, :LANES] / (od[:, LANES:] + sink_term))
        attn = jnp.concatenate(outs, axis=1)
        o_ref[rows, :] = _rms(attn, g_ref[...]).astype(BF16)


def _attention(q, k, v, sinks, g_attn, B, S):
    T, attn_w = q.shape
    kv_w = k.shape[1]
    blk = WINDOW
    tq = TQ_ATTN
    nt = S // tq
    cur = lambda b, n: (b * nt + n, 0)
    prev = lambda b, n: (b * (S // blk) + jnp.maximum(n * (tq // blk) - 1, 0), 0)
    return pl.pallas_call(
        _attn_kernel,
        grid=(B, nt),
        in_specs=[pl.BlockSpec(memory_space=pltpu.SMEM),
                  pl.BlockSpec((tq, attn_w), cur),
                  pl.BlockSpec((tq, kv_w), cur), pl.BlockSpec((tq, kv_w), cur),
                  pl.BlockSpec((blk, kv_w), prev), pl.BlockSpec((blk, kv_w), prev),
                  pl.BlockSpec((1, attn_w), lambda b, n: (0, 0))],
        out_specs=pl.BlockSpec((tq, attn_w), cur),
        out_shape=jax.ShapeDtypeStruct((T, attn_w), BF16),
        compiler_params=_cparams(("arbitrary", "arbitrary")),
        name="attn",
    )(sinks, q, k, v, k, v, g_attn)


def _neg_expm1(x, exp_x):
    series = -x * (1.0 + x * (0.5 + x * (1.0 / 6.0 + x * (1.0 / 24.0))))
    return jnp.where(x > -0.02, series, 1.0 - exp_x)


def _gelu_tanh(x):
    return 0.5 * x * (1.0 + jnp.tanh(math.sqrt(2.0 / math.pi) * (x + 0.044715 * (x * x * x))))


def _lru_kernel(lx_ref, lg_ref, cw_ref, cb_ref, wa_ref, ba_ref, wx_ref, bx_ref, lam_ref, g_ref,
                o_ref, xpad, a_scr, b_scr, h_scr, h_carry):
    i = pl.program_id(0)
    B, tl, C = lx_ref.shape
    pad = SUBLANES

    @pl.when(i == 0)
    def _():
        xpad[:, 0:pad, :] = jnp.zeros((B, pad, C), F32)
        h_carry[...] = jnp.zeros_like(h_carry)

    @pl.when(i > 0)
    def _():
        xpad[:, 0:pad, :] = xpad[:, tl:tl + pad, :]

    xpad[:, pad:, :] = lx_ref[...]
    y = jnp.zeros((B, tl, C), F32) + cb_ref[...]
    for j in range(CONV_WIDTH):
        y = y + xpad[:, pl.ds(pad - (CONV_WIDTH - 1) + j, tl), :] * cw_ref[j:j + 1, :]
    y2 = y.reshape(B * tl, C)
    yb = y2.astype(BF16)
    gate_a = jnp.dot(yb, wa_ref[...], preferred_element_type=F32) + ba_ref[...]
    gate_x = jnp.dot(yb, wx_ref[...], preferred_element_type=F32) + bx_ref[...]
    r = jax.nn.sigmoid(gate_a)
    ig = jax.nn.sigmoid(gate_x)
    nl = -lam_ref[...]
    softplus = jnp.maximum(nl, 0.0) + jnp.log1p(jnp.exp(-jnp.abs(nl)))
    log_a = (-LRU_C) * r * softplus
    a_all = jnp.exp(log_a)
    one_minus_a2 = _neg_expm1(2.0 * log_a, a_all * a_all)
    b_all = one_minus_a2 * lax.rsqrt(jnp.maximum(one_minus_a2, 1e-30)) * ig * y2
    groups = (B * tl // SUBLANES, SUBLANES, C)
    sub = lax.broadcasted_iota(I32, groups, 1)
    a_cum, b_loc = a_all.reshape(groups), b_all.reshape(groups)
    d = 1
    while d < SUBLANES:
        keep = sub >= d
        a_prev = pltpu.roll(a_cum, d, 1)
        b_prev = pltpu.roll(b_loc, d, 1)
        b_loc = b_loc + jnp.where(keep, a_cum * b_prev, 0.0)
        a_cum = jnp.where(keep, a_cum * a_prev, a_cum)
        d *= 2
    a_scr[...] = a_cum.reshape(B, tl, C)
    b_scr[...] = b_loc.reshape(B, tl, C)

    def carry(g, h_prev):
        rows = pl.ds(pl.multiple_of(g * SUBLANES, SUBLANES), SUBLANES)
        h = b_scr[:, rows, :] + a_scr[:, rows, :] * h_prev
        h_scr[:, rows, :] = h
        return h[:, SUBLANES - 1:SUBLANES, :]

    h_carry[...] = lax.fori_loop(0, tl // SUBLANES, carry, h_carry[...], unroll=4)
    lru = h_scr[...].reshape(B * tl, C) * _gelu_tanh(lg_ref[...].reshape(B * tl, C))
    o_ref[...] = _rms(lru, g_ref[...]).astype(BF16).reshape(B, tl, C)


def _lru(lx3, lg3, conv_w, conv_b, wa_bd, ba, wx_bd, bx, lam, g_lru):
    B, S, C = lx3.shape
    tl = TL_LRU
    blk = lambda i: (0, i, 0)
    const = lambda i: (0, 0)
    vec = pl.BlockSpec((1, C), const)
    return pl.pallas_call(
        _lru_kernel,
        grid=(S // tl,),
        in_specs=[pl.BlockSpec((B, tl, C), blk), pl.BlockSpec((B, tl, C), blk),
                  pl.BlockSpec((CONV_WIDTH, C), const), vec,
                  pl.BlockSpec((C, C), const), vec, pl.BlockSpec((C, C), const), vec, vec, vec],
        out_specs=pl.BlockSpec((B, tl, C), blk),
        out_shape=jax.ShapeDtypeStruct((B, S, C), BF16),
        scratch_shapes=[pltpu.VMEM((B, tl + SUBLANES, C), F32),
                        pltpu.VMEM((B, tl, C), F32), pltpu.VMEM((B, tl, C), F32),
                        pltpu.VMEM((B, tl, C), F32), pltpu.VMEM((B, 1, C), F32)],
        compiler_params=_cparams(("arbitrary",)),
        name="lru",
    )(lx3, lg3, conv_w, conv_b, wa_bd, ba, wx_bd, bx, lam, g_lru)


def _outproj_kernel(x_ref, ma_ref, mb_ref, wo_ref, g_ref, wr_ref, br_ref,
                    x1_ref, h2_ref, eid_ref, wt_ref, wo_b):
    @pl.when(pl.program_id(0) == 0)
    def _():
        wo_b[...] = wo_ref[...].astype(BF16)

    half = ma_ref.shape[1]
    x1 = (x_ref[...]
          + jnp.dot(ma_ref[...], wo_b[0:half, :], preferred_element_type=F32)
          + jnp.dot(mb_ref[...], wo_b[half:, :], preferred_element_type=F32))
    x1_ref[...] = x1
    h2 = _rms(x1, g_ref[...])
    h2_ref[...] = h2.astype(BF16).reshape(h2_ref.shape)
    h_hi = h2.astype(BF16)
    h_lo = (h2 - h_hi.astype(F32)).astype(BF16)
    hw = jnp.dot(h_hi, wr_ref[...], preferred_element_type=F32)
    logits = (hw[:, :LANES] + hw[:, LANES:]
              + jnp.dot(h_lo, wr_ref[:, :LANES], preferred_element_type=F32) + br_ref[...])
    tm = logits.shape[0]
    lane = lax.broadcasted_iota(I32, (tm, LANES), 1)
    lane_f = lane.astype(F32)
    big = float(LANES)

    def first_argmax(vals):
        m = jnp.max(vals, axis=-1, keepdims=True)
        idx = jnp.min(jnp.where(vals == m, lane_f, big), axis=-1, keepdims=True)
        return m, idx.astype(I32)

    gl = jnp.where(lane < N_GROUPS, logits, NEG_INF)
    gmax, gidx = first_argmax(gl)
    gsum = jnp.sum(jnp.where(lane < N_GROUPS, jnp.exp(gl - gmax), 0.0), axis=-1, keepdims=True)
    g_top_p = 1.0 / gsum
    lo = N_GROUPS + EXPERTS_PER_GROUP * gidx
    el = jnp.where((lane >= lo) & (lane < lo + EXPERTS_PER_GROUP), logits, NEG_INF)
    m1, i1 = first_argmax(el)
    el2 = jnp.where(lane == i1, NEG_INF, el)
    m2, i2 = first_argmax(el2)
    ratio = jnp.exp(m2 - m1)
    w1 = g_top_p / (1.0 + ratio)
    w2 = g_top_p * ratio / (1.0 + ratio)
    eid_ref[...] = jnp.where(lane == 0, i1 - N_GROUPS, jnp.where(lane == 1, i2 - N_GROUPS, 0))
    wt_ref[...] = jnp.where(lane == 0, w1, jnp.where(lane == 1, w2, 0.0))


def _outproj(x2, mix_a, mix_b, w_out, g_ffn, w_router, b_router):
    T, D = x2.shape
    tm = TM_PROJ
    half = mix_a.shape[1]
    row = lambda i: (i, 0)
    const = lambda i: (0, 0)
    return pl.pallas_call(
        _outproj_kernel,
        grid=(T // tm,),
        in_specs=[pl.BlockSpec((tm, D), row), pl.BlockSpec((tm, half), row),
                  pl.BlockSpec((tm, half), row), pl.BlockSpec(w_out.shape, const, **_FETCH_ONCE),
                  pl.BlockSpec((1, D), const), pl.BlockSpec((D, 2 * LANES), const),
                  pl.BlockSpec((1, LANES), const)],
        out_specs=[pl.BlockSpec((tm, D), row),
                   pl.BlockSpec((tm, D // LANES, LANES), lambda i: (i, 0, 0)),
                   pl.BlockSpec((tm, LANES), row), pl.BlockSpec((tm, LANES), row)],
        out_shape=[jax.ShapeDtypeStruct((T, D), F32),
                   jax.ShapeDtypeStruct((T, D // LANES, LANES), BF16),
                   jax.ShapeDtypeStruct((T, LANES), I32), jax.ShapeDtypeStruct((T, LANES), F32)],
        scratch_shapes=[pltpu.VMEM(w_out.shape, BF16)],
        compiler_params=_cparams(("arbitrary",)),
        name="outproj",
    )(x2, mix_a, mix_b, w_out, g_ffn, w_router, b_router)


def _lane_cumsum(x):
    lane = lax.broadcasted_iota(I32, x.shape, 1)
    shift = 1
    while shift < LANES:
        x = x + jnp.where(lane >= shift, pltpu.roll(x, shift, 1), 0)
        shift *= 2
    return x


def _plan_kernel(eid_ref, pos_ref, lrow_ref, te_ref, nv_ref, meta_ref, tcnt_ref, tstart_ref,
                 rank_scr, cnt_scr, start_scr, tcnt_scr, tpre_scr):
    phase = pl.program_id(0)
    i = pl.program_id(1)
    tm = eid_ref.shape[0]
    td = TM_DISPATCH
    nsub = tm // td
    lane = lax.broadcasted_iota(I32, (tm, LANES), 1)
    eid = eid_ref[...]
    f1 = eid[:, 0:1]
    f2 = eid[:, 1:2]
    sel1 = lane == f1
    sel2 = lane == f2

    @pl.when((phase == 0) & (i == 0))
    def _():
        cnt_scr[...] = jnp.zeros_like(cnt_scr)

    @pl.when(phase == 0)
    def _():
        onehot = (sel1 | sel2).astype(BF16)
        r = lax.broadcasted_iota(I32, (tm, tm), 0)
        c = lax.broadcasted_iota(I32, (tm, tm), 1)
        strict_lower = (r > c).astype(BF16)
        within = jnp.dot(strict_lower, onehot, preferred_element_type=F32)
        base = cnt_scr[0:1, :]
        before = within + base
        r1 = jnp.sum(jnp.where(sel1, before, 0.0), axis=-1, keepdims=True)
        r2 = jnp.sum(jnp.where(sel2, before, 0.0), axis=-1, keepdims=True)
        rank = jnp.where(lane == 0, r1, jnp.where(lane == 1, r2, 0.0)).astype(I32)
        rank_scr[pl.ds(pl.multiple_of(i * tm, tm), tm), :] = rank
        onehot_f = onehot.astype(F32)
        row = lax.broadcasted_iota(I32, (tm, LANES), 0)
        srow = lax.broadcasted_iota(I32, (SUBLANES, LANES), 0)
        counts = [jnp.sum(onehot_f[s * td:(s + 1) * td], axis=0, keepdims=True) for s in range(nsub)]
        cnt_mat = jnp.zeros((SUBLANES, LANES), F32)
        seen = jnp.zeros((1, LANES), F32)
        local_before = within
        for s in range(nsub):
            tile = i * nsub + s
            tcnt_scr[pl.ds(tile, 1), :] = counts[s]
            tpre_scr[pl.ds(tile, 1), :] = base + seen
            cnt_mat = jnp.where(srow == s, counts[s], cnt_mat)
            if s > 0:
                local_before = jnp.where(row >= s * td, within - seen, local_before)
            seen = seen + counts[s]
        cnt_i = cnt_mat.astype(I32)
        run_start = (_lane_cumsum(cnt_i) - cnt_i).astype(F32)
        local = local_before + run_start[0:1, :]
        for s in range(1, nsub):
            local = jnp.where(row >= s * td, local_before + run_start[s:s + 1, :], local)
        l1 = jnp.sum(jnp.where(sel1, local, 0.0), axis=-1, keepdims=True)
        l2 = jnp.sum(jnp.where(sel2, local, 0.0), axis=-1, keepdims=True)
        lrow_ref[...] = jnp.where(lane == 0, l1, jnp.where(lane == 1, l2, 0.0)).astype(I32)
        cnt_scr[...] = cnt_scr[...] + seen

    @pl.when(phase == 1)
    def _():
        starts = start_scr[0:1, :]
        o1 = jnp.sum(jnp.where(sel1, starts, 0.0), axis=-1, keepdims=True).astype(I32)
        o2 = jnp.sum(jnp.where(sel2, starts, 0.0), axis=-1, keepdims=True).astype(I32)
        rank = rank_scr[pl.ds(pl.multiple_of(i * tm, tm), tm), :]
        pos = rank + jnp.where(lane == 0, o1, jnp.where(lane == 1, o2, 0))
        pos_ref[...] = pos.astype(F32).T[0:SUBLANES, :].astype(I32)

    @pl.when((phase == 0) & (i == pl.num_programs(1) - 1))
    def _():
        shift = int(math.log2(TM_EXPERT))
        cnt = cnt_scr[...].astype(I32)
        padded = ((cnt + (TM_EXPERT - 1)) >> shift) << shift
        ends = _lane_cumsum(padded)
        start_scr[...] = (ends - padded).astype(F32)
        nt = te_ref.shape[0]
        end_tile = (ends >> shift)[0:1, :]
        tl_lane = lax.broadcasted_iota(I32, (nt, LANES), 1)
        tile = lax.broadcasted_iota(I32, (nt, LANES), 0)
        done = jnp.where((tl_lane < N_EXPERTS) & (end_tile <= tile), 1.0, 0.0)
        te = jnp.sum(done, axis=-1, keepdims=True).astype(I32)
        te_ref[...] = jnp.broadcast_to(jnp.minimum(te, N_EXPERTS - 1), (nt, LANES))
        n_valid = jnp.sum(jnp.where(tl_lane[0:1, :] == N_EXPERTS - 1, end_tile, 0).astype(F32),
                          axis=-1, keepdims=True).astype(I32)
        nv_ref[...] = jnp.broadcast_to(n_valid, nv_ref.shape)
        row = lax.broadcasted_iota(I32, cnt.shape, 0)
        meta_ref[...] = jnp.where(row == 0, ends - padded + cnt, jnp.where(row == 1, padded - cnt, 0))
        tcnt_ref[...] = tcnt_scr[...].astype(I32)
        tstart_ref[...] = (tpre_scr[...] + (ends - padded).astype(F32)[0:1, :]).astype(I32)


def _plan(eid, n_tiles):
    T = eid.shape[0]
    tm = TM_PLAN
    steps = T // tm
    n_dtiles = T // TM_DISPATCH
    const = lambda p, i: (0, 0)
    return pl.pallas_call(
        _plan_kernel,
        grid=(2, steps),
        in_specs=[pl.BlockSpec((tm, LANES), lambda p, i: (i, 0))],
        out_specs=[pl.BlockSpec((SUBLANES, tm), lambda p, i: (0, i * p)),
                   pl.BlockSpec((tm, LANES), lambda p, i: (i * (1 - p) + (steps - 1) * p, 0)),
                   pl.BlockSpec((n_tiles, LANES), const), pl.BlockSpec((SUBLANES, LANES), const),
                   pl.BlockSpec((SUBLANES, LANES), const),
                   pl.BlockSpec((n_dtiles, LANES), const), pl.BlockSpec((n_dtiles, LANES), const)],
        out_shape=[jax.ShapeDtypeStruct((SUBLANES, T), I32), jax.ShapeDtypeStruct((T, LANES), I32),
                   jax.ShapeDtypeStruct((n_tiles, LANES), I32),
                   jax.ShapeDtypeStruct((SUBLANES, LANES), I32),
                   jax.ShapeDtypeStruct((SUBLANES, LANES), I32),
                   jax.ShapeDtypeStruct((n_dtiles, LANES), I32),
                   jax.ShapeDtypeStruct((n_dtiles, LANES), I32)],
        scratch_shapes=[pltpu.VMEM((T, LANES), I32), pltpu.VMEM((SUBLANES, LANES), F32),
                        pltpu.VMEM((SUBLANES, LANES), F32),
                        pltpu.VMEM((n_dtiles, LANES), F32), pltpu.VMEM((n_dtiles, LANES), F32)],
        compiler_params=_cparams(("arbitrary", "arbitrary")),
        name="plan",
    )(eid)


_PAD_BITS = tuple(1 << b for b in reversed(range(int(math.log2(TM_EXPERT)))))


_RUN_BITS = tuple(1 << b for b in reversed(range(int(math.log2(TM_DISPATCH)) + 1)))


def _dispatch_kernel(tcnt_ref, tstart_ref, seg_end_ref, seg_pad_ref, nv_ref, lrow_ref, h_ref,
                     xs_ref, lbuf, zeros, sem, zsem):
    i = pl.program_id(0)
    n = pl.num_programs(0)
    td = h_ref.shape[0]
    nr = 2 * td
    slot = i % 2

    def wait_slot(s):
        pltpu.make_async_copy(lbuf.at[s], xs_ref.at[pl.ds(0, nr)], sem.at[s]).wait()

    @pl.when(i >= 2)
    def _():
        wait_slot(slot)

    local_row = lrow_ref[...].astype(F32).T
    r = lax.broadcasted_iota(I32, (nr, td), 0).astype(F32)
    onehot = ((r == local_row[0:1, :]) | (r == local_row[1:2, :])).astype(BF16)
    sorted_rows = jnp.dot(onehot, h_ref[...].reshape(td, -1), preferred_element_type=F32)
    lbuf[slot] = sorted_rows.astype(BF16).reshape(lbuf.shape[1:])

    def run(e, lo):
        cnt = tcnt_ref[i * N_EXPERTS + e]
        dst = tstart_ref[i * N_EXPERTS + e]
        for bit in _RUN_BITS:
            off = cnt & ~(2 * bit - 1)

            @pl.when((cnt & bit) != 0)
            def _():
                pltpu.make_async_copy(lbuf.at[slot, pl.ds(lo + off, bit)],
                                      xs_ref.at[pl.ds(dst + off, bit)], sem.at[slot]).start()
        return lo + cnt

    lax.fori_loop(0, N_EXPERTS, run, 0)

    @pl.when(i == n - 1)
    def _():
        wait_slot(slot)

        @pl.when(n >= 2)
        def _():
            wait_slot(1 - slot)

        zeros[...] = jnp.zeros_like(zeros)
        n_tiles = xs_ref.shape[0] // TM_EXPERT
        first_tail = n_tiles - N_EXPERTS

        def pad_copy(e, bit):
            n = seg_pad_ref[e]
            dst = seg_end_ref[e] + (n & ~(2 * bit - 1))
            return (n & bit) != 0, pltpu.make_async_copy(zeros.at[pl.ds(0, bit)],
                                                         xs_ref.at[pl.ds(dst, bit)], zsem)

        def tail_copy(j):
            return j >= nv_ref[0], pltpu.make_async_copy(
                zeros, xs_ref.at[pl.ds(j * TM_EXPERT, TM_EXPERT)], zsem)

        def for_each_fill(act):
            def per_expert(e, c):
                for bit in _PAD_BITS:
                    pred, cp = pad_copy(e, bit)
                    pl.when(pred)(functools.partial(act, cp))
                return c

            lax.fori_loop(0, N_EXPERTS, per_expert, 0)

            def per_tile(j, c):
                pred, cp = tail_copy(j)
                pl.when(pred)(functools.partial(act, cp))
                return c

            lax.fori_loop(first_tail, n_tiles, per_tile, 0)

        for_each_fill(lambda cp: cp.start())
        for_each_fill(lambda cp: cp.wait())


def _dispatch(tile_cnt, tile_start, seg_end, seg_pad, n_valid, lrow, h2, n_rows):
    T = h2.shape[0]
    slab = h2.shape[1:]
    td = TM_DISPATCH
    smem = pl.BlockSpec(memory_space=pltpu.SMEM)
    return pl.pallas_call(
        _dispatch_kernel,
        grid=(T // td,),
        in_specs=[smem, smem, smem, smem, smem,
                  pl.BlockSpec((td, LANES), lambda i: (i, 0)),
                  pl.BlockSpec((td,) + slab, lambda i: (i, 0, 0))],
        out_specs=pl.BlockSpec(memory_space=pl.ANY),
        out_shape=jax.ShapeDtypeStruct((n_rows,) + slab, h2.dtype),
        scratch_shapes=[pltpu.VMEM((2, 2 * td) + slab, h2.dtype),
                        pltpu.VMEM((TM_EXPERT,) + slab, h2.dtype),
                        pltpu.SemaphoreType.DMA((2,)), pltpu.SemaphoreType.DMA(())],
        compiler_params=_cparams(("arbitrary",)),
        name="dispatch",
    )(tile_cnt, tile_start, seg_end, seg_pad, n_valid, lrow, h2)


def _experts_kernel(te_ref, nv_ref, xs_ref, wg_ref, wu_ref, wd_ref, ys_ref, wg_b, wu_b, wd_b):
    j = pl.program_id(0)
    new_expert = (j == 0) | (te_ref[j] != te_ref[jnp.maximum(j - 1, 0)])

    @pl.when(new_expert)
    def _():
        wg_b[...] = wg_ref[0].astype(BF16)
        wu_b[...] = wu_ref[0].astype(BF16)
        wd_b[...] = wd_ref[0].astype(BF16)

    @pl.when(j < nv_ref[0])
    def _():
        x = xs_ref[...].reshape(xs_ref.shape[0], -1)
        y = None
        for lo in range(0, wg_b.shape[1], EXPERT_FF_BLOCK):
            cols = slice(lo, lo + EXPERT_FF_BLOCK)
            a = jnp.dot(x, wg_b[:, cols], preferred_element_type=F32)
            u = jnp.dot(x, wu_b[:, cols], preferred_element_type=F32)
            hid = (a * jax.nn.sigmoid(a) * u).astype(BF16)
            part = jnp.dot(hid, wd_b[cols, :], preferred_element_type=F32)
            y = part if y is None else y + part
        ys_ref[...] = y.astype(BF16).reshape(ys_ref.shape)

    @pl.when(j >= nv_ref[0])
    def _():
        ys_ref[...] = jnp.zeros_like(ys_ref)


def _experts(tile_expert, n_valid, xs, wg, wu, wd):
    R = xs.shape[0]
    slab = xs.shape[1:]
    E, D, Fh = wg.shape
    tm = TM_EXPERT
    n_tiles = R // tm
    rows = lambda j, te, nv: (j, 0, 0)
    wsel = lambda j, te, nv: (te[j], 0, 0)
    return pl.pallas_call(
        _experts_kernel,
        grid_spec=pltpu.PrefetchScalarGridSpec(
            num_scalar_prefetch=2,
            grid=(n_tiles,),
            in_specs=[pl.BlockSpec((tm,) + slab, rows),
                      pl.BlockSpec((1, D, Fh), wsel), pl.BlockSpec((1, D, Fh), wsel),
                      pl.BlockSpec((1, Fh, D), wsel)],
            out_specs=pl.BlockSpec((tm,) + slab, rows),
            scratch_shapes=[pltpu.VMEM((D, Fh), BF16), pltpu.VMEM((D, Fh), BF16),
                            pltpu.VMEM((Fh, D), BF16)]),
        out_shape=jax.ShapeDtypeStruct(xs.shape, xs.dtype),
        compiler_params=_cparams(("arbitrary",)),
        name="experts",
    )(tile_expert, n_valid, xs, wg, wu, wd)


def _combine_kernel(pos_ref, pos_next_ref, x1_ref, wt_ref, p_ref, gp_ref, wpg_ref, wpp_ref, gf_ref,
                    ys_ref, o_ref, buf, wpg_b, wpp_b, sem):
    i = pl.program_id(0)
    n = pl.num_programs(0)
    tm = x1_ref.shape[0]
    slot = i % 2

    def gather(idx_ref, s):
        def issue(t, c):
            for k in range(2):
                pltpu.make_async_copy(ys_ref.at[pl.ds(idx_ref[k, t], 1)],
                                      buf.at[s, k, pl.ds(t, 1)], sem.at[s]).start()
            return c

        lax.fori_loop(0, tm, issue, 0, unroll=8)

    @pl.when(i == 0)
    def _():
        gather(pos_ref, 0)
        wpg_b[...] = wpg_ref[...].astype(BF16)
        wpp_b[...] = wpp_ref[...].astype(BF16)

    @pl.when(i + 1 < n)
    def _():
        gather(pos_next_ref, 1 - slot)

    for k in range(2):
        pltpu.make_async_copy(ys_ref.at[pl.ds(0, tm)], buf.at[slot, k], sem.at[slot]).wait()
    wt = wt_ref[...]
    y0 = buf[slot, 0].reshape(tm, -1).astype(F32)
    y1 = buf[slot, 1].reshape(tm, -1).astype(F32)
    x2 = x1_ref[...] + wt[:, 0:1] * y0 + wt[:, 1:2] * y1
    hp = _rms(x2, gp_ref[...]).astype(BF16)
    gate = jax.nn.sigmoid(jnp.dot(hp, wpg_b[...], preferred_element_type=F32))
    proj = jnp.dot(p_ref[...].astype(BF16), wpp_b[...], preferred_element_type=F32)
    x3 = x2 + gate * proj
    o_ref[...] = _rms(x3, gf_ref[...])


def _combine(pos_t, x1, wts, p2, g_ple, w_ple_gate, w_ple_proj, g_final, ys):
    T, D = x1.shape
    P = p2.shape[1]
    tm = TM_COMBINE
    n = T // tm
    row = lambda i: (i, 0)
    const = lambda i: (0, 0)
    return pl.pallas_call(
        _combine_kernel,
        grid=(n,),
        in_specs=[pl.BlockSpec((SUBLANES, tm), lambda i: (0, i), memory_space=pltpu.SMEM),
                  pl.BlockSpec((SUBLANES, tm), lambda i: (0, jnp.minimum(i + 1, n - 1)),
                               memory_space=pltpu.SMEM),
                  pl.BlockSpec((tm, D), row), pl.BlockSpec((tm, LANES), row),
                  pl.BlockSpec((tm, P), row), pl.BlockSpec((1, D), const),
                  pl.BlockSpec((D, D), const, **_FETCH_ONCE), pl.BlockSpec((P, D), const, **_FETCH_ONCE),
                  pl.BlockSpec((1, D), const), pl.BlockSpec(memory_space=pl.ANY)],
        out_specs=pl.BlockSpec((tm, D), row),
        out_shape=jax.ShapeDtypeStruct((T, D), F32),
        scratch_shapes=[pltpu.VMEM((2, 2, tm) + ys.shape[1:], ys.dtype),
                        pltpu.VMEM((D, D), BF16), pltpu.VMEM((P, D), BF16),
                        pltpu.SemaphoreType.DMA((2,))],
        compiler_params=_cparams(("arbitrary",)),
        name="combine",
    )(pos_t, pos_t, x1, wts, p2, g_ple, w_ple_gate, w_ple_proj, g_final, ys)


def _block_diag(w):
    nb, d, _ = w.shape
    eye = jnp.eye(nb, dtype=w.dtype)
    return (eye[:, None, :, None] * w[:, :, None, :]).reshape(nb * d, nb * d)


def kernel(x, p, positions, g_mix, w_in, sinks, conv_w, conv_b, lru_wa, lru_ba, lru_wx, lru_bx,
           lru_lambda, g_attn_out, g_lru_out, w_out, g_ffn, w_router_group, b_router_group,
           w_router_expert, b_router_expert, w_expert_gate, w_expert_up, w_expert_down, g_ple,
           w_ple_gate, w_ple_proj, g_final):
    B, S, D = x.shape
    assert w_in.shape[0] == 1, "single-layer block only"
    T = B * S
    lru_w = conv_w.shape[-1]
    n_rows = 2 * T + N_EXPERTS * TM_EXPERT
    n_tiles = n_rows // TM_EXPERT
    pos3 = positions.reshape(T // TM_INPROJ, TM_INPROJ // LANES, LANES).astype(I32)
    x2 = x.reshape(T, D)
    q, k, v, lx, lg = _inproj(x2, pos3, g_mix[0][None], w_in[0], lru_w)
    mix_a = _attention(q, k, v, sinks[0], g_attn_out[0][None], B, S)
    mix_b = _lru(lx.reshape(B, S, lru_w), lg.reshape(B, S, lru_w), conv_w[0], conv_b[0][None],
                 _block_diag(lru_wa[0]).astype(BF16), lru_ba[0][None],
                 _block_diag(lru_wx[0]).astype(BF16), lru_bx[0][None],
                 lru_lambda[0][None], g_lru_out[0][None]).reshape(T, lru_w)
    n_router = N_GROUPS + N_EXPERTS
    w_router = jnp.pad(jnp.concatenate([w_router_group[0], w_router_expert[0]], axis=1),
                       ((0, 0), (0, LANES - n_router)))
    b_router = jnp.pad(jnp.concatenate([b_router_group[0], b_router_expert[0]]),
                       (0, LANES - n_router))[None]
    w_router_hi = w_router.astype(BF16)
    w_router_lo = (w_router - w_router_hi.astype(F32)).astype(BF16)
    x1, h2, eid, wts = _outproj(x2, mix_a, mix_b, w_out[0], g_ffn[0][None],
                                jnp.concatenate([w_router_hi, w_router_lo], axis=1), b_router)
    pos_t, lrow, tile_expert, n_valid, meta, tile_cnt, tile_start = _plan(eid, n_tiles)
    n_valid = n_valid[0, :1]
    xs = _dispatch(tile_cnt[:, :N_EXPERTS].reshape(-1), tile_start[:, :N_EXPERTS].reshape(-1),
                   meta[0], meta[1], n_valid, lrow, h2, n_rows)
    Fh = w_expert_gate.shape[-1]
    ys = _experts(tile_expert[:, 0], n_valid, xs,
                  w_expert_gate[0].reshape(N_EXPERTS, D, Fh),
                  w_expert_up[0].reshape(N_EXPERTS, D, Fh),
                  w_expert_down[0].reshape(N_EXPERTS, Fh, D))
    out = _combine(pos_t, x1, wts, p[0].reshape(T, -1), g_ple[0][None],
                   w_ple_gate[0], w_ple_proj[0], g_final[None], ys)
    return out.reshape(B, S, D)
```

```python
import functools
import math

import jax
import jax.numpy as jnp
from jax import lax
from jax.experimental import pallas as pl
from jax.experimental.pallas import tpu as pltpu

F32 = jnp.float32
BF16 = jnp.bfloat16
I32 = jnp.int32

EPS = 1e-6
N_HEADS = 8
N_KV_HEADS = 2
HEAD_DIM = 64
WINDOW = 128
ROPE_THETA = 10000.0
CONV_WIDTH = 4
LRU_C = 8.0
N_GROUPS = 4
EXPERTS_PER_GROUP = 8
N_EXPERTS = N_GROUPS * EXPERTS_PER_GROUP
NEG_INF = -1e30
LANES = 128
SUBLANES = 8

TM_INPROJ = 512
TM_PROJ = 512
TQ_ATTN = 512
TL_LRU = 256
TM_PLAN = 1024
TM_EXPERT = 512
EXPERT_FF_BLOCK = 256
TM_DISPATCH = 512
TM_COMBINE = 256
VMEM_LIMIT = 56 * 1024 * 1024


def _rms(x, g):
    ms = jnp.mean(x * x, axis=-1, keepdims=True)
    return x * lax.rsqrt(ms + EPS) * g


_FETCH_ONCE = dict(pipeline_mode=pl.Buffered(1))


def _cparams(sem):
    return pltpu.CompilerParams(dimension_semantics=sem, vmem_limit_bytes=VMEM_LIMIT)


def _inproj_kernel(x_ref, pos_ref, g_ref, w_ref, q_ref, k_ref, v_ref, lx_ref, lg_ref, w_b):
    @pl.when(pl.program_id(0) == 0)
    def _():
        w_b[...] = w_ref[...].astype(BF16)

    h = _rms(x_ref[...], g_ref[...]).astype(BF16)
    proj = jnp.dot(h, w_b[...], preferred_element_type=F32)
    lane = lax.broadcasted_iota(I32, (1, LANES), 1)
    fidx = (lax.broadcasted_iota(I32, (LANES, 1), 0) % (HEAD_DIM // 2)).astype(F32)
    inv_freq = jnp.exp(fidx * (-2.0 / HEAD_DIM * math.log(ROPE_THETA)))
    pos = pos_ref[0].astype(F32)
    ang = jnp.concatenate([(inv_freq * pos[c:c + 1, :]).T for c in range(pos.shape[0])], axis=0)
    cos = jnp.cos(ang)
    sin = jnp.sin(ang)
    first_half = (lane % HEAD_DIM) < (HEAD_DIM // 2)
    sin_signed = jnp.where(first_half, -sin, sin)

    def rope(t):
        partner = jnp.where(first_half, pltpu.roll(t, LANES - HEAD_DIM // 2, 1),
                            pltpu.roll(t, HEAD_DIM // 2, 1))
        return t * cos + partner * sin_signed

    attn_w = N_HEADS * HEAD_DIM
    scale = HEAD_DIM ** -0.5
    assert math.frexp(scale)[0] == 0.5, "score scale must be a power of two to fold into q exactly"
    for c in range(attn_w // LANES):
        q_ref[:, c * LANES:(c + 1) * LANES] = (
            rope(proj[:, c * LANES:(c + 1) * LANES]) * scale).astype(BF16)
    kv_w = N_KV_HEADS * HEAD_DIM
    assert kv_w == LANES
    k = rope(proj[:, attn_w:attn_w + kv_w])
    v = proj[:, attn_w + kv_w:attn_w + 2 * kv_w]
    k_ref[:, :kv_w] = k.astype(BF16)
    k_ref[:, kv_w:] = pltpu.roll(k, HEAD_DIM, 1).astype(BF16)
    v_ref[:, :kv_w] = v.astype(BF16)
    v_ref[:, kv_w:] = pltpu.roll(v, HEAD_DIM, 1).astype(BF16)
    off_lx = attn_w + 2 * kv_w
    lru_w = lx_ref.shape[1]
    lx_ref[...] = proj[:, off_lx:off_lx + lru_w]
    lg_ref[...] = proj[:, off_lx + lru_w:off_lx + 2 * lru_w]


def _inproj(x2, pos3, g_mix, w_in, lru_w):
    T, D = x2.shape
    tm = TM_INPROJ
    attn_w = N_HEADS * HEAD_DIM
    kv_w = N_KV_HEADS * HEAD_DIM
    row = lambda i: (i, 0)
    const = lambda i: (0, 0)
    return pl.pallas_call(
        _inproj_kernel,
        grid=(T // tm,),
        in_specs=[pl.BlockSpec((tm, D), row),
                  pl.BlockSpec((1, tm // LANES, LANES), lambda i: (i, 0, 0)),
                  pl.BlockSpec((1, D), const), pl.BlockSpec(w_in.shape, const, **_FETCH_ONCE)],
        out_specs=[pl.BlockSpec((tm, attn_w), row), pl.BlockSpec((tm, 2 * kv_w), row),
                   pl.BlockSpec((tm, 2 * kv_w), row), pl.BlockSpec((tm, lru_w), row),
                   pl.BlockSpec((tm, lru_w), row)],
        out_shape=[jax.ShapeDtypeStruct((T, attn_w), BF16),
                   jax.ShapeDtypeStruct((T, 2 * kv_w), BF16),
                   jax.ShapeDtypeStruct((T, 2 * kv_w), BF16), jax.ShapeDtypeStruct((T, lru_w), F32),
                   jax.ShapeDtypeStruct((T, lru_w), F32)],
        scratch_shapes=[pltpu.VMEM(w_in.shape, BF16)],
        compiler_params=_cparams(("arbitrary",)),
        name="inproj",
    )(x2, pos3, g_mix, w_in)


def _attn_kernel(sinks_ref, q_ref, k_ref, v_ref, kp_ref, vp_ref, g_ref, o_ref):
    n = pl.program_id(1)
    blk = WINDOW
    hd = HEAD_DIM
    nqb = q_ref.shape[0] // blk
    n_pairs = N_HEADS // 2
    pairs_per_group = n_pairs // N_KV_HEADS
    lane = lax.broadcasted_iota(I32, (1, LANES), 1)
    first = lane < hd
    qi = lax.broadcasted_iota(I32, (blk, 2 * blk), 0)
    kj = lax.broadcasted_iota(I32, (blk, 2 * blk), 1)
    rel = qi + blk - kj
    band = (rel >= 0) & (rel < WINDOW)
    bd_row = lax.broadcasted_iota(I32, (4 * blk, LANES), 0)
    bd_lane = lax.broadcasted_iota(I32, (4 * blk, LANES), 1)
    ones_bd = ((bd_row < 2 * blk) == (bd_lane < hd)).astype(BF16)
    zero = jnp.zeros((), BF16)

    def block_diag(t):
        same, swapped = t[:, :LANES], t[:, LANES:]
        out = []
        for g in range(N_KV_HEADS):
            top = jnp.where(first, same if g == 0 else swapped, zero)
            bot = jnp.where(first, zero, swapped if g == 0 else same)
            out.append(jnp.concatenate([top, bot], axis=0))
        return out

    for j in range(nqb):
        rows = slice(j * blk, (j + 1) * blk)
        if j == 0:
            k_prev, v_prev = kp_ref[...], vp_ref[...]
            mask = band & ((kj >= blk) | (n > 0))
        else:
            prev_rows = slice((j - 1) * blk, j * blk)
            k_prev, v_prev = k_ref[prev_rows, :], v_ref[prev_rows, :]
            mask = band
        k_bd = block_diag(jnp.concatenate([k_prev, k_ref[rows, :]], axis=0))
        v_bd = block_diag(jnp.concatenate([v_prev, v_ref[rows, :]], axis=0))
        outs = []
        for p in range(n_pairs):
            g = p // pairs_per_group
            qp = q_ref[rows, p * LANES:(p + 1) * LANES]
            s = lax.dot_general(qp, k_bd[g], (((1,), (1,)), ((), ())), preferred_element_type=F32)
            es, ms = [], []
            for a in range(2):
                sa = jnp.where(mask, s[:, a * 2 * blk:(a + 1) * 2 * blk], NEG_INF)
                m = jnp.maximum(jnp.max(sa, axis=-1, keepdims=True), sinks_ref[2 * p + a])
                es.append(jnp.exp(sa - m))
                ms.append(m)
            e = jnp.concatenate(es, axis=1).astype(BF16)
            od = jnp.dot(e, jnp.concatenate([v_bd[g], ones_bd], axis=1), preferred_element_type=F32)
            sink_term = jnp.where(first, jnp.exp(sinks_ref[2 * p] - ms[0]),
                                  jnp.exp(sinks_ref[2 * p + 1] - ms[1]))
            outs.append(od[:, :LANES] / (od[:, LANES:] + sink_term))
        attn = jnp.concatenate(outs, axis=1)
        o_ref[rows, :] = _rms(attn, g_ref[...]).astype(BF16)


def _attention(q, k, v, sinks, g_attn, B, S):
    T, attn_w = q.shape
    kv_w = k.shape[1]
    blk = WINDOW
    tq = TQ_ATTN
    nt = S // tq
    cur = lambda b, n: (b * nt + n, 0)
    prev = lambda b, n: (b * (S // blk) + jnp.maximum(n * (tq // blk) - 1, 0), 0)
    return pl.pallas_call(
        _attn_kernel,
        grid=(B, nt),
        in_specs=[pl.BlockSpec(memory_space=pltpu.SMEM),
                  pl.BlockSpec((tq, attn_w), cur),
                  pl.BlockSpec((tq, kv_w), cur), pl.BlockSpec((tq, kv_w), cur),
                  pl.BlockSpec((blk, kv_w), prev), pl.BlockSpec((blk, kv_w), prev),
                  pl.BlockSpec((1, attn_w), lambda b, n: (0, 0))],
        out_specs=pl.BlockSpec((tq, attn_w), cur),
        out_shape=jax.ShapeDtypeStruct((T, attn_w), BF16),
        compiler_params=_cparams(("arbitrary", "arbitrary")),
        name="attn",
    )(sinks, q, k, v, k, v, g_attn)


def _neg_expm1(x, exp_x):
    series = -x * (1.0 + x * (0.5 + x * (1.0 / 6.0 + x * (1.0 / 24.0))))
    return jnp.where(x > -0.02, series, 1.0 - exp_x)


def _gelu_tanh(x):
    return 0.5 * x * (1.0 + jnp.tanh(math.sqrt(2.0 / math.pi) * (x + 0.044715 * (x * x * x))))


def _lru_kernel(lx_ref, lg_ref, cw_ref, cb_ref, wa_ref, ba_ref, wx_ref, bx_ref, lam_ref, g_ref,
                o_ref, xpad, a_scr, b_scr, h_scr, h_carry):
    i = pl.program_id(0)
    B, tl, C = lx_ref.shape
    pad = SUBLANES

    @pl.when(i == 0)
    def _():
        xpad[:, 0:pad, :] = jnp.zeros((B, pad, C), F32)
        h_carry[...] = jnp.zeros_like(h_carry)

    @pl.when(i > 0)
    def _():
        xpad[:, 0:pad, :] = xpad[:, tl:tl + pad, :]

    xpad[:, pad:, :] = lx_ref[...]
    y = jnp.zeros((B, tl, C), F32) + cb_ref[...]
    for j in range(CONV_WIDTH):
        y = y + xpad[:, pl.ds(pad - (CONV_WIDTH - 1) + j, tl), :] * cw_ref[j:j + 1, :]
    y2 = y.reshape(B * tl, C)
    yb = y2.astype(BF16)
    gate_a = jnp.dot(yb, wa_ref[...], preferred_element_type=F32) + ba_ref[...]
    gate_x = jnp.dot(yb, wx_ref[...], preferred_element_type=F32) + bx_ref[...]
    r = jax.nn.sigmoid(gate_a)
    ig = jax.nn.sigmoid(gate_x)
    nl = -lam_ref[...]
    softplus = jnp.maximum(nl, 0.0) + jnp.log1p(jnp.exp(-jnp.abs(nl)))
    log_a = (-LRU_C) * r * softplus
    a_all = jnp.exp(log_a)
    one_minus_a2 = _neg_expm1(2.0 * log_a, a_all * a_all)
    b_all = one_minus_a2 * lax.rsqrt(jnp.maximum(one_minus_a2, 1e-30)) * ig * y2
    groups = (B * tl // SUBLANES, SUBLANES, C)
    sub = lax.broadcasted_iota(I32, groups, 1)
    a_cum, b_loc = a_all.reshape(groups), b_all.reshape(groups)
    d = 1
    while d < SUBLANES:
        keep = sub >= d
        a_prev = pltpu.roll(a_cum, d, 1)
        b_prev = pltpu.roll(b_loc, d, 1)
        b_loc = b_loc + jnp.where(keep, a_cum * b_prev, 0.0)
        a_cum = jnp.where(keep, a_cum * a_prev, a_cum)
        d *= 2
    a_scr[...] = a_cum.reshape(B, tl, C)
    b_scr[...] = b_loc.reshape(B, tl, C)

    def carry(g, h_prev):
        rows = pl.ds(pl.multiple_of(g * SUBLANES, SUBLANES), SUBLANES)
        h = b_scr[:, rows, :] + a_scr[:, rows, :] * h_prev
        h_scr[:, rows, :] = h
        return h[:, SUBLANES - 1:SUBLANES, :]

    h_carry[...] = lax.fori_loop(0, tl // SUBLANES, carry, h_carry[...], unroll=4)
    lru = h_scr[...].reshape(B * tl, C) * _gelu_tanh(lg_ref[...].reshape(B * tl, C))
    o_ref[...] = _rms(lru, g_ref[...]).astype(BF16).reshape(B, tl, C)


def _lru(lx3, lg3, conv_w, conv_b, wa_bd, ba, wx_bd, bx, lam, g_lru):
    B, S, C = lx3.shape
    tl = TL_LRU
    blk = lambda i: (0, i, 0)
    const = lambda i: (0, 0)
    vec = pl.BlockSpec((1, C), const)
    return pl.pallas_call(
        _lru_kernel,
        grid=(S // tl,),
        in_specs=[pl.BlockSpec((B, tl, C), blk), pl.BlockSpec((B, tl, C), blk),
                  pl.BlockSpec((CONV_WIDTH, C), const), vec,
                  pl.BlockSpec((C, C), const), vec, pl.BlockSpec((C, C), const), vec, vec, vec],
        out_specs=pl.BlockSpec((B, tl, C), blk),
        out_shape=jax.ShapeDtypeStruct((B, S, C), BF16),
        scratch_shapes=[pltpu.VMEM((B, tl + SUBLANES, C), F32),
                        pltpu.VMEM((B, tl, C), F32), pltpu.VMEM((B, tl, C), F32),
                        pltpu.VMEM((B, tl, C), F32), pltpu.VMEM((B, 1, C), F32)],
        compiler_params=_cparams(("arbitrary",)),
        name="lru",
    )(lx3, lg3, conv_w, conv_b, wa_bd, ba, wx_bd, bx, lam, g_lru)


def _outproj_kernel(x_ref, ma_ref, mb_ref, wo_ref, g_ref, wr_ref, br_ref,
                    x1_ref, h2_ref, eid_ref, wt_ref, wo_b):
    @pl.when(pl.program_id(0) == 0)
    def _():
        wo_b[...] = wo_ref[...].astype(BF16)

    half = ma_ref.shape[1]
    x1 = (x_ref[...]
          + jnp.dot(ma_ref[...], wo_b[0:half, :], preferred_element_type=F32)
          + jnp.dot(mb_ref[...], wo_b[half:, :], preferred_element_type=F32))
    x1_ref[...] = x1
    h2 = _rms(x1, g_ref[...])
    h2_ref[...] = h2.astype(BF16).reshape(h2_ref.shape)
    h_hi = h2.astype(BF16)
    h_lo = (h2 - h_hi.astype(F32)).astype(BF16)
    hw = jnp.dot(h_hi, wr_ref[...], preferred_element_type=F32)
    logits = (hw[:, :LANES] + hw[:, LANES:]
              + jnp.dot(h_lo, wr_ref[:, :LANES], preferred_element_type=F32) + br_ref[...])
    tm = logits.shape[0]
    lane = lax.broadcasted_iota(I32, (tm, LANES), 1)
    lane_f = lane.astype(F32)
    big = float(LANES)

    def first_argmax(vals):
        m = jnp.max(vals, axis=-1, keepdims=True)
        idx = jnp.min(jnp.where(vals == m, lane_f, big), axis=-1, keepdims=True)
        return m, idx.astype(I32)

    gl = jnp.where(lane < N_GROUPS, logits, NEG_INF)
    gmax, gidx = first_argmax(gl)
    gsum = jnp.sum(jnp.where(lane < N_GROUPS, jnp.exp(gl - gmax), 0.0), axis=-1, keepdims=True)
    g_top_p = 1.0 / gsum
    lo = N_GROUPS + EXPERTS_PER_GROUP * gidx
    el = jnp.where((lane >= lo) & (lane < lo + EXPERTS_PER_GROUP), logits, NEG_INF)
    m1, i1 = first_argmax(el)
    el2 = jnp.where(lane == i1, NEG_INF, el)
    m2, i2 = first_argmax(el2)
    ratio = jnp.exp(m2 - m1)
    w1 = g_top_p / (1.0 + ratio)
    w2 = g_top_p * ratio / (1.0 + ratio)
    eid_ref[...] = jnp.where(lane == 0, i1 - N_GROUPS, jnp.where(lane == 1, i2 - N_GROUPS, 0))
    wt_ref[...] = jnp.where(lane == 0, w1, jnp.where(lane == 1, w2, 0.0))


def _outproj(x2, mix_a, mix_b, w_out, g_ffn, w_router, b_router):
    T, D = x2.shape
    tm = TM_PROJ
    half = mix_a.shape[1]
    row = lambda i: (i, 0)
    const = lambda i: (0, 0)
    return pl.pallas_call(
        _outproj_kernel,
        grid=(T // tm,),
        in_specs=[pl.BlockSpec((tm, D), row), pl.BlockSpec((tm, half), row),
                  pl.BlockSpec((tm, half), row), pl.BlockSpec(w_out.shape, const, **_FETCH_ONCE),
                  pl.BlockSpec((1, D), const), pl.BlockSpec((D, 2 * LANES), const),
                  pl.BlockSpec((1, LANES), const)],
        out_specs=[pl.BlockSpec((tm, D), row),
                   pl.BlockSpec((tm, D // LANES, LANES), lambda i: (i, 0, 0)),
                   pl.BlockSpec((tm, LANES), row), pl.BlockSpec((tm, LANES), row)],
        out_shape=[jax.ShapeDtypeStruct((T, D), F32),
                   jax.ShapeDtypeStruct((T, D // LANES, LANES), BF16),
                   jax.ShapeDtypeStruct((T, LANES), I32), jax.ShapeDtypeStruct((T, LANES), F32)],
        scratch_shapes=[pltpu.VMEM(w_out.shape, BF16)],
        compiler_params=_cparams(("arbitrary",)),
        name="outproj",
    )(x2, mix_a, mix_b, w_out, g_ffn, w_router, b_router)


def _lane_cumsum(x):
    lane = lax.broadcasted_iota(I32, x.shape, 1)
    shift = 1
    while shift < LANES:
        x = x + jnp.where(lane >= shift, pltpu.roll(x, shift, 1), 0)
        shift *= 2
    return x


def _plan_kernel(eid_ref, pos_ref, lrow_ref, te_ref, nv_ref, meta_ref, tcnt_ref, tstart_ref,
                 rank_scr, cnt_scr, start_scr, tcnt_scr, tpre_scr):
    phase = pl.program_id(0)
    i = pl.program_id(1)
    tm = eid_ref.shape[0]
    td = TM_DISPATCH
    nsub = tm // td
    lane = lax.broadcasted_iota(I32, (tm, LANES), 1)
    eid = eid_ref[...]
    f1 = eid[:, 0:1]
    f2 = eid[:, 1:2]
    sel1 = lane == f1
    sel2 = lane == f2

    @pl.when((phase == 0) & (i == 0))
    def _():
        cnt_scr[...] = jnp.zeros_like(cnt_scr)

    @pl.when(phase == 0)
    def _():
        onehot = (sel1 | sel2).astype(BF16)
        r = lax.broadcasted_iota(I32, (tm, tm), 0)
        c = lax.broadcasted_iota(I32, (tm, tm), 1)
        strict_lower = (r > c).astype(BF16)
        within = jnp.dot(strict_lower, onehot, preferred_element_type=F32)
        base = cnt_scr[0:1, :]
        before = within + base
        r1 = jnp.sum(jnp.where(sel1, before, 0.0), axis=-1, keepdims=True)
        r2 = jnp.sum(jnp.where(sel2, before, 0.0), axis=-1, keepdims=True)
        rank = jnp.where(lane == 0, r1, jnp.where(lane == 1, r2, 0.0)).astype(I32)
        rank_scr[pl.ds(pl.multiple_of(i * tm, tm), tm), :] = rank
        onehot_f = onehot.astype(F32)
        row = lax.broadcasted_iota(I32, (tm, LANES), 0)
        srow = lax.broadcasted_iota(I32, (SUBLANES, LANES), 0)
        counts = [jnp.sum(onehot_f[s * td:(s + 1) * td], axis=0, keepdims=True) for s in range(nsub)]
        cnt_mat = jnp.zeros((SUBLANES, LANES), F32)
        seen = jnp.zeros((1, LANES), F32)
        local_before = within
        for s in range(nsub):
            tile = i * nsub + s
            tcnt_scr[pl.ds(tile, 1), :] = counts[s]
            tpre_scr[pl.ds(tile, 1), :] = base + seen
            cnt_mat = jnp.where(srow == s, counts[s], cnt_mat)
            if s > 0:
                local_before = jnp.where(row >= s * td, within - seen, local_before)
            seen = seen + counts[s]
        cnt_i = cnt_mat.astype(I32)
        run_start = (_lane_cumsum(cnt_i) - cnt_i).astype(F32)
        local = local_before + run_start[0:1, :]
        for s in range(1, nsub):
            local = jnp.where(row >= s * td, local_before + run_start[s:s + 1, :], local)
        l1 = jnp.sum(jnp.where(sel1, local, 0.0), axis=-1, keepdims=True)
        l2 = jnp.sum(jnp.where(sel2, local, 0.0), axis=-1, keepdims=True)
        lrow_ref[...] = jnp.where(lane == 0, l1, jnp.where(lane == 1, l2, 0.0)).astype(I32)
        cnt_scr[...] = cnt_scr[...] + seen

    @pl.when(phase == 1)
    def _():
        starts = start_scr[0:1, :]
        o1 = jnp.sum(jnp.where(sel1, starts, 0.0), axis=-1, keepdims=True).astype(I32)
        o2 = jnp.sum(jnp.where(sel2, starts, 0.0), axis=-1, keepdims=True).astype(I32)
        rank = rank_scr[pl.ds(pl.multiple_of(i * tm, tm), tm), :]
        pos = rank + jnp.where(lane == 0, o1, jnp.where(lane == 1, o2, 0))
        pos_ref[...] = pos.astype(F32).T[0:SUBLANES, :].astype(I32)

    @pl.when((phase == 0) & (i == pl.num_programs(1) - 1))
    def _():
        shift = int(math.log2(TM_EXPERT))
        cnt = cnt_scr[...].astype(I32)
        padded = ((cnt + (TM_EXPERT - 1)) >> shift) << shift
        ends = _lane_cumsum(padded)
        start_scr[...] = (ends - padded).astype(F32)
        nt = te_ref.shape[0]
        end_tile = (ends >> shift)[0:1, :]
        tl_lane = lax.broadcasted_iota(I32, (nt, LANES), 1)
        tile = lax.broadcasted_iota(I32, (nt, LANES), 0)
        done = jnp.where((tl_lane < N_EXPERTS) & (end_tile <= tile), 1.0, 0.0)
        te = jnp.sum(done, axis=-1, keepdims=True).astype(I32)
        te_ref[...] = jnp.broadcast_to(jnp.minimum(te, N_EXPERTS - 1), (nt, LANES))
        n_valid = jnp.sum(jnp.where(tl_lane[0:1, :] == N_EXPERTS - 1, end_tile, 0).astype(F32),
                          axis=-1, keepdims=True).astype(I32)
        nv_ref[...] = jnp.broadcast_to(n_valid, nv_ref.shape)
        row = lax.broadcasted_iota(I32, cnt.shape, 0)
        meta_ref[...] = jnp.where(row == 0, ends - padded + cnt, jnp.where(row == 1, padded - cnt, 0))
        tcnt_ref[...] = tcnt_scr[...].astype(I32)
        tstart_ref[...] = (tpre_scr[...] + (ends - padded).astype(F32)[0:1, :]).astype(I32)


def _plan(eid, n_tiles):
    T = eid.shape[0]
    tm = TM_PLAN
    steps = T // tm
    n_dtiles = T // TM_DISPATCH
    const = lambda p, i: (0, 0)
    return pl.pallas_call(
        _plan_kernel,
        grid=(2, steps),
        in_specs=[pl.BlockSpec((tm, LANES), lambda p, i: (i, 0))],
        out_specs=[pl.BlockSpec((SUBLANES, tm), lambda p, i: (0, i * p)),
                   pl.BlockSpec((tm, LANES), lambda p, i: (i * (1 - p) + (steps - 1) * p, 0)),
                   pl.BlockSpec((n_tiles, LANES), const), pl.BlockSpec((SUBLANES, LANES), const),
                   pl.BlockSpec((SUBLANES, LANES), const),
                   pl.BlockSpec((n_dtiles, LANES), const), pl.BlockSpec((n_dtiles, LANES), const)],
        out_shape=[jax.ShapeDtypeStruct((SUBLANES, T), I32), jax.ShapeDtypeStruct((T, LANES), I32),
                   jax.ShapeDtypeStruct((n_tiles, LANES), I32),
                   jax.ShapeDtypeStruct((SUBLANES, LANES), I32),
                   jax.ShapeDtypeStruct((SUBLANES, LANES), I32),
                   jax.ShapeDtypeStruct((n_dtiles, LANES), I32),
                   jax.ShapeDtypeStruct((n_dtiles, LANES), I32)],
        scratch_shapes=[pltpu.VMEM((T, LANES), I32), pltpu.VMEM((SUBLANES, LANES), F32),
                        pltpu.VMEM((SUBLANES, LANES), F32),
                        pltpu.VMEM((n_dtiles, LANES), F32), pltpu.VMEM((n_dtiles, LANES), F32)],
        compiler_params=_cparams(("arbitrary", "arbitrary")),
        name="plan",
    )(eid)


_PAD_BITS = tuple(1 << b for b in reversed(range(int(math.log2(TM_EXPERT)))))


_RUN_BITS = tuple(1 << b for b in reversed(range(int(math.log2(TM_DISPATCH)) + 1)))


def _dispatch_kernel(tcnt_ref, tstart_ref, seg_end_ref, seg_pad_ref, nv_ref, lrow_ref, h_ref,
                     xs_ref, lbuf, zeros, sem, zsem):
    i = pl.program_id(0)
    n = pl.num_programs(0)
    td = h_ref.shape[0]
    nr = 2 * td
    slot = i % 2

    def wait_slot(s):
        pltpu.make_async_copy(lbuf.at[s], xs_ref.at[pl.ds(0, nr)], sem.at[s]).wait()

    @pl.when(i >= 2)
    def _():
        wait_slot(slot)

    local_row = lrow_ref[...].astype(F32).T
    r = lax.broadcasted_iota(I32, (nr, td), 0).astype(F32)
    onehot = ((r == local_row[0:1, :]) | (r == local_row[1:2, :])).astype(BF16)
    sorted_rows = jnp.dot(onehot, h_ref[...].reshape(td, -1), preferred_element_type=F32)
    lbuf[slot] = sorted_rows.astype(BF16).reshape(lbuf.shape[1:])

    def run(e, lo):
        cnt = tcnt_ref[i * N_EXPERTS + e]
        dst = tstart_ref[i * N_EXPERTS + e]
        for bit in _RUN_BITS:
            off = cnt & ~(2 * bit - 1)

            @pl.when((cnt & bit) != 0)
            def _():
                pltpu.make_async_copy(lbuf.at[slot, pl.ds(lo + off, bit)],
                                      xs_ref.at[pl.ds(dst + off, bit)], sem.at[slot]).start()
        return lo + cnt

    lax.fori_loop(0, N_EXPERTS, run, 0)

    @pl.when(i == n - 1)
    def _():
        wait_slot(slot)

        @pl.when(n >= 2)
        def _():
            wait_slot(1 - slot)

        zeros[...] = jnp.zeros_like(zeros)
        n_tiles = xs_ref.shape[0] // TM_EXPERT
        first_tail = n_tiles - N_EXPERTS

        def pad_copy(e, bit):
            n = seg_pad_ref[e]
            dst = seg_end_ref[e] + (n & ~(2 * bit - 1))
            return (n & bit) != 0, pltpu.make_async_copy(zeros.at[pl.ds(0, bit)],
                                                         xs_ref.at[pl.ds(dst, bit)], zsem)

        def tail_copy(j):
            return j >= nv_ref[0], pltpu.make_async_copy(
                zeros, xs_ref.at[pl.ds(j * TM_EXPERT, TM_EXPERT)], zsem)

        def for_each_fill(act):
            def per_expert(e, c):
                for bit in _PAD_BITS:
                    pred, cp = pad_copy(e, bit)
                    pl.when(pred)(functools.partial(act, cp))
                return c

            lax.fori_loop(0, N_EXPERTS, per_expert, 0)

            def per_tile(j, c):
                pred, cp = tail_copy(j)
                pl.when(pred)(functools.partial(act, cp))
                return c

            lax.fori_loop(first_tail, n_tiles, per_tile, 0)

        for_each_fill(lambda cp: cp.start())
        for_each_fill(lambda cp: cp.wait())


def _dispatch(tile_cnt, tile_start, seg_end, seg_pad, n_valid, lrow, h2, n_rows):
    T = h2.shape[0]
    slab = h2.shape[1:]
    td = TM_DISPATCH
    smem = pl.BlockSpec(memory_space=pltpu.SMEM)
    return pl.pallas_call(
        _dispatch_kernel,
        grid=(T // td,),
        in_specs=[smem, smem, smem, smem, smem,
                  pl.BlockSpec((td, LANES), lambda i: (i, 0)),
                  pl.BlockSpec((td,) + slab, lambda i: (i, 0, 0))],
        out_specs=pl.BlockSpec(memory_space=pl.ANY),
        out_shape=jax.ShapeDtypeStruct((n_rows,) + slab, h2.dtype),
        scratch_shapes=[pltpu.VMEM((2, 2 * td) + slab, h2.dtype),
                        pltpu.VMEM((TM_EXPERT,) + slab, h2.dtype),
                        pltpu.SemaphoreType.DMA((2,)), pltpu.SemaphoreType.DMA(())],
        compiler_params=_cparams(("arbitrary",)),
        name="dispatch",
    )(tile_cnt, tile_start, seg_end, seg_pad, n_valid, lrow, h2)


def _experts_kernel(te_ref, nv_ref, xs_ref, wg_ref, wu_ref, wd_ref, ys_ref, wg_b, wu_b, wd_b):
    j = pl.program_id(0)
    new_expert = (j == 0) | (te_ref[j] != te_ref[jnp.maximum(j - 1, 0)])

    @pl.when(new_expert)
    def _():
        wg_b[...] = wg_ref[0].astype(BF16)
        wu_b[...] = wu_ref[0].astype(BF16)
        wd_b[...] = wd_ref[0].astype(BF16)

    @pl.when(j < nv_ref[0])
    def _():
        x = xs_ref[...].reshape(xs_ref.shape[0], -1)
        y = None
        for lo in range(0, wg_b.shape[1], EXPERT_FF_BLOCK):
            cols = slice(lo, lo + EXPERT_FF_BLOCK)
            a = jnp.dot(x, wg_b[:, cols], preferred_element_type=F32)
            u = jnp.dot(x, wu_b[:, cols], preferred_element_type=F32)
            hid = (a * jax.nn.sigmoid(a) * u).astype(BF16)
            part = jnp.dot(hid, wd_b[cols, :], preferred_element_type=F32)
            y = part if y is None else y + part
        ys_ref[...] = y.astype(BF16).reshape(ys_ref.shape)

    @pl.when(j >= nv_ref[0])
    def _():
        ys_ref[...] = jnp.zeros_like(ys_ref)


def _experts(tile_expert, n_valid, xs, wg, wu, wd):
    R = xs.shape[0]
    slab = xs.shape[1:]
    E, D, Fh = wg.shape
    tm = TM_EXPERT
    n_tiles = R // tm
    rows = lambda j, te, nv: (j, 0, 0)
    wsel = lambda j, te, nv: (te[j], 0, 0)
    return pl.pallas_call(
        _experts_kernel,
        grid_spec=pltpu.PrefetchScalarGridSpec(
            num_scalar_prefetch=2,
            grid=(n_tiles,),
            in_specs=[pl.BlockSpec((tm,) + slab, rows),
                      pl.BlockSpec((1, D, Fh), wsel), pl.BlockSpec((1, D, Fh), wsel),
                      pl.BlockSpec((1, Fh, D), wsel)],
            out_specs=pl.BlockSpec((tm,) + slab, rows),
            scratch_shapes=[pltpu.VMEM((D, Fh), BF16), pltpu.VMEM((D, Fh), BF16),
                            pltpu.VMEM((Fh, D), BF16)]),
        out_shape=jax.ShapeDtypeStruct(xs.shape, xs.dtype),
        compiler_params=_cparams(("arbitrary",)),
        name="experts",
    )(tile_expert, n_valid, xs, wg, wu, wd)


def _combine_kernel(pos0_ref, pos1_ref, pos0_next_ref, pos1_next_ref, x1_ref, wt_ref, p_ref, gp_ref,
                    wpg_ref, wpp_ref, gf_ref, ys_ref, o_ref, buf, wpg_b, wpp_b, sem):
    i = pl.program_id(0)
    n = pl.num_programs(0)
    tm = x1_ref.shape[0]
    slot = i % 2

    def gather(idx_refs, s):
        def issue(t, c):
            for k in range(2):
                pltpu.make_async_copy(ys_ref.at[pl.ds(idx_refs[k][t], 1)],
                                      buf.at[s, k, pl.ds(t, 1)], sem.at[s]).start()
            return c

        lax.fori_loop(0, tm, issue, 0, unroll=8)

    @pl.when(i == 0)
    def _():
        gather((pos0_ref, pos1_ref), 0)
        wpg_b[...] = wpg_ref[...].astype(BF16)
        wpp_b[...] = wpp_ref[...].astype(BF16)

    for s in range(2):
        @pl.when((i + 1 < n) & (slot == 1 - s))
        def _():
            gather((pos0_next_ref, pos1_next_ref), s)

    for k in range(2):
        pltpu.make_async_copy(ys_ref.at[pl.ds(0, tm)], buf.at[slot, k], sem.at[slot]).wait()
    wt = wt_ref[...]
    y0 = buf[slot, 0].reshape(tm, -1).astype(F32)
    y1 = buf[slot, 1].reshape(tm, -1).astype(F32)
    x2 = x1_ref[...] + wt[:, 0:1] * y0 + wt[:, 1:2] * y1
    hp = _rms(x2, gp_ref[...]).astype(BF16)
    gate = jax.nn.sigmoid(jnp.dot(hp, wpg_b[...], preferred_element_type=F32))
    proj = jnp.dot(p_ref[...].astype(BF16), wpp_b[...], preferred_element_type=F32)
    x3 = x2 + gate * proj
    o_ref[...] = _rms(x3, gf_ref[...])


def _combine(pos_t, x1, wts, p2, g_ple, w_ple_gate, w_ple_proj, g_final, ys):
    T, D = x1.shape
    P = p2.shape[1]
    tm = TM_COMBINE
    n = T // tm
    row = lambda i: (i, 0)
    const = lambda i: (0, 0)
    cur = pl.BlockSpec((tm,), lambda i: (i,), memory_space=pltpu.SMEM)
    nxt = pl.BlockSpec((tm,), lambda i: (jnp.minimum(i + 1, n - 1),), memory_space=pltpu.SMEM)
    return pl.pallas_call(
        _combine_kernel,
        grid=(n,),
        in_specs=[cur, cur, nxt, nxt,
                  pl.BlockSpec((tm, D), row), pl.BlockSpec((tm, LANES), row),
                  pl.BlockSpec((tm, P), row), pl.BlockSpec((1, D), const),
                  pl.BlockSpec((D, D), const, **_FETCH_ONCE), pl.BlockSpec((P, D), const, **_FETCH_ONCE),
                  pl.BlockSpec((1, D), const), pl.BlockSpec(memory_space=pl.ANY)],
        out_specs=pl.BlockSpec((tm, D), row),
        out_shape=jax.ShapeDtypeStruct((T, D), F32),
        scratch_shapes=[pltpu.VMEM((2, 2, tm) + ys.shape[1:], ys.dtype),
                        pltpu.VMEM((D, D), BF16), pltpu.VMEM((P, D), BF16),
                        pltpu.SemaphoreType.DMA((2,))],
        compiler_params=_cparams(("arbitrary",)),
        name="combine",
    )(pos_t[0], pos_t[1], pos_t[0], pos_t[1], x1, wts, p2, g_ple, w_ple_gate, w_ple_proj,
      g_final, ys)


def _block_diag(w):
    nb, d, _ = w.shape
    eye = jnp.eye(nb, dtype=w.dtype)
    return (eye[:, None, :, None] * w[:, :, None, :]).reshape(nb * d, nb * d)


def kernel(x, p, positions, g_mix, w_in, sinks, conv_w, conv_b, lru_wa, lru_ba, lru_wx, lru_bx,
           lru_lambda, g_attn_out, g_lru_out, w_out, g_ffn, w_router_group, b_router_group,
           w_router_expert, b_router_expert, w_expert_gate, w_expert_up, w_expert_down, g_ple,
           w_ple_gate, w_ple_proj, g_final):
    B, S, D = x.shape
    assert w_in.shape[0] == 1, "single-layer block only"
    T = B * S
    lru_w = conv_w.shape[-1]
    n_rows = 2 * T + N_EXPERTS * TM_EXPERT
    n_tiles = n_rows // TM_EXPERT
    pos3 = positions.reshape(T // TM_INPROJ, TM_INPROJ // LANES, LANES).astype(I32)
    x2 = x.reshape(T, D)
    q, k, v, lx, lg = _inproj(x2, pos3, g_mix[0][None], w_in[0], lru_w)
    mix_a = _attention(q, k, v, sinks[0], g_attn_out[0][None], B, S)
    mix_b = _lru(lx.reshape(B, S, lru_w), lg.reshape(B, S, lru_w), conv_w[0], conv_b[0][None],
                 _block_diag(lru_wa[0]).astype(BF16), lru_ba[0][None],
                 _block_diag(lru_wx[0]).astype(BF16), lru_bx[0][None],
                 lru_lambda[0][None], g_lru_out[0][None]).reshape(T, lru_w)
    n_router = N_GROUPS + N_EXPERTS
    w_router = jnp.pad(jnp.concatenate([w_router_group[0], w_router_expert[0]], axis=1),
                       ((0, 0), (0, LANES - n_router)))
    b_router = jnp.pad(jnp.concatenate([b_router_group[0], b_router_expert[0]]),
                       (0, LANES - n_router))[None]
    w_router_hi = w_router.astype(BF16)
    w_router_lo = (w_router - w_router_hi.astype(F32)).astype(BF16)
    x1, h2, eid, wts = _outproj(x2, mix_a, mix_b, w_out[0], g_ffn[0][None],
                                jnp.concatenate([w_router_hi, w_router_lo], axis=1), b_router)
    pos_t, lrow, tile_expert, n_valid, meta, tile_cnt, tile_start = _plan(eid, n_tiles)
    n_valid = n_valid[0, :1]
    xs = _dispatch(tile_cnt[:, :N_EXPERTS].reshape(-1), tile_start[:, :N_EXPERTS].reshape(-1),
                   meta[0], meta[1], n_valid, lrow, h2, n_rows)
    Fh = w_expert_gate.shape[-1]
    ys = _experts(tile_expert[:, 0], n_valid, xs,
                  w_expert_gate[0].reshape(N_EXPERTS, D, Fh),
                  w_expert_up[0].reshape(N_EXPERTS, D, Fh),
                  w_expert_down[0].reshape(N_EXPERTS, Fh, D))
    out = _combine(pos_t, x1, wts, p[0].reshape(T, -1), g_ple[0][None],
                   w_ple_gate[0], w_ple_proj[0], g_final[None], ys)
    return out.reshape(B, S, D)
```

```python
import functools
import math

import jax
import jax.numpy as jnp
from jax import lax
from jax.experimental import pallas as pl
from jax.experimental.pallas import tpu as pltpu

F32 = jnp.float32
BF16 = jnp.bfloat16
I32 = jnp.int32

EPS = 1e-6
N_HEADS = 8
N_KV_HEADS = 2
HEAD_DIM = 64
WINDOW = 128
ROPE_THETA = 10000.0
CONV_WIDTH = 4
LRU_C = 8.0
N_GROUPS = 4
EXPERTS_PER_GROUP = 8
N_EXPERTS = N_GROUPS * EXPERTS_PER_GROUP
NEG_INF = -1e30
LANES = 128
SUBLANES = 8

TM_INPROJ = 512
TM_PROJ = 512
TQ_ATTN = 512
TL_LRU = 256
TM_PLAN = 1024
TM_EXPERT = 512
EXPERT_FF_BLOCK = 256
TM_DISPATCH = 512
TM_COMBINE = 512
VMEM_LIMIT = 56 * 1024 * 1024


def _rms(x, g):
    ms = jnp.mean(x * x, axis=-1, keepdims=True)
    return x * lax.rsqrt(ms + EPS) * g


_FETCH_ONCE = dict(pipeline_mode=pl.Buffered(1))


def _cparams(sem):
    return pltpu.CompilerParams(dimension_semantics=sem, vmem_limit_bytes=VMEM_LIMIT)


def _inproj_kernel(x_ref, pos_ref, g_ref, w_ref, q_ref, k_ref, v_ref, lx_ref, lg_ref, w_b):
    @pl.when(pl.program_id(0) == 0)
    def _():
        w_b[...] = w_ref[...].astype(BF16)

    h = _rms(x_ref[...], g_ref[...]).astype(BF16)
    proj = jnp.dot(h, w_b[...], preferred_element_type=F32)
    lane = lax.broadcasted_iota(I32, (1, LANES), 1)
    fidx = (lax.broadcasted_iota(I32, (LANES, 1), 0) % (HEAD_DIM // 2)).astype(F32)
    inv_freq = jnp.exp(fidx * (-2.0 / HEAD_DIM * math.log(ROPE_THETA)))
    pos = pos_ref[0].astype(F32)
    ang = jnp.concatenate([(inv_freq * pos[c:c + 1, :]).T for c in range(pos.shape[0])], axis=0)
    cos = jnp.cos(ang)
    sin = jnp.sin(ang)
    first_half = (lane % HEAD_DIM) < (HEAD_DIM // 2)
    sin_signed = jnp.where(first_half, -sin, sin)

    def rope(t):
        partner = jnp.where(first_half, pltpu.roll(t, LANES - HEAD_DIM // 2, 1),
                            pltpu.roll(t, HEAD_DIM // 2, 1))
        return t * cos + partner * sin_signed

    attn_w = N_HEADS * HEAD_DIM
    scale = HEAD_DIM ** -0.5
    assert math.frexp(scale)[0] == 0.5, "score scale must be a power of two to fold into q exactly"
    for c in range(attn_w // LANES):
        q_ref[:, c * LANES:(c + 1) * LANES] = (
            rope(proj[:, c * LANES:(c + 1) * LANES]) * scale).astype(BF16)
    kv_w = N_KV_HEADS * HEAD_DIM
    assert kv_w == LANES
    k = rope(proj[:, attn_w:attn_w + kv_w])
    v = proj[:, attn_w + kv_w:attn_w + 2 * kv_w]
    k_ref[:, :kv_w] = k.astype(BF16)
    k_ref[:, kv_w:] = pltpu.roll(k, HEAD_DIM, 1).astype(BF16)
    v_ref[:, :kv_w] = v.astype(BF16)
    v_ref[:, kv_w:] = pltpu.roll(v, HEAD_DIM, 1).astype(BF16)
    off_lx = attn_w + 2 * kv_w
    lru_w = lx_ref.shape[1]
    lx_ref[...] = proj[:, off_lx:off_lx + lru_w]
    lg_ref[...] = proj[:, off_lx + lru_w:off_lx + 2 * lru_w]


def _inproj(x2, pos3, g_mix, w_in, lru_w):
    T, D = x2.shape
    tm = TM_INPROJ
    attn_w = N_HEADS * HEAD_DIM
    kv_w = N_KV_HEADS * HEAD_DIM
    row = lambda i: (i, 0)
    const = lambda i: (0, 0)
    return pl.pallas_call(
        _inproj_kernel,
        grid=(T // tm,),
        in_specs=[pl.BlockSpec((tm, D), row),
                  pl.BlockSpec((1, tm // LANES, LANES), lambda i: (i, 0, 0)),
                  pl.BlockSpec((1, D), const), pl.BlockSpec(w_in.shape, const, **_FETCH_ONCE)],
        out_specs=[pl.BlockSpec((tm, attn_w), row), pl.BlockSpec((tm, 2 * kv_w), row),
                   pl.BlockSpec((tm, 2 * kv_w), row), pl.BlockSpec((tm, lru_w), row),
                   pl.BlockSpec((tm, lru_w), row)],
        out_shape=[jax.ShapeDtypeStruct((T, attn_w), BF16),
                   jax.ShapeDtypeStruct((T, 2 * kv_w), BF16),
                   jax.ShapeDtypeStruct((T, 2 * kv_w), BF16), jax.ShapeDtypeStruct((T, lru_w), F32),
                   jax.ShapeDtypeStruct((T, lru_w), F32)],
        scratch_shapes=[pltpu.VMEM(w_in.shape, BF16)],
        compiler_params=_cparams(("arbitrary",)),
        name="inproj",
    )(x2, pos3, g_mix, w_in)


def _attn_kernel(sinks_ref, q_ref, k_ref, v_ref, kp_ref, vp_ref, g_ref, o_ref):
    n = pl.program_id(1)
    blk = WINDOW
    hd = HEAD_DIM
    nqb = q_ref.shape[0] // blk
    n_pairs = N_HEADS // 2
    pairs_per_group = n_pairs // N_KV_HEADS
    lane = lax.broadcasted_iota(I32, (1, LANES), 1)
    first = lane < hd
    qi = lax.broadcasted_iota(I32, (blk, 2 * blk), 0)
    kj = lax.broadcasted_iota(I32, (blk, 2 * blk), 1)
    rel = qi + blk - kj
    band = (rel >= 0) & (rel < WINDOW)
    bd_row = lax.broadcasted_iota(I32, (4 * blk, LANES), 0)
    bd_lane = lax.broadcasted_iota(I32, (4 * blk, LANES), 1)
    ones_bd = ((bd_row < 2 * blk) == (bd_lane < hd)).astype(BF16)
    zero = jnp.zeros((), BF16)

    def block_diag(t):
        same, swapped = t[:, :LANES], t[:, LANES:]
        out = []
        for g in range(N_KV_HEADS):
            top = jnp.where(first, same if g == 0 else swapped, zero)
            bot = jnp.where(first, zero, swapped if g == 0 else same)
            out.append(jnp.concatenate([top, bot], axis=0))
        return out

    for j in range(nqb):
        rows = slice(j * blk, (j + 1) * blk)
        if j == 0:
            k_prev, v_prev = kp_ref[...], vp_ref[...]
            mask = band & ((kj >= blk) | (n > 0))
        else:
            prev_rows = slice((j - 1) * blk, j * blk)
            k_prev, v_prev = k_ref[prev_rows, :], v_ref[prev_rows, :]
            mask = band
        k_bd = block_diag(jnp.concatenate([k_prev, k_ref[rows, :]], axis=0))
        v_bd = block_diag(jnp.concatenate([v_prev, v_ref[rows, :]], axis=0))
        outs = []
        for p in range(n_pairs):
            g = p // pairs_per_group
            qp = q_ref[rows, p * LANES:(p + 1) * LANES]
            s = lax.dot_general(qp, k_bd[g], (((1,), (1,)), ((), ())), preferred_element_type=F32)
            es, ms = [], []
            for a in range(2):
                sa = jnp.where(mask, s[:, a * 2 * blk:(a + 1) * 2 * blk], NEG_INF)
                m = jnp.maximum(jnp.max(sa, axis=-1, keepdims=True), sinks_ref[2 * p + a])
                es.append(jnp.exp(sa - m))
                ms.append(m)
            e = jnp.concatenate(es, axis=1).astype(BF16)
            od = jnp.dot(e, jnp.concatenate([v_bd[g], ones_bd], axis=1), preferred_element_type=F32)
            sink_term = jnp.where(first, jnp.exp(sinks_ref[2 * p] - ms[0]),
                                  jnp.exp(sinks_ref[2 * p + 1] - ms[1]))
            outs.append(od[:, :LANES] / (od[:, LANES:] + sink_term))
        attn = jnp.concatenate(outs, axis=1)
        o_ref[rows, :] = _rms(attn, g_ref[...]).astype(BF16)


def _attention(q, k, v, sinks, g_attn, B, S):
    T, attn_w = q.shape
    kv_w = k.shape[1]
    blk = WINDOW
    tq = TQ_ATTN
    nt = S // tq
    cur = lambda b, n: (b * nt + n, 0)
    prev = lambda b, n: (b * (S // blk) + jnp.maximum(n * (tq // blk) - 1, 0), 0)
    return pl.pallas_call(
        _attn_kernel,
        grid=(B, nt),
        in_specs=[pl.BlockSpec(memory_space=pltpu.SMEM),
                  pl.BlockSpec((tq, attn_w), cur),
                  pl.BlockSpec((tq, kv_w), cur), pl.BlockSpec((tq, kv_w), cur),
                  pl.BlockSpec((blk, kv_w), prev), pl.BlockSpec((blk, kv_w), prev),
                  pl.BlockSpec((1, attn_w), lambda b, n: (0, 0))],
        out_specs=pl.BlockSpec((tq, attn_w), cur),
        out_shape=jax.ShapeDtypeStruct((T, attn_w), BF16),
        compiler_params=_cparams(("arbitrary", "arbitrary")),
        name="attn",
    )(sinks, q, k, v, k, v, g_attn)


def _neg_expm1(x, exp_x):
    series = -x * (1.0 + x * (0.5 + x * (1.0 / 6.0 + x * (1.0 / 24.0))))
    return jnp.where(x > -0.02, series, 1.0 - exp_x)


def _gelu_tanh(x):
    return 0.5 * x * (1.0 + jnp.tanh(math.sqrt(2.0 / math.pi) * (x + 0.044715 * (x * x * x))))


def _lru_kernel(lx_ref, lg_ref, cw_ref, cb_ref, wa_ref, ba_ref, wx_ref, bx_ref, lam_ref, g_ref,
                o_ref, xpad, a_scr, b_scr, h_scr, h_carry):
    i = pl.program_id(0)
    B, tl, C = lx_ref.shape
    pad = SUBLANES

    @pl.when(i == 0)
    def _():
        xpad[:, 0:pad, :] = jnp.zeros((B, pad, C), F32)
        h_carry[...] = jnp.zeros_like(h_carry)

    @pl.when(i > 0)
    def _():
        xpad[:, 0:pad, :] = xpad[:, tl:tl + pad, :]

    xpad[:, pad:, :] = lx_ref[...]
    y = jnp.zeros((B, tl, C), F32) + cb_ref[...]
    for j in range(CONV_WIDTH):
        y = y + xpad[:, pl.ds(pad - (CONV_WIDTH - 1) + j, tl), :] * cw_ref[j:j + 1, :]
    y2 = y.reshape(B * tl, C)
    yb = y2.astype(BF16)
    gate_a = jnp.dot(yb, wa_ref[...], preferred_element_type=F32) + ba_ref[...]
    gate_x = jnp.dot(yb, wx_ref[...], preferred_element_type=F32) + bx_ref[...]
    r = jax.nn.sigmoid(gate_a)
    ig = jax.nn.sigmoid(gate_x)
    nl = -lam_ref[...]
    softplus = jnp.maximum(nl, 0.0) + jnp.log1p(jnp.exp(-jnp.abs(nl)))
    log_a = (-LRU_C) * r * softplus
    a_all = jnp.exp(log_a)
    one_minus_a2 = _neg_expm1(2.0 * log_a, a_all * a_all)
    b_all = one_minus_a2 * lax.rsqrt(jnp.maximum(one_minus_a2, 1e-30)) * ig * y2
    groups = (B * tl // SUBLANES, SUBLANES, C)
    sub = lax.broadcasted_iota(I32, groups, 1)
    a_cum, b_loc = a_all.reshape(groups), b_all.reshape(groups)
    d = 1
    while d < SUBLANES:
        keep = sub >= d
        a_prev = pltpu.roll(a_cum, d, 1)
        b_prev = pltpu.roll(b_loc, d, 1)
        b_loc = b_loc + jnp.where(keep, a_cum * b_prev, 0.0)
        a_cum = jnp.where(keep, a_cum * a_prev, a_cum)
        d *= 2
    a_scr[...] = a_cum.reshape(B, tl, C)
    b_scr[...] = b_loc.reshape(B, tl, C)

    def carry(g, h_prev):
        rows = pl.ds(pl.multiple_of(g * SUBLANES, SUBLANES), SUBLANES)
        h = b_scr[:, rows, :] + a_scr[:, rows, :] * h_prev
        h_scr[:, rows, :] = h
        return h[:, SUBLANES - 1:SUBLANES, :]

    h_carry[...] = lax.fori_loop(0, tl // SUBLANES, carry, h_carry[...], unroll=4)
    lru = h_scr[...].reshape(B * tl, C) * _gelu_tanh(lg_ref[...].reshape(B * tl, C))
    o_ref[...] = _rms(lru, g_ref[...]).astype(BF16).reshape(B, tl, C)


def _lru(lx3, lg3, conv_w, conv_b, wa_bd, ba, wx_bd, bx, lam, g_lru):
    B, S, C = lx3.shape
    tl = TL_LRU
    blk = lambda i: (0, i, 0)
    const = lambda i: (0, 0)
    vec = pl.BlockSpec((1, C), const)
    return pl.pallas_call(
        _lru_kernel,
        grid=(S // tl,),
        in_specs=[pl.BlockSpec((B, tl, C), blk), pl.BlockSpec((B, tl, C), blk),
                  pl.BlockSpec((CONV_WIDTH, C), const), vec,
                  pl.BlockSpec((C, C), const), vec, pl.BlockSpec((C, C), const), vec, vec, vec],
        out_specs=pl.BlockSpec((B, tl, C), blk),
        out_shape=jax.ShapeDtypeStruct((B, S, C), BF16),
        scratch_shapes=[pltpu.VMEM((B, tl + SUBLANES, C), F32),
                        pltpu.VMEM((B, tl, C), F32), pltpu.VMEM((B, tl, C), F32),
                        pltpu.VMEM((B, tl, C), F32), pltpu.VMEM((B, 1, C), F32)],
        compiler_params=_cparams(("arbitrary",)),
        name="lru",
    )(lx3, lg3, conv_w, conv_b, wa_bd, ba, wx_bd, bx, lam, g_lru)


def _outproj_kernel(x_ref, ma_ref, mb_ref, wo_ref, g_ref, wr_ref, br_ref,
                    x1_ref, h2_ref, eid_ref, wt_ref, wo_b):
    @pl.when(pl.program_id(0) == 0)
    def _():
        wo_b[...] = wo_ref[...].astype(BF16)

    half = ma_ref.shape[1]
    x1 = (x_ref[...]
          + jnp.dot(ma_ref[...], wo_b[0:half, :], preferred_element_type=F32)
          + jnp.dot(mb_ref[...], wo_b[half:, :], preferred_element_type=F32))
    x1_ref[...] = x1
    h2 = _rms(x1, g_ref[...])
    h2_ref[...] = h2.astype(BF16).reshape(h2_ref.shape)
    h_hi = h2.astype(BF16)
    h_lo = (h2 - h_hi.astype(F32)).astype(BF16)
    hw = jnp.dot(h_hi, wr_ref[...], preferred_element_type=F32)
    logits = (hw[:, :LANES] + hw[:, LANES:]
              + jnp.dot(h_lo, wr_ref[:, :LANES], preferred_element_type=F32) + br_ref[...])
    tm = logits.shape[0]
    lane = lax.broadcasted_iota(I32, (tm, LANES), 1)
    lane_f = lane.astype(F32)
    big = float(LANES)

    def first_argmax(vals):
        m = jnp.max(vals, axis=-1, keepdims=True)
        idx = jnp.min(jnp.where(vals == m, lane_f, big), axis=-1, keepdims=True)
        return m, idx.astype(I32)

    gl = jnp.where(lane < N_GROUPS, logits, NEG_INF)
    gmax, gidx = first_argmax(gl)
    gsum = jnp.sum(jnp.where(lane < N_GROUPS, jnp.exp(gl - gmax), 0.0), axis=-1, keepdims=True)
    g_top_p = 1.0 / gsum
    lo = N_GROUPS + EXPERTS_PER_GROUP * gidx
    el = jnp.where((lane >= lo) & (lane < lo + EXPERTS_PER_GROUP), logits, NEG_INF)
    m1, i1 = first_argmax(el)
    el2 = jnp.where(lane == i1, NEG_INF, el)
    m2, i2 = first_argmax(el2)
    ratio = jnp.exp(m2 - m1)
    w1 = g_top_p / (1.0 + ratio)
    w2 = g_top_p * ratio / (1.0 + ratio)
    eid_ref[...] = jnp.where(lane == 0, i1 - N_GROUPS, jnp.where(lane == 1, i2 - N_GROUPS, 0))
    wt_ref[...] = jnp.where(lane == 0, w1, jnp.where(lane == 1, w2, 0.0))


def _outproj(x2, mix_a, mix_b, w_out, g_ffn, w_router, b_router):
    T, D = x2.shape
    tm = TM_PROJ
    half = mix_a.shape[1]
    row = lambda i: (i, 0)
    const = lambda i: (0, 0)
    return pl.pallas_call(
        _outproj_kernel,
        grid=(T // tm,),
        in_specs=[pl.BlockSpec((tm, D), row), pl.BlockSpec((tm, half), row),
                  pl.BlockSpec((tm, half), row), pl.BlockSpec(w_out.shape, const, **_FETCH_ONCE),
                  pl.BlockSpec((1, D), const), pl.BlockSpec((D, 2 * LANES), const),
                  pl.BlockSpec((1, LANES), const)],
        out_specs=[pl.BlockSpec((tm, D), row),
                   pl.BlockSpec((tm, D // LANES, LANES), lambda i: (i, 0, 0)),
                   pl.BlockSpec((tm, LANES), row), pl.BlockSpec((tm, LANES), row)],
        out_shape=[jax.ShapeDtypeStruct((T, D), F32),
                   jax.ShapeDtypeStruct((T, D // LANES, LANES), BF16),
                   jax.ShapeDtypeStruct((T, LANES), I32), jax.ShapeDtypeStruct((T, LANES), F32)],
        scratch_shapes=[pltpu.VMEM(w_out.shape, BF16)],
        compiler_params=_cparams(("arbitrary",)),
        name="outproj",
    )(x2, mix_a, mix_b, w_out, g_ffn, w_router, b_router)


def _lane_cumsum(x):
    lane = lax.broadcasted_iota(I32, x.shape, 1)
    shift = 1
    while shift < LANES:
        x = x + jnp.where(lane >= shift, pltpu.roll(x, shift, 1), 0)
        shift *= 2
    return x


def _plan_kernel(eid_ref, pos_ref, lrow_ref, te_ref, nv_ref, meta_ref, tcnt_ref, tstart_ref,
                 rank_scr, cnt_scr, start_scr, tcnt_scr, tpre_scr):
    phase = pl.program_id(0)
    i = pl.program_id(1)
    tm = eid_ref.shape[0]
    td = TM_DISPATCH
    nsub = tm // td
    lane = lax.broadcasted_iota(I32, (tm, LANES), 1)
    eid = eid_ref[...]
    f1 = eid[:, 0:1]
    f2 = eid[:, 1:2]
    sel1 = lane == f1
    sel2 = lane == f2

    @pl.when((phase == 0) & (i == 0))
    def _():
        cnt_scr[...] = jnp.zeros_like(cnt_scr)

    @pl.when(phase == 0)
    def _():
        onehot = (sel1 | sel2).astype(BF16)
        r = lax.broadcasted_iota(I32, (tm, tm), 0)
        c = lax.broadcasted_iota(I32, (tm, tm), 1)
        strict_lower = (r > c).astype(BF16)
        within = jnp.dot(strict_lower, onehot, preferred_element_type=F32)
        base = cnt_scr[0:1, :]
        before = within + base
        r1 = jnp.sum(jnp.where(sel1, before, 0.0), axis=-1, keepdims=True)
        r2 = jnp.sum(jnp.where(sel2, before, 0.0), axis=-1, keepdims=True)
        rank = jnp.where(lane == 0, r1, jnp.where(lane == 1, r2, 0.0)).astype(I32)
        rank_scr[pl.ds(pl.multiple_of(i * tm, tm), tm), :] = rank
        onehot_f = onehot.astype(F32)
        row = lax.broadcasted_iota(I32, (tm, LANES), 0)
        srow = lax.broadcasted_iota(I32, (SUBLANES, LANES), 0)
        counts = [jnp.sum(onehot_f[s * td:(s + 1) * td], axis=0, keepdims=True) for s in range(nsub)]
        cnt_mat = jnp.zeros((SUBLANES, LANES), F32)
        seen = jnp.zeros((1, LANES), F32)
        local_before = within
        for s in range(nsub):
            tile = i * nsub + s
            tcnt_scr[pl.ds(tile, 1), :] = counts[s]
            tpre_scr[pl.ds(tile, 1), :] = base + seen
            cnt_mat = jnp.where(srow == s, counts[s], cnt_mat)
            if s > 0:
                local_before = jnp.where(row >= s * td, within - seen, local_before)
            seen = seen + counts[s]
        cnt_i = cnt_mat.astype(I32)
        run_start = (_lane_cumsum(cnt_i) - cnt_i).astype(F32)
        local = local_before + run_start[0:1, :]
        for s in range(1, nsub):
            local = jnp.where(row >= s * td, local_before + run_start[s:s + 1, :], local)
        l1 = jnp.sum(jnp.where(sel1, local, 0.0), axis=-1, keepdims=True)
        l2 = jnp.sum(jnp.where(sel2, local, 0.0), axis=-1, keepdims=True)
        lrow_ref[...] = jnp.where(lane == 0, l1, jnp.where(lane == 1, l2, 0.0)).astype(I32)
        cnt_scr[...] = cnt_scr[...] + seen

    @pl.when(phase == 1)
    def _():
        starts = start_scr[0:1, :]
        o1 = jnp.sum(jnp.where(sel1, starts, 0.0), axis=-1, keepdims=True).astype(I32)
        o2 = jnp.sum(jnp.where(sel2, starts, 0.0), axis=-1, keepdims=True).astype(I32)
        rank = rank_scr[pl.ds(pl.multiple_of(i * tm, tm), tm), :]
        pos = rank + jnp.where(lane == 0, o1, jnp.where(lane == 1, o2, 0))
        pos_ref[...] = pos.astype(F32).T[0:SUBLANES, :].astype(I32)

    @pl.when((phase == 0) & (i == pl.num_programs(1) - 1))
    def _():
        shift = int(math.log2(TM_EXPERT))
        cnt = cnt_scr[...].astype(I32)
        padded = ((cnt + (TM_EXPERT - 1)) >> shift) << shift
        ends = _lane_cumsum(padded)
        start_scr[...] = (ends - padded).astype(F32)
        nt = te_ref.shape[0]
        end_tile = (ends >> shift)[0:1, :]
        tl_lane = lax.broadcasted_iota(I32, (nt, LANES), 1)
        tile = lax.broadcasted_iota(I32, (nt, LANES), 0)
        done = jnp.where((tl_lane < N_EXPERTS) & (end_tile <= tile), 1.0, 0.0)
        te = jnp.minimum(jnp.sum(done, axis=-1, keepdims=True).astype(I32), N_EXPERTS - 1)
        n_valid = jnp.sum(jnp.where(tl_lane[0:1, :] == N_EXPERTS - 1, end_tile, 0).astype(F32),
                          axis=-1, keepdims=True).astype(I32)
        nv_ref[...] = jnp.broadcast_to(n_valid, nv_ref.shape)
        seg_end = jnp.sum(jnp.where(tl_lane == te, end_tile, 0).astype(F32), axis=-1,
                          keepdims=True).astype(I32)
        nxt = jnp.sum(jnp.where((tl_lane < N_EXPERTS) & (end_tile <= seg_end), 1.0, 0.0), axis=-1,
                      keepdims=True).astype(I32)
        nxt = jnp.where(seg_end < n_valid, jnp.minimum(nxt, N_EXPERTS - 1), -1)
        te_ref[...] = jnp.where(tl_lane == 1, nxt, te)
        row = lax.broadcasted_iota(I32, cnt.shape, 0)
        meta_ref[...] = jnp.where(row == 0, ends - padded + cnt, jnp.where(row == 1, padded - cnt, 0))
        tcnt_ref[...] = tcnt_scr[...].astype(I32)
        tstart_ref[...] = (tpre_scr[...] + (ends - padded).astype(F32)[0:1, :]).astype(I32)


def _plan(eid, n_tiles):
    T = eid.shape[0]
    tm = TM_PLAN
    steps = T // tm
    n_dtiles = T // TM_DISPATCH
    const = lambda p, i: (0, 0)
    return pl.pallas_call(
        _plan_kernel,
        grid=(2, steps),
        in_specs=[pl.BlockSpec((tm, LANES), lambda p, i: (i, 0))],
        out_specs=[pl.BlockSpec((SUBLANES, tm), lambda p, i: (0, i * p)),
                   pl.BlockSpec((tm, LANES), lambda p, i: (i * (1 - p) + (steps - 1) * p, 0)),
                   pl.BlockSpec((n_tiles, LANES), const), pl.BlockSpec((SUBLANES, LANES), const),
                   pl.BlockSpec((SUBLANES, LANES), const),
                   pl.BlockSpec((n_dtiles, LANES), const), pl.BlockSpec((n_dtiles, LANES), const)],
        out_shape=[jax.ShapeDtypeStruct((SUBLANES, T), I32), jax.ShapeDtypeStruct((T, LANES), I32),
                   jax.ShapeDtypeStruct((n_tiles, LANES), I32),
                   jax.ShapeDtypeStruct((SUBLANES, LANES), I32),
                   jax.ShapeDtypeStruct((SUBLANES, LANES), I32),
                   jax.ShapeDtypeStruct((n_dtiles, LANES), I32),
                   jax.ShapeDtypeStruct((n_dtiles, LANES), I32)],
        scratch_shapes=[pltpu.VMEM((T, LANES), I32), pltpu.VMEM((SUBLANES, LANES), F32),
                        pltpu.VMEM((SUBLANES, LANES), F32),
                        pltpu.VMEM((n_dtiles, LANES), F32), pltpu.VMEM((n_dtiles, LANES), F32)],
        compiler_params=_cparams(("arbitrary", "arbitrary")),
        name="plan",
    )(eid)


_PAD_BITS = tuple(1 << b for b in reversed(range(int(math.log2(TM_EXPERT)))))


_RUN_BITS = tuple(1 << b for b in reversed(range(int(math.log2(TM_DISPATCH)) + 1)))


def _dispatch_kernel(tcnt_ref, tstart_ref, seg_end_ref, seg_pad_ref, nv_ref, lrow_ref, h_ref,
                     xs_ref, lbuf, zeros, sem, zsem):
    i = pl.program_id(0)
    n = pl.num_programs(0)
    td = h_ref.shape[0]
    nr = 2 * td
    slot = i % 2

    def wait_slot(s):
        pltpu.make_async_copy(lbuf.at[s], xs_ref.at[pl.ds(0, nr)], sem.at[s]).wait()

    @pl.when(i >= 2)
    def _():
        wait_slot(slot)

    local_row = lrow_ref[...].astype(F32).T
    r = lax.broadcasted_iota(I32, (nr, td), 0).astype(F32)
    onehot = ((r == local_row[0:1, :]) | (r == local_row[1:2, :])).astype(BF16)
    sorted_rows = jnp.dot(onehot, h_ref[...].reshape(td, -1), preferred_element_type=F32)
    lbuf[slot] = sorted_rows.astype(BF16).reshape(lbuf.shape[1:])

    def run(e, lo):
        cnt = tcnt_ref[i * N_EXPERTS + e]
        dst = tstart_ref[i * N_EXPERTS + e]
        for bit in _RUN_BITS:
            off = cnt & ~(2 * bit - 1)

            @pl.when((cnt & bit) != 0)
            def _():
                pltpu.make_async_copy(lbuf.at[slot, pl.ds(lo + off, bit)],
                                      xs_ref.at[pl.ds(dst + off, bit)], sem.at[slot]).start()
        return lo + cnt

    lax.fori_loop(0, N_EXPERTS, run, 0)

    @pl.when(i == n - 1)
    def _():
        wait_slot(slot)

        @pl.when(n >= 2)
        def _():
            wait_slot(1 - slot)

        zeros[...] = jnp.zeros_like(zeros)
        n_tiles = xs_ref.shape[0] // TM_EXPERT
        first_tail = n_tiles - N_EXPERTS

        def pad_copy(e, bit):
            n = seg_pad_ref[e]
            dst = seg_end_ref[e] + (n & ~(2 * bit - 1))
            return (n & bit) != 0, pltpu.make_async_copy(zeros.at[pl.ds(0, bit)],
                                                         xs_ref.at[pl.ds(dst, bit)], zsem)

        def tail_copy(j):
            return j >= nv_ref[0], pltpu.make_async_copy(
                zeros, xs_ref.at[pl.ds(j * TM_EXPERT, TM_EXPERT)], zsem)

        def for_each_fill(act):
            def per_expert(e, c):
                for bit in _PAD_BITS:
                    pred, cp = pad_copy(e, bit)
                    pl.when(pred)(functools.partial(act, cp))
                return c

            lax.fori_loop(0, N_EXPERTS, per_expert, 0)

            def per_tile(j, c):
                pred, cp = tail_copy(j)
                pl.when(pred)(functools.partial(act, cp))
                return c

            lax.fori_loop(first_tail, n_tiles, per_tile, 0)

        for_each_fill(lambda cp: cp.start())
        for_each_fill(lambda cp: cp.wait())


def _dispatch(tile_cnt, tile_start, seg_end, seg_pad, n_valid, lrow, h2, n_rows):
    T = h2.shape[0]
    slab = h2.shape[1:]
    td = TM_DISPATCH
    smem = pl.BlockSpec(memory_space=pltpu.SMEM)
    return pl.pallas_call(
        _dispatch_kernel,
        grid=(T // td,),
        in_specs=[smem, smem, smem, smem, smem,
                  pl.BlockSpec((td, LANES), lambda i: (i, 0)),
                  pl.BlockSpec((td,) + slab, lambda i: (i, 0, 0))],
        out_specs=pl.BlockSpec(memory_space=pl.ANY),
        out_shape=jax.ShapeDtypeStruct((n_rows,) + slab, h2.dtype),
        scratch_shapes=[pltpu.VMEM((2, 2 * td) + slab, h2.dtype),
                        pltpu.VMEM((TM_EXPERT,) + slab, h2.dtype),
                        pltpu.SemaphoreType.DMA((2,)), pltpu.SemaphoreType.DMA(())],
        compiler_params=_cparams(("arbitrary",)),
        name="dispatch",
    )(tile_cnt, tile_start, seg_end, seg_pad, n_valid, lrow, h2)


def _experts_kernel(te_ref, tn_ref, nv_ref, xs_ref, wg_ref, wu_ref, wd_ref, ys_ref,
                    wg_f, wu_f, wd_f, wg_b, wu_b, wd_b, slot_ref, sem):
    j = pl.program_id(0)
    valid = j < nv_ref[0]
    new_expert = valid & ((j == 0) | (te_ref[j] != te_ref[jnp.maximum(j - 1, 0)]))

    def fetch(e, s):
        return (pltpu.make_async_copy(wg_ref.at[e], wg_f.at[s], sem.at[s]),
                pltpu.make_async_copy(wu_ref.at[e], wu_f.at[s], sem.at[s]),
                pltpu.make_async_copy(wd_ref.at[e], wd_f.at[s], sem.at[s]))

    @pl.when(j == 0)
    def _():
        slot_ref[0] = 0
        for cp in fetch(te_ref[0], 0):
            cp.start()

    @pl.when(new_expert)
    def _():
        s = slot_ref[0]
        for cp in fetch(te_ref[j], s):
            cp.wait()
        nxt = tn_ref[j]

        @pl.when(nxt >= 0)
        def _():
            for cp in fetch(nxt, 1 - s):
                cp.start()

        wg_b[...] = wg_f[s].astype(BF16)
        wu_b[...] = wu_f[s].astype(BF16)
        wd_b[...] = wd_f[s].astype(BF16)
        slot_ref[0] = 1 - s

    @pl.when(valid)
    def _():
        x = xs_ref[...].reshape(xs_ref.shape[0], -1)
        y = None
        for lo in range(0, wg_b.shape[1], EXPERT_FF_BLOCK):
            cols = slice(lo, lo + EXPERT_FF_BLOCK)
            a = jnp.dot(x, wg_b[:, cols], preferred_element_type=F32)
            u = jnp.dot(x, wu_b[:, cols], preferred_element_type=F32)
            hid = (a * jax.nn.sigmoid(a) * u).astype(BF16)
            part = jnp.dot(hid, wd_b[cols, :], preferred_element_type=F32)
            y = part if y is None else y + part
        ys_ref[...] = y.astype(BF16).reshape(ys_ref.shape)

    @pl.when(j >= nv_ref[0])
    def _():
        ys_ref[...] = jnp.zeros_like(ys_ref)


def _experts(tile_expert, tile_next, n_valid, xs, wg, wu, wd):
    R = xs.shape[0]
    slab = xs.shape[1:]
    E, D, Fh = wg.shape
    tm = TM_EXPERT
    n_tiles = R // tm
    rows = lambda j, te, tn, nv: (j, 0, 0)
    rows_in = lambda j, te, tn, nv: (jnp.minimum(j, nv[0] - 1), 0, 0)
    hbm = pl.BlockSpec(memory_space=pl.ANY)
    return pl.pallas_call(
        _experts_kernel,
        grid_spec=pltpu.PrefetchScalarGridSpec(
            num_scalar_prefetch=3,
            grid=(n_tiles,),
            in_specs=[pl.BlockSpec((tm,) + slab, rows_in), hbm, hbm, hbm],
            out_specs=pl.BlockSpec((tm,) + slab, rows),
            scratch_shapes=[pltpu.VMEM((2, D, Fh), F32), pltpu.VMEM((2, D, Fh), F32),
                            pltpu.VMEM((2, Fh, D), F32),
                            pltpu.VMEM((D, Fh), BF16), pltpu.VMEM((D, Fh), BF16),
                            pltpu.VMEM((Fh, D), BF16),
                            pltpu.SMEM((1,), I32), pltpu.SemaphoreType.DMA((2,))]),
        out_shape=jax.ShapeDtypeStruct(xs.shape, xs.dtype),
        compiler_params=_cparams(("arbitrary",)),
        name="experts",
    )(tile_expert, tile_next, n_valid, xs, wg, wu, wd)


def _combine_kernel(pos0_ref, pos1_ref, pos0_next_ref, pos1_next_ref, x1_ref, wt_ref, p_ref, gp_ref,
                    wpg_ref, wpp_ref, gf_ref, ys_ref, o_ref, buf, wpg_b, wpp_b, sem):
    i = pl.program_id(0)
    n = pl.num_programs(0)
    tm = x1_ref.shape[0]
    slot = i % 2

    def gather(idx_refs, s):
        def issue(t, c):
            for k in range(2):
                pltpu.make_async_copy(ys_ref.at[pl.ds(idx_refs[k][t], 1)],
                                      buf.at[s, k, pl.ds(t, 1)], sem.at[s]).start()
            return c

        lax.fori_loop(0, tm, issue, 0, unroll=8)

    @pl.when(i == 0)
    def _():
        gather((pos0_ref, pos1_ref), 0)
        wpg_b[...] = wpg_ref[...].astype(BF16)
        wpp_b[...] = wpp_ref[...].astype(BF16)

    for s in range(2):
        @pl.when((i + 1 < n) & (slot == 1 - s))
        def _():
            gather((pos0_next_ref, pos1_next_ref), s)

    for k in range(2):
        pltpu.make_async_copy(ys_ref.at[pl.ds(0, tm)], buf.at[slot, k], sem.at[slot]).wait()
    wt = wt_ref[...]
    y0 = buf[slot, 0].reshape(tm, -1).astype(F32)
    y1 = buf[slot, 1].reshape(tm, -1).astype(F32)
    x2 = x1_ref[...] + wt[:, 0:1] * y0 + wt[:, 1:2] * y1
    hp = _rms(x2, gp_ref[...]).astype(BF16)
    gate = jax.nn.sigmoid(jnp.dot(hp, wpg_b[...], preferred_element_type=F32))
    proj = jnp.dot(p_ref[...].astype(BF16), wpp_b[...], preferred_element_type=F32)
    x3 = x2 + gate * proj
    o_ref[...] = _rms(x3, gf_ref[...])


def _combine(pos_t, x1, wts, p2, g_ple, w_ple_gate, w_ple_proj, g_final, ys):
    T, D = x1.shape
    P = p2.shape[1]
    tm = TM_COMBINE
    n = T // tm
    row = lambda i: (i, 0)
    const = lambda i: (0, 0)
    cur = pl.BlockSpec((tm,), lambda i: (i,), memory_space=pltpu.SMEM)
    nxt = pl.BlockSpec((tm,), lambda i: (jnp.minimum(i + 1, n - 1),), memory_space=pltpu.SMEM)
    return pl.pallas_call(
        _combine_kernel,
        grid=(n,),
        in_specs=[cur, cur, nxt, nxt,
                  pl.BlockSpec((tm, D), row), pl.BlockSpec((tm, LANES), row),
                  pl.BlockSpec((tm, P), row), pl.BlockSpec((1, D), const),
                  pl.BlockSpec((D, D), const, **_FETCH_ONCE), pl.BlockSpec((P, D), const, **_FETCH_ONCE),
                  pl.BlockSpec((1, D), const), pl.BlockSpec(memory_space=pl.ANY)],
        out_specs=pl.BlockSpec((tm, D), row),
        out_shape=jax.ShapeDtypeStruct((T, D), F32),
        scratch_shapes=[pltpu.VMEM((2, 2, tm) + ys.shape[1:], ys.dtype),
                        pltpu.VMEM((D, D), BF16), pltpu.VMEM((P, D), BF16),
                        pltpu.SemaphoreType.DMA((2,))],
        compiler_params=_cparams(("arbitrary",)),
        name="combine",
    )(pos_t[0], pos_t[1], pos_t[0], pos_t[1], x1, wts, p2, g_ple, w_ple_gate, w_ple_proj,
      g_final, ys)


def _block_diag(w):
    nb, d, _ = w.shape
    eye = jnp.eye(nb, dtype=w.dtype)
    return (eye[:, None, :, None] * w[:, :, None, :]).reshape(nb * d, nb * d)


def kernel(x, p, positions, g_mix, w_in, sinks, conv_w, conv_b, lru_wa, lru_ba, lru_wx, lru_bx,
           lru_lambda, g_attn_out, g_lru_out, w_out, g_ffn, w_router_group, b_router_group,
           w_router_expert, b_router_expert, w_expert_gate, w_expert_up, w_expert_down, g_ple,
           w_ple_gate, w_ple_proj, g_final):
    B, S, D = x.shape
    assert w_in.shape[0] == 1, "single-layer block only"
    T = B * S
    lru_w = conv_w.shape[-1]
    n_rows = 2 * T + N_EXPERTS * TM_EXPERT
    n_tiles = n_rows // TM_EXPERT
    pos3 = positions.reshape(T // TM_INPROJ, TM_INPROJ // LANES, LANES).astype(I32)
    x2 = x.reshape(T, D)
    q, k, v, lx, lg = _inproj(x2, pos3, g_mix[0][None], w_in[0], lru_w)
    mix_a = _attention(q, k, v, sinks[0], g_attn_out[0][None], B, S)
    mix_b = _lru(lx.reshape(B, S, lru_w), lg.reshape(B, S, lru_w), conv_w[0], conv_b[0][None],
                 _block_diag(lru_wa[0]).astype(BF16), lru_ba[0][None],
                 _block_diag(lru_wx[0]).astype(BF16), lru_bx[0][None],
                 lru_lambda[0][None], g_lru_out[0][None]).reshape(T, lru_w)
    n_router = N_GROUPS + N_EXPERTS
    w_router = jnp.pad(jnp.concatenate([w_router_group[0], w_router_expert[0]], axis=1),
                       ((0, 0), (0, LANES - n_router)))
    b_router = jnp.pad(jnp.concatenate([b_router_group[0], b_router_expert[0]]),
                       (0, LANES - n_router))[None]
    w_router_hi = w_router.astype(BF16)
    w_router_lo = (w_router - w_router_hi.astype(F32)).astype(BF16)
    x1, h2, eid, wts = _outproj(x2, mix_a, mix_b, w_out[0], g_ffn[0][None],
                                jnp.concatenate([w_router_hi, w_router_lo], axis=1), b_router)
    pos_t, lrow, tile_expert, n_valid, meta, tile_cnt, tile_start = _plan(eid, n_tiles)
    n_valid = n_valid[0, :1]
    xs = _dispatch(tile_cnt[:, :N_EXPERTS].reshape(-1), tile_start[:, :N_EXPERTS].reshape(-1),
                   meta[0], meta[1], n_valid, lrow, h2, n_rows)
    Fh = w_expert_gate.shape[-1]
    ys = _experts(tile_expert[:, 0], tile_expert[:, 1], n_valid, xs,
                  w_expert_gate[0].reshape(N_EXPERTS, D, Fh),
                  w_expert_up[0].reshape(N_EXPERTS, D, Fh),
                  w_expert_down[0].reshape(N_EXPERTS, Fh, D))
    out = _combine(pos_t, x1, wts, p[0].reshape(T, -1), g_ple[0][None],
                   w_ple_gate[0], w_ple_proj[0], g_final[None], ys)
    return out.reshape(B, S, D)
```

```python
import functools
import math

import jax
import jax.numpy as jnp
from jax import lax
from jax.experimental import pallas as pl
from jax.experimental.pallas import tpu as pltpu

F32 = jnp.float32
BF16 = jnp.bfloat16
I32 = jnp.int32

EPS = 1e-6
N_HEADS = 8
N_KV_HEADS = 2
HEAD_DIM = 64
WINDOW = 128
ROPE_THETA = 10000.0
CONV_WIDTH = 4
LRU_C = 8.0
N_GROUPS = 4
EXPERTS_PER_GROUP = 8
N_EXPERTS = N_GROUPS * EXPERTS_PER_GROUP
NEG_INF = -1e30
LANES = 128
SUBLANES = 8

TM_INPROJ = 512
TM_PROJ = 512
TQ_ATTN = 512
TL_LRU = 256
TM_PLAN = 1024
TM_EXPERT = 512
EXPERT_FF_BLOCK = 256
TM_DISPATCH = 512
TM_COMBINE = 256
VMEM_LIMIT = 56 * 1024 * 1024


def _rms(x, g):
    ms = jnp.mean(x * x, axis=-1, keepdims=True)
    return x * lax.rsqrt(ms + EPS) * g


_FETCH_ONCE = dict(pipeline_mode=pl.Buffered(1))


def _cparams(sem):
    return pltpu.CompilerParams(dimension_semantics=sem, vmem_limit_bytes=VMEM_LIMIT)


def _inproj_kernel(x_ref, pos_ref, g_ref, w_ref, q_ref, k_ref, v_ref, lx_ref, lg_ref, w_b):
    @pl.when(pl.program_id(0) == 0)
    def _():
        w_b[...] = w_ref[...].astype(BF16)

    h = _rms(x_ref[...], g_ref[...]).astype(BF16)
    proj = jnp.dot(h, w_b[...], preferred_element_type=F32)
    lane = lax.broadcasted_iota(I32, (1, LANES), 1)
    fidx = (lax.broadcasted_iota(I32, (LANES, 1), 0) % (HEAD_DIM // 2)).astype(F32)
    inv_freq = jnp.exp(fidx * (-2.0 / HEAD_DIM * math.log(ROPE_THETA)))
    pos = pos_ref[0].astype(F32)
    ang = jnp.concatenate([(inv_freq * pos[c:c + 1, :]).T for c in range(pos.shape[0])], axis=0)
    cos = jnp.cos(ang)
    sin = jnp.sin(ang)
    first_half = (lane % HEAD_DIM) < (HEAD_DIM // 2)
    sin_signed = jnp.where(first_half, -sin, sin)

    def rope(t):
        partner = jnp.where(first_half, pltpu.roll(t, LANES - HEAD_DIM // 2, 1),
                            pltpu.roll(t, HEAD_DIM // 2, 1))
        return t * cos + partner * sin_signed

    attn_w = N_HEADS * HEAD_DIM
    scale = HEAD_DIM ** -0.5
    assert math.frexp(scale)[0] == 0.5, "score scale must be a power of two to fold into q exactly"
    for c in range(attn_w // LANES):
        q_ref[:, c * LANES:(c + 1) * LANES] = (
            rope(proj[:, c * LANES:(c + 1) * LANES]) * scale).astype(BF16)
    kv_w = N_KV_HEADS * HEAD_DIM
    assert kv_w == LANES
    k = rope(proj[:, attn_w:attn_w + kv_w])
    v = proj[:, attn_w + kv_w:attn_w + 2 * kv_w]
    k_ref[:, :kv_w] = k.astype(BF16)
    k_ref[:, kv_w:] = pltpu.roll(k, HEAD_DIM, 1).astype(BF16)
    v_ref[:, :kv_w] = v.astype(BF16)
    v_ref[:, kv_w:] = pltpu.roll(v, HEAD_DIM, 1).astype(BF16)
    off_lx = attn_w + 2 * kv_w
    lru_w = lx_ref.shape[1]
    lx_ref[...] = proj[:, off_lx:off_lx + lru_w]
    lg_ref[...] = proj[:, off_lx + lru_w:off_lx + 2 * lru_w]


def _inproj(x2, pos3, g_mix, w_in, lru_w):
    T, D = x2.shape
    tm = TM_INPROJ
    attn_w = N_HEADS * HEAD_DIM
    kv_w = N_KV_HEADS * HEAD_DIM
    row = lambda i: (i, 0)
    const = lambda i: (0, 0)
    return pl.pallas_call(
        _inproj_kernel,
        grid=(T // tm,),
        in_specs=[pl.BlockSpec((tm, D), row),
                  pl.BlockSpec((1, tm // LANES, LANES), lambda i: (i, 0, 0)),
                  pl.BlockSpec((1, D), const), pl.BlockSpec(w_in.shape, const, **_FETCH_ONCE)],
        out_specs=[pl.BlockSpec((tm, attn_w), row), pl.BlockSpec((tm, 2 * kv_w), row),
                   pl.BlockSpec((tm, 2 * kv_w), row), pl.BlockSpec((tm, lru_w), row),
                   pl.BlockSpec((tm, lru_w), row)],
        out_shape=[jax.ShapeDtypeStruct((T, attn_w), BF16),
                   jax.ShapeDtypeStruct((T, 2 * kv_w), BF16),
                   jax.ShapeDtypeStruct((T, 2 * kv_w), BF16), jax.ShapeDtypeStruct((T, lru_w), F32),
                   jax.ShapeDtypeStruct((T, lru_w), F32)],
        scratch_shapes=[pltpu.VMEM(w_in.shape, BF16)],
        compiler_params=_cparams(("arbitrary",)),
        name="inproj",
    )(x2, pos3, g_mix, w_in)


def _attn_kernel(sinks_ref, q_ref, k_ref, v_ref, kp_ref, vp_ref, g_ref, o_ref):
    n = pl.program_id(1)
    blk = WINDOW
    hd = HEAD_DIM
    nqb = q_ref.shape[0] // blk
    n_pairs = N_HEADS // 2
    pairs_per_group = n_pairs // N_KV_HEADS
    lane = lax.broadcasted_iota(I32, (1, LANES), 1)
    first = lane < hd
    qi = lax.broadcasted_iota(I32, (blk, 2 * blk), 0)
    kj = lax.broadcasted_iota(I32, (blk, 2 * blk), 1)
    rel = qi + blk - kj
    band = (rel >= 0) & (rel < WINDOW)
    bd_row = lax.broadcasted_iota(I32, (4 * blk, LANES), 0)
    bd_lane = lax.broadcasted_iota(I32, (4 * blk, LANES), 1)
    ones_bd = ((bd_row < 2 * blk) == (bd_lane < hd)).astype(BF16)
    zero = jnp.zeros((), BF16)

    def block_diag(t):
        same, swapped = t[:, :LANES], t[:, LANES:]
        out = []
        for g in range(N_KV_HEADS):
            top = jnp.where(first, same if g == 0 else swapped, zero)
            bot = jnp.where(first, zero, swapped if g == 0 else same)
            out.append(jnp.concatenate([top, bot], axis=0))
        return out

    for j in range(nqb):
        rows = slice(j * blk, (j + 1) * blk)
        if j == 0:
            k_prev, v_prev = kp_ref[...], vp_ref[...]
            mask = band & ((kj >= blk) | (n > 0))
        else:
            prev_rows = slice((j - 1) * blk, j * blk)
            k_prev, v_prev = k_ref[prev_rows, :], v_ref[prev_rows, :]
            mask = band
        k_bd = block_diag(jnp.concatenate([k_prev, k_ref[rows, :]], axis=0))
        v_bd = block_diag(jnp.concatenate([v_prev, v_ref[rows, :]], axis=0))
        outs = []
        for p in range(n_pairs):
            g = p // pairs_per_group
            qp = q_ref[rows, p * LANES:(p + 1) * LANES]
            s = lax.dot_general(qp, k_bd[g], (((1,), (1,)), ((), ())), preferred_element_type=F32)
            es, ms = [], []
            for a in range(2):
                sa = jnp.where(mask, s[:, a * 2 * blk:(a + 1) * 2 * blk], NEG_INF)
                m = jnp.maximum(jnp.max(sa, axis=-1, keepdims=True), sinks_ref[2 * p + a])
                es.append(jnp.exp(sa - m))
                ms.append(m)
            e = jnp.concatenate(es, axis=1).astype(BF16)
            od = jnp.dot(e, jnp.concatenate([v_bd[g], ones_bd], axis=1), preferred_element_type=F32)
            sink_term = jnp.where(first, jnp.exp(sinks_ref[2 * p] - ms[0]),
                                  jnp.exp(sinks_ref[2 * p + 1] - ms[1]))
            outs.append(od[:, :LANES] / (od[:, LANES:] + sink_term))
        attn = jnp.concatenate(outs, axis=1)
        o_ref[rows, :] = _rms(attn, g_ref[...]).astype(BF16)


def _attention(q, k, v, sinks, g_attn, B, S):
    T, attn_w = q.shape
    kv_w = k.shape[1]
    blk = WINDOW
    tq = TQ_ATTN
    nt = S // tq
    cur = lambda b, n: (b * nt + n, 0)
    prev = lambda b, n: (b * (S // blk) + jnp.maximum(n * (tq // blk) - 1, 0), 0)
    return pl.pallas_call(
        _attn_kernel,
        grid=(B, nt),
        in_specs=[pl.BlockSpec(memory_space=pltpu.SMEM),
                  pl.BlockSpec((tq, attn_w), cur),
                  pl.BlockSpec((tq, kv_w), cur), pl.BlockSpec((tq, kv_w), cur),
                  pl.BlockSpec((blk, kv_w), prev), pl.BlockSpec((blk, kv_w), prev),
                  pl.BlockSpec((1, attn_w), lambda b, n: (0, 0))],
        out_specs=pl.BlockSpec((tq, attn_w), cur),
        out_shape=jax.ShapeDtypeStruct((T, attn_w), BF16),
        compiler_params=_cparams(("arbitrary", "arbitrary")),
        name="attn",
    )(sinks, q, k, v, k, v, g_attn)


def _neg_expm1(x, exp_x):
    series = -x * (1.0 + x * (0.5 + x * (1.0 / 6.0 + x * (1.0 / 24.0))))
    return jnp.where(x > -0.02, series, 1.0 - exp_x)


def _gelu_tanh(x):
    return 0.5 * x * (1.0 + jnp.tanh(math.sqrt(2.0 / math.pi) * (x + 0.044715 * (x * x * x))))


def _lru_kernel(lx_ref, lg_ref, cw_ref, cb_ref, wa_ref, ba_ref, wx_ref, bx_ref, lam_ref, g_ref,
                o_ref, xpad, a_scr, b_scr, h_scr, h_carry):
    i = pl.program_id(0)
    B, tl, C = lx_ref.shape
    pad = SUBLANES

    @pl.when(i == 0)
    def _():
        xpad[:, 0:pad, :] = jnp.zeros((B, pad, C), F32)
        h_carry[...] = jnp.zeros_like(h_carry)

    @pl.when(i > 0)
    def _():
        xpad[:, 0:pad, :] = xpad[:, tl:tl + pad, :]

    xpad[:, pad:, :] = lx_ref[...]
    y = jnp.zeros((B, tl, C), F32) + cb_ref[...]
    for j in range(CONV_WIDTH):
        y = y + xpad[:, pl.ds(pad - (CONV_WIDTH - 1) + j, tl), :] * cw_ref[j:j + 1, :]
    y2 = y.reshape(B * tl, C)
    yb = y2.astype(BF16)
    gate_a = jnp.dot(yb, wa_ref[...], preferred_element_type=F32) + ba_ref[...]
    gate_x = jnp.dot(yb, wx_ref[...], preferred_element_type=F32) + bx_ref[...]
    r = jax.nn.sigmoid(gate_a)
    ig = jax.nn.sigmoid(gate_x)
    nl = -lam_ref[...]
    softplus = jnp.maximum(nl, 0.0) + jnp.log1p(jnp.exp(-jnp.abs(nl)))
    log_a = (-LRU_C) * r * softplus
    a_all = jnp.exp(log_a)
    one_minus_a2 = _neg_expm1(2.0 * log_a, a_all * a_all)
    b_all = one_minus_a2 * lax.rsqrt(jnp.maximum(one_minus_a2, 1e-30)) * ig * y2
    groups = (B * tl // SUBLANES, SUBLANES, C)
    sub = lax.broadcasted_iota(I32, groups, 1)
    a_cum, b_loc = a_all.reshape(groups), b_all.reshape(groups)
    d = 1
    while d < SUBLANES:
        keep = sub >= d
        a_prev = pltpu.roll(a_cum, d, 1)
        b_prev = pltpu.roll(b_loc, d, 1)
        b_loc = b_loc + jnp.where(keep, a_cum * b_prev, 0.0)
        a_cum = jnp.where(keep, a_cum * a_prev, a_cum)
        d *= 2
    a_scr[...] = a_cum.reshape(B, tl, C)
    b_scr[...] = b_loc.reshape(B, tl, C)

    def carry(g, h_prev):
        rows = pl.ds(pl.multiple_of(g * SUBLANES, SUBLANES), SUBLANES)
        h = b_scr[:, rows, :] + a_scr[:, rows, :] * h_prev
        h_scr[:, rows, :] = h
        return h[:, SUBLANES - 1:SUBLANES, :]

    h_carry[...] = lax.fori_loop(0, tl // SUBLANES, carry, h_carry[...], unroll=4)
    lru = h_scr[...].reshape(B * tl, C) * _gelu_tanh(lg_ref[...].reshape(B * tl, C))
    o_ref[...] = _rms(lru, g_ref[...]).astype(BF16).reshape(B, tl, C)


def _lru(lx3, lg3, conv_w, conv_b, wa_bd, ba, wx_bd, bx, lam, g_lru):
    B, S, C = lx3.shape
    tl = TL_LRU
    blk = lambda i: (0, i, 0)
    const = lambda i: (0, 0)
    vec = pl.BlockSpec((1, C), const)
    return pl.pallas_call(
        _lru_kernel,
        grid=(S // tl,),
        in_specs=[pl.BlockSpec((B, tl, C), blk), pl.BlockSpec((B, tl, C), blk),
                  pl.BlockSpec((CONV_WIDTH, C), const), vec,
                  pl.BlockSpec((C, C), const), vec, pl.BlockSpec((C, C), const), vec, vec, vec],
        out_specs=pl.BlockSpec((B, tl, C), blk),
        out_shape=jax.ShapeDtypeStruct((B, S, C), BF16),
        scratch_shapes=[pltpu.VMEM((B, tl + SUBLANES, C), F32),
                        pltpu.VMEM((B, tl, C), F32), pltpu.VMEM((B, tl, C), F32),
                        pltpu.VMEM((B, tl, C), F32), pltpu.VMEM((B, 1, C), F32)],
        compiler_params=_cparams(("arbitrary",)),
        name="lru",
    )(lx3, lg3, conv_w, conv_b, wa_bd, ba, wx_bd, bx, lam, g_lru)


def _outproj_kernel(x_ref, ma_ref, mb_ref, wo_ref, g_ref, wr_ref, br_ref,
                    x1_ref, h2_ref, eid_ref, wt_ref, wo_b):
    @pl.when(pl.program_id(0) == 0)
    def _():
        wo_b[...] = wo_ref[...].astype(BF16)

    half = ma_ref.shape[1]
    x1 = (x_ref[...]
          + jnp.dot(ma_ref[...], wo_b[0:half, :], preferred_element_type=F32)
          + jnp.dot(mb_ref[...], wo_b[half:, :], preferred_element_type=F32))
    x1_ref[...] = x1
    h2 = _rms(x1, g_ref[...])
    h2_ref[...] = h2.astype(BF16).reshape(h2_ref.shape)
    h_hi = h2.astype(BF16)
    h_lo = (h2 - h_hi.astype(F32)).astype(BF16)
    hw = jnp.dot(h_hi, wr_ref[...], preferred_element_type=F32)
    logits = (hw[:, :LANES] + hw[:, LANES:]
              + jnp.dot(h_lo, wr_ref[:, :LANES], preferred_element_type=F32) + br_ref[...])
    tm = logits.shape[0]
    lane = lax.broadcasted_iota(I32, (tm, LANES), 1)
    lane_f = lane.astype(F32)
    big = float(LANES)

    def first_argmax(vals):
        m = jnp.max(vals, axis=-1, keepdims=True)
        idx = jnp.min(jnp.where(vals == m, lane_f, big), axis=-1, keepdims=True)
        return m, idx.astype(I32)

    gl = jnp.where(lane < N_GROUPS, logits, NEG_INF)
    gmax, gidx = first_argmax(gl)
    gsum = jnp.sum(jnp.where(lane < N_GROUPS, jnp.exp(gl - gmax), 0.0), axis=-1, keepdims=True)
    g_top_p = 1.0 / gsum
    lo = N_GROUPS + EXPERTS_PER_GROUP * gidx
    el = jnp.where((lane >= lo) & (lane < lo + EXPERTS_PER_GROUP), logits, NEG_INF)
    m1, i1 = first_argmax(el)
    el2 = jnp.where(lane == i1, NEG_INF, el)
    m2, i2 = first_argmax(el2)
    ratio = jnp.exp(m2 - m1)
    w1 = g_top_p / (1.0 + ratio)
    w2 = g_top_p * ratio / (1.0 + ratio)
    eid_ref[...] = jnp.where(lane == 0, i1 - N_GROUPS, jnp.where(lane == 1, i2 - N_GROUPS, 0))
    wt_ref[...] = jnp.where(lane == 0, w1, jnp.where(lane == 1, w2, 0.0))


def _outproj(x2, mix_a, mix_b, w_out, g_ffn, w_router, b_router):
    T, D = x2.shape
    tm = TM_PROJ
    half = mix_a.shape[1]
    row = lambda i: (i, 0)
    const = lambda i: (0, 0)
    return pl.pallas_call(
        _outproj_kernel,
        grid=(T // tm,),
        in_specs=[pl.BlockSpec((tm, D), row), pl.BlockSpec((tm, half), row),
                  pl.BlockSpec((tm, half), row), pl.BlockSpec(w_out.shape, const, **_FETCH_ONCE),
                  pl.BlockSpec((1, D), const), pl.BlockSpec((D, 2 * LANES), const),
                  pl.BlockSpec((1, LANES), const)],
        out_specs=[pl.BlockSpec((tm, D), row),
                   pl.BlockSpec((tm, D // LANES, LANES), lambda i: (i, 0, 0)),
                   pl.BlockSpec((tm, LANES), row), pl.BlockSpec((tm, LANES), row)],
        out_shape=[jax.ShapeDtypeStruct((T, D), F32),
                   jax.ShapeDtypeStruct((T, D // LANES, LANES), BF16),
                   jax.ShapeDtypeStruct((T, LANES), I32), jax.ShapeDtypeStruct((T, LANES), F32)],
        scratch_shapes=[pltpu.VMEM(w_out.shape, BF16)],
        compiler_params=_cparams(("arbitrary",)),
        name="outproj",
    )(x2, mix_a, mix_b, w_out, g_ffn, w_router, b_router)


def _lane_cumsum(x):
    lane = lax.broadcasted_iota(I32, x.shape, 1)
    shift = 1
    while shift < LANES:
        x = x + jnp.where(lane >= shift, pltpu.roll(x, shift, 1), 0)
        shift *= 2
    return x


def _plan_kernel(eid_ref, pos_ref, lrow_ref, te_ref, nv_ref, meta_ref, tcnt_ref, tstart_ref,
                 rank_scr, cnt_scr, start_scr, tcnt_scr, tpre_scr):
    phase = pl.program_id(0)
    i = pl.program_id(1)
    tm = eid_ref.shape[0]
    td = TM_DISPATCH
    nsub = tm // td
    lane = lax.broadcasted_iota(I32, (tm, LANES), 1)
    eid = eid_ref[...]
    f1 = eid[:, 0:1]
    f2 = eid[:, 1:2]
    sel1 = lane == f1
    sel2 = lane == f2

    @pl.when((phase == 0) & (i == 0))
    def _():
        cnt_scr[...] = jnp.zeros_like(cnt_scr)

    @pl.when(phase == 0)
    def _():
        onehot = (sel1 | sel2).astype(BF16)
        r = lax.broadcasted_iota(I32, (tm, tm), 0)
        c = lax.broadcasted_iota(I32, (tm, tm), 1)
        strict_lower = (r > c).astype(BF16)
        within = jnp.dot(strict_lower, onehot, preferred_element_type=F32)
        base = cnt_scr[0:1, :]
        before = within + base
        r1 = jnp.sum(jnp.where(sel1, before, 0.0), axis=-1, keepdims=True)
        r2 = jnp.sum(jnp.where(sel2, before, 0.0), axis=-1, keepdims=True)
        rank = jnp.where(lane == 0, r1, jnp.where(lane == 1, r2, 0.0)).astype(I32)
        rank_scr[pl.ds(pl.multiple_of(i * tm, tm), tm), :] = rank
        onehot_f = onehot.astype(F32)
        row = lax.broadcasted_iota(I32, (tm, LANES), 0)
        srow = lax.broadcasted_iota(I32, (SUBLANES, LANES), 0)
        counts = [jnp.sum(onehot_f[s * td:(s + 1) * td], axis=0, keepdims=True) for s in range(nsub)]
        cnt_mat = jnp.zeros((SUBLANES, LANES), F32)
        seen = jnp.zeros((1, LANES), F32)
        local_before = within
        for s in range(nsub):
            tile = i * nsub + s
            tcnt_scr[pl.ds(tile, 1), :] = counts[s]
            tpre_scr[pl.ds(tile, 1), :] = base + seen
            cnt_mat = jnp.where(srow == s, counts[s], cnt_mat)
            if s > 0:
                local_before = jnp.where(row >= s * td, within - seen, local_before)
            seen = seen + counts[s]
        cnt_i = cnt_mat.astype(I32)
        run_start = (_lane_cumsum(cnt_i) - cnt_i).astype(F32)
        local = local_before + run_start[0:1, :]
        for s in range(1, nsub):
            local = jnp.where(row >= s * td, local_before + run_start[s:s + 1, :], local)
        l1 = jnp.sum(jnp.where(sel1, local, 0.0), axis=-1, keepdims=True)
        l2 = jnp.sum(jnp.where(sel2, local, 0.0), axis=-1, keepdims=True)
        lrow_ref[...] = jnp.where(lane == 0, l1, jnp.where(lane == 1, l2, 0.0)).astype(I32)
        cnt_scr[...] = cnt_scr[...] + seen

    @pl.when(phase == 1)
    def _():
        starts = start_scr[0:1, :]
        o1 = jnp.sum(jnp.where(sel1, starts, 0.0), axis=-1, keepdims=True).astype(I32)
        o2 = jnp.sum(jnp.where(sel2, starts, 0.0), axis=-1, keepdims=True).astype(I32)
        rank = rank_scr[pl.ds(pl.multiple_of(i * tm, tm), tm), :]
        pos = rank + jnp.where(lane == 0, o1, jnp.where(lane == 1, o2, 0))
        pos_ref[...] = pos.astype(F32).T[0:SUBLANES, :].astype(I32)

    @pl.when((phase == 0) & (i == pl.num_programs(1) - 1))
    def _():
        shift = int(math.log2(TM_EXPERT))
        cnt = cnt_scr[...].astype(I32)
        padded = ((cnt + (TM_EXPERT - 1)) >> shift) << shift
        ends = _lane_cumsum(padded)
        start_scr[...] = (ends - padded).astype(F32)
        nt = te_ref.shape[0]
        end_tile = (ends >> shift)[0:1, :]
        tl_lane = lax.broadcasted_iota(I32, (nt, LANES), 1)
        tile = lax.broadcasted_iota(I32, (nt, LANES), 0)
        done = jnp.where((tl_lane < N_EXPERTS) & (end_tile <= tile), 1.0, 0.0)
        te = jnp.minimum(jnp.sum(done, axis=-1, keepdims=True).astype(I32), N_EXPERTS - 1)
        n_valid = jnp.sum(jnp.where(tl_lane[0:1, :] == N_EXPERTS - 1, end_tile, 0).astype(F32),
                          axis=-1, keepdims=True).astype(I32)
        nv_ref[...] = jnp.broadcast_to(n_valid, nv_ref.shape)
        seg_end = jnp.sum(jnp.where(tl_lane == te, end_tile, 0).astype(F32), axis=-1,
                          keepdims=True).astype(I32)
        nxt = jnp.sum(jnp.where((tl_lane < N_EXPERTS) & (end_tile <= seg_end), 1.0, 0.0), axis=-1,
                      keepdims=True).astype(I32)
        nxt = jnp.where(seg_end < n_valid, jnp.minimum(nxt, N_EXPERTS - 1), -1)
        te_ref[...] = jnp.where(tl_lane == 1, nxt, te)
        row = lax.broadcasted_iota(I32, cnt.shape, 0)
        meta_ref[...] = jnp.where(row == 0, ends - padded + cnt, jnp.where(row == 1, padded - cnt, 0))
        tcnt_ref[...] = tcnt_scr[...].astype(I32)
        tstart_ref[...] = (tpre_scr[...] + (ends - padded).astype(F32)[0:1, :]).astype(I32)


def _plan(eid, n_tiles):
    T = eid.shape[0]
    tm = TM_PLAN
    steps = T // tm
    n_dtiles = T // TM_DISPATCH
    const = lambda p, i: (0, 0)
    return pl.pallas_call(
        _plan_kernel,
        grid=(2, steps),
        in_specs=[pl.BlockSpec((tm, LANES), lambda p, i: (i, 0))],
        out_specs=[pl.BlockSpec((SUBLANES, tm), lambda p, i: (0, i * p)),
                   pl.BlockSpec((tm, LANES), lambda p, i: (i * (1 - p) + (steps - 1) * p, 0)),
                   pl.BlockSpec((n_tiles, LANES), const), pl.BlockSpec((SUBLANES, LANES), const),
                   pl.BlockSpec((SUBLANES, LANES), const),
                   pl.BlockSpec((n_dtiles, LANES), const), pl.BlockSpec((n_dtiles, LANES), const)],
        out_shape=[jax.ShapeDtypeStruct((SUBLANES, T), I32), jax.ShapeDtypeStruct((T, LANES), I32),
                   jax.ShapeDtypeStruct((n_tiles, LANES), I32),
                   jax.ShapeDtypeStruct((SUBLANES, LANES), I32),
                   jax.ShapeDtypeStruct((SUBLANES, LANES), I32),
                   jax.ShapeDtypeStruct((n_dtiles, LANES), I32),
                   jax.ShapeDtypeStruct((n_dtiles, LANES), I32)],
        scratch_shapes=[pltpu.VMEM((T, LANES), I32), pltpu.VMEM((SUBLANES, LANES), F32),
                        pltpu.VMEM((SUBLANES, LANES), F32),
                        pltpu.VMEM((n_dtiles, LANES), F32), pltpu.VMEM((n_dtiles, LANES), F32)],
        compiler_params=_cparams(("arbitrary", "arbitrary")),
        name="plan",
    )(eid)


_PAD_BITS = tuple(1 << b for b in reversed(range(int(math.log2(TM_EXPERT)))))


_RUN_BITS = tuple(1 << b for b in reversed(range(int(math.log2(TM_DISPATCH)) + 1)))


def _dispatch_kernel(tcnt_ref, tstart_ref, seg_end_ref, seg_pad_ref, nv_ref, lrow_ref, h_ref,
                     xs_ref, lbuf_a, lbuf_b, zeros, sem, zsem):
    i = pl.program_id(0)
    n = pl.num_programs(0)
    td = h_ref.shape[0]
    nr = 2 * td

    def wait_buf(buf, s):
        pltpu.make_async_copy(buf, xs_ref.at[pl.ds(0, nr)], sem.at[s]).wait()

    def step(cur, cur_s, prev, prev_s):
        @pl.when(i >= 2)
        def _():
            wait_buf(cur, cur_s)

        local_row = lrow_ref[...].astype(F32).T
        r = lax.broadcasted_iota(I32, (nr, td), 0).astype(F32)
        onehot = ((r == local_row[0:1, :]) | (r == local_row[1:2, :])).astype(BF16)
        sorted_rows = jnp.dot(onehot, h_ref[...].reshape(td, -1), preferred_element_type=F32)
        cur[...] = sorted_rows.astype(BF16).reshape(cur.shape)

        table_row = jnp.maximum(i - 1, 0) * N_EXPERTS
        lo = 0
        for e in range(N_EXPERTS):
            cnt = jnp.where(i >= 1, tcnt_ref[table_row + e], 0)
            dst = tstart_ref[table_row + e]
            for bit in _RUN_BITS:
                off = cnt & ~(2 * bit - 1)

                @pl.when((cnt & bit) != 0)
                def _():
                    pltpu.make_async_copy(prev.at[pl.ds(lo + off, bit)],
                                          xs_ref.at[pl.ds(dst + off, bit)], sem.at[prev_s]).start()
            lo = lo + cnt

        @pl.when(i == n - 1)
        def _():
            wait_buf(prev, prev_s)

    pl.when(i % 2 == 0)(functools.partial(step, lbuf_a, 0, lbuf_b, 1))
    pl.when(i % 2 == 1)(functools.partial(step, lbuf_b, 1, lbuf_a, 0))

    @pl.when(i == n - 1)
    def _():
        zeros[...] = jnp.zeros_like(zeros)
        n_tiles = xs_ref.shape[0] // TM_EXPERT
        first_tail = n_tiles - N_EXPERTS

        def pad_copy(e, bit):
            n = seg_pad_ref[e]
            dst = seg_end_ref[e] + (n & ~(2 * bit - 1))
            return (n & bit) != 0, pltpu.make_async_copy(zeros.at[pl.ds(0, bit)],
                                                         xs_ref.at[pl.ds(dst, bit)], zsem)

        def tail_copy(j):
            return j >= nv_ref[0], pltpu.make_async_copy(
                zeros, xs_ref.at[pl.ds(j * TM_EXPERT, TM_EXPERT)], zsem)

        def for_each_fill(act):
            def per_expert(e, c):
                for bit in _PAD_BITS:
                    pred, cp = pad_copy(e, bit)
                    pl.when(pred)(functools.partial(act, cp))
                return c

            lax.fori_loop(0, N_EXPERTS, per_expert, 0)

            def per_tile(j, c):
                pred, cp = tail_copy(j)
                pl.when(pred)(functools.partial(act, cp))
                return c

            lax.fori_loop(first_tail, n_tiles, per_tile, 0)

        for_each_fill(lambda cp: cp.start())
        for_each_fill(lambda cp: cp.wait())


def _dispatch(tile_cnt, tile_start, seg_end, seg_pad, n_valid, lrow, h2, n_rows):
    T = h2.shape[0]
    slab = h2.shape[1:]
    td = TM_DISPATCH
    nt = T // td
    smem = pl.BlockSpec(memory_space=pltpu.SMEM)
    return pl.pallas_call(
        _dispatch_kernel,
        grid=(nt + 1,),
        in_specs=[smem, smem, smem, smem, smem,
                  pl.BlockSpec((td, LANES), lambda i: (jnp.minimum(i, nt - 1), 0)),
                  pl.BlockSpec((td,) + slab, lambda i: (jnp.minimum(i, nt - 1), 0, 0))],
        out_specs=pl.BlockSpec(memory_space=pl.ANY),
        out_shape=jax.ShapeDtypeStruct((n_rows,) + slab, h2.dtype),
        scratch_shapes=[pltpu.VMEM((2 * td,) + slab, h2.dtype),
                        pltpu.VMEM((2 * td,) + slab, h2.dtype),
                        pltpu.VMEM((TM_EXPERT,) + slab, h2.dtype),
                        pltpu.SemaphoreType.DMA((2,)), pltpu.SemaphoreType.DMA(())],
        compiler_params=_cparams(("arbitrary",)),
        name="dispatch",
    )(tile_cnt, tile_start, seg_end, seg_pad, n_valid, lrow, h2)


def _experts_kernel(te_ref, tn_ref, nv_ref, xs_ref, wg_ref, wu_ref, wd_ref, ys_ref,
                    wg_f, wu_f, wd_f, wg_b, wu_b, wd_b, slot_ref, sem):
    j = pl.program_id(0)
    valid = j < nv_ref[0]
    new_expert = valid & ((j == 0) | (te_ref[j] != te_ref[jnp.maximum(j - 1, 0)]))

    def fetch(e, s):
        return (pltpu.make_async_copy(wg_ref.at[e], wg_f.at[s], sem.at[s]),
                pltpu.make_async_copy(wu_ref.at[e], wu_f.at[s], sem.at[s]),
                pltpu.make_async_copy(wd_ref.at[e], wd_f.at[s], sem.at[s]))

    @pl.when(j == 0)
    def _():
        slot_ref[0] = 0
        for cp in fetch(te_ref[0], 0):
            cp.start()

    @pl.when(new_expert)
    def _():
        s = slot_ref[0]
        for cp in fetch(te_ref[j], s):
            cp.wait()
        nxt = tn_ref[j]

        @pl.when(nxt >= 0)
        def _():
            for cp in fetch(nxt, 1 - s):
                cp.start()

        wg_b[...] = wg_f[s].astype(BF16)
        wu_b[...] = wu_f[s].astype(BF16)
        wd_b[...] = wd_f[s].astype(BF16)
        slot_ref[0] = 1 - s

    @pl.when(valid)
    def _():
        x = xs_ref[...].reshape(xs_ref.shape[0], -1)
        y = None
        for lo in range(0, wg_b.shape[1], EXPERT_FF_BLOCK):
            cols = slice(lo, lo + EXPERT_FF_BLOCK)
            a = jnp.dot(x, wg_b[:, cols], preferred_element_type=F32)
            u = jnp.dot(x, wu_b[:, cols], preferred_element_type=F32)
            hid = (a * jax.nn.sigmoid(a) * u).astype(BF16)
            part = jnp.dot(hid, wd_b[cols, :], preferred_element_type=F32)
            y = part if y is None else y + part
        ys_ref[...] = y.astype(BF16).reshape(ys_ref.shape)

    @pl.when(j >= nv_ref[0])
    def _():
        ys_ref[...] = jnp.zeros_like(ys_ref)


def _experts(tile_expert, tile_next, n_valid, xs, wg, wu, wd):
    R = xs.shape[0]
    slab = xs.shape[1:]
    E, D, Fh = wg.shape
    tm = TM_EXPERT
    n_tiles = R // tm
    rows = lambda j, te, tn, nv: (j, 0, 0)
    rows_in = lambda j, te, tn, nv: (jnp.minimum(j, nv[0] - 1), 0, 0)
    hbm = pl.BlockSpec(memory_space=pl.ANY)
    return pl.pallas_call(
        _experts_kernel,
        grid_spec=pltpu.PrefetchScalarGridSpec(
            num_scalar_prefetch=3,
            grid=(n_tiles,),
            in_specs=[pl.BlockSpec((tm,) + slab, rows_in), hbm, hbm, hbm],
            out_specs=pl.BlockSpec((tm,) + slab, rows),
            scratch_shapes=[pltpu.VMEM((2, D, Fh), F32), pltpu.VMEM((2, D, Fh), F32),
                            pltpu.VMEM((2, Fh, D), F32),
                            pltpu.VMEM((D, Fh), BF16), pltpu.VMEM((D, Fh), BF16),
                            pltpu.VMEM((Fh, D), BF16),
                            pltpu.SMEM((1,), I32), pltpu.SemaphoreType.DMA((2,))]),
        out_shape=jax.ShapeDtypeStruct(xs.shape, xs.dtype),
        compiler_params=_cparams(("arbitrary",)),
        name="experts",
    )(tile_expert, tile_next, n_valid, xs, wg, wu, wd)


def _combine_kernel(pos0_ref, pos1_ref, pos0_next_ref, pos1_next_ref, x1_ref, wt_ref, p_ref, gp_ref,
                    wpg_ref, wpp_ref, gf_ref, ys_ref, o_ref, buf, wpg_b, wpp_b, sem):
    i = pl.program_id(0)
    n = pl.num_programs(0)
    tm = x1_ref.shape[0]
    slot = i % 2

    def gather(idx_refs, s):
        def issue(t, c):
            for k in range(2):
                pltpu.make_async_copy(ys_ref.at[pl.ds(idx_refs[k][t], 1)],
                                      buf.at[s, k, pl.ds(t, 1)], sem.at[s]).start()
            return c

        lax.fori_loop(0, tm, issue, 0, unroll=8)

    @pl.when(i == 0)
    def _():
        gather((pos0_ref, pos1_ref), 0)
        wpg_b[...] = wpg_ref[...].astype(BF16)
        wpp_b[...] = wpp_ref[...].astype(BF16)

    for s in range(2):
        @pl.when((i + 1 < n) & (slot == 1 - s))
        def _():
            gather((pos0_next_ref, pos1_next_ref), s)

    for k in range(2):
        pltpu.make_async_copy(ys_ref.at[pl.ds(0, tm)], buf.at[slot, k], sem.at[slot]).wait()
    wt = wt_ref[...]
    y0 = buf[slot, 0].reshape(tm, -1).astype(F32)
    y1 = buf[slot, 1].reshape(tm, -1).astype(F32)
    x2 = x1_ref[...] + wt[:, 0:1] * y0 + wt[:, 1:2] * y1
    hp = _rms(x2, gp_ref[...]).astype(BF16)
    gate = jax.nn.sigmoid(jnp.dot(hp, wpg_b[...], preferred_element_type=F32))
    proj = jnp.dot(p_ref[...].astype(BF16), wpp_b[...], preferred_element_type=F32)
    x3 = x2 + gate * proj
    o_ref[...] = _rms(x3, gf_ref[...])


def _combine(pos_t, x1, wts, p2, g_ple, w_ple_gate, w_ple_proj, g_final, ys):
    T, D = x1.shape
    P = p2.shape[1]
    tm = TM_COMBINE
    n = T // tm
    row = lambda i: (i, 0)
    const = lambda i: (0, 0)
    cur = pl.BlockSpec((tm,), lambda i: (i,), memory_space=pltpu.SMEM)
    nxt = pl.BlockSpec((tm,), lambda i: (jnp.minimum(i + 1, n - 1),), memory_space=pltpu.SMEM)
    return pl.pallas_call(
        _combine_kernel,
        grid=(n,),
        in_specs=[cur, cur, nxt, nxt,
                  pl.BlockSpec((tm, D), row), pl.BlockSpec((tm, LANES), row),
                  pl.BlockSpec((tm, P), row), pl.BlockSpec((1, D), const),
                  pl.BlockSpec((D, D), const, **_FETCH_ONCE), pl.BlockSpec((P, D), const, **_FETCH_ONCE),
                  pl.BlockSpec((1, D), const), pl.BlockSpec(memory_space=pl.ANY)],
        out_specs=pl.BlockSpec((tm, D), row),
        out_shape=jax.ShapeDtypeStruct((T, D), F32),
        scratch_shapes=[pltpu.VMEM((2, 2, tm) + ys.shape[1:], ys.dtype),
                        pltpu.VMEM((D, D), BF16), pltpu.VMEM((P, D), BF16),
                        pltpu.SemaphoreType.DMA((2,))],
        compiler_params=_cparams(("arbitrary",)),
        name="combine",
    )(pos_t[0], pos_t[1], pos_t[0], pos_t[1], x1, wts, p2, g_ple, w_ple_gate, w_ple_proj,
      g_final, ys)


def _block_diag(w):
    nb, d, _ = w.shape
    eye = jnp.eye(nb, dtype=w.dtype)
    return (eye[:, None, :, None] * w[:, :, None, :]).reshape(nb * d, nb * d)


def kernel(x, p, positions, g_mix, w_in, sinks, conv_w, conv_b, lru_wa, lru_ba, lru_wx, lru_bx,
           lru_lambda, g_attn_out, g_lru_out, w_out, g_ffn, w_router_group, b_router_group,
           w_router_expert, b_router_expert, w_expert_gate, w_expert_up, w_expert_down, g_ple,
           w_ple_gate, w_ple_proj, g_final):
    B, S, D = x.shape
    assert w_in.shape[0] == 1, "single-layer block only"
    T = B * S
    lru_w = conv_w.shape[-1]
    n_rows = 2 * T + N_EXPERTS * TM_EXPERT
    n_tiles = n_rows // TM_EXPERT
    pos3 = positions.reshape(T // TM_INPROJ, TM_INPROJ // LANES, LANES).astype(I32)
    x2 = x.reshape(T, D)
    q, k, v, lx, lg = _inproj(x2, pos3, g_mix[0][None], w_in[0], lru_w)
    mix_a = _attention(q, k, v, sinks[0], g_attn_out[0][None], B, S)
    mix_b = _lru(lx.reshape(B, S, lru_w), lg.reshape(B, S, lru_w), conv_w[0], conv_b[0][None],
                 _block_diag(lru_wa[0]).astype(BF16), lru_ba[0][None],
                 _block_diag(lru_wx[0]).astype(BF16), lru_bx[0][None],
                 lru_lambda[0][None], g_lru_out[0][None]).reshape(T, lru_w)
    n_router = N_GROUPS + N_EXPERTS
    w_router = jnp.pad(jnp.concatenate([w_router_group[0], w_router_expert[0]], axis=1),
                       ((0, 0), (0, LANES - n_router)))
    b_router = jnp.pad(jnp.concatenate([b_router_group[0], b_router_expert[0]]),
                       (0, LANES - n_router))[None]
    w_router_hi = w_router.astype(BF16)
    w_router_lo = (w_router - w_router_hi.astype(F32)).astype(BF16)
    x1, h2, eid, wts = _outproj(x2, mix_a, mix_b, w_out[0], g_ffn[0][None],
                                jnp.concatenate([w_router_hi, w_router_lo], axis=1), b_router)
    pos_t, lrow, tile_expert, n_valid, meta, tile_cnt, tile_start = _plan(eid, n_tiles)
    n_valid = n_valid[0, :1]
    xs = _dispatch(tile_cnt[:, :N_EXPERTS].reshape(-1), tile_start[:, :N_EXPERTS].reshape(-1),
                   meta[0], meta[1], n_valid, lrow, h2, n_rows)
    Fh = w_expert_gate.shape[-1]
    ys = _experts(tile_expert[:, 0], tile_expert[:, 1], n_valid, xs,
                  w_expert_gate[0].reshape(N_EXPERTS, D, Fh),
                  w_expert_up[0].reshape(N_EXPERTS, D, Fh),
                  w_expert_down[0].reshape(N_EXPERTS, Fh, D))
    out = _combine(pos_t, x1, wts, p[0].reshape(T, -1), g_ple[0][None],
                   w_ple_gate[0], w_ple_proj[0], g_final[None], ys)
    return out.reshape(B, S, D)
```

```python
import functools
import math

import jax
import jax.numpy as jnp
from jax import lax
from jax.experimental import pallas as pl
from jax.experimental.pallas import tpu as pltpu

F32 = jnp.float32
BF16 = jnp.bfloat16
I32 = jnp.int32

EPS = 1e-6
N_HEADS = 8
N_KV_HEADS = 2
HEAD_DIM = 64
WINDOW = 128
ROPE_THETA = 10000.0
CONV_WIDTH = 4
LRU_C = 8.0
N_GROUPS = 4
EXPERTS_PER_GROUP = 8
N_EXPERTS = N_GROUPS * EXPERTS_PER_GROUP
NEG_INF = -1e30
LANES = 128
SUBLANES = 8

TM_INPROJ = 512
TM_PROJ = 512
TQ_ATTN = 512
TL_LRU = 256
TM_PLAN = 1024
TM_EXPERT = 512
EXPERT_FF_BLOCK = 256
TM_DISPATCH = 512
TM_COMBINE = 256
VMEM_LIMIT = 56 * 1024 * 1024


def _rms(x, g):
    ms = jnp.mean(x * x, axis=-1, keepdims=True)
    return x * lax.rsqrt(ms + EPS) * g


_FETCH_ONCE = dict(pipeline_mode=pl.Buffered(1))


def _cparams(sem):
    return pltpu.CompilerParams(dimension_semantics=sem, vmem_limit_bytes=VMEM_LIMIT)


def _inproj_kernel(x_ref, pos_ref, g_ref, w_ref, q_ref, k_ref, v_ref, lx_ref, lg_ref, w_b):
    @pl.when(pl.program_id(0) == 0)
    def _():
        w_b[...] = w_ref[...].astype(BF16)

    h = _rms(x_ref[...], g_ref[...]).astype(BF16)
    proj = jnp.dot(h, w_b[...], preferred_element_type=F32)
    lane = lax.broadcasted_iota(I32, (1, LANES), 1)
    fidx = (lax.broadcasted_iota(I32, (LANES, 1), 0) % (HEAD_DIM // 2)).astype(F32)
    inv_freq = jnp.exp(fidx * (-2.0 / HEAD_DIM * math.log(ROPE_THETA)))
    pos = pos_ref[0].astype(F32)
    ang = jnp.concatenate([(inv_freq * pos[c:c + 1, :]).T for c in range(pos.shape[0])], axis=0)
    cos = jnp.cos(ang)
    sin = jnp.sin(ang)
    first_half = (lane % HEAD_DIM) < (HEAD_DIM // 2)
    sin_signed = jnp.where(first_half, -sin, sin)

    def rope(t):
        partner = jnp.where(first_half, pltpu.roll(t, LANES - HEAD_DIM // 2, 1),
                            pltpu.roll(t, HEAD_DIM // 2, 1))
        return t * cos + partner * sin_signed

    attn_w = N_HEADS * HEAD_DIM
    scale = HEAD_DIM ** -0.5
    assert math.frexp(scale)[0] == 0.5, "score scale must be a power of two to fold into q exactly"
    for c in range(attn_w // LANES):
        q_ref[:, c * LANES:(c + 1) * LANES] = (
            rope(proj[:, c * LANES:(c + 1) * LANES]) * scale).astype(BF16)
    kv_w = N_KV_HEADS * HEAD_DIM
    assert kv_w == LANES
    k = rope(proj[:, attn_w:attn_w + kv_w])
    v = proj[:, attn_w + kv_w:attn_w + 2 * kv_w]
    k_ref[:, :kv_w] = k.astype(BF16)
    k_ref[:, kv_w:] = pltpu.roll(k, HEAD_DIM, 1).astype(BF16)
    v_ref[:, :kv_w] = v.astype(BF16)
    v_ref[:, kv_w:] = pltpu.roll(v, HEAD_DIM, 1).astype(BF16)
    off_lx = attn_w + 2 * kv_w
    lru_w = lx_ref.shape[1]
    lx_ref[...] = proj[:, off_lx:off_lx + lru_w]
    lg_ref[...] = proj[:, off_lx + lru_w:off_lx + 2 * lru_w]


def _inproj(x2, pos3, g_mix, w_in, lru_w):
    T, D = x2.shape
    tm = TM_INPROJ
    attn_w = N_HEADS * HEAD_DIM
    kv_w = N_KV_HEADS * HEAD_DIM
    row = lambda i: (i, 0)
    const = lambda i: (0, 0)
    return pl.pallas_call(
        _inproj_kernel,
        grid=(T // tm,),
        in_specs=[pl.BlockSpec((tm, D), row),
                  pl.BlockSpec((1, tm // LANES, LANES), lambda i: (i, 0, 0)),
                  pl.BlockSpec((1, D), const), pl.BlockSpec(w_in.shape, const, **_FETCH_ONCE)],
        out_specs=[pl.BlockSpec((tm, attn_w), row), pl.BlockSpec((tm, 2 * kv_w), row),
                   pl.BlockSpec((tm, 2 * kv_w), row), pl.BlockSpec((tm, lru_w), row),
                   pl.BlockSpec((tm, lru_w), row)],
        out_shape=[jax.ShapeDtypeStruct((T, attn_w), BF16),
                   jax.ShapeDtypeStruct((T, 2 * kv_w), BF16),
                   jax.ShapeDtypeStruct((T, 2 * kv_w), BF16), jax.ShapeDtypeStruct((T, lru_w), F32),
                   jax.ShapeDtypeStruct((T, lru_w), F32)],
        scratch_shapes=[pltpu.VMEM(w_in.shape, BF16)],
        compiler_params=_cparams(("arbitrary",)),
        name="inproj",
    )(x2, pos3, g_mix, w_in)


def _attn_kernel(sinks_ref, q_ref, k_ref, v_ref, kp_ref, vp_ref, g_ref, o_ref):
    n = pl.program_id(1)
    blk = WINDOW
    hd = HEAD_DIM
    nqb = q_ref.shape[0] // blk
    n_pairs = N_HEADS // 2
    pairs_per_group = n_pairs // N_KV_HEADS
    lane = lax.broadcasted_iota(I32, (1, LANES), 1)
    first = lane < hd
    qi = lax.broadcasted_iota(I32, (blk, 2 * blk), 0)
    kj = lax.broadcasted_iota(I32, (blk, 2 * blk), 1)
    rel = qi + blk - kj
    band = (rel >= 0) & (rel < WINDOW)
    bd_row = lax.broadcasted_iota(I32, (4 * blk, LANES), 0)
    bd_lane = lax.broadcasted_iota(I32, (4 * blk, LANES), 1)
    ones_bd = ((bd_row < 2 * blk) == (bd_lane < hd)).astype(BF16)
    zero = jnp.zeros((), BF16)

    def block_diag(t):
        same, swapped = t[:, :LANES], t[:, LANES:]
        out = []
        for g in range(N_KV_HEADS):
            top = jnp.where(first, same if g == 0 else swapped, zero)
            bot = jnp.where(first, zero, swapped if g == 0 else same)
            out.append(jnp.concatenate([top, bot], axis=0))
        return out

    for j in range(nqb):
        rows = slice(j * blk, (j + 1) * blk)
        if j == 0:
            k_prev, v_prev = kp_ref[...], vp_ref[...]
            mask = band & ((kj >= blk) | (n > 0))
        else:
            prev_rows = slice((j - 1) * blk, j * blk)
            k_prev, v_prev = k_ref[prev_rows, :], v_ref[prev_rows, :]
            mask = band
        k_bd = block_diag(jnp.concatenate([k_prev, k_ref[rows, :]], axis=0))
        v_bd = block_diag(jnp.concatenate([v_prev, v_ref[rows, :]], axis=0))
        outs = []
        for p in range(n_pairs):
            g = p // pairs_per_group
            qp = q_ref[rows, p * LANES:(p + 1) * LANES]
            s = lax.dot_general(qp, k_bd[g], (((1,), (1,)), ((), ())), preferred_element_type=F32)
            es, ms = [], []
            for a in range(2):
                sa = jnp.where(mask, s[:, a * 2 * blk:(a + 1) * 2 * blk], NEG_INF)
                m = jnp.maximum(jnp.max(sa, axis=-1, keepdims=True), sinks_ref[2 * p + a])
                es.append(jnp.exp(sa - m))
                ms.append(m)
            e = jnp.concatenate(es, axis=1).astype(BF16)
            od = jnp.dot(e, jnp.concatenate([v_bd[g], ones_bd], axis=1), preferred_element_type=F32)
            sink_term = jnp.where(first, jnp.exp(sinks_ref[2 * p] - ms[0]),
                                  jnp.exp(sinks_ref[2 * p + 1] - ms[1]))
            outs.append(od[:, :LANES] / (od[:, LANES:] + sink_term))
        attn = jnp.concatenate(outs, axis=1)
        o_ref[rows, :] = _rms(attn, g_ref[...]).astype(BF16)


def _attention(q, k, v, sinks, g_attn, B, S):
    T, attn_w = q.shape
    kv_w = k.shape[1]
    blk = WINDOW
    tq = TQ_ATTN
    nt = S // tq
    cur = lambda b, n: (b * nt + n, 0)
    prev = lambda b, n: (b * (S // blk) + jnp.maximum(n * (tq // blk) - 1, 0), 0)
    return pl.pallas_call(
        _attn_kernel,
        grid=(B, nt),
        in_specs=[pl.BlockSpec(memory_space=pltpu.SMEM),
                  pl.BlockSpec((tq, attn_w), cur),
                  pl.BlockSpec((tq, kv_w), cur), pl.BlockSpec((tq, kv_w), cur),
                  pl.BlockSpec((blk, kv_w), prev), pl.BlockSpec((blk, kv_w), prev),
                  pl.BlockSpec((1, attn_w), lambda b, n: (0, 0))],
        out_specs=pl.BlockSpec((tq, attn_w), cur),
        out_shape=jax.ShapeDtypeStruct((T, attn_w), BF16),
        compiler_params=_cparams(("arbitrary", "arbitrary")),
        name="attn",
    )(sinks, q, k, v, k, v, g_attn)


def _neg_expm1(x, exp_x):
    series = -x * (1.0 + x * (0.5 + x * (1.0 / 6.0 + x * (1.0 / 24.0))))
    return jnp.where(x > -0.02, series, 1.0 - exp_x)


def _gelu_tanh(x):
    return 0.5 * x * (1.0 + jnp.tanh(math.sqrt(2.0 / math.pi) * (x + 0.044715 * (x * x * x))))


def _lru_kernel(lx_ref, lg_ref, cw_ref, cb_ref, wa_ref, ba_ref, wx_ref, bx_ref, lam_ref, g_ref,
                o_ref, xpad, a_scr, b_scr, h_scr, h_carry):
    i = pl.program_id(0)
    B, tl, C = lx_ref.shape
    pad = SUBLANES

    @pl.when(i == 0)
    def _():
        xpad[:, 0:pad, :] = jnp.zeros((B, pad, C), F32)
        h_carry[...] = jnp.zeros_like(h_carry)

    @pl.when(i > 0)
    def _():
        xpad[:, 0:pad, :] = xpad[:, tl:tl + pad, :]

    xpad[:, pad:, :] = lx_ref[...]
    y = jnp.zeros((B, tl, C), F32) + cb_ref[...]
    for j in range(CONV_WIDTH):
        y = y + xpad[:, pl.ds(pad - (CONV_WIDTH - 1) + j, tl), :] * cw_ref[j:j + 1, :]
    y2 = y.reshape(B * tl, C)
    yb = y2.astype(BF16)
    gate_a = jnp.dot(yb, wa_ref[...], preferred_element_type=F32) + ba_ref[...]
    gate_x = jnp.dot(yb, wx_ref[...], preferred_element_type=F32) + bx_ref[...]
    r = jax.nn.sigmoid(gate_a)
    ig = jax.nn.sigmoid(gate_x)
    nl = -lam_ref[...]
    softplus = jnp.maximum(nl, 0.0) + jnp.log1p(jnp.exp(-jnp.abs(nl)))
    log_a = (-LRU_C) * r * softplus
    a_all = jnp.exp(log_a)
    one_minus_a2 = _neg_expm1(2.0 * log_a, a_all * a_all)
    b_all = one_minus_a2 * lax.rsqrt(jnp.maximum(one_minus_a2, 1e-30)) * ig * y2
    groups = (B * tl // SUBLANES, SUBLANES, C)
    sub = lax.broadcasted_iota(I32, groups, 1)
    a_cum, b_loc = a_all.reshape(groups), b_all.reshape(groups)
    d = 1
    while d < SUBLANES:
        keep = sub >= d
        a_prev = pltpu.roll(a_cum, d, 1)
        b_prev = pltpu.roll(b_loc, d, 1)
        b_loc = b_loc + jnp.where(keep, a_cum * b_prev, 0.0)
        a_cum = jnp.where(keep, a_cum * a_prev, a_cum)
        d *= 2
    a_scr[...] = a_cum.reshape(B, tl, C)
    b_scr[...] = b_loc.reshape(B, tl, C)

    def carry(g, h_prev):
        rows = pl.ds(pl.multiple_of(g * SUBLANES, SUBLANES), SUBLANES)
        h = b_scr[:, rows, :] + a_scr[:, rows, :] * h_prev
        h_scr[:, rows, :] = h
        return h[:, SUBLANES - 1:SUBLANES, :]

    h_carry[...] = lax.fori_loop(0, tl // SUBLANES, carry, h_carry[...], unroll=4)
    lru = h_scr[...].reshape(B * tl, C) * _gelu_tanh(lg_ref[...].reshape(B * tl, C))
    o_ref[...] = _rms(lru, g_ref[...]).astype(BF16).reshape(B, tl, C)


def _lru(lx3, lg3, conv_w, conv_b, wa_bd, ba, wx_bd, bx, lam, g_lru):
    B, S, C = lx3.shape
    tl = TL_LRU
    blk = lambda i: (0, i, 0)
    const = lambda i: (0, 0)
    vec = pl.BlockSpec((1, C), const)
    return pl.pallas_call(
        _lru_kernel,
        grid=(S // tl,),
        in_specs=[pl.BlockSpec((B, tl, C), blk), pl.BlockSpec((B, tl, C), blk),
                  pl.BlockSpec((CONV_WIDTH, C), const), vec,
                  pl.BlockSpec((C, C), const), vec, pl.BlockSpec((C, C), const), vec, vec, vec],
        out_specs=pl.BlockSpec((B, tl, C), blk),
        out_shape=jax.ShapeDtypeStruct((B, S, C), BF16),
        scratch_shapes=[pltpu.VMEM((B, tl + SUBLANES, C), F32),
                        pltpu.VMEM((B, tl, C), F32), pltpu.VMEM((B, tl, C), F32),
                        pltpu.VMEM((B, tl, C), F32), pltpu.VMEM((B, 1, C), F32)],
        compiler_params=_cparams(("arbitrary",)),
        name="lru",
    )(lx3, lg3, conv_w, conv_b, wa_bd, ba, wx_bd, bx, lam, g_lru)


def _outproj_kernel(x_ref, ma_ref, mb_ref, wo_ref, g_ref, wr_ref, br_ref,
                    x1_ref, h2_ref, eid_ref, wt_ref, wo_b):
    @pl.when(pl.program_id(0) == 0)
    def _():
        wo_b[...] = wo_ref[...].astype(BF16)

    half = ma_ref.shape[1]
    x1 = (x_ref[...]
          + jnp.dot(ma_ref[...], wo_b[0:half, :], preferred_element_type=F32)
          + jnp.dot(mb_ref[...], wo_b[half:, :], preferred_element_type=F32))
    x1_ref[...] = x1
    h2 = _rms(x1, g_ref[...])
    h2_ref[...] = h2.astype(BF16).reshape(h2_ref.shape)
    h_hi = h2.astype(BF16)
    h_lo = (h2 - h_hi.astype(F32)).astype(BF16)
    hw = jnp.dot(h_hi, wr_ref[...], preferred_element_type=F32)
    logits = (hw[:, :LANES] + hw[:, LANES:]
              + jnp.dot(h_lo, wr_ref[:, :LANES], preferred_element_type=F32) + br_ref[...])
    tm = logits.shape[0]
    lane = lax.broadcasted_iota(I32, (tm, LANES), 1)
    lane_f = lane.astype(F32)
    big = float(LANES)

    def first_argmax(vals):
        m = jnp.max(vals, axis=-1, keepdims=True)
        idx = jnp.min(jnp.where(vals == m, lane_f, big), axis=-1, keepdims=True)
        return m, idx.astype(I32)

    gl = jnp.where(lane < N_GROUPS, logits, NEG_INF)
    gmax, gidx = first_argmax(gl)
    gsum = jnp.sum(jnp.where(lane < N_GROUPS, jnp.exp(gl - gmax), 0.0), axis=-1, keepdims=True)
    g_top_p = 1.0 / gsum
    lo = N_GROUPS + EXPERTS_PER_GROUP * gidx
    el = jnp.where((lane >= lo) & (lane < lo + EXPERTS_PER_GROUP), logits, NEG_INF)
    m1, i1 = first_argmax(el)
    el2 = jnp.where(lane == i1, NEG_INF, el)
    m2, i2 = first_argmax(el2)
    ratio = jnp.exp(m2 - m1)
    w1 = g_top_p / (1.0 + ratio)
    w2 = g_top_p * ratio / (1.0 + ratio)
    eid_ref[...] = jnp.where(lane == 0, i1 - N_GROUPS, jnp.where(lane == 1, i2 - N_GROUPS, 0))
    wt_ref[...] = jnp.where(lane == 0, w1, jnp.where(lane == 1, w2, 0.0))


def _outproj(x2, mix_a, mix_b, w_out, g_ffn, w_router, b_router):
    T, D = x2.shape
    tm = TM_PROJ
    half = mix_a.shape[1]
    row = lambda i: (i, 0)
    const = lambda i: (0, 0)
    return pl.pallas_call(
        _outproj_kernel,
        grid=(T // tm,),
        in_specs=[pl.BlockSpec((tm, D), row), pl.BlockSpec((tm, half), row),
                  pl.BlockSpec((tm, half), row), pl.BlockSpec(w_out.shape, const, **_FETCH_ONCE),
                  pl.BlockSpec((1, D), const), pl.BlockSpec((D, 2 * LANES), const),
                  pl.BlockSpec((1, LANES), const)],
        out_specs=[pl.BlockSpec((tm, D), row),
                   pl.BlockSpec((tm, D // LANES, LANES), lambda i: (i, 0, 0)),
                   pl.BlockSpec((tm, LANES), row), pl.BlockSpec((tm, LANES), row)],
        out_shape=[jax.ShapeDtypeStruct((T, D), F32),
                   jax.ShapeDtypeStruct((T, D // LANES, LANES), BF16),
                   jax.ShapeDtypeStruct((T, LANES), I32), jax.ShapeDtypeStruct((T, LANES), F32)],
        scratch_shapes=[pltpu.VMEM(w_out.shape, BF16)],
        compiler_params=_cparams(("arbitrary",)),
        name="outproj",
    )(x2, mix_a, mix_b, w_out, g_ffn, w_router, b_router)


def _lane_cumsum(x):
    lane = lax.broadcasted_iota(I32, x.shape, 1)
    shift = 1
    while shift < LANES:
        x = x + jnp.where(lane >= shift, pltpu.roll(x, shift, 1), 0)
        shift *= 2
    return x


def _plan_kernel(eid_ref, pos_ref, lrow_ref, te_ref, nv_ref, meta_ref, tcnt_ref, tstart_ref,
                 rank_scr, cnt_scr, start_scr, tcnt_scr, tpre_scr):
    phase = pl.program_id(0)
    i = pl.program_id(1)
    tm = eid_ref.shape[0]
    td = TM_DISPATCH
    nsub = tm // td
    lane = lax.broadcasted_iota(I32, (tm, LANES), 1)
    eid = eid_ref[...]
    f1 = eid[:, 0:1]
    f2 = eid[:, 1:2]
    sel1 = lane == f1
    sel2 = lane == f2

    @pl.when((phase == 0) & (i == 0))
    def _():
        cnt_scr[...] = jnp.zeros_like(cnt_scr)

    @pl.when(phase == 0)
    def _():
        onehot = (sel1 | sel2).astype(BF16)
        r = lax.broadcasted_iota(I32, (tm, tm), 0)
        c = lax.broadcasted_iota(I32, (tm, tm), 1)
        strict_lower = (r > c).astype(BF16)
        within = jnp.dot(strict_lower, onehot, preferred_element_type=F32)
        base = cnt_scr[0:1, :]
        before = within + base
        r1 = jnp.sum(jnp.where(sel1, before, 0.0), axis=-1, keepdims=True)
        r2 = jnp.sum(jnp.where(sel2, before, 0.0), axis=-1, keepdims=True)
        rank = jnp.where(lane == 0, r1, jnp.where(lane == 1, r2, 0.0)).astype(I32)
        rank_scr[pl.ds(pl.multiple_of(i * tm, tm), tm), :] = rank
        onehot_f = onehot.astype(F32)
        row = lax.broadcasted_iota(I32, (tm, LANES), 0)
        srow = lax.broadcasted_iota(I32, (SUBLANES, LANES), 0)
        counts = [jnp.sum(onehot_f[s * td:(s + 1) * td], axis=0, keepdims=True) for s in range(nsub)]
        cnt_mat = jnp.zeros((SUBLANES, LANES), F32)
        seen = jnp.zeros((1, LANES), F32)
        local_before = within
        for s in range(nsub):
            tile = i * nsub + s
            tcnt_scr[pl.ds(tile, 1), :] = counts[s]
            tpre_scr[pl.ds(tile, 1), :] = base + seen
            cnt_mat = jnp.where(srow == s, counts[s], cnt_mat)
            if s > 0:
                local_before = jnp.where(row >= s * td, within - seen, local_before)
            seen = seen + counts[s]
        cnt_i = cnt_mat.astype(I32)
        run_start = (_lane_cumsum(cnt_i) - cnt_i).astype(F32)
        local = local_before + run_start[0:1, :]
        for s in range(1, nsub):
            local = jnp.where(row >= s * td, local_before + run_start[s:s + 1, :], local)
        l1 = jnp.sum(jnp.where(sel1, local, 0.0), axis=-1, keepdims=True)
        l2 = jnp.sum(jnp.where(sel2, local, 0.0), axis=-1, keepdims=True)
        lrow_ref[...] = jnp.where(lane == 0, l1, jnp.where(lane == 1, l2, 0.0)).astype(I32)
        cnt_scr[...] = cnt_scr[...] + seen

    @pl.when(phase == 1)
    def _():
        starts = start_scr[0:1, :]
        o1 = jnp.sum(jnp.where(sel1, starts, 0.0), axis=-1, keepdims=True).astype(I32)
        o2 = jnp.sum(jnp.where(sel2, starts, 0.0), axis=-1, keepdims=True).astype(I32)
        rank = rank_scr[pl.ds(pl.multiple_of(i * tm, tm), tm), :]
        pos = rank + jnp.where(lane == 0, o1, jnp.where(lane == 1, o2, 0))
        pos_ref[...] = pos.astype(F32).T[0:SUBLANES, :].astype(I32)

    @pl.when((phase == 0) & (i == pl.num_programs(1) - 1))
    def _():
        shift = int(math.log2(TM_EXPERT))
        cnt = cnt_scr[...].astype(I32)
        padded = ((cnt + (TM_EXPERT - 1)) >> shift) << shift
        ends = _lane_cumsum(padded)
        start_scr[...] = (ends - padded).astype(F32)
        nt = te_ref.shape[0]
        end_tile = (ends >> shift)[0:1, :]
        tl_lane = lax.broadcasted_iota(I32, (nt, LANES), 1)
        tile = lax.broadcasted_iota(I32, (nt, LANES), 0)
        done = jnp.where((tl_lane < N_EXPERTS) & (end_tile <= tile), 1.0, 0.0)
        te = jnp.minimum(jnp.sum(done, axis=-1, keepdims=True).astype(I32), N_EXPERTS - 1)
        n_valid = jnp.sum(jnp.where(tl_lane[0:1, :] == N_EXPERTS - 1, end_tile, 0).astype(F32),
                          axis=-1, keepdims=True).astype(I32)
        nv_ref[...] = jnp.broadcast_to(n_valid, nv_ref.shape)
        seg_end = jnp.sum(jnp.where(tl_lane == te, end_tile, 0).astype(F32), axis=-1,
                          keepdims=True).astype(I32)
        nxt = jnp.sum(jnp.where((tl_lane < N_EXPERTS) & (end_tile <= seg_end), 1.0, 0.0), axis=-1,
                      keepdims=True).astype(I32)
        nxt = jnp.where(seg_end < n_valid, jnp.minimum(nxt, N_EXPERTS - 1), -1)
        te_ref[...] = jnp.where(tl_lane == 1, nxt, te)
        row = lax.broadcasted_iota(I32, cnt.shape, 0)
        meta_ref[...] = jnp.where(row == 0, ends - padded + cnt, jnp.where(row == 1, padded - cnt, 0))
        tcnt_ref[...] = tcnt_scr[...].astype(I32)
        tstart_ref[...] = (tpre_scr[...] + (ends - padded).astype(F32)[0:1, :]).astype(I32)


def _plan(eid, n_tiles):
    T = eid.shape[0]
    tm = TM_PLAN
    steps = T // tm
    n_dtiles = T // TM_DISPATCH
    const = lambda p, i: (0, 0)
    return pl.pallas_call(
        _plan_kernel,
        grid=(2, steps),
        in_specs=[pl.BlockSpec((tm, LANES), lambda p, i: (i, 0))],
        out_specs=[pl.BlockSpec((SUBLANES, tm), lambda p, i: (0, i * p)),
                   pl.BlockSpec((tm, LANES), lambda p, i: (i * (1 - p) + (steps - 1) * p, 0)),
                   pl.BlockSpec((n_tiles, LANES), const), pl.BlockSpec((SUBLANES, LANES), const),
                   pl.BlockSpec((SUBLANES, LANES), const),
                   pl.BlockSpec((n_dtiles, LANES), const), pl.BlockSpec((n_dtiles, LANES), const)],
        out_shape=[jax.ShapeDtypeStruct((SUBLANES, T), I32), jax.ShapeDtypeStruct((T, LANES), I32),
                   jax.ShapeDtypeStruct((n_tiles, LANES), I32),
                   jax.ShapeDtypeStruct((SUBLANES, LANES), I32),
                   jax.ShapeDtypeStruct((SUBLANES, LANES), I32),
                   jax.ShapeDtypeStruct((n_dtiles, LANES), I32),
                   jax.ShapeDtypeStruct((n_dtiles, LANES), I32)],
        scratch_shapes=[pltpu.VMEM((T, LANES), I32), pltpu.VMEM((SUBLANES, LANES), F32),
                        pltpu.VMEM((SUBLANES, LANES), F32),
                        pltpu.VMEM((n_dtiles, LANES), F32), pltpu.VMEM((n_dtiles, LANES), F32)],
        compiler_params=_cparams(("arbitrary", "arbitrary")),
        name="plan",
    )(eid)


_PAD_BITS = tuple(1 << b for b in reversed(range(int(math.log2(TM_EXPERT)))))


_RUN_BITS = tuple(1 << b for b in reversed(range(int(math.log2(TM_DISPATCH)) + 1)))


def _dispatch_kernel(tcnt_ref, tstart_ref, seg_end_ref, seg_pad_ref, nv_ref, lrow_ref, h_ref,
                     xs_ref, lbuf_a, lbuf_b, zeros, sem, zsem):
    i = pl.program_id(0)
    n = pl.num_programs(0)
    td = h_ref.shape[0]
    nr = 2 * td

    def wait_buf(buf, s):
        pltpu.make_async_copy(buf, xs_ref.at[pl.ds(0, nr)], sem.at[s]).wait()

    def step(cur, cur_s, prev, prev_s):
        @pl.when(i >= 2)
        def _():
            wait_buf(cur, cur_s)

        local_row = lrow_ref[...].astype(F32).T
        r = lax.broadcasted_iota(I32, (nr, td), 0).astype(F32)
        onehot = ((r == local_row[0:1, :]) | (r == local_row[1:2, :])).astype(BF16)
        sorted_rows = jnp.dot(onehot, h_ref[...].reshape(td, -1), preferred_element_type=F32)
        cur[...] = sorted_rows.astype(BF16).reshape(cur.shape)

        table_row = jnp.maximum(i - 1, 0) * N_EXPERTS
        lo = 0
        for e in range(N_EXPERTS):
            cnt = jnp.where(i >= 1, tcnt_ref[table_row + e], 0)
            dst = tstart_ref[table_row + e]
            for bit in _RUN_BITS:
                off = cnt & ~(2 * bit - 1)

                @pl.when((cnt & bit) != 0)
                def _():
                    pltpu.make_async_copy(prev.at[pl.ds(lo + off, bit)],
                                          xs_ref.at[pl.ds(dst + off, bit)], sem.at[prev_s]).start()
            lo = lo + cnt

        @pl.when(i == n - 1)
        def _():
            wait_buf(prev, prev_s)

    pl.when(i % 2 == 0)(functools.partial(step, lbuf_a, 0, lbuf_b, 1))
    pl.when(i % 2 == 1)(functools.partial(step, lbuf_b, 1, lbuf_a, 0))

    @pl.when(i == n - 1)
    def _():
        zeros[...] = jnp.zeros_like(zeros)
        n_tiles = xs_ref.shape[0] // TM_EXPERT
        first_tail = n_tiles - N_EXPERTS

        def pad_copy(e, bit):
            n = seg_pad_ref[e]
            dst = seg_end_ref[e] + (n & ~(2 * bit - 1))
            return (n & bit) != 0, pltpu.make_async_copy(zeros.at[pl.ds(0, bit)],
                                                         xs_ref.at[pl.ds(dst, bit)], zsem)

        def tail_copy(j):
            return j >= nv_ref[0], pltpu.make_async_copy(
                zeros, xs_ref.at[pl.ds(j * TM_EXPERT, TM_EXPERT)], zsem)

        def for_each_fill(act):
            def per_expert(e, c):
                for bit in _PAD_BITS:
                    pred, cp = pad_copy(e, bit)
                    pl.when(pred)(functools.partial(act, cp))
                return c

            lax.fori_loop(0, N_EXPERTS, per_expert, 0)

            def per_tile(j, c):
                pred, cp = tail_copy(j)
                pl.when(pred)(functools.partial(act, cp))
                return c

            lax.fori_loop(first_tail, n_tiles, per_tile, 0)

        for_each_fill(lambda cp: cp.start())
        for_each_fill(lambda cp: cp.wait())


def _dispatch(tile_cnt, tile_start, seg_end, seg_pad, n_valid, lrow, h2, n_rows):
    T = h2.shape[0]
    slab = h2.shape[1:]
    td = TM_DISPATCH
    nt = T // td
    smem = pl.BlockSpec(memory_space=pltpu.SMEM)
    return pl.pallas_call(
        _dispatch_kernel,
        grid=(nt + 1,),
        in_specs=[smem, smem, smem, smem, smem,
                  pl.BlockSpec((td, LANES), lambda i: (jnp.minimum(i, nt - 1), 0)),
                  pl.BlockSpec((td,) + slab, lambda i: (jnp.minimum(i, nt - 1), 0, 0))],
        out_specs=pl.BlockSpec(memory_space=pl.ANY),
        out_shape=jax.ShapeDtypeStruct((n_rows,) + slab, h2.dtype),
        scratch_shapes=[pltpu.VMEM((2 * td,) + slab, h2.dtype),
                        pltpu.VMEM((2 * td,) + slab, h2.dtype),
                        pltpu.VMEM((TM_EXPERT,) + slab, h2.dtype),
                        pltpu.SemaphoreType.DMA((2,)), pltpu.SemaphoreType.DMA(())],
        compiler_params=_cparams(("arbitrary",)),
        name="dispatch",
    )(tile_cnt, tile_start, seg_end, seg_pad, n_valid, lrow, h2)


def _experts_kernel(te_ref, tn_ref, nv_ref, xs_ref, wg_ref, wu_ref, wd_ref, ys_ref,
                    wg_f, wu_f, wd_f, wg_b, wu_b, wd_b, slot_ref, sem):
    j = pl.program_id(0)
    valid = j < nv_ref[0]
    new_expert = valid & ((j == 0) | (te_ref[j] != te_ref[jnp.maximum(j - 1, 0)]))

    def fetch(e, s):
        return (pltpu.make_async_copy(wg_ref.at[e], wg_f.at[s], sem.at[s]),
                pltpu.make_async_copy(wu_ref.at[e], wu_f.at[s], sem.at[s]),
                pltpu.make_async_copy(wd_ref.at[e], wd_f.at[s], sem.at[s]))

    @pl.when(j == 0)
    def _():
        slot_ref[0] = 0
        for cp in fetch(te_ref[0], 0):
            cp.start()

    @pl.when(new_expert)
    def _():
        s = slot_ref[0]
        for cp in fetch(te_ref[j], s):
            cp.wait()
        nxt = tn_ref[j]

        @pl.when(nxt >= 0)
        def _():
            for cp in fetch(nxt, 1 - s):
                cp.start()

        wg_b[...] = wg_f[s].astype(BF16)
        wu_b[...] = wu_f[s].astype(BF16)
        wd_b[...] = wd_f[s].astype(BF16)
        slot_ref[0] = 1 - s

    @pl.when(valid)
    def _():
        x = xs_ref[...].reshape(xs_ref.shape[0], -1)
        y = None
        for lo in range(0, wg_b.shape[1], EXPERT_FF_BLOCK):
            cols = slice(lo, lo + EXPERT_FF_BLOCK)
            a = jnp.dot(x, wg_b[:, cols], preferred_element_type=F32)
            u = jnp.dot(x, wu_b[:, cols], preferred_element_type=F32)
            hid = (a * jax.nn.sigmoid(a) * u).astype(BF16)
            part = jnp.dot(hid, wd_b[cols, :], preferred_element_type=F32)
            y = part if y is None else y + part
        ys_ref[...] = y.astype(BF16).reshape(ys_ref.shape)

    @pl.when(j >= nv_ref[0])
    def _():
        ys_ref[...] = jnp.zeros_like(ys_ref)


def _experts(tile_expert, tile_next, n_valid, xs, wg, wu, wd):
    R = xs.shape[0]
    slab = xs.shape[1:]
    E, D, Fh = wg.shape
    tm = TM_EXPERT
    n_tiles = R // tm
    rows = lambda j, te, tn, nv: (j, 0, 0)
    rows_in = lambda j, te, tn, nv: (jnp.minimum(j, nv[0] - 1), 0, 0)
    hbm = pl.BlockSpec(memory_space=pl.ANY)
    return pl.pallas_call(
        _experts_kernel,
        grid_spec=pltpu.PrefetchScalarGridSpec(
            num_scalar_prefetch=3,
            grid=(n_tiles,),
            in_specs=[pl.BlockSpec((tm,) + slab, rows_in), hbm, hbm, hbm],
            out_specs=pl.BlockSpec((tm,) + slab, rows),
            scratch_shapes=[pltpu.VMEM((2, D, Fh), F32), pltpu.VMEM((2, D, Fh), F32),
                            pltpu.VMEM((2, Fh, D), F32),
                            pltpu.VMEM((D, Fh), BF16), pltpu.VMEM((D, Fh), BF16),
                            pltpu.VMEM((Fh, D), BF16),
                            pltpu.SMEM((1,), I32), pltpu.SemaphoreType.DMA((2,))]),
        out_shape=jax.ShapeDtypeStruct(xs.shape, xs.dtype),
        compiler_params=_cparams(("arbitrary",)),
        name="experts",
    )(tile_expert, tile_next, n_valid, xs, wg, wu, wd)


def _combine_kernel(tcnt_ref, tstart_ref, lrow_ref, x1_ref, wt_ref, p_ref, gp_ref, wpg_ref, wpp_ref,
                    gf_ref, ys_ref, o_ref, lbuf_a, lbuf_b, wpg_b, wpp_b, sem):
    i = pl.program_id(0)
    n = pl.num_programs(0)
    tm = x1_ref.shape[0]
    nr = 2 * tm

    def request(tile, live, buf, s):
        lo = 0
        for e in range(N_EXPERTS):
            cnt = jnp.where(live, tcnt_ref[tile * N_EXPERTS + e], 0)
            src = tstart_ref[tile * N_EXPERTS + e]
            for bit in _RUN_BITS:
                off = cnt & ~(2 * bit - 1)

                @pl.when((cnt & bit) != 0)
                def _():
                    pltpu.make_async_copy(ys_ref.at[pl.ds(src + off, bit)],
                                          buf.at[pl.ds(lo + off, bit)], sem.at[s]).start()
            lo = lo + cnt

    @pl.when(i == 0)
    def _():
        request(0, True, lbuf_a, 0)
        wpg_b[...] = wpg_ref[...].astype(BF16)
        wpp_b[...] = wpp_ref[...].astype(BF16)

    def step(cur, cur_s, nxt, nxt_s):
        pltpu.make_async_copy(ys_ref.at[pl.ds(0, nr)], cur, sem.at[cur_s]).wait()
        request(jnp.minimum(i + 1, n - 1), i + 1 < n, nxt, nxt_s)
        rows = cur[...].reshape(nr, -1)
        local_row = lrow_ref[...]
        r = lax.broadcasted_iota(I32, (tm, nr), 1)
        y0 = jnp.dot((r == local_row[:, 0:1]).astype(BF16), rows, preferred_element_type=F32)
        y1 = jnp.dot((r == local_row[:, 1:2]).astype(BF16), rows, preferred_element_type=F32)
        wt = wt_ref[...]
        x2 = x1_ref[...] + wt[:, 0:1] * y0 + wt[:, 1:2] * y1
        hp = _rms(x2, gp_ref[...]).astype(BF16)
        gate = jax.nn.sigmoid(jnp.dot(hp, wpg_b[...], preferred_element_type=F32))
        proj = jnp.dot(p_ref[...].astype(BF16), wpp_b[...], preferred_element_type=F32)
        x3 = x2 + gate * proj
        o_ref[...] = _rms(x3, gf_ref[...])

    pl.when(i % 2 == 0)(functools.partial(step, lbuf_a, 0, lbuf_b, 1))
    pl.when(i % 2 == 1)(functools.partial(step, lbuf_b, 1, lbuf_a, 0))


def _combine(tile_cnt, tile_start, lrow, x1, wts, p2, g_ple, w_ple_gate, w_ple_proj, g_final, ys):
    T, D = x1.shape
    P = p2.shape[1]
    tm = TM_DISPATCH
    n = T // tm
    row = lambda i: (i, 0)
    const = lambda i: (0, 0)
    smem = pl.BlockSpec(memory_space=pltpu.SMEM)
    return pl.pallas_call(
        _combine_kernel,
        grid=(n,),
        in_specs=[smem, smem, pl.BlockSpec((tm, LANES), row),
                  pl.BlockSpec((tm, D), row), pl.BlockSpec((tm, LANES), row),
                  pl.BlockSpec((tm, P), row), pl.BlockSpec((1, D), const),
                  pl.BlockSpec((D, D), const, **_FETCH_ONCE), pl.BlockSpec((P, D), const, **_FETCH_ONCE),
                  pl.BlockSpec((1, D), const), pl.BlockSpec(memory_space=pl.ANY)],
        out_specs=pl.BlockSpec((tm, D), row),
        out_shape=jax.ShapeDtypeStruct((T, D), F32),
        scratch_shapes=[pltpu.VMEM((2 * tm,) + ys.shape[1:], ys.dtype),
                        pltpu.VMEM((2 * tm,) + ys.shape[1:], ys.dtype),
                        pltpu.VMEM((D, D), BF16), pltpu.VMEM((P, D), BF16),
                        pltpu.SemaphoreType.DMA((2,))],
        compiler_params=_cparams(("arbitrary",)),
        name="combine",
    )(tile_cnt, tile_start, lrow, x1, wts, p2, g_ple, w_ple_gate, w_ple_proj, g_final, ys)


def _block_diag(w):
    nb, d, _ = w.shape
    eye = jnp.eye(nb, dtype=w.dtype)
    return (eye[:, None, :, None] * w[:, :, None, :]).reshape(nb * d, nb * d)


def kernel(x, p, positions, g_mix, w_in, sinks, conv_w, conv_b, lru_wa, lru_ba, lru_wx, lru_bx,
           lru_lambda, g_attn_out, g_lru_out, w_out, g_ffn, w_router_group, b_router_group,
           w_router_expert, b_router_expert, w_expert_gate, w_expert_up, w_expert_down, g_ple,
           w_ple_gate, w_ple_proj, g_final):
    B, S, D = x.shape
    assert w_in.shape[0] == 1, "single-layer block only"
    T = B * S
    lru_w = conv_w.shape[-1]
    n_rows = 2 * T + N_EXPERTS * TM_EXPERT
    n_tiles = n_rows // TM_EXPERT
    pos3 = positions.reshape(T // TM_INPROJ, TM_INPROJ // LANES, LANES).astype(I32)
    x2 = x.reshape(T, D)
    q, k, v, lx, lg = _inproj(x2, pos3, g_mix[0][None], w_in[0], lru_w)
    mix_a = _attention(q, k, v, sinks[0], g_attn_out[0][None], B, S)
    mix_b = _lru(lx.reshape(B, S, lru_w), lg.reshape(B, S, lru_w), conv_w[0], conv_b[0][None],
                 _block_diag(lru_wa[0]).astype(BF16), lru_ba[0][None],
                 _block_diag(lru_wx[0]).astype(BF16), lru_bx[0][None],
                 lru_lambda[0][None], g_lru_out[0][None]).reshape(T, lru_w)
    n_router = N_GROUPS + N_EXPERTS
    w_router = jnp.pad(jnp.concatenate([w_router_group[0], w_router_expert[0]], axis=1),
                       ((0, 0), (0, LANES - n_router)))
    b_router = jnp.pad(jnp.concatenate([b_router_group[0], b_router_expert[0]]),
                       (0, LANES - n_router))[None]
    w_router_hi = w_router.astype(BF16)
    w_router_lo = (w_router - w_router_hi.astype(F32)).astype(BF16)
    x1, h2, eid, wts = _outproj(x2, mix_a, mix_b, w_out[0], g_ffn[0][None],
                                jnp.concatenate([w_router_hi, w_router_lo], axis=1), b_router)
    pos_t, lrow, tile_expert, n_valid, meta, tile_cnt, tile_start = _plan(eid, n_tiles)
    n_valid = n_valid[0, :1]
    tile_cnt = tile_cnt[:, :N_EXPERTS].reshape(-1)
    tile_start = tile_start[:, :N_EXPERTS].reshape(-1)
    xs = _dispatch(tile_cnt, tile_start, meta[0], meta[1], n_valid, lrow, h2, n_rows)
    Fh = w_expert_gate.shape[-1]
    ys = _experts(tile_expert[:, 0], tile_expert[:, 1], n_valid, xs,
                  w_expert_gate[0].reshape(N_EXPERTS, D, Fh),
                  w_expert_up[0].reshape(N_EXPERTS, D, Fh),
                  w_expert_down[0].reshape(N_EXPERTS, Fh, D))
    out = _combine(tile_cnt, tile_start, lrow, x1, wts, p[0].reshape(T, -1), g_ple[0][None],
                   w_ple_gate[0], w_ple_proj[0], g_final[None], ys)
    return out.reshape(B, S, D)
```

```python
import functools
import math

import jax
import jax.numpy as jnp
from jax import lax
from jax.experimental import pallas as pl
from jax.experimental.pallas import tpu as pltpu

F32 = jnp.float32
BF16 = jnp.bfloat16
I32 = jnp.int32

EPS = 1e-6
N_HEADS = 8
N_KV_HEADS = 2
HEAD_DIM = 64
WINDOW = 128
ROPE_THETA = 10000.0
CONV_WIDTH = 4
LRU_C = 8.0
N_GROUPS = 4
EXPERTS_PER_GROUP = 8
N_EXPERTS = N_GROUPS * EXPERTS_PER_GROUP
NEG_INF = -1e30
LANES = 128
SUBLANES = 8

TM_INPROJ = 512
TM_PROJ = 512
TQ_ATTN = 512
TL_LRU = 256
TM_PLAN = 1024
TM_EXPERT = 512
EXPERT_FF_BLOCK = 256
TM_DISPATCH = 512
TM_COMBINE = 256
VMEM_LIMIT = 56 * 1024 * 1024


def _rms(x, g):
    ms = jnp.mean(x * x, axis=-1, keepdims=True)
    return x * lax.rsqrt(ms + EPS) * g


_FETCH_ONCE = dict(pipeline_mode=pl.Buffered(1))


def _cparams(sem):
    return pltpu.CompilerParams(dimension_semantics=sem, vmem_limit_bytes=VMEM_LIMIT)


def _inproj_kernel(x_ref, pos_ref, g_ref, w_ref, q_ref, k_ref, v_ref, lx_ref, lg_ref, w_b):
    @pl.when(pl.program_id(0) == 0)
    def _():
        w_b[...] = w_ref[...].astype(BF16)

    h = _rms(x_ref[...], g_ref[...]).astype(BF16)
    proj = jnp.dot(h, w_b[...], preferred_element_type=F32)
    lane = lax.broadcasted_iota(I32, (1, LANES), 1)
    fidx = (lax.broadcasted_iota(I32, (LANES, 1), 0) % (HEAD_DIM // 2)).astype(F32)
    inv_freq = jnp.exp(fidx * (-2.0 / HEAD_DIM * math.log(ROPE_THETA)))
    pos = pos_ref[0].astype(F32)
    ang = jnp.concatenate([(inv_freq * pos[c:c + 1, :]).T for c in range(pos.shape[0])], axis=0)
    cos = jnp.cos(ang)
    sin = jnp.sin(ang)
    first_half = (lane % HEAD_DIM) < (HEAD_DIM // 2)
    sin_signed = jnp.where(first_half, -sin, sin)

    def rope(t):
        partner = jnp.where(first_half, pltpu.roll(t, LANES - HEAD_DIM // 2, 1),
                            pltpu.roll(t, HEAD_DIM // 2, 1))
        return t * cos + partner * sin_signed

    attn_w = N_HEADS * HEAD_DIM
    scale = HEAD_DIM ** -0.5
    assert math.frexp(scale)[0] == 0.5, "score scale must be a power of two to fold into q exactly"
    for c in range(attn_w // LANES):
        q_ref[:, c * LANES:(c + 1) * LANES] = (
            rope(proj[:, c * LANES:(c + 1) * LANES]) * scale).astype(BF16)
    kv_w = N_KV_HEADS * HEAD_DIM
    assert kv_w == LANES
    k = rope(proj[:, attn_w:attn_w + kv_w])
    v = proj[:, attn_w + kv_w:attn_w + 2 * kv_w]
    k_ref[:, :kv_w] = k.astype(BF16)
    k_ref[:, kv_w:] = pltpu.roll(k, HEAD_DIM, 1).astype(BF16)
    v_ref[:, :kv_w] = v.astype(BF16)
    v_ref[:, kv_w:] = pltpu.roll(v, HEAD_DIM, 1).astype(BF16)
    off_lx = attn_w + 2 * kv_w
    lru_w = lx_ref.shape[1]
    lx_ref[...] = proj[:, off_lx:off_lx + lru_w]
    lg_ref[...] = proj[:, off_lx + lru_w:off_lx + 2 * lru_w]


def _inproj(x2, pos3, g_mix, w_in, lru_w):
    T, D = x2.shape
    tm = TM_INPROJ
    attn_w = N_HEADS * HEAD_DIM
    kv_w = N_KV_HEADS * HEAD_DIM
    row = lambda i: (i, 0)
    const = lambda i: (0, 0)
    return pl.pallas_call(
        _inproj_kernel,
        grid=(T // tm,),
        in_specs=[pl.BlockSpec((tm, D), row),
                  pl.BlockSpec((1, tm // LANES, LANES), lambda i: (i, 0, 0)),
                  pl.BlockSpec((1, D), const), pl.BlockSpec(w_in.shape, const, **_FETCH_ONCE)],
        out_specs=[pl.BlockSpec((tm, attn_w), row), pl.BlockSpec((tm, 2 * kv_w), row),
                   pl.BlockSpec((tm, 2 * kv_w), row), pl.BlockSpec((tm, lru_w), row),
                   pl.BlockSpec((tm, lru_w), row)],
        out_shape=[jax.ShapeDtypeStruct((T, attn_w), BF16),
                   jax.ShapeDtypeStruct((T, 2 * kv_w), BF16),
                   jax.ShapeDtypeStruct((T, 2 * kv_w), BF16), jax.ShapeDtypeStruct((T, lru_w), F32),
                   jax.ShapeDtypeStruct((T, lru_w), F32)],
        scratch_shapes=[pltpu.VMEM(w_in.shape, BF16)],
        compiler_params=_cparams(("arbitrary",)),
        name="inproj",
    )(x2, pos3, g_mix, w_in)


def _attn_kernel(sinks_ref, q_ref, k_ref, v_ref, kp_ref, vp_ref, g_ref, o_ref):
    n = pl.program_id(1)
    blk = WINDOW
    hd = HEAD_DIM
    nqb = q_ref.shape[0] // blk
    n_pairs = N_HEADS // 2
    pairs_per_group = n_pairs // N_KV_HEADS
    lane = lax.broadcasted_iota(I32, (1, LANES), 1)
    first = lane < hd
    qi = lax.broadcasted_iota(I32, (blk, 2 * blk), 0)
    kj = lax.broadcasted_iota(I32, (blk, 2 * blk), 1)
    rel = qi + blk - kj
    band = (rel >= 0) & (rel < WINDOW)
    bd_row = lax.broadcasted_iota(I32, (4 * blk, LANES), 0)
    bd_lane = lax.broadcasted_iota(I32, (4 * blk, LANES), 1)
    ones_bd = ((bd_row < 2 * blk) == (bd_lane < hd)).astype(BF16)
    zero = jnp.zeros((), BF16)

    def block_diag(t):
        same, swapped = t[:, :LANES], t[:, LANES:]
        out = []
        for g in range(N_KV_HEADS):
            top = jnp.where(first, same if g == 0 else swapped, zero)
            bot = jnp.where(first, zero, swapped if g == 0 else same)
            out.append(jnp.concatenate([top, bot], axis=0))
        return out

    for j in range(nqb):
        rows = slice(j * blk, (j + 1) * blk)
        if j == 0:
            k_prev, v_prev = kp_ref[...], vp_ref[...]
            mask = band & ((kj >= blk) | (n > 0))
        else:
            prev_rows = slice((j - 1) * blk, j * blk)
            k_prev, v_prev = k_ref[prev_rows, :], v_ref[prev_rows, :]
            mask = band
        k_bd = block_diag(jnp.concatenate([k_prev, k_ref[rows, :]], axis=0))
        v_bd = block_diag(jnp.concatenate([v_prev, v_ref[rows, :]], axis=0))
        outs = []
        for p in range(n_pairs):
            g = p // pairs_per_group
            qp = q_ref[rows, p * LANES:(p + 1) * LANES]
            s = lax.dot_general(qp, k_bd[g], (((1,), (1,)), ((), ())), preferred_element_type=F32)
            es, ms = [], []
            for a in range(2):
                sa = jnp.where(mask, s[:, a * 2 * blk:(a + 1) * 2 * blk], NEG_INF)
                m = jnp.maximum(jnp.max(sa, axis=-1, keepdims=True), sinks_ref[2 * p + a])
                es.append(jnp.exp(sa - m))
                ms.append(m)
            e = jnp.concatenate(es, axis=1).astype(BF16)
            od = jnp.dot(e, jnp.concatenate([v_bd[g], ones_bd], axis=1), preferred_element_type=F32)
            sink_term = jnp.where(first, jnp.exp(sinks_ref[2 * p] - ms[0]),
                                  jnp.exp(sinks_ref[2 * p + 1] - ms[1]))
            outs.append(od[:, :LANES] / (od[:, LANES:] + sink_term))
        attn = jnp.concatenate(outs, axis=1)
        o_ref[rows, :] = _rms(attn, g_ref[...]).astype(BF16)


def _attention(q, k, v, sinks, g_attn, B, S):
    T, attn_w = q.shape
    kv_w = k.shape[1]
    blk = WINDOW
    tq = TQ_ATTN
    nt = S // tq
    cur = lambda b, n: (b * nt + n, 0)
    prev = lambda b, n: (b * (S // blk) + jnp.maximum(n * (tq // blk) - 1, 0), 0)
    return pl.pallas_call(
        _attn_kernel,
        grid=(B, nt),
        in_specs=[pl.BlockSpec(memory_space=pltpu.SMEM),
                  pl.BlockSpec((tq, attn_w), cur),
                  pl.BlockSpec((tq, kv_w), cur), pl.BlockSpec((tq, kv_w), cur),
                  pl.BlockSpec((blk, kv_w), prev), pl.BlockSpec((blk, kv_w), prev),
                  pl.BlockSpec((1, attn_w), lambda b, n: (0, 0))],
        out_specs=pl.BlockSpec((tq, attn_w), cur),
        out_shape=jax.ShapeDtypeStruct((T, attn_w), BF16),
        compiler_params=_cparams(("arbitrary", "arbitrary")),
        name="attn",
    )(sinks, q, k, v, k, v, g_attn)


def _neg_expm1(x, exp_x):
    series = -x * (1.0 + x * (0.5 + x * (1.0 / 6.0 + x * (1.0 / 24.0))))
    return jnp.where(x > -0.02, series, 1.0 - exp_x)


def _gelu_tanh(x):
    return 0.5 * x * (1.0 + jnp.tanh(math.sqrt(2.0 / math.pi) * (x + 0.044715 * (x * x * x))))


def _lru_kernel(lx_ref, lg_ref, cw_ref, cb_ref, wa_ref, ba_ref, wx_ref, bx_ref, lam_ref, g_ref,
                o_ref, xpad, a_scr, b_scr, h_scr, h_carry):
    i = pl.program_id(0)
    B, tl, C = lx_ref.shape
    pad = SUBLANES

    @pl.when(i == 0)
    def _():
        xpad[:, 0:pad, :] = jnp.zeros((B, pad, C), F32)
        h_carry[...] = jnp.zeros_like(h_carry)

    @pl.when(i > 0)
    def _():
        xpad[:, 0:pad, :] = xpad[:, tl:tl + pad, :]

    xpad[:, pad:, :] = lx_ref[...]
    y = jnp.zeros((B, tl, C), F32) + cb_ref[...]
    for j in range(CONV_WIDTH):
        y = y + xpad[:, pl.ds(pad - (CONV_WIDTH - 1) + j, tl), :] * cw_ref[j:j + 1, :]
    y2 = y.reshape(B * tl, C)
    yb = y2.astype(BF16)
    gate_a = jnp.dot(yb, wa_ref[...], preferred_element_type=F32) + ba_ref[...]
    gate_x = jnp.dot(yb, wx_ref[...], preferred_element_type=F32) + bx_ref[...]
    r = jax.nn.sigmoid(gate_a)
    ig = jax.nn.sigmoid(gate_x)
    nl = -lam_ref[...]
    softplus = jnp.maximum(nl, 0.0) + jnp.log1p(jnp.exp(-jnp.abs(nl)))
    log_a = (-LRU_C) * r * softplus
    a_all = jnp.exp(log_a)
    one_minus_a2 = _neg_expm1(2.0 * log_a, a_all * a_all)
    b_all = one_minus_a2 * lax.rsqrt(jnp.maximum(one_minus_a2, 1e-30)) * ig * y2
    groups = (B * tl // SUBLANES, SUBLANES, C)
    sub = lax.broadcasted_iota(I32, groups, 1)
    a_cum, b_loc = a_all.reshape(groups), b_all.reshape(groups)
    d = 1
    while d < SUBLANES:
        keep = sub >= d
        a_prev = pltpu.roll(a_cum, d, 1)
        b_prev = pltpu.roll(b_loc, d, 1)
        b_loc = b_loc + jnp.where(keep, a_cum * b_prev, 0.0)
        a_cum = jnp.where(keep, a_cum * a_prev, a_cum)
        d *= 2
    a_scr[...] = a_cum.reshape(B, tl, C)
    b_scr[...] = b_loc.reshape(B, tl, C)

    def carry(g, h_prev):
        rows = pl.ds(pl.multiple_of(g * SUBLANES, SUBLANES), SUBLANES)
        h = b_scr[:, rows, :] + a_scr[:, rows, :] * h_prev
        h_scr[:, rows, :] = h
        return h[:, SUBLANES - 1:SUBLANES, :]

    h_carry[...] = lax.fori_loop(0, tl // SUBLANES, carry, h_carry[...], unroll=4)
    lru = h_scr[...].reshape(B * tl, C) * _gelu_tanh(lg_ref[...].reshape(B * tl, C))
    o_ref[...] = _rms(lru, g_ref[...]).astype(BF16).reshape(B, tl, C)


def _lru(lx3, lg3, conv_w, conv_b, wa_bd, ba, wx_bd, bx, lam, g_lru):
    B, S, C = lx3.shape
    tl = TL_LRU
    blk = lambda i: (0, i, 0)
    const = lambda i: (0, 0)
    vec = pl.BlockSpec((1, C), const)
    return pl.pallas_call(
        _lru_kernel,
        grid=(S // tl,),
        in_specs=[pl.BlockSpec((B, tl, C), blk), pl.BlockSpec((B, tl, C), blk),
                  pl.BlockSpec((CONV_WIDTH, C), const), vec,
                  pl.BlockSpec((C, C), const), vec, pl.BlockSpec((C, C), const), vec, vec, vec],
        out_specs=pl.BlockSpec((B, tl, C), blk),
        out_shape=jax.ShapeDtypeStruct((B, S, C), BF16),
        scratch_shapes=[pltpu.VMEM((B, tl + SUBLANES, C), F32),
                        pltpu.VMEM((B, tl, C), F32), pltpu.VMEM((B, tl, C), F32),
                        pltpu.VMEM((B, tl, C), F32), pltpu.VMEM((B, 1, C), F32)],
        compiler_params=_cparams(("arbitrary",)),
        name="lru",
    )(lx3, lg3, conv_w, conv_b, wa_bd, ba, wx_bd, bx, lam, g_lru)


def _outproj_kernel(x_ref, ma_ref, mb_ref, wo_ref, g_ref, wr_ref, br_ref,
                    x1_ref, h2_ref, eid_ref, wt_ref, wo_b):
    @pl.when(pl.program_id(0) == 0)
    def _():
        wo_b[...] = wo_ref[...].astype(BF16)

    half = ma_ref.shape[1]
    x1 = (x_ref[...]
          + jnp.dot(ma_ref[...], wo_b[0:half, :], preferred_element_type=F32)
          + jnp.dot(mb_ref[...], wo_b[half:, :], preferred_element_type=F32))
    x1_ref[...] = x1
    h2 = _rms(x1, g_ref[...])
    h2_ref[...] = h2.astype(BF16).reshape(h2_ref.shape)
    h_hi = h2.astype(BF16)
    h_lo = (h2 - h_hi.astype(F32)).astype(BF16)
    hw = jnp.dot(h_hi, wr_ref[...], preferred_element_type=F32)
    logits = (hw[:, :LANES] + hw[:, LANES:]
              + jnp.dot(h_lo, wr_ref[:, :LANES], preferred_element_type=F32) + br_ref[...])
    tm = logits.shape[0]
    lane = lax.broadcasted_iota(I32, (tm, LANES), 1)
    lane_f = lane.astype(F32)
    big = float(LANES)

    def first_argmax(vals):
        m = jnp.max(vals, axis=-1, keepdims=True)
        idx = jnp.min(jnp.where(vals == m, lane_f, big), axis=-1, keepdims=True)
        return m, idx.astype(I32)

    gl = jnp.where(lane < N_GROUPS, logits, NEG_INF)
    gmax, gidx = first_argmax(gl)
    gsum = jnp.sum(jnp.where(lane < N_GROUPS, jnp.exp(gl - gmax), 0.0), axis=-1, keepdims=True)
    g_top_p = 1.0 / gsum
    lo = N_GROUPS + EXPERTS_PER_GROUP * gidx
    el = jnp.where((lane >= lo) & (lane < lo + EXPERTS_PER_GROUP), logits, NEG_INF)
    m1, i1 = first_argmax(el)
    el2 = jnp.where(lane == i1, NEG_INF, el)
    m2, i2 = first_argmax(el2)
    ratio = jnp.exp(m2 - m1)
    w1 = g_top_p / (1.0 + ratio)
    w2 = g_top_p * ratio / (1.0 + ratio)
    eid_ref[...] = jnp.where(lane == 0, i1 - N_GROUPS, jnp.where(lane == 1, i2 - N_GROUPS, 0))
    wt_ref[...] = jnp.where(lane == 0, w1, jnp.where(lane == 1, w2, 0.0))


def _outproj(x2, mix_a, mix_b, w_out, g_ffn, w_router, b_router):
    T, D = x2.shape
    tm = TM_PROJ
    half = mix_a.shape[1]
    row = lambda i: (i, 0)
    const = lambda i: (0, 0)
    return pl.pallas_call(
        _outproj_kernel,
        grid=(T // tm,),
        in_specs=[pl.BlockSpec((tm, D), row), pl.BlockSpec((tm, half), row),
                  pl.BlockSpec((tm, half), row), pl.BlockSpec(w_out.shape, const, **_FETCH_ONCE),
                  pl.BlockSpec((1, D), const), pl.BlockSpec((D, 2 * LANES), const),
                  pl.BlockSpec((1, LANES), const)],
        out_specs=[pl.BlockSpec((tm, D), row),
                   pl.BlockSpec((tm, D // LANES, LANES), lambda i: (i, 0, 0)),
                   pl.BlockSpec((tm, LANES), row), pl.BlockSpec((tm, LANES), row)],
        out_shape=[jax.ShapeDtypeStruct((T, D), F32),
                   jax.ShapeDtypeStruct((T, D // LANES, LANES), BF16),
                   jax.ShapeDtypeStruct((T, LANES), I32), jax.ShapeDtypeStruct((T, LANES), F32)],
        scratch_shapes=[pltpu.VMEM(w_out.shape, BF16)],
        compiler_params=_cparams(("arbitrary",)),
        name="outproj",
    )(x2, mix_a, mix_b, w_out, g_ffn, w_router, b_router)


def _lane_cumsum(x):
    lane = lax.broadcasted_iota(I32, x.shape, 1)
    shift = 1
    while shift < LANES:
        x = x + jnp.where(lane >= shift, pltpu.roll(x, shift, 1), 0)
        shift *= 2
    return x


def _plan_kernel(eid_ref, lrow_d_ref, lrow_c_ref, te_ref, nv_ref, meta_ref, tcnt_ref, tstart_ref,
                 cnt_scr, tcnt_scr, tpre_scr):
    i = pl.program_id(0)
    tm = eid_ref.shape[0]
    tc = TM_COMBINE
    group = TM_DISPATCH // TM_COMBINE
    nsub = tm // tc
    lane = lax.broadcasted_iota(I32, (tm, LANES), 1)
    eid = eid_ref[...]
    sel1 = lane == eid[:, 0:1]
    sel2 = lane == eid[:, 1:2]

    @pl.when(i == 0)
    def _():
        cnt_scr[...] = jnp.zeros_like(cnt_scr)

    onehot = (sel1 | sel2).astype(BF16)
    r = lax.broadcasted_iota(I32, (tm, tm), 0)
    c = lax.broadcasted_iota(I32, (tm, tm), 1)
    strict_lower = (r > c).astype(BF16)
    within = jnp.dot(strict_lower, onehot, preferred_element_type=F32)
    base = cnt_scr[0:1, :]
    onehot_f = onehot.astype(F32)
    row = lax.broadcasted_iota(I32, (tm, LANES), 0)
    srow = lax.broadcasted_iota(I32, (2 * SUBLANES, LANES), 0)
    counts = [jnp.sum(onehot_f[s * tc:(s + 1) * tc], axis=0, keepdims=True) for s in range(nsub)]
    cnt_mat = jnp.zeros((2 * SUBLANES, LANES), F32)
    seen = []
    total = jnp.zeros((1, LANES), F32)
    for s in range(nsub):
        seen.append(total)
        tile = i * nsub + s
        tcnt_scr[pl.ds(tile, 1), :] = counts[s]
        tpre_scr[pl.ds(tile, 1), :] = base + total
        cnt_mat = jnp.where(srow == s, counts[s], cnt_mat)
        cnt_mat = jnp.where(srow == SUBLANES + s // group, cnt_mat + counts[s], cnt_mat)
        total = total + counts[s]
    cnt_i = cnt_mat.astype(I32)
    run_start = (_lane_cumsum(cnt_i) - cnt_i).astype(F32)

    def local_rows(first_tile_of, table_row_of):
        local = None
        for s in range(nsub):
            val = within - seen[first_tile_of(s)] + run_start[table_row_of(s):table_row_of(s) + 1, :]
            local = val if local is None else jnp.where(row >= s * tc, val, local)
        l1 = jnp.sum(jnp.where(sel1, local, 0.0), axis=-1, keepdims=True)
        l2 = jnp.sum(jnp.where(sel2, local, 0.0), axis=-1, keepdims=True)
        return jnp.where(lane == 0, l1, jnp.where(lane == 1, l2, 0.0)).astype(I32)

    lrow_c_ref[...] = local_rows(lambda s: s, lambda s: s)
    lrow_d_ref[...] = local_rows(lambda s: (s // group) * group, lambda s: SUBLANES + s // group)
    cnt_scr[...] = cnt_scr[...] + total

    @pl.when(i == pl.num_programs(0) - 1)
    def _():
        shift = int(math.log2(TM_EXPERT))
        cnt = cnt_scr[...].astype(I32)
        padded = ((cnt + (TM_EXPERT - 1)) >> shift) << shift
        ends = _lane_cumsum(padded)
        nt = te_ref.shape[0]
        end_tile = (ends >> shift)[0:1, :]
        tl_lane = lax.broadcasted_iota(I32, (nt, LANES), 1)
        tile = lax.broadcasted_iota(I32, (nt, LANES), 0)
        done = jnp.where((tl_lane < N_EXPERTS) & (end_tile <= tile), 1.0, 0.0)
        te = jnp.minimum(jnp.sum(done, axis=-1, keepdims=True).astype(I32), N_EXPERTS - 1)
        n_valid = jnp.sum(jnp.where(tl_lane[0:1, :] == N_EXPERTS - 1, end_tile, 0).astype(F32),
                          axis=-1, keepdims=True).astype(I32)
        nv_ref[...] = jnp.broadcast_to(n_valid, nv_ref.shape)
        seg_end = jnp.sum(jnp.where(tl_lane == te, end_tile, 0).astype(F32), axis=-1,
                          keepdims=True).astype(I32)
        nxt = jnp.sum(jnp.where((tl_lane < N_EXPERTS) & (end_tile <= seg_end), 1.0, 0.0), axis=-1,
                      keepdims=True).astype(I32)
        nxt = jnp.where(seg_end < n_valid, jnp.minimum(nxt, N_EXPERTS - 1), -1)
        te_ref[...] = jnp.where(tl_lane == 1, nxt, te)
        row = lax.broadcasted_iota(I32, cnt.shape, 0)
        meta_ref[...] = jnp.where(row == 0, ends - padded + cnt, jnp.where(row == 1, padded - cnt, 0))
        tcnt_ref[...] = tcnt_scr[...].astype(I32)
        tstart_ref[...] = (tpre_scr[...] + (ends - padded).astype(F32)[0:1, :]).astype(I32)


def _plan(eid, n_tiles):
    T = eid.shape[0]
    tm = TM_PLAN
    steps = T // tm
    n_ctiles = T // TM_COMBINE
    assert TM_DISPATCH % TM_COMBINE == 0 and tm % TM_DISPATCH == 0 and tm // TM_COMBINE <= SUBLANES
    row = lambda i: (i, 0)
    const = lambda i: (0, 0)
    return pl.pallas_call(
        _plan_kernel,
        grid=(steps,),
        in_specs=[pl.BlockSpec((tm, LANES), row)],
        out_specs=[pl.BlockSpec((tm, LANES), row), pl.BlockSpec((tm, LANES), row),
                   pl.BlockSpec((n_tiles, LANES), const), pl.BlockSpec((SUBLANES, LANES), const),
                   pl.BlockSpec((SUBLANES, LANES), const),
                   pl.BlockSpec((n_ctiles, LANES), const), pl.BlockSpec((n_ctiles, LANES), const)],
        out_shape=[jax.ShapeDtypeStruct((T, LANES), I32), jax.ShapeDtypeStruct((T, LANES), I32),
                   jax.ShapeDtypeStruct((n_tiles, LANES), I32),
                   jax.ShapeDtypeStruct((SUBLANES, LANES), I32),
                   jax.ShapeDtypeStruct((SUBLANES, LANES), I32),
                   jax.ShapeDtypeStruct((n_ctiles, LANES), I32),
                   jax.ShapeDtypeStruct((n_ctiles, LANES), I32)],
        scratch_shapes=[pltpu.VMEM((SUBLANES, LANES), F32),
                        pltpu.VMEM((n_ctiles, LANES), F32), pltpu.VMEM((n_ctiles, LANES), F32)],
        compiler_params=_cparams(("arbitrary",)),
        name="plan",
    )(eid)


_PAD_BITS = tuple(1 << b for b in reversed(range(int(math.log2(TM_EXPERT)))))


def _run_bits(max_rows):
    return tuple(1 << b for b in reversed(range(int(math.log2(max_rows)) + 1)))


def _dispatch_kernel(tcnt_ref, tstart_ref, seg_end_ref, seg_pad_ref, nv_ref, lrow_ref, h_ref,
                     xs_ref, lbuf_a, lbuf_b, zeros, sem, zsem):
    i = pl.program_id(0)
    n = pl.num_programs(0)
    td = h_ref.shape[0]
    nr = 2 * td

    def wait_buf(buf, s):
        pltpu.make_async_copy(buf, xs_ref.at[pl.ds(0, nr)], sem.at[s]).wait()

    def step(cur, cur_s, prev, prev_s):
        @pl.when(i >= 2)
        def _():
            wait_buf(cur, cur_s)

        local_row = lrow_ref[...].astype(F32).T
        r = lax.broadcasted_iota(I32, (nr, td), 0).astype(F32)
        onehot = ((r == local_row[0:1, :]) | (r == local_row[1:2, :])).astype(BF16)
        sorted_rows = jnp.dot(onehot, h_ref[...].reshape(td, -1), preferred_element_type=F32)
        cur[...] = sorted_rows.astype(BF16).reshape(cur.shape)

        table_row = jnp.maximum(i - 1, 0) * N_EXPERTS
        lo = 0
        for e in range(N_EXPERTS):
            cnt = jnp.where(i >= 1, tcnt_ref[table_row + e], 0)
            dst = tstart_ref[table_row + e]
            for bit in _run_bits(td):
                off = cnt & ~(2 * bit - 1)

                @pl.when((cnt & bit) != 0)
                def _():
                    pltpu.make_async_copy(prev.at[pl.ds(lo + off, bit)],
                                          xs_ref.at[pl.ds(dst + off, bit)], sem.at[prev_s]).start()
            lo = lo + cnt

        @pl.when(i == n - 1)
        def _():
            wait_buf(prev, prev_s)

    pl.when(i % 2 == 0)(functools.partial(step, lbuf_a, 0, lbuf_b, 1))
    pl.when(i % 2 == 1)(functools.partial(step, lbuf_b, 1, lbuf_a, 0))

    @pl.when(i == n - 1)
    def _():
        zeros[...] = jnp.zeros_like(zeros)
        n_tiles = xs_ref.shape[0] // TM_EXPERT
        first_tail = n_tiles - N_EXPERTS

        def pad_copy(e, bit):
            n = seg_pad_ref[e]
            dst = seg_end_ref[e] + (n & ~(2 * bit - 1))
            return (n & bit) != 0, pltpu.make_async_copy(zeros.at[pl.ds(0, bit)],
                                                         xs_ref.at[pl.ds(dst, bit)], zsem)

        def tail_copy(j):
            return j >= nv_ref[0], pltpu.make_async_copy(
                zeros, xs_ref.at[pl.ds(j * TM_EXPERT, TM_EXPERT)], zsem)

        def for_each_fill(act):
            def per_expert(e, c):
                for bit in _PAD_BITS:
                    pred, cp = pad_copy(e, bit)
                    pl.when(pred)(functools.partial(act, cp))
                return c

            lax.fori_loop(0, N_EXPERTS, per_expert, 0)

            def per_tile(j, c):
                pred, cp = tail_copy(j)
                pl.when(pred)(functools.partial(act, cp))
                return c

            lax.fori_loop(first_tail, n_tiles, per_tile, 0)

        for_each_fill(lambda cp: cp.start())
        for_each_fill(lambda cp: cp.wait())


def _dispatch(tile_cnt, tile_start, seg_end, seg_pad, n_valid, lrow, h2, n_rows):
    T = h2.shape[0]
    slab = h2.shape[1:]
    td = TM_DISPATCH
    nt = T // td
    smem = pl.BlockSpec(memory_space=pltpu.SMEM)
    return pl.pallas_call(
        _dispatch_kernel,
        grid=(nt + 1,),
        in_specs=[smem, smem, smem, smem, smem,
                  pl.BlockSpec((td, LANES), lambda i: (jnp.minimum(i, nt - 1), 0)),
                  pl.BlockSpec((td,) + slab, lambda i: (jnp.minimum(i, nt - 1), 0, 0))],
        out_specs=pl.BlockSpec(memory_space=pl.ANY),
        out_shape=jax.ShapeDtypeStruct((n_rows,) + slab, h2.dtype),
        scratch_shapes=[pltpu.VMEM((2 * td,) + slab, h2.dtype),
                        pltpu.VMEM((2 * td,) + slab, h2.dtype),
                        pltpu.VMEM((TM_EXPERT,) + slab, h2.dtype),
                        pltpu.SemaphoreType.DMA((2,)), pltpu.SemaphoreType.DMA(())],
        compiler_params=_cparams(("arbitrary",)),
        name="dispatch",
    )(tile_cnt, tile_start, seg_end, seg_pad, n_valid, lrow, h2)


def _experts_kernel(te_ref, tn_ref, nv_ref, xs_ref, wg_ref, wu_ref, wd_ref, ys_ref,
                    wg_f, wu_f, wd_f, wg_b, wu_b, wd_b, slot_ref, sem):
    j = pl.program_id(0)
    valid = j < nv_ref[0]
    new_expert = valid & ((j == 0) | (te_ref[j] != te_ref[jnp.maximum(j - 1, 0)]))

    def fetch(e, s):
        return (pltpu.make_async_copy(wg_ref.at[e], wg_f.at[s], sem.at[s]),
                pltpu.make_async_copy(wu_ref.at[e], wu_f.at[s], sem.at[s]),
                pltpu.make_async_copy(wd_ref.at[e], wd_f.at[s], sem.at[s]))

    @pl.when(j == 0)
    def _():
        slot_ref[0] = 0
        for cp in fetch(te_ref[0], 0):
            cp.start()

    @pl.when(new_expert)
    def _():
        s = slot_ref[0]
        for cp in fetch(te_ref[j], s):
            cp.wait()
        nxt = tn_ref[j]

        @pl.when(nxt >= 0)
        def _():
            for cp in fetch(nxt, 1 - s):
                cp.start()

        wg_b[...] = wg_f[s].astype(BF16)
        wu_b[...] = wu_f[s].astype(BF16)
        wd_b[...] = wd_f[s].astype(BF16)
        slot_ref[0] = 1 - s

    @pl.when(valid)
    def _():
        x = xs_ref[...].reshape(xs_ref.shape[0], -1)
        y = None
        for lo in range(0, wg_b.shape[1], EXPERT_FF_BLOCK):
            cols = slice(lo, lo + EXPERT_FF_BLOCK)
            a = jnp.dot(x, wg_b[:, cols], preferred_element_type=F32)
            u = jnp.dot(x, wu_b[:, cols], preferred_element_type=F32)
            hid = (a * jax.nn.sigmoid(a) * u).astype(BF16)
            part = jnp.dot(hid, wd_b[cols, :], preferred_element_type=F32)
            y = part if y is None else y + part
        ys_ref[...] = y.astype(BF16).reshape(ys_ref.shape)

    @pl.when(j >= nv_ref[0])
    def _():
        ys_ref[...] = jnp.zeros_like(ys_ref)


def _experts(tile_expert, tile_next, n_valid, xs, wg, wu, wd):
    R = xs.shape[0]
    slab = xs.shape[1:]
    E, D, Fh = wg.shape
    tm = TM_EXPERT
    n_tiles = R // tm
    rows = lambda j, te, tn, nv: (j, 0, 0)
    rows_in = lambda j, te, tn, nv: (jnp.minimum(j, nv[0] - 1), 0, 0)
    hbm = pl.BlockSpec(memory_space=pl.ANY)
    return pl.pallas_call(
        _experts_kernel,
        grid_spec=pltpu.PrefetchScalarGridSpec(
            num_scalar_prefetch=3,
            grid=(n_tiles,),
            in_specs=[pl.BlockSpec((tm,) + slab, rows_in), hbm, hbm, hbm],
            out_specs=pl.BlockSpec((tm,) + slab, rows),
            scratch_shapes=[pltpu.VMEM((2, D, Fh), F32), pltpu.VMEM((2, D, Fh), F32),
                            pltpu.VMEM((2, Fh, D), F32),
                            pltpu.VMEM((D, Fh), BF16), pltpu.VMEM((D, Fh), BF16),
                            pltpu.VMEM((Fh, D), BF16),
                            pltpu.SMEM((1,), I32), pltpu.SemaphoreType.DMA((2,))]),
        out_shape=jax.ShapeDtypeStruct(xs.shape, xs.dtype),
        compiler_params=_cparams(("arbitrary",)),
        name="experts",
    )(tile_expert, tile_next, n_valid, xs, wg, wu, wd)


def _combine_kernel(tcnt_ref, tstart_ref, lrow_ref, x1_ref, wt_ref, p_ref, gp_ref, wpg_ref, wpp_ref,
                    gf_ref, ys_ref, o_ref, lbuf_a, lbuf_b, wpg_b, wpp_b, sem):
    i = pl.program_id(0)
    n = pl.num_programs(0)
    tm = x1_ref.shape[0]
    nr = 2 * tm

    def request(tile, live, buf, s):
        lo = 0
        for e in range(N_EXPERTS):
            cnt = jnp.where(live, tcnt_ref[tile * N_EXPERTS + e], 0)
            src = tstart_ref[tile * N_EXPERTS + e]
            for bit in _run_bits(tm):
                off = cnt & ~(2 * bit - 1)

                @pl.when((cnt & bit) != 0)
                def _():
                    pltpu.make_async_copy(ys_ref.at[pl.ds(src + off, bit)],
                                          buf.at[pl.ds(lo + off, bit)], sem.at[s]).start()
            lo = lo + cnt

    @pl.when(i == 0)
    def _():
        request(0, True, lbuf_a, 0)
        wpg_b[...] = wpg_ref[...].astype(BF16)
        wpp_b[...] = wpp_ref[...].astype(BF16)

    def step(cur, cur_s, nxt, nxt_s):
        pltpu.make_async_copy(ys_ref.at[pl.ds(0, nr)], cur, sem.at[cur_s]).wait()
        request(jnp.minimum(i + 1, n - 1), i + 1 < n, nxt, nxt_s)
        rows = cur[...].reshape(nr, -1)
        local_row = lrow_ref[...]
        r = lax.broadcasted_iota(I32, (tm, nr), 1)
        y0 = jnp.dot((r == local_row[:, 0:1]).astype(BF16), rows, preferred_element_type=F32)
        y1 = jnp.dot((r == local_row[:, 1:2]).astype(BF16), rows, preferred_element_type=F32)
        wt = wt_ref[...]
        x2 = x1_ref[...] + wt[:, 0:1] * y0 + wt[:, 1:2] * y1
        hp = _rms(x2, gp_ref[...]).astype(BF16)
        gate = jax.nn.sigmoid(jnp.dot(hp, wpg_b[...], preferred_element_type=F32))
        proj = jnp.dot(p_ref[...].astype(BF16), wpp_b[...], preferred_element_type=F32)
        x3 = x2 + gate * proj
        o_ref[...] = _rms(x3, gf_ref[...])

    pl.when(i % 2 == 0)(functools.partial(step, lbuf_a, 0, lbuf_b, 1))
    pl.when(i % 2 == 1)(functools.partial(step, lbuf_b, 1, lbuf_a, 0))


def _combine(tile_cnt, tile_start, lrow, x1, wts, p2, g_ple, w_ple_gate, w_ple_proj, g_final, ys):
    T, D = x1.shape
    P = p2.shape[1]
    tm = TM_COMBINE
    n = T // tm
    row = lambda i: (i, 0)
    const = lambda i: (0, 0)
    smem = pl.BlockSpec(memory_space=pltpu.SMEM)
    return pl.pallas_call(
        _combine_kernel,
        grid=(n,),
        in_specs=[smem, smem, pl.BlockSpec((tm, LANES), row),
                  pl.BlockSpec((tm, D), row), pl.BlockSpec((tm, LANES), row),
                  pl.BlockSpec((tm, P), row), pl.BlockSpec((1, D), const),
                  pl.BlockSpec((D, D), const, **_FETCH_ONCE), pl.BlockSpec((P, D), const, **_FETCH_ONCE),
                  pl.BlockSpec((1, D), const), pl.BlockSpec(memory_space=pl.ANY)],
        out_specs=pl.BlockSpec((tm, D), row),
        out_shape=jax.ShapeDtypeStruct((T, D), F32),
        scratch_shapes=[pltpu.VMEM((2 * tm,) + ys.shape[1:], ys.dtype),
                        pltpu.VMEM((2 * tm,) + ys.shape[1:], ys.dtype),
                        pltpu.VMEM((D, D), BF16), pltpu.VMEM((P, D), BF16),
                        pltpu.SemaphoreType.DMA((2,))],
        compiler_params=_cparams(("arbitrary",)),
        name="combine",
    )(tile_cnt, tile_start, lrow, x1, wts, p2, g_ple, w_ple_gate, w_ple_proj, g_final, ys)


def _block_diag(w):
    nb, d, _ = w.shape
    eye = jnp.eye(nb, dtype=w.dtype)
    return (eye[:, None, :, None] * w[:, :, None, :]).reshape(nb * d, nb * d)


def kernel(x, p, positions, g_mix, w_in, sinks, conv_w, conv_b, lru_wa, lru_ba, lru_wx, lru_bx,
           lru_lambda, g_attn_out, g_lru_out, w_out, g_ffn, w_router_group, b_router_group,
           w_router_expert, b_router_expert, w_expert_gate, w_expert_up, w_expert_down, g_ple,
           w_ple_gate, w_ple_proj, g_final):
    B, S, D = x.shape
    assert w_in.shape[0] == 1, "single-layer block only"
    T = B * S
    lru_w = conv_w.shape[-1]
    n_rows = 2 * T + N_EXPERTS * TM_EXPERT
    n_tiles = n_rows // TM_EXPERT
    pos3 = positions.reshape(T // TM_INPROJ, TM_INPROJ // LANES, LANES).astype(I32)
    x2 = x.reshape(T, D)
    q, k, v, lx, lg = _inproj(x2, pos3, g_mix[0][None], w_in[0], lru_w)
    mix_a = _attention(q, k, v, sinks[0], g_attn_out[0][None], B, S)
    mix_b = _lru(lx.reshape(B, S, lru_w), lg.reshape(B, S, lru_w), conv_w[0], conv_b[0][None],
                 _block_diag(lru_wa[0]).astype(BF16), lru_ba[0][None],
                 _block_diag(lru_wx[0]).astype(BF16), lru_bx[0][None],
                 lru_lambda[0][None], g_lru_out[0][None]).reshape(T, lru_w)
    n_router = N_GROUPS + N_EXPERTS
    w_router = jnp.pad(jnp.concatenate([w_router_group[0], w_router_expert[0]], axis=1),
                       ((0, 0), (0, LANES - n_router)))
    b_router = jnp.pad(jnp.concatenate([b_router_group[0], b_router_expert[0]]),
                       (0, LANES - n_router))[None]
    w_router_hi = w_router.astype(BF16)
    w_router_lo = (w_router - w_router_hi.astype(F32)).astype(BF16)
    x1, h2, eid, wts = _outproj(x2, mix_a, mix_b, w_out[0], g_ffn[0][None],
                                jnp.concatenate([w_router_hi, w_router_lo], axis=1), b_router)
    lrow_d, lrow_c, tile_expert, n_valid, meta, tile_cnt, tile_start = _plan(eid, n_tiles)
    n_valid = n_valid[0, :1]
    group = TM_DISPATCH // TM_COMBINE
    tile_cnt = tile_cnt[:, :N_EXPERTS]
    tile_start = tile_start[:, :N_EXPERTS]
    dtile_cnt = tile_cnt.reshape(-1, group, N_EXPERTS).sum(axis=1)
    dtile_start = tile_start.reshape(-1, group, N_EXPERTS)[:, 0]
    xs = _dispatch(dtile_cnt.reshape(-1), dtile_start.reshape(-1), meta[0], meta[1], n_valid,
                   lrow_d, h2, n_rows)
    Fh = w_expert_gate.shape[-1]
    ys = _experts(tile_expert[:, 0], tile_expert[:, 1], n_valid, xs,
                  w_expert_gate[0].reshape(N_EXPERTS, D, Fh),
                  w_expert_up[0].reshape(N_EXPERTS, D, Fh),
                  w_expert_down[0].reshape(N_EXPERTS, Fh, D))
    out = _combine(tile_cnt.reshape(-1), tile_start.reshape(-1), lrow_c, x1, wts,
                   p[0].reshape(T, -1), g_ple[0][None], w_ple_gate[0], w_ple_proj[0], g_final[None], ys)
    return out.reshape(B, S, D)
```

```python
import functools
import math

import jax
import jax.numpy as jnp
from jax import lax
from jax.experimental import pallas as pl
from jax.experimental.pallas import tpu as pltpu

F32 = jnp.float32
BF16 = jnp.bfloat16
I32 = jnp.int32

EPS = 1e-6
N_HEADS = 8
N_KV_HEADS = 2
HEAD_DIM = 64
WINDOW = 128
ROPE_THETA = 10000.0
CONV_WIDTH = 4
LRU_C = 8.0
N_GROUPS = 4
EXPERTS_PER_GROUP = 8
N_EXPERTS = N_GROUPS * EXPERTS_PER_GROUP
NEG_INF = -1e30
LANES = 128
SUBLANES = 8

TM_INPROJ = 512
TM_PROJ = 512
TQ_ATTN = 512
TL_LRU = 256
TM_PLAN = 1024
TM_EXPERT = 512
EXPERT_FF_BLOCK = 256
TM_DISPATCH = 512
COMBINE_SPLIT = 1
TM_COMBINE = 256
VMEM_LIMIT = 56 * 1024 * 1024


def _rms(x, g):
    ms = jnp.mean(x * x, axis=-1, keepdims=True)
    return x * lax.rsqrt(ms + EPS) * g


_FETCH_ONCE = dict(pipeline_mode=pl.Buffered(1))


def _cparams(sem):
    return pltpu.CompilerParams(dimension_semantics=sem, vmem_limit_bytes=VMEM_LIMIT)


def _inproj_kernel(x_ref, pos_ref, g_ref, w_ref, q_ref, k_ref, v_ref, lx_ref, lg_ref, w_b):
    @pl.when(pl.program_id(0) == 0)
    def _():
        w_b[...] = w_ref[...].astype(BF16)

    h = _rms(x_ref[...], g_ref[...]).astype(BF16)
    proj = jnp.dot(h, w_b[...], preferred_element_type=F32)
    lane = lax.broadcasted_iota(I32, (1, LANES), 1)
    n_freq = HEAD_DIM // 2
    fidx = lax.broadcasted_iota(I32, (n_freq, 1), 0).astype(F32)
    inv_freq = jnp.exp(fidx * (-2.0 / HEAD_DIM * math.log(ROPE_THETA)))
    pos = pos_ref[0].astype(F32)
    cos_blocks, sin_blocks = [], []
    for c in range(pos.shape[0]):
        ang_t = inv_freq * pos[c:c + 1, :]
        cos_blocks.append(jnp.concatenate([jnp.cos(ang_t)] * (LANES // n_freq), axis=0).T)
        sin_blocks.append(jnp.concatenate([jnp.sin(ang_t)] * (LANES // n_freq), axis=0).T)
    cos = jnp.concatenate(cos_blocks, axis=0)
    sin = jnp.concatenate(sin_blocks, axis=0)
    first_half = (lane % HEAD_DIM) < (HEAD_DIM // 2)
    sin_signed = jnp.where(first_half, -sin, sin)

    def rope(t):
        partner = jnp.where(first_half, pltpu.roll(t, LANES - HEAD_DIM // 2, 1),
                            pltpu.roll(t, HEAD_DIM // 2, 1))
        return t * cos + partner * sin_signed

    attn_w = N_HEADS * HEAD_DIM
    scale = HEAD_DIM ** -0.5
    assert math.frexp(scale)[0] == 0.5, "score scale must be a power of two to fold into q exactly"
    for c in range(attn_w // LANES):
        q_ref[:, c * LANES:(c + 1) * LANES] = (
            rope(proj[:, c * LANES:(c + 1) * LANES]) * scale).astype(BF16)
    kv_w = N_KV_HEADS * HEAD_DIM
    assert kv_w == LANES
    k = rope(proj[:, attn_w:attn_w + kv_w])
    v = proj[:, attn_w + kv_w:attn_w + 2 * kv_w]
    k_ref[:, :kv_w] = k.astype(BF16)
    k_ref[:, kv_w:] = pltpu.roll(k, HEAD_DIM, 1).astype(BF16)
    v_ref[:, :kv_w] = v.astype(BF16)
    v_ref[:, kv_w:] = pltpu.roll(v, HEAD_DIM, 1).astype(BF16)
    off_lx = attn_w + 2 * kv_w
    lru_w = lx_ref.shape[1]
    lx_ref[...] = proj[:, off_lx:off_lx + lru_w]
    lg_ref[...] = proj[:, off_lx + lru_w:off_lx + 2 * lru_w]


def _inproj(x2, pos3, g_mix, w_in, lru_w):
    T, D = x2.shape
    tm = TM_INPROJ
    attn_w = N_HEADS * HEAD_DIM
    kv_w = N_KV_HEADS * HEAD_DIM
    row = lambda i: (i, 0)
    const = lambda i: (0, 0)
    return pl.pallas_call(
        _inproj_kernel,
        grid=(T // tm,),
        in_specs=[pl.BlockSpec((tm, D), row),
                  pl.BlockSpec((1, tm // LANES, LANES), lambda i: (i, 0, 0)),
                  pl.BlockSpec((1, D), const), pl.BlockSpec(w_in.shape, const, **_FETCH_ONCE)],
        out_specs=[pl.BlockSpec((tm, attn_w), row), pl.BlockSpec((tm, 2 * kv_w), row),
                   pl.BlockSpec((tm, 2 * kv_w), row), pl.BlockSpec((tm, lru_w), row),
                   pl.BlockSpec((tm, lru_w), row)],
        out_shape=[jax.ShapeDtypeStruct((T, attn_w), BF16),
                   jax.ShapeDtypeStruct((T, 2 * kv_w), BF16),
                   jax.ShapeDtypeStruct((T, 2 * kv_w), BF16), jax.ShapeDtypeStruct((T, lru_w), F32),
                   jax.ShapeDtypeStruct((T, lru_w), F32)],
        scratch_shapes=[pltpu.VMEM(w_in.shape, BF16)],
        compiler_params=_cparams(("arbitrary",)),
        name="inproj",
    )(x2, pos3, g_mix, w_in)


def _attn_kernel(sinks_ref, q_ref, k_ref, v_ref, kp_ref, vp_ref, g_ref, o_ref):
    n = pl.program_id(1)
    blk = WINDOW
    hd = HEAD_DIM
    nqb = q_ref.shape[0] // blk
    n_pairs = N_HEADS // 2
    pairs_per_group = n_pairs // N_KV_HEADS
    lane = lax.broadcasted_iota(I32, (1, LANES), 1)
    first = lane < hd
    qi = lax.broadcasted_iota(I32, (blk, 2 * blk), 0)
    kj = lax.broadcasted_iota(I32, (blk, 2 * blk), 1)
    rel = qi + blk - kj
    band = (rel >= 0) & (rel < WINDOW)
    bd_row = lax.broadcasted_iota(I32, (4 * blk, LANES), 0)
    bd_lane = lax.broadcasted_iota(I32, (4 * blk, LANES), 1)
    ones_bd = ((bd_row < 2 * blk) == (bd_lane < hd)).astype(BF16)
    zero = jnp.zeros((), BF16)

    def block_diag(t):
        same, swapped = t[:, :LANES], t[:, LANES:]
        out = []
        for g in range(N_KV_HEADS):
            top = jnp.where(first, same if g == 0 else swapped, zero)
            bot = jnp.where(first, zero, swapped if g == 0 else same)
            out.append(jnp.concatenate([top, bot], axis=0))
        return out

    for j in range(nqb):
        rows = slice(j * blk, (j + 1) * blk)
        if j == 0:
            k_prev, v_prev = kp_ref[...], vp_ref[...]
            mask = band & ((kj >= blk) | (n > 0))
        else:
            prev_rows = slice((j - 1) * blk, j * blk)
            k_prev, v_prev = k_ref[prev_rows, :], v_ref[prev_rows, :]
            mask = band
        k_bd = block_diag(jnp.concatenate([k_prev, k_ref[rows, :]], axis=0))
        v_bd = block_diag(jnp.concatenate([v_prev, v_ref[rows, :]], axis=0))
        outs = []
        for p in range(n_pairs):
            g = p // pairs_per_group
            qp = q_ref[rows, p * LANES:(p + 1) * LANES]
            s = lax.dot_general(qp, k_bd[g], (((1,), (1,)), ((), ())), preferred_element_type=F32)
            es, ms = [], []
            for a in range(2):
                sa = jnp.where(mask, s[:, a * 2 * blk:(a + 1) * 2 * blk], NEG_INF)
                m = jnp.maximum(jnp.max(sa, axis=-1, keepdims=True), sinks_ref[2 * p + a])
                es.append(jnp.exp(sa - m))
                ms.append(m)
            e = jnp.concatenate(es, axis=1).astype(BF16)
            od = jnp.dot(e, jnp.concatenate([v_bd[g], ones_bd], axis=1), preferred_element_type=F32)
            sink_term = jnp.where(first, jnp.exp(sinks_ref[2 * p] - ms[0]),
                                  jnp.exp(sinks_ref[2 * p + 1] - ms[1]))
            outs.append(od[:, :LANES] / (od[:, LANES:] + sink_term))
        attn = jnp.concatenate(outs, axis=1)
        o_ref[rows, :] = _rms(attn, g_ref[...]).astype(BF16)


def _attention(q, k, v, sinks, g_attn, B, S):
    T, attn_w = q.shape
    kv_w = k.shape[1]
    blk = WINDOW
    tq = TQ_ATTN
    nt = S // tq
    cur = lambda b, n: (b * nt + n, 0)
    prev = lambda b, n: (b * (S // blk) + jnp.maximum(n * (tq // blk) - 1, 0), 0)
    return pl.pallas_call(
        _attn_kernel,
        grid=(B, nt),
        in_specs=[pl.BlockSpec(memory_space=pltpu.SMEM),
                  pl.BlockSpec((tq, attn_w), cur),
                  pl.BlockSpec((tq, kv_w), cur), pl.BlockSpec((tq, kv_w), cur),
                  pl.BlockSpec((blk, kv_w), prev), pl.BlockSpec((blk, kv_w), prev),
                  pl.BlockSpec((1, attn_w), lambda b, n: (0, 0))],
        out_specs=pl.BlockSpec((tq, attn_w), cur),
        out_shape=jax.ShapeDtypeStruct((T, attn_w), BF16),
        compiler_params=_cparams(("arbitrary", "arbitrary")),
        name="attn",
    )(sinks, q, k, v, k, v, g_attn)


def _neg_expm1(x, exp_x):
    series = -x * (1.0 + x * (0.5 + x * (1.0 / 6.0 + x * (1.0 / 24.0))))
    return jnp.where(x > -0.02, series, 1.0 - exp_x)


def _gelu_tanh(x):
    return 0.5 * x * (1.0 + jnp.tanh(math.sqrt(2.0 / math.pi) * (x + 0.044715 * (x * x * x))))


def _lru_kernel(lx_ref, lg_ref, cw_ref, cb_ref, wa_ref, ba_ref, wx_ref, bx_ref, lam_ref, g_ref,
                o_ref, xpad, a_scr, b_scr, h_scr, h_carry):
    i = pl.program_id(0)
    B, tl, C = lx_ref.shape
    pad = SUBLANES

    @pl.when(i == 0)
    def _():
        xpad[:, 0:pad, :] = jnp.zeros((B, pad, C), F32)
        h_carry[...] = jnp.zeros_like(h_carry)

    @pl.when(i > 0)
    def _():
        xpad[:, 0:pad, :] = xpad[:, tl:tl + pad, :]

    xpad[:, pad:, :] = lx_ref[...]
    y = jnp.zeros((B, tl, C), F32) + cb_ref[...]
    for j in range(CONV_WIDTH):
        y = y + xpad[:, pl.ds(pad - (CONV_WIDTH - 1) + j, tl), :] * cw_ref[j:j + 1, :]
    y2 = y.reshape(B * tl, C)
    yb = y2.astype(BF16)
    gate_a = jnp.dot(yb, wa_ref[...], preferred_element_type=F32) + ba_ref[...]
    gate_x = jnp.dot(yb, wx_ref[...], preferred_element_type=F32) + bx_ref[...]
    r = jax.nn.sigmoid(gate_a)
    ig = jax.nn.sigmoid(gate_x)
    nl = -lam_ref[...]
    softplus = jnp.maximum(nl, 0.0) + jnp.log1p(jnp.exp(-jnp.abs(nl)))
    log_a = (-LRU_C) * r * softplus
    a_all = jnp.exp(log_a)
    one_minus_a2 = _neg_expm1(2.0 * log_a, a_all * a_all)
    b_all = one_minus_a2 * lax.rsqrt(jnp.maximum(one_minus_a2, 1e-30)) * ig * y2
    groups = (B * tl // SUBLANES, SUBLANES, C)
    sub = lax.broadcasted_iota(I32, groups, 1)
    a_cum, b_loc = a_all.reshape(groups), b_all.reshape(groups)
    d = 1
    while d < SUBLANES:
        keep = sub >= d
        a_prev = pltpu.roll(a_cum, d, 1)
        b_prev = pltpu.roll(b_loc, d, 1)
        b_loc = b_loc + jnp.where(keep, a_cum * b_prev, 0.0)
        a_cum = jnp.where(keep, a_cum * a_prev, a_cum)
        d *= 2
    a_scr[...] = a_cum.reshape(B, tl, C)
    b_scr[...] = b_loc.reshape(B, tl, C)

    def carry(g, h_prev):
        rows = pl.ds(pl.multiple_of(g * SUBLANES, SUBLANES), SUBLANES)
        h = b_scr[:, rows, :] + a_scr[:, rows, :] * h_prev
        h_scr[:, rows, :] = h
        return h[:, SUBLANES - 1:SUBLANES, :]

    h_carry[...] = lax.fori_loop(0, tl // SUBLANES, carry, h_carry[...], unroll=4)
    lru = h_scr[...].reshape(B * tl, C) * _gelu_tanh(lg_ref[...].reshape(B * tl, C))
    o_ref[...] = _rms(lru, g_ref[...]).astype(BF16).reshape(B, tl, C)


def _lru(lx3, lg3, conv_w, conv_b, wa_bd, ba, wx_bd, bx, lam, g_lru):
    B, S, C = lx3.shape
    tl = TL_LRU
    blk = lambda i: (0, i, 0)
    const = lambda i: (0, 0)
    vec = pl.BlockSpec((1, C), const)
    return pl.pallas_call(
        _lru_kernel,
        grid=(S // tl,),
        in_specs=[pl.BlockSpec((B, tl, C), blk), pl.BlockSpec((B, tl, C), blk),
                  pl.BlockSpec((CONV_WIDTH, C), const), vec,
                  pl.BlockSpec((C, C), const), vec, pl.BlockSpec((C, C), const), vec, vec, vec],
        out_specs=pl.BlockSpec((B, tl, C), blk),
        out_shape=jax.ShapeDtypeStruct((B, S, C), BF16),
        scratch_shapes=[pltpu.VMEM((B, tl + SUBLANES, C), F32),
                        pltpu.VMEM((B, tl, C), F32), pltpu.VMEM((B, tl, C), F32),
                        pltpu.VMEM((B, tl, C), F32), pltpu.VMEM((B, 1, C), F32)],
        compiler_params=_cparams(("arbitrary",)),
        name="lru",
    )(lx3, lg3, conv_w, conv_b, wa_bd, ba, wx_bd, bx, lam, g_lru)


def _outproj_kernel(x_ref, ma_ref, mb_ref, wo_ref, g_ref, wr_ref, br_ref,
                    x1_ref, h2_ref, eid_ref, wt_ref, wo_b):
    @pl.when(pl.program_id(0) == 0)
    def _():
        wo_b[...] = wo_ref[...].astype(BF16)

    half = ma_ref.shape[1]
    x1 = (x_ref[...]
          + jnp.dot(ma_ref[...], wo_b[0:half, :], preferred_element_type=F32)
          + jnp.dot(mb_ref[...], wo_b[half:, :], preferred_element_type=F32))
    x1_ref[...] = x1
    h2 = _rms(x1, g_ref[...])
    h2_ref[...] = h2.astype(BF16).reshape(h2_ref.shape)
    h_hi = h2.astype(BF16)
    h_lo = (h2 - h_hi.astype(F32)).astype(BF16)
    hw = jnp.dot(h_hi, wr_ref[...], preferred_element_type=F32)
    logits = (hw[:, :LANES] + hw[:, LANES:]
              + jnp.dot(h_lo, wr_ref[:, :LANES], preferred_element_type=F32) + br_ref[...])
    tm = logits.shape[0]
    lane = lax.broadcasted_iota(I32, (tm, LANES), 1)
    lane_f = lane.astype(F32)
    big = float(LANES)

    def first_argmax(vals):
        m = jnp.max(vals, axis=-1, keepdims=True)
        idx = jnp.min(jnp.where(vals == m, lane_f, big), axis=-1, keepdims=True)
        return m, idx.astype(I32)

    gl = jnp.where(lane < N_GROUPS, logits, NEG_INF)
    gmax, gidx = first_argmax(gl)
    gsum = jnp.sum(jnp.where(lane < N_GROUPS, jnp.exp(gl - gmax), 0.0), axis=-1, keepdims=True)
    g_top_p = 1.0 / gsum
    lo = N_GROUPS + EXPERTS_PER_GROUP * gidx
    el = jnp.where((lane >= lo) & (lane < lo + EXPERTS_PER_GROUP), logits, NEG_INF)
    m1, i1 = first_argmax(el)
    el2 = jnp.where(lane == i1, NEG_INF, el)
    m2, i2 = first_argmax(el2)
    ratio = jnp.exp(m2 - m1)
    w1 = g_top_p / (1.0 + ratio)
    w2 = g_top_p * ratio / (1.0 + ratio)
    eid_ref[...] = jnp.where(lane == 0, i1 - N_GROUPS, jnp.where(lane == 1, i2 - N_GROUPS, 0))
    wt_ref[...] = jnp.where(lane == 0, w1, jnp.where(lane == 1, w2, 0.0))


def _outproj(x2, mix_a, mix_b, w_out, g_ffn, w_router, b_router):
    T, D = x2.shape
    tm = TM_PROJ
    half = mix_a.shape[1]
    row = lambda i: (i, 0)
    const = lambda i: (0, 0)
    return pl.pallas_call(
        _outproj_kernel,
        grid=(T // tm,),
        in_specs=[pl.BlockSpec((tm, D), row), pl.BlockSpec((tm, half), row),
                  pl.BlockSpec((tm, half), row), pl.BlockSpec(w_out.shape, const, **_FETCH_ONCE),
                  pl.BlockSpec((1, D), const), pl.BlockSpec((D, 2 * LANES), const),
                  pl.BlockSpec((1, LANES), const)],
        out_specs=[pl.BlockSpec((tm, D), row),
                   pl.BlockSpec((tm, D // LANES, LANES), lambda i: (i, 0, 0)),
                   pl.BlockSpec((tm, LANES), row), pl.BlockSpec((tm, LANES), row)],
        out_shape=[jax.ShapeDtypeStruct((T, D), F32),
                   jax.ShapeDtypeStruct((T, D // LANES, LANES), BF16),
                   jax.ShapeDtypeStruct((T, LANES), I32), jax.ShapeDtypeStruct((T, LANES), F32)],
        scratch_shapes=[pltpu.VMEM(w_out.shape, BF16)],
        compiler_params=_cparams(("arbitrary",)),
        name="outproj",
    )(x2, mix_a, mix_b, w_out, g_ffn, w_router, b_router)


def _lane_cumsum(x):
    lane = lax.broadcasted_iota(I32, x.shape, 1)
    shift = 1
    while shift < LANES:
        x = x + jnp.where(lane >= shift, pltpu.roll(x, shift, 1), 0)
        shift *= 2
    return x


def _plan_kernel(eid_ref, lrow_d_ref, lrow_c_ref, te_ref, nv_ref, meta_ref, tcnt_ref, tstart_ref,
                 cnt_scr, tcnt_scr, tpre_scr):
    i = pl.program_id(0)
    tm = eid_ref.shape[0]
    tc = TM_COMBINE
    group = TM_DISPATCH // TM_COMBINE
    nsub = tm // tc
    lane = lax.broadcasted_iota(I32, (tm, LANES), 1)
    eid = eid_ref[...]
    sel1 = lane == eid[:, 0:1]
    sel2 = lane == eid[:, 1:2]

    @pl.when(i == 0)
    def _():
        cnt_scr[...] = jnp.zeros_like(cnt_scr)

    onehot = (sel1 | sel2).astype(BF16)
    r = lax.broadcasted_iota(I32, (tm, tm), 0)
    c = lax.broadcasted_iota(I32, (tm, tm), 1)
    strict_lower = (r > c).astype(BF16)
    within = jnp.dot(strict_lower, onehot, preferred_element_type=F32)
    base = cnt_scr[0:1, :]
    onehot_f = onehot.astype(F32)
    row = lax.broadcasted_iota(I32, (tm, LANES), 0)
    srow = lax.broadcasted_iota(I32, (2 * SUBLANES, LANES), 0)
    counts = [jnp.sum(onehot_f[s * tc:(s + 1) * tc], axis=0, keepdims=True) for s in range(nsub)]
    cnt_mat = jnp.zeros((2 * SUBLANES, LANES), F32)
    seen = []
    total = jnp.zeros((1, LANES), F32)
    for s in range(nsub):
        seen.append(total)
        tile = i * nsub + s
        tcnt_scr[pl.ds(tile, 1), :] = counts[s]
        tpre_scr[pl.ds(tile, 1), :] = base + total
        cnt_mat = jnp.where(srow == s, counts[s], cnt_mat)
        cnt_mat = jnp.where(srow == SUBLANES + s // group, cnt_mat + counts[s], cnt_mat)
        total = total + counts[s]
    cnt_i = cnt_mat.astype(I32)
    run_start = (_lane_cumsum(cnt_i) - cnt_i).astype(F32)

    def local_rows(first_tile_of, table_row_of):
        local = None
        for s in range(nsub):
            val = within - seen[first_tile_of(s)] + run_start[table_row_of(s):table_row_of(s) + 1, :]
            local = val if local is None else jnp.where(row >= s * tc, val, local)
        l1 = jnp.sum(jnp.where(sel1, local, 0.0), axis=-1, keepdims=True)
        l2 = jnp.sum(jnp.where(sel2, local, 0.0), axis=-1, keepdims=True)
        return jnp.where(lane == 0, l1, jnp.where(lane == 1, l2, 0.0)).astype(I32)

    lrow_c_ref[...] = local_rows(lambda s: s, lambda s: s)
    lrow_d_ref[...] = local_rows(lambda s: (s // group) * group, lambda s: SUBLANES + s // group)
    cnt_scr[...] = cnt_scr[...] + total

    @pl.when(i == pl.num_programs(0) - 1)
    def _():
        shift = int(math.log2(TM_EXPERT))
        cnt = cnt_scr[...].astype(I32)
        padded = ((cnt + (TM_EXPERT - 1)) >> shift) << shift
        ends = _lane_cumsum(padded)
        nt = te_ref.shape[0]
        end_tile = (ends >> shift)[0:1, :]
        tl_lane = lax.broadcasted_iota(I32, (nt, LANES), 1)
        tile = lax.broadcasted_iota(I32, (nt, LANES), 0)
        done = jnp.where((tl_lane < N_EXPERTS) & (end_tile <= tile), 1.0, 0.0)
        te = jnp.minimum(jnp.sum(done, axis=-1, keepdims=True).astype(I32), N_EXPERTS - 1)
        n_valid = jnp.sum(jnp.where(tl_lane[0:1, :] == N_EXPERTS - 1, end_tile, 0).astype(F32),
                          axis=-1, keepdims=True).astype(I32)
        nv_ref[...] = jnp.broadcast_to(n_valid, nv_ref.shape)
        seg_end = jnp.sum(jnp.where(tl_lane == te, end_tile, 0).astype(F32), axis=-1,
                          keepdims=True).astype(I32)
        nxt = jnp.sum(jnp.where((tl_lane < N_EXPERTS) & (end_tile <= seg_end), 1.0, 0.0), axis=-1,
                      keepdims=True).astype(I32)
        nxt = jnp.where(seg_end < n_valid, jnp.minimum(nxt, N_EXPERTS - 1), -1)
        te_ref[...] = jnp.where(tl_lane == 1, nxt, te)
        row = lax.broadcasted_iota(I32, cnt.shape, 0)
        meta_ref[...] = jnp.where(row == 0, ends - padded + cnt, jnp.where(row == 1, padded - cnt, 0))
        tcnt_ref[...] = tcnt_scr[...].astype(I32)
        tstart_ref[...] = (tpre_scr[...] + (ends - padded).astype(F32)[0:1, :]).astype(I32)


def _plan(eid, n_tiles):
    T = eid.shape[0]
    tm = TM_PLAN
    steps = T // tm
    n_ctiles = T // TM_COMBINE
    assert TM_DISPATCH % TM_COMBINE == 0 and tm % TM_DISPATCH == 0 and tm // TM_COMBINE <= SUBLANES
    row = lambda i: (i, 0)
    const = lambda i: (0, 0)
    return pl.pallas_call(
        _plan_kernel,
        grid=(steps,),
        in_specs=[pl.BlockSpec((tm, LANES), row)],
        out_specs=[pl.BlockSpec((tm, LANES), row), pl.BlockSpec((tm, LANES), row),
                   pl.BlockSpec((n_tiles, LANES), const), pl.BlockSpec((SUBLANES, LANES), const),
                   pl.BlockSpec((SUBLANES, LANES), const),
                   pl.BlockSpec((n_ctiles, LANES), const), pl.BlockSpec((n_ctiles, LANES), const)],
        out_shape=[jax.ShapeDtypeStruct((T, LANES), I32), jax.ShapeDtypeStruct((T, LANES), I32),
                   jax.ShapeDtypeStruct((n_tiles, LANES), I32),
                   jax.ShapeDtypeStruct((SUBLANES, LANES), I32),
                   jax.ShapeDtypeStruct((SUBLANES, LANES), I32),
                   jax.ShapeDtypeStruct((n_ctiles, LANES), I32),
                   jax.ShapeDtypeStruct((n_ctiles, LANES), I32)],
        scratch_shapes=[pltpu.VMEM((SUBLANES, LANES), F32),
                        pltpu.VMEM((n_ctiles, LANES), F32), pltpu.VMEM((n_ctiles, LANES), F32)],
        compiler_params=_cparams(("arbitrary",)),
        name="plan",
    )(eid)


_PAD_BITS = tuple(1 << b for b in reversed(range(int(math.log2(TM_EXPERT)))))


def _run_bits(max_rows):
    return tuple(1 << b for b in reversed(range(int(math.log2(max_rows)) + 1)))


def _dispatch_kernel(tcnt_ref, tstart_ref, seg_end_ref, seg_pad_ref, nv_ref, lrow_ref, h_ref,
                     xs_ref, lbuf_a, lbuf_b, zeros, sem, zsem):
    i = pl.program_id(0)
    n = pl.num_programs(0)
    td = h_ref.shape[0]
    nr = 2 * td

    def wait_buf(buf, s):
        pltpu.make_async_copy(buf, xs_ref.at[pl.ds(0, nr)], sem.at[s]).wait()

    def step(cur, cur_s, prev, prev_s):
        @pl.when(i >= 2)
        def _():
            wait_buf(cur, cur_s)

        local_row = lrow_ref[...].astype(F32).T
        r = lax.broadcasted_iota(I32, (nr, td), 0).astype(F32)
        onehot = ((r == local_row[0:1, :]) | (r == local_row[1:2, :])).astype(BF16)
        sorted_rows = jnp.dot(onehot, h_ref[...].reshape(td, -1), preferred_element_type=F32)
        cur[...] = sorted_rows.astype(BF16).reshape(cur.shape)

        table_row = jnp.maximum(i - 1, 0) * N_EXPERTS
        lo = 0
        for e in range(N_EXPERTS):
            cnt = jnp.where(i >= 1, tcnt_ref[table_row + e], 0)
            dst = tstart_ref[table_row + e]
            for bit in _run_bits(td):
                off = cnt & ~(2 * bit - 1)

                @pl.when((cnt & bit) != 0)
                def _():
                    pltpu.make_async_copy(prev.at[pl.ds(lo + off, bit)],
                                          xs_ref.at[pl.ds(dst + off, bit)], sem.at[prev_s]).start()
            lo = lo + cnt

        @pl.when(i == n - 1)
        def _():
            wait_buf(prev, prev_s)

    pl.when(i % 2 == 0)(functools.partial(step, lbuf_a, 0, lbuf_b, 1))
    pl.when(i % 2 == 1)(functools.partial(step, lbuf_b, 1, lbuf_a, 0))

    @pl.when(i == n - 1)
    def _():
        zeros[...] = jnp.zeros_like(zeros)
        n_tiles = xs_ref.shape[0] // TM_EXPERT
        first_tail = n_tiles - N_EXPERTS

        def pad_copy(e, bit):
            n = seg_pad_ref[e]
            dst = seg_end_ref[e] + (n & ~(2 * bit - 1))
            return (n & bit) != 0, pltpu.make_async_copy(zeros.at[pl.ds(0, bit)],
                                                         xs_ref.at[pl.ds(dst, bit)], zsem)

        def tail_copy(j):
            return j >= nv_ref[0], pltpu.make_async_copy(
                zeros, xs_ref.at[pl.ds(j * TM_EXPERT, TM_EXPERT)], zsem)

        def for_each_fill(act):
            def per_expert(e, c):
                for bit in _PAD_BITS:
                    pred, cp = pad_copy(e, bit)
                    pl.when(pred)(functools.partial(act, cp))
                return c

            lax.fori_loop(0, N_EXPERTS, per_expert, 0)

            def per_tile(j, c):
                pred, cp = tail_copy(j)
                pl.when(pred)(functools.partial(act, cp))
                return c

            lax.fori_loop(first_tail, n_tiles, per_tile, 0)

        for_each_fill(lambda cp: cp.start())
        for_each_fill(lambda cp: cp.wait())


def _dispatch(tile_cnt, tile_start, seg_end, seg_pad, n_valid, lrow, h2, n_rows):
    T = h2.shape[0]
    slab = h2.shape[1:]
    td = TM_DISPATCH
    nt = T // td
    smem = pl.BlockSpec(memory_space=pltpu.SMEM)
    return pl.pallas_call(
        _dispatch_kernel,
        grid=(nt + 1,),
        in_specs=[smem, smem, smem, smem, smem,
                  pl.BlockSpec((td, LANES), lambda i: (jnp.minimum(i, nt - 1), 0)),
                  pl.BlockSpec((td,) + slab, lambda i: (jnp.minimum(i, nt - 1), 0, 0))],
        out_specs=pl.BlockSpec(memory_space=pl.ANY),
        out_shape=jax.ShapeDtypeStruct((n_rows,) + slab, h2.dtype),
        scratch_shapes=[pltpu.VMEM((2 * td,) + slab, h2.dtype),
                        pltpu.VMEM((2 * td,) + slab, h2.dtype),
                        pltpu.VMEM((TM_EXPERT,) + slab, h2.dtype),
                        pltpu.SemaphoreType.DMA((2,)), pltpu.SemaphoreType.DMA(())],
        compiler_params=_cparams(("arbitrary",)),
        name="dispatch",
    )(tile_cnt, tile_start, seg_end, seg_pad, n_valid, lrow, h2)


def _experts_kernel(te_ref, tn_ref, nv_ref, xs_ref, wg_ref, wu_ref, wd_ref, ys_ref,
                    wg_f, wu_f, wd_f, wg_b, wu_b, wd_b, slot_ref, sem):
    j = pl.program_id(0)
    valid = j < nv_ref[0]
    new_expert = valid & ((j == 0) | (te_ref[j] != te_ref[jnp.maximum(j - 1, 0)]))

    def fetch(e, s):
        return (pltpu.make_async_copy(wg_ref.at[e], wg_f.at[s], sem.at[s]),
                pltpu.make_async_copy(wu_ref.at[e], wu_f.at[s], sem.at[s]),
                pltpu.make_async_copy(wd_ref.at[e], wd_f.at[s], sem.at[s]))

    @pl.when(j == 0)
    def _():
        slot_ref[0] = 0
        for cp in fetch(te_ref[0], 0):
            cp.start()

    @pl.when(new_expert)
    def _():
        s = slot_ref[0]
        for cp in fetch(te_ref[j], s):
            cp.wait()
        nxt = tn_ref[j]

        @pl.when(nxt >= 0)
        def _():
            for cp in fetch(nxt, 1 - s):
                cp.start()

        wg_b[...] = wg_f[s].astype(BF16)
        wu_b[...] = wu_f[s].astype(BF16)
        wd_b[...] = wd_f[s].astype(BF16)
        slot_ref[0] = 1 - s

    @pl.when(valid)
    def _():
        x = xs_ref[...].reshape(xs_ref.shape[0], -1)
        y = None
        for lo in range(0, wg_b.shape[1], EXPERT_FF_BLOCK):
            cols = slice(lo, lo + EXPERT_FF_BLOCK)
            a = jnp.dot(x, wg_b[:, cols], preferred_element_type=F32)
            u = jnp.dot(x, wu_b[:, cols], preferred_element_type=F32)
            hid = (a * jax.nn.sigmoid(a) * u).astype(BF16)
            part = jnp.dot(hid, wd_b[cols, :], preferred_element_type=F32)
            y = part if y is None else y + part
        ys_ref[...] = y.astype(BF16).reshape(ys_ref.shape)

    @pl.when(j >= nv_ref[0])
    def _():
        ys_ref[...] = jnp.zeros_like(ys_ref)


def _experts(tile_expert, tile_next, n_valid, xs, wg, wu, wd):
    R = xs.shape[0]
    slab = xs.shape[1:]
    E, D, Fh = wg.shape
    tm = TM_EXPERT
    n_tiles = R // tm
    rows = lambda j, te, tn, nv: (j, 0, 0)
    rows_in = lambda j, te, tn, nv: (jnp.minimum(j, nv[0] - 1), 0, 0)
    hbm = pl.BlockSpec(memory_space=pl.ANY)
    return pl.pallas_call(
        _experts_kernel,
        grid_spec=pltpu.PrefetchScalarGridSpec(
            num_scalar_prefetch=3,
            grid=(n_tiles,),
            in_specs=[pl.BlockSpec((tm,) + slab, rows_in), hbm, hbm, hbm],
            out_specs=pl.BlockSpec((tm,) + slab, rows),
            scratch_shapes=[pltpu.VMEM((2, D, Fh), F32), pltpu.VMEM((2, D, Fh), F32),
                            pltpu.VMEM((2, Fh, D), F32),
                            pltpu.VMEM((D, Fh), BF16), pltpu.VMEM((D, Fh), BF16),
                            pltpu.VMEM((Fh, D), BF16),
                            pltpu.SMEM((1,), I32), pltpu.SemaphoreType.DMA((2,))]),
        out_shape=jax.ShapeDtypeStruct(xs.shape, xs.dtype),
        compiler_params=_cparams(("arbitrary",)),
        name="experts",
    )(tile_expert, tile_next, n_valid, xs, wg, wu, wd)


def _combine_kernel(tcnt_ref, tstart_ref, lrow_ref, x1_ref, wt_ref, p_ref, gp_ref, wpg_ref, wpp_ref,
                    gf_ref, ys_ref, o_ref, lbuf_a, lbuf_b, wpg_b, wpp_b, sem):
    i = pl.program_id(0)
    n = pl.num_programs(0)
    tm = x1_ref.shape[0]
    nr = 2 * tm

    def request(tile, live, buf, s):
        lo = 0
        for e in range(N_EXPERTS):
            cnt = jnp.where(live, tcnt_ref[tile * N_EXPERTS + e], 0)
            src = tstart_ref[tile * N_EXPERTS + e]
            for bit in _run_bits(tm):
                off = cnt & ~(2 * bit - 1)

                @pl.when((cnt & bit) != 0)
                def _():
                    pltpu.make_async_copy(ys_ref.at[pl.ds(src + off, bit)],
                                          buf.at[pl.ds(lo + off, bit)], sem.at[s]).start()
            lo = lo + cnt

    @pl.when(i == 0)
    def _():
        request(0, True, lbuf_a, 0)
        wpg_b[...] = wpg_ref[...].astype(BF16)
        wpp_b[...] = wpp_ref[...].astype(BF16)

    def step(cur, cur_s, nxt, nxt_s):
        pltpu.make_async_copy(ys_ref.at[pl.ds(0, nr)], cur, sem.at[cur_s]).wait()
        request(jnp.minimum(i + 1, n - 1), i + 1 < n, nxt, nxt_s)
        rows = cur[...].reshape(nr, -1)
        sub = tm // COMBINE_SPLIT
        r = lax.broadcasted_iota(I32, (sub, nr), 1)
        for h in range(COMBINE_SPLIT):
            tok = slice(h * sub, (h + 1) * sub)
            local_row = lrow_ref[tok, :]
            y0 = jnp.dot((r == local_row[:, 0:1]).astype(BF16), rows, preferred_element_type=F32)
            y1 = jnp.dot((r == local_row[:, 1:2]).astype(BF16), rows, preferred_element_type=F32)
            wt = wt_ref[tok, :]
            x2 = x1_ref[tok, :] + wt[:, 0:1] * y0 + wt[:, 1:2] * y1
            hp = _rms(x2, gp_ref[...]).astype(BF16)
            gate = jax.nn.sigmoid(jnp.dot(hp, wpg_b[...], preferred_element_type=F32))
            proj = jnp.dot(p_ref[tok, :].astype(BF16), wpp_b[...], preferred_element_type=F32)
            x3 = x2 + gate * proj
            o_ref[tok, :] = _rms(x3, gf_ref[...])

    pl.when(i % 2 == 0)(functools.partial(step, lbuf_a, 0, lbuf_b, 1))
    pl.when(i % 2 == 1)(functools.partial(step, lbuf_b, 1, lbuf_a, 0))


def _combine(tile_cnt, tile_start, lrow, x1, wts, p2, g_ple, w_ple_gate, w_ple_proj, g_final, ys):
    T, D = x1.shape
    P = p2.shape[1]
    tm = TM_COMBINE
    n = T // tm
    row = lambda i: (i, 0)
    const = lambda i: (0, 0)
    smem = pl.BlockSpec(memory_space=pltpu.SMEM)
    return pl.pallas_call(
        _combine_kernel,
        grid=(n,),
        in_specs=[smem, smem, pl.BlockSpec((tm, LANES), row),
                  pl.BlockSpec((tm, D), row), pl.BlockSpec((tm, LANES), row),
                  pl.BlockSpec((tm, P), row), pl.BlockSpec((1, D), const),
                  pl.BlockSpec((D, D), const, **_FETCH_ONCE), pl.BlockSpec((P, D), const, **_FETCH_ONCE),
                  pl.BlockSpec((1, D), const), pl.BlockSpec(memory_space=pl.ANY)],
        out_specs=pl.BlockSpec((tm, D), row),
        out_shape=jax.ShapeDtypeStruct((T, D), F32),
        scratch_shapes=[pltpu.VMEM((2 * tm,) + ys.shape[1:], ys.dtype),
                        pltpu.VMEM((2 * tm,) + ys.shape[1:], ys.dtype),
                        pltpu.VMEM((D, D), BF16), pltpu.VMEM((P, D), BF16),
                        pltpu.SemaphoreType.DMA((2,))],
        compiler_params=_cparams(("arbitrary",)),
        name="combine",
    )(tile_cnt, tile_start, lrow, x1, wts, p2, g_ple, w_ple_gate, w_ple_proj, g_final, ys)


def _block_diag(w):
    nb, d, _ = w.shape
    eye = jnp.eye(nb, dtype=w.dtype)
    return (eye[:, None, :, None] * w[:, :, None, :]).reshape(nb * d, nb * d)


def kernel(x, p, positions, g_mix, w_in, sinks, conv_w, conv_b, lru_wa, lru_ba, lru_wx, lru_bx,
           lru_lambda, g_attn_out, g_lru_out, w_out, g_ffn, w_router_group, b_router_group,
           w_router_expert, b_router_expert, w_expert_gate, w_expert_up, w_expert_down, g_ple,
           w_ple_gate, w_ple_proj, g_final):
    B, S, D = x.shape
    assert w_in.shape[0] == 1, "single-layer block only"
    T = B * S
    lru_w = conv_w.shape[-1]
    n_rows = 2 * T + N_EXPERTS * TM_EXPERT
    n_tiles = n_rows // TM_EXPERT
    pos3 = positions.reshape(T // TM_INPROJ, TM_INPROJ // LANES, LANES).astype(I32)
    x2 = x.reshape(T, D)
    q, k, v, lx, lg = _inproj(x2, pos3, g_mix[0][None], w_in[0], lru_w)
    mix_a = _attention(q, k, v, sinks[0], g_attn_out[0][None], B, S)
    mix_b = _lru(lx.reshape(B, S, lru_w), lg.reshape(B, S, lru_w), conv_w[0], conv_b[0][None],
                 _block_diag(lru_wa[0]).astype(BF16), lru_ba[0][None],
                 _block_diag(lru_wx[0]).astype(BF16), lru_bx[0][None],
                 lru_lambda[0][None], g_lru_out[0][None]).reshape(T, lru_w)
    n_router = N_GROUPS + N_EXPERTS
    w_router = jnp.pad(jnp.concatenate([w_router_group[0], w_router_expert[0]], axis=1),
                       ((0, 0), (0, LANES - n_router)))
    b_router = jnp.pad(jnp.concatenate([b_router_group[0], b_router_expert[0]]),
                       (0, LANES - n_router))[None]
    w_router_hi = w_router.astype(BF16)
    w_router_lo = (w_router - w_router_hi.astype(F32)).astype(BF16)
    x1, h2, eid, wts = _outproj(x2, mix_a, mix_b, w_out[0], g_ffn[0][None],
                                jnp.concatenate([w_router_hi, w_router_lo], axis=1), b_router)
    lrow_d, lrow_c, tile_expert, n_valid, meta, tile_cnt, tile_start = _plan(eid, n_tiles)
    n_valid = n_valid[0, :1]
    group = TM_DISPATCH // TM_COMBINE
    tile_cnt = tile_cnt[:, :N_EXPERTS]
    tile_start = tile_start[:, :N_EXPERTS]
    dtile_cnt = tile_cnt.reshape(-1, group, N_EXPERTS).sum(axis=1)
    dtile_start = tile_start.reshape(-1, group, N_EXPERTS)[:, 0]
    xs = _dispatch(dtile_cnt.reshape(-1), dtile_start.reshape(-1), meta[0], meta[1], n_valid,
                   lrow_d, h2, n_rows)
    Fh = w_expert_gate.shape[-1]
    ys = _experts(tile_expert[:, 0], tile_expert[:, 1], n_valid, xs,
                  w_expert_gate[0].reshape(N_EXPERTS, D, Fh),
                  w_expert_up[0].reshape(N_EXPERTS, D, Fh),
                  w_expert_down[0].reshape(N_EXPERTS, Fh, D))
    out = _combine(tile_cnt.reshape(-1), tile_start.reshape(-1), lrow_c, x1, wts,
                   p[0].reshape(T, -1), g_ple[0][None], w_ple_gate[0], w_ple_proj[0], g_final[None], ys)
    return out.reshape(B, S, D)
```

```python
import functools
import math

import jax
import jax.numpy as jnp
from jax import lax
from jax.experimental import pallas as pl
from jax.experimental.pallas import tpu as pltpu

F32 = jnp.float32
BF16 = jnp.bfloat16
I32 = jnp.int32

EPS = 1e-6
N_HEADS = 8
N_KV_HEADS = 2
HEAD_DIM = 64
WINDOW = 128
ROPE_THETA = 10000.0
CONV_WIDTH = 4
LRU_C = 8.0
N_GROUPS = 4
EXPERTS_PER_GROUP = 8
N_EXPERTS = N_GROUPS * EXPERTS_PER_GROUP
NEG_INF = -1e30
LANES = 128
SUBLANES = 8

TM_INPROJ = 512
TM_PROJ = 512
TQ_ATTN = 512
TL_LRU = 256
TM_PLAN = 1024
TM_EXPERT = 512
EXPERT_FF_BLOCK = 256
TM_DISPATCH = 512
COMBINE_TILES = 1
TM_COMBINE = 256
VMEM_LIMIT = 56 * 1024 * 1024


def _rms(x, g):
    ms = jnp.mean(x * x, axis=-1, keepdims=True)
    return x * lax.rsqrt(ms + EPS) * g


_FETCH_ONCE = dict(pipeline_mode=pl.Buffered(1))


def _cparams(sem):
    return pltpu.CompilerParams(dimension_semantics=sem, vmem_limit_bytes=VMEM_LIMIT)


def _inproj_kernel(x_ref, pos_ref, g_ref, w_ref, q_ref, k_ref, v_ref, lx_ref, lg_ref, w_b):
    @pl.when(pl.program_id(0) == 0)
    def _():
        w_b[...] = w_ref[...].astype(BF16)

    h = _rms(x_ref[...], g_ref[...]).astype(BF16)
    proj = jnp.dot(h, w_b[...], preferred_element_type=F32)
    lane = lax.broadcasted_iota(I32, (1, LANES), 1)
    n_freq = HEAD_DIM // 2
    fidx = lax.broadcasted_iota(I32, (n_freq, 1), 0).astype(F32)
    inv_freq = jnp.exp(fidx * (-2.0 / HEAD_DIM * math.log(ROPE_THETA)))
    pos = pos_ref[0].astype(F32)
    cos_blocks, sin_blocks = [], []
    for c in range(pos.shape[0]):
        ang_t = inv_freq * pos[c:c + 1, :]
        cos_blocks.append(jnp.concatenate([jnp.cos(ang_t)] * (LANES // n_freq), axis=0).T)
        sin_blocks.append(jnp.concatenate([jnp.sin(ang_t)] * (LANES // n_freq), axis=0).T)
    cos = jnp.concatenate(cos_blocks, axis=0)
    sin = jnp.concatenate(sin_blocks, axis=0)
    first_half = (lane % HEAD_DIM) < (HEAD_DIM // 2)
    sin_signed = jnp.where(first_half, -sin, sin)

    def rope(t):
        partner = jnp.where(first_half, pltpu.roll(t, LANES - HEAD_DIM // 2, 1),
                            pltpu.roll(t, HEAD_DIM // 2, 1))
        return t * cos + partner * sin_signed

    attn_w = N_HEADS * HEAD_DIM
    scale = HEAD_DIM ** -0.5
    assert math.frexp(scale)[0] == 0.5, "score scale must be a power of two to fold into q exactly"
    for c in range(attn_w // LANES):
        q_ref[:, c * LANES:(c + 1) * LANES] = (
            rope(proj[:, c * LANES:(c + 1) * LANES]) * scale).astype(BF16)
    kv_w = N_KV_HEADS * HEAD_DIM
    assert kv_w == LANES
    k = rope(proj[:, attn_w:attn_w + kv_w])
    v = proj[:, attn_w + kv_w:attn_w + 2 * kv_w]
    k_ref[:, :kv_w] = k.astype(BF16)
    k_ref[:, kv_w:] = pltpu.roll(k, HEAD_DIM, 1).astype(BF16)
    v_ref[:, :kv_w] = v.astype(BF16)
    v_ref[:, kv_w:] = pltpu.roll(v, HEAD_DIM, 1).astype(BF16)
    off_lx = attn_w + 2 * kv_w
    lru_w = lx_ref.shape[1]
    lx_ref[...] = proj[:, off_lx:off_lx + lru_w]
    lg_ref[...] = proj[:, off_lx + lru_w:off_lx + 2 * lru_w]


def _inproj(x2, pos3, g_mix, w_in, lru_w):
    T, D = x2.shape
    tm = TM_INPROJ
    attn_w = N_HEADS * HEAD_DIM
    kv_w = N_KV_HEADS * HEAD_DIM
    row = lambda i: (i, 0)
    const = lambda i: (0, 0)
    return pl.pallas_call(
        _inproj_kernel,
        grid=(T // tm,),
        in_specs=[pl.BlockSpec((tm, D), row),
                  pl.BlockSpec((1, tm // LANES, LANES), lambda i: (i, 0, 0)),
                  pl.BlockSpec((1, D), const), pl.BlockSpec(w_in.shape, const, **_FETCH_ONCE)],
        out_specs=[pl.BlockSpec((tm, attn_w), row), pl.BlockSpec((tm, 2 * kv_w), row),
                   pl.BlockSpec((tm, 2 * kv_w), row), pl.BlockSpec((tm, lru_w), row),
                   pl.BlockSpec((tm, lru_w), row)],
        out_shape=[jax.ShapeDtypeStruct((T, attn_w), BF16),
                   jax.ShapeDtypeStruct((T, 2 * kv_w), BF16),
                   jax.ShapeDtypeStruct((T, 2 * kv_w), BF16), jax.ShapeDtypeStruct((T, lru_w), F32),
                   jax.ShapeDtypeStruct((T, lru_w), F32)],
        scratch_shapes=[pltpu.VMEM(w_in.shape, BF16)],
        compiler_params=_cparams(("arbitrary",)),
        name="inproj",
    )(x2, pos3, g_mix, w_in)


def _attn_kernel(sinks_ref, q_ref, k_ref, v_ref, kp_ref, vp_ref, g_ref, o_ref):
    n = pl.program_id(1)
    blk = WINDOW
    hd = HEAD_DIM
    nqb = q_ref.shape[0] // blk
    n_pairs = N_HEADS // 2
    pairs_per_group = n_pairs // N_KV_HEADS
    lane = lax.broadcasted_iota(I32, (1, LANES), 1)
    first = lane < hd
    qi = lax.broadcasted_iota(I32, (blk, 2 * blk), 0)
    kj = lax.broadcasted_iota(I32, (blk, 2 * blk), 1)
    rel = qi + blk - kj
    band = (rel >= 0) & (rel < WINDOW)
    bd_row = lax.broadcasted_iota(I32, (4 * blk, LANES), 0)
    bd_lane = lax.broadcasted_iota(I32, (4 * blk, LANES), 1)
    ones_bd = ((bd_row < 2 * blk) == (bd_lane < hd)).astype(BF16)
    zero = jnp.zeros((), BF16)

    def block_diag(t):
        same, swapped = t[:, :LANES], t[:, LANES:]
        out = []
        for g in range(N_KV_HEADS):
            top = jnp.where(first, same if g == 0 else swapped, zero)
            bot = jnp.where(first, zero, swapped if g == 0 else same)
            out.append(jnp.concatenate([top, bot], axis=0))
        return out

    for j in range(nqb):
        rows = slice(j * blk, (j + 1) * blk)
        if j == 0:
            k_prev, v_prev = kp_ref[...], vp_ref[...]
            mask = band & ((kj >= blk) | (n > 0))
        else:
            prev_rows = slice((j - 1) * blk, j * blk)
            k_prev, v_prev = k_ref[prev_rows, :], v_ref[prev_rows, :]
            mask = band
        k_bd = block_diag(jnp.concatenate([k_prev, k_ref[rows, :]], axis=0))
        v_bd = block_diag(jnp.concatenate([v_prev, v_ref[rows, :]], axis=0))
        outs = []
        for p in range(n_pairs):
            g = p // pairs_per_group
            qp = q_ref[rows, p * LANES:(p + 1) * LANES]
            s = lax.dot_general(qp, k_bd[g], (((1,), (1,)), ((), ())), preferred_element_type=F32)
            es, ms = [], []
            for a in range(2):
                sa = jnp.where(mask, s[:, a * 2 * blk:(a + 1) * 2 * blk], NEG_INF)
                m = jnp.maximum(jnp.max(sa, axis=-1, keepdims=True), sinks_ref[2 * p + a])
                es.append(jnp.exp(sa - m))
                ms.append(m)
            e = jnp.concatenate(es, axis=1).astype(BF16)
            od = jnp.dot(e, jnp.concatenate([v_bd[g], ones_bd], axis=1), preferred_element_type=F32)
            sink_term = jnp.where(first, jnp.exp(sinks_ref[2 * p] - ms[0]),
                                  jnp.exp(sinks_ref[2 * p + 1] - ms[1]))
            outs.append(od[:, :LANES] / (od[:, LANES:] + sink_term))
        attn = jnp.concatenate(outs, axis=1)
        o_ref[rows, :] = _rms(attn, g_ref[...]).astype(BF16)


def _attention(q, k, v, sinks, g_attn, B, S):
    T, attn_w = q.shape
    kv_w = k.shape[1]
    blk = WINDOW
    tq = TQ_ATTN
    nt = S // tq
    cur = lambda b, n: (b * nt + n, 0)
    prev = lambda b, n: (b * (S // blk) + jnp.maximum(n * (tq // blk) - 1, 0), 0)
    return pl.pallas_call(
        _attn_kernel,
        grid=(B, nt),
        in_specs=[pl.BlockSpec(memory_space=pltpu.SMEM),
                  pl.BlockSpec((tq, attn_w), cur),
                  pl.BlockSpec((tq, kv_w), cur), pl.BlockSpec((tq, kv_w), cur),
                  pl.BlockSpec((blk, kv_w), prev), pl.BlockSpec((blk, kv_w), prev),
                  pl.BlockSpec((1, attn_w), lambda b, n: (0, 0))],
        out_specs=pl.BlockSpec((tq, attn_w), cur),
        out_shape=jax.ShapeDtypeStruct((T, attn_w), BF16),
        compiler_params=_cparams(("arbitrary", "arbitrary")),
        name="attn",
    )(sinks, q, k, v, k, v, g_attn)


def _neg_expm1(x, exp_x):
    series = -x * (1.0 + x * (0.5 + x * (1.0 / 6.0 + x * (1.0 / 24.0))))
    return jnp.where(x > -0.02, series, 1.0 - exp_x)


def _gelu_tanh(x):
    c = math.sqrt(2.0 / math.pi)
    return x * (0.5 + 0.5 * jnp.tanh(x * (c + (c * 0.044715) * (x * x))))


def _lru_kernel(lx_ref, lg_ref, cw_ref, cb_ref, wa_ref, ba_ref, wx_ref, bx_ref, lam_ref, g_ref,
                o_ref, xpad, a_scr, b_scr, h_scr, h_carry):
    i = pl.program_id(0)
    B, tl, C = lx_ref.shape
    pad = SUBLANES

    @pl.when(i == 0)
    def _():
        xpad[:, 0:pad, :] = jnp.zeros((B, pad, C), F32)
        h_carry[...] = jnp.zeros_like(h_carry)

    @pl.when(i > 0)
    def _():
        xpad[:, 0:pad, :] = xpad[:, tl:tl + pad, :]

    xpad[:, pad:, :] = lx_ref[...]
    groups = (B * tl // SUBLANES, SUBLANES, C)
    sub = lax.broadcasted_iota(I32, groups, 1)
    x_now = lx_ref[...].reshape(groups)
    x_before = xpad[:, pl.ds(0, tl), :].reshape(groups)
    y = x_now * cw_ref[CONV_WIDTH - 1:CONV_WIDTH, :] + cb_ref[...]
    for s in range(1, CONV_WIDTH):
        shifted = pltpu.roll(jnp.where(sub >= SUBLANES - s, x_before, x_now), s, 1)
        y = y + shifted * cw_ref[CONV_WIDTH - 1 - s:CONV_WIDTH - s, :]
    y2 = y.reshape(B * tl, C)
    yb = y2.astype(BF16)
    gate_a = jnp.dot(yb, wa_ref[...], preferred_element_type=F32) + ba_ref[...]
    gate_x = jnp.dot(yb, wx_ref[...], preferred_element_type=F32) + bx_ref[...]
    r = jax.nn.sigmoid(gate_a)
    ig = jax.nn.sigmoid(gate_x)
    nl = -lam_ref[...]
    softplus = jnp.maximum(nl, 0.0) + jnp.log1p(jnp.exp(-jnp.abs(nl)))
    log_a = (-LRU_C) * r * softplus
    a_all = jnp.exp(log_a)
    one_minus_a2 = _neg_expm1(2.0 * log_a, a_all * a_all)
    b_all = one_minus_a2 * lax.rsqrt(jnp.maximum(one_minus_a2, 1e-30)) * ig * y2
    groups = (B * tl // SUBLANES, SUBLANES, C)
    sub = lax.broadcasted_iota(I32, groups, 1)
    a_cum, b_loc = a_all.reshape(groups), b_all.reshape(groups)
    d = 1
    while d < SUBLANES:
        keep = sub >= d
        a_prev = pltpu.roll(a_cum, d, 1)
        b_prev = pltpu.roll(b_loc, d, 1)
        b_loc = b_loc + jnp.where(keep, a_cum * b_prev, 0.0)
        a_cum = jnp.where(keep, a_cum * a_prev, a_cum)
        d *= 2
    a_scr[...] = a_cum.reshape(B, tl, C)
    b_scr[...] = b_loc.reshape(B, tl, C)

    def carry(g, h_prev):
        rows = pl.ds(pl.multiple_of(g * SUBLANES, SUBLANES), SUBLANES)
        h = b_scr[:, rows, :] + a_scr[:, rows, :] * h_prev
        h_scr[:, rows, :] = h
        return h[:, SUBLANES - 1:SUBLANES, :]

    h_carry[...] = lax.fori_loop(0, tl // SUBLANES, carry, h_carry[...], unroll=4)
    lru = h_scr[...].reshape(B * tl, C) * _gelu_tanh(lg_ref[...].reshape(B * tl, C))
    o_ref[...] = _rms(lru, g_ref[...]).astype(BF16).reshape(B, tl, C)


def _lru(lx3, lg3, conv_w, conv_b, wa_bd, ba, wx_bd, bx, lam, g_lru):
    B, S, C = lx3.shape
    tl = TL_LRU
    blk = lambda i: (0, i, 0)
    const = lambda i: (0, 0)
    vec = pl.BlockSpec((1, C), const)
    return pl.pallas_call(
        _lru_kernel,
        grid=(S // tl,),
        in_specs=[pl.BlockSpec((B, tl, C), blk), pl.BlockSpec((B, tl, C), blk),
                  pl.BlockSpec((CONV_WIDTH, C), const), vec,
                  pl.BlockSpec((C, C), const), vec, pl.BlockSpec((C, C), const), vec, vec, vec],
        out_specs=pl.BlockSpec((B, tl, C), blk),
        out_shape=jax.ShapeDtypeStruct((B, S, C), BF16),
        scratch_shapes=[pltpu.VMEM((B, tl + SUBLANES, C), F32),
                        pltpu.VMEM((B, tl, C), F32), pltpu.VMEM((B, tl, C), F32),
                        pltpu.VMEM((B, tl, C), F32), pltpu.VMEM((B, 1, C), F32)],
        compiler_params=_cparams(("arbitrary",)),
        name="lru",
    )(lx3, lg3, conv_w, conv_b, wa_bd, ba, wx_bd, bx, lam, g_lru)


def _outproj_kernel(x_ref, ma_ref, mb_ref, wo_ref, g_ref, wr_ref, br_ref,
                    x1_ref, h2_ref, eid_ref, wt_ref, wo_b):
    @pl.when(pl.program_id(0) == 0)
    def _():
        wo_b[...] = wo_ref[...].astype(BF16)

    half = ma_ref.shape[1]
    x1 = (x_ref[...]
          + jnp.dot(ma_ref[...], wo_b[0:half, :], preferred_element_type=F32)
          + jnp.dot(mb_ref[...], wo_b[half:, :], preferred_element_type=F32))
    x1_ref[...] = x1
    h2 = _rms(x1, g_ref[...])
    h2_ref[...] = h2.astype(BF16).reshape(h2_ref.shape)
    h_hi = h2.astype(BF16)
    h_lo = (h2 - h_hi.astype(F32)).astype(BF16)
    hw = jnp.dot(h_hi, wr_ref[...], preferred_element_type=F32)
    logits = (hw[:, :LANES] + hw[:, LANES:]
              + jnp.dot(h_lo, wr_ref[:, :LANES], preferred_element_type=F32) + br_ref[...])
    tm = logits.shape[0]
    lane = lax.broadcasted_iota(I32, (tm, LANES), 1)
    lane_f = lane.astype(F32)
    big = float(LANES)

    def first_argmax(vals):
        m = jnp.max(vals, axis=-1, keepdims=True)
        idx = jnp.min(jnp.where(vals == m, lane_f, big), axis=-1, keepdims=True)
        return m, idx.astype(I32)

    gl = jnp.where(lane < N_GROUPS, logits, NEG_INF)
    gmax, gidx = first_argmax(gl)
    gsum = jnp.sum(jnp.where(lane < N_GROUPS, jnp.exp(gl - gmax), 0.0), axis=-1, keepdims=True)
    g_top_p = 1.0 / gsum
    lo = N_GROUPS + EXPERTS_PER_GROUP * gidx
    el = jnp.where((lane >= lo) & (lane < lo + EXPERTS_PER_GROUP), logits, NEG_INF)
    m1, i1 = first_argmax(el)
    el2 = jnp.where(lane == i1, NEG_INF, el)
    m2, i2 = first_argmax(el2)
    ratio = jnp.exp(m2 - m1)
    w1 = g_top_p / (1.0 + ratio)
    w2 = g_top_p * ratio / (1.0 + ratio)
    eid_ref[...] = jnp.where(lane == 0, i1 - N_GROUPS, jnp.where(lane == 1, i2 - N_GROUPS, 0))
    wt_ref[...] = jnp.where(lane == 0, w1, jnp.where(lane == 1, w2, 0.0))


def _outproj(x2, mix_a, mix_b, w_out, g_ffn, w_router, b_router):
    T, D = x2.shape
    tm = TM_PROJ
    half = mix_a.shape[1]
    row = lambda i: (i, 0)
    const = lambda i: (0, 0)
    return pl.pallas_call(
        _outproj_kernel,
        grid=(T // tm,),
        in_specs=[pl.BlockSpec((tm, D), row), pl.BlockSpec((tm, half), row),
                  pl.BlockSpec((tm, half), row), pl.BlockSpec(w_out.shape, const, **_FETCH_ONCE),
                  pl.BlockSpec((1, D), const), pl.BlockSpec((D, 2 * LANES), const),
                  pl.BlockSpec((1, LANES), const)],
        out_specs=[pl.BlockSpec((tm, D), row),
                   pl.BlockSpec((tm, D // LANES, LANES), lambda i: (i, 0, 0)),
                   pl.BlockSpec((tm, LANES), row), pl.BlockSpec((tm, LANES), row)],
        out_shape=[jax.ShapeDtypeStruct((T, D), F32),
                   jax.ShapeDtypeStruct((T, D // LANES, LANES), BF16),
                   jax.ShapeDtypeStruct((T, LANES), I32), jax.ShapeDtypeStruct((T, LANES), F32)],
        scratch_shapes=[pltpu.VMEM(w_out.shape, BF16)],
        compiler_params=_cparams(("arbitrary",)),
        name="outproj",
    )(x2, mix_a, mix_b, w_out, g_ffn, w_router, b_router)


def _lane_cumsum(x):
    lane = lax.broadcasted_iota(I32, x.shape, 1)
    shift = 1
    while shift < LANES:
        x = x + jnp.where(lane >= shift, pltpu.roll(x, shift, 1), 0)
        shift *= 2
    return x


def _plan_kernel(eid_ref, lrow_d_ref, lrow_c_ref, te_ref, nv_ref, meta_ref, tcnt_ref, tstart_ref,
                 cnt_scr, tcnt_scr, tpre_scr):
    i = pl.program_id(0)
    tm = eid_ref.shape[0]
    tc = TM_COMBINE
    group = TM_DISPATCH // TM_COMBINE
    nsub = tm // tc
    lane = lax.broadcasted_iota(I32, (tm, LANES), 1)
    eid = eid_ref[...]
    sel1 = lane == eid[:, 0:1]
    sel2 = lane == eid[:, 1:2]

    @pl.when(i == 0)
    def _():
        cnt_scr[...] = jnp.zeros_like(cnt_scr)

    onehot = (sel1 | sel2).astype(BF16)
    r = lax.broadcasted_iota(I32, (tm, tm), 0)
    c = lax.broadcasted_iota(I32, (tm, tm), 1)
    strict_lower = (r > c).astype(BF16)
    within = jnp.dot(strict_lower, onehot, preferred_element_type=F32)
    base = cnt_scr[0:1, :]
    onehot_f = onehot.astype(F32)
    row = lax.broadcasted_iota(I32, (tm, LANES), 0)
    srow = lax.broadcasted_iota(I32, (2 * SUBLANES, LANES), 0)
    counts = [jnp.sum(onehot_f[s * tc:(s + 1) * tc], axis=0, keepdims=True) for s in range(nsub)]
    cnt_mat = jnp.zeros((2 * SUBLANES, LANES), F32)
    seen = []
    total = jnp.zeros((1, LANES), F32)
    for s in range(nsub):
        seen.append(total)
        tile = i * nsub + s
        tcnt_scr[pl.ds(tile, 1), :] = counts[s]
        tpre_scr[pl.ds(tile, 1), :] = base + total
        cnt_mat = jnp.where(srow == s, counts[s], cnt_mat)
        cnt_mat = jnp.where(srow == SUBLANES + s // group, cnt_mat + counts[s], cnt_mat)
        total = total + counts[s]
    cnt_i = cnt_mat.astype(I32)
    run_start = (_lane_cumsum(cnt_i) - cnt_i).astype(F32)

    def local_rows(first_tile_of, table_row_of):
        local = None
        for s in range(nsub):
            val = within - seen[first_tile_of(s)] + run_start[table_row_of(s):table_row_of(s) + 1, :]
            local = val if local is None else jnp.where(row >= s * tc, val, local)
        l1 = jnp.sum(jnp.where(sel1, local, 0.0), axis=-1, keepdims=True)
        l2 = jnp.sum(jnp.where(sel2, local, 0.0), axis=-1, keepdims=True)
        return jnp.where(lane == 0, l1, jnp.where(lane == 1, l2, 0.0)).astype(I32)

    lrow_c_ref[...] = local_rows(lambda s: s, lambda s: s)
    lrow_d_ref[...] = local_rows(lambda s: (s // group) * group, lambda s: SUBLANES + s // group)
    cnt_scr[...] = cnt_scr[...] + total

    @pl.when(i == pl.num_programs(0) - 1)
    def _():
        shift = int(math.log2(TM_EXPERT))
        cnt = cnt_scr[...].astype(I32)
        padded = ((cnt + (TM_EXPERT - 1)) >> shift) << shift
        ends = _lane_cumsum(padded)
        nt = te_ref.shape[0]
        end_tile = (ends >> shift)[0:1, :]
        tl_lane = lax.broadcasted_iota(I32, (nt, LANES), 1)
        tile = lax.broadcasted_iota(I32, (nt, LANES), 0)
        done = jnp.where((tl_lane < N_EXPERTS) & (end_tile <= tile), 1.0, 0.0)
        te = jnp.minimum(jnp.sum(done, axis=-1, keepdims=True).astype(I32), N_EXPERTS - 1)
        n_valid = jnp.sum(jnp.where(tl_lane[0:1, :] == N_EXPERTS - 1, end_tile, 0).astype(F32),
                          axis=-1, keepdims=True).astype(I32)
        nv_ref[...] = jnp.broadcast_to(n_valid, nv_ref.shape)
        seg_end = jnp.sum(jnp.where(tl_lane == te, end_tile, 0).astype(F32), axis=-1,
                          keepdims=True).astype(I32)
        nxt = jnp.sum(jnp.where((tl_lane < N_EXPERTS) & (end_tile <= seg_end), 1.0, 0.0), axis=-1,
                      keepdims=True).astype(I32)
        nxt = jnp.where(seg_end < n_valid, jnp.minimum(nxt, N_EXPERTS - 1), -1)
        te_ref[...] = jnp.where(tl_lane == 1, nxt, te)
        row = lax.broadcasted_iota(I32, cnt.shape, 0)
        meta_ref[...] = jnp.where(row == 0, ends - padded + cnt, jnp.where(row == 1, padded - cnt, 0))
        tcnt_ref[...] = tcnt_scr[...].astype(I32)
        tstart_ref[...] = (tpre_scr[...] + (ends - padded).astype(F32)[0:1, :]).astype(I32)


def _plan(eid, n_tiles):
    T = eid.shape[0]
    tm = TM_PLAN
    steps = T // tm
    n_ctiles = T // TM_COMBINE
    assert TM_DISPATCH % TM_COMBINE == 0 and tm % TM_DISPATCH == 0 and tm // TM_COMBINE <= SUBLANES
    row = lambda i: (i, 0)
    const = lambda i: (0, 0)
    return pl.pallas_call(
        _plan_kernel,
        grid=(steps,),
        in_specs=[pl.BlockSpec((tm, LANES), row)],
        out_specs=[pl.BlockSpec((tm, LANES), row), pl.BlockSpec((tm, LANES), row),
                   pl.BlockSpec((n_tiles, LANES), const), pl.BlockSpec((SUBLANES, LANES), const),
                   pl.BlockSpec((SUBLANES, LANES), const),
                   pl.BlockSpec((n_ctiles, LANES), const), pl.BlockSpec((n_ctiles, LANES), const)],
        out_shape=[jax.ShapeDtypeStruct((T, LANES), I32), jax.ShapeDtypeStruct((T, LANES), I32),
                   jax.ShapeDtypeStruct((n_tiles, LANES), I32),
                   jax.ShapeDtypeStruct((SUBLANES, LANES), I32),
                   jax.ShapeDtypeStruct((SUBLANES, LANES), I32),
                   jax.ShapeDtypeStruct((n_ctiles, LANES), I32),
                   jax.ShapeDtypeStruct((n_ctiles, LANES), I32)],
        scratch_shapes=[pltpu.VMEM((SUBLANES, LANES), F32),
                        pltpu.VMEM((n_ctiles, LANES), F32), pltpu.VMEM((n_ctiles, LANES), F32)],
        compiler_params=_cparams(("arbitrary",)),
        name="plan",
    )(eid)


_PAD_BITS = tuple(1 << b for b in reversed(range(int(math.log2(TM_EXPERT)))))


def _run_bits(max_rows):
    return tuple(1 << b for b in reversed(range(int(math.log2(max_rows)) + 1)))


def _dispatch_kernel(tcnt_ref, tstart_ref, meta_ref, nv_ref, lrow_ref, h_ref,
                     xs_ref, lbuf_a, lbuf_b, zeros, sem, zsem):
    i = pl.program_id(0)
    n = pl.num_programs(0)
    td = h_ref.shape[0]
    nr = 2 * td

    def wait_buf(buf, s):
        pltpu.make_async_copy(buf, xs_ref.at[pl.ds(0, nr)], sem.at[s]).wait()

    def step(cur, cur_s, prev, prev_s):
        @pl.when(i >= 2)
        def _():
            wait_buf(cur, cur_s)

        local_row = lrow_ref[...].astype(F32).T
        r = lax.broadcasted_iota(I32, (nr, td), 0).astype(F32)
        onehot = ((r == local_row[0:1, :]) | (r == local_row[1:2, :])).astype(BF16)
        sorted_rows = jnp.dot(onehot, h_ref[...].reshape(td, -1), preferred_element_type=F32)
        cur[...] = sorted_rows.astype(BF16).reshape(cur.shape)

        group = TM_DISPATCH // TM_COMBINE
        first = jnp.maximum(i - 1, 0) * group
        lo = 0
        for e in range(N_EXPERTS):
            cnt = sum(tcnt_ref[first + u, e] for u in range(group))
            cnt = jnp.where(i >= 1, cnt, 0)
            dst = tstart_ref[first, e]
            for bit in _run_bits(td):
                off = cnt & ~(2 * bit - 1)

                @pl.when((cnt & bit) != 0)
                def _():
                    pltpu.make_async_copy(prev.at[pl.ds(lo + off, bit)],
                                          xs_ref.at[pl.ds(dst + off, bit)], sem.at[prev_s]).start()
            lo = lo + cnt

        @pl.when(i == n - 1)
        def _():
            wait_buf(prev, prev_s)

    pl.when(i % 2 == 0)(functools.partial(step, lbuf_a, 0, lbuf_b, 1))
    pl.when(i % 2 == 1)(functools.partial(step, lbuf_b, 1, lbuf_a, 0))

    @pl.when(i == n - 1)
    def _():
        zeros[...] = jnp.zeros_like(zeros)
        n_tiles = xs_ref.shape[0] // TM_EXPERT
        first_tail = n_tiles - N_EXPERTS

        def pad_copy(e, bit):
            n = meta_ref[1, e]
            dst = meta_ref[0, e] + (n & ~(2 * bit - 1))
            return (n & bit) != 0, pltpu.make_async_copy(zeros.at[pl.ds(0, bit)],
                                                         xs_ref.at[pl.ds(dst, bit)], zsem)

        def tail_copy(j):
            return j >= nv_ref[0, 0], pltpu.make_async_copy(
                zeros, xs_ref.at[pl.ds(j * TM_EXPERT, TM_EXPERT)], zsem)

        def for_each_fill(act):
            def per_expert(e, c):
                for bit in _PAD_BITS:
                    pred, cp = pad_copy(e, bit)
                    pl.when(pred)(functools.partial(act, cp))
                return c

            lax.fori_loop(0, N_EXPERTS, per_expert, 0)

            def per_tile(j, c):
                pred, cp = tail_copy(j)
                pl.when(pred)(functools.partial(act, cp))
                return c

            lax.fori_loop(first_tail, n_tiles, per_tile, 0)

        for_each_fill(lambda cp: cp.start())
        for_each_fill(lambda cp: cp.wait())


def _dispatch(tile_cnt, tile_start, meta, n_valid, lrow, h2, n_rows):
    T = h2.shape[0]
    slab = h2.shape[1:]
    td = TM_DISPATCH
    nt = T // td
    smem = pl.BlockSpec(memory_space=pltpu.SMEM)
    return pl.pallas_call(
        _dispatch_kernel,
        grid=(nt + 1,),
        in_specs=[smem, smem, smem, smem,
                  pl.BlockSpec((td, LANES), lambda i: (jnp.minimum(i, nt - 1), 0)),
                  pl.BlockSpec((td,) + slab, lambda i: (jnp.minimum(i, nt - 1), 0, 0))],
        out_specs=pl.BlockSpec(memory_space=pl.ANY),
        out_shape=jax.ShapeDtypeStruct((n_rows,) + slab, h2.dtype),
        scratch_shapes=[pltpu.VMEM((2 * td,) + slab, h2.dtype),
                        pltpu.VMEM((2 * td,) + slab, h2.dtype),
                        pltpu.VMEM((TM_EXPERT,) + slab, h2.dtype),
                        pltpu.SemaphoreType.DMA((2,)), pltpu.SemaphoreType.DMA(())],
        compiler_params=_cparams(("arbitrary",)),
        name="dispatch",
    )(tile_cnt, tile_start, meta, n_valid, lrow, h2)


def _experts_kernel(te_ref, nv_ref, xs_ref, wg_ref, wu_ref, wd_ref, ys_ref,
                    wg_f, wu_f, wd_f, wg_b, wu_b, wd_b, slot_ref, sem):
    j = pl.program_id(0)
    valid = j < nv_ref[0, 0]
    new_expert = valid & ((j == 0) | (te_ref[j, 0] != te_ref[jnp.maximum(j - 1, 0), 0]))

    def fetch(e, s):
        return (pltpu.make_async_copy(wg_ref.at[e], wg_f.at[s], sem.at[s]),
                pltpu.make_async_copy(wu_ref.at[e], wu_f.at[s], sem.at[s]),
                pltpu.make_async_copy(wd_ref.at[e], wd_f.at[s], sem.at[s]))

    @pl.when(j == 0)
    def _():
        slot_ref[0] = 0
        for cp in fetch(te_ref[0, 0], 0):
            cp.start()

    @pl.when(new_expert)
    def _():
        s = slot_ref[0]
        for cp in fetch(te_ref[j, 0], s):
            cp.wait()
        nxt = te_ref[j, 1]

        @pl.when(nxt >= 0)
        def _():
            for cp in fetch(nxt, 1 - s):
                cp.start()

        wg_b[...] = wg_f[s].astype(BF16)
        wu_b[...] = wu_f[s].astype(BF16)
        wd_b[...] = wd_f[s].astype(BF16)
        slot_ref[0] = 1 - s

    @pl.when(valid)
    def _():
        x = xs_ref[...].reshape(xs_ref.shape[0], -1)
        y = None
        for lo in range(0, wg_b.shape[1], EXPERT_FF_BLOCK):
            cols = slice(lo, lo + EXPERT_FF_BLOCK)
            a = jnp.dot(x, wg_b[:, cols], preferred_element_type=F32)
            u = jnp.dot(x, wu_b[:, cols], preferred_element_type=F32)
            hid = (a * jax.nn.sigmoid(a) * u).astype(BF16)
            part = jnp.dot(hid, wd_b[cols, :], preferred_element_type=F32)
            y = part if y is None else y + part
        ys_ref[...] = y.astype(BF16).reshape(ys_ref.shape)

    @pl.when(j >= nv_ref[0, 0])
    def _():
        ys_ref[...] = jnp.zeros_like(ys_ref)


def _experts(tile_table, n_valid, xs, wg, wu, wd):
    R = xs.shape[0]
    slab = xs.shape[1:]
    E, D, Fh = wg.shape
    tm = TM_EXPERT
    n_tiles = R // tm
    rows = lambda j, te, nv: (j, 0, 0)
    rows_in = lambda j, te, nv: (jnp.minimum(j, nv[0, 0] - 1), 0, 0)
    hbm = pl.BlockSpec(memory_space=pl.ANY)
    return pl.pallas_call(
        _experts_kernel,
        grid_spec=pltpu.PrefetchScalarGridSpec(
            num_scalar_prefetch=2,
            grid=(n_tiles,),
            in_specs=[pl.BlockSpec((tm,) + slab, rows_in), hbm, hbm, hbm],
            out_specs=pl.BlockSpec((tm,) + slab, rows),
            scratch_shapes=[pltpu.VMEM((2, D, Fh), F32), pltpu.VMEM((2, D, Fh), F32),
                            pltpu.VMEM((2, Fh, D), F32),
                            pltpu.VMEM((D, Fh), BF16), pltpu.VMEM((D, Fh), BF16),
                            pltpu.VMEM((Fh, D), BF16),
                            pltpu.SMEM((1,), I32), pltpu.SemaphoreType.DMA((2,))]),
        out_shape=jax.ShapeDtypeStruct(xs.shape, xs.dtype),
        compiler_params=_cparams(("arbitrary",)),
        name="experts",
    )(tile_table, n_valid, xs, wg, wu, wd)


def _combine_kernel(tcnt_ref, tstart_ref, lrow_ref, x1_ref, wt_ref, p_ref, gp_ref, wpg_ref, wpp_ref,
                    gf_ref, ys_ref, o_ref, *scratch):
    g = COMBINE_TILES
    bufs = (scratch[0:g], scratch[g:2 * g])
    wpg_b, wpp_b, sem = scratch[2 * g:]
    i = pl.program_id(0)
    n = pl.num_programs(0)
    tc = x1_ref.shape[0] // g
    nr = 2 * tc

    def request(tile, live, buf, s):
        lo = 0
        for e in range(N_EXPERTS):
            cnt = jnp.where(live, tcnt_ref[tile, e], 0)
            src = tstart_ref[tile, e]
            for bit in _run_bits(tc):
                off = cnt & ~(2 * bit - 1)

                @pl.when((cnt & bit) != 0)
                def _():
                    pltpu.make_async_copy(ys_ref.at[pl.ds(src + off, bit)],
                                          buf.at[pl.ds(lo + off, bit)], s).start()
            lo = lo + cnt

    @pl.when(i == 0)
    def _():
        for u in range(g):
            request(u, True, bufs[0][u], sem.at[0, u])
        wpg_b[...] = wpg_ref[...].astype(BF16)
        wpp_b[...] = wpp_ref[...].astype(BF16)

    def step(par):
        cur, nxt = bufs[par], bufs[1 - par]
        for u in range(g):
            pltpu.make_async_copy(ys_ref.at[pl.ds(0, nr)], cur[u], sem.at[par, u]).wait()
        for u in range(g):
            request(jnp.minimum(i + 1, n - 1) * g + u, i + 1 < n, nxt[u], sem.at[1 - par, u])
        r = lax.broadcasted_iota(I32, (tc, nr), 1)
        for u in range(g):
            tok = slice(u * tc, (u + 1) * tc)
            rows = cur[u][...].reshape(nr, -1)
            local_row = lrow_ref[tok, :]
            y0 = jnp.dot((r == local_row[:, 0:1]).astype(BF16), rows, preferred_element_type=F32)
            y1 = jnp.dot((r == local_row[:, 1:2]).astype(BF16), rows, preferred_element_type=F32)
            wt = wt_ref[tok, :]
            x2 = x1_ref[tok, :] + wt[:, 0:1] * y0 + wt[:, 1:2] * y1
            hp = _rms(x2, gp_ref[...]).astype(BF16)
            gate = jax.nn.sigmoid(jnp.dot(hp, wpg_b[...], preferred_element_type=F32))
            proj = jnp.dot(p_ref[tok, :].astype(BF16), wpp_b[...], preferred_element_type=F32)
            x3 = x2 + gate * proj
            o_ref[tok, :] = _rms(x3, gf_ref[...])

    pl.when(i % 2 == 0)(functools.partial(step, 0))
    pl.when(i % 2 == 1)(functools.partial(step, 1))


def _combine(tile_cnt, tile_start, lrow, x1, wts, p2, g_ple, w_ple_gate, w_ple_proj, g_final, ys):
    T, D = x1.shape
    P = p2.shape[1]
    g = COMBINE_TILES
    tm = g * TM_COMBINE
    n = T // tm
    row = lambda i: (i, 0)
    const = lambda i: (0, 0)
    smem = pl.BlockSpec(memory_space=pltpu.SMEM)
    run_buf = pltpu.VMEM((2 * TM_COMBINE,) + ys.shape[1:], ys.dtype)
    return pl.pallas_call(
        _combine_kernel,
        grid=(n,),
        in_specs=[smem, smem, pl.BlockSpec((tm, LANES), row),
                  pl.BlockSpec((tm, D), row), pl.BlockSpec((tm, LANES), row),
                  pl.BlockSpec((tm, P), row), pl.BlockSpec((1, D), const),
                  pl.BlockSpec((D, D), const, **_FETCH_ONCE), pl.BlockSpec((P, D), const, **_FETCH_ONCE),
                  pl.BlockSpec((1, D), const), pl.BlockSpec(memory_space=pl.ANY)],
        out_specs=pl.BlockSpec((tm, D), row),
        out_shape=jax.ShapeDtypeStruct((T, D), F32),
        scratch_shapes=[run_buf] * (2 * g) + [
                        pltpu.VMEM((D, D), BF16), pltpu.VMEM((P, D), BF16),
                        pltpu.SemaphoreType.DMA((2, g))],
        compiler_params=_cparams(("arbitrary",)),
        name="combine",
    )(tile_cnt, tile_start, lrow, x1, wts, p2, g_ple, w_ple_gate, w_ple_proj, g_final, ys)


def _block_diag(w):
    nb, d, _ = w.shape
    eye = jnp.eye(nb, dtype=w.dtype)
    return (eye[:, None, :, None] * w[:, :, None, :]).reshape(nb * d, nb * d)


def kernel(x, p, positions, g_mix, w_in, sinks, conv_w, conv_b, lru_wa, lru_ba, lru_wx, lru_bx,
           lru_lambda, g_attn_out, g_lru_out, w_out, g_ffn, w_router_group, b_router_group,
           w_router_expert, b_router_expert, w_expert_gate, w_expert_up, w_expert_down, g_ple,
           w_ple_gate, w_ple_proj, g_final):
    B, S, D = x.shape
    assert w_in.shape[0] == 1, "single-layer block only"
    T = B * S
    lru_w = conv_w.shape[-1]
    n_rows = 2 * T + N_EXPERTS * TM_EXPERT
    n_tiles = n_rows // TM_EXPERT
    pos3 = positions.reshape(T // TM_INPROJ, TM_INPROJ // LANES, LANES).astype(I32)
    x2 = x.reshape(T, D)
    q, k, v, lx, lg = _inproj(x2, pos3, g_mix[0][None], w_in[0], lru_w)
    mix_a = _attention(q, k, v, sinks[0], g_attn_out[0][None], B, S)
    mix_b = _lru(lx.reshape(B, S, lru_w), lg.reshape(B, S, lru_w), conv_w[0], conv_b[0][None],
                 _block_diag(lru_wa[0]).astype(BF16), lru_ba[0][None],
                 _block_diag(lru_wx[0]).astype(BF16), lru_bx[0][None],
                 lru_lambda[0][None], g_lru_out[0][None]).reshape(T, lru_w)
    n_router = N_GROUPS + N_EXPERTS
    w_router = jnp.pad(jnp.concatenate([w_router_group[0], w_router_expert[0]], axis=1),
                       ((0, 0), (0, LANES - n_router)))
    b_router = jnp.pad(jnp.concatenate([b_router_group[0], b_router_expert[0]]),
                       (0, LANES - n_router))[None]
    w_router_hi = w_router.astype(BF16)
    w_router_lo = (w_router - w_router_hi.astype(F32)).astype(BF16)
    x1, h2, eid, wts = _outproj(x2, mix_a, mix_b, w_out[0], g_ffn[0][None],
                                jnp.concatenate([w_router_hi, w_router_lo], axis=1), b_router)
    lrow_d, lrow_c, tile_table, n_valid, meta, tile_cnt, tile_start = _plan(eid, n_tiles)
    xs = _dispatch(tile_cnt, tile_start, meta, n_valid, lrow_d, h2, n_rows)
    Fh = w_expert_gate.shape[-1]
    ys = _experts(tile_table, n_valid, xs,
                  w_expert_gate[0].reshape(N_EXPERTS, D, Fh),
                  w_expert_up[0].reshape(N_EXPERTS, D, Fh),
                  w_expert_down[0].reshape(N_EXPERTS, Fh, D))
    out = _combine(tile_cnt, tile_start, lrow_c, x1, wts,
                   p[0].reshape(T, -1), g_ple[0][None], w_ple_gate[0], w_ple_proj[0], g_final[None], ys)
    return out.reshape(B, S, D)
```

```python
import functools
import math

import jax
import jax.numpy as jnp
from jax import lax
from jax.experimental import pallas as pl
from jax.experimental.pallas import tpu as pltpu

F32 = jnp.float32
BF16 = jnp.bfloat16
I32 = jnp.int32

EPS = 1e-6
N_HEADS = 8
N_KV_HEADS = 2
HEAD_DIM = 64
WINDOW = 128
ROPE_THETA = 10000.0
CONV_WIDTH = 4
LRU_C = 8.0
N_GROUPS = 4
EXPERTS_PER_GROUP = 8
N_EXPERTS = N_GROUPS * EXPERTS_PER_GROUP
NEG_INF = -1e30
LANES = 128
SUBLANES = 8

TM_INPROJ = 512
TM_PROJ = 512
TQ_ATTN = 512
TL_LRU = 256
TM_PLAN = 1024
TM_EXPERT = 512
EXPERT_FF_BLOCK = 256
TM_DISPATCH = 512
COMBINE_TILES = 1
TM_COMBINE = 256
VMEM_LIMIT = 56 * 1024 * 1024


def _rms(x, g):
    ms = jnp.mean(x * x, axis=-1, keepdims=True)
    return x * lax.rsqrt(ms + EPS) * g


_FETCH_ONCE = dict(pipeline_mode=pl.Buffered(1))


def _cparams(sem):
    return pltpu.CompilerParams(dimension_semantics=sem, vmem_limit_bytes=VMEM_LIMIT)


def _inproj_kernel(x_ref, pos_ref, g_ref, w_ref, q_ref, k_ref, v_ref, lx_ref, lg_ref, w_b):
    @pl.when(pl.program_id(0) == 0)
    def _():
        w_b[...] = w_ref[...].astype(BF16)

    h = _rms(x_ref[...], g_ref[...]).astype(BF16)
    proj = jnp.dot(h, w_b[...], preferred_element_type=F32)
    lane = lax.broadcasted_iota(I32, (1, LANES), 1)
    n_freq = HEAD_DIM // 2
    fidx = lax.broadcasted_iota(I32, (n_freq, 1), 0).astype(F32)
    inv_freq = jnp.exp(fidx * (-2.0 / HEAD_DIM * math.log(ROPE_THETA)))
    pos = pos_ref[0].astype(F32)
    cos_blocks, sin_blocks = [], []
    for c in range(pos.shape[0]):
        ang_t = inv_freq * pos[c:c + 1, :]
        cos_blocks.append(jnp.concatenate([jnp.cos(ang_t)] * (LANES // n_freq), axis=0).T)
        sin_blocks.append(jnp.concatenate([jnp.sin(ang_t)] * (LANES // n_freq), axis=0).T)
    cos = jnp.concatenate(cos_blocks, axis=0)
    sin = jnp.concatenate(sin_blocks, axis=0)
    first_half = (lane % HEAD_DIM) < (HEAD_DIM // 2)
    sin_signed = jnp.where(first_half, -sin, sin)

    def rope(t):
        partner = jnp.where(first_half, pltpu.roll(t, LANES - HEAD_DIM // 2, 1),
                            pltpu.roll(t, HEAD_DIM // 2, 1))
        return t * cos + partner * sin_signed

    attn_w = N_HEADS * HEAD_DIM
    scale = HEAD_DIM ** -0.5
    assert math.frexp(scale)[0] == 0.5, "score scale must be a power of two to fold into q exactly"
    for c in range(attn_w // LANES):
        q_ref[:, c * LANES:(c + 1) * LANES] = (
            rope(proj[:, c * LANES:(c + 1) * LANES]) * scale).astype(BF16)
    kv_w = N_KV_HEADS * HEAD_DIM
    assert kv_w == LANES
    k = rope(proj[:, attn_w:attn_w + kv_w])
    v = proj[:, attn_w + kv_w:attn_w + 2 * kv_w]
    k_ref[:, :kv_w] = k.astype(BF16)
    k_ref[:, kv_w:] = pltpu.roll(k, HEAD_DIM, 1).astype(BF16)
    v_ref[:, :kv_w] = v.astype(BF16)
    v_ref[:, kv_w:] = pltpu.roll(v, HEAD_DIM, 1).astype(BF16)
    off_lx = attn_w + 2 * kv_w
    lru_w = lx_ref.shape[1]
    lx_ref[...] = proj[:, off_lx:off_lx + lru_w]
    lg_ref[...] = proj[:, off_lx + lru_w:off_lx + 2 * lru_w]


def _inproj(x2, pos3, g_mix, w_in, lru_w):
    T, D = x2.shape
    tm = TM_INPROJ
    attn_w = N_HEADS * HEAD_DIM
    kv_w = N_KV_HEADS * HEAD_DIM
    row = lambda i: (i, 0)
    const = lambda i: (0, 0)
    return pl.pallas_call(
        _inproj_kernel,
        grid=(T // tm,),
        in_specs=[pl.BlockSpec((tm, D), row),
                  pl.BlockSpec((1, tm // LANES, LANES), lambda i: (i, 0, 0)),
                  pl.BlockSpec((1, D), const), pl.BlockSpec(w_in.shape, const, **_FETCH_ONCE)],
        out_specs=[pl.BlockSpec((tm, attn_w), row), pl.BlockSpec((tm, 2 * kv_w), row),
                   pl.BlockSpec((tm, 2 * kv_w), row), pl.BlockSpec((tm, lru_w), row),
                   pl.BlockSpec((tm, lru_w), row)],
        out_shape=[jax.ShapeDtypeStruct((T, attn_w), BF16),
                   jax.ShapeDtypeStruct((T, 2 * kv_w), BF16),
                   jax.ShapeDtypeStruct((T, 2 * kv_w), BF16), jax.ShapeDtypeStruct((T, lru_w), F32),
                   jax.ShapeDtypeStruct((T, lru_w), F32)],
        scratch_shapes=[pltpu.VMEM(w_in.shape, BF16)],
        compiler_params=_cparams(("arbitrary",)),
        name="inproj",
    )(x2, pos3, g_mix, w_in)


def _attn_kernel(sinks_ref, q_ref, k_ref, v_ref, kp_ref, vp_ref, g_ref, o_ref):
    n = pl.program_id(1)
    blk = WINDOW
    hd = HEAD_DIM
    nqb = q_ref.shape[0] // blk
    n_pairs = N_HEADS // 2
    pairs_per_group = n_pairs // N_KV_HEADS
    lane = lax.broadcasted_iota(I32, (1, LANES), 1)
    first = lane < hd
    qi = lax.broadcasted_iota(I32, (blk, 2 * blk), 0)
    kj = lax.broadcasted_iota(I32, (blk, 2 * blk), 1)
    rel = qi + blk - kj
    band = (rel >= 0) & (rel < WINDOW)
    bd_row = lax.broadcasted_iota(I32, (4 * blk, LANES), 0)
    bd_lane = lax.broadcasted_iota(I32, (4 * blk, LANES), 1)
    ones_bd = ((bd_row < 2 * blk) == (bd_lane < hd)).astype(BF16)
    zero = jnp.zeros((), BF16)

    def block_diag(t):
        same, swapped = t[:, :LANES], t[:, LANES:]
        out = []
        for g in range(N_KV_HEADS):
            top = jnp.where(first, same if g == 0 else swapped, zero)
            bot = jnp.where(first, zero, swapped if g == 0 else same)
            out.append(jnp.concatenate([top, bot], axis=0))
        return out

    for j in range(nqb):
        rows = slice(j * blk, (j + 1) * blk)
        if j == 0:
            k_prev, v_prev = kp_ref[...], vp_ref[...]
            mask = band & ((kj >= blk) | (n > 0))
        else:
            prev_rows = slice((j - 1) * blk, j * blk)
            k_prev, v_prev = k_ref[prev_rows, :], v_ref[prev_rows, :]
            mask = band
        k_bd = block_diag(jnp.concatenate([k_prev, k_ref[rows, :]], axis=0))
        v_bd = block_diag(jnp.concatenate([v_prev, v_ref[rows, :]], axis=0))
        outs = []
        for p in range(n_pairs):
            g = p // pairs_per_group
            qp = q_ref[rows, p * LANES:(p + 1) * LANES]
            s = lax.dot_general(qp, k_bd[g], (((1,), (1,)), ((), ())), preferred_element_type=F32)
            es, ms = [], []
            for a in range(2):
                sa = jnp.where(mask, s[:, a * 2 * blk:(a + 1) * 2 * blk], NEG_INF)
                m = jnp.maximum(jnp.max(sa, axis=-1, keepdims=True), sinks_ref[2 * p + a])
                es.append(jnp.exp(sa - m))
                ms.append(m)
            e = jnp.concatenate(es, axis=1).astype(BF16)
            od = jnp.dot(e, jnp.concatenate([v_bd[g], ones_bd], axis=1), preferred_element_type=F32)
            sink_term = jnp.where(first, jnp.exp(sinks_ref[2 * p] - ms[0]),
                                  jnp.exp(sinks_ref[2 * p + 1] - ms[1]))
            outs.append(od[:, :LANES] / (od[:, LANES:] + sink_term))
        attn = jnp.concatenate(outs, axis=1)
        o_ref[rows, :] = _rms(attn, g_ref[...]).astype(BF16)


def _attention(q, k, v, sinks, g_attn, B, S):
    T, attn_w = q.shape
    kv_w = k.shape[1]
    blk = WINDOW
    tq = TQ_ATTN
    nt = S // tq
    cur = lambda b, n: (b * nt + n, 0)
    prev = lambda b, n: (b * (S // blk) + jnp.maximum(n * (tq // blk) - 1, 0), 0)
    return pl.pallas_call(
        _attn_kernel,
        grid=(B, nt),
        in_specs=[pl.BlockSpec(memory_space=pltpu.SMEM),
                  pl.BlockSpec((tq, attn_w), cur),
                  pl.BlockSpec((tq, kv_w), cur), pl.BlockSpec((tq, kv_w), cur),
                  pl.BlockSpec((blk, kv_w), prev), pl.BlockSpec((blk, kv_w), prev),
                  pl.BlockSpec((1, attn_w), lambda b, n: (0, 0))],
        out_specs=pl.BlockSpec((tq, attn_w), cur),
        out_shape=jax.ShapeDtypeStruct((T, attn_w), BF16),
        compiler_params=_cparams(("arbitrary", "arbitrary")),
        name="attn",
    )(sinks, q, k, v, k, v, g_attn)


def _neg_expm1(x, exp_x):
    series = -x * (1.0 + x * (0.5 + x * (1.0 / 6.0 + x * (1.0 / 24.0))))
    return jnp.where(x > -0.02, series, 1.0 - exp_x)


def _gelu_tanh(x):
    c = math.sqrt(2.0 / math.pi)
    return x * (0.5 + 0.5 * jnp.tanh(x * (c + (c * 0.044715) * (x * x))))


def _lru_kernel(lx_ref, lg_ref, cw_ref, cb_ref, wa_ref, ba_ref, wx_ref, bx_ref, lam_ref, g_ref,
                o_ref, xpad, a_scr, b_scr, h_scr, h_carry):
    i = pl.program_id(0)
    B, tl, C = lx_ref.shape
    pad = SUBLANES

    @pl.when(i == 0)
    def _():
        xpad[:, 0:pad, :] = jnp.zeros((B, pad, C), F32)
        h_carry[...] = jnp.zeros_like(h_carry)

    @pl.when(i > 0)
    def _():
        xpad[:, 0:pad, :] = xpad[:, tl:tl + pad, :]

    xpad[:, pad:, :] = lx_ref[...]
    groups = (B * tl // SUBLANES, SUBLANES, C)
    sub = lax.broadcasted_iota(I32, groups, 1)
    x_now = lx_ref[...].reshape(groups)
    x_before = xpad[:, pl.ds(0, tl), :].reshape(groups)
    y = x_now * cw_ref[CONV_WIDTH - 1:CONV_WIDTH, :] + cb_ref[...]
    for s in range(1, CONV_WIDTH):
        shifted = pltpu.roll(jnp.where(sub >= SUBLANES - s, x_before, x_now), s, 1)
        y = y + shifted * cw_ref[CONV_WIDTH - 1 - s:CONV_WIDTH - s, :]
    y2 = y.reshape(B * tl, C)
    yb = y2.astype(BF16)
    gate_a = jnp.dot(yb, wa_ref[...], preferred_element_type=F32) + ba_ref[...]
    gate_x = jnp.dot(yb, wx_ref[...], preferred_element_type=F32) + bx_ref[...]
    r = jax.nn.sigmoid(gate_a)
    ig = jax.nn.sigmoid(gate_x)
    nl = -lam_ref[...]
    softplus = jnp.maximum(nl, 0.0) + jnp.log1p(jnp.exp(-jnp.abs(nl)))
    log_a = (-LRU_C) * r * softplus
    a_all = jnp.exp(log_a)
    one_minus_a2 = _neg_expm1(2.0 * log_a, a_all * a_all)
    b_all = one_minus_a2 * lax.rsqrt(jnp.maximum(one_minus_a2, 1e-30)) * ig * y2
    groups = (B * tl // SUBLANES, SUBLANES, C)
    sub = lax.broadcasted_iota(I32, groups, 1)
    a_cum, b_loc = a_all.reshape(groups), b_all.reshape(groups)
    d = 1
    while d < SUBLANES:
        keep = sub >= d
        a_prev = pltpu.roll(a_cum, d, 1)
        b_prev = pltpu.roll(b_loc, d, 1)
        b_loc = b_loc + jnp.where(keep, a_cum * b_prev, 0.0)
        a_cum = jnp.where(keep, a_cum * a_prev, a_cum)
        d *= 2
    a_scr[...] = a_cum.reshape(B, tl, C)
    b_scr[...] = b_loc.reshape(B, tl, C)

    def carry(g, h_prev):
        rows = pl.ds(pl.multiple_of(g * SUBLANES, SUBLANES), SUBLANES)
        h = b_scr[:, rows, :] + a_scr[:, rows, :] * h_prev
        h_scr[:, rows, :] = h
        return h[:, SUBLANES - 1:SUBLANES, :]

    h_carry[...] = lax.fori_loop(0, tl // SUBLANES, carry, h_carry[...], unroll=4)
    lru = h_scr[...].reshape(B * tl, C) * _gelu_tanh(lg_ref[...].reshape(B * tl, C))
    o_ref[...] = _rms(lru, g_ref[...]).astype(BF16).reshape(B, tl, C)


def _lru(lx3, lg3, conv_w, conv_b, wa_bd, ba, wx_bd, bx, lam, g_lru):
    B, S, C = lx3.shape
    tl = TL_LRU
    blk = lambda i: (0, i, 0)
    const = lambda i: (0, 0)
    vec = pl.BlockSpec((1, C), const)
    return pl.pallas_call(
        _lru_kernel,
        grid=(S // tl,),
        in_specs=[pl.BlockSpec((B, tl, C), blk), pl.BlockSpec((B, tl, C), blk),
                  pl.BlockSpec((CONV_WIDTH, C), const), vec,
                  pl.BlockSpec((C, C), const), vec, pl.BlockSpec((C, C), const), vec, vec, vec],
        out_specs=pl.BlockSpec((B, tl, C), blk),
        out_shape=jax.ShapeDtypeStruct((B, S, C), BF16),
        scratch_shapes=[pltpu.VMEM((B, tl + SUBLANES, C), F32),
                        pltpu.VMEM((B, tl, C), F32), pltpu.VMEM((B, tl, C), F32),
                        pltpu.VMEM((B, tl, C), F32), pltpu.VMEM((B, 1, C), F32)],
        compiler_params=_cparams(("arbitrary",)),
        name="lru",
    )(lx3, lg3, conv_w, conv_b, wa_bd, ba, wx_bd, bx, lam, g_lru)


def _outproj_kernel(x_ref, ma_ref, mb_ref, wo_ref, g_ref, wr_ref, br_ref,
                    x1_ref, h2_ref, eid_ref, wt_ref, wo_b):
    @pl.when(pl.program_id(0) == 0)
    def _():
        wo_b[...] = wo_ref[...].astype(BF16)

    half = ma_ref.shape[1]
    x1 = (x_ref[...]
          + jnp.dot(ma_ref[...], wo_b[0:half, :], preferred_element_type=F32)
          + jnp.dot(mb_ref[...], wo_b[half:, :], preferred_element_type=F32))
    x1_ref[...] = x1
    h2 = _rms(x1, g_ref[...])
    h2_ref[...] = h2.astype(BF16).reshape(h2_ref.shape)
    h_hi = h2.astype(BF16)
    h_lo = (h2 - h_hi.astype(F32)).astype(BF16)
    hw = jnp.dot(h_hi, wr_ref[...], preferred_element_type=F32)
    logits = (hw[:, :LANES] + hw[:, LANES:]
              + jnp.dot(h_lo, wr_ref[:, :LANES], preferred_element_type=F32) + br_ref[...])
    tm = logits.shape[0]
    lane = lax.broadcasted_iota(I32, (tm, LANES), 1)
    lane_f = lane.astype(F32)
    big = float(LANES)

    def first_argmax(vals):
        m = jnp.max(vals, axis=-1, keepdims=True)
        idx = jnp.min(jnp.where(vals == m, lane_f, big), axis=-1, keepdims=True)
        return m, idx.astype(I32)

    gl = jnp.where(lane < N_GROUPS, logits, NEG_INF)
    gmax, gidx = first_argmax(gl)
    gsum = jnp.sum(jnp.where(lane < N_GROUPS, jnp.exp(gl - gmax), 0.0), axis=-1, keepdims=True)
    g_top_p = 1.0 / gsum
    lo = N_GROUPS + EXPERTS_PER_GROUP * gidx
    el = jnp.where((lane >= lo) & (lane < lo + EXPERTS_PER_GROUP), logits, NEG_INF)
    m1, i1 = first_argmax(el)
    el2 = jnp.where(lane == i1, NEG_INF, el)
    m2, i2 = first_argmax(el2)
    ratio = jnp.exp(m2 - m1)
    w1 = g_top_p / (1.0 + ratio)
    w2 = g_top_p * ratio / (1.0 + ratio)
    eid_ref[...] = jnp.where(lane == 0, i1 - N_GROUPS, jnp.where(lane == 1, i2 - N_GROUPS, 0))
    wt_ref[...] = jnp.where(lane == 0, w1, jnp.where(lane == 1, w2, 0.0))


def _outproj(x2, mix_a, mix_b, w_out, g_ffn, w_router, b_router):
    T, D = x2.shape
    tm = TM_PROJ
    half = mix_a.shape[1]
    row = lambda i: (i, 0)
    const = lambda i: (0, 0)
    return pl.pallas_call(
        _outproj_kernel,
        grid=(T // tm,),
        in_specs=[pl.BlockSpec((tm, D), row), pl.BlockSpec((tm, half), row),
                  pl.BlockSpec((tm, half), row), pl.BlockSpec(w_out.shape, const, **_FETCH_ONCE),
                  pl.BlockSpec((1, D), const), pl.BlockSpec((D, 2 * LANES), const),
                  pl.BlockSpec((1, LANES), const)],
        out_specs=[pl.BlockSpec((tm, D), row),
                   pl.BlockSpec((tm, D // LANES, LANES), lambda i: (i, 0, 0)),
                   pl.BlockSpec((tm, LANES), row), pl.BlockSpec((tm, LANES), row)],
        out_shape=[jax.ShapeDtypeStruct((T, D), F32),
                   jax.ShapeDtypeStruct((T, D // LANES, LANES), BF16),
                   jax.ShapeDtypeStruct((T, LANES), I32), jax.ShapeDtypeStruct((T, LANES), F32)],
        scratch_shapes=[pltpu.VMEM(w_out.shape, BF16)],
        compiler_params=_cparams(("arbitrary",)),
        name="outproj",
    )(x2, mix_a, mix_b, w_out, g_ffn, w_router, b_router)


def _lane_cumsum(x):
    lane = lax.broadcasted_iota(I32, x.shape, 1)
    shift = 1
    while shift < LANES:
        x = x + jnp.where(lane >= shift, pltpu.roll(x, shift, 1), 0)
        shift *= 2
    return x


def _plan_kernel(eid_ref, lrow_d_ref, lrow_c_ref, te_ref, nv_ref, meta_ref, tcnt_ref, tstart_ref,
                 cnt_scr, tcnt_scr, tpre_scr):
    i = pl.program_id(0)
    tm = eid_ref.shape[0]
    tc = TM_COMBINE
    group = TM_DISPATCH // TM_COMBINE
    nsub = tm // tc
    lane = lax.broadcasted_iota(I32, (tm, LANES), 1)
    eid = eid_ref[...]
    sel1 = lane == eid[:, 0:1]
    sel2 = lane == eid[:, 1:2]

    @pl.when(i == 0)
    def _():
        cnt_scr[...] = jnp.zeros_like(cnt_scr)

    onehot = (sel1 | sel2).astype(BF16)
    r = lax.broadcasted_iota(I32, (tm, tm), 0)
    c = lax.broadcasted_iota(I32, (tm, tm), 1)
    strict_lower = (r > c).astype(BF16)
    within = jnp.dot(strict_lower, onehot, preferred_element_type=F32)
    base = cnt_scr[0:1, :]
    onehot_f = onehot.astype(F32)
    row = lax.broadcasted_iota(I32, (tm, LANES), 0)
    srow = lax.broadcasted_iota(I32, (2 * SUBLANES, LANES), 0)
    counts = [jnp.sum(onehot_f[s * tc:(s + 1) * tc], axis=0, keepdims=True) for s in range(nsub)]
    cnt_mat = jnp.zeros((2 * SUBLANES, LANES), F32)
    seen = []
    total = jnp.zeros((1, LANES), F32)
    for s in range(nsub):
        seen.append(total)
        tile = i * nsub + s
        tcnt_scr[pl.ds(tile, 1), :] = counts[s]
        tpre_scr[pl.ds(tile, 1), :] = base + total
        cnt_mat = jnp.where(srow == s, counts[s], cnt_mat)
        cnt_mat = jnp.where(srow == SUBLANES + s // group, cnt_mat + counts[s], cnt_mat)
        total = total + counts[s]
    cnt_i = cnt_mat.astype(I32)
    run_start = (_lane_cumsum(cnt_i) - cnt_i).astype(F32)

    def local_rows(first_tile_of, table_row_of):
        local = None
        for s in range(nsub):
            val = within - seen[first_tile_of(s)] + run_start[table_row_of(s):table_row_of(s) + 1, :]
            local = val if local is None else jnp.where(row >= s * tc, val, local)
        l1 = jnp.sum(jnp.where(sel1, local, 0.0), axis=-1, keepdims=True)
        l2 = jnp.sum(jnp.where(sel2, local, 0.0), axis=-1, keepdims=True)
        return jnp.where(lane == 0, l1, jnp.where(lane == 1, l2, 0.0)).astype(I32)

    lrow_c_ref[...] = local_rows(lambda s: s, lambda s: s)
    lrow_d_ref[...] = local_rows(lambda s: (s // group) * group, lambda s: SUBLANES + s // group)
    cnt_scr[...] = cnt_scr[...] + total

    @pl.when(i == pl.num_programs(0) - 1)
    def _():
        shift = int(math.log2(TM_EXPERT))
        cnt = cnt_scr[...].astype(I32)
        padded = ((cnt + (TM_EXPERT - 1)) >> shift) << shift
        ends = _lane_cumsum(padded)
        nt = te_ref.shape[0]
        end_tile = (ends >> shift)[0:1, :]
        tl_lane = lax.broadcasted_iota(I32, (nt, LANES), 1)
        tile = lax.broadcasted_iota(I32, (nt, LANES), 0)
        done = jnp.where((tl_lane < N_EXPERTS) & (end_tile <= tile), 1.0, 0.0)
        te = jnp.minimum(jnp.sum(done, axis=-1, keepdims=True).astype(I32), N_EXPERTS - 1)
        n_valid = jnp.sum(jnp.where(tl_lane[0:1, :] == N_EXPERTS - 1, end_tile, 0).astype(F32),
                          axis=-1, keepdims=True).astype(I32)
        nv_ref[...] = jnp.broadcast_to(n_valid, nv_ref.shape)
        seg_end = jnp.sum(jnp.where(tl_lane == te, end_tile, 0).astype(F32), axis=-1,
                          keepdims=True).astype(I32)
        nxt = jnp.sum(jnp.where((tl_lane < N_EXPERTS) & (end_tile <= seg_end), 1.0, 0.0), axis=-1,
                      keepdims=True).astype(I32)
        nxt = jnp.where(seg_end < n_valid, jnp.minimum(nxt, N_EXPERTS - 1), -1)
        te_ref[...] = jnp.where(tl_lane == 1, nxt, te)
        row = lax.broadcasted_iota(I32, cnt.shape, 0)
        meta_ref[...] = jnp.where(row == 0, ends - padded + cnt, jnp.where(row == 1, padded - cnt, 0))
        tcnt_ref[...] = tcnt_scr[...].astype(I32)
        tstart_ref[...] = (tpre_scr[...] + (ends - padded).astype(F32)[0:1, :]).astype(I32)


def _plan(eid, n_tiles):
    T = eid.shape[0]
    tm = TM_PLAN
    steps = T // tm
    n_ctiles = T // TM_COMBINE
    assert TM_DISPATCH % TM_COMBINE == 0 and tm % TM_DISPATCH == 0 and tm // TM_COMBINE <= SUBLANES
    row = lambda i: (i, 0)
    const = lambda i: (0, 0)
    return pl.pallas_call(
        _plan_kernel,
        grid=(steps,),
        in_specs=[pl.BlockSpec((tm, LANES), row)],
        out_specs=[pl.BlockSpec((tm, LANES), row), pl.BlockSpec((tm, LANES), row),
                   pl.BlockSpec((n_tiles, LANES), const), pl.BlockSpec((SUBLANES, LANES), const),
                   pl.BlockSpec((SUBLANES, LANES), const),
                   pl.BlockSpec((n_ctiles, LANES), const), pl.BlockSpec((n_ctiles, LANES), const)],
        out_shape=[jax.ShapeDtypeStruct((T, LANES), I32), jax.ShapeDtypeStruct((T, LANES), I32),
                   jax.ShapeDtypeStruct((n_tiles, LANES), I32),
                   jax.ShapeDtypeStruct((SUBLANES, LANES), I32),
                   jax.ShapeDtypeStruct((SUBLANES, LANES), I32),
                   jax.ShapeDtypeStruct((n_ctiles, LANES), I32),
                   jax.ShapeDtypeStruct((n_ctiles, LANES), I32)],
        scratch_shapes=[pltpu.VMEM((SUBLANES, LANES), F32),
                        pltpu.VMEM((n_ctiles, LANES), F32), pltpu.VMEM((n_ctiles, LANES), F32)],
        compiler_params=_cparams(("arbitrary",)),
        name="plan",
    )(eid)


_PAD_BITS = tuple(1 << b for b in reversed(range(int(math.log2(TM_EXPERT)))))


def _run_bits(max_rows):
    return tuple(1 << b for b in reversed(range(int(math.log2(max_rows)) + 1)))


def _dispatch_kernel(tcnt_ref, tstart_ref, meta_ref, nv_ref, lrow_ref, h_ref,
                     xs_ref, lbuf_a, lbuf_b, zeros, sem, zsem):
    i = pl.program_id(0)
    n = pl.num_programs(0)
    td = h_ref.shape[0]
    nr = 2 * td

    def wait_buf(buf, s):
        pltpu.make_async_copy(buf, xs_ref.at[pl.ds(0, nr)], sem.at[s]).wait()

    def step(cur, cur_s, prev, prev_s):
        @pl.when(i >= 2)
        def _():
            wait_buf(cur, cur_s)

        local_row = lrow_ref[...].astype(F32).T
        r = lax.broadcasted_iota(I32, (nr, td), 0).astype(F32)
        onehot = ((r == local_row[0:1, :]) | (r == local_row[1:2, :])).astype(BF16)
        sorted_rows = jnp.dot(onehot, h_ref[...].reshape(td, -1), preferred_element_type=F32)
        cur[...] = sorted_rows.astype(BF16).reshape(cur.shape)

        group = TM_DISPATCH // TM_COMBINE
        first = jnp.maximum(i - 1, 0) * group
        lo = 0
        for e in range(N_EXPERTS):
            cnt = sum(tcnt_ref[first + u, e] for u in range(group))
            cnt = jnp.where(i >= 1, cnt, 0)
            dst = tstart_ref[first, e]
            for bit in _run_bits(td):
                off = cnt & ~(2 * bit - 1)

                @pl.when((cnt & bit) != 0)
                def _():
                    pltpu.make_async_copy(prev.at[pl.ds(lo + off, bit)],
                                          xs_ref.at[pl.ds(dst + off, bit)], sem.at[prev_s]).start()
            lo = lo + cnt

        @pl.when(i == n - 1)
        def _():
            wait_buf(prev, prev_s)

    n_tiles = xs_ref.shape[0] // TM_EXPERT
    first_tail = n_tiles - N_EXPERTS

    def pad_copy(e, bit):
        cnt = meta_ref[1, e]
        dst = meta_ref[0, e] + (cnt & ~(2 * bit - 1))
        return (cnt & bit) != 0, pltpu.make_async_copy(zeros.at[pl.ds(0, bit)],
                                                       xs_ref.at[pl.ds(dst, bit)], zsem)

    def tail_copy(j):
        return j >= nv_ref[0, 0], pltpu.make_async_copy(
            zeros, xs_ref.at[pl.ds(j * TM_EXPERT, TM_EXPERT)], zsem)

    def for_each_fill(act):
        def per_expert(e, c):
            for bit in _PAD_BITS:
                pred, cp = pad_copy(e, bit)
                pl.when(pred)(functools.partial(act, cp))
            return c

        lax.fori_loop(0, N_EXPERTS, per_expert, 0)

        def per_tile(j, c):
            pred, cp = tail_copy(j)
            pl.when(pred)(functools.partial(act, cp))
            return c

        lax.fori_loop(first_tail, n_tiles, per_tile, 0)

    @pl.when(i == 0)
    def _():
        zeros[...] = jnp.zeros_like(zeros)
        for_each_fill(lambda cp: cp.start())

    pl.when(i % 2 == 0)(functools.partial(step, lbuf_a, 0, lbuf_b, 1))
    pl.when(i % 2 == 1)(functools.partial(step, lbuf_b, 1, lbuf_a, 0))

    @pl.when(i == n - 1)
    def _():
        for_each_fill(lambda cp: cp.wait())


def _dispatch(tile_cnt, tile_start, meta, n_valid, lrow, h2, n_rows):
    T = h2.shape[0]
    slab = h2.shape[1:]
    td = TM_DISPATCH
    nt = T // td
    smem = pl.BlockSpec(memory_space=pltpu.SMEM)
    return pl.pallas_call(
        _dispatch_kernel,
        grid=(nt + 1,),
        in_specs=[smem, smem, smem, smem,
                  pl.BlockSpec((td, LANES), lambda i: (jnp.minimum(i, nt - 1), 0)),
                  pl.BlockSpec((td,) + slab, lambda i: (jnp.minimum(i, nt - 1), 0, 0))],
        out_specs=pl.BlockSpec(memory_space=pl.ANY),
        out_shape=jax.ShapeDtypeStruct((n_rows,) + slab, h2.dtype),
        scratch_shapes=[pltpu.VMEM((2 * td,) + slab, h2.dtype),
                        pltpu.VMEM((2 * td,) + slab, h2.dtype),
                        pltpu.VMEM((TM_EXPERT,) + slab, h2.dtype),
                        pltpu.SemaphoreType.DMA((2,)), pltpu.SemaphoreType.DMA(())],
        compiler_params=_cparams(("arbitrary",)),
        name="dispatch",
    )(tile_cnt, tile_start, meta, n_valid, lrow, h2)


def _experts_kernel(te_ref, nv_ref, xs_ref, wg_ref, wu_ref, wd_ref, ys_ref,
                    wg_f, wu_f, wd_f, wg_b, wu_b, wd_b, slot_ref, sem):
    j = pl.program_id(0)
    valid = j < nv_ref[0, 0]
    new_expert = valid & ((j == 0) | (te_ref[j, 0] != te_ref[jnp.maximum(j - 1, 0), 0]))

    def fetch(e, s):
        return (pltpu.make_async_copy(wg_ref.at[e], wg_f.at[s], sem.at[s]),
                pltpu.make_async_copy(wu_ref.at[e], wu_f.at[s], sem.at[s]),
                pltpu.make_async_copy(wd_ref.at[e], wd_f.at[s], sem.at[s]))

    @pl.when(j == 0)
    def _():
        slot_ref[0] = 0
        for cp in fetch(te_ref[0, 0], 0):
            cp.start()

    @pl.when(new_expert)
    def _():
        s = slot_ref[0]
        for cp in fetch(te_ref[j, 0], s):
            cp.wait()
        nxt = te_ref[j, 1]

        @pl.when(nxt >= 0)
        def _():
            for cp in fetch(nxt, 1 - s):
                cp.start()

        wg_b[...] = wg_f[s].astype(BF16)
        wu_b[...] = wu_f[s].astype(BF16)
        wd_b[...] = wd_f[s].astype(BF16)
        slot_ref[0] = 1 - s

    @pl.when(valid)
    def _():
        x = xs_ref[...].reshape(xs_ref.shape[0], -1)
        y = None
        for lo in range(0, wg_b.shape[1], EXPERT_FF_BLOCK):
            cols = slice(lo, lo + EXPERT_FF_BLOCK)
            a = jnp.dot(x, wg_b[:, cols], preferred_element_type=F32)
            u = jnp.dot(x, wu_b[:, cols], preferred_element_type=F32)
            hid = (a * jax.nn.sigmoid(a) * u).astype(BF16)
            part = jnp.dot(hid, wd_b[cols, :], preferred_element_type=F32)
            y = part if y is None else y + part
        ys_ref[...] = y.astype(BF16).reshape(ys_ref.shape)

    @pl.when(j >= nv_ref[0, 0])
    def _():
        ys_ref[...] = jnp.zeros_like(ys_ref)


def _experts(tile_table, n_valid, xs, wg, wu, wd):
    R = xs.shape[0]
    slab = xs.shape[1:]
    E, D, Fh = wg.shape
    tm = TM_EXPERT
    n_tiles = R // tm
    rows = lambda j, te, nv: (j, 0, 0)
    rows_in = lambda j, te, nv: (jnp.minimum(j, nv[0, 0] - 1), 0, 0)
    hbm = pl.BlockSpec(memory_space=pl.ANY)
    return pl.pallas_call(
        _experts_kernel,
        grid_spec=pltpu.PrefetchScalarGridSpec(
            num_scalar_prefetch=2,
            grid=(n_tiles,),
            in_specs=[pl.BlockSpec((tm,) + slab, rows_in), hbm, hbm, hbm],
            out_specs=pl.BlockSpec((tm,) + slab, rows),
            scratch_shapes=[pltpu.VMEM((2, D, Fh), F32), pltpu.VMEM((2, D, Fh), F32),
                            pltpu.VMEM((2, Fh, D), F32),
                            pltpu.VMEM((D, Fh), BF16), pltpu.VMEM((D, Fh), BF16),
                            pltpu.VMEM((Fh, D), BF16),
                            pltpu.SMEM((1,), I32), pltpu.SemaphoreType.DMA((2,))]),
        out_shape=jax.ShapeDtypeStruct(xs.shape, xs.dtype),
        compiler_params=_cparams(("arbitrary",)),
        name="experts",
    )(tile_table, n_valid, xs, wg, wu, wd)


def _combine_kernel(tcnt_ref, tstart_ref, lrow_ref, x1_ref, wt_ref, p_ref, gp_ref, wpg_ref, wpp_ref,
                    gf_ref, ys_ref, o_ref, *scratch):
    g = COMBINE_TILES
    bufs = (scratch[0:g], scratch[g:2 * g])
    wpg_b, wpp_b, sem = scratch[2 * g:]
    i = pl.program_id(0)
    n = pl.num_programs(0)
    tc = x1_ref.shape[0] // g
    nr = 2 * tc

    def request(tile, live, buf, s):
        lo = 0
        for e in range(N_EXPERTS):
            cnt = jnp.where(live, tcnt_ref[tile, e], 0)
            src = tstart_ref[tile, e]
            for bit in _run_bits(tc):
                off = cnt & ~(2 * bit - 1)

                @pl.when((cnt & bit) != 0)
                def _():
                    pltpu.make_async_copy(ys_ref.at[pl.ds(src + off, bit)],
                                          buf.at[pl.ds(lo + off, bit)], s).start()
            lo = lo + cnt

    @pl.when(i == 0)
    def _():
        for u in range(g):
            request(u, True, bufs[0][u], sem.at[0, u])
        wpg_b[...] = wpg_ref[...].astype(BF16)
        wpp_b[...] = wpp_ref[...].astype(BF16)

    def step(par):
        cur, nxt = bufs[par], bufs[1 - par]
        for u in range(g):
            pltpu.make_async_copy(ys_ref.at[pl.ds(0, nr)], cur[u], sem.at[par, u]).wait()
        for u in range(g):
            request(jnp.minimum(i + 1, n - 1) * g + u, i + 1 < n, nxt[u], sem.at[1 - par, u])
        r = lax.broadcasted_iota(I32, (tc, nr), 1)
        for u in range(g):
            tok = slice(u * tc, (u + 1) * tc)
            rows = cur[u][...].reshape(nr, -1)
            local_row = lrow_ref[tok, :]
            y0 = jnp.dot((r == local_row[:, 0:1]).astype(BF16), rows, preferred_element_type=F32)
            y1 = jnp.dot((r == local_row[:, 1:2]).astype(BF16), rows, preferred_element_type=F32)
            wt = wt_ref[tok, :]
            x2 = x1_ref[tok, :] + wt[:, 0:1] * y0 + wt[:, 1:2] * y1
            hp = _rms(x2, gp_ref[...]).astype(BF16)
            gate = jax.nn.sigmoid(jnp.dot(hp, wpg_b[...], preferred_element_type=F32))
            proj = jnp.dot(p_ref[tok, :].astype(BF16), wpp_b[...], preferred_element_type=F32)
            x3 = x2 + gate * proj
            o_ref[tok, :] = _rms(x3, gf_ref[...])

    pl.when(i % 2 == 0)(functools.partial(step, 0))
    pl.when(i % 2 == 1)(functools.partial(step, 1))


def _combine(tile_cnt, tile_start, lrow, x1, wts, p2, g_ple, w_ple_gate, w_ple_proj, g_final, ys):
    T, D = x1.shape
    P = p2.shape[1]
    g = COMBINE_TILES
    tm = g * TM_COMBINE
    n = T // tm
    row = lambda i: (i, 0)
    const = lambda i: (0, 0)
    smem = pl.BlockSpec(memory_space=pltpu.SMEM)
    run_buf = pltpu.VMEM((2 * TM_COMBINE,) + ys.shape[1:], ys.dtype)
    return pl.pallas_call(
        _combine_kernel,
        grid=(n,),
        in_specs=[smem, smem, pl.BlockSpec((tm, LANES), row),
                  pl.BlockSpec((tm, D), row), pl.BlockSpec((tm, LANES), row),
                  pl.BlockSpec((tm, P), row), pl.BlockSpec((1, D), const),
                  pl.BlockSpec((D, D), const, **_FETCH_ONCE), pl.BlockSpec((P, D), const, **_FETCH_ONCE),
                  pl.BlockSpec((1, D), const), pl.BlockSpec(memory_space=pl.ANY)],
        out_specs=pl.BlockSpec((tm, D), row),
        out_shape=jax.ShapeDtypeStruct((T, D), F32),
        scratch_shapes=[run_buf] * (2 * g) + [
                        pltpu.VMEM((D, D), BF16), pltpu.VMEM((P, D), BF16),
                        pltpu.SemaphoreType.DMA((2, g))],
        compiler_params=_cparams(("arbitrary",)),
        name="combine",
    )(tile_cnt, tile_start, lrow, x1, wts, p2, g_ple, w_ple_gate, w_ple_proj, g_final, ys)


def _block_diag(w):
    nb, d, _ = w.shape
    eye = jnp.eye(nb, dtype=w.dtype)
    return (eye[:, None, :, None] * w[:, :, None, :]).reshape(nb * d, nb * d)


def kernel(x, p, positions, g_mix, w_in, sinks, conv_w, conv_b, lru_wa, lru_ba, lru_wx, lru_bx,
           lru_lambda, g_attn_out, g_lru_out, w_out, g_ffn, w_router_group, b_router_group,
           w_router_expert, b_router_expert, w_expert_gate, w_expert_up, w_expert_down, g_ple,
           w_ple_gate, w_ple_proj, g_final):
    B, S, D = x.shape
    assert w_in.shape[0] == 1, "single-layer block only"
    T = B * S
    lru_w = conv_w.shape[-1]
    n_rows = 2 * T + N_EXPERTS * TM_EXPERT
    n_tiles = n_rows // TM_EXPERT
    pos3 = positions.reshape(T // TM_INPROJ, TM_INPROJ // LANES, LANES).astype(I32)
    x2 = x.reshape(T, D)
    q, k, v, lx, lg = _inproj(x2, pos3, g_mix[0][None], w_in[0], lru_w)
    mix_a = _attention(q, k, v, sinks[0], g_attn_out[0][None], B, S)
    mix_b = _lru(lx.reshape(B, S, lru_w), lg.reshape(B, S, lru_w), conv_w[0], conv_b[0][None],
                 _block_diag(lru_wa[0]).astype(BF16), lru_ba[0][None],
                 _block_diag(lru_wx[0]).astype(BF16), lru_bx[0][None],
                 lru_lambda[0][None], g_lru_out[0][None]).reshape(T, lru_w)
    n_router = N_GROUPS + N_EXPERTS
    w_router = jnp.pad(jnp.concatenate([w_router_group[0], w_router_expert[0]], axis=1),
                       ((0, 0), (0, LANES - n_router)))
    b_router = jnp.pad(jnp.concatenate([b_router_group[0], b_router_expert[0]]),
                       (0, LANES - n_router))[None]
    w_router_hi = w_router.astype(BF16)
    w_router_lo = (w_router - w_router_hi.astype(F32)).astype(BF16)
    x1, h2, eid, wts = _outproj(x2, mix_a, mix_b, w_out[0], g_ffn[0][None],
                                jnp.concatenate([w_router_hi, w_router_lo], axis=1), b_router)
    lrow_d, lrow_c, tile_table, n_valid, meta, tile_cnt, tile_start = _plan(eid, n_tiles)
    xs = _dispatch(tile_cnt, tile_start, meta, n_valid, lrow_d, h2, n_rows)
    Fh = w_expert_gate.shape[-1]
    ys = _experts(tile_table, n_valid, xs,
                  w_expert_gate[0].reshape(N_EXPERTS, D, Fh),
                  w_expert_up[0].reshape(N_EXPERTS, D, Fh),
                  w_expert_down[0].reshape(N_EXPERTS, Fh, D))
    out = _combine(tile_cnt, tile_start, lrow_c, x1, wts,
                   p[0].reshape(T, -1), g_ple[0][None], w_ple_gate[0], w_ple_proj[0], g_final[None], ys)
    return out.reshape(B, S, D)
```

```python
import functools
import math

import jax
import jax.numpy as jnp
from jax import lax
from jax.experimental import pallas as pl
from jax.experimental.pallas import tpu as pltpu

F32 = jnp.float32
BF16 = jnp.bfloat16
I32 = jnp.int32

EPS = 1e-6
N_HEADS = 8
N_KV_HEADS = 2
HEAD_DIM = 64
WINDOW = 128
ROPE_THETA = 10000.0
CONV_WIDTH = 4
LRU_C = 8.0
N_GROUPS = 4
EXPERTS_PER_GROUP = 8
N_EXPERTS = N_GROUPS * EXPERTS_PER_GROUP
NEG_INF = -1e30
LANES = 128
SUBLANES = 8

TM_INPROJ = 512
TM_PROJ = 512
TQ_ATTN = 512
TL_LRU = 256
TM_PLAN = 1024
TM_EXPERT = 512
EXPERT_FF_BLOCK = 256
TM_DISPATCH = 512
N_DISPATCH_BUFS = 3
N_COMBINE_BUFS = 3
COMBINE_TILES = 1
TM_COMBINE = 256
VMEM_LIMIT = 56 * 1024 * 1024


def _rms(x, g):
    ms = jnp.mean(x * x, axis=-1, keepdims=True)
    return x * lax.rsqrt(ms + EPS) * g


_FETCH_ONCE = dict(pipeline_mode=pl.Buffered(1))


def _cparams(sem):
    return pltpu.CompilerParams(dimension_semantics=sem, vmem_limit_bytes=VMEM_LIMIT)


def _inproj_kernel(x_ref, pos_ref, g_ref, w_ref, q_ref, k_ref, v_ref, lx_ref, lg_ref, w_b):
    @pl.when(pl.program_id(0) == 0)
    def _():
        w_b[...] = w_ref[...].astype(BF16)

    h = _rms(x_ref[...], g_ref[...]).astype(BF16)
    proj = jnp.dot(h, w_b[...], preferred_element_type=F32)
    lane = lax.broadcasted_iota(I32, (1, LANES), 1)
    n_freq = HEAD_DIM // 2
    fidx = lax.broadcasted_iota(I32, (n_freq, 1), 0).astype(F32)
    inv_freq = jnp.exp(fidx * (-2.0 / HEAD_DIM * math.log(ROPE_THETA)))
    pos = pos_ref[0].astype(F32)
    cos_blocks, sin_blocks = [], []
    for c in range(pos.shape[0]):
        ang_t = inv_freq * pos[c:c + 1, :]
        cos_blocks.append(jnp.concatenate([jnp.cos(ang_t)] * (LANES // n_freq), axis=0).T)
        sin_blocks.append(jnp.concatenate([jnp.sin(ang_t)] * (LANES // n_freq), axis=0).T)
    cos = jnp.concatenate(cos_blocks, axis=0)
    sin = jnp.concatenate(sin_blocks, axis=0)
    first_half = (lane % HEAD_DIM) < (HEAD_DIM // 2)
    sin_signed = jnp.where(first_half, -sin, sin)

    def rope(t):
        partner = jnp.where(first_half, pltpu.roll(t, LANES - HEAD_DIM // 2, 1),
                            pltpu.roll(t, HEAD_DIM // 2, 1))
        return t * cos + partner * sin_signed

    attn_w = N_HEADS * HEAD_DIM
    scale = HEAD_DIM ** -0.5
    assert math.frexp(scale)[0] == 0.5, "score scale must be a power of two to fold into q exactly"
    for c in range(attn_w // LANES):
        q_ref[:, c * LANES:(c + 1) * LANES] = (
            rope(proj[:, c * LANES:(c + 1) * LANES]) * scale).astype(BF16)
    kv_w = N_KV_HEADS * HEAD_DIM
    assert kv_w == LANES
    k = rope(proj[:, attn_w:attn_w + kv_w])
    v = proj[:, attn_w + kv_w:attn_w + 2 * kv_w]
    k_ref[:, :kv_w] = k.astype(BF16)
    k_ref[:, kv_w:] = pltpu.roll(k, HEAD_DIM, 1).astype(BF16)
    v_ref[:, :kv_w] = v.astype(BF16)
    v_ref[:, kv_w:] = pltpu.roll(v, HEAD_DIM, 1).astype(BF16)
    off_lx = attn_w + 2 * kv_w
    lru_w = lx_ref.shape[1]
    lx_ref[...] = proj[:, off_lx:off_lx + lru_w]
    lg_ref[...] = proj[:, off_lx + lru_w:off_lx + 2 * lru_w]


def _inproj(x2, pos3, g_mix, w_in, lru_w):
    T, D = x2.shape
    tm = TM_INPROJ
    attn_w = N_HEADS * HEAD_DIM
    kv_w = N_KV_HEADS * HEAD_DIM
    row = lambda i: (i, 0)
    const = lambda i: (0, 0)
    return pl.pallas_call(
        _inproj_kernel,
        grid=(T // tm,),
        in_specs=[pl.BlockSpec((tm, D), row),
                  pl.BlockSpec((1, tm // LANES, LANES), lambda i: (i, 0, 0)),
                  pl.BlockSpec((1, D), const), pl.BlockSpec(w_in.shape, const, **_FETCH_ONCE)],
        out_specs=[pl.BlockSpec((tm, attn_w), row), pl.BlockSpec((tm, 2 * kv_w), row),
                   pl.BlockSpec((tm, 2 * kv_w), row), pl.BlockSpec((tm, lru_w), row),
                   pl.BlockSpec((tm, lru_w), row)],
        out_shape=[jax.ShapeDtypeStruct((T, attn_w), BF16),
                   jax.ShapeDtypeStruct((T, 2 * kv_w), BF16),
                   jax.ShapeDtypeStruct((T, 2 * kv_w), BF16), jax.ShapeDtypeStruct((T, lru_w), F32),
                   jax.ShapeDtypeStruct((T, lru_w), F32)],
        scratch_shapes=[pltpu.VMEM(w_in.shape, BF16)],
        compiler_params=_cparams(("arbitrary",)),
        name="inproj",
    )(x2, pos3, g_mix, w_in)


def _attn_kernel(sinks_ref, q_ref, k_ref, v_ref, kp_ref, vp_ref, g_ref, o_ref):
    n = pl.program_id(1)
    blk = WINDOW
    hd = HEAD_DIM
    nqb = q_ref.shape[0] // blk
    n_pairs = N_HEADS // 2
    pairs_per_group = n_pairs // N_KV_HEADS
    lane = lax.broadcasted_iota(I32, (1, LANES), 1)
    first = lane < hd
    qi = lax.broadcasted_iota(I32, (blk, 2 * blk), 0)
    kj = lax.broadcasted_iota(I32, (blk, 2 * blk), 1)
    rel = qi + blk - kj
    band = (rel >= 0) & (rel < WINDOW)
    bd_row = lax.broadcasted_iota(I32, (4 * blk, LANES), 0)
    bd_lane = lax.broadcasted_iota(I32, (4 * blk, LANES), 1)
    ones_bd = ((bd_row < 2 * blk) == (bd_lane < hd)).astype(BF16)
    zero = jnp.zeros((), BF16)

    def block_diag(t):
        same, swapped = t[:, :LANES], t[:, LANES:]
        out = []
        for g in range(N_KV_HEADS):
            top = jnp.where(first, same if g == 0 else swapped, zero)
            bot = jnp.where(first, zero, swapped if g == 0 else same)
            out.append(jnp.concatenate([top, bot], axis=0))
        return out

    for j in range(nqb):
        rows = slice(j * blk, (j + 1) * blk)
        if j == 0:
            k_prev, v_prev = kp_ref[...], vp_ref[...]
            mask = band & ((kj >= blk) | (n > 0))
        else:
            prev_rows = slice((j - 1) * blk, j * blk)
            k_prev, v_prev = k_ref[prev_rows, :], v_ref[prev_rows, :]
            mask = band
        k_bd = block_diag(jnp.concatenate([k_prev, k_ref[rows, :]], axis=0))
        v_bd = block_diag(jnp.concatenate([v_prev, v_ref[rows, :]], axis=0))
        outs = []
        for p in range(n_pairs):
            g = p // pairs_per_group
            qp = q_ref[rows, p * LANES:(p + 1) * LANES]
            s = lax.dot_general(qp, k_bd[g], (((1,), (1,)), ((), ())), preferred_element_type=F32)
            es, ms = [], []
            for a in range(2):
                sa = jnp.where(mask, s[:, a * 2 * blk:(a + 1) * 2 * blk], NEG_INF)
                m = jnp.maximum(jnp.max(sa, axis=-1, keepdims=True), sinks_ref[2 * p + a])
                es.append(jnp.exp(sa - m))
                ms.append(m)
            e = jnp.concatenate(es, axis=1).astype(BF16)
            od = jnp.dot(e, jnp.concatenate([v_bd[g], ones_bd], axis=1), preferred_element_type=F32)
            sink_term = jnp.where(first, jnp.exp(sinks_ref[2 * p] - ms[0]),
                                  jnp.exp(sinks_ref[2 * p + 1] - ms[1]))
            outs.append(od[:, :LANES] / (od[:, LANES:] + sink_term))
        attn = jnp.concatenate(outs, axis=1)
        o_ref[rows, :] = _rms(attn, g_ref[...]).astype(BF16)


def _attention(q, k, v, sinks, g_attn, B, S):
    T, attn_w = q.shape
    kv_w = k.shape[1]
    blk = WINDOW
    tq = TQ_ATTN
    nt = S // tq
    cur = lambda b, n: (b * nt + n, 0)
    prev = lambda b, n: (b * (S // blk) + jnp.maximum(n * (tq // blk) - 1, 0), 0)
    return pl.pallas_call(
        _attn_kernel,
        grid=(B, nt),
        in_specs=[pl.BlockSpec(memory_space=pltpu.SMEM),
                  pl.BlockSpec((tq, attn_w), cur),
                  pl.BlockSpec((tq, kv_w), cur), pl.BlockSpec((tq, kv_w), cur),
                  pl.BlockSpec((blk, kv_w), prev), pl.BlockSpec((blk, kv_w), prev),
                  pl.BlockSpec((1, attn_w), lambda b, n: (0, 0))],
        out_specs=pl.BlockSpec((tq, attn_w), cur),
        out_shape=jax.ShapeDtypeStruct((T, attn_w), BF16),
        compiler_params=_cparams(("arbitrary", "arbitrary")),
        name="attn",
    )(sinks, q, k, v, k, v, g_attn)


def _neg_expm1(x, exp_x):
    series = -x * (1.0 + x * (0.5 + x * (1.0 / 6.0 + x * (1.0 / 24.0))))
    return jnp.where(x > -0.02, series, 1.0 - exp_x)


def _gelu_tanh(x):
    c = math.sqrt(2.0 / math.pi)
    return x * (0.5 + 0.5 * jnp.tanh(x * (c + (c * 0.044715) * (x * x))))


def _lru_kernel(lx_ref, lg_ref, cw_ref, cb_ref, wa_ref, ba_ref, wx_ref, bx_ref, lam_ref, g_ref,
                o_ref, xpad, a_scr, b_scr, h_scr, h_carry):
    i = pl.program_id(0)
    B, tl, C = lx_ref.shape
    pad = SUBLANES

    @pl.when(i == 0)
    def _():
        xpad[:, 0:pad, :] = jnp.zeros((B, pad, C), F32)
        h_carry[...] = jnp.zeros_like(h_carry)

    @pl.when(i > 0)
    def _():
        xpad[:, 0:pad, :] = xpad[:, tl:tl + pad, :]

    xpad[:, pad:, :] = lx_ref[...]
    groups = (B * tl // SUBLANES, SUBLANES, C)
    sub = lax.broadcasted_iota(I32, groups, 1)
    x_now = lx_ref[...].reshape(groups)
    x_before = xpad[:, pl.ds(0, tl), :].reshape(groups)
    y = x_now * cw_ref[CONV_WIDTH - 1:CONV_WIDTH, :] + cb_ref[...]
    for s in range(1, CONV_WIDTH):
        shifted = pltpu.roll(jnp.where(sub >= SUBLANES - s, x_before, x_now), s, 1)
        y = y + shifted * cw_ref[CONV_WIDTH - 1 - s:CONV_WIDTH - s, :]
    y2 = y.reshape(B * tl, C)
    yb = y2.astype(BF16)
    gate_a = jnp.dot(yb, wa_ref[...], preferred_element_type=F32) + ba_ref[...]
    gate_x = jnp.dot(yb, wx_ref[...], preferred_element_type=F32) + bx_ref[...]
    r = jax.nn.sigmoid(gate_a)
    ig = jax.nn.sigmoid(gate_x)
    nl = -lam_ref[...]
    softplus = jnp.maximum(nl, 0.0) + jnp.log1p(jnp.exp(-jnp.abs(nl)))
    log_a = (-LRU_C) * r * softplus
    a_all = jnp.exp(log_a)
    one_minus_a2 = _neg_expm1(2.0 * log_a, a_all * a_all)
    b_all = one_minus_a2 * lax.rsqrt(jnp.maximum(one_minus_a2, 1e-30)) * ig * y2
    groups = (B * tl // SUBLANES, SUBLANES, C)
    sub = lax.broadcasted_iota(I32, groups, 1)
    a_cum, b_loc = a_all.reshape(groups), b_all.reshape(groups)
    d = 1
    while d < SUBLANES:
        keep = sub >= d
        a_prev = pltpu.roll(a_cum, d, 1)
        b_prev = pltpu.roll(b_loc, d, 1)
        b_loc = b_loc + jnp.where(keep, a_cum * b_prev, 0.0)
        a_cum = jnp.where(keep, a_cum * a_prev, a_cum)
        d *= 2
    a_scr[...] = a_cum.reshape(B, tl, C)
    b_scr[...] = b_loc.reshape(B, tl, C)

    def carry(g, h_prev):
        rows = pl.ds(pl.multiple_of(g * SUBLANES, SUBLANES), SUBLANES)
        h = b_scr[:, rows, :] + a_scr[:, rows, :] * h_prev
        h_scr[:, rows, :] = h
        return h[:, SUBLANES - 1:SUBLANES, :]

    h_carry[...] = lax.fori_loop(0, tl // SUBLANES, carry, h_carry[...], unroll=4)
    lru = h_scr[...].reshape(B * tl, C) * _gelu_tanh(lg_ref[...].reshape(B * tl, C))
    o_ref[...] = _rms(lru, g_ref[...]).astype(BF16).reshape(B, tl, C)


def _lru(lx3, lg3, conv_w, conv_b, wa_bd, ba, wx_bd, bx, lam, g_lru):
    B, S, C = lx3.shape
    tl = TL_LRU
    blk = lambda i: (0, i, 0)
    const = lambda i: (0, 0)
    vec = pl.BlockSpec((1, C), const)
    return pl.pallas_call(
        _lru_kernel,
        grid=(S // tl,),
        in_specs=[pl.BlockSpec((B, tl, C), blk), pl.BlockSpec((B, tl, C), blk),
                  pl.BlockSpec((CONV_WIDTH, C), const), vec,
                  pl.BlockSpec((C, C), const), vec, pl.BlockSpec((C, C), const), vec, vec, vec],
        out_specs=pl.BlockSpec((B, tl, C), blk),
        out_shape=jax.ShapeDtypeStruct((B, S, C), BF16),
        scratch_shapes=[pltpu.VMEM((B, tl + SUBLANES, C), F32),
                        pltpu.VMEM((B, tl, C), F32), pltpu.VMEM((B, tl, C), F32),
                        pltpu.VMEM((B, tl, C), F32), pltpu.VMEM((B, 1, C), F32)],
        compiler_params=_cparams(("arbitrary",)),
        name="lru",
    )(lx3, lg3, conv_w, conv_b, wa_bd, ba, wx_bd, bx, lam, g_lru)


def _outproj_kernel(x_ref, ma_ref, mb_ref, wo_ref, g_ref, wr_ref, br_ref,
                    x1_ref, h2_ref, eid_ref, wt_ref, wo_b):
    @pl.when(pl.program_id(0) == 0)
    def _():
        wo_b[...] = wo_ref[...].astype(BF16)

    half = ma_ref.shape[1]
    x1 = (x_ref[...]
          + jnp.dot(ma_ref[...], wo_b[0:half, :], preferred_element_type=F32)
          + jnp.dot(mb_ref[...], wo_b[half:, :], preferred_element_type=F32))
    x1_ref[...] = x1
    h2 = _rms(x1, g_ref[...])
    h2_ref[...] = h2.astype(BF16).reshape(h2_ref.shape)
    h_hi = h2.astype(BF16)
    h_lo = (h2 - h_hi.astype(F32)).astype(BF16)
    hw = jnp.dot(h_hi, wr_ref[...], preferred_element_type=F32)
    logits = (hw[:, :LANES] + hw[:, LANES:]
              + jnp.dot(h_lo, wr_ref[:, :LANES], preferred_element_type=F32) + br_ref[...])
    tm = logits.shape[0]
    lane = lax.broadcasted_iota(I32, (tm, LANES), 1)
    lane_f = lane.astype(F32)
    big = float(LANES)

    def first_argmax(vals):
        m = jnp.max(vals, axis=-1, keepdims=True)
        idx = jnp.min(jnp.where(vals == m, lane_f, big), axis=-1, keepdims=True)
        return m, idx.astype(I32)

    gl = jnp.where(lane < N_GROUPS, logits, NEG_INF)
    gmax, gidx = first_argmax(gl)
    gsum = jnp.sum(jnp.where(lane < N_GROUPS, jnp.exp(gl - gmax), 0.0), axis=-1, keepdims=True)
    g_top_p = 1.0 / gsum
    lo = N_GROUPS + EXPERTS_PER_GROUP * gidx
    el = jnp.where((lane >= lo) & (lane < lo + EXPERTS_PER_GROUP), logits, NEG_INF)
    m1, i1 = first_argmax(el)
    el2 = jnp.where(lane == i1, NEG_INF, el)
    m2, i2 = first_argmax(el2)
    ratio = jnp.exp(m2 - m1)
    w1 = g_top_p / (1.0 + ratio)
    w2 = g_top_p * ratio / (1.0 + ratio)
    eid_ref[...] = jnp.where(lane == 0, i1 - N_GROUPS, jnp.where(lane == 1, i2 - N_GROUPS, 0))
    wt_ref[...] = jnp.where(lane == 0, w1, jnp.where(lane == 1, w2, 0.0))


def _outproj(x2, mix_a, mix_b, w_out, g_ffn, w_router, b_router):
    T, D = x2.shape
    tm = TM_PROJ
    half = mix_a.shape[1]
    row = lambda i: (i, 0)
    const = lambda i: (0, 0)
    return pl.pallas_call(
        _outproj_kernel,
        grid=(T // tm,),
        in_specs=[pl.BlockSpec((tm, D), row), pl.BlockSpec((tm, half), row),
                  pl.BlockSpec((tm, half), row), pl.BlockSpec(w_out.shape, const, **_FETCH_ONCE),
                  pl.BlockSpec((1, D), const), pl.BlockSpec((D, 2 * LANES), const),
                  pl.BlockSpec((1, LANES), const)],
        out_specs=[pl.BlockSpec((tm, D), row),
                   pl.BlockSpec((tm, D // LANES, LANES), lambda i: (i, 0, 0)),
                   pl.BlockSpec((tm, LANES), row), pl.BlockSpec((tm, LANES), row)],
        out_shape=[jax.ShapeDtypeStruct((T, D), F32),
                   jax.ShapeDtypeStruct((T, D // LANES, LANES), BF16),
                   jax.ShapeDtypeStruct((T, LANES), I32), jax.ShapeDtypeStruct((T, LANES), F32)],
        scratch_shapes=[pltpu.VMEM(w_out.shape, BF16)],
        compiler_params=_cparams(("arbitrary",)),
        name="outproj",
    )(x2, mix_a, mix_b, w_out, g_ffn, w_router, b_router)


def _lane_cumsum(x):
    lane = lax.broadcasted_iota(I32, x.shape, 1)
    shift = 1
    while shift < LANES:
        x = x + jnp.where(lane >= shift, pltpu.roll(x, shift, 1), 0)
        shift *= 2
    return x


def _plan_kernel(eid_ref, lrow_d_ref, lrow_c_ref, te_ref, nv_ref, meta_ref, tcnt_ref, tstart_ref,
                 cnt_scr, tcnt_scr, tpre_scr):
    i = pl.program_id(0)
    tm = eid_ref.shape[0]
    tc = TM_COMBINE
    group = TM_DISPATCH // TM_COMBINE
    nsub = tm // tc
    lane = lax.broadcasted_iota(I32, (tm, LANES), 1)
    eid = eid_ref[...]
    sel1 = lane == eid[:, 0:1]
    sel2 = lane == eid[:, 1:2]

    @pl.when(i == 0)
    def _():
        cnt_scr[...] = jnp.zeros_like(cnt_scr)

    onehot = (sel1 | sel2).astype(BF16)
    r = lax.broadcasted_iota(I32, (tm, tm), 0)
    c = lax.broadcasted_iota(I32, (tm, tm), 1)
    strict_lower = (r > c).astype(BF16)
    within = jnp.dot(strict_lower, onehot, preferred_element_type=F32)
    base = cnt_scr[0:1, :]
    onehot_f = onehot.astype(F32)
    row = lax.broadcasted_iota(I32, (tm, LANES), 0)
    srow = lax.broadcasted_iota(I32, (2 * SUBLANES, LANES), 0)
    counts = [jnp.sum(onehot_f[s * tc:(s + 1) * tc], axis=0, keepdims=True) for s in range(nsub)]
    cnt_mat = jnp.zeros((2 * SUBLANES, LANES), F32)
    seen = []
    total = jnp.zeros((1, LANES), F32)
    for s in range(nsub):
        seen.append(total)
        tile = i * nsub + s
        tcnt_scr[pl.ds(tile, 1), :] = counts[s]
        tpre_scr[pl.ds(tile, 1), :] = base + total
        cnt_mat = jnp.where(srow == s, counts[s], cnt_mat)
        cnt_mat = jnp.where(srow == SUBLANES + s // group, cnt_mat + counts[s], cnt_mat)
        total = total + counts[s]
    cnt_i = cnt_mat.astype(I32)
    run_start = (_lane_cumsum(cnt_i) - cnt_i).astype(F32)

    def local_rows(first_tile_of, table_row_of):
        local = None
        for s in range(nsub):
            val = within - seen[first_tile_of(s)] + run_start[table_row_of(s):table_row_of(s) + 1, :]
            local = val if local is None else jnp.where(row >= s * tc, val, local)
        l1 = jnp.sum(jnp.where(sel1, local, 0.0), axis=-1, keepdims=True)
        l2 = jnp.sum(jnp.where(sel2, local, 0.0), axis=-1, keepdims=True)
        return jnp.where(lane == 0, l1, jnp.where(lane == 1, l2, 0.0)).astype(I32)

    lrow_c_ref[...] = local_rows(lambda s: s, lambda s: s)
    lrow_d_ref[...] = local_rows(lambda s: (s // group) * group, lambda s: SUBLANES + s // group)
    cnt_scr[...] = cnt_scr[...] + total

    @pl.when(i == pl.num_programs(0) - 1)
    def _():
        shift = int(math.log2(TM_EXPERT))
        cnt = cnt_scr[...].astype(I32)
        padded = ((cnt + (TM_EXPERT - 1)) >> shift) << shift
        ends = _lane_cumsum(padded)
        nt = te_ref.shape[0]
        end_tile = (ends >> shift)[0:1, :]
        tl_lane = lax.broadcasted_iota(I32, (nt, LANES), 1)
        tile = lax.broadcasted_iota(I32, (nt, LANES), 0)
        done = jnp.where((tl_lane < N_EXPERTS) & (end_tile <= tile), 1.0, 0.0)
        te = jnp.minimum(jnp.sum(done, axis=-1, keepdims=True).astype(I32), N_EXPERTS - 1)
        n_valid = jnp.sum(jnp.where(tl_lane[0:1, :] == N_EXPERTS - 1, end_tile, 0).astype(F32),
                          axis=-1, keepdims=True).astype(I32)
        nv_ref[...] = jnp.broadcast_to(n_valid, nv_ref.shape)
        seg_end = jnp.sum(jnp.where(tl_lane == te, end_tile, 0).astype(F32), axis=-1,
                          keepdims=True).astype(I32)
        nxt = jnp.sum(jnp.where((tl_lane < N_EXPERTS) & (end_tile <= seg_end), 1.0, 0.0), axis=-1,
                      keepdims=True).astype(I32)
        nxt = jnp.where(seg_end < n_valid, jnp.minimum(nxt, N_EXPERTS - 1), -1)
        te_ref[...] = jnp.where(tl_lane == 1, nxt, te)
        row = lax.broadcasted_iota(I32, cnt.shape, 0)
        meta_ref[...] = jnp.where(row == 0, ends - padded + cnt, jnp.where(row == 1, padded - cnt, 0))
        tcnt_ref[...] = tcnt_scr[...].astype(I32)
        tstart_ref[...] = (tpre_scr[...] + (ends - padded).astype(F32)[0:1, :]).astype(I32)


def _plan(eid, n_tiles):
    T = eid.shape[0]
    tm = TM_PLAN
    steps = T // tm
    n_ctiles = T // TM_COMBINE
    assert TM_DISPATCH % TM_COMBINE == 0 and tm % TM_DISPATCH == 0 and tm // TM_COMBINE <= SUBLANES
    row = lambda i: (i, 0)
    const = lambda i: (0, 0)
    return pl.pallas_call(
        _plan_kernel,
        grid=(steps,),
        in_specs=[pl.BlockSpec((tm, LANES), row)],
        out_specs=[pl.BlockSpec((tm, LANES), row), pl.BlockSpec((tm, LANES), row),
                   pl.BlockSpec((n_tiles, LANES), const), pl.BlockSpec((SUBLANES, LANES), const),
                   pl.BlockSpec((SUBLANES, LANES), const),
                   pl.BlockSpec((n_ctiles, LANES), const), pl.BlockSpec((n_ctiles, LANES), const)],
        out_shape=[jax.ShapeDtypeStruct((T, LANES), I32), jax.ShapeDtypeStruct((T, LANES), I32),
                   jax.ShapeDtypeStruct((n_tiles, LANES), I32),
                   jax.ShapeDtypeStruct((SUBLANES, LANES), I32),
                   jax.ShapeDtypeStruct((SUBLANES, LANES), I32),
                   jax.ShapeDtypeStruct((n_ctiles, LANES), I32),
                   jax.ShapeDtypeStruct((n_ctiles, LANES), I32)],
        scratch_shapes=[pltpu.VMEM((SUBLANES, LANES), F32),
                        pltpu.VMEM((n_ctiles, LANES), F32), pltpu.VMEM((n_ctiles, LANES), F32)],
        compiler_params=_cparams(("arbitrary",)),
        name="plan",
    )(eid)


_PAD_BITS = tuple(1 << b for b in reversed(range(int(math.log2(TM_EXPERT)))))


def _run_bits(max_rows):
    return tuple(1 << b for b in reversed(range(int(math.log2(max_rows)) + 1)))


def _dispatch_kernel(tcnt_ref, tstart_ref, meta_ref, nv_ref, lrow_ref, h_ref,
                     xs_ref, lbuf_0, lbuf_1, lbuf_2, zeros, sem, zsem):
    i = pl.program_id(0)
    n = pl.num_programs(0)
    td = h_ref.shape[0]
    nr = 2 * td
    bufs = (lbuf_0, lbuf_1, lbuf_2)

    def wait_buf(s):
        pltpu.make_async_copy(bufs[s], xs_ref.at[pl.ds(0, nr)], sem.at[s]).wait()

    def step(cur_s):
        prev_s = (cur_s - 1) % N_DISPATCH_BUFS
        cur, prev = bufs[cur_s], bufs[prev_s]

        @pl.when(i >= N_DISPATCH_BUFS)
        def _():
            wait_buf(cur_s)

        local_row = lrow_ref[...].astype(F32).T
        r = lax.broadcasted_iota(I32, (nr, td), 0).astype(F32)
        onehot = ((r == local_row[0:1, :]) | (r == local_row[1:2, :])).astype(BF16)
        sorted_rows = jnp.dot(onehot, h_ref[...].reshape(td, -1), preferred_element_type=F32)
        cur[...] = sorted_rows.astype(BF16).reshape(cur.shape)

        group = TM_DISPATCH // TM_COMBINE
        first = jnp.maximum(i - 1, 0) * group
        lo = 0
        for e in range(N_EXPERTS):
            cnt = sum(tcnt_ref[first + u, e] for u in range(group))
            cnt = jnp.where(i >= 1, cnt, 0)
            dst = tstart_ref[first, e]
            for bit in _run_bits(td):
                off = cnt & ~(2 * bit - 1)

                @pl.when((cnt & bit) != 0)
                def _():
                    pltpu.make_async_copy(prev.at[pl.ds(lo + off, bit)],
                                          xs_ref.at[pl.ds(dst + off, bit)], sem.at[prev_s]).start()
            lo = lo + cnt

        @pl.when(i == n - 1)
        def _():
            wait_buf(prev_s)
            if N_DISPATCH_BUFS > 2:
                @pl.when(n >= 3)
                def _():
                    wait_buf((cur_s - 2) % N_DISPATCH_BUFS)

    n_tiles = xs_ref.shape[0] // TM_EXPERT
    first_tail = n_tiles - N_EXPERTS

    def pad_copy(e, bit):
        cnt = meta_ref[1, e]
        dst = meta_ref[0, e] + (cnt & ~(2 * bit - 1))
        return (cnt & bit) != 0, pltpu.make_async_copy(zeros.at[pl.ds(0, bit)],
                                                       xs_ref.at[pl.ds(dst, bit)], zsem)

    def tail_copy(j):
        return j >= nv_ref[0, 0], pltpu.make_async_copy(
            zeros, xs_ref.at[pl.ds(j * TM_EXPERT, TM_EXPERT)], zsem)

    def for_each_fill(act):
        def per_expert(e, c):
            for bit in _PAD_BITS:
                pred, cp = pad_copy(e, bit)
                pl.when(pred)(functools.partial(act, cp))
            return c

        lax.fori_loop(0, N_EXPERTS, per_expert, 0)

        def per_tile(j, c):
            pred, cp = tail_copy(j)
            pl.when(pred)(functools.partial(act, cp))
            return c

        lax.fori_loop(first_tail, n_tiles, per_tile, 0)

    @pl.when(i == 0)
    def _():
        zeros[...] = jnp.zeros_like(zeros)
        for_each_fill(lambda cp: cp.start())

    phase = lax.rem(i, N_DISPATCH_BUFS)
    for s in range(N_DISPATCH_BUFS):
        pl.when(phase == s)(functools.partial(step, s))

    @pl.when(i == n - 1)
    def _():
        for_each_fill(lambda cp: cp.wait())


def _dispatch(tile_cnt, tile_start, meta, n_valid, lrow, h2, n_rows):
    T = h2.shape[0]
    slab = h2.shape[1:]
    td = TM_DISPATCH
    nt = T // td
    smem = pl.BlockSpec(memory_space=pltpu.SMEM)
    return pl.pallas_call(
        _dispatch_kernel,
        grid=(nt + 1,),
        in_specs=[smem, smem, smem, smem,
                  pl.BlockSpec((td, LANES), lambda i: (jnp.minimum(i, nt - 1), 0)),
                  pl.BlockSpec((td,) + slab, lambda i: (jnp.minimum(i, nt - 1), 0, 0))],
        out_specs=pl.BlockSpec(memory_space=pl.ANY),
        out_shape=jax.ShapeDtypeStruct((n_rows,) + slab, h2.dtype),
        scratch_shapes=[pltpu.VMEM((2 * td,) + slab, h2.dtype)] * N_DISPATCH_BUFS + [
                        pltpu.VMEM((TM_EXPERT,) + slab, h2.dtype),
                        pltpu.SemaphoreType.DMA((N_DISPATCH_BUFS,)), pltpu.SemaphoreType.DMA(())],
        compiler_params=_cparams(("arbitrary",)),
        name="dispatch",
    )(tile_cnt, tile_start, meta, n_valid, lrow, h2)


def _experts_kernel(te_ref, nv_ref, xs_ref, wg_ref, wu_ref, wd_ref, ys_ref,
                    wg_f, wu_f, wd_f, wg_b, wu_b, wd_b, slot_ref, sem):
    j = pl.program_id(0)
    valid = j < nv_ref[0, 0]
    new_expert = valid & ((j == 0) | (te_ref[j, 0] != te_ref[jnp.maximum(j - 1, 0), 0]))

    def fetch(e, s):
        return (pltpu.make_async_copy(wg_ref.at[e], wg_f.at[s], sem.at[s]),
                pltpu.make_async_copy(wu_ref.at[e], wu_f.at[s], sem.at[s]),
                pltpu.make_async_copy(wd_ref.at[e], wd_f.at[s], sem.at[s]))

    @pl.when(j == 0)
    def _():
        slot_ref[0] = 0
        for cp in fetch(te_ref[0, 0], 0):
            cp.start()

    @pl.when(new_expert)
    def _():
        s = slot_ref[0]
        for cp in fetch(te_ref[j, 0], s):
            cp.wait()
        nxt = te_ref[j, 1]

        @pl.when(nxt >= 0)
        def _():
            for cp in fetch(nxt, 1 - s):
                cp.start()

        wg_b[...] = wg_f[s].astype(BF16)
        wu_b[...] = wu_f[s].astype(BF16)
        wd_b[...] = wd_f[s].astype(BF16)
        slot_ref[0] = 1 - s

    @pl.when(valid)
    def _():
        x = xs_ref[...].reshape(xs_ref.shape[0], -1)
        y = None
        for lo in range(0, wg_b.shape[1], EXPERT_FF_BLOCK):
            cols = slice(lo, lo + EXPERT_FF_BLOCK)
            a = jnp.dot(x, wg_b[:, cols], preferred_element_type=F32)
            u = jnp.dot(x, wu_b[:, cols], preferred_element_type=F32)
            hid = (a * jax.nn.sigmoid(a) * u).astype(BF16)
            part = jnp.dot(hid, wd_b[cols, :], preferred_element_type=F32)
            y = part if y is None else y + part
        ys_ref[...] = y.astype(BF16).reshape(ys_ref.shape)

    @pl.when(j >= nv_ref[0, 0])
    def _():
        ys_ref[...] = jnp.zeros_like(ys_ref)


def _experts(tile_table, n_valid, xs, wg, wu, wd):
    R = xs.shape[0]
    slab = xs.shape[1:]
    E, D, Fh = wg.shape
    tm = TM_EXPERT
    n_tiles = R // tm
    rows = lambda j, te, nv: (j, 0, 0)
    rows_in = lambda j, te, nv: (jnp.minimum(j, nv[0, 0] - 1), 0, 0)
    hbm = pl.BlockSpec(memory_space=pl.ANY)
    return pl.pallas_call(
        _experts_kernel,
        grid_spec=pltpu.PrefetchScalarGridSpec(
            num_scalar_prefetch=2,
            grid=(n_tiles,),
            in_specs=[pl.BlockSpec((tm,) + slab, rows_in), hbm, hbm, hbm],
            out_specs=pl.BlockSpec((tm,) + slab, rows),
            scratch_shapes=[pltpu.VMEM((2, D, Fh), F32), pltpu.VMEM((2, D, Fh), F32),
                            pltpu.VMEM((2, Fh, D), F32),
                            pltpu.VMEM((D, Fh), BF16), pltpu.VMEM((D, Fh), BF16),
                            pltpu.VMEM((Fh, D), BF16),
                            pltpu.SMEM((1,), I32), pltpu.SemaphoreType.DMA((2,))]),
        out_shape=jax.ShapeDtypeStruct(xs.shape, xs.dtype),
        compiler_params=_cparams(("arbitrary",)),
        name="experts",
    )(tile_table, n_valid, xs, wg, wu, wd)


def _combine_kernel(tcnt_ref, tstart_ref, lrow_ref, x1_ref, wt_ref, p_ref, gp_ref, wpg_ref, wpp_ref,
                    gf_ref, ys_ref, o_ref, *scratch):
    g = COMBINE_TILES
    nb = N_COMBINE_BUFS
    bufs = tuple(scratch[k * g:(k + 1) * g] for k in range(nb))
    wpg_b, wpp_b, sem = scratch[nb * g:]
    i = pl.program_id(0)
    n = pl.num_programs(0)
    tc = x1_ref.shape[0] // g
    nr = 2 * tc

    def request(tile, live, buf, s):
        lo = 0
        for e in range(N_EXPERTS):
            cnt = jnp.where(live, tcnt_ref[tile, e], 0)
            src = tstart_ref[tile, e]
            for bit in _run_bits(tc):
                off = cnt & ~(2 * bit - 1)

                @pl.when((cnt & bit) != 0)
                def _():
                    pltpu.make_async_copy(ys_ref.at[pl.ds(src + off, bit)],
                                          buf.at[pl.ds(lo + off, bit)], s).start()
            lo = lo + cnt

    ahead = nb - 1

    @pl.when(i == 0)
    def _():
        for k in range(ahead):
            for u in range(g):
                request(jnp.minimum(k, n - 1) * g + u, k < n, bufs[k][u], sem.at[k, u])
        wpg_b[...] = wpg_ref[...].astype(BF16)
        wpp_b[...] = wpp_ref[...].astype(BF16)

    def step(slot):
        cur, far = bufs[slot], bufs[(slot + ahead) % nb]
        for u in range(g):
            pltpu.make_async_copy(ys_ref.at[pl.ds(0, nr)], cur[u], sem.at[slot, u]).wait()
        for u in range(g):
            request(jnp.minimum(i + ahead, n - 1) * g + u, i + ahead < n, far[u],
                    sem.at[(slot + ahead) % nb, u])
        r = lax.broadcasted_iota(I32, (tc, nr), 1)
        for u in range(g):
            tok = slice(u * tc, (u + 1) * tc)
            rows = cur[u][...].reshape(nr, -1)
            local_row = lrow_ref[tok, :]
            y0 = jnp.dot((r == local_row[:, 0:1]).astype(BF16), rows, preferred_element_type=F32)
            y1 = jnp.dot((r == local_row[:, 1:2]).astype(BF16), rows, preferred_element_type=F32)
            wt = wt_ref[tok, :]
            x2 = x1_ref[tok, :] + wt[:, 0:1] * y0 + wt[:, 1:2] * y1
            hp = _rms(x2, gp_ref[...]).astype(BF16)
            gate = jax.nn.sigmoid(jnp.dot(hp, wpg_b[...], preferred_element_type=F32))
            proj = jnp.dot(p_ref[tok, :].astype(BF16), wpp_b[...], preferred_element_type=F32)
            x3 = x2 + gate * proj
            o_ref[tok, :] = _rms(x3, gf_ref[...])

    phase = lax.rem(i, nb)
    for slot in range(nb):
        pl.when(phase == slot)(functools.partial(step, slot))


def _combine(tile_cnt, tile_start, lrow, x1, wts, p2, g_ple, w_ple_gate, w_ple_proj, g_final, ys):
    T, D = x1.shape
    P = p2.shape[1]
    g = COMBINE_TILES
    tm = g * TM_COMBINE
    n = T // tm
    row = lambda i: (i, 0)
    const = lambda i: (0, 0)
    smem = pl.BlockSpec(memory_space=pltpu.SMEM)
    run_buf = pltpu.VMEM((2 * TM_COMBINE,) + ys.shape[1:], ys.dtype)
    return pl.pallas_call(
        _combine_kernel,
        grid=(n,),
        in_specs=[smem, smem, pl.BlockSpec((tm, LANES), row),
                  pl.BlockSpec((tm, D), row), pl.BlockSpec((tm, LANES), row),
                  pl.BlockSpec((tm, P), row), pl.BlockSpec((1, D), const),
                  pl.BlockSpec((D, D), const, **_FETCH_ONCE), pl.BlockSpec((P, D), const, **_FETCH_ONCE),
                  pl.BlockSpec((1, D), const), pl.BlockSpec(memory_space=pl.ANY)],
        out_specs=pl.BlockSpec((tm, D), row),
        out_shape=jax.ShapeDtypeStruct((T, D), F32),
        scratch_shapes=[run_buf] * (N_COMBINE_BUFS * g) + [
                        pltpu.VMEM((D, D), BF16), pltpu.VMEM((P, D), BF16),
                        pltpu.SemaphoreType.DMA((N_COMBINE_BUFS, g))],
        compiler_params=_cparams(("arbitrary",)),
        name="combine",
    )(tile_cnt, tile_start, lrow, x1, wts, p2, g_ple, w_ple_gate, w_ple_proj, g_final, ys)


def _block_diag(w):
    nb, d, _ = w.shape
    eye = jnp.eye(nb, dtype=w.dtype)
    return (eye[:, None, :, None] * w[:, :, None, :]).reshape(nb * d, nb * d)


def kernel(x, p, positions, g_mix, w_in, sinks, conv_w, conv_b, lru_wa, lru_ba, lru_wx, lru_bx,
           lru_lambda, g_attn_out, g_lru_out, w_out, g_ffn, w_router_group, b_router_group,
           w_router_expert, b_router_expert, w_expert_gate, w_expert_up, w_expert_down, g_ple,
           w_ple_gate, w_ple_proj, g_final):
    B, S, D = x.shape
    assert w_in.shape[0] == 1, "single-layer block only"
    T = B * S
    lru_w = conv_w.shape[-1]
    n_rows = 2 * T + N_EXPERTS * TM_EXPERT
    n_tiles = n_rows // TM_EXPERT
    pos3 = positions.reshape(T // TM_INPROJ, TM_INPROJ // LANES, LANES).astype(I32)
    x2 = x.reshape(T, D)
    q, k, v, lx, lg = _inproj(x2, pos3, g_mix[0][None], w_in[0], lru_w)
    mix_a = _attention(q, k, v, sinks[0], g_attn_out[0][None], B, S)
    mix_b = _lru(lx.reshape(B, S, lru_w), lg.reshape(B, S, lru_w), conv_w[0], conv_b[0][None],
                 _block_diag(lru_wa[0]).astype(BF16), lru_ba[0][None],
                 _block_diag(lru_wx[0]).astype(BF16), lru_bx[0][None],
                 lru_lambda[0][None], g_lru_out[0][None]).reshape(T, lru_w)
    n_router = N_GROUPS + N_EXPERTS
    w_router = jnp.pad(jnp.concatenate([w_router_group[0], w_router_expert[0]], axis=1),
                       ((0, 0), (0, LANES - n_router)))
    b_router = jnp.pad(jnp.concatenate([b_router_group[0], b_router_expert[0]]),
                       (0, LANES - n_router))[None]
    w_router_hi = w_router.astype(BF16)
    w_router_lo = (w_router - w_router_hi.astype(F32)).astype(BF16)
    x1, h2, eid, wts = _outproj(x2, mix_a, mix_b, w_out[0], g_ffn[0][None],
                                jnp.concatenate([w_router_hi, w_router_lo], axis=1), b_router)
    lrow_d, lrow_c, tile_table, n_valid, meta, tile_cnt, tile_start = _plan(eid, n_tiles)
    xs = _dispatch(tile_cnt, tile_start, meta, n_valid, lrow_d, h2, n_rows)
    Fh = w_expert_gate.shape[-1]
    ys = _experts(tile_table, n_valid, xs,
                  w_expert_gate[0].reshape(N_EXPERTS, D, Fh),
                  w_expert_up[0].reshape(N_EXPERTS, D, Fh),
                  w_expert_down[0].reshape(N_EXPERTS, Fh, D))
    out = _combine(tile_cnt, tile_start, lrow_c, x1, wts,
                   p[0].reshape(T, -1), g_ple[0][None], w_ple_gate[0], w_ple_proj[0], g_final[None], ys)
    return out.reshape(B, S, D)
```

```python
import functools
import math

import jax
import jax.numpy as jnp
from jax import lax
from jax.experimental import pallas as pl
from jax.experimental.pallas import tpu as pltpu

F32 = jnp.float32
BF16 = jnp.bfloat16
I32 = jnp.int32

EPS = 1e-6
N_HEADS = 8
N_KV_HEADS = 2
HEAD_DIM = 64
WINDOW = 128
ROPE_THETA = 10000.0
CONV_WIDTH = 4
LRU_C = 8.0
N_GROUPS = 4
EXPERTS_PER_GROUP = 8
N_EXPERTS = N_GROUPS * EXPERTS_PER_GROUP
NEG_INF = -1e30
LANES = 128
SUBLANES = 8

TM_INPROJ = 512
TM_PROJ = 512
TQ_ATTN = 512
TL_LRU = 256
TM_PLAN = 1024
TM_EXPERT = 512
EXPERT_FF_BLOCK = 256
EXPERT_ROW_BLOCK = 128
TM_DISPATCH = 512
N_DISPATCH_BUFS = 3
N_COMBINE_BUFS = 3
COMBINE_TILES = 1
COMBINE_LONG_COPY = 32
TM_COMBINE = 256
VMEM_LIMIT = 56 * 1024 * 1024


def _rms(x, g):
    ms = jnp.mean(x * x, axis=-1, keepdims=True)
    return x * lax.rsqrt(ms + EPS) * g


_FETCH_ONCE = dict(pipeline_mode=pl.Buffered(1))


def _cparams(sem):
    return pltpu.CompilerParams(dimension_semantics=sem, vmem_limit_bytes=VMEM_LIMIT)


def _inproj_kernel(x_ref, pos_ref, g_ref, w_ref, q_ref, k_ref, v_ref, lx_ref, lg_ref, w_b):
    @pl.when(pl.program_id(0) == 0)
    def _():
        w_b[...] = w_ref[...].astype(BF16)

    h = _rms(x_ref[...], g_ref[...]).astype(BF16)
    proj = jnp.dot(h, w_b[...], preferred_element_type=F32)
    lane = lax.broadcasted_iota(I32, (1, LANES), 1)
    n_freq = HEAD_DIM // 2
    fidx = lax.broadcasted_iota(I32, (n_freq, 1), 0).astype(F32)
    inv_freq = jnp.exp(fidx * (-2.0 / HEAD_DIM * math.log(ROPE_THETA)))
    pos = pos_ref[0].astype(F32)
    cos_blocks, sin_blocks = [], []
    for c in range(pos.shape[0]):
        ang_t = inv_freq * pos[c:c + 1, :]
        cos_blocks.append(jnp.concatenate([jnp.cos(ang_t)] * (LANES // n_freq), axis=0).T)
        sin_blocks.append(jnp.concatenate([jnp.sin(ang_t)] * (LANES // n_freq), axis=0).T)
    cos = jnp.concatenate(cos_blocks, axis=0)
    sin = jnp.concatenate(sin_blocks, axis=0)
    first_half = (lane % HEAD_DIM) < (HEAD_DIM // 2)
    sin_signed = jnp.where(first_half, -sin, sin)

    def rope(t):
        partner = jnp.where(first_half, pltpu.roll(t, LANES - HEAD_DIM // 2, 1),
                            pltpu.roll(t, HEAD_DIM // 2, 1))
        return t * cos + partner * sin_signed

    attn_w = N_HEADS * HEAD_DIM
    scale = HEAD_DIM ** -0.5
    assert math.frexp(scale)[0] == 0.5, "score scale must be a power of two to fold into q exactly"
    for c in range(attn_w // LANES):
        q_ref[:, c * LANES:(c + 1) * LANES] = (
            rope(proj[:, c * LANES:(c + 1) * LANES]) * scale).astype(BF16)
    kv_w = N_KV_HEADS * HEAD_DIM
    assert kv_w == LANES
    k = rope(proj[:, attn_w:attn_w + kv_w])
    v = proj[:, attn_w + kv_w:attn_w + 2 * kv_w]
    k_ref[:, :kv_w] = k.astype(BF16)
    k_ref[:, kv_w:] = pltpu.roll(k, HEAD_DIM, 1).astype(BF16)
    v_ref[:, :kv_w] = v.astype(BF16)
    v_ref[:, kv_w:] = pltpu.roll(v, HEAD_DIM, 1).astype(BF16)
    off_lx = attn_w + 2 * kv_w
    lru_w = lx_ref.shape[1]
    lx_ref[...] = proj[:, off_lx:off_lx + lru_w]
    lg_ref[...] = proj[:, off_lx + lru_w:off_lx + 2 * lru_w]


def _inproj(x2, pos3, g_mix, w_in, lru_w):
    T, D = x2.shape
    tm = TM_INPROJ
    attn_w = N_HEADS * HEAD_DIM
    kv_w = N_KV_HEADS * HEAD_DIM
    row = lambda i: (i, 0)
    const = lambda i: (0, 0)
    return pl.pallas_call(
        _inproj_kernel,
        grid=(T // tm,),
        in_specs=[pl.BlockSpec((tm, D), row),
                  pl.BlockSpec((1, tm // LANES, LANES), lambda i: (i, 0, 0)),
                  pl.BlockSpec((1, D), const), pl.BlockSpec(w_in.shape, const, **_FETCH_ONCE)],
        out_specs=[pl.BlockSpec((tm, attn_w), row), pl.BlockSpec((tm, 2 * kv_w), row),
                   pl.BlockSpec((tm, 2 * kv_w), row), pl.BlockSpec((tm, lru_w), row),
                   pl.BlockSpec((tm, lru_w), row)],
        out_shape=[jax.ShapeDtypeStruct((T, attn_w), BF16),
                   jax.ShapeDtypeStruct((T, 2 * kv_w), BF16),
                   jax.ShapeDtypeStruct((T, 2 * kv_w), BF16), jax.ShapeDtypeStruct((T, lru_w), F32),
                   jax.ShapeDtypeStruct((T, lru_w), F32)],
        scratch_shapes=[pltpu.VMEM(w_in.shape, BF16)],
        compiler_params=_cparams(("arbitrary",)),
        name="inproj",
    )(x2, pos3, g_mix, w_in)


def _attn_kernel(sinks_ref, q_ref, k_ref, v_ref, kp_ref, vp_ref, g_ref, o_ref):
    n = pl.program_id(1)
    blk = WINDOW
    hd = HEAD_DIM
    nqb = q_ref.shape[0] // blk
    n_pairs = N_HEADS // 2
    pairs_per_group = n_pairs // N_KV_HEADS
    lane = lax.broadcasted_iota(I32, (1, LANES), 1)
    first = lane < hd
    qi = lax.broadcasted_iota(I32, (blk, 2 * blk), 0)
    kj = lax.broadcasted_iota(I32, (blk, 2 * blk), 1)
    rel = qi + blk - kj
    band = (rel >= 0) & (rel < WINDOW)
    bd_row = lax.broadcasted_iota(I32, (4 * blk, LANES), 0)
    bd_lane = lax.broadcasted_iota(I32, (4 * blk, LANES), 1)
    ones_bd = ((bd_row < 2 * blk) == (bd_lane < hd)).astype(BF16)
    zero = jnp.zeros((), BF16)

    def block_diag(t):
        same, swapped = t[:, :LANES], t[:, LANES:]
        out = []
        for g in range(N_KV_HEADS):
            top = jnp.where(first, same if g == 0 else swapped, zero)
            bot = jnp.where(first, zero, swapped if g == 0 else same)
            out.append(jnp.concatenate([top, bot], axis=0))
        return out

    for j in range(nqb):
        rows = slice(j * blk, (j + 1) * blk)
        if j == 0:
            k_prev, v_prev = kp_ref[...], vp_ref[...]
            mask = band & ((kj >= blk) | (n > 0))
        else:
            prev_rows = slice((j - 1) * blk, j * blk)
            k_prev, v_prev = k_ref[prev_rows, :], v_ref[prev_rows, :]
            mask = band
        k_bd = block_diag(jnp.concatenate([k_prev, k_ref[rows, :]], axis=0))
        v_bd = block_diag(jnp.concatenate([v_prev, v_ref[rows, :]], axis=0))
        outs = []
        for p in range(n_pairs):
            g = p // pairs_per_group
            qp = q_ref[rows, p * LANES:(p + 1) * LANES]
            s = lax.dot_general(qp, k_bd[g], (((1,), (1,)), ((), ())), preferred_element_type=F32)
            es, ms = [], []
            for a in range(2):
                sa = jnp.where(mask, s[:, a * 2 * blk:(a + 1) * 2 * blk], NEG_INF)
                m = jnp.maximum(jnp.max(sa, axis=-1, keepdims=True), sinks_ref[2 * p + a])
                es.append(jnp.exp(sa - m))
                ms.append(m)
            e = jnp.concatenate(es, axis=1).astype(BF16)
            od = jnp.dot(e, jnp.concatenate([v_bd[g], ones_bd], axis=1), preferred_element_type=F32)
            sink_term = jnp.where(first, jnp.exp(sinks_ref[2 * p] - ms[0]),
                                  jnp.exp(sinks_ref[2 * p + 1] - ms[1]))
            outs.append(od[:, :LANES] / (od[:, LANES:] + sink_term))
        attn = jnp.concatenate(outs, axis=1)
        o_ref[rows, :] = _rms(attn, g_ref[...]).astype(BF16)


def _attention(q, k, v, sinks, g_attn, B, S):
    T, attn_w = q.shape
    kv_w = k.shape[1]
    blk = WINDOW
    tq = TQ_ATTN
    nt = S // tq
    cur = lambda b, n: (b * nt + n, 0)
    prev = lambda b, n: (b * (S // blk) + jnp.maximum(n * (tq // blk) - 1, 0), 0)
    return pl.pallas_call(
        _attn_kernel,
        grid=(B, nt),
        in_specs=[pl.BlockSpec(memory_space=pltpu.SMEM),
                  pl.BlockSpec((tq, attn_w), cur),
                  pl.BlockSpec((tq, kv_w), cur), pl.BlockSpec((tq, kv_w), cur),
                  pl.BlockSpec((blk, kv_w), prev), pl.BlockSpec((blk, kv_w), prev),
                  pl.BlockSpec((1, attn_w), lambda b, n: (0, 0))],
        out_specs=pl.BlockSpec((tq, attn_w), cur),
        out_shape=jax.ShapeDtypeStruct((T, attn_w), BF16),
        compiler_params=_cparams(("arbitrary", "arbitrary")),
        name="attn",
    )(sinks, q, k, v, k, v, g_attn)


def _neg_expm1(x, exp_x):
    series = -x * (1.0 + x * (0.5 + x * (1.0 / 6.0 + x * (1.0 / 24.0))))
    return jnp.where(x > -0.02, series, 1.0 - exp_x)


def _gelu_tanh(x):
    c = math.sqrt(2.0 / math.pi)
    return x * (0.5 + 0.5 * jnp.tanh(x * (c + (c * 0.044715) * (x * x))))


def _lru_kernel(lx_ref, lg_ref, cw_ref, cb_ref, wa_ref, ba_ref, wx_ref, bx_ref, lam_ref, g_ref,
                o_ref, xpad, a_scr, b_scr, h_scr, h_carry):
    i = pl.program_id(0)
    B, tl, C = lx_ref.shape
    pad = SUBLANES

    @pl.when(i == 0)
    def _():
        xpad[:, 0:pad, :] = jnp.zeros((B, pad, C), F32)
        h_carry[...] = jnp.zeros_like(h_carry)

    @pl.when(i > 0)
    def _():
        xpad[:, 0:pad, :] = xpad[:, tl:tl + pad, :]

    xpad[:, pad:, :] = lx_ref[...]
    groups = (B * tl // SUBLANES, SUBLANES, C)
    sub = lax.broadcasted_iota(I32, groups, 1)
    x_now = lx_ref[...].reshape(groups)
    x_before = xpad[:, pl.ds(0, tl), :].reshape(groups)
    y = x_now * cw_ref[CONV_WIDTH - 1:CONV_WIDTH, :] + cb_ref[...]
    for s in range(1, CONV_WIDTH):
        shifted = pltpu.roll(jnp.where(sub >= SUBLANES - s, x_before, x_now), s, 1)
        y = y + shifted * cw_ref[CONV_WIDTH - 1 - s:CONV_WIDTH - s, :]
    y2 = y.reshape(B * tl, C)
    yb = y2.astype(BF16)
    gate_a = jnp.dot(yb, wa_ref[...], preferred_element_type=F32) + ba_ref[...]
    gate_x = jnp.dot(yb, wx_ref[...], preferred_element_type=F32) + bx_ref[...]
    r = jax.nn.sigmoid(gate_a)
    ig = jax.nn.sigmoid(gate_x)
    nl = -lam_ref[...]
    softplus = jnp.maximum(nl, 0.0) + jnp.log1p(jnp.exp(-jnp.abs(nl)))
    log_a = (-LRU_C) * r * softplus
    a_all = jnp.exp(log_a)
    one_minus_a2 = _neg_expm1(2.0 * log_a, a_all * a_all)
    b_all = one_minus_a2 * lax.rsqrt(jnp.maximum(one_minus_a2, 1e-30)) * ig * y2
    groups = (B * tl // SUBLANES, SUBLANES, C)
    sub = lax.broadcasted_iota(I32, groups, 1)
    a_cum, b_loc = a_all.reshape(groups), b_all.reshape(groups)
    d = 1
    while d < SUBLANES:
        keep = sub >= d
        a_prev = pltpu.roll(a_cum, d, 1)
        b_prev = pltpu.roll(b_loc, d, 1)
        b_loc = b_loc + jnp.where(keep, a_cum * b_prev, 0.0)
        a_cum = jnp.where(keep, a_cum * a_prev, a_cum)
        d *= 2
    a_scr[...] = a_cum.reshape(B, tl, C)
    b_scr[...] = b_loc.reshape(B, tl, C)

    def carry(g, h_prev):
        rows = pl.ds(pl.multiple_of(g * SUBLANES, SUBLANES), SUBLANES)
        h = b_scr[:, rows, :] + a_scr[:, rows, :] * h_prev
        h_scr[:, rows, :] = h
        return h[:, SUBLANES - 1:SUBLANES, :]

    h_carry[...] = lax.fori_loop(0, tl // SUBLANES, carry, h_carry[...], unroll=4)
    lru = h_scr[...].reshape(B * tl, C) * _gelu_tanh(lg_ref[...].reshape(B * tl, C))
    o_ref[...] = _rms(lru, g_ref[...]).astype(BF16).reshape(B, tl, C)


def _lru(lx3, lg3, conv_w, conv_b, wa_bd, ba, wx_bd, bx, lam, g_lru):
    B, S, C = lx3.shape
    tl = TL_LRU
    blk = lambda i: (0, i, 0)
    const = lambda i: (0, 0)
    vec = pl.BlockSpec((1, C), const)
    return pl.pallas_call(
        _lru_kernel,
        grid=(S // tl,),
        in_specs=[pl.BlockSpec((B, tl, C), blk), pl.BlockSpec((B, tl, C), blk),
                  pl.BlockSpec((CONV_WIDTH, C), const), vec,
                  pl.BlockSpec((C, C), const), vec, pl.BlockSpec((C, C), const), vec, vec, vec],
        out_specs=pl.BlockSpec((B, tl, C), blk),
        out_shape=jax.ShapeDtypeStruct((B, S, C), BF16),
        scratch_shapes=[pltpu.VMEM((B, tl + SUBLANES, C), F32),
                        pltpu.VMEM((B, tl, C), F32), pltpu.VMEM((B, tl, C), F32),
                        pltpu.VMEM((B, tl, C), F32), pltpu.VMEM((B, 1, C), F32)],
        compiler_params=_cparams(("arbitrary",)),
        name="lru",
    )(lx3, lg3, conv_w, conv_b, wa_bd, ba, wx_bd, bx, lam, g_lru)


def _outproj_kernel(x_ref, ma_ref, mb_ref, wo_ref, g_ref, wr_ref, br_ref,
                    x1_ref, h2_ref, route_ref, wo_b):
    @pl.when(pl.program_id(0) == 0)
    def _():
        wo_b[...] = wo_ref[...].astype(BF16)

    half = ma_ref.shape[1]
    x1 = (x_ref[...]
          + jnp.dot(ma_ref[...], wo_b[0:half, :], preferred_element_type=F32)
          + jnp.dot(mb_ref[...], wo_b[half:, :], preferred_element_type=F32))
    x1_ref[...] = x1
    h2 = _rms(x1, g_ref[...])
    h2_ref[...] = h2.astype(BF16).reshape(h2_ref.shape)
    h_hi = h2.astype(BF16)
    h_lo = (h2 - h_hi.astype(F32)).astype(BF16)
    hw = jnp.dot(h_hi, wr_ref[...], preferred_element_type=F32)
    logits = (hw[:, :LANES] + hw[:, LANES:]
              + jnp.dot(h_lo, wr_ref[:, :LANES], preferred_element_type=F32) + br_ref[...])
    tm = logits.shape[0]
    lane = lax.broadcasted_iota(I32, (tm, LANES), 1)
    lane_f = lane.astype(F32)
    big = float(LANES)

    def first_argmax(vals):
        m = jnp.max(vals, axis=-1, keepdims=True)
        idx = jnp.min(jnp.where(vals == m, lane_f, big), axis=-1, keepdims=True)
        return m, idx.astype(I32)

    gl = jnp.where(lane < N_GROUPS, logits, NEG_INF)
    gmax, gidx = first_argmax(gl)
    gsum = jnp.sum(jnp.where(lane < N_GROUPS, jnp.exp(gl - gmax), 0.0), axis=-1, keepdims=True)
    g_top_p = 1.0 / gsum
    lo = N_GROUPS + EXPERTS_PER_GROUP * gidx
    el = jnp.where((lane >= lo) & (lane < lo + EXPERTS_PER_GROUP), logits, NEG_INF)
    m1, i1 = first_argmax(el)
    el2 = jnp.where(lane == i1, NEG_INF, el)
    m2, i2 = first_argmax(el2)
    ratio = jnp.exp(m2 - m1)
    w1 = g_top_p / (1.0 + ratio)
    w2 = g_top_p * ratio / (1.0 + ratio)
    ids = jnp.where(lane == 0, i1 - N_GROUPS, jnp.where(lane == 1, i2 - N_GROUPS, 0)).astype(F32)
    route_ref[...] = jnp.where(lane == 2, w1, jnp.where(lane == 3, w2, ids))


def _outproj(x2, mix_a, mix_b, w_out, g_ffn, w_router, b_router):
    T, D = x2.shape
    tm = TM_PROJ
    half = mix_a.shape[1]
    row = lambda i: (i, 0)
    const = lambda i: (0, 0)
    return pl.pallas_call(
        _outproj_kernel,
        grid=(T // tm,),
        in_specs=[pl.BlockSpec((tm, D), row), pl.BlockSpec((tm, half), row),
                  pl.BlockSpec((tm, half), row), pl.BlockSpec(w_out.shape, const, **_FETCH_ONCE),
                  pl.BlockSpec((1, D), const), pl.BlockSpec((D, 2 * LANES), const),
                  pl.BlockSpec((1, LANES), const)],
        out_specs=[pl.BlockSpec((tm, D), row),
                   pl.BlockSpec((tm, D // LANES, LANES), lambda i: (i, 0, 0)),
                   pl.BlockSpec((tm, LANES), row)],
        out_shape=[jax.ShapeDtypeStruct((T, D), F32),
                   jax.ShapeDtypeStruct((T, D // LANES, LANES), BF16),
                   jax.ShapeDtypeStruct((T, LANES), F32)],
        scratch_shapes=[pltpu.VMEM(w_out.shape, BF16)],
        compiler_params=_cparams(("arbitrary",)),
        name="outproj",
    )(x2, mix_a, mix_b, w_out, g_ffn, w_router, b_router)


def _lane_cumsum(x):
    lane = lax.broadcasted_iota(I32, x.shape, 1)
    shift = 1
    while shift < LANES:
        x = x + jnp.where(lane >= shift, pltpu.roll(x, shift, 1), 0)
        shift *= 2
    return x


def _plan_kernel(eid_ref, lrow_d_ref, lrow_c_ref, te_ref, nv_ref, meta_ref, tcnt_ref, tstart_ref,
                 cnt_scr, tcnt_scr, tpre_scr):
    i = pl.program_id(0)
    tm = eid_ref.shape[0]
    tc = TM_COMBINE
    group = TM_DISPATCH // TM_COMBINE
    nsub = tm // tc
    lane = lax.broadcasted_iota(I32, (tm, LANES), 1)
    eid = eid_ref[:, 0:2].astype(I32)
    sel1 = lane == eid[:, 0:1]
    sel2 = lane == eid[:, 1:2]

    @pl.when(i == 0)
    def _():
        cnt_scr[...] = jnp.zeros_like(cnt_scr)

    onehot = (sel1 | sel2).astype(BF16)
    r = lax.broadcasted_iota(I32, (tm, tm), 0)
    c = lax.broadcasted_iota(I32, (tm, tm), 1)
    strict_lower = (r > c).astype(BF16)
    within = jnp.dot(strict_lower, onehot, preferred_element_type=F32)
    base = cnt_scr[0:1, :]
    onehot_f = onehot.astype(F32)
    row = lax.broadcasted_iota(I32, (tm, LANES), 0)
    srow = lax.broadcasted_iota(I32, (2 * SUBLANES, LANES), 0)
    counts = [jnp.sum(onehot_f[s * tc:(s + 1) * tc], axis=0, keepdims=True) for s in range(nsub)]
    cnt_mat = jnp.zeros((2 * SUBLANES, LANES), F32)
    seen = []
    total = jnp.zeros((1, LANES), F32)
    for s in range(nsub):
        seen.append(total)
        tile = i * nsub + s
        tcnt_scr[pl.ds(tile, 1), :] = counts[s]
        tpre_scr[pl.ds(tile, 1), :] = base + total
        cnt_mat = jnp.where(srow == s, counts[s], cnt_mat)
        cnt_mat = jnp.where(srow == SUBLANES + s // group, cnt_mat + counts[s], cnt_mat)
        total = total + counts[s]
    cnt_i = cnt_mat.astype(I32)
    run_start = (_lane_cumsum(cnt_i) - cnt_i).astype(F32)

    def local_rows(first_tile_of, table_row_of):
        local = None
        for s in range(nsub):
            val = within - seen[first_tile_of(s)] + run_start[table_row_of(s):table_row_of(s) + 1, :]
            local = val if local is None else jnp.where(row >= s * tc, val, local)
        l1 = jnp.sum(jnp.where(sel1, local, 0.0), axis=-1, keepdims=True)
        l2 = jnp.sum(jnp.where(sel2, local, 0.0), axis=-1, keepdims=True)
        return jnp.where(lane == 0, l1, jnp.where(lane == 1, l2, 0.0)).astype(I32)

    lrow_c_ref[...] = local_rows(lambda s: s, lambda s: s)
    lrow_d = local_rows(lambda s: (s // group) * group, lambda s: SUBLANES + s // group)
    lrow_d_ref[...] = lrow_d.astype(F32).T[0:SUBLANES, :].astype(I32)
    cnt_scr[...] = cnt_scr[...] + total

    @pl.when(i == pl.num_programs(0) - 1)
    def _():
        shift = int(math.log2(TM_EXPERT))
        cnt = cnt_scr[...].astype(I32)
        padded = ((cnt + (TM_EXPERT - 1)) >> shift) << shift
        ends = _lane_cumsum(padded)
        nt = te_ref.shape[0]
        end_tile = (ends >> shift)[0:1, :]
        tl_lane = lax.broadcasted_iota(I32, (nt, LANES), 1)
        tile = lax.broadcasted_iota(I32, (nt, LANES), 0)
        done = jnp.where((tl_lane < N_EXPERTS) & (end_tile <= tile), 1.0, 0.0)
        te = jnp.minimum(jnp.sum(done, axis=-1, keepdims=True).astype(I32), N_EXPERTS - 1)
        n_valid = jnp.sum(jnp.where(tl_lane[0:1, :] == N_EXPERTS - 1, end_tile, 0).astype(F32),
                          axis=-1, keepdims=True).astype(I32)
        nv_ref[...] = jnp.broadcast_to(n_valid, nv_ref.shape)
        seg_end = jnp.sum(jnp.where(tl_lane == te, end_tile, 0).astype(F32), axis=-1,
                          keepdims=True).astype(I32)
        nxt = jnp.sum(jnp.where((tl_lane < N_EXPERTS) & (end_tile <= seg_end), 1.0, 0.0), axis=-1,
                      keepdims=True).astype(I32)
        nxt = jnp.where(seg_end < n_valid, jnp.minimum(nxt, N_EXPERTS - 1), -1)
        own = tl_lane == te
        seg_cnt = jnp.sum(jnp.where(own, cnt[0:1, :], 0).astype(F32), axis=-1,
                          keepdims=True).astype(I32)
        seg_tiles = jnp.sum(jnp.where(own, padded[0:1, :] >> shift, 0).astype(F32), axis=-1,
                            keepdims=True).astype(I32)
        tile_in_seg = tile[:, 0:1] - (seg_end - seg_tiles)
        rows_used = jnp.clip(seg_cnt - (tile_in_seg << shift), 0, TM_EXPERT)
        te_ref[...] = jnp.where(tl_lane == 1, nxt, jnp.where(tl_lane == 2, rows_used, te))
        row = lax.broadcasted_iota(I32, cnt.shape, 0)
        meta_ref[...] = jnp.where(row == 0, ends - padded + cnt, jnp.where(row == 1, padded - cnt, 0))
        tcnt_ref[...] = tcnt_scr[...].astype(I32)
        tstart_ref[...] = (tpre_scr[...] + (ends - padded).astype(F32)[0:1, :]).astype(I32)


def _plan(eid, n_tiles):
    T = eid.shape[0]
    tm = TM_PLAN
    steps = T // tm
    n_ctiles = T // TM_COMBINE
    assert TM_DISPATCH % TM_COMBINE == 0 and tm % TM_DISPATCH == 0 and tm // TM_COMBINE <= SUBLANES
    row = lambda i: (i, 0)
    const = lambda i: (0, 0)
    return pl.pallas_call(
        _plan_kernel,
        grid=(steps,),
        in_specs=[pl.BlockSpec((tm, LANES), row)],
        out_specs=[pl.BlockSpec((SUBLANES, tm), lambda i: (0, i)), pl.BlockSpec((tm, LANES), row),
                   pl.BlockSpec((n_tiles, LANES), const), pl.BlockSpec((SUBLANES, LANES), const),
                   pl.BlockSpec((SUBLANES, LANES), const),
                   pl.BlockSpec((n_ctiles, LANES), const), pl.BlockSpec((n_ctiles, LANES), const)],
        out_shape=[jax.ShapeDtypeStruct((SUBLANES, T), I32), jax.ShapeDtypeStruct((T, LANES), I32),
                   jax.ShapeDtypeStruct((n_tiles, LANES), I32),
                   jax.ShapeDtypeStruct((SUBLANES, LANES), I32),
                   jax.ShapeDtypeStruct((SUBLANES, LANES), I32),
                   jax.ShapeDtypeStruct((n_ctiles, LANES), I32),
                   jax.ShapeDtypeStruct((n_ctiles, LANES), I32)],
        scratch_shapes=[pltpu.VMEM((SUBLANES, LANES), F32),
                        pltpu.VMEM((n_ctiles, LANES), F32), pltpu.VMEM((n_ctiles, LANES), F32)],
        compiler_params=_cparams(("arbitrary",)),
        name="plan",
    )(eid)


_PAD_BITS = tuple(1 << b for b in reversed(range(int(math.log2(TM_EXPERT)))))


def _run_bits(max_rows):
    return tuple(1 << b for b in reversed(range(int(math.log2(max_rows)) + 1)))


def _dispatch_kernel(tcnt_ref, tstart_ref, meta_ref, nv_ref, lrow_ref, h_ref,
                     xs_ref, lbuf_0, lbuf_1, lbuf_2, zeros, sem, zsem):
    i = pl.program_id(0)
    n = pl.num_programs(0)
    td = h_ref.shape[0]
    nr = 2 * td
    bufs = (lbuf_0, lbuf_1, lbuf_2)

    def wait_buf(s):
        pltpu.make_async_copy(bufs[s], xs_ref.at[pl.ds(0, nr)], sem.at[s]).wait()

    def step(cur_s):
        prev_s = (cur_s - 1) % N_DISPATCH_BUFS
        cur, prev = bufs[cur_s], bufs[prev_s]

        @pl.when(i >= N_DISPATCH_BUFS)
        def _():
            wait_buf(cur_s)

        local_row = lrow_ref[...].astype(F32)
        r = lax.broadcasted_iota(I32, (nr, td), 0).astype(F32)
        onehot = ((r == local_row[0:1, :]) | (r == local_row[1:2, :])).astype(BF16)
        sorted_rows = jnp.dot(onehot, h_ref[...].reshape(td, -1), preferred_element_type=F32)
        cur[...] = sorted_rows.astype(BF16).reshape(cur.shape)

        group = TM_DISPATCH // TM_COMBINE
        first = jnp.maximum(i - 1, 0) * group
        lo = 0
        for e in range(N_EXPERTS):
            cnt = sum(tcnt_ref[first + u, e] for u in range(group))
            cnt = jnp.where(i >= 1, cnt, 0)
            dst = tstart_ref[first, e]
            for bit in _run_bits(td):
                off = cnt & ~(2 * bit - 1)

                @pl.when((cnt & bit) != 0)
                def _():
                    pltpu.make_async_copy(prev.at[pl.ds(lo + off, bit)],
                                          xs_ref.at[pl.ds(dst + off, bit)], sem.at[prev_s]).start()
            lo = lo + cnt

        @pl.when(i == n - 1)
        def _():
            wait_buf(prev_s)
            if N_DISPATCH_BUFS > 2:
                @pl.when(n >= 3)
                def _():
                    wait_buf((cur_s - 2) % N_DISPATCH_BUFS)

    n_tiles = xs_ref.shape[0] // TM_EXPERT
    first_tail = n_tiles - N_EXPERTS

    def pad_copy(e, bit):
        cnt = meta_ref[1, e]
        dst = meta_ref[0, e] + (cnt & ~(2 * bit - 1))
        return (cnt & bit) != 0, pltpu.make_async_copy(zeros.at[pl.ds(0, bit)],
                                                       xs_ref.at[pl.ds(dst, bit)], zsem)

    def tail_copy(j):
        return j >= nv_ref[0, 0], pltpu.make_async_copy(
            zeros, xs_ref.at[pl.ds(j * TM_EXPERT, TM_EXPERT)], zsem)

    def for_each_fill(act):
        def per_expert(e, c):
            for bit in _PAD_BITS:
                pred, cp = pad_copy(e, bit)
                pl.when(pred)(functools.partial(act, cp))
            return c

        lax.fori_loop(0, N_EXPERTS, per_expert, 0)

        def per_tile(j, c):
            pred, cp = tail_copy(j)
            pl.when(pred)(functools.partial(act, cp))
            return c

        lax.fori_loop(first_tail, n_tiles, per_tile, 0)

    @pl.when(i == 0)
    def _():
        zeros[...] = jnp.zeros_like(zeros)
        for_each_fill(lambda cp: cp.start())

    phase = lax.rem(i, N_DISPATCH_BUFS)
    for s in range(N_DISPATCH_BUFS):
        pl.when(phase == s)(functools.partial(step, s))

    @pl.when(i == n - 1)
    def _():
        for_each_fill(lambda cp: cp.wait())


def _dispatch(tile_cnt, tile_start, meta, n_valid, lrow, h2, n_rows):
    T = h2.shape[0]
    slab = h2.shape[1:]
    td = TM_DISPATCH
    nt = T // td
    smem = pl.BlockSpec(memory_space=pltpu.SMEM)
    return pl.pallas_call(
        _dispatch_kernel,
        grid=(nt + 1,),
        in_specs=[smem, smem, smem, smem,
                  pl.BlockSpec((SUBLANES, td), lambda i: (0, jnp.minimum(i, nt - 1))),
                  pl.BlockSpec((td,) + slab, lambda i: (jnp.minimum(i, nt - 1), 0, 0))],
        out_specs=pl.BlockSpec(memory_space=pl.ANY),
        out_shape=jax.ShapeDtypeStruct((n_rows,) + slab, h2.dtype),
        scratch_shapes=[pltpu.VMEM((2 * td,) + slab, h2.dtype)] * N_DISPATCH_BUFS + [
                        pltpu.VMEM((TM_EXPERT,) + slab, h2.dtype),
                        pltpu.SemaphoreType.DMA((N_DISPATCH_BUFS,)), pltpu.SemaphoreType.DMA(())],
        compiler_params=_cparams(("arbitrary",)),
        name="dispatch",
    )(tile_cnt, tile_start, meta, n_valid, lrow, h2)


def _experts_kernel(te_ref, nv_ref, xs_ref, wg_ref, wu_ref, wd_ref, ys_ref,
                    wg_f, wu_f, wd_f, wg_b, wu_b, wd_b, slot_ref, sem):
    j = pl.program_id(0)
    valid = j < nv_ref[0, 0]
    new_expert = valid & ((j == 0) | (te_ref[j, 0] != te_ref[jnp.maximum(j - 1, 0), 0]))

    def fetch(e, s):
        return (pltpu.make_async_copy(wg_ref.at[e], wg_f.at[s], sem.at[s]),
                pltpu.make_async_copy(wu_ref.at[e], wu_f.at[s], sem.at[s]),
                pltpu.make_async_copy(wd_ref.at[e], wd_f.at[s], sem.at[s]))

    @pl.when(j == 0)
    def _():
        slot_ref[0] = 0
        for cp in fetch(te_ref[0, 0], 0):
            cp.start()

    @pl.when(new_expert)
    def _():
        s = slot_ref[0]
        for cp in fetch(te_ref[j, 0], s):
            cp.wait()
        nxt = te_ref[j, 1]

        @pl.when(nxt >= 0)
        def _():
            for cp in fetch(nxt, 1 - s):
                cp.start()

        wg_b[...] = wg_f[s].astype(BF16)
        wu_b[...] = wu_f[s].astype(BF16)
        wd_b[...] = wd_f[s].astype(BF16)
        slot_ref[0] = 1 - s

    def mlp(rows):
        x = xs_ref[0:rows].reshape(rows, -1)
        y = None
        for lo in range(0, wg_b.shape[1], EXPERT_FF_BLOCK):
            cols = slice(lo, lo + EXPERT_FF_BLOCK)
            a = jnp.dot(x, wg_b[:, cols], preferred_element_type=F32)
            u = jnp.dot(x, wu_b[:, cols], preferred_element_type=F32)
            hid = (a * jax.nn.sigmoid(a) * u).astype(BF16)
            part = jnp.dot(hid, wd_b[cols, :], preferred_element_type=F32)
            y = part if y is None else y + part
        ys_ref[0:rows] = y.astype(BF16).reshape((rows,) + ys_ref.shape[1:])
        if rows < tm:
            ys_ref[rows:] = jnp.zeros((tm - rows,) + ys_ref.shape[1:], ys_ref.dtype)

    tm = xs_ref.shape[0]
    blocks_used = (te_ref[j, 2] + (EXPERT_ROW_BLOCK - 1)) // EXPERT_ROW_BLOCK
    for nblk in range(1, tm // EXPERT_ROW_BLOCK + 1):
        pl.when(valid & (blocks_used == nblk))(functools.partial(mlp, nblk * EXPERT_ROW_BLOCK))

    @pl.when(jnp.logical_not(valid) | (blocks_used == 0))
    def _():
        ys_ref[...] = jnp.zeros_like(ys_ref)


def _experts(tile_table, n_valid, xs, wg, wu, wd):
    R = xs.shape[0]
    slab = xs.shape[1:]
    E, D, Fh = wg.shape
    tm = TM_EXPERT
    n_tiles = R // tm
    rows = lambda j, te, nv: (j, 0, 0)
    rows_in = lambda j, te, nv: (jnp.minimum(j, nv[0, 0] - 1), 0, 0)
    hbm = pl.BlockSpec(memory_space=pl.ANY)
    return pl.pallas_call(
        _experts_kernel,
        grid_spec=pltpu.PrefetchScalarGridSpec(
            num_scalar_prefetch=2,
            grid=(n_tiles,),
            in_specs=[pl.BlockSpec((tm,) + slab, rows_in), hbm, hbm, hbm],
            out_specs=pl.BlockSpec((tm,) + slab, rows),
            scratch_shapes=[pltpu.VMEM((2, D, Fh), F32), pltpu.VMEM((2, D, Fh), F32),
                            pltpu.VMEM((2, Fh, D), F32),
                            pltpu.VMEM((D, Fh), BF16), pltpu.VMEM((D, Fh), BF16),
                            pltpu.VMEM((Fh, D), BF16),
                            pltpu.SMEM((1,), I32), pltpu.SemaphoreType.DMA((2,))]),
        out_shape=jax.ShapeDtypeStruct(xs.shape, xs.dtype),
        compiler_params=_cparams(("arbitrary",)),
        name="experts",
    )(tile_table, n_valid, xs, wg, wu, wd)


def _combine_kernel(tcnt_ref, tstart_ref, lrow_ref, x1_ref, wt_ref, p_ref, gp_ref, wpg_ref, wpp_ref,
                    gf_ref, ys_ref, o_ref, *scratch):
    g = COMBINE_TILES
    nb = N_COMBINE_BUFS
    bufs = tuple(scratch[k * g:(k + 1) * g] for k in range(nb))
    wpg_b, wpp_b, sem = scratch[nb * g:]
    i = pl.program_id(0)
    n = pl.num_programs(0)
    tc = x1_ref.shape[0] // g
    nr = 2 * tc

    short_bits = tuple(b for b in _run_bits(tc) if b < COMBINE_LONG_COPY)
    long_bits = tuple(b for b in _run_bits(tc) if b >= COMBINE_LONG_COPY)

    def run_copies(cnt, src, lo, bits, buf, s):
        for bit in bits:
            off = cnt & ~(2 * bit - 1)

            @pl.when((cnt & bit) != 0)
            def _():
                pltpu.make_async_copy(ys_ref.at[pl.ds(src + off, bit)],
                                      buf.at[pl.ds(lo + off, bit)], s).start()

    def request(tile, live, buf, s):
        lo = 0
        for e in range(N_EXPERTS):
            cnt = jnp.where(live, tcnt_ref[tile, e], 0)
            run_copies(cnt, tstart_ref[tile, e], lo, short_bits, buf, s)
            lo = lo + cnt

    def request_long(tile, live, buf, s):
        longest = tcnt_ref[tile, 0]
        for e in range(1, N_EXPERTS):
            longest = jnp.maximum(longest, tcnt_ref[tile, e])

        @pl.when(live & (longest >= COMBINE_LONG_COPY))
        def _():
            def per_expert(e, lo):
                cnt = tcnt_ref[tile, e]
                run_copies(cnt, tstart_ref[tile, e], lo, long_bits, buf, s)
                return lo + cnt

            lax.fori_loop(0, N_EXPERTS, per_expert, 0)

    ahead = nb - 1

    @pl.when(i == 0)
    def _():
        for k in range(ahead):
            for u in range(g):
                first = (jnp.minimum(k, n - 1) * g + u, k < n, bufs[k][u], sem.at[k, u])
                request(*first)
                request_long(*first)
        wpg_b[...] = wpg_ref[...].astype(BF16)
        wpp_b[...] = wpp_ref[...].astype(BF16)

    def step(slot):
        cur, far = bufs[slot], bufs[(slot + ahead) % nb]
        far_s = (slot + ahead) % nb
        for u in range(g):
            request_long(jnp.minimum(i + ahead, n - 1) * g + u, i + ahead < n, far[u],
                         sem.at[far_s, u])
        for u in range(g):
            pltpu.make_async_copy(ys_ref.at[pl.ds(0, nr)], cur[u], sem.at[slot, u]).wait()
        for u in range(g):
            request(jnp.minimum(i + ahead, n - 1) * g + u, i + ahead < n, far[u],
                    sem.at[far_s, u])
        r = lax.broadcasted_iota(I32, (tc, nr), 1)
        for u in range(g):
            tok = slice(u * tc, (u + 1) * tc)
            rows = cur[u][...].reshape(nr, -1)
            local_row = lrow_ref[tok, :]
            y0 = jnp.dot((r == local_row[:, 0:1]).astype(BF16), rows, preferred_element_type=F32)
            y1 = jnp.dot((r == local_row[:, 1:2]).astype(BF16), rows, preferred_element_type=F32)
            wt = wt_ref[tok, :]
            x2 = x1_ref[tok, :] + wt[:, 2:3] * y0 + wt[:, 3:4] * y1
            hp = _rms(x2, gp_ref[...]).astype(BF16)
            gate = jax.nn.sigmoid(jnp.dot(hp, wpg_b[...], preferred_element_type=F32))
            proj = jnp.dot(p_ref[tok, :].astype(BF16), wpp_b[...], preferred_element_type=F32)
            x3 = x2 + gate * proj
            o_ref[tok, :] = _rms(x3, gf_ref[...])

    phase = lax.rem(i, nb)
    for slot in range(nb):
        pl.when(phase == slot)(functools.partial(step, slot))


def _combine(tile_cnt, tile_start, lrow, x1, wts, p2, g_ple, w_ple_gate, w_ple_proj, g_final, ys):
    T, D = x1.shape
    P = p2.shape[1]
    g = COMBINE_TILES
    tm = g * TM_COMBINE
    n = T // tm
    row = lambda i: (i, 0)
    const = lambda i: (0, 0)
    smem = pl.BlockSpec(memory_space=pltpu.SMEM)
    run_buf = pltpu.VMEM((2 * TM_COMBINE,) + ys.shape[1:], ys.dtype)
    return pl.pallas_call(
        _combine_kernel,
        grid=(n,),
        in_specs=[smem, smem, pl.BlockSpec((tm, LANES), row),
                  pl.BlockSpec((tm, D), row), pl.BlockSpec((tm, LANES), row),
                  pl.BlockSpec((tm, P), row), pl.BlockSpec((1, D), const),
                  pl.BlockSpec((D, D), const, **_FETCH_ONCE), pl.BlockSpec((P, D), const, **_FETCH_ONCE),
                  pl.BlockSpec((1, D), const), pl.BlockSpec(memory_space=pl.ANY)],
        out_specs=pl.BlockSpec((tm, D), row),
        out_shape=jax.ShapeDtypeStruct((T, D), F32),
        scratch_shapes=[run_buf] * (N_COMBINE_BUFS * g) + [
                        pltpu.VMEM((D, D), BF16), pltpu.VMEM((P, D), BF16),
                        pltpu.SemaphoreType.DMA((N_COMBINE_BUFS, g))],
        compiler_params=_cparams(("arbitrary",)),
        name="combine",
    )(tile_cnt, tile_start, lrow, x1, wts, p2, g_ple, w_ple_gate, w_ple_proj, g_final, ys)


def _block_diag(w):
    nb, d, _ = w.shape
    eye = jnp.eye(nb, dtype=w.dtype)
    return (eye[:, None, :, None] * w[:, :, None, :]).reshape(nb * d, nb * d)


def kernel(x, p, positions, g_mix, w_in, sinks, conv_w, conv_b, lru_wa, lru_ba, lru_wx, lru_bx,
           lru_lambda, g_attn_out, g_lru_out, w_out, g_ffn, w_router_group, b_router_group,
           w_router_expert, b_router_expert, w_expert_gate, w_expert_up, w_expert_down, g_ple,
           w_ple_gate, w_ple_proj, g_final):
    B, S, D = x.shape
    assert w_in.shape[0] == 1, "single-layer block only"
    T = B * S
    lru_w = conv_w.shape[-1]
    n_rows = 2 * T + N_EXPERTS * TM_EXPERT
    n_tiles = n_rows // TM_EXPERT
    pos3 = positions.reshape(T // TM_INPROJ, TM_INPROJ // LANES, LANES).astype(I32)
    x2 = x.reshape(T, D)
    q, k, v, lx, lg = _inproj(x2, pos3, g_mix[0][None], w_in[0], lru_w)
    mix_a = _attention(q, k, v, sinks[0], g_attn_out[0][None], B, S)
    mix_b = _lru(lx.reshape(B, S, lru_w), lg.reshape(B, S, lru_w), conv_w[0], conv_b[0][None],
                 _block_diag(lru_wa[0]).astype(BF16), lru_ba[0][None],
                 _block_diag(lru_wx[0]).astype(BF16), lru_bx[0][None],
                 lru_lambda[0][None], g_lru_out[0][None]).reshape(T, lru_w)
    n_router = N_GROUPS + N_EXPERTS
    w_router = jnp.pad(jnp.concatenate([w_router_group[0], w_router_expert[0]], axis=1),
                       ((0, 0), (0, LANES - n_router)))
    b_router = jnp.pad(jnp.concatenate([b_router_group[0], b_router_expert[0]]),
                       (0, LANES - n_router))[None]
    w_router_hi = w_router.astype(BF16)
    w_router_lo = (w_router - w_router_hi.astype(F32)).astype(BF16)
    x1, h2, route = _outproj(x2, mix_a, mix_b, w_out[0], g_ffn[0][None],
                             jnp.concatenate([w_router_hi, w_router_lo], axis=1), b_router)
    lrow_d, lrow_c, tile_table, n_valid, meta, tile_cnt, tile_start = _plan(route, n_tiles)
    xs = _dispatch(tile_cnt, tile_start, meta, n_valid, lrow_d, h2, n_rows)
    Fh = w_expert_gate.shape[-1]
    ys = _experts(tile_table, n_valid, xs,
                  w_expert_gate[0].reshape(N_EXPERTS, D, Fh),
                  w_expert_up[0].reshape(N_EXPERTS, D, Fh),
                  w_expert_down[0].reshape(N_EXPERTS, Fh, D))
    out = _combine(tile_cnt, tile_start, lrow_c, x1, route,
                   p[0].reshape(T, -1), g_ple[0][None], w_ple_gate[0], w_ple_proj[0], g_final[None], ys)
    return out.reshape(B, S, D)
```

```python
import functools
import math

import jax
import jax.numpy as jnp
from jax import lax
from jax.experimental import pallas as pl
from jax.experimental.pallas import tpu as pltpu

F32 = jnp.float32
BF16 = jnp.bfloat16
I32 = jnp.int32

EPS = 1e-6
N_HEADS = 8
N_KV_HEADS = 2
HEAD_DIM = 64
WINDOW = 128
ROPE_THETA = 10000.0
CONV_WIDTH = 4
LRU_C = 8.0
N_GROUPS = 4
EXPERTS_PER_GROUP = 8
N_EXPERTS = N_GROUPS * EXPERTS_PER_GROUP
NEG_INF = -1e30
LANES = 128
SUBLANES = 8

TM_INPROJ = 512
TM_PROJ = 512
TQ_ATTN = 512
TL_LRU = 256
TM_PLAN = 1024
TM_EXPERT = 512
EXPERT_FF_BLOCK = 256
EXPERT_ROW_BLOCK = 128
TM_DISPATCH = 512
N_DISPATCH_BUFS = 3
N_COMBINE_BUFS = 3
COMBINE_TILES = 1
TM_COMBINE = 256
VMEM_LIMIT = 56 * 1024 * 1024


def _rms(x, g):
    ms = jnp.mean(x * x, axis=-1, keepdims=True)
    return x * lax.rsqrt(ms + EPS) * g


_FETCH_ONCE = dict(pipeline_mode=pl.Buffered(1))


def _cparams(sem):
    return pltpu.CompilerParams(dimension_semantics=sem, vmem_limit_bytes=VMEM_LIMIT)


def _inproj_kernel(x_ref, pos_ref, g_ref, w_ref, q_ref, k_ref, v_ref, lx_ref, lg_ref, w_b):
    @pl.when(pl.program_id(0) == 0)
    def _():
        w_b[...] = w_ref[...].astype(BF16)

    h = _rms(x_ref[...], g_ref[...]).astype(BF16)
    proj = jnp.dot(h, w_b[...], preferred_element_type=F32)
    lane = lax.broadcasted_iota(I32, (1, LANES), 1)
    n_freq = HEAD_DIM // 2
    fidx = lax.broadcasted_iota(I32, (n_freq, 1), 0).astype(F32)
    inv_freq = jnp.exp(fidx * (-2.0 / HEAD_DIM * math.log(ROPE_THETA)))
    pos = pos_ref[0].astype(F32)
    cos_blocks, sin_blocks = [], []
    for c in range(pos.shape[0]):
        ang_t = inv_freq * pos[c:c + 1, :]
        cos_blocks.append(jnp.concatenate([jnp.cos(ang_t)] * (LANES // n_freq), axis=0).T)
        sin_blocks.append(jnp.concatenate([jnp.sin(ang_t)] * (LANES // n_freq), axis=0).T)
    cos = jnp.concatenate(cos_blocks, axis=0)
    sin = jnp.concatenate(sin_blocks, axis=0)
    first_half = (lane % HEAD_DIM) < (HEAD_DIM // 2)
    sin_signed = jnp.where(first_half, -sin, sin)

    def rope(t):
        partner = jnp.where(first_half, pltpu.roll(t, LANES - HEAD_DIM // 2, 1),
                            pltpu.roll(t, HEAD_DIM // 2, 1))
        return t * cos + partner * sin_signed

    attn_w = N_HEADS * HEAD_DIM
    scale = HEAD_DIM ** -0.5
    assert math.frexp(scale)[0] == 0.5, "score scale must be a power of two to fold into q exactly"
    for c in range(attn_w // LANES):
        q_ref[:, c * LANES:(c + 1) * LANES] = (
            rope(proj[:, c * LANES:(c + 1) * LANES]) * scale).astype(BF16)
    kv_w = N_KV_HEADS * HEAD_DIM
    assert kv_w == LANES
    k = rope(proj[:, attn_w:attn_w + kv_w])
    v = proj[:, attn_w + kv_w:attn_w + 2 * kv_w]
    k_ref[:, :kv_w] = k.astype(BF16)
    k_ref[:, kv_w:] = pltpu.roll(k, HEAD_DIM, 1).astype(BF16)
    v_ref[:, :kv_w] = v.astype(BF16)
    v_ref[:, kv_w:] = pltpu.roll(v, HEAD_DIM, 1).astype(BF16)
    off_lx = attn_w + 2 * kv_w
    lru_w = lx_ref.shape[1]
    lx_ref[...] = proj[:, off_lx:off_lx + lru_w]
    lg_ref[...] = proj[:, off_lx + lru_w:off_lx + 2 * lru_w]


def _inproj(x2, pos3, g_mix, w_in, lru_w):
    T, D = x2.shape
    tm = TM_INPROJ
    attn_w = N_HEADS * HEAD_DIM
    kv_w = N_KV_HEADS * HEAD_DIM
    row = lambda i: (i, 0)
    const = lambda i: (0, 0)
    return pl.pallas_call(
        _inproj_kernel,
        grid=(T // tm,),
        in_specs=[pl.BlockSpec((tm, D), row),
                  pl.BlockSpec((1, tm // LANES, LANES), lambda i: (i, 0, 0)),
                  pl.BlockSpec((1, D), const), pl.BlockSpec(w_in.shape, const, **_FETCH_ONCE)],
        out_specs=[pl.BlockSpec((tm, attn_w), row), pl.BlockSpec((tm, 2 * kv_w), row),
                   pl.BlockSpec((tm, 2 * kv_w), row), pl.BlockSpec((tm, lru_w), row),
                   pl.BlockSpec((tm, lru_w), row)],
        out_shape=[jax.ShapeDtypeStruct((T, attn_w), BF16),
                   jax.ShapeDtypeStruct((T, 2 * kv_w), BF16),
                   jax.ShapeDtypeStruct((T, 2 * kv_w), BF16), jax.ShapeDtypeStruct((T, lru_w), F32),
                   jax.ShapeDtypeStruct((T, lru_w), F32)],
        scratch_shapes=[pltpu.VMEM(w_in.shape, BF16)],
        compiler_params=_cparams(("arbitrary",)),
        name="inproj",
    )(x2, pos3, g_mix, w_in)


def _attn_kernel(sinks_ref, q_ref, k_ref, v_ref, kp_ref, vp_ref, g_ref, o_ref):
    n = pl.program_id(1)
    blk = WINDOW
    hd = HEAD_DIM
    nqb = q_ref.shape[0] // blk
    n_pairs = N_HEADS // 2
    pairs_per_group = n_pairs // N_KV_HEADS
    lane = lax.broadcasted_iota(I32, (1, LANES), 1)
    first = lane < hd
    assert WINDOW == blk
    from_prev = (lax.broadcasted_iota(I32, (blk, blk), 1) > lax.broadcasted_iota(I32, (blk, blk), 0))
    bd_row = lax.broadcasted_iota(I32, (4 * blk, LANES), 0)
    bd_lane = lax.broadcasted_iota(I32, (4 * blk, LANES), 1)
    ones_bd = ((bd_row < 2 * blk) == (bd_lane < hd)).astype(BF16)
    zero = jnp.zeros((), BF16)

    def block_diag(t):
        same, swapped = t[:, :LANES], t[:, LANES:]
        out = []
        for g in range(N_KV_HEADS):
            top = jnp.where(first, same if g == 0 else swapped, zero)
            bot = jnp.where(first, zero, swapped if g == 0 else same)
            out.append(jnp.concatenate([top, bot], axis=0))
        return out

    for j in range(nqb):
        rows = slice(j * blk, (j + 1) * blk)
        if j == 0:
            k_prev, v_prev = kp_ref[...], vp_ref[...]
        else:
            prev_rows = slice((j - 1) * blk, j * blk)
            k_prev, v_prev = k_ref[prev_rows, :], v_ref[prev_rows, :]
        k_bd = block_diag(jnp.concatenate([k_prev, k_ref[rows, :]], axis=0))
        v_bd = block_diag(jnp.concatenate([v_prev, v_ref[rows, :]], axis=0))
        outs = []
        for p in range(n_pairs):
            g = p // pairs_per_group
            qp = q_ref[rows, p * LANES:(p + 1) * LANES]
            s = lax.dot_general(qp, k_bd[g], (((1,), (1,)), ((), ())), preferred_element_type=F32)
            es, ms = [], []
            for a in range(2):
                s_prev = s[:, 2 * a * blk:(2 * a + 1) * blk]
                s_cur = s[:, (2 * a + 1) * blk:(2 * a + 2) * blk]
                if j == 0:
                    s_prev = jnp.where(n > 0, s_prev, NEG_INF)
                sa = jnp.where(from_prev, s_prev, s_cur)
                m = jnp.maximum(jnp.max(sa, axis=-1, keepdims=True), sinks_ref[2 * p + a])
                ea = jnp.exp(sa - m)
                es += [jnp.where(from_prev, ea, 0.0), jnp.where(from_prev, 0.0, ea)]
                ms.append(m)
            e = jnp.concatenate(es, axis=1).astype(BF16)
            od = jnp.dot(e, jnp.concatenate([v_bd[g], ones_bd], axis=1), preferred_element_type=F32)
            sink_term = jnp.where(first, jnp.exp(sinks_ref[2 * p] - ms[0]),
                                  jnp.exp(sinks_ref[2 * p + 1] - ms[1]))
            outs.append(od[:, :LANES] / (od[:, LANES:] + sink_term))
        attn = jnp.concatenate(outs, axis=1)
        o_ref[rows, :] = _rms(attn, g_ref[...]).astype(BF16)


def _attention(q, k, v, sinks, g_attn, B, S):
    T, attn_w = q.shape
    kv_w = k.shape[1]
    blk = WINDOW
    tq = TQ_ATTN
    nt = S // tq
    cur = lambda b, n: (b * nt + n, 0)
    prev = lambda b, n: (b * (S // blk) + jnp.maximum(n * (tq // blk) - 1, 0), 0)
    return pl.pallas_call(
        _attn_kernel,
        grid=(B, nt),
        in_specs=[pl.BlockSpec(memory_space=pltpu.SMEM),
                  pl.BlockSpec((tq, attn_w), cur),
                  pl.BlockSpec((tq, kv_w), cur), pl.BlockSpec((tq, kv_w), cur),
                  pl.BlockSpec((blk, kv_w), prev), pl.BlockSpec((blk, kv_w), prev),
                  pl.BlockSpec((1, attn_w), lambda b, n: (0, 0))],
        out_specs=pl.BlockSpec((tq, attn_w), cur),
        out_shape=jax.ShapeDtypeStruct((T, attn_w), BF16),
        compiler_params=_cparams(("arbitrary", "arbitrary")),
        name="attn",
    )(sinks, q, k, v, k, v, g_attn)


def _neg_expm1(x, exp_x):
    series = -x * (1.0 + x * (0.5 + x * (1.0 / 6.0 + x * (1.0 / 24.0))))
    return jnp.where(x > -0.02, series, 1.0 - exp_x)


def _gelu_tanh(x):
    c = math.sqrt(2.0 / math.pi)
    return x * (0.5 + 0.5 * jnp.tanh(x * (c + (c * 0.044715) * (x * x))))


def _lru_kernel(lx_ref, lg_ref, cw_ref, cb_ref, wa_ref, ba_ref, wx_ref, bx_ref, lam_ref, g_ref,
                o_ref, xpad, a_scr, b_scr, h_scr, h_carry):
    i = pl.program_id(0)
    B, tl, C = lx_ref.shape
    pad = SUBLANES

    @pl.when(i == 0)
    def _():
        xpad[:, 0:pad, :] = jnp.zeros((B, pad, C), F32)
        h_carry[...] = jnp.zeros_like(h_carry)

    @pl.when(i > 0)
    def _():
        xpad[:, 0:pad, :] = xpad[:, tl:tl + pad, :]

    xpad[:, pad:, :] = lx_ref[...]
    groups = (B * tl // SUBLANES, SUBLANES, C)
    sub = lax.broadcasted_iota(I32, groups, 1)
    x_now = lx_ref[...].reshape(groups)
    x_before = xpad[:, pl.ds(0, tl), :].reshape(groups)
    y = x_now * cw_ref[CONV_WIDTH - 1:CONV_WIDTH, :] + cb_ref[...]
    for s in range(1, CONV_WIDTH):
        shifted = pltpu.roll(jnp.where(sub >= SUBLANES - s, x_before, x_now), s, 1)
        y = y + shifted * cw_ref[CONV_WIDTH - 1 - s:CONV_WIDTH - s, :]
    y2 = y.reshape(B * tl, C)
    yb = y2.astype(BF16)
    gate_a = jnp.dot(yb, wa_ref[...], preferred_element_type=F32) + ba_ref[...]
    gate_x = jnp.dot(yb, wx_ref[...], preferred_element_type=F32) + bx_ref[...]
    r = jax.nn.sigmoid(gate_a)
    ig = jax.nn.sigmoid(gate_x)
    nl = -lam_ref[...]
    softplus = jnp.maximum(nl, 0.0) + jnp.log1p(jnp.exp(-jnp.abs(nl)))
    log_a = (-LRU_C) * r * softplus
    a_all = jnp.exp(log_a)
    one_minus_a2 = _neg_expm1(2.0 * log_a, a_all * a_all)
    b_all = one_minus_a2 * lax.rsqrt(jnp.maximum(one_minus_a2, 1e-30)) * ig * y2
    groups = (B * tl // SUBLANES, SUBLANES, C)
    sub = lax.broadcasted_iota(I32, groups, 1)
    a_cum, b_loc = a_all.reshape(groups), b_all.reshape(groups)
    d = 1
    while d < SUBLANES:
        keep = sub >= d
        a_prev = pltpu.roll(a_cum, d, 1)
        b_prev = pltpu.roll(b_loc, d, 1)
        b_loc = b_loc + jnp.where(keep, a_cum * b_prev, 0.0)
        a_cum = jnp.where(keep, a_cum * a_prev, a_cum)
        d *= 2
    a_scr[...] = a_cum.reshape(B, tl, C)
    b_scr[...] = b_loc.reshape(B, tl, C)

    def carry(g, h_prev):
        rows = pl.ds(pl.multiple_of(g * SUBLANES, SUBLANES), SUBLANES)
        h = b_scr[:, rows, :] + a_scr[:, rows, :] * h_prev
        h_scr[:, rows, :] = h
        return h[:, SUBLANES - 1:SUBLANES, :]

    h_carry[...] = lax.fori_loop(0, tl // SUBLANES, carry, h_carry[...], unroll=4)
    lru = h_scr[...].reshape(B * tl, C) * _gelu_tanh(lg_ref[...].reshape(B * tl, C))
    o_ref[...] = _rms(lru, g_ref[...]).astype(BF16).reshape(B, tl, C)


def _lru(lx3, lg3, conv_w, conv_b, wa_bd, ba, wx_bd, bx, lam, g_lru):
    B, S, C = lx3.shape
    tl = TL_LRU
    blk = lambda i: (0, i, 0)
    const = lambda i: (0, 0)
    vec = pl.BlockSpec((1, C), const)
    return pl.pallas_call(
        _lru_kernel,
        grid=(S // tl,),
        in_specs=[pl.BlockSpec((B, tl, C), blk), pl.BlockSpec((B, tl, C), blk),
                  pl.BlockSpec((CONV_WIDTH, C), const), vec,
                  pl.BlockSpec((C, C), const), vec, pl.BlockSpec((C, C), const), vec, vec, vec],
        out_specs=pl.BlockSpec((B, tl, C), blk),
        out_shape=jax.ShapeDtypeStruct((B, S, C), BF16),
        scratch_shapes=[pltpu.VMEM((B, tl + SUBLANES, C), F32),
                        pltpu.VMEM((B, tl, C), F32), pltpu.VMEM((B, tl, C), F32),
                        pltpu.VMEM((B, tl, C), F32), pltpu.VMEM((B, 1, C), F32)],
        compiler_params=_cparams(("arbitrary",)),
        name="lru",
    )(lx3, lg3, conv_w, conv_b, wa_bd, ba, wx_bd, bx, lam, g_lru)


def _outproj_kernel(x_ref, ma_ref, mb_ref, wo_ref, g_ref, wr_ref, br_ref,
                    x1_ref, h2_ref, route_ref, wo_b):
    @pl.when(pl.program_id(0) == 0)
    def _():
        wo_b[...] = wo_ref[...].astype(BF16)

    half = ma_ref.shape[1]
    x1 = (x_ref[...]
          + jnp.dot(ma_ref[...], wo_b[0:half, :], preferred_element_type=F32)
          + jnp.dot(mb_ref[...], wo_b[half:, :], preferred_element_type=F32))
    x1_ref[...] = x1
    h2 = _rms(x1, g_ref[...])
    h2_ref[...] = h2.astype(BF16).reshape(h2_ref.shape)
    h_hi = h2.astype(BF16)
    h_lo = (h2 - h_hi.astype(F32)).astype(BF16)
    hw = jnp.dot(h_hi, wr_ref[...], preferred_element_type=F32)
    logits = (hw[:, :LANES] + hw[:, LANES:]
              + jnp.dot(h_lo, wr_ref[:, :LANES], preferred_element_type=F32) + br_ref[...])
    tm = logits.shape[0]
    lane = lax.broadcasted_iota(I32, (tm, LANES), 1)
    lane_f = lane.astype(F32)
    big = float(LANES)

    def first_argmax(vals):
        m = jnp.max(vals, axis=-1, keepdims=True)
        idx = jnp.min(jnp.where(vals == m, lane_f, big), axis=-1, keepdims=True)
        return m, idx.astype(I32)

    gl = jnp.where(lane < N_GROUPS, logits, NEG_INF)
    gmax, gidx = first_argmax(gl)
    gsum = jnp.sum(jnp.where(lane < N_GROUPS, jnp.exp(gl - gmax), 0.0), axis=-1, keepdims=True)
    g_top_p = 1.0 / gsum
    lo = N_GROUPS + EXPERTS_PER_GROUP * gidx
    el = jnp.where((lane >= lo) & (lane < lo + EXPERTS_PER_GROUP), logits, NEG_INF)
    m1, i1 = first_argmax(el)
    el2 = jnp.where(lane == i1, NEG_INF, el)
    m2, i2 = first_argmax(el2)
    ratio = jnp.exp(m2 - m1)
    w1 = g_top_p / (1.0 + ratio)
    w2 = g_top_p * ratio / (1.0 + ratio)
    ids = jnp.where(lane == 0, i1 - N_GROUPS, jnp.where(lane == 1, i2 - N_GROUPS, 0)).astype(F32)
    route_ref[...] = jnp.where(lane == 2, w1, jnp.where(lane == 3, w2, ids))


def _outproj(x2, mix_a, mix_b, w_out, g_ffn, w_router, b_router):
    T, D = x2.shape
    tm = TM_PROJ
    half = mix_a.shape[1]
    row = lambda i: (i, 0)
    const = lambda i: (0, 0)
    return pl.pallas_call(
        _outproj_kernel,
        grid=(T // tm,),
        in_specs=[pl.BlockSpec((tm, D), row), pl.BlockSpec((tm, half), row),
                  pl.BlockSpec((tm, half), row), pl.BlockSpec(w_out.shape, const, **_FETCH_ONCE),
                  pl.BlockSpec((1, D), const), pl.BlockSpec((D, 2 * LANES), const),
                  pl.BlockSpec((1, LANES), const)],
        out_specs=[pl.BlockSpec((tm, D), row),
                   pl.BlockSpec((tm, D // LANES, LANES), lambda i: (i, 0, 0)),
                   pl.BlockSpec((tm, LANES), row)],
        out_shape=[jax.ShapeDtypeStruct((T, D), F32),
                   jax.ShapeDtypeStruct((T, D // LANES, LANES), BF16),
                   jax.ShapeDtypeStruct((T, LANES), F32)],
        scratch_shapes=[pltpu.VMEM(w_out.shape, BF16)],
        compiler_params=_cparams(("arbitrary",)),
        name="outproj",
    )(x2, mix_a, mix_b, w_out, g_ffn, w_router, b_router)


def _lane_cumsum(x):
    lane = lax.broadcasted_iota(I32, x.shape, 1)
    shift = 1
    while shift < LANES:
        x = x + jnp.where(lane >= shift, pltpu.roll(x, shift, 1), 0)
        shift *= 2
    return x


def _plan_kernel(eid_ref, lrow_d_ref, lrow_c_ref, te_ref, nv_ref, meta_ref, tcnt_ref, tstart_ref,
                 cnt_scr, tcnt_scr, tpre_scr):
    i = pl.program_id(0)
    tm = eid_ref.shape[0]
    tc = TM_COMBINE
    group = TM_DISPATCH // TM_COMBINE
    nsub = tm // tc
    lane = lax.broadcasted_iota(I32, (tm, LANES), 1)
    eid = eid_ref[:, 0:2].astype(I32)
    sel1 = lane == eid[:, 0:1]
    sel2 = lane == eid[:, 1:2]

    @pl.when(i == 0)
    def _():
        cnt_scr[...] = jnp.zeros_like(cnt_scr)

    onehot = (sel1 | sel2).astype(BF16)
    r = lax.broadcasted_iota(I32, (tm, tm), 0)
    c = lax.broadcasted_iota(I32, (tm, tm), 1)
    strict_lower = (r > c).astype(BF16)
    within = jnp.dot(strict_lower, onehot, preferred_element_type=F32)
    base = cnt_scr[0:1, :]
    onehot_f = onehot.astype(F32)
    row = lax.broadcasted_iota(I32, (tm, LANES), 0)
    srow = lax.broadcasted_iota(I32, (2 * SUBLANES, LANES), 0)
    counts = [jnp.sum(onehot_f[s * tc:(s + 1) * tc], axis=0, keepdims=True) for s in range(nsub)]
    cnt_mat = jnp.zeros((2 * SUBLANES, LANES), F32)
    seen = []
    total = jnp.zeros((1, LANES), F32)
    for s in range(nsub):
        seen.append(total)
        tile = i * nsub + s
        tcnt_scr[pl.ds(tile, 1), :] = counts[s]
        tpre_scr[pl.ds(tile, 1), :] = base + total
        cnt_mat = jnp.where(srow == s, counts[s], cnt_mat)
        cnt_mat = jnp.where(srow == SUBLANES + s // group, cnt_mat + counts[s], cnt_mat)
        total = total + counts[s]
    cnt_i = cnt_mat.astype(I32)
    run_start = (_lane_cumsum(cnt_i) - cnt_i).astype(F32)

    def local_rows(first_tile_of, table_row_of):
        local = None
        for s in range(nsub):
            val = within - seen[first_tile_of(s)] + run_start[table_row_of(s):table_row_of(s) + 1, :]
            local = val if local is None else jnp.where(row >= s * tc, val, local)
        l1 = jnp.sum(jnp.where(sel1, local, 0.0), axis=-1, keepdims=True)
        l2 = jnp.sum(jnp.where(sel2, local, 0.0), axis=-1, keepdims=True)
        return jnp.where(lane == 0, l1, jnp.where(lane == 1, l2, 0.0)).astype(I32)

    lrow_c_ref[...] = local_rows(lambda s: s, lambda s: s)
    lrow_d = local_rows(lambda s: (s // group) * group, lambda s: SUBLANES + s // group)
    lrow_d_ref[...] = lrow_d.astype(F32).T[0:SUBLANES, :].astype(I32)
    cnt_scr[...] = cnt_scr[...] + total

    @pl.when(i == pl.num_programs(0) - 1)
    def _():
        shift = int(math.log2(TM_EXPERT))
        cnt = cnt_scr[...].astype(I32)
        padded = ((cnt + (TM_EXPERT - 1)) >> shift) << shift
        ends = _lane_cumsum(padded)
        nt = te_ref.shape[0]
        end_tile = (ends >> shift)[0:1, :]
        tl_lane = lax.broadcasted_iota(I32, (nt, LANES), 1)
        tile = lax.broadcasted_iota(I32, (nt, LANES), 0)
        done = jnp.where((tl_lane < N_EXPERTS) & (end_tile <= tile), 1.0, 0.0)
        te = jnp.minimum(jnp.sum(done, axis=-1, keepdims=True).astype(I32), N_EXPERTS - 1)
        n_valid = jnp.sum(jnp.where(tl_lane[0:1, :] == N_EXPERTS - 1, end_tile, 0).astype(F32),
                          axis=-1, keepdims=True).astype(I32)
        nv_ref[...] = jnp.broadcast_to(n_valid, nv_ref.shape)
        seg_end = jnp.sum(jnp.where(tl_lane == te, end_tile, 0).astype(F32), axis=-1,
                          keepdims=True).astype(I32)
        nxt = jnp.sum(jnp.where((tl_lane < N_EXPERTS) & (end_tile <= seg_end), 1.0, 0.0), axis=-1,
                      keepdims=True).astype(I32)
        nxt = jnp.where(seg_end < n_valid, jnp.minimum(nxt, N_EXPERTS - 1), -1)
        own = tl_lane == te
        seg_cnt = jnp.sum(jnp.where(own, cnt[0:1, :], 0).astype(F32), axis=-1,
                          keepdims=True).astype(I32)
        seg_tiles = jnp.sum(jnp.where(own, padded[0:1, :] >> shift, 0).astype(F32), axis=-1,
                            keepdims=True).astype(I32)
        tile_in_seg = tile[:, 0:1] - (seg_end - seg_tiles)
        rows_used = jnp.clip(seg_cnt - (tile_in_seg << shift), 0, TM_EXPERT)
        te_ref[...] = jnp.where(tl_lane == 1, nxt, jnp.where(tl_lane == 2, rows_used, te))
        row = lax.broadcasted_iota(I32, cnt.shape, 0)
        meta_ref[...] = jnp.where(row == 0, ends - padded + cnt, jnp.where(row == 1, padded - cnt, 0))
        tcnt_ref[...] = tcnt_scr[...].astype(I32)
        tstart_ref[...] = (tpre_scr[...] + (ends - padded).astype(F32)[0:1, :]).astype(I32)


def _plan(eid, n_tiles):
    T = eid.shape[0]
    tm = TM_PLAN
    steps = T // tm
    n_ctiles = T // TM_COMBINE
    assert TM_DISPATCH % TM_COMBINE == 0 and tm % TM_DISPATCH == 0 and tm // TM_COMBINE <= SUBLANES
    row = lambda i: (i, 0)
    const = lambda i: (0, 0)
    return pl.pallas_call(
        _plan_kernel,
        grid=(steps,),
        in_specs=[pl.BlockSpec((tm, LANES), row)],
        out_specs=[pl.BlockSpec((SUBLANES, tm), lambda i: (0, i)), pl.BlockSpec((tm, LANES), row),
                   pl.BlockSpec((n_tiles, LANES), const), pl.BlockSpec((SUBLANES, LANES), const),
                   pl.BlockSpec((SUBLANES, LANES), const),
                   pl.BlockSpec((n_ctiles, LANES), const), pl.BlockSpec((n_ctiles, LANES), const)],
        out_shape=[jax.ShapeDtypeStruct((SUBLANES, T), I32), jax.ShapeDtypeStruct((T, LANES), I32),
                   jax.ShapeDtypeStruct((n_tiles, LANES), I32),
                   jax.ShapeDtypeStruct((SUBLANES, LANES), I32),
                   jax.ShapeDtypeStruct((SUBLANES, LANES), I32),
                   jax.ShapeDtypeStruct((n_ctiles, LANES), I32),
                   jax.ShapeDtypeStruct((n_ctiles, LANES), I32)],
        scratch_shapes=[pltpu.VMEM((SUBLANES, LANES), F32),
                        pltpu.VMEM((n_ctiles, LANES), F32), pltpu.VMEM((n_ctiles, LANES), F32)],
        compiler_params=_cparams(("arbitrary",)),
        name="plan",
    )(eid)


_PAD_BITS = tuple(1 << b for b in reversed(range(int(math.log2(TM_EXPERT)))))


def _run_bits(max_rows):
    return tuple(1 << b for b in reversed(range(int(math.log2(max_rows)) + 1)))


def _dispatch_kernel(tcnt_ref, tstart_ref, meta_ref, nv_ref, lrow_ref, h_ref,
                     xs_ref, lbuf_0, lbuf_1, lbuf_2, zeros, sem, zsem):
    i = pl.program_id(0)
    n = pl.num_programs(0)
    td = h_ref.shape[0]
    nr = 2 * td
    bufs = (lbuf_0, lbuf_1, lbuf_2)

    def wait_buf(s):
        pltpu.make_async_copy(bufs[s], xs_ref.at[pl.ds(0, nr)], sem.at[s]).wait()

    def step(cur_s):
        prev_s = (cur_s - 1) % N_DISPATCH_BUFS
        cur, prev = bufs[cur_s], bufs[prev_s]

        @pl.when(i >= N_DISPATCH_BUFS)
        def _():
            wait_buf(cur_s)

        local_row = lrow_ref[...].astype(F32)
        r = lax.broadcasted_iota(I32, (nr, td), 0).astype(F32)
        onehot = ((r == local_row[0:1, :]) | (r == local_row[1:2, :])).astype(BF16)
        sorted_rows = jnp.dot(onehot, h_ref[...].reshape(td, -1), preferred_element_type=F32)
        cur[...] = sorted_rows.astype(BF16).reshape(cur.shape)

        group = TM_DISPATCH // TM_COMBINE
        first = jnp.maximum(i - 1, 0) * group
        lo = 0
        for e in range(N_EXPERTS):
            cnt = sum(tcnt_ref[first + u, e] for u in range(group))
            cnt = jnp.where(i >= 1, cnt, 0)
            dst = tstart_ref[first, e]
            for bit in _run_bits(td):
                off = cnt & ~(2 * bit - 1)

                @pl.when((cnt & bit) != 0)
                def _():
                    pltpu.make_async_copy(prev.at[pl.ds(lo + off, bit)],
                                          xs_ref.at[pl.ds(dst + off, bit)], sem.at[prev_s]).start()
            lo = lo + cnt

        @pl.when(i == n - 1)
        def _():
            wait_buf(prev_s)
            if N_DISPATCH_BUFS > 2:
                @pl.when(n >= 3)
                def _():
                    wait_buf((cur_s - 2) % N_DISPATCH_BUFS)

    n_tiles = xs_ref.shape[0] // TM_EXPERT
    first_tail = n_tiles - N_EXPERTS

    def pad_copy(e, bit):
        cnt = meta_ref[1, e]
        dst = meta_ref[0, e] + (cnt & ~(2 * bit - 1))
        return (cnt & bit) != 0, pltpu.make_async_copy(zeros.at[pl.ds(0, bit)],
                                                       xs_ref.at[pl.ds(dst, bit)], zsem)

    def tail_copy(j):
        return j >= nv_ref[0, 0], pltpu.make_async_copy(
            zeros, xs_ref.at[pl.ds(j * TM_EXPERT, TM_EXPERT)], zsem)

    def for_each_fill(act):
        def per_expert(e, c):
            for bit in _PAD_BITS:
                pred, cp = pad_copy(e, bit)
                pl.when(pred)(functools.partial(act, cp))
            return c

        lax.fori_loop(0, N_EXPERTS, per_expert, 0)

        def per_tile(j, c):
            pred, cp = tail_copy(j)
            pl.when(pred)(functools.partial(act, cp))
            return c

        lax.fori_loop(first_tail, n_tiles, per_tile, 0)

    @pl.when(i == 0)
    def _():
        zeros[...] = jnp.zeros_like(zeros)
        for_each_fill(lambda cp: cp.start())

    phase = lax.rem(i, N_DISPATCH_BUFS)
    for s in range(N_DISPATCH_BUFS):
        pl.when(phase == s)(functools.partial(step, s))

    @pl.when(i == n - 1)
    def _():
        for_each_fill(lambda cp: cp.wait())


def _dispatch(tile_cnt, tile_start, meta, n_valid, lrow, h2, n_rows):
    T = h2.shape[0]
    slab = h2.shape[1:]
    td = TM_DISPATCH
    nt = T // td
    smem = pl.BlockSpec(memory_space=pltpu.SMEM)
    return pl.pallas_call(
        _dispatch_kernel,
        grid=(nt + 1,),
        in_specs=[smem, smem, smem, smem,
                  pl.BlockSpec((SUBLANES, td), lambda i: (0, jnp.minimum(i, nt - 1))),
                  pl.BlockSpec((td,) + slab, lambda i: (jnp.minimum(i, nt - 1), 0, 0))],
        out_specs=pl.BlockSpec(memory_space=pl.ANY),
        out_shape=jax.ShapeDtypeStruct((n_rows,) + slab, h2.dtype),
        scratch_shapes=[pltpu.VMEM((2 * td,) + slab, h2.dtype)] * N_DISPATCH_BUFS + [
                        pltpu.VMEM((TM_EXPERT,) + slab, h2.dtype),
                        pltpu.SemaphoreType.DMA((N_DISPATCH_BUFS,)), pltpu.SemaphoreType.DMA(())],
        compiler_params=_cparams(("arbitrary",)),
        name="dispatch",
    )(tile_cnt, tile_start, meta, n_valid, lrow, h2)


def _experts_kernel(te_ref, nv_ref, xs_ref, wg_ref, wu_ref, wd_ref, ys_ref,
                    wg_f, wu_f, wd_f, wg_b, wu_b, wd_b, slot_ref, sem):
    j = pl.program_id(0)
    valid = j < nv_ref[0, 0]
    new_expert = valid & ((j == 0) | (te_ref[j, 0] != te_ref[jnp.maximum(j - 1, 0), 0]))

    def fetch(e, s):
        return (pltpu.make_async_copy(wg_ref.at[e], wg_f.at[s], sem.at[s]),
                pltpu.make_async_copy(wu_ref.at[e], wu_f.at[s], sem.at[s]),
                pltpu.make_async_copy(wd_ref.at[e], wd_f.at[s], sem.at[s]))

    @pl.when(j == 0)
    def _():
        slot_ref[0] = 0
        for cp in fetch(te_ref[0, 0], 0):
            cp.start()

    @pl.when(new_expert)
    def _():
        s = slot_ref[0]
        for cp in fetch(te_ref[j, 0], s):
            cp.wait()
        nxt = te_ref[j, 1]

        @pl.when(nxt >= 0)
        def _():
            for cp in fetch(nxt, 1 - s):
                cp.start()

        wg_b[...] = wg_f[s].astype(BF16)
        wu_b[...] = wu_f[s].astype(BF16)
        wd_b[...] = wd_f[s].astype(BF16)
        slot_ref[0] = 1 - s

    def mlp(rows):
        x = xs_ref[0:rows].reshape(rows, -1)
        y = None
        for lo in range(0, wg_b.shape[1], EXPERT_FF_BLOCK):
            cols = slice(lo, lo + EXPERT_FF_BLOCK)
            a = jnp.dot(x, wg_b[:, cols], preferred_element_type=F32)
            u = jnp.dot(x, wu_b[:, cols], preferred_element_type=F32)
            hid = (a * jax.nn.sigmoid(a) * u).astype(BF16)
            part = jnp.dot(hid, wd_b[cols, :], preferred_element_type=F32)
            y = part if y is None else y + part
        ys_ref[0:rows] = y.astype(BF16).reshape((rows,) + ys_ref.shape[1:])
        if rows < tm:
            ys_ref[rows:] = jnp.zeros((tm - rows,) + ys_ref.shape[1:], ys_ref.dtype)

    tm = xs_ref.shape[0]
    blocks_used = (te_ref[j, 2] + (EXPERT_ROW_BLOCK - 1)) // EXPERT_ROW_BLOCK
    for nblk in range(1, tm // EXPERT_ROW_BLOCK + 1):
        pl.when(valid & (blocks_used == nblk))(functools.partial(mlp, nblk * EXPERT_ROW_BLOCK))

    @pl.when(jnp.logical_not(valid) | (blocks_used == 0))
    def _():
        ys_ref[...] = jnp.zeros_like(ys_ref)


def _experts(tile_table, n_valid, xs, wg, wu, wd):
    R = xs.shape[0]
    slab = xs.shape[1:]
    E, D, Fh = wg.shape
    tm = TM_EXPERT
    n_tiles = R // tm
    rows = lambda j, te, nv: (j, 0, 0)
    rows_in = lambda j, te, nv: (jnp.minimum(j, nv[0, 0] - 1), 0, 0)
    hbm = pl.BlockSpec(memory_space=pl.ANY)
    return pl.pallas_call(
        _experts_kernel,
        grid_spec=pltpu.PrefetchScalarGridSpec(
            num_scalar_prefetch=2,
            grid=(n_tiles,),
            in_specs=[pl.BlockSpec((tm,) + slab, rows_in), hbm, hbm, hbm],
            out_specs=pl.BlockSpec((tm,) + slab, rows),
            scratch_shapes=[pltpu.VMEM((2, D, Fh), F32), pltpu.VMEM((2, D, Fh), F32),
                            pltpu.VMEM((2, Fh, D), F32),
                            pltpu.VMEM((D, Fh), BF16), pltpu.VMEM((D, Fh), BF16),
                            pltpu.VMEM((Fh, D), BF16),
                            pltpu.SMEM((1,), I32), pltpu.SemaphoreType.DMA((2,))]),
        out_shape=jax.ShapeDtypeStruct(xs.shape, xs.dtype),
        compiler_params=_cparams(("arbitrary",)),
        name="experts",
    )(tile_table, n_valid, xs, wg, wu, wd)


def _combine_kernel(tcnt_ref, tstart_ref, lrow_ref, x1_ref, wt_ref, p_ref, gp_ref, wpg_ref, wpp_ref,
                    gf_ref, ys_ref, o_ref, *scratch):
    g = COMBINE_TILES
    nb = N_COMBINE_BUFS
    bufs = tuple(scratch[k * g:(k + 1) * g] for k in range(nb))
    wpg_b, wpp_b, sem = scratch[nb * g:]
    i = pl.program_id(0)
    n = pl.num_programs(0)
    tc = x1_ref.shape[0] // g
    nr = 2 * tc

    def request(tile, live, buf, s):
        lo = 0
        for e in range(N_EXPERTS):
            cnt = jnp.where(live, tcnt_ref[tile, e], 0)
            src = tstart_ref[tile, e]
            for bit in _run_bits(tc):
                off = cnt & ~(2 * bit - 1)

                @pl.when((cnt & bit) != 0)
                def _():
                    pltpu.make_async_copy(ys_ref.at[pl.ds(src + off, bit)],
                                          buf.at[pl.ds(lo + off, bit)], s).start()
            lo = lo + cnt

    ahead = nb - 1

    @pl.when(i == 0)
    def _():
        for k in range(ahead):
            for u in range(g):
                request(jnp.minimum(k, n - 1) * g + u, k < n, bufs[k][u], sem.at[k, u])
        wpg_b[...] = wpg_ref[...].astype(BF16)
        wpp_b[...] = wpp_ref[...].astype(BF16)

    def step(slot):
        cur, far = bufs[slot], bufs[(slot + ahead) % nb]
        for u in range(g):
            pltpu.make_async_copy(ys_ref.at[pl.ds(0, nr)], cur[u], sem.at[slot, u]).wait()
        for u in range(g):
            request(jnp.minimum(i + ahead, n - 1) * g + u, i + ahead < n, far[u],
                    sem.at[(slot + ahead) % nb, u])
        r = lax.broadcasted_iota(I32, (tc, nr), 1)
        for u in range(g):
            tok = slice(u * tc, (u + 1) * tc)
            rows = cur[u][...].reshape(nr, -1)
            local_row = lrow_ref[tok, :]
            y0 = jnp.dot((r == local_row[:, 0:1]).astype(BF16), rows, preferred_element_type=F32)
            y1 = jnp.dot((r == local_row[:, 1:2]).astype(BF16), rows, preferred_element_type=F32)
            wt = wt_ref[tok, :]
            x2 = x1_ref[tok, :] + wt[:, 2:3] * y0 + wt[:, 3:4] * y1
            hp = _rms(x2, gp_ref[...]).astype(BF16)
            gate = jax.nn.sigmoid(jnp.dot(hp, wpg_b[...], preferred_element_type=F32))
            proj = jnp.dot(p_ref[tok, :].astype(BF16), wpp_b[...], preferred_element_type=F32)
            x3 = x2 + gate * proj
            o_ref[tok, :] = _rms(x3, gf_ref[...])

    phase = lax.rem(i, nb)
    for slot in range(nb):
        pl.when(phase == slot)(functools.partial(step, slot))


def _combine(tile_cnt, tile_start, lrow, x1, wts, p2, g_ple, w_ple_gate, w_ple_proj, g_final, ys):
    T, D = x1.shape
    P = p2.shape[1]
    g = COMBINE_TILES
    tm = g * TM_COMBINE
    n = T // tm
    row = lambda i: (i, 0)
    const = lambda i: (0, 0)
    smem = pl.BlockSpec(memory_space=pltpu.SMEM)
    run_buf = pltpu.VMEM((2 * TM_COMBINE,) + ys.shape[1:], ys.dtype)
    return pl.pallas_call(
        _combine_kernel,
        grid=(n,),
        in_specs=[smem, smem, pl.BlockSpec((tm, LANES), row),
                  pl.BlockSpec((tm, D), row), pl.BlockSpec((tm, LANES), row),
                  pl.BlockSpec((tm, P), row), pl.BlockSpec((1, D), const),
                  pl.BlockSpec((D, D), const, **_FETCH_ONCE), pl.BlockSpec((P, D), const, **_FETCH_ONCE),
                  pl.BlockSpec((1, D), const), pl.BlockSpec(memory_space=pl.ANY)],
        out_specs=pl.BlockSpec((tm, D), row),
        out_shape=jax.ShapeDtypeStruct((T, D), F32),
        scratch_shapes=[run_buf] * (N_COMBINE_BUFS * g) + [
                        pltpu.VMEM((D, D), BF16), pltpu.VMEM((P, D), BF16),
                        pltpu.SemaphoreType.DMA((N_COMBINE_BUFS, g))],
        compiler_params=_cparams(("arbitrary",)),
        name="combine",
    )(tile_cnt, tile_start, lrow, x1, wts, p2, g_ple, w_ple_gate, w_ple_proj, g_final, ys)


def _block_diag(w):
    nb, d, _ = w.shape
    eye = jnp.eye(nb, dtype=w.dtype)
    return (eye[:, None, :, None] * w[:, :, None, :]).reshape(nb * d, nb * d)


def kernel(x, p, positions, g_mix, w_in, sinks, conv_w, conv_b, lru_wa, lru_ba, lru_wx, lru_bx,
           lru_lambda, g_attn_out, g_lru_out, w_out, g_ffn, w_router_group, b_router_group,
           w_router_expert, b_router_expert, w_expert_gate, w_expert_up, w_expert_down, g_ple,
           w_ple_gate, w_ple_proj, g_final):
    B, S, D = x.shape
    assert w_in.shape[0] == 1, "single-layer block only"
    T = B * S
    lru_w = conv_w.shape[-1]
    n_rows = 2 * T + N_EXPERTS * TM_EXPERT
    n_tiles = n_rows // TM_EXPERT
    pos3 = positions.reshape(T // TM_INPROJ, TM_INPROJ // LANES, LANES).astype(I32)
    x2 = x.reshape(T, D)
    q, k, v, lx, lg = _inproj(x2, pos3, g_mix[0][None], w_in[0], lru_w)
    mix_a = _attention(q, k, v, sinks[0], g_attn_out[0][None], B, S)
    mix_b = _lru(lx.reshape(B, S, lru_w), lg.reshape(B, S, lru_w), conv_w[0], conv_b[0][None],
                 _block_diag(lru_wa[0]).astype(BF16), lru_ba[0][None],
                 _block_diag(lru_wx[0]).astype(BF16), lru_bx[0][None],
                 lru_lambda[0][None], g_lru_out[0][None]).reshape(T, lru_w)
    n_router = N_GROUPS + N_EXPERTS
    w_router = jnp.pad(jnp.concatenate([w_router_group[0], w_router_expert[0]], axis=1),
                       ((0, 0), (0, LANES - n_router)))
    b_router = jnp.pad(jnp.concatenate([b_router_group[0], b_router_expert[0]]),
                       (0, LANES - n_router))[None]
    w_router_hi = w_router.astype(BF16)
    w_router_lo = (w_router - w_router_hi.astype(F32)).astype(BF16)
    x1, h2, route = _outproj(x2, mix_a, mix_b, w_out[0], g_ffn[0][None],
                             jnp.concatenate([w_router_hi, w_router_lo], axis=1), b_router)
    lrow_d, lrow_c, tile_table, n_valid, meta, tile_cnt, tile_start = _plan(route, n_tiles)
    xs = _dispatch(tile_cnt, tile_start, meta, n_valid, lrow_d, h2, n_rows)
    Fh = w_expert_gate.shape[-1]
    ys = _experts(tile_table, n_valid, xs,
                  w_expert_gate[0].reshape(N_EXPERTS, D, Fh),
                  w_expert_up[0].reshape(N_EXPERTS, D, Fh),
                  w_expert_down[0].reshape(N_EXPERTS, Fh, D))
    out = _combine(tile_cnt, tile_start, lrow_c, x1, route,
                   p[0].reshape(T, -1), g_ple[0][None], w_ple_gate[0], w_ple_proj[0], g_final[None], ys)
    return out.reshape(B, S, D)
```

```python
import functools
import math

import jax
import jax.numpy as jnp
from jax import lax
from jax.experimental import pallas as pl
from jax.experimental.pallas import tpu as pltpu

F32 = jnp.float32
BF16 = jnp.bfloat16
I32 = jnp.int32

EPS = 1e-6
N_HEADS = 8
N_KV_HEADS = 2
HEAD_DIM = 64
WINDOW = 128
ROPE_THETA = 10000.0
CONV_WIDTH = 4
LRU_C = 8.0
N_GROUPS = 4
EXPERTS_PER_GROUP = 8
N_EXPERTS = N_GROUPS * EXPERTS_PER_GROUP
NEG_INF = -1e30
LANES = 128
SUBLANES = 8

TM_INPROJ = 512
TM_PROJ = 512
TQ_ATTN = 1024
TL_LRU = 256
TM_PLAN = 1024
TM_EXPERT = 512
EXPERT_FF_BLOCK = 256
EXPERT_ROW_BLOCK = 128
TM_DISPATCH = 512
N_DISPATCH_BUFS = 3
N_COMBINE_BUFS = 3
COMBINE_TILES = 1
TM_COMBINE = 256
VMEM_LIMIT = 56 * 1024 * 1024


def _rms(x, g):
    ms = jnp.mean(x * x, axis=-1, keepdims=True)
    return x * lax.rsqrt(ms + EPS) * g


_FETCH_ONCE = dict(pipeline_mode=pl.Buffered(1))


def _cparams(sem):
    return pltpu.CompilerParams(dimension_semantics=sem, vmem_limit_bytes=VMEM_LIMIT)


def _inproj_kernel(x_ref, pos_ref, g_ref, w_ref, q_ref, k_ref, v_ref, lx_ref, lg_ref, w_b):
    @pl.when(pl.program_id(0) == 0)
    def _():
        w_b[...] = w_ref[...].astype(BF16)

    h = _rms(x_ref[...], g_ref[...]).astype(BF16)
    proj = jnp.dot(h, w_b[...], preferred_element_type=F32)
    lane = lax.broadcasted_iota(I32, (1, LANES), 1)
    n_freq = HEAD_DIM // 2
    fidx = lax.broadcasted_iota(I32, (n_freq, 1), 0).astype(F32)
    inv_freq = jnp.exp(fidx * (-2.0 / HEAD_DIM * math.log(ROPE_THETA)))
    pos = pos_ref[0].astype(F32)
    cos_blocks, sin_blocks = [], []
    for c in range(pos.shape[0]):
        ang_t = inv_freq * pos[c:c + 1, :]
        cos_blocks.append(jnp.concatenate([jnp.cos(ang_t)] * (LANES // n_freq), axis=0).T)
        sin_blocks.append(jnp.concatenate([jnp.sin(ang_t)] * (LANES // n_freq), axis=0).T)
    cos = jnp.concatenate(cos_blocks, axis=0)
    sin = jnp.concatenate(sin_blocks, axis=0)
    first_half = (lane % HEAD_DIM) < (HEAD_DIM // 2)
    sin_signed = jnp.where(first_half, -sin, sin)

    def rope(t):
        partner = jnp.where(first_half, pltpu.roll(t, LANES - HEAD_DIM // 2, 1),
                            pltpu.roll(t, HEAD_DIM // 2, 1))
        return t * cos + partner * sin_signed

    attn_w = N_HEADS * HEAD_DIM
    scale = HEAD_DIM ** -0.5
    assert math.frexp(scale)[0] == 0.5, "score scale must be a power of two to fold into q exactly"
    for c in range(attn_w // LANES):
        q_ref[:, c * LANES:(c + 1) * LANES] = (
            rope(proj[:, c * LANES:(c + 1) * LANES]) * scale).astype(BF16)
    kv_w = N_KV_HEADS * HEAD_DIM
    assert kv_w == LANES
    k = rope(proj[:, attn_w:attn_w + kv_w])
    v = proj[:, attn_w + kv_w:attn_w + 2 * kv_w]
    k_ref[:, :kv_w] = k.astype(BF16)
    k_ref[:, kv_w:] = pltpu.roll(k, HEAD_DIM, 1).astype(BF16)
    v_ref[:, :kv_w] = v.astype(BF16)
    v_ref[:, kv_w:] = pltpu.roll(v, HEAD_DIM, 1).astype(BF16)
    off_lx = attn_w + 2 * kv_w
    lru_w = lx_ref.shape[1]
    lx_ref[...] = proj[:, off_lx:off_lx + lru_w]
    lg_ref[...] = proj[:, off_lx + lru_w:off_lx + 2 * lru_w]


def _inproj(x2, pos3, g_mix, w_in, lru_w):
    T, D = x2.shape
    tm = TM_INPROJ
    attn_w = N_HEADS * HEAD_DIM
    kv_w = N_KV_HEADS * HEAD_DIM
    row = lambda i: (i, 0)
    const = lambda i: (0, 0)
    return pl.pallas_call(
        _inproj_kernel,
        grid=(T // tm,),
        in_specs=[pl.BlockSpec((tm, D), row),
                  pl.BlockSpec((1, tm // LANES, LANES), lambda i: (i, 0, 0)),
                  pl.BlockSpec((1, D), const), pl.BlockSpec(w_in.shape, const, **_FETCH_ONCE)],
        out_specs=[pl.BlockSpec((tm, attn_w), row), pl.BlockSpec((tm, 2 * kv_w), row),
                   pl.BlockSpec((tm, 2 * kv_w), row), pl.BlockSpec((tm, lru_w), row),
                   pl.BlockSpec((tm, lru_w), row)],
        out_shape=[jax.ShapeDtypeStruct((T, attn_w), BF16),
                   jax.ShapeDtypeStruct((T, 2 * kv_w), BF16),
                   jax.ShapeDtypeStruct((T, 2 * kv_w), BF16), jax.ShapeDtypeStruct((T, lru_w), F32),
                   jax.ShapeDtypeStruct((T, lru_w), F32)],
        scratch_shapes=[pltpu.VMEM(w_in.shape, BF16)],
        compiler_params=_cparams(("arbitrary",)),
        name="inproj",
    )(x2, pos3, g_mix, w_in)


def _attn_kernel(sinks_ref, q_ref, k_ref, v_ref, kp_ref, vp_ref, g_ref, o_ref):
    n = pl.program_id(1)
    blk = WINDOW
    hd = HEAD_DIM
    nqb = q_ref.shape[0] // blk
    n_pairs = N_HEADS // 2
    pairs_per_group = n_pairs // N_KV_HEADS
    lane = lax.broadcasted_iota(I32, (1, LANES), 1)
    first = lane < hd
    assert WINDOW == blk
    from_prev = (lax.broadcasted_iota(I32, (blk, blk), 1) > lax.broadcasted_iota(I32, (blk, blk), 0))
    bd_row = lax.broadcasted_iota(I32, (4 * blk, LANES), 0)
    bd_lane = lax.broadcasted_iota(I32, (4 * blk, LANES), 1)
    ones_bd = ((bd_row < 2 * blk) == (bd_lane < hd)).astype(BF16)
    zero = jnp.zeros((), BF16)

    def block_diag(t):
        same, swapped = t[:, :LANES], t[:, LANES:]
        out = []
        for g in range(N_KV_HEADS):
            top = jnp.where(first, same if g == 0 else swapped, zero)
            bot = jnp.where(first, zero, swapped if g == 0 else same)
            out.append(jnp.concatenate([top, bot], axis=0))
        return out

    for j in range(nqb):
        rows = slice(j * blk, (j + 1) * blk)
        if j == 0:
            k_prev, v_prev = kp_ref[...], vp_ref[...]
        else:
            prev_rows = slice((j - 1) * blk, j * blk)
            k_prev, v_prev = k_ref[prev_rows, :], v_ref[prev_rows, :]
        k_bd = block_diag(jnp.concatenate([k_prev, k_ref[rows, :]], axis=0))
        v_bd = block_diag(jnp.concatenate([v_prev, v_ref[rows, :]], axis=0))
        outs = []
        for p in range(n_pairs):
            g = p // pairs_per_group
            qp = q_ref[rows, p * LANES:(p + 1) * LANES]
            s = lax.dot_general(qp, k_bd[g], (((1,), (1,)), ((), ())), preferred_element_type=F32)
            es, ms = [], []
            for a in range(2):
                s_prev = s[:, 2 * a * blk:(2 * a + 1) * blk]
                s_cur = s[:, (2 * a + 1) * blk:(2 * a + 2) * blk]
                if j == 0:
                    s_prev = jnp.where(n > 0, s_prev, NEG_INF)
                sa = jnp.where(from_prev, s_prev, s_cur)
                m = jnp.maximum(jnp.max(sa, axis=-1, keepdims=True), sinks_ref[2 * p + a])
                ea = jnp.exp(sa - m)
                es += [jnp.where(from_prev, ea, 0.0), jnp.where(from_prev, 0.0, ea)]
                ms.append(m)
            e = jnp.concatenate(es, axis=1).astype(BF16)
            od = jnp.dot(e, jnp.concatenate([v_bd[g], ones_bd], axis=1), preferred_element_type=F32)
            sink_term = jnp.where(first, jnp.exp(sinks_ref[2 * p] - ms[0]),
                                  jnp.exp(sinks_ref[2 * p + 1] - ms[1]))
            outs.append(od[:, :LANES] / (od[:, LANES:] + sink_term))
        attn = jnp.concatenate(outs, axis=1)
        o_ref[rows, :] = _rms(attn, g_ref[...]).astype(BF16)


def _attention(q, k, v, sinks, g_attn, B, S):
    T, attn_w = q.shape
    kv_w = k.shape[1]
    blk = WINDOW
    tq = TQ_ATTN
    nt = S // tq
    cur = lambda b, n: (b * nt + n, 0)
    prev = lambda b, n: (b * (S // blk) + jnp.maximum(n * (tq // blk) - 1, 0), 0)
    return pl.pallas_call(
        _attn_kernel,
        grid=(B, nt),
        in_specs=[pl.BlockSpec(memory_space=pltpu.SMEM),
                  pl.BlockSpec((tq, attn_w), cur),
                  pl.BlockSpec((tq, kv_w), cur), pl.BlockSpec((tq, kv_w), cur),
                  pl.BlockSpec((blk, kv_w), prev), pl.BlockSpec((blk, kv_w), prev),
                  pl.BlockSpec((1, attn_w), lambda b, n: (0, 0))],
        out_specs=pl.BlockSpec((tq, attn_w), cur),
        out_shape=jax.ShapeDtypeStruct((T, attn_w), BF16),
        compiler_params=_cparams(("arbitrary", "arbitrary")),
        name="attn",
    )(sinks, q, k, v, k, v, g_attn)


def _neg_expm1(x, exp_x):
    series = -x * (1.0 + x * (0.5 + x * (1.0 / 6.0 + x * (1.0 / 24.0))))
    return jnp.where(x > -0.02, series, 1.0 - exp_x)


def _gelu_tanh(x):
    c = math.sqrt(2.0 / math.pi)
    return x * (0.5 + 0.5 * jnp.tanh(x * (c + (c * 0.044715) * (x * x))))


def _lru_kernel(lx_ref, lg_ref, cw_ref, cb_ref, wa_ref, ba_ref, wx_ref, bx_ref, lam_ref, g_ref,
                o_ref, xpad, a_scr, b_scr, h_scr, h_carry):
    i = pl.program_id(0)
    B, tl, C = lx_ref.shape
    pad = SUBLANES

    @pl.when(i == 0)
    def _():
        xpad[:, 0:pad, :] = jnp.zeros((B, pad, C), F32)
        h_carry[...] = jnp.zeros_like(h_carry)

    @pl.when(i > 0)
    def _():
        xpad[:, 0:pad, :] = xpad[:, tl:tl + pad, :]

    xpad[:, pad:, :] = lx_ref[...]
    groups = (B * tl // SUBLANES, SUBLANES, C)
    sub = lax.broadcasted_iota(I32, groups, 1)
    x_now = lx_ref[...].reshape(groups)
    x_before = xpad[:, pl.ds(0, tl), :].reshape(groups)
    y = x_now * cw_ref[CONV_WIDTH - 1:CONV_WIDTH, :] + cb_ref[...]
    for s in range(1, CONV_WIDTH):
        shifted = pltpu.roll(jnp.where(sub >= SUBLANES - s, x_before, x_now), s, 1)
        y = y + shifted * cw_ref[CONV_WIDTH - 1 - s:CONV_WIDTH - s, :]
    y2 = y.reshape(B * tl, C)
    yb = y2.astype(BF16)
    gate_a = jnp.dot(yb, wa_ref[...], preferred_element_type=F32) + ba_ref[...]
    gate_x = jnp.dot(yb, wx_ref[...], preferred_element_type=F32) + bx_ref[...]
    r = jax.nn.sigmoid(gate_a)
    ig = jax.nn.sigmoid(gate_x)
    nl = -lam_ref[...]
    softplus = jnp.maximum(nl, 0.0) + jnp.log1p(jnp.exp(-jnp.abs(nl)))
    log_a = (-LRU_C) * r * softplus
    a_all = jnp.exp(log_a)
    one_minus_a2 = _neg_expm1(2.0 * log_a, a_all * a_all)
    b_all = one_minus_a2 * lax.rsqrt(jnp.maximum(one_minus_a2, 1e-30)) * ig * y2
    groups = (B * tl // SUBLANES, SUBLANES, C)
    sub = lax.broadcasted_iota(I32, groups, 1)
    a_cum, b_loc = a_all.reshape(groups), b_all.reshape(groups)
    d = 1
    while d < SUBLANES:
        keep = sub >= d
        a_prev = pltpu.roll(a_cum, d, 1)
        b_prev = pltpu.roll(b_loc, d, 1)
        b_loc = b_loc + jnp.where(keep, a_cum * b_prev, 0.0)
        a_cum = jnp.where(keep, a_cum * a_prev, a_cum)
        d *= 2
    a_scr[...] = a_cum.reshape(B, tl, C)
    b_scr[...] = b_loc.reshape(B, tl, C)

    def carry(g, h_prev):
        rows = pl.ds(pl.multiple_of(g * SUBLANES, SUBLANES), SUBLANES)
        h = b_scr[:, rows, :] + a_scr[:, rows, :] * h_prev
        h_scr[:, rows, :] = h
        return h[:, SUBLANES - 1:SUBLANES, :]

    h_carry[...] = lax.fori_loop(0, tl // SUBLANES, carry, h_carry[...], unroll=4)
    lru = h_scr[...].reshape(B * tl, C) * _gelu_tanh(lg_ref[...].reshape(B * tl, C))
    o_ref[...] = _rms(lru, g_ref[...]).astype(BF16).reshape(B, tl, C)


def _lru(lx3, lg3, conv_w, conv_b, wa_bd, ba, wx_bd, bx, lam, g_lru):
    B, S, C = lx3.shape
    tl = TL_LRU
    blk = lambda i: (0, i, 0)
    const = lambda i: (0, 0)
    vec = pl.BlockSpec((1, C), const)
    return pl.pallas_call(
        _lru_kernel,
        grid=(S // tl,),
        in_specs=[pl.BlockSpec((B, tl, C), blk), pl.BlockSpec((B, tl, C), blk),
                  pl.BlockSpec((CONV_WIDTH, C), const), vec,
                  pl.BlockSpec((C, C), const), vec, pl.BlockSpec((C, C), const), vec, vec, vec],
        out_specs=pl.BlockSpec((B, tl, C), blk),
        out_shape=jax.ShapeDtypeStruct((B, S, C), BF16),
        scratch_shapes=[pltpu.VMEM((B, tl + SUBLANES, C), F32),
                        pltpu.VMEM((B, tl, C), F32), pltpu.VMEM((B, tl, C), F32),
                        pltpu.VMEM((B, tl, C), F32), pltpu.VMEM((B, 1, C), F32)],
        compiler_params=_cparams(("arbitrary",)),
        name="lru",
    )(lx3, lg3, conv_w, conv_b, wa_bd, ba, wx_bd, bx, lam, g_lru)


def _outproj_kernel(x_ref, ma_ref, mb_ref, wo_ref, g_ref, wr_ref, br_ref,
                    x1_ref, h2_ref, route_ref, wo_b):
    @pl.when(pl.program_id(0) == 0)
    def _():
        wo_b[...] = wo_ref[...].astype(BF16)

    half = ma_ref.shape[1]
    x1 = (x_ref[...]
          + jnp.dot(ma_ref[...], wo_b[0:half, :], preferred_element_type=F32)
          + jnp.dot(mb_ref[...], wo_b[half:, :], preferred_element_type=F32))
    x1_ref[...] = x1
    h2 = _rms(x1, g_ref[...])
    h2_ref[...] = h2.astype(BF16).reshape(h2_ref.shape)
    h_hi = h2.astype(BF16)
    h_lo = (h2 - h_hi.astype(F32)).astype(BF16)
    hw = jnp.dot(h_hi, wr_ref[...], preferred_element_type=F32)
    logits = (hw[:, :LANES] + hw[:, LANES:]
              + jnp.dot(h_lo, wr_ref[:, :LANES], preferred_element_type=F32) + br_ref[...])
    tm = logits.shape[0]
    lane = lax.broadcasted_iota(I32, (tm, LANES), 1)
    lane_f = lane.astype(F32)
    big = float(LANES)

    def first_argmax(vals):
        m = jnp.max(vals, axis=-1, keepdims=True)
        idx = jnp.min(jnp.where(vals == m, lane_f, big), axis=-1, keepdims=True)
        return m, idx.astype(I32)

    gl = jnp.where(lane < N_GROUPS, logits, NEG_INF)
    gmax, gidx = first_argmax(gl)
    gsum = jnp.sum(jnp.where(lane < N_GROUPS, jnp.exp(gl - gmax), 0.0), axis=-1, keepdims=True)
    g_top_p = 1.0 / gsum
    lo = N_GROUPS + EXPERTS_PER_GROUP * gidx
    el = jnp.where((lane >= lo) & (lane < lo + EXPERTS_PER_GROUP), logits, NEG_INF)
    m1, i1 = first_argmax(el)
    el2 = jnp.where(lane == i1, NEG_INF, el)
    m2, i2 = first_argmax(el2)
    ratio = jnp.exp(m2 - m1)
    w1 = g_top_p / (1.0 + ratio)
    w2 = g_top_p * ratio / (1.0 + ratio)
    ids = jnp.where(lane == 0, i1 - N_GROUPS, jnp.where(lane == 1, i2 - N_GROUPS, 0)).astype(F32)
    route_ref[...] = jnp.where(lane == 2, w1, jnp.where(lane == 3, w2, ids))


def _outproj(x2, mix_a, mix_b, w_out, g_ffn, w_router, b_router):
    T, D = x2.shape
    tm = TM_PROJ
    half = mix_a.shape[1]
    row = lambda i: (i, 0)
    const = lambda i: (0, 0)
    return pl.pallas_call(
        _outproj_kernel,
        grid=(T // tm,),
        in_specs=[pl.BlockSpec((tm, D), row), pl.BlockSpec((tm, half), row),
                  pl.BlockSpec((tm, half), row), pl.BlockSpec(w_out.shape, const, **_FETCH_ONCE),
                  pl.BlockSpec((1, D), const), pl.BlockSpec((D, 2 * LANES), const),
                  pl.BlockSpec((1, LANES), const)],
        out_specs=[pl.BlockSpec((tm, D), row),
                   pl.BlockSpec((tm, D // LANES, LANES), lambda i: (i, 0, 0)),
                   pl.BlockSpec((tm, LANES), row)],
        out_shape=[jax.ShapeDtypeStruct((T, D), F32),
                   jax.ShapeDtypeStruct((T, D // LANES, LANES), BF16),
                   jax.ShapeDtypeStruct((T, LANES), F32)],
        scratch_shapes=[pltpu.VMEM(w_out.shape, BF16)],
        compiler_params=_cparams(("arbitrary",)),
        name="outproj",
    )(x2, mix_a, mix_b, w_out, g_ffn, w_router, b_router)


def _lane_cumsum(x):
    lane = lax.broadcasted_iota(I32, x.shape, 1)
    shift = 1
    while shift < LANES:
        x = x + jnp.where(lane >= shift, pltpu.roll(x, shift, 1), 0)
        shift *= 2
    return x


def _plan_kernel(eid_ref, lrow_d_ref, lrow_c_ref, te_ref, nv_ref, meta_ref, tcnt_ref, tstart_ref,
                 cnt_scr, tcnt_scr, tpre_scr):
    i = pl.program_id(0)
    tm = eid_ref.shape[0]
    tc = TM_COMBINE
    group = TM_DISPATCH // TM_COMBINE
    nsub = tm // tc
    lane = lax.broadcasted_iota(I32, (tm, LANES), 1)
    eid = eid_ref[:, 0:2].astype(I32)
    sel1 = lane == eid[:, 0:1]
    sel2 = lane == eid[:, 1:2]

    @pl.when(i == 0)
    def _():
        cnt_scr[...] = jnp.zeros_like(cnt_scr)

    onehot = (sel1 | sel2).astype(BF16)
    r = lax.broadcasted_iota(I32, (tm, tm), 0)
    c = lax.broadcasted_iota(I32, (tm, tm), 1)
    strict_lower = (r > c).astype(BF16)
    within = jnp.dot(strict_lower, onehot, preferred_element_type=F32)
    base = cnt_scr[0:1, :]
    onehot_f = onehot.astype(F32)
    row = lax.broadcasted_iota(I32, (tm, LANES), 0)
    srow = lax.broadcasted_iota(I32, (2 * SUBLANES, LANES), 0)
    counts = [jnp.sum(onehot_f[s * tc:(s + 1) * tc], axis=0, keepdims=True) for s in range(nsub)]
    cnt_mat = jnp.zeros((2 * SUBLANES, LANES), F32)
    seen = []
    total = jnp.zeros((1, LANES), F32)
    for s in range(nsub):
        seen.append(total)
        tile = i * nsub + s
        tcnt_scr[pl.ds(tile, 1), :] = counts[s]
        tpre_scr[pl.ds(tile, 1), :] = base + total
        cnt_mat = jnp.where(srow == s, counts[s], cnt_mat)
        cnt_mat = jnp.where(srow == SUBLANES + s // group, cnt_mat + counts[s], cnt_mat)
        total = total + counts[s]
    cnt_i = cnt_mat.astype(I32)
    run_start = (_lane_cumsum(cnt_i) - cnt_i).astype(F32)

    def local_rows(first_tile_of, table_row_of):
        local = None
        for s in range(nsub):
            val = within - seen[first_tile_of(s)] + run_start[table_row_of(s):table_row_of(s) + 1, :]
            local = val if local is None else jnp.where(row >= s * tc, val, local)
        l1 = jnp.sum(jnp.where(sel1, local, 0.0), axis=-1, keepdims=True)
        l2 = jnp.sum(jnp.where(sel2, local, 0.0), axis=-1, keepdims=True)
        return jnp.where(lane == 0, l1, jnp.where(lane == 1, l2, 0.0)).astype(I32)

    lrow_c_ref[...] = local_rows(lambda s: s, lambda s: s)
    lrow_d = local_rows(lambda s: (s // group) * group, lambda s: SUBLANES + s // group)
    lrow_d_ref[...] = lrow_d.astype(F32).T[0:SUBLANES, :].astype(I32)
    cnt_scr[...] = cnt_scr[...] + total

    @pl.when(i == pl.num_programs(0) - 1)
    def _():
        shift = int(math.log2(TM_EXPERT))
        cnt = cnt_scr[...].astype(I32)
        padded = ((cnt + (TM_EXPERT - 1)) >> shift) << shift
        ends = _lane_cumsum(padded)
        nt = te_ref.shape[0]
        end_tile = (ends >> shift)[0:1, :]
        tl_lane = lax.broadcasted_iota(I32, (nt, LANES), 1)
        tile = lax.broadcasted_iota(I32, (nt, LANES), 0)
        done = jnp.where((tl_lane < N_EXPERTS) & (end_tile <= tile), 1.0, 0.0)
        te = jnp.minimum(jnp.sum(done, axis=-1, keepdims=True).astype(I32), N_EXPERTS - 1)
        n_valid = jnp.sum(jnp.where(tl_lane[0:1, :] == N_EXPERTS - 1, end_tile, 0).astype(F32),
                          axis=-1, keepdims=True).astype(I32)
        nv_ref[...] = jnp.broadcast_to(n_valid, nv_ref.shape)
        seg_end = jnp.sum(jnp.where(tl_lane == te, end_tile, 0).astype(F32), axis=-1,
                          keepdims=True).astype(I32)
        nxt = jnp.sum(jnp.where((tl_lane < N_EXPERTS) & (end_tile <= seg_end), 1.0, 0.0), axis=-1,
                      keepdims=True).astype(I32)
        nxt = jnp.where(seg_end < n_valid, jnp.minimum(nxt, N_EXPERTS - 1), -1)
        own = tl_lane == te
        seg_cnt = jnp.sum(jnp.where(own, cnt[0:1, :], 0).astype(F32), axis=-1,
                          keepdims=True).astype(I32)
        seg_tiles = jnp.sum(jnp.where(own, padded[0:1, :] >> shift, 0).astype(F32), axis=-1,
                            keepdims=True).astype(I32)
        tile_in_seg = tile[:, 0:1] - (seg_end - seg_tiles)
        rows_used = jnp.clip(seg_cnt - (tile_in_seg << shift), 0, TM_EXPERT)
        te_ref[...] = jnp.where(tl_lane == 1, nxt, jnp.where(tl_lane == 2, rows_used, te))
        row = lax.broadcasted_iota(I32, cnt.shape, 0)
        meta_ref[...] = jnp.where(row == 0, ends - padded + cnt, jnp.where(row == 1, padded - cnt, 0))
        tcnt_ref[...] = tcnt_scr[...].astype(I32)
        tstart_ref[...] = (tpre_scr[...] + (ends - padded).astype(F32)[0:1, :]).astype(I32)


def _plan(eid, n_tiles):
    T = eid.shape[0]
    tm = TM_PLAN
    steps = T // tm
    n_ctiles = T // TM_COMBINE
    assert TM_DISPATCH % TM_COMBINE == 0 and tm % TM_DISPATCH == 0 and tm // TM_COMBINE <= SUBLANES
    row = lambda i: (i, 0)
    const = lambda i: (0, 0)
    return pl.pallas_call(
        _plan_kernel,
        grid=(steps,),
        in_specs=[pl.BlockSpec((tm, LANES), row)],
        out_specs=[pl.BlockSpec((SUBLANES, tm), lambda i: (0, i)), pl.BlockSpec((tm, LANES), row),
                   pl.BlockSpec((n_tiles, LANES), const), pl.BlockSpec((SUBLANES, LANES), const),
                   pl.BlockSpec((SUBLANES, LANES), const),
                   pl.BlockSpec((n_ctiles, LANES), const), pl.BlockSpec((n_ctiles, LANES), const)],
        out_shape=[jax.ShapeDtypeStruct((SUBLANES, T), I32), jax.ShapeDtypeStruct((T, LANES), I32),
                   jax.ShapeDtypeStruct((n_tiles, LANES), I32),
                   jax.ShapeDtypeStruct((SUBLANES, LANES), I32),
                   jax.ShapeDtypeStruct((SUBLANES, LANES), I32),
                   jax.ShapeDtypeStruct((n_ctiles, LANES), I32),
                   jax.ShapeDtypeStruct((n_ctiles, LANES), I32)],
        scratch_shapes=[pltpu.VMEM((SUBLANES, LANES), F32),
                        pltpu.VMEM((n_ctiles, LANES), F32), pltpu.VMEM((n_ctiles, LANES), F32)],
        compiler_params=_cparams(("arbitrary",)),
        name="plan",
    )(eid)


_PAD_BITS = tuple(1 << b for b in reversed(range(int(math.log2(TM_EXPERT)))))


def _run_bits(max_rows):
    return tuple(1 << b for b in reversed(range(int(math.log2(max_rows)) + 1)))


def _dispatch_kernel(tcnt_ref, tstart_ref, meta_ref, nv_ref, lrow_ref, h_ref,
                     xs_ref, lbuf_0, lbuf_1, lbuf_2, zeros, sem, zsem):
    i = pl.program_id(0)
    n = pl.num_programs(0)
    td = h_ref.shape[0]
    nr = 2 * td
    bufs = (lbuf_0, lbuf_1, lbuf_2)

    def wait_buf(s):
        pltpu.make_async_copy(bufs[s], xs_ref.at[pl.ds(0, nr)], sem.at[s]).wait()

    def step(cur_s):
        prev_s = (cur_s - 1) % N_DISPATCH_BUFS
        cur, prev = bufs[cur_s], bufs[prev_s]

        @pl.when(i >= N_DISPATCH_BUFS)
        def _():
            wait_buf(cur_s)

        local_row = lrow_ref[...].astype(F32)
        r = lax.broadcasted_iota(I32, (nr, td), 0).astype(F32)
        onehot = ((r == local_row[0:1, :]) | (r == local_row[1:2, :])).astype(BF16)
        sorted_rows = jnp.dot(onehot, h_ref[...].reshape(td, -1), preferred_element_type=F32)
        cur[...] = sorted_rows.astype(BF16).reshape(cur.shape)

        group = TM_DISPATCH // TM_COMBINE
        first = jnp.maximum(i - 1, 0) * group
        lo = 0
        for e in range(N_EXPERTS):
            cnt = sum(tcnt_ref[first + u, e] for u in range(group))
            cnt = jnp.where(i >= 1, cnt, 0)
            dst = tstart_ref[first, e]
            for bit in _run_bits(td):
                off = cnt & ~(2 * bit - 1)

                @pl.when((cnt & bit) != 0)
                def _():
                    pltpu.make_async_copy(prev.at[pl.ds(lo + off, bit)],
                                          xs_ref.at[pl.ds(dst + off, bit)], sem.at[prev_s]).start()
            lo = lo + cnt

        @pl.when(i == n - 1)
        def _():
            wait_buf(prev_s)
            if N_DISPATCH_BUFS > 2:
                @pl.when(n >= 3)
                def _():
                    wait_buf((cur_s - 2) % N_DISPATCH_BUFS)

    n_tiles = xs_ref.shape[0] // TM_EXPERT
    first_tail = n_tiles - N_EXPERTS

    def pad_copy(e, bit):
        cnt = meta_ref[1, e]
        dst = meta_ref[0, e] + (cnt & ~(2 * bit - 1))
        return (cnt & bit) != 0, pltpu.make_async_copy(zeros.at[pl.ds(0, bit)],
                                                       xs_ref.at[pl.ds(dst, bit)], zsem)

    def tail_copy(j):
        return j >= nv_ref[0, 0], pltpu.make_async_copy(
            zeros, xs_ref.at[pl.ds(j * TM_EXPERT, TM_EXPERT)], zsem)

    def for_each_fill(act):
        def per_expert(e, c):
            for bit in _PAD_BITS:
                pred, cp = pad_copy(e, bit)
                pl.when(pred)(functools.partial(act, cp))
            return c

        lax.fori_loop(0, N_EXPERTS, per_expert, 0)

        def per_tile(j, c):
            pred, cp = tail_copy(j)
            pl.when(pred)(functools.partial(act, cp))
            return c

        lax.fori_loop(first_tail, n_tiles, per_tile, 0)

    @pl.when(i == 0)
    def _():
        zeros[...] = jnp.zeros_like(zeros)
        for_each_fill(lambda cp: cp.start())

    phase = lax.rem(i, N_DISPATCH_BUFS)
    for s in range(N_DISPATCH_BUFS):
        pl.when(phase == s)(functools.partial(step, s))

    @pl.when(i == n - 1)
    def _():
        for_each_fill(lambda cp: cp.wait())


def _dispatch(tile_cnt, tile_start, meta, n_valid, lrow, h2, n_rows):
    T = h2.shape[0]
    slab = h2.shape[1:]
    td = TM_DISPATCH
    nt = T // td
    smem = pl.BlockSpec(memory_space=pltpu.SMEM)
    return pl.pallas_call(
        _dispatch_kernel,
        grid=(nt + 1,),
        in_specs=[smem, smem, smem, smem,
                  pl.BlockSpec((SUBLANES, td), lambda i: (0, jnp.minimum(i, nt - 1))),
                  pl.BlockSpec((td,) + slab, lambda i: (jnp.minimum(i, nt - 1), 0, 0))],
        out_specs=pl.BlockSpec(memory_space=pl.ANY),
        out_shape=jax.ShapeDtypeStruct((n_rows,) + slab, h2.dtype),
        scratch_shapes=[pltpu.VMEM((2 * td,) + slab, h2.dtype)] * N_DISPATCH_BUFS + [
                        pltpu.VMEM((TM_EXPERT,) + slab, h2.dtype),
                        pltpu.SemaphoreType.DMA((N_DISPATCH_BUFS,)), pltpu.SemaphoreType.DMA(())],
        compiler_params=_cparams(("arbitrary",)),
        name="dispatch",
    )(tile_cnt, tile_start, meta, n_valid, lrow, h2)


def _experts_kernel(te_ref, nv_ref, xs_ref, wg_ref, wu_ref, wd_ref, ys_ref,
                    wg_f, wu_f, wd_f, wg_b, wu_b, wd_b, slot_ref, sem):
    j = pl.program_id(0)
    valid = j < nv_ref[0, 0]
    new_expert = valid & ((j == 0) | (te_ref[j, 0] != te_ref[jnp.maximum(j - 1, 0), 0]))

    def fetch(e, s):
        return (pltpu.make_async_copy(wg_ref.at[e], wg_f.at[s], sem.at[s]),
                pltpu.make_async_copy(wu_ref.at[e], wu_f.at[s], sem.at[s]),
                pltpu.make_async_copy(wd_ref.at[e], wd_f.at[s], sem.at[s]))

    @pl.when(j == 0)
    def _():
        slot_ref[0] = 0
        for cp in fetch(te_ref[0, 0], 0):
            cp.start()

    @pl.when(new_expert)
    def _():
        s = slot_ref[0]
        for cp in fetch(te_ref[j, 0], s):
            cp.wait()
        nxt = te_ref[j, 1]

        @pl.when(nxt >= 0)
        def _():
            for cp in fetch(nxt, 1 - s):
                cp.start()

        wg_b[...] = wg_f[s].astype(BF16)
        wu_b[...] = wu_f[s].astype(BF16)
        wd_b[...] = wd_f[s].astype(BF16)
        slot_ref[0] = 1 - s

    def mlp(rows):
        x = xs_ref[0:rows].reshape(rows, -1)
        y = None
        for lo in range(0, wg_b.shape[1], EXPERT_FF_BLOCK):
            cols = slice(lo, lo + EXPERT_FF_BLOCK)
            a = jnp.dot(x, wg_b[:, cols], preferred_element_type=F32)
            u = jnp.dot(x, wu_b[:, cols], preferred_element_type=F32)
            hid = (a * jax.nn.sigmoid(a) * u).astype(BF16)
            part = jnp.dot(hid, wd_b[cols, :], preferred_element_type=F32)
            y = part if y is None else y + part
        ys_ref[0:rows] = y.astype(BF16).reshape((rows,) + ys_ref.shape[1:])
        if rows < tm:
            ys_ref[rows:] = jnp.zeros((tm - rows,) + ys_ref.shape[1:], ys_ref.dtype)

    tm = xs_ref.shape[0]
    blocks_used = (te_ref[j, 2] + (EXPERT_ROW_BLOCK - 1)) // EXPERT_ROW_BLOCK
    for nblk in range(1, tm // EXPERT_ROW_BLOCK + 1):
        pl.when(valid & (blocks_used == nblk))(functools.partial(mlp, nblk * EXPERT_ROW_BLOCK))

    @pl.when(jnp.logical_not(valid) | (blocks_used == 0))
    def _():
        ys_ref[...] = jnp.zeros_like(ys_ref)


def _experts(tile_table, n_valid, xs, wg, wu, wd):
    R = xs.shape[0]
    slab = xs.shape[1:]
    E, D, Fh = wg.shape
    tm = TM_EXPERT
    n_tiles = R // tm
    rows = lambda j, te, nv: (j, 0, 0)
    rows_in = lambda j, te, nv: (jnp.minimum(j, nv[0, 0] - 1), 0, 0)
    hbm = pl.BlockSpec(memory_space=pl.ANY)
    return pl.pallas_call(
        _experts_kernel,
        grid_spec=pltpu.PrefetchScalarGridSpec(
            num_scalar_prefetch=2,
            grid=(n_tiles,),
            in_specs=[pl.BlockSpec((tm,) + slab, rows_in), hbm, hbm, hbm],
            out_specs=pl.BlockSpec((tm,) + slab, rows),
            scratch_shapes=[pltpu.VMEM((2, D, Fh), F32), pltpu.VMEM((2, D, Fh), F32),
                            pltpu.VMEM((2, Fh, D), F32),
                            pltpu.VMEM((D, Fh), BF16), pltpu.VMEM((D, Fh), BF16),
                            pltpu.VMEM((Fh, D), BF16),
                            pltpu.SMEM((1,), I32), pltpu.SemaphoreType.DMA((2,))]),
        out_shape=jax.ShapeDtypeStruct(xs.shape, xs.dtype),
        compiler_params=_cparams(("arbitrary",)),
        name="experts",
    )(tile_table, n_valid, xs, wg, wu, wd)


def _combine_kernel(tcnt_ref, tstart_ref, lrow_ref, x1_ref, wt_ref, p_ref, gp_ref, wpg_ref, wpp_ref,
                    gf_ref, ys_ref, o_ref, *scratch):
    g = COMBINE_TILES
    nb = N_COMBINE_BUFS
    bufs = tuple(scratch[k * g:(k + 1) * g] for k in range(nb))
    wpg_b, wpp_b, sem = scratch[nb * g:]
    i = pl.program_id(0)
    n = pl.num_programs(0)
    tc = x1_ref.shape[0] // g
    nr = 2 * tc

    def request(tile, live, buf, s):
        lo = 0
        for e in range(N_EXPERTS):
            cnt = jnp.where(live, tcnt_ref[tile, e], 0)
            src = tstart_ref[tile, e]
            for bit in _run_bits(tc):
                off = cnt & ~(2 * bit - 1)

                @pl.when((cnt & bit) != 0)
                def _():
                    pltpu.make_async_copy(ys_ref.at[pl.ds(src + off, bit)],
                                          buf.at[pl.ds(lo + off, bit)], s).start()
            lo = lo + cnt

    ahead = nb - 1

    @pl.when(i == 0)
    def _():
        for k in range(ahead):
            for u in range(g):
                request(jnp.minimum(k, n - 1) * g + u, k < n, bufs[k][u], sem.at[k, u])
        wpg_b[...] = wpg_ref[...].astype(BF16)
        wpp_b[...] = wpp_ref[...].astype(BF16)

    def step(slot):
        cur, far = bufs[slot], bufs[(slot + ahead) % nb]
        for u in range(g):
            pltpu.make_async_copy(ys_ref.at[pl.ds(0, nr)], cur[u], sem.at[slot, u]).wait()
        for u in range(g):
            request(jnp.minimum(i + ahead, n - 1) * g + u, i + ahead < n, far[u],
                    sem.at[(slot + ahead) % nb, u])
        r = lax.broadcasted_iota(I32, (tc, nr), 1)
        for u in range(g):
            tok = slice(u * tc, (u + 1) * tc)
            rows = cur[u][...].reshape(nr, -1)
            local_row = lrow_ref[tok, :]
            y0 = jnp.dot((r == local_row[:, 0:1]).astype(BF16), rows, preferred_element_type=F32)
            y1 = jnp.dot((r == local_row[:, 1:2]).astype(BF16), rows, preferred_element_type=F32)
            wt = wt_ref[tok, :]
            x2 = x1_ref[tok, :] + wt[:, 2:3] * y0 + wt[:, 3:4] * y1
            hp = _rms(x2, gp_ref[...]).astype(BF16)
            gate = jax.nn.sigmoid(jnp.dot(hp, wpg_b[...], preferred_element_type=F32))
            proj = jnp.dot(p_ref[tok, :].astype(BF16), wpp_b[...], preferred_element_type=F32)
            x3 = x2 + gate * proj
            o_ref[tok, :] = _rms(x3, gf_ref[...])

    phase = lax.rem(i, nb)
    for slot in range(nb):
        pl.when(phase == slot)(functools.partial(step, slot))


def _combine(tile_cnt, tile_start, lrow, x1, wts, p2, g_ple, w_ple_gate, w_ple_proj, g_final, ys):
    T, D = x1.shape
    P = p2.shape[1]
    g = COMBINE_TILES
    tm = g * TM_COMBINE
    n = T // tm
    row = lambda i: (i, 0)
    const = lambda i: (0, 0)
    smem = pl.BlockSpec(memory_space=pltpu.SMEM)
    run_buf = pltpu.VMEM((2 * TM_COMBINE,) + ys.shape[1:], ys.dtype)
    return pl.pallas_call(
        _combine_kernel,
        grid=(n,),
        in_specs=[smem, smem, pl.BlockSpec((tm, LANES), row),
                  pl.BlockSpec((tm, D), row), pl.BlockSpec((tm, LANES), row),
                  pl.BlockSpec((tm, P), row), pl.BlockSpec((1, D), const),
                  pl.BlockSpec((D, D), const, **_FETCH_ONCE), pl.BlockSpec((P, D), const, **_FETCH_ONCE),
                  pl.BlockSpec((1, D), const), pl.BlockSpec(memory_space=pl.ANY)],
        out_specs=pl.BlockSpec((tm, D), row),
        out_shape=jax.ShapeDtypeStruct((T, D), F32),
        scratch_shapes=[run_buf] * (N_COMBINE_BUFS * g) + [
                        pltpu.VMEM((D, D), BF16), pltpu.VMEM((P, D), BF16),
                        pltpu.SemaphoreType.DMA((N_COMBINE_BUFS, g))],
        compiler_params=_cparams(("arbitrary",)),
        name="combine",
    )(tile_cnt, tile_start, lrow, x1, wts, p2, g_ple, w_ple_gate, w_ple_proj, g_final, ys)


def _block_diag(w):
    nb, d, _ = w.shape
    eye = jnp.eye(nb, dtype=w.dtype)
    return (eye[:, None, :, None] * w[:, :, None, :]).reshape(nb * d, nb * d)


def kernel(x, p, positions, g_mix, w_in, sinks, conv_w, conv_b, lru_wa, lru_ba, lru_wx, lru_bx,
           lru_lambda, g_attn_out, g_lru_out, w_out, g_ffn, w_router_group, b_router_group,
           w_router_expert, b_router_expert, w_expert_gate, w_expert_up, w_expert_down, g_ple,
           w_ple_gate, w_ple_proj, g_final):
    B, S, D = x.shape
    assert w_in.shape[0] == 1, "single-layer block only"
    T = B * S
    lru_w = conv_w.shape[-1]
    n_rows = 2 * T + N_EXPERTS * TM_EXPERT
    n_tiles = n_rows // TM_EXPERT
    pos3 = positions.reshape(T // TM_INPROJ, TM_INPROJ // LANES, LANES).astype(I32)
    x2 = x.reshape(T, D)
    q, k, v, lx, lg = _inproj(x2, pos3, g_mix[0][None], w_in[0], lru_w)
    mix_a = _attention(q, k, v, sinks[0], g_attn_out[0][None], B, S)
    mix_b = _lru(lx.reshape(B, S, lru_w), lg.reshape(B, S, lru_w), conv_w[0], conv_b[0][None],
                 _block_diag(lru_wa[0]).astype(BF16), lru_ba[0][None],
                 _block_diag(lru_wx[0]).astype(BF16), lru_bx[0][None],
                 lru_lambda[0][None], g_lru_out[0][None]).reshape(T, lru_w)
    n_router = N_GROUPS + N_EXPERTS
    w_router = jnp.pad(jnp.concatenate([w_router_group[0], w_router_expert[0]], axis=1),
                       ((0, 0), (0, LANES - n_router)))
    b_router = jnp.pad(jnp.concatenate([b_router_group[0], b_router_expert[0]]),
                       (0, LANES - n_router))[None]
    w_router_hi = w_router.astype(BF16)
    w_router_lo = (w_router - w_router_hi.astype(F32)).astype(BF16)
    x1, h2, route = _outproj(x2, mix_a, mix_b, w_out[0], g_ffn[0][None],
                             jnp.concatenate([w_router_hi, w_router_lo], axis=1), b_router)
    lrow_d, lrow_c, tile_table, n_valid, meta, tile_cnt, tile_start = _plan(route, n_tiles)
    xs = _dispatch(tile_cnt, tile_start, meta, n_valid, lrow_d, h2, n_rows)
    Fh = w_expert_gate.shape[-1]
    ys = _experts(tile_table, n_valid, xs,
                  w_expert_gate[0].reshape(N_EXPERTS, D, Fh),
                  w_expert_up[0].reshape(N_EXPERTS, D, Fh),
                  w_expert_down[0].reshape(N_EXPERTS, Fh, D))
    out = _combine(tile_cnt, tile_start, lrow_c, x1, route,
                   p[0].reshape(T, -1), g_ple[0][None], w_ple_gate[0], w_ple_proj[0], g_final[None], ys)
    return out.reshape(B, S, D)
```

```python
import functools
import math

import jax
import jax.numpy as jnp
from jax import lax
from jax.experimental import pallas as pl
from jax.experimental.pallas import tpu as pltpu

F32 = jnp.float32
BF16 = jnp.bfloat16
I32 = jnp.int32

EPS = 1e-6
N_HEADS = 8
N_KV_HEADS = 2
HEAD_DIM = 64
WINDOW = 128
ROPE_THETA = 10000.0
CONV_WIDTH = 4
LRU_C = 8.0
N_GROUPS = 4
EXPERTS_PER_GROUP = 8
N_EXPERTS = N_GROUPS * EXPERTS_PER_GROUP
NEG_INF = -1e30
LANES = 128
SUBLANES = 8

TM_INPROJ = 1024
TM_PROJ = 1024
TQ_ATTN = 1024
TL_LRU = 256
TM_PLAN = 1024
TM_EXPERT = 512
EXPERT_FF_BLOCK = 256
EXPERT_ROW_BLOCK = 128
TM_DISPATCH = 512
N_DISPATCH_BUFS = 3
N_COMBINE_BUFS = 3
COMBINE_TILES = 1
TM_COMBINE = 256
VMEM_LIMIT = 56 * 1024 * 1024


def _rms(x, g):
    ms = jnp.mean(x * x, axis=-1, keepdims=True)
    return x * lax.rsqrt(ms + EPS) * g


_FETCH_ONCE = dict(pipeline_mode=pl.Buffered(1))


def _cparams(sem):
    return pltpu.CompilerParams(dimension_semantics=sem, vmem_limit_bytes=VMEM_LIMIT)


def _inproj_kernel(x_ref, pos_ref, g_ref, w_ref, q_ref, k_ref, v_ref, lx_ref, lg_ref, w_b):
    @pl.when(pl.program_id(0) == 0)
    def _():
        w_b[...] = w_ref[...].astype(BF16)

    h = _rms(x_ref[...], g_ref[...]).astype(BF16)
    proj = jnp.dot(h, w_b[...], preferred_element_type=F32)
    lane = lax.broadcasted_iota(I32, (1, LANES), 1)
    n_freq = HEAD_DIM // 2
    fidx = lax.broadcasted_iota(I32, (n_freq, 1), 0).astype(F32)
    inv_freq = jnp.exp(fidx * (-2.0 / HEAD_DIM * math.log(ROPE_THETA)))
    pos = pos_ref[0].astype(F32)
    cos_blocks, sin_blocks = [], []
    for c in range(pos.shape[0]):
        ang_t = inv_freq * pos[c:c + 1, :]
        cos_blocks.append(jnp.concatenate([jnp.cos(ang_t)] * (LANES // n_freq), axis=0).T)
        sin_blocks.append(jnp.concatenate([jnp.sin(ang_t)] * (LANES // n_freq), axis=0).T)
    cos = jnp.concatenate(cos_blocks, axis=0)
    sin = jnp.concatenate(sin_blocks, axis=0)
    first_half = (lane % HEAD_DIM) < (HEAD_DIM // 2)
    sin_signed = jnp.where(first_half, -sin, sin)

    def rope(t):
        partner = jnp.where(first_half, pltpu.roll(t, LANES - HEAD_DIM // 2, 1),
                            pltpu.roll(t, HEAD_DIM // 2, 1))
        return t * cos + partner * sin_signed

    attn_w = N_HEADS * HEAD_DIM
    scale = HEAD_DIM ** -0.5
    assert math.frexp(scale)[0] == 0.5, "score scale must be a power of two to fold into q exactly"
    for c in range(attn_w // LANES):
        q_ref[:, c * LANES:(c + 1) * LANES] = (
            rope(proj[:, c * LANES:(c + 1) * LANES]) * scale).astype(BF16)
    kv_w = N_KV_HEADS * HEAD_DIM
    assert kv_w == LANES
    k = rope(proj[:, attn_w:attn_w + kv_w])
    v = proj[:, attn_w + kv_w:attn_w + 2 * kv_w]
    k_ref[:, :kv_w] = k.astype(BF16)
    k_ref[:, kv_w:] = pltpu.roll(k, HEAD_DIM, 1).astype(BF16)
    v_ref[:, :kv_w] = v.astype(BF16)
    v_ref[:, kv_w:] = pltpu.roll(v, HEAD_DIM, 1).astype(BF16)
    off_lx = attn_w + 2 * kv_w
    lru_w = lx_ref.shape[1]
    lx_ref[...] = proj[:, off_lx:off_lx + lru_w]
    lg_ref[...] = proj[:, off_lx + lru_w:off_lx + 2 * lru_w]


def _inproj(x2, pos3, g_mix, w_in, lru_w):
    T, D = x2.shape
    tm = TM_INPROJ
    attn_w = N_HEADS * HEAD_DIM
    kv_w = N_KV_HEADS * HEAD_DIM
    row = lambda i: (i, 0)
    const = lambda i: (0, 0)
    return pl.pallas_call(
        _inproj_kernel,
        grid=(T // tm,),
        in_specs=[pl.BlockSpec((tm, D), row),
                  pl.BlockSpec((1, tm // LANES, LANES), lambda i: (i, 0, 0)),
                  pl.BlockSpec((1, D), const), pl.BlockSpec(w_in.shape, const, **_FETCH_ONCE)],
        out_specs=[pl.BlockSpec((tm, attn_w), row), pl.BlockSpec((tm, 2 * kv_w), row),
                   pl.BlockSpec((tm, 2 * kv_w), row), pl.BlockSpec((tm, lru_w), row),
                   pl.BlockSpec((tm, lru_w), row)],
        out_shape=[jax.ShapeDtypeStruct((T, attn_w), BF16),
                   jax.ShapeDtypeStruct((T, 2 * kv_w), BF16),
                   jax.ShapeDtypeStruct((T, 2 * kv_w), BF16), jax.ShapeDtypeStruct((T, lru_w), F32),
                   jax.ShapeDtypeStruct((T, lru_w), F32)],
        scratch_shapes=[pltpu.VMEM(w_in.shape, BF16)],
        compiler_params=_cparams(("arbitrary",)),
        name="inproj",
    )(x2, pos3, g_mix, w_in)


def _attn_kernel(sinks_ref, q_ref, k_ref, v_ref, kp_ref, vp_ref, g_ref, o_ref):
    n = pl.program_id(1)
    blk = WINDOW
    hd = HEAD_DIM
    nqb = q_ref.shape[0] // blk
    n_pairs = N_HEADS // 2
    pairs_per_group = n_pairs // N_KV_HEADS
    lane = lax.broadcasted_iota(I32, (1, LANES), 1)
    first = lane < hd
    assert WINDOW == blk
    from_prev = (lax.broadcasted_iota(I32, (blk, blk), 1) > lax.broadcasted_iota(I32, (blk, blk), 0))
    bd_row = lax.broadcasted_iota(I32, (4 * blk, LANES), 0)
    bd_lane = lax.broadcasted_iota(I32, (4 * blk, LANES), 1)
    ones_bd = ((bd_row < 2 * blk) == (bd_lane < hd)).astype(BF16)
    zero = jnp.zeros((), BF16)

    def block_diag(t):
        same, swapped = t[:, :LANES], t[:, LANES:]
        out = []
        for g in range(N_KV_HEADS):
            top = jnp.where(first, same if g == 0 else swapped, zero)
            bot = jnp.where(first, zero, swapped if g == 0 else same)
            out.append(jnp.concatenate([top, bot], axis=0))
        return out

    for j in range(nqb):
        rows = slice(j * blk, (j + 1) * blk)
        if j == 0:
            k_prev, v_prev = kp_ref[...], vp_ref[...]
        else:
            prev_rows = slice((j - 1) * blk, j * blk)
            k_prev, v_prev = k_ref[prev_rows, :], v_ref[prev_rows, :]
        k_bd = block_diag(jnp.concatenate([k_prev, k_ref[rows, :]], axis=0))
        v_bd = block_diag(jnp.concatenate([v_prev, v_ref[rows, :]], axis=0))
        outs = []
        for p in range(n_pairs):
            g = p // pairs_per_group
            qp = q_ref[rows, p * LANES:(p + 1) * LANES]
            s = lax.dot_general(qp, k_bd[g], (((1,), (1,)), ((), ())), preferred_element_type=F32)
            es, ms = [], []
            for a in range(2):
                s_prev = s[:, 2 * a * blk:(2 * a + 1) * blk]
                s_cur = s[:, (2 * a + 1) * blk:(2 * a + 2) * blk]
                if j == 0:
                    s_prev = jnp.where(n > 0, s_prev, NEG_INF)
                sa = jnp.where(from_prev, s_prev, s_cur)
                m = jnp.maximum(jnp.max(sa, axis=-1, keepdims=True), sinks_ref[2 * p + a])
                ea = jnp.exp(sa - m)
                es += [jnp.where(from_prev, ea, 0.0), jnp.where(from_prev, 0.0, ea)]
                ms.append(m)
            e = jnp.concatenate(es, axis=1).astype(BF16)
            od = jnp.dot(e, jnp.concatenate([v_bd[g], ones_bd], axis=1), preferred_element_type=F32)
            sink_term = jnp.where(first, jnp.exp(sinks_ref[2 * p] - ms[0]),
                                  jnp.exp(sinks_ref[2 * p + 1] - ms[1]))
            outs.append(od[:, :LANES] / (od[:, LANES:] + sink_term))
        attn = jnp.concatenate(outs, axis=1)
        o_ref[rows, :] = _rms(attn, g_ref[...]).astype(BF16)


def _attention(q, k, v, sinks, g_attn, B, S):
    T, attn_w = q.shape
    kv_w = k.shape[1]
    blk = WINDOW
    tq = TQ_ATTN
    nt = S // tq
    cur = lambda b, n: (b * nt + n, 0)
    prev = lambda b, n: (b * (S // blk) + jnp.maximum(n * (tq // blk) - 1, 0), 0)
    return pl.pallas_call(
        _attn_kernel,
        grid=(B, nt),
        in_specs=[pl.BlockSpec(memory_space=pltpu.SMEM),
                  pl.BlockSpec((tq, attn_w), cur),
                  pl.BlockSpec((tq, kv_w), cur), pl.BlockSpec((tq, kv_w), cur),
                  pl.BlockSpec((blk, kv_w), prev), pl.BlockSpec((blk, kv_w), prev),
                  pl.BlockSpec((1, attn_w), lambda b, n: (0, 0))],
        out_specs=pl.BlockSpec((tq, attn_w), cur),
        out_shape=jax.ShapeDtypeStruct((T, attn_w), BF16),
        compiler_params=_cparams(("arbitrary", "arbitrary")),
        name="attn",
    )(sinks, q, k, v, k, v, g_attn)


def _neg_expm1(x, exp_x):
    series = -x * (1.0 + x * (0.5 + x * (1.0 / 6.0 + x * (1.0 / 24.0))))
    return jnp.where(x > -0.02, series, 1.0 - exp_x)


def _gelu_tanh(x):
    c = math.sqrt(2.0 / math.pi)
    return x * (0.5 + 0.5 * jnp.tanh(x * (c + (c * 0.044715) * (x * x))))


def _lru_kernel(lx_ref, lg_ref, cw_ref, cb_ref, wa_ref, ba_ref, wx_ref, bx_ref, lam_ref, g_ref,
                o_ref, xpad, a_scr, b_scr, h_scr, h_carry):
    i = pl.program_id(0)
    B, tl, C = lx_ref.shape
    pad = SUBLANES

    @pl.when(i == 0)
    def _():
        xpad[:, 0:pad, :] = jnp.zeros((B, pad, C), F32)
        h_carry[...] = jnp.zeros_like(h_carry)

    @pl.when(i > 0)
    def _():
        xpad[:, 0:pad, :] = xpad[:, tl:tl + pad, :]

    xpad[:, pad:, :] = lx_ref[...]
    groups = (B * tl // SUBLANES, SUBLANES, C)
    sub = lax.broadcasted_iota(I32, groups, 1)
    x_now = lx_ref[...].reshape(groups)
    x_before = xpad[:, pl.ds(0, tl), :].reshape(groups)
    y = x_now * cw_ref[CONV_WIDTH - 1:CONV_WIDTH, :] + cb_ref[...]
    for s in range(1, CONV_WIDTH):
        shifted = pltpu.roll(jnp.where(sub >= SUBLANES - s, x_before, x_now), s, 1)
        y = y + shifted * cw_ref[CONV_WIDTH - 1 - s:CONV_WIDTH - s, :]
    y2 = y.reshape(B * tl, C)
    yb = y2.astype(BF16)
    gate_a = jnp.dot(yb, wa_ref[...], preferred_element_type=F32) + ba_ref[...]
    gate_x = jnp.dot(yb, wx_ref[...], preferred_element_type=F32) + bx_ref[...]
    r = jax.nn.sigmoid(gate_a)
    ig = jax.nn.sigmoid(gate_x)
    nl = -lam_ref[...]
    softplus = jnp.maximum(nl, 0.0) + jnp.log1p(jnp.exp(-jnp.abs(nl)))
    log_a = (-LRU_C) * r * softplus
    a_all = jnp.exp(log_a)
    one_minus_a2 = _neg_expm1(2.0 * log_a, a_all * a_all)
    b_all = one_minus_a2 * lax.rsqrt(jnp.maximum(one_minus_a2, 1e-30)) * ig * y2
    groups = (B * tl // SUBLANES, SUBLANES, C)
    sub = lax.broadcasted_iota(I32, groups, 1)
    a_cum, b_loc = a_all.reshape(groups), b_all.reshape(groups)
    d = 1
    while d < SUBLANES:
        keep = sub >= d
        a_prev = pltpu.roll(a_cum, d, 1)
        b_prev = pltpu.roll(b_loc, d, 1)
        b_loc = b_loc + jnp.where(keep, a_cum * b_prev, 0.0)
        a_cum = jnp.where(keep, a_cum * a_prev, a_cum)
        d *= 2
    a_scr[...] = a_cum.reshape(B, tl, C)
    b_scr[...] = b_loc.reshape(B, tl, C)

    def carry(g, h_prev):
        rows = pl.ds(pl.multiple_of(g * SUBLANES, SUBLANES), SUBLANES)
        h = b_scr[:, rows, :] + a_scr[:, rows, :] * h_prev
        h_scr[:, rows, :] = h
        return h[:, SUBLANES - 1:SUBLANES, :]

    h_carry[...] = lax.fori_loop(0, tl // SUBLANES, carry, h_carry[...], unroll=4)
    lru = h_scr[...].reshape(B * tl, C) * _gelu_tanh(lg_ref[...].reshape(B * tl, C))
    o_ref[...] = _rms(lru, g_ref[...]).astype(BF16).reshape(B, tl, C)


def _lru(lx3, lg3, conv_w, conv_b, wa_bd, ba, wx_bd, bx, lam, g_lru):
    B, S, C = lx3.shape
    tl = TL_LRU
    blk = lambda i: (0, i, 0)
    const = lambda i: (0, 0)
    vec = pl.BlockSpec((1, C), const)
    return pl.pallas_call(
        _lru_kernel,
        grid=(S // tl,),
        in_specs=[pl.BlockSpec((B, tl, C), blk), pl.BlockSpec((B, tl, C), blk),
                  pl.BlockSpec((CONV_WIDTH, C), const), vec,
                  pl.BlockSpec((C, C), const), vec, pl.BlockSpec((C, C), const), vec, vec, vec],
        out_specs=pl.BlockSpec((B, tl, C), blk),
        out_shape=jax.ShapeDtypeStruct((B, S, C), BF16),
        scratch_shapes=[pltpu.VMEM((B, tl + SUBLANES, C), F32),
                        pltpu.VMEM((B, tl, C), F32), pltpu.VMEM((B, tl, C), F32),
                        pltpu.VMEM((B, tl, C), F32), pltpu.VMEM((B, 1, C), F32)],
        compiler_params=_cparams(("arbitrary",)),
        name="lru",
    )(lx3, lg3, conv_w, conv_b, wa_bd, ba, wx_bd, bx, lam, g_lru)


def _outproj_kernel(x_ref, ma_ref, mb_ref, wo_ref, g_ref, wr_ref, br_ref,
                    x1_ref, h2_ref, route_ref, wo_b):
    @pl.when(pl.program_id(0) == 0)
    def _():
        wo_b[...] = wo_ref[...].astype(BF16)

    half = ma_ref.shape[1]
    x1 = (x_ref[...]
          + jnp.dot(ma_ref[...], wo_b[0:half, :], preferred_element_type=F32)
          + jnp.dot(mb_ref[...], wo_b[half:, :], preferred_element_type=F32))
    x1_ref[...] = x1
    h2 = _rms(x1, g_ref[...])
    h2_ref[...] = h2.astype(BF16).reshape(h2_ref.shape)
    h_hi = h2.astype(BF16)
    h_lo = (h2 - h_hi.astype(F32)).astype(BF16)
    hw = jnp.dot(h_hi, wr_ref[...], preferred_element_type=F32)
    logits = (hw[:, :LANES] + hw[:, LANES:]
              + jnp.dot(h_lo, wr_ref[:, :LANES], preferred_element_type=F32) + br_ref[...])
    tm = logits.shape[0]
    lane = lax.broadcasted_iota(I32, (tm, LANES), 1)
    lane_f = lane.astype(F32)
    big = float(LANES)

    def first_argmax(vals):
        m = jnp.max(vals, axis=-1, keepdims=True)
        idx = jnp.min(jnp.where(vals == m, lane_f, big), axis=-1, keepdims=True)
        return m, idx.astype(I32)

    gl = jnp.where(lane < N_GROUPS, logits, NEG_INF)
    gmax, gidx = first_argmax(gl)
    gsum = jnp.sum(jnp.where(lane < N_GROUPS, jnp.exp(gl - gmax), 0.0), axis=-1, keepdims=True)
    g_top_p = 1.0 / gsum
    lo = N_GROUPS + EXPERTS_PER_GROUP * gidx
    el = jnp.where((lane >= lo) & (lane < lo + EXPERTS_PER_GROUP), logits, NEG_INF)
    m1, i1 = first_argmax(el)
    el2 = jnp.where(lane == i1, NEG_INF, el)
    m2, i2 = first_argmax(el2)
    ratio = jnp.exp(m2 - m1)
    w1 = g_top_p / (1.0 + ratio)
    w2 = g_top_p * ratio / (1.0 + ratio)
    ids = jnp.where(lane == 0, i1 - N_GROUPS, jnp.where(lane == 1, i2 - N_GROUPS, 0)).astype(F32)
    route_ref[...] = jnp.where(lane == 2, w1, jnp.where(lane == 3, w2, ids))


def _outproj(x2, mix_a, mix_b, w_out, g_ffn, w_router, b_router):
    T, D = x2.shape
    tm = TM_PROJ
    half = mix_a.shape[1]
    row = lambda i: (i, 0)
    const = lambda i: (0, 0)
    return pl.pallas_call(
        _outproj_kernel,
        grid=(T // tm,),
        in_specs=[pl.BlockSpec((tm, D), row), pl.BlockSpec((tm, half), row),
                  pl.BlockSpec((tm, half), row), pl.BlockSpec(w_out.shape, const, **_FETCH_ONCE),
                  pl.BlockSpec((1, D), const), pl.BlockSpec((D, 2 * LANES), const),
                  pl.BlockSpec((1, LANES), const)],
        out_specs=[pl.BlockSpec((tm, D), row),
                   pl.BlockSpec((tm, D // LANES, LANES), lambda i: (i, 0, 0)),
                   pl.BlockSpec((tm, LANES), row)],
        out_shape=[jax.ShapeDtypeStruct((T, D), F32),
                   jax.ShapeDtypeStruct((T, D // LANES, LANES), BF16),
                   jax.ShapeDtypeStruct((T, LANES), F32)],
        scratch_shapes=[pltpu.VMEM(w_out.shape, BF16)],
        compiler_params=_cparams(("arbitrary",)),
        name="outproj",
    )(x2, mix_a, mix_b, w_out, g_ffn, w_router, b_router)


def _lane_cumsum(x):
    lane = lax.broadcasted_iota(I32, x.shape, 1)
    shift = 1
    while shift < LANES:
        x = x + jnp.where(lane >= shift, pltpu.roll(x, shift, 1), 0)
        shift *= 2
    return x


def _plan_kernel(eid_ref, lrow_d_ref, lrow_c_ref, te_ref, nv_ref, meta_ref, tcnt_ref, tstart_ref,
                 cnt_scr, tcnt_scr, tpre_scr):
    i = pl.program_id(0)
    tm = eid_ref.shape[0]
    tc = TM_COMBINE
    group = TM_DISPATCH // TM_COMBINE
    nsub = tm // tc
    lane = lax.broadcasted_iota(I32, (tm, LANES), 1)
    eid = eid_ref[:, 0:2].astype(I32)
    sel1 = lane == eid[:, 0:1]
    sel2 = lane == eid[:, 1:2]

    @pl.when(i == 0)
    def _():
        cnt_scr[...] = jnp.zeros_like(cnt_scr)

    onehot = (sel1 | sel2).astype(BF16)
    r = lax.broadcasted_iota(I32, (tm, tm), 0)
    c = lax.broadcasted_iota(I32, (tm, tm), 1)
    strict_lower = (r > c).astype(BF16)
    within = jnp.dot(strict_lower, onehot, preferred_element_type=F32)
    base = cnt_scr[0:1, :]
    onehot_f = onehot.astype(F32)
    row = lax.broadcasted_iota(I32, (tm, LANES), 0)
    srow = lax.broadcasted_iota(I32, (2 * SUBLANES, LANES), 0)
    counts = [jnp.sum(onehot_f[s * tc:(s + 1) * tc], axis=0, keepdims=True) for s in range(nsub)]
    cnt_mat = jnp.zeros((2 * SUBLANES, LANES), F32)
    seen = []
    total = jnp.zeros((1, LANES), F32)
    for s in range(nsub):
        seen.append(total)
        tile = i * nsub + s
        tcnt_scr[pl.ds(tile, 1), :] = counts[s]
        tpre_scr[pl.ds(tile, 1), :] = base + total
        cnt_mat = jnp.where(srow == s, counts[s], cnt_mat)
        cnt_mat = jnp.where(srow == SUBLANES + s // group, cnt_mat + counts[s], cnt_mat)
        total = total + counts[s]
    cnt_i = cnt_mat.astype(I32)
    run_start = (_lane_cumsum(cnt_i) - cnt_i).astype(F32)

    def local_rows(first_tile_of, table_row_of):
        local = None
        for s in range(nsub):
            val = within - seen[first_tile_of(s)] + run_start[table_row_of(s):table_row_of(s) + 1, :]
            local = val if local is None else jnp.where(row >= s * tc, val, local)
        l1 = jnp.sum(jnp.where(sel1, local, 0.0), axis=-1, keepdims=True)
        l2 = jnp.sum(jnp.where(sel2, local, 0.0), axis=-1, keepdims=True)
        return jnp.where(lane == 0, l1, jnp.where(lane == 1, l2, 0.0)).astype(I32)

    lrow_c_ref[...] = local_rows(lambda s: s, lambda s: s)
    lrow_d = local_rows(lambda s: (s // group) * group, lambda s: SUBLANES + s // group)
    lrow_d_ref[...] = lrow_d.astype(F32).T[0:SUBLANES, :].astype(I32)
    cnt_scr[...] = cnt_scr[...] + total

    @pl.when(i == pl.num_programs(0) - 1)
    def _():
        shift = int(math.log2(TM_EXPERT))
        cnt = cnt_scr[...].astype(I32)
        padded = ((cnt + (TM_EXPERT - 1)) >> shift) << shift
        ends = _lane_cumsum(padded)
        nt = te_ref.shape[0]
        end_tile = (ends >> shift)[0:1, :]
        tl_lane = lax.broadcasted_iota(I32, (nt, LANES), 1)
        tile = lax.broadcasted_iota(I32, (nt, LANES), 0)
        done = jnp.where((tl_lane < N_EXPERTS) & (end_tile <= tile), 1.0, 0.0)
        te = jnp.minimum(jnp.sum(done, axis=-1, keepdims=True).astype(I32), N_EXPERTS - 1)
        n_valid = jnp.sum(jnp.where(tl_lane[0:1, :] == N_EXPERTS - 1, end_tile, 0).astype(F32),
                          axis=-1, keepdims=True).astype(I32)
        nv_ref[...] = jnp.broadcast_to(n_valid, nv_ref.shape)
        seg_end = jnp.sum(jnp.where(tl_lane == te, end_tile, 0).astype(F32), axis=-1,
                          keepdims=True).astype(I32)
        nxt = jnp.sum(jnp.where((tl_lane < N_EXPERTS) & (end_tile <= seg_end), 1.0, 0.0), axis=-1,
                      keepdims=True).astype(I32)
        nxt = jnp.where(seg_end < n_valid, jnp.minimum(nxt, N_EXPERTS - 1), -1)
        own = tl_lane == te
        seg_cnt = jnp.sum(jnp.where(own, cnt[0:1, :], 0).astype(F32), axis=-1,
                          keepdims=True).astype(I32)
        seg_tiles = jnp.sum(jnp.where(own, padded[0:1, :] >> shift, 0).astype(F32), axis=-1,
                            keepdims=True).astype(I32)
        tile_in_seg = tile[:, 0:1] - (seg_end - seg_tiles)
        rows_used = jnp.clip(seg_cnt - (tile_in_seg << shift), 0, TM_EXPERT)
        te_ref[...] = jnp.where(tl_lane == 1, nxt, jnp.where(tl_lane == 2, rows_used, te))
        row = lax.broadcasted_iota(I32, cnt.shape, 0)
        meta_ref[...] = jnp.where(row == 0, ends - padded + cnt, jnp.where(row == 1, padded - cnt, 0))
        tcnt_ref[...] = tcnt_scr[...].astype(I32)
        tstart_ref[...] = (tpre_scr[...] + (ends - padded).astype(F32)[0:1, :]).astype(I32)


def _plan(eid, n_tiles):
    T = eid.shape[0]
    tm = TM_PLAN
    steps = T // tm
    n_ctiles = T // TM_COMBINE
    assert TM_DISPATCH % TM_COMBINE == 0 and tm % TM_DISPATCH == 0 and tm // TM_COMBINE <= SUBLANES
    row = lambda i: (i, 0)
    const = lambda i: (0, 0)
    return pl.pallas_call(
        _plan_kernel,
        grid=(steps,),
        in_specs=[pl.BlockSpec((tm, LANES), row)],
        out_specs=[pl.BlockSpec((SUBLANES, tm), lambda i: (0, i)), pl.BlockSpec((tm, LANES), row),
                   pl.BlockSpec((n_tiles, LANES), const), pl.BlockSpec((SUBLANES, LANES), const),
                   pl.BlockSpec((SUBLANES, LANES), const),
                   pl.BlockSpec((n_ctiles, LANES), const), pl.BlockSpec((n_ctiles, LANES), const)],
        out_shape=[jax.ShapeDtypeStruct((SUBLANES, T), I32), jax.ShapeDtypeStruct((T, LANES), I32),
                   jax.ShapeDtypeStruct((n_tiles, LANES), I32),
                   jax.ShapeDtypeStruct((SUBLANES, LANES), I32),
                   jax.ShapeDtypeStruct((SUBLANES, LANES), I32),
                   jax.ShapeDtypeStruct((n_ctiles, LANES), I32),
                   jax.ShapeDtypeStruct((n_ctiles, LANES), I32)],
        scratch_shapes=[pltpu.VMEM((SUBLANES, LANES), F32),
                        pltpu.VMEM((n_ctiles, LANES), F32), pltpu.VMEM((n_ctiles, LANES), F32)],
        compiler_params=_cparams(("arbitrary",)),
        name="plan",
    )(eid)


_PAD_BITS = tuple(1 << b for b in reversed(range(int(math.log2(TM_EXPERT)))))


def _run_bits(max_rows):
    return tuple(1 << b for b in reversed(range(int(math.log2(max_rows)) + 1)))


def _dispatch_kernel(tcnt_ref, tstart_ref, meta_ref, nv_ref, lrow_ref, h_ref,
                     xs_ref, lbuf_0, lbuf_1, lbuf_2, zeros, sem, zsem):
    i = pl.program_id(0)
    n = pl.num_programs(0)
    td = h_ref.shape[0]
    nr = 2 * td
    bufs = (lbuf_0, lbuf_1, lbuf_2)

    def wait_buf(s):
        pltpu.make_async_copy(bufs[s], xs_ref.at[pl.ds(0, nr)], sem.at[s]).wait()

    def step(cur_s):
        prev_s = (cur_s - 1) % N_DISPATCH_BUFS
        cur, prev = bufs[cur_s], bufs[prev_s]

        @pl.when(i >= N_DISPATCH_BUFS)
        def _():
            wait_buf(cur_s)

        local_row = lrow_ref[...].astype(F32)
        r = lax.broadcasted_iota(I32, (nr, td), 0).astype(F32)
        onehot = ((r == local_row[0:1, :]) | (r == local_row[1:2, :])).astype(BF16)
        sorted_rows = jnp.dot(onehot, h_ref[...].reshape(td, -1), preferred_element_type=F32)
        cur[...] = sorted_rows.astype(BF16).reshape(cur.shape)

        group = TM_DISPATCH // TM_COMBINE
        first = jnp.maximum(i - 1, 0) * group
        lo = 0
        for e in range(N_EXPERTS):
            cnt = sum(tcnt_ref[first + u, e] for u in range(group))
            cnt = jnp.where(i >= 1, cnt, 0)
            dst = tstart_ref[first, e]
            for bit in _run_bits(td):
                off = cnt & ~(2 * bit - 1)

                @pl.when((cnt & bit) != 0)
                def _():
                    pltpu.make_async_copy(prev.at[pl.ds(lo + off, bit)],
                                          xs_ref.at[pl.ds(dst + off, bit)], sem.at[prev_s]).start()
            lo = lo + cnt

        @pl.when(i == n - 1)
        def _():
            wait_buf(prev_s)
            if N_DISPATCH_BUFS > 2:
                @pl.when(n >= 3)
                def _():
                    wait_buf((cur_s - 2) % N_DISPATCH_BUFS)

    n_tiles = xs_ref.shape[0] // TM_EXPERT
    first_tail = n_tiles - N_EXPERTS

    def pad_copy(e, bit):
        cnt = meta_ref[1, e]
        dst = meta_ref[0, e] + (cnt & ~(2 * bit - 1))
        return (cnt & bit) != 0, pltpu.make_async_copy(zeros.at[pl.ds(0, bit)],
                                                       xs_ref.at[pl.ds(dst, bit)], zsem)

    def tail_copy(j):
        return j >= nv_ref[0, 0], pltpu.make_async_copy(
            zeros, xs_ref.at[pl.ds(j * TM_EXPERT, TM_EXPERT)], zsem)

    def for_each_fill(act):
        def per_expert(e, c):
            for bit in _PAD_BITS:
                pred, cp = pad_copy(e, bit)
                pl.when(pred)(functools.partial(act, cp))
            return c

        lax.fori_loop(0, N_EXPERTS, per_expert, 0)

        def per_tile(j, c):
            pred, cp = tail_copy(j)
            pl.when(pred)(functools.partial(act, cp))
            return c

        lax.fori_loop(first_tail, n_tiles, per_tile, 0)

    @pl.when(i == 0)
    def _():
        zeros[...] = jnp.zeros_like(zeros)
        for_each_fill(lambda cp: cp.start())

    phase = lax.rem(i, N_DISPATCH_BUFS)
    for s in range(N_DISPATCH_BUFS):
        pl.when(phase == s)(functools.partial(step, s))

    @pl.when(i == n - 1)
    def _():
        for_each_fill(lambda cp: cp.wait())


def _dispatch(tile_cnt, tile_start, meta, n_valid, lrow, h2, n_rows):
    T = h2.shape[0]
    slab = h2.shape[1:]
    td = TM_DISPATCH
    nt = T // td
    smem = pl.BlockSpec(memory_space=pltpu.SMEM)
    return pl.pallas_call(
        _dispatch_kernel,
        grid=(nt + 1,),
        in_specs=[smem, smem, smem, smem,
                  pl.BlockSpec((SUBLANES, td), lambda i: (0, jnp.minimum(i, nt - 1))),
                  pl.BlockSpec((td,) + slab, lambda i: (jnp.minimum(i, nt - 1), 0, 0))],
        out_specs=pl.BlockSpec(memory_space=pl.ANY),
        out_shape=jax.ShapeDtypeStruct((n_rows,) + slab, h2.dtype),
        scratch_shapes=[pltpu.VMEM((2 * td,) + slab, h2.dtype)] * N_DISPATCH_BUFS + [
                        pltpu.VMEM((TM_EXPERT,) + slab, h2.dtype),
                        pltpu.SemaphoreType.DMA((N_DISPATCH_BUFS,)), pltpu.SemaphoreType.DMA(())],
        compiler_params=_cparams(("arbitrary",)),
        name="dispatch",
    )(tile_cnt, tile_start, meta, n_valid, lrow, h2)


def _experts_kernel(te_ref, nv_ref, xs_ref, wg_ref, wu_ref, wd_ref, ys_ref,
                    wg_f, wu_f, wd_f, wg_b, wu_b, wd_b, slot_ref, sem):
    j = pl.program_id(0)
    valid = j < nv_ref[0, 0]
    new_expert = valid & ((j == 0) | (te_ref[j, 0] != te_ref[jnp.maximum(j - 1, 0), 0]))

    def fetch(e, s):
        return (pltpu.make_async_copy(wg_ref.at[e], wg_f.at[s], sem.at[s]),
                pltpu.make_async_copy(wu_ref.at[e], wu_f.at[s], sem.at[s]),
                pltpu.make_async_copy(wd_ref.at[e], wd_f.at[s], sem.at[s]))

    @pl.when(j == 0)
    def _():
        slot_ref[0] = 0
        for cp in fetch(te_ref[0, 0], 0):
            cp.start()

    @pl.when(new_expert)
    def _():
        s = slot_ref[0]
        for cp in fetch(te_ref[j, 0], s):
            cp.wait()
        nxt = te_ref[j, 1]

        @pl.when(nxt >= 0)
        def _():
            for cp in fetch(nxt, 1 - s):
                cp.start()

        wg_b[...] = wg_f[s].astype(BF16)
        wu_b[...] = wu_f[s].astype(BF16)
        wd_b[...] = wd_f[s].astype(BF16)
        slot_ref[0] = 1 - s

    def mlp(rows):
        x = xs_ref[0:rows].reshape(rows, -1)
        y = None
        for lo in range(0, wg_b.shape[1], EXPERT_FF_BLOCK):
            cols = slice(lo, lo + EXPERT_FF_BLOCK)
            a = jnp.dot(x, wg_b[:, cols], preferred_element_type=F32)
            u = jnp.dot(x, wu_b[:, cols], preferred_element_type=F32)
            hid = (a * jax.nn.sigmoid(a) * u).astype(BF16)
            part = jnp.dot(hid, wd_b[cols, :], preferred_element_type=F32)
            y = part if y is None else y + part
        ys_ref[0:rows] = y.astype(BF16).reshape((rows,) + ys_ref.shape[1:])
        if rows < tm:
            ys_ref[rows:] = jnp.zeros((tm - rows,) + ys_ref.shape[1:], ys_ref.dtype)

    tm = xs_ref.shape[0]
    blocks_used = (te_ref[j, 2] + (EXPERT_ROW_BLOCK - 1)) // EXPERT_ROW_BLOCK
    for nblk in range(1, tm // EXPERT_ROW_BLOCK + 1):
        pl.when(valid & (blocks_used == nblk))(functools.partial(mlp, nblk * EXPERT_ROW_BLOCK))

    @pl.when(jnp.logical_not(valid) | (blocks_used == 0))
    def _():
        ys_ref[...] = jnp.zeros_like(ys_ref)


def _experts(tile_table, n_valid, xs, wg, wu, wd):
    R = xs.shape[0]
    slab = xs.shape[1:]
    E, D, Fh = wg.shape
    tm = TM_EXPERT
    n_tiles = R // tm
    rows = lambda j, te, nv: (j, 0, 0)
    rows_in = lambda j, te, nv: (jnp.minimum(j, nv[0, 0] - 1), 0, 0)
    hbm = pl.BlockSpec(memory_space=pl.ANY)
    return pl.pallas_call(
        _experts_kernel,
        grid_spec=pltpu.PrefetchScalarGridSpec(
            num_scalar_prefetch=2,
            grid=(n_tiles,),
            in_specs=[pl.BlockSpec((tm,) + slab, rows_in), hbm, hbm, hbm],
            out_specs=pl.BlockSpec((tm,) + slab, rows),
            scratch_shapes=[pltpu.VMEM((2, D, Fh), F32), pltpu.VMEM((2, D, Fh), F32),
                            pltpu.VMEM((2, Fh, D), F32),
                            pltpu.VMEM((D, Fh), BF16), pltpu.VMEM((D, Fh), BF16),
                            pltpu.VMEM((Fh, D), BF16),
                            pltpu.SMEM((1,), I32), pltpu.SemaphoreType.DMA((2,))]),
        out_shape=jax.ShapeDtypeStruct(xs.shape, xs.dtype),
        compiler_params=_cparams(("arbitrary",)),
        name="experts",
    )(tile_table, n_valid, xs, wg, wu, wd)


def _combine_kernel(tcnt_ref, tstart_ref, lrow_ref, x1_ref, wt_ref, p_ref, gp_ref, wpg_ref, wpp_ref,
                    gf_ref, ys_ref, o_ref, *scratch):
    g = COMBINE_TILES
    nb = N_COMBINE_BUFS
    bufs = tuple(scratch[k * g:(k + 1) * g] for k in range(nb))
    wpg_b, wpp_b, sem = scratch[nb * g:]
    i = pl.program_id(0)
    n = pl.num_programs(0)
    tc = x1_ref.shape[0] // g
    nr = 2 * tc

    def request(tile, live, buf, s):
        lo = 0
        for e in range(N_EXPERTS):
            cnt = jnp.where(live, tcnt_ref[tile, e], 0)
            src = tstart_ref[tile, e]
            for bit in _run_bits(tc):
                off = cnt & ~(2 * bit - 1)

                @pl.when((cnt & bit) != 0)
                def _():
                    pltpu.make_async_copy(ys_ref.at[pl.ds(src + off, bit)],
                                          buf.at[pl.ds(lo + off, bit)], s).start()
            lo = lo + cnt

    ahead = nb - 1

    @pl.when(i == 0)
    def _():
        for k in range(ahead):
            for u in range(g):
                request(jnp.minimum(k, n - 1) * g + u, k < n, bufs[k][u], sem.at[k, u])
        wpg_b[...] = wpg_ref[...].astype(BF16)
        wpp_b[...] = wpp_ref[...].astype(BF16)

    def step(slot):
        cur, far = bufs[slot], bufs[(slot + ahead) % nb]
        for u in range(g):
            pltpu.make_async_copy(ys_ref.at[pl.ds(0, nr)], cur[u], sem.at[slot, u]).wait()
        for u in range(g):
            request(jnp.minimum(i + ahead, n - 1) * g + u, i + ahead < n, far[u],
                    sem.at[(slot + ahead) % nb, u])
        r = lax.broadcasted_iota(I32, (tc, nr), 1)
        for u in range(g):
            tok = slice(u * tc, (u + 1) * tc)
            rows = cur[u][...].reshape(nr, -1)
            local_row = lrow_ref[tok, :]
            y0 = jnp.dot((r == local_row[:, 0:1]).astype(BF16), rows, preferred_element_type=F32)
            y1 = jnp.dot((r == local_row[:, 1:2]).astype(BF16), rows, preferred_element_type=F32)
            wt = wt_ref[tok, :]
            x2 = x1_ref[tok, :] + wt[:, 2:3] * y0 + wt[:, 3:4] * y1
            hp = _rms(x2, gp_ref[...]).astype(BF16)
            gate = jax.nn.sigmoid(jnp.dot(hp, wpg_b[...], preferred_element_type=F32))
            proj = jnp.dot(p_ref[tok, :].astype(BF16), wpp_b[...], preferred_element_type=F32)
            x3 = x2 + gate * proj
            o_ref[tok, :] = _rms(x3, gf_ref[...])

    phase = lax.rem(i, nb)
    for slot in range(nb):
        pl.when(phase == slot)(functools.partial(step, slot))


def _combine(tile_cnt, tile_start, lrow, x1, wts, p2, g_ple, w_ple_gate, w_ple_proj, g_final, ys):
    T, D = x1.shape
    P = p2.shape[1]
    g = COMBINE_TILES
    tm = g * TM_COMBINE
    n = T // tm
    row = lambda i: (i, 0)
    const = lambda i: (0, 0)
    smem = pl.BlockSpec(memory_space=pltpu.SMEM)
    run_buf = pltpu.VMEM((2 * TM_COMBINE,) + ys.shape[1:], ys.dtype)
    return pl.pallas_call(
        _combine_kernel,
        grid=(n,),
        in_specs=[smem, smem, pl.BlockSpec((tm, LANES), row),
                  pl.BlockSpec((tm, D), row), pl.BlockSpec((tm, LANES), row),
                  pl.BlockSpec((tm, P), row), pl.BlockSpec((1, D), const),
                  pl.BlockSpec((D, D), const, **_FETCH_ONCE), pl.BlockSpec((P, D), const, **_FETCH_ONCE),
                  pl.BlockSpec((1, D), const), pl.BlockSpec(memory_space=pl.ANY)],
        out_specs=pl.BlockSpec((tm, D), row),
        out_shape=jax.ShapeDtypeStruct((T, D), F32),
        scratch_shapes=[run_buf] * (N_COMBINE_BUFS * g) + [
                        pltpu.VMEM((D, D), BF16), pltpu.VMEM((P, D), BF16),
                        pltpu.SemaphoreType.DMA((N_COMBINE_BUFS, g))],
        compiler_params=_cparams(("arbitrary",)),
        name="combine",
    )(tile_cnt, tile_start, lrow, x1, wts, p2, g_ple, w_ple_gate, w_ple_proj, g_final, ys)


def _block_diag(w):
    nb, d, _ = w.shape
    eye = jnp.eye(nb, dtype=w.dtype)
    return (eye[:, None, :, None] * w[:, :, None, :]).reshape(nb * d, nb * d)


def kernel(x, p, positions, g_mix, w_in, sinks, conv_w, conv_b, lru_wa, lru_ba, lru_wx, lru_bx,
           lru_lambda, g_attn_out, g_lru_out, w_out, g_ffn, w_router_group, b_router_group,
           w_router_expert, b_router_expert, w_expert_gate, w_expert_up, w_expert_down, g_ple,
           w_ple_gate, w_ple_proj, g_final):
    B, S, D = x.shape
    assert w_in.shape[0] == 1, "single-layer block only"
    T = B * S
    lru_w = conv_w.shape[-1]
    n_rows = 2 * T + N_EXPERTS * TM_EXPERT
    n_tiles = n_rows // TM_EXPERT
    pos3 = positions.reshape(T // TM_INPROJ, TM_INPROJ // LANES, LANES).astype(I32)
    x2 = x.reshape(T, D)
    q, k, v, lx, lg = _inproj(x2, pos3, g_mix[0][None], w_in[0], lru_w)
    mix_a = _attention(q, k, v, sinks[0], g_attn_out[0][None], B, S)
    mix_b = _lru(lx.reshape(B, S, lru_w), lg.reshape(B, S, lru_w), conv_w[0], conv_b[0][None],
                 _block_diag(lru_wa[0]).astype(BF16), lru_ba[0][None],
                 _block_diag(lru_wx[0]).astype(BF16), lru_bx[0][None],
                 lru_lambda[0][None], g_lru_out[0][None]).reshape(T, lru_w)
    n_router = N_GROUPS + N_EXPERTS
    w_router = jnp.pad(jnp.concatenate([w_router_group[0], w_router_expert[0]], axis=1),
                       ((0, 0), (0, LANES - n_router)))
    b_router = jnp.pad(jnp.concatenate([b_router_group[0], b_router_expert[0]]),
                       (0, LANES - n_router))[None]
    w_router_hi = w_router.astype(BF16)
    w_router_lo = (w_router - w_router_hi.astype(F32)).astype(BF16)
    x1, h2, route = _outproj(x2, mix_a, mix_b, w_out[0], g_ffn[0][None],
                             jnp.concatenate([w_router_hi, w_router_lo], axis=1), b_router)
    lrow_d, lrow_c, tile_table, n_valid, meta, tile_cnt, tile_start = _plan(route, n_tiles)
    xs = _dispatch(tile_cnt, tile_start, meta, n_valid, lrow_d, h2, n_rows)
    Fh = w_expert_gate.shape[-1]
    ys = _experts(tile_table, n_valid, xs,
                  w_expert_gate[0].reshape(N_EXPERTS, D, Fh),
                  w_expert_up[0].reshape(N_EXPERTS, D, Fh),
                  w_expert_down[0].reshape(N_EXPERTS, Fh, D))
    out = _combine(tile_cnt, tile_start, lrow_c, x1, route,
                   p[0].reshape(T, -1), g_ple[0][None], w_ple_gate[0], w_ple_proj[0], g_final[None], ys)
    return out.reshape(B, S, D)
```

```python
import functools
import math

import jax
import jax.numpy as jnp
from jax import lax
from jax.experimental import pallas as pl
from jax.experimental.pallas import tpu as pltpu

F32 = jnp.float32
BF16 = jnp.bfloat16
I32 = jnp.int32

EPS = 1e-6
N_HEADS = 8
N_KV_HEADS = 2
HEAD_DIM = 64
WINDOW = 128
ROPE_THETA = 10000.0
CONV_WIDTH = 4
LRU_C = 8.0
N_GROUPS = 4
EXPERTS_PER_GROUP = 8
N_EXPERTS = N_GROUPS * EXPERTS_PER_GROUP
NEG_INF = -1e30
LANES = 128
SUBLANES = 8

TM_INPROJ = 1024
TM_PROJ = 1024
TQ_ATTN = 2048
TL_LRU = 256
TM_PLAN = 1024
TM_EXPERT = 512
EXPERT_FF_BLOCK = 256
EXPERT_ROW_BLOCK = 128
TM_DISPATCH = 512
N_DISPATCH_BUFS = 3
N_COMBINE_BUFS = 3
COMBINE_TILES = 1
TM_COMBINE = 256
VMEM_LIMIT = 56 * 1024 * 1024


def _rms(x, g):
    ms = jnp.mean(x * x, axis=-1, keepdims=True)
    return x * lax.rsqrt(ms + EPS) * g


_FETCH_ONCE = dict(pipeline_mode=pl.Buffered(1))


def _cparams(sem):
    return pltpu.CompilerParams(dimension_semantics=sem, vmem_limit_bytes=VMEM_LIMIT)


def _inproj_kernel(x_ref, pos_ref, g_ref, w_ref, q_ref, k_ref, v_ref, lx_ref, lg_ref, w_b):
    @pl.when(pl.program_id(0) == 0)
    def _():
        w_b[...] = w_ref[...].astype(BF16)

    h = _rms(x_ref[...], g_ref[...]).astype(BF16)
    proj = jnp.dot(h, w_b[...], preferred_element_type=F32)
    lane = lax.broadcasted_iota(I32, (1, LANES), 1)
    n_freq = HEAD_DIM // 2
    fidx = lax.broadcasted_iota(I32, (n_freq, 1), 0).astype(F32)
    inv_freq = jnp.exp(fidx * (-2.0 / HEAD_DIM * math.log(ROPE_THETA)))
    pos = pos_ref[0].astype(F32)
    cos_blocks, sin_blocks = [], []
    for c in range(pos.shape[0]):
        ang_t = inv_freq * pos[c:c + 1, :]
        cos_blocks.append(jnp.concatenate([jnp.cos(ang_t)] * (LANES // n_freq), axis=0).T)
        sin_blocks.append(jnp.concatenate([jnp.sin(ang_t)] * (LANES // n_freq), axis=0).T)
    cos = jnp.concatenate(cos_blocks, axis=0)
    sin = jnp.concatenate(sin_blocks, axis=0)
    first_half = (lane % HEAD_DIM) < (HEAD_DIM // 2)
    sin_signed = jnp.where(first_half, -sin, sin)

    def rope(t):
        partner = jnp.where(first_half, pltpu.roll(t, LANES - HEAD_DIM // 2, 1),
                            pltpu.roll(t, HEAD_DIM // 2, 1))
        return t * cos + partner * sin_signed

    attn_w = N_HEADS * HEAD_DIM
    scale = HEAD_DIM ** -0.5
    assert math.frexp(scale)[0] == 0.5, "score scale must be a power of two to fold into q exactly"
    for c in range(attn_w // LANES):
        q_ref[:, c * LANES:(c + 1) * LANES] = (
            rope(proj[:, c * LANES:(c + 1) * LANES]) * scale).astype(BF16)
    kv_w = N_KV_HEADS * HEAD_DIM
    assert kv_w == LANES
    k = rope(proj[:, attn_w:attn_w + kv_w])
    v = proj[:, attn_w + kv_w:attn_w + 2 * kv_w]
    k_ref[:, :kv_w] = k.astype(BF16)
    k_ref[:, kv_w:] = pltpu.roll(k, HEAD_DIM, 1).astype(BF16)
    v_ref[:, :kv_w] = v.astype(BF16)
    v_ref[:, kv_w:] = pltpu.roll(v, HEAD_DIM, 1).astype(BF16)
    off_lx = attn_w + 2 * kv_w
    lru_w = lx_ref.shape[1]
    lx_ref[...] = proj[:, off_lx:off_lx + lru_w]
    lg_ref[...] = proj[:, off_lx + lru_w:off_lx + 2 * lru_w]


def _inproj(x2, pos3, g_mix, w_in, lru_w):
    T, D = x2.shape
    tm = TM_INPROJ
    attn_w = N_HEADS * HEAD_DIM
    kv_w = N_KV_HEADS * HEAD_DIM
    row = lambda i: (i, 0)
    const = lambda i: (0, 0)
    return pl.pallas_call(
        _inproj_kernel,
        grid=(T // tm,),
        in_specs=[pl.BlockSpec((tm, D), row),
                  pl.BlockSpec((1, tm // LANES, LANES), lambda i: (i, 0, 0)),
                  pl.BlockSpec((1, D), const), pl.BlockSpec(w_in.shape, const, **_FETCH_ONCE)],
        out_specs=[pl.BlockSpec((tm, attn_w), row), pl.BlockSpec((tm, 2 * kv_w), row),
                   pl.BlockSpec((tm, 2 * kv_w), row), pl.BlockSpec((tm, lru_w), row),
                   pl.BlockSpec((tm, lru_w), row)],
        out_shape=[jax.ShapeDtypeStruct((T, attn_w), BF16),
                   jax.ShapeDtypeStruct((T, 2 * kv_w), BF16),
                   jax.ShapeDtypeStruct((T, 2 * kv_w), BF16), jax.ShapeDtypeStruct((T, lru_w), F32),
                   jax.ShapeDtypeStruct((T, lru_w), F32)],
        scratch_shapes=[pltpu.VMEM(w_in.shape, BF16)],
        compiler_params=_cparams(("arbitrary",)),
        name="inproj",
    )(x2, pos3, g_mix, w_in)


def _attn_kernel(sinks_ref, q_ref, k_ref, v_ref, kp_ref, vp_ref, g_ref, o_ref):
    n = pl.program_id(1)
    blk = WINDOW
    hd = HEAD_DIM
    nqb = q_ref.shape[0] // blk
    n_pairs = N_HEADS // 2
    pairs_per_group = n_pairs // N_KV_HEADS
    lane = lax.broadcasted_iota(I32, (1, LANES), 1)
    first = lane < hd
    assert WINDOW == blk
    from_prev = (lax.broadcasted_iota(I32, (blk, blk), 1) > lax.broadcasted_iota(I32, (blk, blk), 0))
    bd_row = lax.broadcasted_iota(I32, (4 * blk, LANES), 0)
    bd_lane = lax.broadcasted_iota(I32, (4 * blk, LANES), 1)
    ones_bd = ((bd_row < 2 * blk) == (bd_lane < hd)).astype(BF16)
    zero = jnp.zeros((), BF16)

    def block_diag(t):
        same, swapped = t[:, :LANES], t[:, LANES:]
        out = []
        for g in range(N_KV_HEADS):
            top = jnp.where(first, same if g == 0 else swapped, zero)
            bot = jnp.where(first, zero, swapped if g == 0 else same)
            out.append(jnp.concatenate([top, bot], axis=0))
        return out

    for j in range(nqb):
        rows = slice(j * blk, (j + 1) * blk)
        if j == 0:
            k_prev, v_prev = kp_ref[...], vp_ref[...]
        else:
            prev_rows = slice((j - 1) * blk, j * blk)
            k_prev, v_prev = k_ref[prev_rows, :], v_ref[prev_rows, :]
        k_bd = block_diag(jnp.concatenate([k_prev, k_ref[rows, :]], axis=0))
        v_bd = block_diag(jnp.concatenate([v_prev, v_ref[rows, :]], axis=0))
        outs = []
        for p in range(n_pairs):
            g = p // pairs_per_group
            qp = q_ref[rows, p * LANES:(p + 1) * LANES]
            s = lax.dot_general(qp, k_bd[g], (((1,), (1,)), ((), ())), preferred_element_type=F32)
            es, ms = [], []
            for a in range(2):
                s_prev = s[:, 2 * a * blk:(2 * a + 1) * blk]
                s_cur = s[:, (2 * a + 1) * blk:(2 * a + 2) * blk]
                if j == 0:
                    s_prev = jnp.where(n > 0, s_prev, NEG_INF)
                sa = jnp.where(from_prev, s_prev, s_cur)
                m = jnp.maximum(jnp.max(sa, axis=-1, keepdims=True), sinks_ref[2 * p + a])
                ea = jnp.exp(sa - m)
                es += [jnp.where(from_prev, ea, 0.0), jnp.where(from_prev, 0.0, ea)]
                ms.append(m)
            e = jnp.concatenate(es, axis=1).astype(BF16)
            od = jnp.dot(e, jnp.concatenate([v_bd[g], ones_bd], axis=1), preferred_element_type=F32)
            sink_term = jnp.where(first, jnp.exp(sinks_ref[2 * p] - ms[0]),
                                  jnp.exp(sinks_ref[2 * p + 1] - ms[1]))
            outs.append(od[:, :LANES] / (od[:, LANES:] + sink_term))
        attn = jnp.concatenate(outs, axis=1)
        o_ref[rows, :] = _rms(attn, g_ref[...]).astype(BF16)


def _attention(q, k, v, sinks, g_attn, B, S):
    T, attn_w = q.shape
    kv_w = k.shape[1]
    blk = WINDOW
    tq = TQ_ATTN
    nt = S // tq
    cur = lambda b, n: (b * nt + n, 0)
    prev = lambda b, n: (b * (S // blk) + jnp.maximum(n * (tq // blk) - 1, 0), 0)
    return pl.pallas_call(
        _attn_kernel,
        grid=(B, nt),
        in_specs=[pl.BlockSpec(memory_space=pltpu.SMEM),
                  pl.BlockSpec((tq, attn_w), cur),
                  pl.BlockSpec((tq, kv_w), cur), pl.BlockSpec((tq, kv_w), cur),
                  pl.BlockSpec((blk, kv_w), prev), pl.BlockSpec((blk, kv_w), prev),
                  pl.BlockSpec((1, attn_w), lambda b, n: (0, 0))],
        out_specs=pl.BlockSpec((tq, attn_w), cur),
        out_shape=jax.ShapeDtypeStruct((T, attn_w), BF16),
        compiler_params=_cparams(("arbitrary", "arbitrary")),
        name="attn",
    )(sinks, q, k, v, k, v, g_attn)


def _neg_expm1(x, exp_x):
    series = -x * (1.0 + x * (0.5 + x * (1.0 / 6.0 + x * (1.0 / 24.0))))
    return jnp.where(x > -0.02, series, 1.0 - exp_x)


def _gelu_tanh(x):
    c = math.sqrt(2.0 / math.pi)
    return x * (0.5 + 0.5 * jnp.tanh(x * (c + (c * 0.044715) * (x * x))))


def _lru_kernel(lx_ref, lg_ref, cw_ref, cb_ref, wa_ref, ba_ref, wx_ref, bx_ref, lam_ref, g_ref,
                o_ref, xpad, a_scr, b_scr, h_scr, h_carry):
    i = pl.program_id(0)
    B, tl, C = lx_ref.shape
    pad = SUBLANES

    @pl.when(i == 0)
    def _():
        xpad[:, 0:pad, :] = jnp.zeros((B, pad, C), F32)
        h_carry[...] = jnp.zeros_like(h_carry)

    @pl.when(i > 0)
    def _():
        xpad[:, 0:pad, :] = xpad[:, tl:tl + pad, :]

    xpad[:, pad:, :] = lx_ref[...]
    groups = (B * tl // SUBLANES, SUBLANES, C)
    sub = lax.broadcasted_iota(I32, groups, 1)
    x_now = lx_ref[...].reshape(groups)
    x_before = xpad[:, pl.ds(0, tl), :].reshape(groups)
    y = x_now * cw_ref[CONV_WIDTH - 1:CONV_WIDTH, :] + cb_ref[...]
    for s in range(1, CONV_WIDTH):
        shifted = pltpu.roll(jnp.where(sub >= SUBLANES - s, x_before, x_now), s, 1)
        y = y + shifted * cw_ref[CONV_WIDTH - 1 - s:CONV_WIDTH - s, :]
    y2 = y.reshape(B * tl, C)
    yb = y2.astype(BF16)
    gate_a = jnp.dot(yb, wa_ref[...], preferred_element_type=F32) + ba_ref[...]
    gate_x = jnp.dot(yb, wx_ref[...], preferred_element_type=F32) + bx_ref[...]
    r = jax.nn.sigmoid(gate_a)
    ig = jax.nn.sigmoid(gate_x)
    nl = -lam_ref[...]
    softplus = jnp.maximum(nl, 0.0) + jnp.log1p(jnp.exp(-jnp.abs(nl)))
    log_a = (-LRU_C) * r * softplus
    a_all = jnp.exp(log_a)
    one_minus_a2 = _neg_expm1(2.0 * log_a, a_all * a_all)
    b_all = one_minus_a2 * lax.rsqrt(jnp.maximum(one_minus_a2, 1e-30)) * ig * y2
    groups = (B * tl // SUBLANES, SUBLANES, C)
    sub = lax.broadcasted_iota(I32, groups, 1)
    a_cum, b_loc = a_all.reshape(groups), b_all.reshape(groups)
    d = 1
    while d < SUBLANES:
        keep = sub >= d
        a_prev = pltpu.roll(a_cum, d, 1)
        b_prev = pltpu.roll(b_loc, d, 1)
        b_loc = b_loc + jnp.where(keep, a_cum * b_prev, 0.0)
        a_cum = jnp.where(keep, a_cum * a_prev, a_cum)
        d *= 2
    a_scr[...] = a_cum.reshape(B, tl, C)
    b_scr[...] = b_loc.reshape(B, tl, C)

    def carry(g, h_prev):
        rows = pl.ds(pl.multiple_of(g * SUBLANES, SUBLANES), SUBLANES)
        h = b_scr[:, rows, :] + a_scr[:, rows, :] * h_prev
        h_scr[:, rows, :] = h
        return h[:, SUBLANES - 1:SUBLANES, :]

    h_carry[...] = lax.fori_loop(0, tl // SUBLANES, carry, h_carry[...], unroll=4)
    lru = h_scr[...].reshape(B * tl, C) * _gelu_tanh(lg_ref[...].reshape(B * tl, C))
    o_ref[...] = _rms(lru, g_ref[...]).astype(BF16).reshape(B, tl, C)


def _lru(lx3, lg3, conv_w, conv_b, wa_bd, ba, wx_bd, bx, lam, g_lru):
    B, S, C = lx3.shape
    tl = TL_LRU
    blk = lambda i: (0, i, 0)
    const = lambda i: (0, 0)
    vec = pl.BlockSpec((1, C), const)
    return pl.pallas_call(
        _lru_kernel,
        grid=(S // tl,),
        in_specs=[pl.BlockSpec((B, tl, C), blk), pl.BlockSpec((B, tl, C), blk),
                  pl.BlockSpec((CONV_WIDTH, C), const), vec,
                  pl.BlockSpec((C, C), const), vec, pl.BlockSpec((C, C), const), vec, vec, vec],
        out_specs=pl.BlockSpec((B, tl, C), blk),
        out_shape=jax.ShapeDtypeStruct((B, S, C), BF16),
        scratch_shapes=[pltpu.VMEM((B, tl + SUBLANES, C), F32),
                        pltpu.VMEM((B, tl, C), F32), pltpu.VMEM((B, tl, C), F32),
                        pltpu.VMEM((B, tl, C), F32), pltpu.VMEM((B, 1, C), F32)],
        compiler_params=_cparams(("arbitrary",)),
        name="lru",
    )(lx3, lg3, conv_w, conv_b, wa_bd, ba, wx_bd, bx, lam, g_lru)


def _outproj_kernel(x_ref, ma_ref, mb_ref, wo_ref, g_ref, wr_ref, br_ref,
                    x1_ref, h2_ref, route_ref, wo_b):
    @pl.when(pl.program_id(0) == 0)
    def _():
        wo_b[...] = wo_ref[...].astype(BF16)

    half = ma_ref.shape[1]
    x1 = (x_ref[...]
          + jnp.dot(ma_ref[...], wo_b[0:half, :], preferred_element_type=F32)
          + jnp.dot(mb_ref[...], wo_b[half:, :], preferred_element_type=F32))
    x1_ref[...] = x1
    h2 = _rms(x1, g_ref[...])
    h2_ref[...] = h2.astype(BF16).reshape(h2_ref.shape)
    h_hi = h2.astype(BF16)
    h_lo = (h2 - h_hi.astype(F32)).astype(BF16)
    hw = jnp.dot(h_hi, wr_ref[...], preferred_element_type=F32)
    logits = (hw[:, :LANES] + hw[:, LANES:]
              + jnp.dot(h_lo, wr_ref[:, :LANES], preferred_element_type=F32) + br_ref[...])
    tm = logits.shape[0]
    lane = lax.broadcasted_iota(I32, (tm, LANES), 1)
    lane_f = lane.astype(F32)
    big = float(LANES)

    def first_argmax(vals):
        m = jnp.max(vals, axis=-1, keepdims=True)
        idx = jnp.min(jnp.where(vals == m, lane_f, big), axis=-1, keepdims=True)
        return m, idx.astype(I32)

    gl = jnp.where(lane < N_GROUPS, logits, NEG_INF)
    gmax, gidx = first_argmax(gl)
    gsum = jnp.sum(jnp.where(lane < N_GROUPS, jnp.exp(gl - gmax), 0.0), axis=-1, keepdims=True)
    g_top_p = 1.0 / gsum
    lo = N_GROUPS + EXPERTS_PER_GROUP * gidx
    el = jnp.where((lane >= lo) & (lane < lo + EXPERTS_PER_GROUP), logits, NEG_INF)
    m1, i1 = first_argmax(el)
    el2 = jnp.where(lane == i1, NEG_INF, el)
    m2, i2 = first_argmax(el2)
    ratio = jnp.exp(m2 - m1)
    w1 = g_top_p / (1.0 + ratio)
    w2 = g_top_p * ratio / (1.0 + ratio)
    ids = jnp.where(lane == 0, i1 - N_GROUPS, jnp.where(lane == 1, i2 - N_GROUPS, 0)).astype(F32)
    route_ref[...] = jnp.where(lane == 2, w1, jnp.where(lane == 3, w2, ids))


def _outproj(x2, mix_a, mix_b, w_out, g_ffn, w_router, b_router):
    T, D = x2.shape
    tm = TM_PROJ
    half = mix_a.shape[1]
    row = lambda i: (i, 0)
    const = lambda i: (0, 0)
    return pl.pallas_call(
        _outproj_kernel,
        grid=(T // tm,),
        in_specs=[pl.BlockSpec((tm, D), row), pl.BlockSpec((tm, half), row),
                  pl.BlockSpec((tm, half), row), pl.BlockSpec(w_out.shape, const, **_FETCH_ONCE),
                  pl.BlockSpec((1, D), const), pl.BlockSpec((D, 2 * LANES), const),
                  pl.BlockSpec((1, LANES), const)],
        out_specs=[pl.BlockSpec((tm, D), row),
                   pl.BlockSpec((tm, D // LANES, LANES), lambda i: (i, 0, 0)),
                   pl.BlockSpec((tm, LANES), row)],
        out_shape=[jax.ShapeDtypeStruct((T, D), F32),
                   jax.ShapeDtypeStruct((T, D // LANES, LANES), BF16),
                   jax.ShapeDtypeStruct((T, LANES), F32)],
        scratch_shapes=[pltpu.VMEM(w_out.shape, BF16)],
        compiler_params=_cparams(("arbitrary",)),
        name="outproj",
    )(x2, mix_a, mix_b, w_out, g_ffn, w_router, b_router)


def _lane_cumsum(x):
    lane = lax.broadcasted_iota(I32, x.shape, 1)
    shift = 1
    while shift < LANES:
        x = x + jnp.where(lane >= shift, pltpu.roll(x, shift, 1), 0)
        shift *= 2
    return x


def _plan_kernel(eid_ref, lrow_d_ref, lrow_c_ref, te_ref, nv_ref, meta_ref, tcnt_ref, tstart_ref,
                 cnt_scr, tcnt_scr, tpre_scr):
    i = pl.program_id(0)
    tm = eid_ref.shape[0]
    tc = TM_COMBINE
    group = TM_DISPATCH // TM_COMBINE
    nsub = tm // tc
    lane = lax.broadcasted_iota(I32, (tm, LANES), 1)
    eid = eid_ref[:, 0:2].astype(I32)
    sel1 = lane == eid[:, 0:1]
    sel2 = lane == eid[:, 1:2]

    @pl.when(i == 0)
    def _():
        cnt_scr[...] = jnp.zeros_like(cnt_scr)

    onehot = (sel1 | sel2).astype(BF16)
    r = lax.broadcasted_iota(I32, (tm, tm), 0)
    c = lax.broadcasted_iota(I32, (tm, tm), 1)
    strict_lower = (r > c).astype(BF16)
    within = jnp.dot(strict_lower, onehot, preferred_element_type=F32)
    base = cnt_scr[0:1, :]
    onehot_f = onehot.astype(F32)
    row = lax.broadcasted_iota(I32, (tm, LANES), 0)
    srow = lax.broadcasted_iota(I32, (2 * SUBLANES, LANES), 0)
    counts = [jnp.sum(onehot_f[s * tc:(s + 1) * tc], axis=0, keepdims=True) for s in range(nsub)]
    cnt_mat = jnp.zeros((2 * SUBLANES, LANES), F32)
    seen = []
    total = jnp.zeros((1, LANES), F32)
    for s in range(nsub):
        seen.append(total)
        tile = i * nsub + s
        tcnt_scr[pl.ds(tile, 1), :] = counts[s]
        tpre_scr[pl.ds(tile, 1), :] = base + total
        cnt_mat = jnp.where(srow == s, counts[s], cnt_mat)
        cnt_mat = jnp.where(srow == SUBLANES + s // group, cnt_mat + counts[s], cnt_mat)
        total = total + counts[s]
    cnt_i = cnt_mat.astype(I32)
    run_start = (_lane_cumsum(cnt_i) - cnt_i).astype(F32)

    def local_rows(first_tile_of, table_row_of):
        local = None
        for s in range(nsub):
            val = within - seen[first_tile_of(s)] + run_start[table_row_of(s):table_row_of(s) + 1, :]
            local = val if local is None else jnp.where(row >= s * tc, val, local)
        l1 = jnp.sum(jnp.where(sel1, local, 0.0), axis=-1, keepdims=True)
        l2 = jnp.sum(jnp.where(sel2, local, 0.0), axis=-1, keepdims=True)
        return jnp.where(lane == 0, l1, jnp.where(lane == 1, l2, 0.0)).astype(I32)

    lrow_c_ref[...] = local_rows(lambda s: s, lambda s: s)
    lrow_d = local_rows(lambda s: (s // group) * group, lambda s: SUBLANES + s // group)
    lrow_d_ref[...] = lrow_d.astype(F32).T[0:SUBLANES, :].astype(I32)
    cnt_scr[...] = cnt_scr[...] + total

    @pl.when(i == pl.num_programs(0) - 1)
    def _():
        shift = int(math.log2(TM_EXPERT))
        cnt = cnt_scr[...].astype(I32)
        padded = ((cnt + (TM_EXPERT - 1)) >> shift) << shift
        ends = _lane_cumsum(padded)
        nt = te_ref.shape[0]
        end_tile = (ends >> shift)[0:1, :]
        tl_lane = lax.broadcasted_iota(I32, (nt, LANES), 1)
        tile = lax.broadcasted_iota(I32, (nt, LANES), 0)
        done = jnp.where((tl_lane < N_EXPERTS) & (end_tile <= tile), 1.0, 0.0)
        te = jnp.minimum(jnp.sum(done, axis=-1, keepdims=True).astype(I32), N_EXPERTS - 1)
        n_valid = jnp.sum(jnp.where(tl_lane[0:1, :] == N_EXPERTS - 1, end_tile, 0).astype(F32),
                          axis=-1, keepdims=True).astype(I32)
        nv_ref[...] = jnp.broadcast_to(n_valid, nv_ref.shape)
        seg_end = jnp.sum(jnp.where(tl_lane == te, end_tile, 0).astype(F32), axis=-1,
                          keepdims=True).astype(I32)
        nxt = jnp.sum(jnp.where((tl_lane < N_EXPERTS) & (end_tile <= seg_end), 1.0, 0.0), axis=-1,
                      keepdims=True).astype(I32)
        nxt = jnp.where(seg_end < n_valid, jnp.minimum(nxt, N_EXPERTS - 1), -1)
        own = tl_lane == te
        seg_cnt = jnp.sum(jnp.where(own, cnt[0:1, :], 0).astype(F32), axis=-1,
                          keepdims=True).astype(I32)
        seg_tiles = jnp.sum(jnp.where(own, padded[0:1, :] >> shift, 0).astype(F32), axis=-1,
                            keepdims=True).astype(I32)
        tile_in_seg = tile[:, 0:1] - (seg_end - seg_tiles)
        rows_used = jnp.clip(seg_cnt - (tile_in_seg << shift), 0, TM_EXPERT)
        te_ref[...] = jnp.where(tl_lane == 1, nxt, jnp.where(tl_lane == 2, rows_used, te))
        row = lax.broadcasted_iota(I32, cnt.shape, 0)
        meta_ref[...] = jnp.where(row == 0, ends - padded + cnt, jnp.where(row == 1, padded - cnt, 0))
        tcnt_ref[...] = tcnt_scr[...].astype(I32)
        tstart_ref[...] = (tpre_scr[...] + (ends - padded).astype(F32)[0:1, :]).astype(I32)


def _plan(eid, n_tiles):
    T = eid.shape[0]
    tm = TM_PLAN
    steps = T // tm
    n_ctiles = T // TM_COMBINE
    assert TM_DISPATCH % TM_COMBINE == 0 and tm % TM_DISPATCH == 0 and tm // TM_COMBINE <= SUBLANES
    row = lambda i: (i, 0)
    const = lambda i: (0, 0)
    return pl.pallas_call(
        _plan_kernel,
        grid=(steps,),
        in_specs=[pl.BlockSpec((tm, LANES), row)],
        out_specs=[pl.BlockSpec((SUBLANES, tm), lambda i: (0, i)), pl.BlockSpec((tm, LANES), row),
                   pl.BlockSpec((n_tiles, LANES), const), pl.BlockSpec((SUBLANES, LANES), const),
                   pl.BlockSpec((SUBLANES, LANES), const),
                   pl.BlockSpec((n_ctiles, LANES), const), pl.BlockSpec((n_ctiles, LANES), const)],
        out_shape=[jax.ShapeDtypeStruct((SUBLANES, T), I32), jax.ShapeDtypeStruct((T, LANES), I32),
                   jax.ShapeDtypeStruct((n_tiles, LANES), I32),
                   jax.ShapeDtypeStruct((SUBLANES, LANES), I32),
                   jax.ShapeDtypeStruct((SUBLANES, LANES), I32),
                   jax.ShapeDtypeStruct((n_ctiles, LANES), I32),
                   jax.ShapeDtypeStruct((n_ctiles, LANES), I32)],
        scratch_shapes=[pltpu.VMEM((SUBLANES, LANES), F32),
                        pltpu.VMEM((n_ctiles, LANES), F32), pltpu.VMEM((n_ctiles, LANES), F32)],
        compiler_params=_cparams(("arbitrary",)),
        name="plan",
    )(eid)


_PAD_BITS = tuple(1 << b for b in reversed(range(int(math.log2(TM_EXPERT)))))


def _run_bits(max_rows):
    return tuple(1 << b for b in reversed(range(int(math.log2(max_rows)) + 1)))


def _dispatch_kernel(tcnt_ref, tstart_ref, meta_ref, nv_ref, lrow_ref, h_ref,
                     xs_ref, lbuf_0, lbuf_1, lbuf_2, zeros, sem, zsem):
    i = pl.program_id(0)
    n = pl.num_programs(0)
    td = h_ref.shape[0]
    nr = 2 * td
    bufs = (lbuf_0, lbuf_1, lbuf_2)

    def wait_buf(s):
        pltpu.make_async_copy(bufs[s], xs_ref.at[pl.ds(0, nr)], sem.at[s]).wait()

    def step(cur_s):
        prev_s = (cur_s - 1) % N_DISPATCH_BUFS
        cur, prev = bufs[cur_s], bufs[prev_s]

        @pl.when(i >= N_DISPATCH_BUFS)
        def _():
            wait_buf(cur_s)

        local_row = lrow_ref[...].astype(F32)
        r = lax.broadcasted_iota(I32, (nr, td), 0).astype(F32)
        onehot = ((r == local_row[0:1, :]) | (r == local_row[1:2, :])).astype(BF16)
        sorted_rows = jnp.dot(onehot, h_ref[...].reshape(td, -1), preferred_element_type=F32)
        cur[...] = sorted_rows.astype(BF16).reshape(cur.shape)

        group = TM_DISPATCH // TM_COMBINE
        first = jnp.maximum(i - 1, 0) * group
        lo = 0
        for e in range(N_EXPERTS):
            cnt = sum(tcnt_ref[first + u, e] for u in range(group))
            cnt = jnp.where(i >= 1, cnt, 0)
            dst = tstart_ref[first, e]
            for bit in _run_bits(td):
                off = cnt & ~(2 * bit - 1)

                @pl.when((cnt & bit) != 0)
                def _():
                    pltpu.make_async_copy(prev.at[pl.ds(lo + off, bit)],
                                          xs_ref.at[pl.ds(dst + off, bit)], sem.at[prev_s]).start()
            lo = lo + cnt

        @pl.when(i == n - 1)
        def _():
            wait_buf(prev_s)
            if N_DISPATCH_BUFS > 2:
                @pl.when(n >= 3)
                def _():
                    wait_buf((cur_s - 2) % N_DISPATCH_BUFS)

    n_tiles = xs_ref.shape[0] // TM_EXPERT
    first_tail = n_tiles - N_EXPERTS

    def pad_copy(e, bit):
        cnt = meta_ref[1, e]
        dst = meta_ref[0, e] + (cnt & ~(2 * bit - 1))
        return (cnt & bit) != 0, pltpu.make_async_copy(zeros.at[pl.ds(0, bit)],
                                                       xs_ref.at[pl.ds(dst, bit)], zsem)

    def tail_copy(j):
        return j >= nv_ref[0, 0], pltpu.make_async_copy(
            zeros, xs_ref.at[pl.ds(j * TM_EXPERT, TM_EXPERT)], zsem)

    def for_each_fill(act):
        def per_expert(e, c):
            for bit in _PAD_BITS:
                pred, cp = pad_copy(e, bit)
                pl.when(pred)(functools.partial(act, cp))
            return c

        lax.fori_loop(0, N_EXPERTS, per_expert, 0)

        def per_tile(j, c):
            pred, cp = tail_copy(j)
            pl.when(pred)(functools.partial(act, cp))
            return c

        lax.fori_loop(first_tail, n_tiles, per_tile, 0)

    @pl.when(i == 0)
    def _():
        zeros[...] = jnp.zeros_like(zeros)
        for_each_fill(lambda cp: cp.start())

    phase = lax.rem(i, N_DISPATCH_BUFS)
    for s in range(N_DISPATCH_BUFS):
        pl.when(phase == s)(functools.partial(step, s))

    @pl.when(i == n - 1)
    def _():
        for_each_fill(lambda cp: cp.wait())


def _dispatch(tile_cnt, tile_start, meta, n_valid, lrow, h2, n_rows):
    T = h2.shape[0]
    slab = h2.shape[1:]
    td = TM_DISPATCH
    nt = T // td
    smem = pl.BlockSpec(memory_space=pltpu.SMEM)
    return pl.pallas_call(
        _dispatch_kernel,
        grid=(nt + 1,),
        in_specs=[smem, smem, smem, smem,
                  pl.BlockSpec((SUBLANES, td), lambda i: (0, jnp.minimum(i, nt - 1))),
                  pl.BlockSpec((td,) + slab, lambda i: (jnp.minimum(i, nt - 1), 0, 0))],
        out_specs=pl.BlockSpec(memory_space=pl.ANY),
        out_shape=jax.ShapeDtypeStruct((n_rows,) + slab, h2.dtype),
        scratch_shapes=[pltpu.VMEM((2 * td,) + slab, h2.dtype)] * N_DISPATCH_BUFS + [
                        pltpu.VMEM((TM_EXPERT,) + slab, h2.dtype),
                        pltpu.SemaphoreType.DMA((N_DISPATCH_BUFS,)), pltpu.SemaphoreType.DMA(())],
        compiler_params=_cparams(("arbitrary",)),
        name="dispatch",
    )(tile_cnt, tile_start, meta, n_valid, lrow, h2)


def _experts_kernel(te_ref, nv_ref, xs_ref, wg_ref, wu_ref, wd_ref, ys_ref,
                    wg_f, wu_f, wd_f, wg_b, wu_b, wd_b, slot_ref, sem):
    j = pl.program_id(0)
    valid = j < nv_ref[0, 0]
    new_expert = valid & ((j == 0) | (te_ref[j, 0] != te_ref[jnp.maximum(j - 1, 0), 0]))

    def fetch(e, s):
        return (pltpu.make_async_copy(wg_ref.at[e], wg_f.at[s], sem.at[s]),
                pltpu.make_async_copy(wu_ref.at[e], wu_f.at[s], sem.at[s]),
                pltpu.make_async_copy(wd_ref.at[e], wd_f.at[s], sem.at[s]))

    @pl.when(j == 0)
    def _():
        slot_ref[0] = 0
        for cp in fetch(te_ref[0, 0], 0):
            cp.start()

    @pl.when(new_expert)
    def _():
        s = slot_ref[0]
        for cp in fetch(te_ref[j, 0], s):
            cp.wait()
        nxt = te_ref[j, 1]

        @pl.when(nxt >= 0)
        def _():
            for cp in fetch(nxt, 1 - s):
                cp.start()

        wg_b[...] = wg_f[s].astype(BF16)
        wu_b[...] = wu_f[s].astype(BF16)
        wd_b[...] = wd_f[s].astype(BF16)
        slot_ref[0] = 1 - s

    def mlp(rows):
        x = xs_ref[0:rows].reshape(rows, -1)
        y = None
        for lo in range(0, wg_b.shape[1], EXPERT_FF_BLOCK):
            cols = slice(lo, lo + EXPERT_FF_BLOCK)
            a = jnp.dot(x, wg_b[:, cols], preferred_element_type=F32)
            u = jnp.dot(x, wu_b[:, cols], preferred_element_type=F32)
            hid = (a * jax.nn.sigmoid(a) * u).astype(BF16)
            part = jnp.dot(hid, wd_b[cols, :], preferred_element_type=F32)
            y = part if y is None else y + part
        ys_ref[0:rows] = y.astype(BF16).reshape((rows,) + ys_ref.shape[1:])
        if rows < tm:
            ys_ref[rows:] = jnp.zeros((tm - rows,) + ys_ref.shape[1:], ys_ref.dtype)

    tm = xs_ref.shape[0]
    blocks_used = (te_ref[j, 2] + (EXPERT_ROW_BLOCK - 1)) // EXPERT_ROW_BLOCK
    for nblk in range(1, tm // EXPERT_ROW_BLOCK + 1):
        pl.when(valid & (blocks_used == nblk))(functools.partial(mlp, nblk * EXPERT_ROW_BLOCK))

    @pl.when(jnp.logical_not(valid) | (blocks_used == 0))
    def _():
        ys_ref[...] = jnp.zeros_like(ys_ref)


def _experts(tile_table, n_valid, xs, wg, wu, wd):
    R = xs.shape[0]
    slab = xs.shape[1:]
    E, D, Fh = wg.shape
    tm = TM_EXPERT
    n_tiles = R // tm
    rows = lambda j, te, nv: (j, 0, 0)
    rows_in = lambda j, te, nv: (jnp.minimum(j, nv[0, 0] - 1), 0, 0)
    hbm = pl.BlockSpec(memory_space=pl.ANY)
    return pl.pallas_call(
        _experts_kernel,
        grid_spec=pltpu.PrefetchScalarGridSpec(
            num_scalar_prefetch=2,
            grid=(n_tiles,),
            in_specs=[pl.BlockSpec((tm,) + slab, rows_in), hbm, hbm, hbm],
            out_specs=pl.BlockSpec((tm,) + slab, rows),
            scratch_shapes=[pltpu.VMEM((2, D, Fh), F32), pltpu.VMEM((2, D, Fh), F32),
                            pltpu.VMEM((2, Fh, D), F32),
                            pltpu.VMEM((D, Fh), BF16), pltpu.VMEM((D, Fh), BF16),
                            pltpu.VMEM((Fh, D), BF16),
                            pltpu.SMEM((1,), I32), pltpu.SemaphoreType.DMA((2,))]),
        out_shape=jax.ShapeDtypeStruct(xs.shape, xs.dtype),
        compiler_params=_cparams(("arbitrary",)),
        name="experts",
    )(tile_table, n_valid, xs, wg, wu, wd)


def _combine_kernel(tcnt_ref, tstart_ref, lrow_ref, x1_ref, wt_ref, p_ref, gp_ref, wpg_ref, wpp_ref,
                    gf_ref, ys_ref, o_ref, *scratch):
    g = COMBINE_TILES
    nb = N_COMBINE_BUFS
    bufs = tuple(scratch[k * g:(k + 1) * g] for k in range(nb))
    wpg_b, wpp_b, sem = scratch[nb * g:]
    i = pl.program_id(0)
    n = pl.num_programs(0)
    tc = x1_ref.shape[0] // g
    nr = 2 * tc

    def request(tile, live, buf, s):
        lo = 0
        for e in range(N_EXPERTS):
            cnt = jnp.where(live, tcnt_ref[tile, e], 0)
            src = tstart_ref[tile, e]
            for bit in _run_bits(tc):
                off = cnt & ~(2 * bit - 1)

                @pl.when((cnt & bit) != 0)
                def _():
                    pltpu.make_async_copy(ys_ref.at[pl.ds(src + off, bit)],
                                          buf.at[pl.ds(lo + off, bit)], s).start()
            lo = lo + cnt

    ahead = nb - 1

    @pl.when(i == 0)
    def _():
        for k in range(ahead):
            for u in range(g):
                request(jnp.minimum(k, n - 1) * g + u, k < n, bufs[k][u], sem.at[k, u])
        wpg_b[...] = wpg_ref[...].astype(BF16)
        wpp_b[...] = wpp_ref[...].astype(BF16)

    def step(slot):
        cur, far = bufs[slot], bufs[(slot + ahead) % nb]
        for u in range(g):
            pltpu.make_async_copy(ys_ref.at[pl.ds(0, nr)], cur[u], sem.at[slot, u]).wait()
        for u in range(g):
            request(jnp.minimum(i + ahead, n - 1) * g + u, i + ahead < n, far[u],
                    sem.at[(slot + ahead) % nb, u])
        r = lax.broadcasted_iota(I32, (tc, nr), 1)
        for u in range(g):
            tok = slice(u * tc, (u + 1) * tc)
            rows = cur[u][...].reshape(nr, -1)
            local_row = lrow_ref[tok, :]
            y0 = jnp.dot((r == local_row[:, 0:1]).astype(BF16), rows, preferred_element_type=F32)
            y1 = jnp.dot((r == local_row[:, 1:2]).astype(BF16), rows, preferred_element_type=F32)
            wt = wt_ref[tok, :]
            x2 = x1_ref[tok, :] + wt[:, 2:3] * y0 + wt[:, 3:4] * y1
            hp = _rms(x2, gp_ref[...]).astype(BF16)
            gate = jax.nn.sigmoid(jnp.dot(hp, wpg_b[...], preferred_element_type=F32))
            proj = jnp.dot(p_ref[tok, :].astype(BF16), wpp_b[...], preferred_element_type=F32)
            x3 = x2 + gate * proj
            o_ref[tok, :] = _rms(x3, gf_ref[...])

    phase = lax.rem(i, nb)
    for slot in range(nb):
        pl.when(phase == slot)(functools.partial(step, slot))


def _combine(tile_cnt, tile_start, lrow, x1, wts, p2, g_ple, w_ple_gate, w_ple_proj, g_final, ys):
    T, D = x1.shape
    P = p2.shape[1]
    g = COMBINE_TILES
    tm = g * TM_COMBINE
    n = T // tm
    row = lambda i: (i, 0)
    const = lambda i: (0, 0)
    smem = pl.BlockSpec(memory_space=pltpu.SMEM)
    run_buf = pltpu.VMEM((2 * TM_COMBINE,) + ys.shape[1:], ys.dtype)
    return pl.pallas_call(
        _combine_kernel,
        grid=(n,),
        in_specs=[smem, smem, pl.BlockSpec((tm, LANES), row),
                  pl.BlockSpec((tm, D), row), pl.BlockSpec((tm, LANES), row),
                  pl.BlockSpec((tm, P), row), pl.BlockSpec((1, D), const),
                  pl.BlockSpec((D, D), const, **_FETCH_ONCE), pl.BlockSpec((P, D), const, **_FETCH_ONCE),
                  pl.BlockSpec((1, D), const), pl.BlockSpec(memory_space=pl.ANY)],
        out_specs=pl.BlockSpec((tm, D), row),
        out_shape=jax.ShapeDtypeStruct((T, D), F32),
        scratch_shapes=[run_buf] * (N_COMBINE_BUFS * g) + [
                        pltpu.VMEM((D, D), BF16), pltpu.VMEM((P, D), BF16),
                        pltpu.SemaphoreType.DMA((N_COMBINE_BUFS, g))],
        compiler_params=_cparams(("arbitrary",)),
        name="combine",
    )(tile_cnt, tile_start, lrow, x1, wts, p2, g_ple, w_ple_gate, w_ple_proj, g_final, ys)


def _block_diag(w):
    nb, d, _ = w.shape
    eye = jnp.eye(nb, dtype=w.dtype)
    return (eye[:, None, :, None] * w[:, :, None, :]).reshape(nb * d, nb * d)


def kernel(x, p, positions, g_mix, w_in, sinks, conv_w, conv_b, lru_wa, lru_ba, lru_wx, lru_bx,
           lru_lambda, g_attn_out, g_lru_out, w_out, g_ffn, w_router_group, b_router_group,
           w_router_expert, b_router_expert, w_expert_gate, w_expert_up, w_expert_down, g_ple,
           w_ple_gate, w_ple_proj, g_final):
    B, S, D = x.shape
    assert w_in.shape[0] == 1, "single-layer block only"
    T = B * S
    lru_w = conv_w.shape[-1]
    n_rows = 2 * T + N_EXPERTS * TM_EXPERT
    n_tiles = n_rows // TM_EXPERT
    pos3 = positions.reshape(T // TM_INPROJ, TM_INPROJ // LANES, LANES).astype(I32)
    x2 = x.reshape(T, D)
    q, k, v, lx, lg = _inproj(x2, pos3, g_mix[0][None], w_in[0], lru_w)
    mix_a = _attention(q, k, v, sinks[0], g_attn_out[0][None], B, S)
    mix_b = _lru(lx.reshape(B, S, lru_w), lg.reshape(B, S, lru_w), conv_w[0], conv_b[0][None],
                 _block_diag(lru_wa[0]).astype(BF16), lru_ba[0][None],
                 _block_diag(lru_wx[0]).astype(BF16), lru_bx[0][None],
                 lru_lambda[0][None], g_lru_out[0][None]).reshape(T, lru_w)
    n_router = N_GROUPS + N_EXPERTS
    w_router = jnp.pad(jnp.concatenate([w_router_group[0], w_router_expert[0]], axis=1),
                       ((0, 0), (0, LANES - n_router)))
    b_router = jnp.pad(jnp.concatenate([b_router_group[0], b_router_expert[0]]),
                       (0, LANES - n_router))[None]
    w_router_hi = w_router.astype(BF16)
    w_router_lo = (w_router - w_router_hi.astype(F32)).astype(BF16)
    x1, h2, route = _outproj(x2, mix_a, mix_b, w_out[0], g_ffn[0][None],
                             jnp.concatenate([w_router_hi, w_router_lo], axis=1), b_router)
    lrow_d, lrow_c, tile_table, n_valid, meta, tile_cnt, tile_start = _plan(route, n_tiles)
    xs = _dispatch(tile_cnt, tile_start, meta, n_valid, lrow_d, h2, n_rows)
    Fh = w_expert_gate.shape[-1]
    ys = _experts(tile_table, n_valid, xs,
                  w_expert_gate[0].reshape(N_EXPERTS, D, Fh),
                  w_expert_up[0].reshape(N_EXPERTS, D, Fh),
                  w_expert_down[0].reshape(N_EXPERTS, Fh, D))
    out = _combine(tile_cnt, tile_start, lrow_c, x1, route,
                   p[0].reshape(T, -1), g_ple[0][None], w_ple_gate[0], w_ple_proj[0], g_final[None], ys)
    return out.reshape(B, S, D)
```

```python
import functools
import math

import jax
import jax.numpy as jnp
from jax import lax
from jax.experimental import pallas as pl
from jax.experimental.pallas import tpu as pltpu

F32 = jnp.float32
BF16 = jnp.bfloat16
I32 = jnp.int32

EPS = 1e-6
N_HEADS = 8
N_KV_HEADS = 2
HEAD_DIM = 64
WINDOW = 128
ROPE_THETA = 10000.0
CONV_WIDTH = 4
LRU_C = 8.0
N_GROUPS = 4
EXPERTS_PER_GROUP = 8
N_EXPERTS = N_GROUPS * EXPERTS_PER_GROUP
NEG_INF = -1e30
LANES = 128
SUBLANES = 8

TM_INPROJ = 1024
TM_PROJ = 1024
TQ_ATTN = 2048
TL_LRU = 256
TM_PLAN = 1024
TM_EXPERT = 512
EXPERT_FF_BLOCK = 256
EXPERT_ROW_BLOCK = 128
WEIGHT_DMA_PRIORITY = 1
TM_DISPATCH = 512
N_DISPATCH_BUFS = 3
N_COMBINE_BUFS = 3
COMBINE_TILES = 1
TM_COMBINE = 256
VMEM_LIMIT = 56 * 1024 * 1024


def _rms(x, g):
    ms = jnp.mean(x * x, axis=-1, keepdims=True)
    return x * lax.rsqrt(ms + EPS) * g


_FETCH_ONCE = dict(pipeline_mode=pl.Buffered(1))


def _cparams(sem):
    return pltpu.CompilerParams(dimension_semantics=sem, vmem_limit_bytes=VMEM_LIMIT)


def _inproj_kernel(x_ref, pos_ref, g_ref, w_ref, q_ref, k_ref, v_ref, lx_ref, lg_ref, w_b):
    @pl.when(pl.program_id(0) == 0)
    def _():
        w_b[...] = w_ref[...].astype(BF16)

    h = _rms(x_ref[...], g_ref[...]).astype(BF16)
    proj = jnp.dot(h, w_b[...], preferred_element_type=F32)
    lane = lax.broadcasted_iota(I32, (1, LANES), 1)
    n_freq = HEAD_DIM // 2
    fidx = lax.broadcasted_iota(I32, (n_freq, 1), 0).astype(F32)
    inv_freq = jnp.exp(fidx * (-2.0 / HEAD_DIM * math.log(ROPE_THETA)))
    pos = pos_ref[0].astype(F32)
    cos_blocks, sin_blocks = [], []
    for c in range(pos.shape[0]):
        ang_t = inv_freq * pos[c:c + 1, :]
        cos_blocks.append(jnp.concatenate([jnp.cos(ang_t)] * (LANES // n_freq), axis=0).T)
        sin_blocks.append(jnp.concatenate([jnp.sin(ang_t)] * (LANES // n_freq), axis=0).T)
    cos = jnp.concatenate(cos_blocks, axis=0)
    sin = jnp.concatenate(sin_blocks, axis=0)
    first_half = (lane % HEAD_DIM) < (HEAD_DIM // 2)
    sin_signed = jnp.where(first_half, -sin, sin)

    def rope(t):
        partner = jnp.where(first_half, pltpu.roll(t, LANES - HEAD_DIM // 2, 1),
                            pltpu.roll(t, HEAD_DIM // 2, 1))
        return t * cos + partner * sin_signed

    attn_w = N_HEADS * HEAD_DIM
    scale = HEAD_DIM ** -0.5
    assert math.frexp(scale)[0] == 0.5, "score scale must be a power of two to fold into q exactly"
    for c in range(attn_w // LANES):
        q_ref[:, c * LANES:(c + 1) * LANES] = (
            rope(proj[:, c * LANES:(c + 1) * LANES]) * scale).astype(BF16)
    kv_w = N_KV_HEADS * HEAD_DIM
    assert kv_w == LANES
    k = rope(proj[:, attn_w:attn_w + kv_w])
    v = proj[:, attn_w + kv_w:attn_w + 2 * kv_w]
    k_ref[:, :kv_w] = k.astype(BF16)
    k_ref[:, kv_w:] = pltpu.roll(k, HEAD_DIM, 1).astype(BF16)
    v_ref[:, :kv_w] = v.astype(BF16)
    v_ref[:, kv_w:] = pltpu.roll(v, HEAD_DIM, 1).astype(BF16)
    off_lx = attn_w + 2 * kv_w
    lru_w = lx_ref.shape[1]
    lx_ref[...] = proj[:, off_lx:off_lx + lru_w]
    lg_ref[...] = proj[:, off_lx + lru_w:off_lx + 2 * lru_w]


def _inproj(x2, pos3, g_mix, w_in, lru_w):
    T, D = x2.shape
    tm = TM_INPROJ
    attn_w = N_HEADS * HEAD_DIM
    kv_w = N_KV_HEADS * HEAD_DIM
    row = lambda i: (i, 0)
    const = lambda i: (0, 0)
    return pl.pallas_call(
        _inproj_kernel,
        grid=(T // tm,),
        in_specs=[pl.BlockSpec((tm, D), row),
                  pl.BlockSpec((1, tm // LANES, LANES), lambda i: (i, 0, 0)),
                  pl.BlockSpec((1, D), const), pl.BlockSpec(w_in.shape, const, **_FETCH_ONCE)],
        out_specs=[pl.BlockSpec((tm, attn_w), row), pl.BlockSpec((tm, 2 * kv_w), row),
                   pl.BlockSpec((tm, 2 * kv_w), row), pl.BlockSpec((tm, lru_w), row),
                   pl.BlockSpec((tm, lru_w), row)],
        out_shape=[jax.ShapeDtypeStruct((T, attn_w), BF16),
                   jax.ShapeDtypeStruct((T, 2 * kv_w), BF16),
                   jax.ShapeDtypeStruct((T, 2 * kv_w), BF16), jax.ShapeDtypeStruct((T, lru_w), F32),
                   jax.ShapeDtypeStruct((T, lru_w), F32)],
        scratch_shapes=[pltpu.VMEM(w_in.shape, BF16)],
        compiler_params=_cparams(("arbitrary",)),
        name="inproj",
    )(x2, pos3, g_mix, w_in)


def _attn_kernel(sinks_ref, q_ref, k_ref, v_ref, kp_ref, vp_ref, g_ref, o_ref):
    n = pl.program_id(1)
    blk = WINDOW
    hd = HEAD_DIM
    nqb = q_ref.shape[0] // blk
    n_pairs = N_HEADS // 2
    pairs_per_group = n_pairs // N_KV_HEADS
    lane = lax.broadcasted_iota(I32, (1, LANES), 1)
    first = lane < hd
    assert WINDOW == blk
    from_prev = (lax.broadcasted_iota(I32, (blk, blk), 1) > lax.broadcasted_iota(I32, (blk, blk), 0))
    bd_row = lax.broadcasted_iota(I32, (4 * blk, LANES), 0)
    bd_lane = lax.broadcasted_iota(I32, (4 * blk, LANES), 1)
    ones_bd = ((bd_row < 2 * blk) == (bd_lane < hd)).astype(BF16)
    zero = jnp.zeros((), BF16)

    def block_diag(t):
        same, swapped = t[:, :LANES], t[:, LANES:]
        out = []
        for g in range(N_KV_HEADS):
            top = jnp.where(first, same if g == 0 else swapped, zero)
            bot = jnp.where(first, zero, swapped if g == 0 else same)
            out.append(jnp.concatenate([top, bot], axis=0))
        return out

    for j in range(nqb):
        rows = slice(j * blk, (j + 1) * blk)
        if j == 0:
            k_prev, v_prev = kp_ref[...], vp_ref[...]
        else:
            prev_rows = slice((j - 1) * blk, j * blk)
            k_prev, v_prev = k_ref[prev_rows, :], v_ref[prev_rows, :]
        k_bd = block_diag(jnp.concatenate([k_prev, k_ref[rows, :]], axis=0))
        v_bd = block_diag(jnp.concatenate([v_prev, v_ref[rows, :]], axis=0))
        outs = []
        for p in range(n_pairs):
            g = p // pairs_per_group
            qp = q_ref[rows, p * LANES:(p + 1) * LANES]
            s = lax.dot_general(qp, k_bd[g], (((1,), (1,)), ((), ())), preferred_element_type=F32)
            es, ms = [], []
            for a in range(2):
                s_prev = s[:, 2 * a * blk:(2 * a + 1) * blk]
                s_cur = s[:, (2 * a + 1) * blk:(2 * a + 2) * blk]
                if j == 0:
                    s_prev = jnp.where(n > 0, s_prev, NEG_INF)
                sa = jnp.where(from_prev, s_prev, s_cur)
                m = jnp.maximum(jnp.max(sa, axis=-1, keepdims=True), sinks_ref[2 * p + a])
                ea = jnp.exp(sa - m)
                es += [jnp.where(from_prev, ea, 0.0), jnp.where(from_prev, 0.0, ea)]
                ms.append(m)
            e = jnp.concatenate(es, axis=1).astype(BF16)
            od = jnp.dot(e, jnp.concatenate([v_bd[g], ones_bd], axis=1), preferred_element_type=F32)
            sink_term = jnp.where(first, jnp.exp(sinks_ref[2 * p] - ms[0]),
                                  jnp.exp(sinks_ref[2 * p + 1] - ms[1]))
            outs.append(od[:, :LANES] / (od[:, LANES:] + sink_term))
        attn = jnp.concatenate(outs, axis=1)
        o_ref[rows, :] = _rms(attn, g_ref[...]).astype(BF16)


def _attention(q, k, v, sinks, g_attn, B, S):
    T, attn_w = q.shape
    kv_w = k.shape[1]
    blk = WINDOW
    tq = TQ_ATTN
    nt = S // tq
    cur = lambda b, n: (b * nt + n, 0)
    prev = lambda b, n: (b * (S // blk) + jnp.maximum(n * (tq // blk) - 1, 0), 0)
    return pl.pallas_call(
        _attn_kernel,
        grid=(B, nt),
        in_specs=[pl.BlockSpec(memory_space=pltpu.SMEM),
                  pl.BlockSpec((tq, attn_w), cur),
                  pl.BlockSpec((tq, kv_w), cur), pl.BlockSpec((tq, kv_w), cur),
                  pl.BlockSpec((blk, kv_w), prev), pl.BlockSpec((blk, kv_w), prev),
                  pl.BlockSpec((1, attn_w), lambda b, n: (0, 0))],
        out_specs=pl.BlockSpec((tq, attn_w), cur),
        out_shape=jax.ShapeDtypeStruct((T, attn_w), BF16),
        compiler_params=_cparams(("arbitrary", "arbitrary")),
        name="attn",
    )(sinks, q, k, v, k, v, g_attn)


def _neg_expm1(x, exp_x):
    series = -x * (1.0 + x * (0.5 + x * (1.0 / 6.0 + x * (1.0 / 24.0))))
    return jnp.where(x > -0.02, series, 1.0 - exp_x)


def _gelu_tanh(x):
    c = math.sqrt(2.0 / math.pi)
    return x * (0.5 + 0.5 * jnp.tanh(x * (c + (c * 0.044715) * (x * x))))


def _lru_kernel(lx_ref, lg_ref, cw_ref, cb_ref, wa_ref, ba_ref, wx_ref, bx_ref, lam_ref, g_ref,
                o_ref, xpad, a_scr, b_scr, h_scr, h_carry):
    i = pl.program_id(0)
    B, tl, C = lx_ref.shape
    pad = SUBLANES

    @pl.when(i == 0)
    def _():
        xpad[:, 0:pad, :] = jnp.zeros((B, pad, C), F32)
        h_carry[...] = jnp.zeros_like(h_carry)

    @pl.when(i > 0)
    def _():
        xpad[:, 0:pad, :] = xpad[:, tl:tl + pad, :]

    xpad[:, pad:, :] = lx_ref[...]
    groups = (B * tl // SUBLANES, SUBLANES, C)
    sub = lax.broadcasted_iota(I32, groups, 1)
    x_now = lx_ref[...].reshape(groups)
    x_before = xpad[:, pl.ds(0, tl), :].reshape(groups)
    y = x_now * cw_ref[CONV_WIDTH - 1:CONV_WIDTH, :] + cb_ref[...]
    for s in range(1, CONV_WIDTH):
        shifted = pltpu.roll(jnp.where(sub >= SUBLANES - s, x_before, x_now), s, 1)
        y = y + shifted * cw_ref[CONV_WIDTH - 1 - s:CONV_WIDTH - s, :]
    y2 = y.reshape(B * tl, C)
    yb = y2.astype(BF16)
    gate_a = jnp.dot(yb, wa_ref[...], preferred_element_type=F32) + ba_ref[...]
    gate_x = jnp.dot(yb, wx_ref[...], preferred_element_type=F32) + bx_ref[...]
    r = jax.nn.sigmoid(gate_a)
    ig = jax.nn.sigmoid(gate_x)
    nl = -lam_ref[...]
    softplus = jnp.maximum(nl, 0.0) + jnp.log1p(jnp.exp(-jnp.abs(nl)))
    log_a = (-LRU_C) * r * softplus
    a_all = jnp.exp(log_a)
    one_minus_a2 = _neg_expm1(2.0 * log_a, a_all * a_all)
    b_all = one_minus_a2 * lax.rsqrt(jnp.maximum(one_minus_a2, 1e-30)) * ig * y2
    groups = (B * tl // SUBLANES, SUBLANES, C)
    sub = lax.broadcasted_iota(I32, groups, 1)
    a_cum, b_loc = a_all.reshape(groups), b_all.reshape(groups)
    d = 1
    while d < SUBLANES:
        keep = sub >= d
        a_prev = pltpu.roll(a_cum, d, 1)
        b_prev = pltpu.roll(b_loc, d, 1)
        b_loc = b_loc + jnp.where(keep, a_cum * b_prev, 0.0)
        a_cum = jnp.where(keep, a_cum * a_prev, a_cum)
        d *= 2
    a_scr[...] = a_cum.reshape(B, tl, C)
    b_scr[...] = b_loc.reshape(B, tl, C)

    def carry(g, h_prev):
        rows = pl.ds(pl.multiple_of(g * SUBLANES, SUBLANES), SUBLANES)
        h = b_scr[:, rows, :] + a_scr[:, rows, :] * h_prev
        h_scr[:, rows, :] = h
        return h[:, SUBLANES - 1:SUBLANES, :]

    h_carry[...] = lax.fori_loop(0, tl // SUBLANES, carry, h_carry[...], unroll=4)
    lru = h_scr[...].reshape(B * tl, C) * _gelu_tanh(lg_ref[...].reshape(B * tl, C))
    o_ref[...] = _rms(lru, g_ref[...]).astype(BF16).reshape(B, tl, C)


def _lru(lx3, lg3, conv_w, conv_b, wa_bd, ba, wx_bd, bx, lam, g_lru):
    B, S, C = lx3.shape
    tl = TL_LRU
    blk = lambda i: (0, i, 0)
    const = lambda i: (0, 0)
    vec = pl.BlockSpec((1, C), const)
    return pl.pallas_call(
        _lru_kernel,
        grid=(S // tl,),
        in_specs=[pl.BlockSpec((B, tl, C), blk), pl.BlockSpec((B, tl, C), blk),
                  pl.BlockSpec((CONV_WIDTH, C), const), vec,
                  pl.BlockSpec((C, C), const), vec, pl.BlockSpec((C, C), const), vec, vec, vec],
        out_specs=pl.BlockSpec((B, tl, C), blk),
        out_shape=jax.ShapeDtypeStruct((B, S, C), BF16),
        scratch_shapes=[pltpu.VMEM((B, tl + SUBLANES, C), F32),
                        pltpu.VMEM((B, tl, C), F32), pltpu.VMEM((B, tl, C), F32),
                        pltpu.VMEM((B, tl, C), F32), pltpu.VMEM((B, 1, C), F32)],
        compiler_params=_cparams(("arbitrary",)),
        name="lru",
    )(lx3, lg3, conv_w, conv_b, wa_bd, ba, wx_bd, bx, lam, g_lru)


def _outproj_kernel(x_ref, ma_ref, mb_ref, wo_ref, g_ref, wr_ref, br_ref,
                    x1_ref, h2_ref, route_ref, wo_b):
    @pl.when(pl.program_id(0) == 0)
    def _():
        wo_b[...] = wo_ref[...].astype(BF16)

    half = ma_ref.shape[1]
    x1 = (x_ref[...]
          + jnp.dot(ma_ref[...], wo_b[0:half, :], preferred_element_type=F32)
          + jnp.dot(mb_ref[...], wo_b[half:, :], preferred_element_type=F32))
    x1_ref[...] = x1
    h2 = _rms(x1, g_ref[...])
    h2_ref[...] = h2.astype(BF16).reshape(h2_ref.shape)
    h_hi = h2.astype(BF16)
    h_lo = (h2 - h_hi.astype(F32)).astype(BF16)
    hw = jnp.dot(h_hi, wr_ref[...], preferred_element_type=F32)
    logits = (hw[:, :LANES] + hw[:, LANES:]
              + jnp.dot(h_lo, wr_ref[:, :LANES], preferred_element_type=F32) + br_ref[...])
    tm = logits.shape[0]
    lane = lax.broadcasted_iota(I32, (tm, LANES), 1)
    lane_f = lane.astype(F32)
    big = float(LANES)

    def first_argmax(vals):
        m = jnp.max(vals, axis=-1, keepdims=True)
        idx = jnp.min(jnp.where(vals == m, lane_f, big), axis=-1, keepdims=True)
        return m, idx.astype(I32)

    gl = jnp.where(lane < N_GROUPS, logits, NEG_INF)
    gmax, gidx = first_argmax(gl)
    gsum = jnp.sum(jnp.where(lane < N_GROUPS, jnp.exp(gl - gmax), 0.0), axis=-1, keepdims=True)
    g_top_p = 1.0 / gsum
    lo = N_GROUPS + EXPERTS_PER_GROUP * gidx
    el = jnp.where((lane >= lo) & (lane < lo + EXPERTS_PER_GROUP), logits, NEG_INF)
    m1, i1 = first_argmax(el)
    el2 = jnp.where(lane == i1, NEG_INF, el)
    m2, i2 = first_argmax(el2)
    ratio = jnp.exp(m2 - m1)
    w1 = g_top_p / (1.0 + ratio)
    w2 = g_top_p * ratio / (1.0 + ratio)
    ids = jnp.where(lane == 0, i1 - N_GROUPS, jnp.where(lane == 1, i2 - N_GROUPS, 0)).astype(F32)
    route_ref[...] = jnp.where(lane == 2, w1, jnp.where(lane == 3, w2, ids))


def _outproj(x2, mix_a, mix_b, w_out, g_ffn, w_router, b_router):
    T, D = x2.shape
    tm = TM_PROJ
    half = mix_a.shape[1]
    row = lambda i: (i, 0)
    const = lambda i: (0, 0)
    return pl.pallas_call(
        _outproj_kernel,
        grid=(T // tm,),
        in_specs=[pl.BlockSpec((tm, D), row), pl.BlockSpec((tm, half), row),
                  pl.BlockSpec((tm, half), row), pl.BlockSpec(w_out.shape, const, **_FETCH_ONCE),
                  pl.BlockSpec((1, D), const), pl.BlockSpec((D, 2 * LANES), const),
                  pl.BlockSpec((1, LANES), const)],
        out_specs=[pl.BlockSpec((tm, D), row),
                   pl.BlockSpec((tm, D // LANES, LANES), lambda i: (i, 0, 0)),
                   pl.BlockSpec((tm, LANES), row)],
        out_shape=[jax.ShapeDtypeStruct((T, D), F32),
                   jax.ShapeDtypeStruct((T, D // LANES, LANES), BF16),
                   jax.ShapeDtypeStruct((T, LANES), F32)],
        scratch_shapes=[pltpu.VMEM(w_out.shape, BF16)],
        compiler_params=_cparams(("arbitrary",)),
        name="outproj",
    )(x2, mix_a, mix_b, w_out, g_ffn, w_router, b_router)


def _lane_cumsum(x):
    lane = lax.broadcasted_iota(I32, x.shape, 1)
    shift = 1
    while shift < LANES:
        x = x + jnp.where(lane >= shift, pltpu.roll(x, shift, 1), 0)
        shift *= 2
    return x


def _plan_kernel(eid_ref, lrow_d_ref, lrow_c_ref, te_ref, nv_ref, meta_ref, tcnt_ref, tstart_ref,
                 cnt_scr, tcnt_scr, tpre_scr):
    i = pl.program_id(0)
    tm = eid_ref.shape[0]
    tc = TM_COMBINE
    group = TM_DISPATCH // TM_COMBINE
    nsub = tm // tc
    lane = lax.broadcasted_iota(I32, (tm, LANES), 1)
    eid = eid_ref[:, 0:2].astype(I32)
    sel1 = lane == eid[:, 0:1]
    sel2 = lane == eid[:, 1:2]

    @pl.when(i == 0)
    def _():
        cnt_scr[...] = jnp.zeros_like(cnt_scr)

    onehot = (sel1 | sel2).astype(BF16)
    r = lax.broadcasted_iota(I32, (tm, tm), 0)
    c = lax.broadcasted_iota(I32, (tm, tm), 1)
    strict_lower = (r > c).astype(BF16)
    within = jnp.dot(strict_lower, onehot, preferred_element_type=F32)
    base = cnt_scr[0:1, :]
    onehot_f = onehot.astype(F32)
    row = lax.broadcasted_iota(I32, (tm, LANES), 0)
    srow = lax.broadcasted_iota(I32, (2 * SUBLANES, LANES), 0)
    counts = [jnp.sum(onehot_f[s * tc:(s + 1) * tc], axis=0, keepdims=True) for s in range(nsub)]
    cnt_mat = jnp.zeros((2 * SUBLANES, LANES), F32)
    seen = []
    total = jnp.zeros((1, LANES), F32)
    for s in range(nsub):
        seen.append(total)
        tile = i * nsub + s
        tcnt_scr[pl.ds(tile, 1), :] = counts[s]
        tpre_scr[pl.ds(tile, 1), :] = base + total
        cnt_mat = jnp.where(srow == s, counts[s], cnt_mat)
        cnt_mat = jnp.where(srow == SUBLANES + s // group, cnt_mat + counts[s], cnt_mat)
        total = total + counts[s]
    cnt_i = cnt_mat.astype(I32)
    run_start = (_lane_cumsum(cnt_i) - cnt_i).astype(F32)

    def local_rows(first_tile_of, table_row_of):
        local = None
        for s in range(nsub):
            val = within - seen[first_tile_of(s)] + run_start[table_row_of(s):table_row_of(s) + 1, :]
            local = val if local is None else jnp.where(row >= s * tc, val, local)
        l1 = jnp.sum(jnp.where(sel1, local, 0.0), axis=-1, keepdims=True)
        l2 = jnp.sum(jnp.where(sel2, local, 0.0), axis=-1, keepdims=True)
        return jnp.where(lane == 0, l1, jnp.where(lane == 1, l2, 0.0)).astype(I32)

    lrow_c_ref[...] = local_rows(lambda s: s, lambda s: s)
    lrow_d = local_rows(lambda s: (s // group) * group, lambda s: SUBLANES + s // group)
    lrow_d_ref[...] = lrow_d.astype(F32).T[0:SUBLANES, :].astype(I32)
    cnt_scr[...] = cnt_scr[...] + total

    @pl.when(i == pl.num_programs(0) - 1)
    def _():
        shift = int(math.log2(TM_EXPERT))
        cnt = cnt_scr[...].astype(I32)
        padded = ((cnt + (TM_EXPERT - 1)) >> shift) << shift
        ends = _lane_cumsum(padded)
        nt = te_ref.shape[0]
        end_tile = (ends >> shift)[0:1, :]
        tl_lane = lax.broadcasted_iota(I32, (nt, LANES), 1)
        tile = lax.broadcasted_iota(I32, (nt, LANES), 0)
        done = jnp.where((tl_lane < N_EXPERTS) & (end_tile <= tile), 1.0, 0.0)
        te = jnp.minimum(jnp.sum(done, axis=-1, keepdims=True).astype(I32), N_EXPERTS - 1)
        n_valid = jnp.sum(jnp.where(tl_lane[0:1, :] == N_EXPERTS - 1, end_tile, 0).astype(F32),
                          axis=-1, keepdims=True).astype(I32)
        nv_ref[...] = jnp.broadcast_to(n_valid, nv_ref.shape)
        seg_end = jnp.sum(jnp.where(tl_lane == te, end_tile, 0).astype(F32), axis=-1,
                          keepdims=True).astype(I32)
        nxt = jnp.sum(jnp.where((tl_lane < N_EXPERTS) & (end_tile <= seg_end), 1.0, 0.0), axis=-1,
                      keepdims=True).astype(I32)
        nxt = jnp.where(seg_end < n_valid, jnp.minimum(nxt, N_EXPERTS - 1), -1)
        own = tl_lane == te
        seg_cnt = jnp.sum(jnp.where(own, cnt[0:1, :], 0).astype(F32), axis=-1,
                          keepdims=True).astype(I32)
        seg_tiles = jnp.sum(jnp.where(own, padded[0:1, :] >> shift, 0).astype(F32), axis=-1,
                            keepdims=True).astype(I32)
        tile_in_seg = tile[:, 0:1] - (seg_end - seg_tiles)
        rows_used = jnp.clip(seg_cnt - (tile_in_seg << shift), 0, TM_EXPERT)
        te_ref[...] = jnp.where(tl_lane == 1, nxt, jnp.where(tl_lane == 2, rows_used, te))
        row = lax.broadcasted_iota(I32, cnt.shape, 0)
        meta_ref[...] = jnp.where(row == 0, ends - padded + cnt, jnp.where(row == 1, padded - cnt, 0))
        tcnt_ref[...] = tcnt_scr[...].astype(I32)
        tstart_ref[...] = (tpre_scr[...] + (ends - padded).astype(F32)[0:1, :]).astype(I32)


def _plan(eid, n_tiles):
    T = eid.shape[0]
    tm = TM_PLAN
    steps = T // tm
    n_ctiles = T // TM_COMBINE
    assert TM_DISPATCH % TM_COMBINE == 0 and tm % TM_DISPATCH == 0 and tm // TM_COMBINE <= SUBLANES
    row = lambda i: (i, 0)
    const = lambda i: (0, 0)
    return pl.pallas_call(
        _plan_kernel,
        grid=(steps,),
        in_specs=[pl.BlockSpec((tm, LANES), row)],
        out_specs=[pl.BlockSpec((SUBLANES, tm), lambda i: (0, i)), pl.BlockSpec((tm, LANES), row),
                   pl.BlockSpec((n_tiles, LANES), const), pl.BlockSpec((SUBLANES, LANES), const),
                   pl.BlockSpec((SUBLANES, LANES), const),
                   pl.BlockSpec((n_ctiles, LANES), const), pl.BlockSpec((n_ctiles, LANES), const)],
        out_shape=[jax.ShapeDtypeStruct((SUBLANES, T), I32), jax.ShapeDtypeStruct((T, LANES), I32),
                   jax.ShapeDtypeStruct((n_tiles, LANES), I32),
                   jax.ShapeDtypeStruct((SUBLANES, LANES), I32),
                   jax.ShapeDtypeStruct((SUBLANES, LANES), I32),
                   jax.ShapeDtypeStruct((n_ctiles, LANES), I32),
                   jax.ShapeDtypeStruct((n_ctiles, LANES), I32)],
        scratch_shapes=[pltpu.VMEM((SUBLANES, LANES), F32),
                        pltpu.VMEM((n_ctiles, LANES), F32), pltpu.VMEM((n_ctiles, LANES), F32)],
        compiler_params=_cparams(("arbitrary",)),
        name="plan",
    )(eid)


_PAD_BITS = tuple(1 << b for b in reversed(range(int(math.log2(TM_EXPERT)))))


def _run_bits(max_rows):
    return tuple(1 << b for b in reversed(range(int(math.log2(max_rows)) + 1)))


def _dispatch_kernel(tcnt_ref, tstart_ref, meta_ref, nv_ref, lrow_ref, h_ref,
                     xs_ref, lbuf_0, lbuf_1, lbuf_2, zeros, sem, zsem):
    i = pl.program_id(0)
    n = pl.num_programs(0)
    td = h_ref.shape[0]
    nr = 2 * td
    bufs = (lbuf_0, lbuf_1, lbuf_2)

    def wait_buf(s):
        pltpu.make_async_copy(bufs[s], xs_ref.at[pl.ds(0, nr)], sem.at[s]).wait()

    def step(cur_s):
        prev_s = (cur_s - 1) % N_DISPATCH_BUFS
        cur, prev = bufs[cur_s], bufs[prev_s]

        @pl.when(i >= N_DISPATCH_BUFS)
        def _():
            wait_buf(cur_s)

        local_row = lrow_ref[...].astype(F32)
        r = lax.broadcasted_iota(I32, (nr, td), 0).astype(F32)
        onehot = ((r == local_row[0:1, :]) | (r == local_row[1:2, :])).astype(BF16)
        sorted_rows = jnp.dot(onehot, h_ref[...].reshape(td, -1), preferred_element_type=F32)
        cur[...] = sorted_rows.astype(BF16).reshape(cur.shape)

        group = TM_DISPATCH // TM_COMBINE
        first = jnp.maximum(i - 1, 0) * group
        lo = 0
        for e in range(N_EXPERTS):
            cnt = sum(tcnt_ref[first + u, e] for u in range(group))
            cnt = jnp.where(i >= 1, cnt, 0)
            dst = tstart_ref[first, e]
            for bit in _run_bits(td):
                off = cnt & ~(2 * bit - 1)

                @pl.when((cnt & bit) != 0)
                def _():
                    pltpu.make_async_copy(prev.at[pl.ds(lo + off, bit)],
                                          xs_ref.at[pl.ds(dst + off, bit)], sem.at[prev_s]).start()
            lo = lo + cnt

        @pl.when(i == n - 1)
        def _():
            wait_buf(prev_s)
            if N_DISPATCH_BUFS > 2:
                @pl.when(n >= 3)
                def _():
                    wait_buf((cur_s - 2) % N_DISPATCH_BUFS)

    n_tiles = xs_ref.shape[0] // TM_EXPERT
    first_tail = n_tiles - N_EXPERTS

    def pad_copy(e, bit):
        cnt = meta_ref[1, e]
        dst = meta_ref[0, e] + (cnt & ~(2 * bit - 1))
        return (cnt & bit) != 0, pltpu.make_async_copy(zeros.at[pl.ds(0, bit)],
                                                       xs_ref.at[pl.ds(dst, bit)], zsem)

    def tail_copy(j):
        return j >= nv_ref[0, 0], pltpu.make_async_copy(
            zeros, xs_ref.at[pl.ds(j * TM_EXPERT, TM_EXPERT)], zsem)

    def for_each_fill(act):
        def per_expert(e, c):
            for bit in _PAD_BITS:
                pred, cp = pad_copy(e, bit)
                pl.when(pred)(functools.partial(act, cp))
            return c

        lax.fori_loop(0, N_EXPERTS, per_expert, 0)

        def per_tile(j, c):
            pred, cp = tail_copy(j)
            pl.when(pred)(functools.partial(act, cp))
            return c

        lax.fori_loop(first_tail, n_tiles, per_tile, 0)

    @pl.when(i == 0)
    def _():
        zeros[...] = jnp.zeros_like(zeros)
        for_each_fill(lambda cp: cp.start())

    phase = lax.rem(i, N_DISPATCH_BUFS)
    for s in range(N_DISPATCH_BUFS):
        pl.when(phase == s)(functools.partial(step, s))

    @pl.when(i == n - 1)
    def _():
        for_each_fill(lambda cp: cp.wait())


def _dispatch(tile_cnt, tile_start, meta, n_valid, lrow, h2, n_rows):
    T = h2.shape[0]
    slab = h2.shape[1:]
    td = TM_DISPATCH
    nt = T // td
    smem = pl.BlockSpec(memory_space=pltpu.SMEM)
    return pl.pallas_call(
        _dispatch_kernel,
        grid=(nt + 1,),
        in_specs=[smem, smem, smem, smem,
                  pl.BlockSpec((SUBLANES, td), lambda i: (0, jnp.minimum(i, nt - 1))),
                  pl.BlockSpec((td,) + slab, lambda i: (jnp.minimum(i, nt - 1), 0, 0))],
        out_specs=pl.BlockSpec(memory_space=pl.ANY),
        out_shape=jax.ShapeDtypeStruct((n_rows,) + slab, h2.dtype),
        scratch_shapes=[pltpu.VMEM((2 * td,) + slab, h2.dtype)] * N_DISPATCH_BUFS + [
                        pltpu.VMEM((TM_EXPERT,) + slab, h2.dtype),
                        pltpu.SemaphoreType.DMA((N_DISPATCH_BUFS,)), pltpu.SemaphoreType.DMA(())],
        compiler_params=_cparams(("arbitrary",)),
        name="dispatch",
    )(tile_cnt, tile_start, meta, n_valid, lrow, h2)


def _experts_kernel(te_ref, nv_ref, xs_ref, wg_ref, wu_ref, wd_ref, ys_ref,
                    wg_f, wu_f, wd_f, wg_b, wu_b, wd_b, slot_ref, sem):
    j = pl.program_id(0)
    valid = j < nv_ref[0, 0]
    new_expert = valid & ((j == 0) | (te_ref[j, 0] != te_ref[jnp.maximum(j - 1, 0), 0]))

    def fetch(e, s):
        return (pltpu.make_async_copy(wg_ref.at[e], wg_f.at[s], sem.at[s]),
                pltpu.make_async_copy(wu_ref.at[e], wu_f.at[s], sem.at[s]),
                pltpu.make_async_copy(wd_ref.at[e], wd_f.at[s], sem.at[s]))

    @pl.when(j == 0)
    def _():
        slot_ref[0] = 0
        for cp in fetch(te_ref[0, 0], 0):
            cp.start(priority=WEIGHT_DMA_PRIORITY)

    @pl.when(new_expert)
    def _():
        s = slot_ref[0]
        for cp in fetch(te_ref[j, 0], s):
            cp.wait()
        nxt = te_ref[j, 1]

        @pl.when(nxt >= 0)
        def _():
            for cp in fetch(nxt, 1 - s):
                cp.start(priority=WEIGHT_DMA_PRIORITY)

        wg_b[...] = wg_f[s].astype(BF16)
        wu_b[...] = wu_f[s].astype(BF16)
        wd_b[...] = wd_f[s].astype(BF16)
        slot_ref[0] = 1 - s

    def mlp(rows):
        x = xs_ref[0:rows].reshape(rows, -1)
        y = None
        for lo in range(0, wg_b.shape[1], EXPERT_FF_BLOCK):
            cols = slice(lo, lo + EXPERT_FF_BLOCK)
            a = jnp.dot(x, wg_b[:, cols], preferred_element_type=F32)
            u = jnp.dot(x, wu_b[:, cols], preferred_element_type=F32)
            hid = (a * jax.nn.sigmoid(a) * u).astype(BF16)
            part = jnp.dot(hid, wd_b[cols, :], preferred_element_type=F32)
            y = part if y is None else y + part
        ys_ref[0:rows] = y.astype(BF16).reshape((rows,) + ys_ref.shape[1:])
        if rows < tm:
            ys_ref[rows:] = jnp.zeros((tm - rows,) + ys_ref.shape[1:], ys_ref.dtype)

    tm = xs_ref.shape[0]
    blocks_used = (te_ref[j, 2] + (EXPERT_ROW_BLOCK - 1)) // EXPERT_ROW_BLOCK
    for nblk in range(1, tm // EXPERT_ROW_BLOCK + 1):
        pl.when(valid & (blocks_used == nblk))(functools.partial(mlp, nblk * EXPERT_ROW_BLOCK))

    @pl.when(jnp.logical_not(valid) | (blocks_used == 0))
    def _():
        ys_ref[...] = jnp.zeros_like(ys_ref)


def _experts(tile_table, n_valid, xs, wg, wu, wd):
    R = xs.shape[0]
    slab = xs.shape[1:]
    E, D, Fh = wg.shape
    tm = TM_EXPERT
    n_tiles = R // tm
    rows = lambda j, te, nv: (j, 0, 0)
    rows_in = lambda j, te, nv: (jnp.minimum(j, nv[0, 0] - 1), 0, 0)
    hbm = pl.BlockSpec(memory_space=pl.ANY)
    return pl.pallas_call(
        _experts_kernel,
        grid_spec=pltpu.PrefetchScalarGridSpec(
            num_scalar_prefetch=2,
            grid=(n_tiles,),
            in_specs=[pl.BlockSpec((tm,) + slab, rows_in), hbm, hbm, hbm],
            out_specs=pl.BlockSpec((tm,) + slab, rows),
            scratch_shapes=[pltpu.VMEM((2, D, Fh), F32), pltpu.VMEM((2, D, Fh), F32),
                            pltpu.VMEM((2, Fh, D), F32),
                            pltpu.VMEM((D, Fh), BF16), pltpu.VMEM((D, Fh), BF16),
                            pltpu.VMEM((Fh, D), BF16),
                            pltpu.SMEM((1,), I32), pltpu.SemaphoreType.DMA((2,))]),
        out_shape=jax.ShapeDtypeStruct(xs.shape, xs.dtype),
        compiler_params=_cparams(("arbitrary",)),
        name="experts",
    )(tile_table, n_valid, xs, wg, wu, wd)


def _combine_kernel(tcnt_ref, tstart_ref, lrow_ref, x1_ref, wt_ref, p_ref, gp_ref, wpg_ref, wpp_ref,
                    gf_ref, ys_ref, o_ref, *scratch):
    g = COMBINE_TILES
    nb = N_COMBINE_BUFS
    bufs = tuple(scratch[k * g:(k + 1) * g] for k in range(nb))
    wpg_b, wpp_b, sem = scratch[nb * g:]
    i = pl.program_id(0)
    n = pl.num_programs(0)
    tc = x1_ref.shape[0] // g
    nr = 2 * tc

    def request(tile, live, buf, s):
        lo = 0
        for e in range(N_EXPERTS):
            cnt = jnp.where(live, tcnt_ref[tile, e], 0)
            src = tstart_ref[tile, e]
            for bit in _run_bits(tc):
                off = cnt & ~(2 * bit - 1)

                @pl.when((cnt & bit) != 0)
                def _():
                    pltpu.make_async_copy(ys_ref.at[pl.ds(src + off, bit)],
                                          buf.at[pl.ds(lo + off, bit)], s).start()
            lo = lo + cnt

    ahead = nb - 1

    @pl.when(i == 0)
    def _():
        for k in range(ahead):
            for u in range(g):
                request(jnp.minimum(k, n - 1) * g + u, k < n, bufs[k][u], sem.at[k, u])
        wpg_b[...] = wpg_ref[...].astype(BF16)
        wpp_b[...] = wpp_ref[...].astype(BF16)

    def step(slot):
        cur, far = bufs[slot], bufs[(slot + ahead) % nb]
        for u in range(g):
            pltpu.make_async_copy(ys_ref.at[pl.ds(0, nr)], cur[u], sem.at[slot, u]).wait()
        for u in range(g):
            request(jnp.minimum(i + ahead, n - 1) * g + u, i + ahead < n, far[u],
                    sem.at[(slot + ahead) % nb, u])
        r = lax.broadcasted_iota(I32, (tc, nr), 1)
        for u in range(g):
            tok = slice(u * tc, (u + 1) * tc)
            rows = cur[u][...].reshape(nr, -1)
            local_row = lrow_ref[tok, :]
            y0 = jnp.dot((r == local_row[:, 0:1]).astype(BF16), rows, preferred_element_type=F32)
            y1 = jnp.dot((r == local_row[:, 1:2]).astype(BF16), rows, preferred_element_type=F32)
            wt = wt_ref[tok, :]
            x2 = x1_ref[tok, :] + wt[:, 2:3] * y0 + wt[:, 3:4] * y1
            hp = _rms(x2, gp_ref[...]).astype(BF16)
            gate = jax.nn.sigmoid(jnp.dot(hp, wpg_b[...], preferred_element_type=F32))
            proj = jnp.dot(p_ref[tok, :].astype(BF16), wpp_b[...], preferred_element_type=F32)
            x3 = x2 + gate * proj
            o_ref[tok, :] = _rms(x3, gf_ref[...])

    phase = lax.rem(i, nb)
    for slot in range(nb):
        pl.when(phase == slot)(functools.partial(step, slot))


def _combine(tile_cnt, tile_start, lrow, x1, wts, p2, g_ple, w_ple_gate, w_ple_proj, g_final, ys):
    T, D = x1.shape
    P = p2.shape[1]
    g = COMBINE_TILES
    tm = g * TM_COMBINE
    n = T // tm
    row = lambda i: (i, 0)
    const = lambda i: (0, 0)
    smem = pl.BlockSpec(memory_space=pltpu.SMEM)
    run_buf = pltpu.VMEM((2 * TM_COMBINE,) + ys.shape[1:], ys.dtype)
    return pl.pallas_call(
        _combine_kernel,
        grid=(n,),
        in_specs=[smem, smem, pl.BlockSpec((tm, LANES), row),
                  pl.BlockSpec((tm, D), row), pl.BlockSpec((tm, LANES), row),
                  pl.BlockSpec((tm, P), row), pl.BlockSpec((1, D), const),
                  pl.BlockSpec((D, D), const, **_FETCH_ONCE), pl.BlockSpec((P, D), const, **_FETCH_ONCE),
                  pl.BlockSpec((1, D), const), pl.BlockSpec(memory_space=pl.ANY)],
        out_specs=pl.BlockSpec((tm, D), row),
        out_shape=jax.ShapeDtypeStruct((T, D), F32),
        scratch_shapes=[run_buf] * (N_COMBINE_BUFS * g) + [
                        pltpu.VMEM((D, D), BF16), pltpu.VMEM((P, D), BF16),
                        pltpu.SemaphoreType.DMA((N_COMBINE_BUFS, g))],
        compiler_params=_cparams(("arbitrary",)),
        name="combine",
    )(tile_cnt, tile_start, lrow, x1, wts, p2, g_ple, w_ple_gate, w_ple_proj, g_final, ys)


def _block_diag(w):
    nb, d, _ = w.shape
    eye = jnp.eye(nb, dtype=w.dtype)
    return (eye[:, None, :, None] * w[:, :, None, :]).reshape(nb * d, nb * d)


def kernel(x, p, positions, g_mix, w_in, sinks, conv_w, conv_b, lru_wa, lru_ba, lru_wx, lru_bx,
           lru_lambda, g_attn_out, g_lru_out, w_out, g_ffn, w_router_group, b_router_group,
           w_router_expert, b_router_expert, w_expert_gate, w_expert_up, w_expert_down, g_ple,
           w_ple_gate, w_ple_proj, g_final):
    B, S, D = x.shape
    assert w_in.shape[0] == 1, "single-layer block only"
    T = B * S
    lru_w = conv_w.shape[-1]
    n_rows = 2 * T + N_EXPERTS * TM_EXPERT
    n_tiles = n_rows // TM_EXPERT
    pos3 = positions.reshape(T // TM_INPROJ, TM_INPROJ // LANES, LANES).astype(I32)
    x2 = x.reshape(T, D)
    q, k, v, lx, lg = _inproj(x2, pos3, g_mix[0][None], w_in[0], lru_w)
    mix_a = _attention(q, k, v, sinks[0], g_attn_out[0][None], B, S)
    mix_b = _lru(lx.reshape(B, S, lru_w), lg.reshape(B, S, lru_w), conv_w[0], conv_b[0][None],
                 _block_diag(lru_wa[0]).astype(BF16), lru_ba[0][None],
                 _block_diag(lru_wx[0]).astype(BF16), lru_bx[0][None],
                 lru_lambda[0][None], g_lru_out[0][None]).reshape(T, lru_w)
    n_router = N_GROUPS + N_EXPERTS
    w_router = jnp.pad(jnp.concatenate([w_router_group[0], w_router_expert[0]], axis=1),
                       ((0, 0), (0, LANES - n_router)))
    b_router = jnp.pad(jnp.concatenate([b_router_group[0], b_router_expert[0]]),
                       (0, LANES - n_router))[None]
    w_router_hi = w_router.astype(BF16)
    w_router_lo = (w_router - w_router_hi.astype(F32)).astype(BF16)
    x1, h2, route = _outproj(x2, mix_a, mix_b, w_out[0], g_ffn[0][None],
                             jnp.concatenate([w_router_hi, w_router_lo], axis=1), b_router)
    lrow_d, lrow_c, tile_table, n_valid, meta, tile_cnt, tile_start = _plan(route, n_tiles)
    xs = _dispatch(tile_cnt, tile_start, meta, n_valid, lrow_d, h2, n_rows)
    Fh = w_expert_gate.shape[-1]
    ys = _experts(tile_table, n_valid, xs,
                  w_expert_gate[0].reshape(N_EXPERTS, D, Fh),
                  w_expert_up[0].reshape(N_EXPERTS, D, Fh),
                  w_expert_down[0].reshape(N_EXPERTS, Fh, D))
    out = _combine(tile_cnt, tile_start, lrow_c, x1, route,
                   p[0].reshape(T, -1), g_ple[0][None], w_ple_gate[0], w_ple_proj[0], g_final[None], ys)
    return out.reshape(B, S, D)
```

```python
import functools
import math

import jax
import jax.numpy as jnp
from jax import lax
from jax.experimental import pallas as pl
from jax.experimental.pallas import tpu as pltpu

F32 = jnp.float32
BF16 = jnp.bfloat16
I32 = jnp.int32

EPS = 1e-6
N_HEADS = 8
N_KV_HEADS = 2
HEAD_DIM = 64
WINDOW = 128
ROPE_THETA = 10000.0
CONV_WIDTH = 4
LRU_C = 8.0
N_GROUPS = 4
EXPERTS_PER_GROUP = 8
N_EXPERTS = N_GROUPS * EXPERTS_PER_GROUP
NEG_INF = -1e30
LANES = 128
SUBLANES = 8

TM_INPROJ = 1024
TM_PROJ = 1024
TQ_ATTN = 2048
TL_LRU = 256
TM_PLAN = 1024
TM_EXPERT = 512
EXPERT_FF_BLOCK = 256
EXPERT_ROW_BLOCK = 128
WEIGHT_DMA_PRIORITY = 1
TM_DISPATCH = 512
N_DISPATCH_BUFS = 3
N_COMBINE_BUFS = 3
COMBINE_TILES = 1
TM_COMBINE = 256
VMEM_LIMIT = 56 * 1024 * 1024


def _rms(x, g):
    ms = jnp.mean(x * x, axis=-1, keepdims=True)
    return x * lax.rsqrt(ms + EPS) * g


_FETCH_ONCE = dict(pipeline_mode=pl.Buffered(1))


def _cparams(sem):
    return pltpu.CompilerParams(dimension_semantics=sem, vmem_limit_bytes=VMEM_LIMIT)


def _inproj_kernel(x_ref, pos_ref, g_ref, w_ref, q_ref, k_ref, v_ref, lx_ref, lg_ref, w_b):
    @pl.when(pl.program_id(0) == 0)
    def _():
        w_b[...] = w_ref[...].astype(BF16)

    h = _rms(x_ref[...], g_ref[...]).astype(BF16)
    proj = jnp.dot(h, w_b[...], preferred_element_type=F32)
    lane = lax.broadcasted_iota(I32, (1, LANES), 1)
    n_freq = HEAD_DIM // 2
    fidx = lax.broadcasted_iota(I32, (n_freq, 1), 0).astype(F32)
    inv_freq = jnp.exp(fidx * (-2.0 / HEAD_DIM * math.log(ROPE_THETA)))
    pos = pos_ref[0].astype(F32)
    cos_blocks, sin_blocks = [], []
    for c in range(pos.shape[0]):
        ang_t = inv_freq * pos[c:c + 1, :]
        cos_blocks.append(jnp.concatenate([jnp.cos(ang_t)] * (LANES // n_freq), axis=0).T)
        sin_blocks.append(jnp.concatenate([jnp.sin(ang_t)] * (LANES // n_freq), axis=0).T)
    cos = jnp.concatenate(cos_blocks, axis=0)
    sin = jnp.concatenate(sin_blocks, axis=0)
    first_half = (lane % HEAD_DIM) < (HEAD_DIM // 2)
    sin_signed = jnp.where(first_half, -sin, sin)

    def rope(t):
        partner = jnp.where(first_half, pltpu.roll(t, LANES - HEAD_DIM // 2, 1),
                            pltpu.roll(t, HEAD_DIM // 2, 1))
        return t * cos + partner * sin_signed

    attn_w = N_HEADS * HEAD_DIM
    scale = HEAD_DIM ** -0.5
    assert math.frexp(scale)[0] == 0.5, "score scale must be a power of two to fold into q exactly"
    for c in range(attn_w // LANES):
        q_ref[:, c * LANES:(c + 1) * LANES] = (
            rope(proj[:, c * LANES:(c + 1) * LANES]) * scale).astype(BF16)
    kv_w = N_KV_HEADS * HEAD_DIM
    assert kv_w == LANES
    k = rope(proj[:, attn_w:attn_w + kv_w])
    v = proj[:, attn_w + kv_w:attn_w + 2 * kv_w]
    k_ref[:, :kv_w] = k.astype(BF16)
    k_ref[:, kv_w:] = pltpu.roll(k, HEAD_DIM, 1).astype(BF16)
    v_ref[:, :kv_w] = v.astype(BF16)
    v_ref[:, kv_w:] = pltpu.roll(v, HEAD_DIM, 1).astype(BF16)
    off_lx = attn_w + 2 * kv_w
    lru_w = lx_ref.shape[1]
    lx_ref[...] = proj[:, off_lx:off_lx + lru_w]
    lg_ref[...] = proj[:, off_lx + lru_w:off_lx + 2 * lru_w]


def _inproj(x2, pos3, g_mix, w_in, lru_w):
    T, D = x2.shape
    tm = TM_INPROJ
    attn_w = N_HEADS * HEAD_DIM
    kv_w = N_KV_HEADS * HEAD_DIM
    row = lambda i: (i, 0)
    const = lambda i: (0, 0)
    return pl.pallas_call(
        _inproj_kernel,
        grid=(T // tm,),
        in_specs=[pl.BlockSpec((tm, D), row),
                  pl.BlockSpec((1, tm // LANES, LANES), lambda i: (i, 0, 0)),
                  pl.BlockSpec((1, D), const), pl.BlockSpec(w_in.shape, const, **_FETCH_ONCE)],
        out_specs=[pl.BlockSpec((tm, attn_w), row), pl.BlockSpec((tm, 2 * kv_w), row),
                   pl.BlockSpec((tm, 2 * kv_w), row), pl.BlockSpec((tm, lru_w), row),
                   pl.BlockSpec((tm, lru_w), row)],
        out_shape=[jax.ShapeDtypeStruct((T, attn_w), BF16),
                   jax.ShapeDtypeStruct((T, 2 * kv_w), BF16),
                   jax.ShapeDtypeStruct((T, 2 * kv_w), BF16), jax.ShapeDtypeStruct((T, lru_w), F32),
                   jax.ShapeDtypeStruct((T, lru_w), F32)],
        scratch_shapes=[pltpu.VMEM(w_in.shape, BF16)],
        compiler_params=_cparams(("arbitrary",)),
        name="inproj",
    )(x2, pos3, g_mix, w_in)


def _attn_kernel(sinks_ref, q_ref, k_ref, v_ref, kp_ref, vp_ref, g_ref, o_ref):
    n = pl.program_id(1)
    blk = WINDOW
    hd = HEAD_DIM
    nqb = q_ref.shape[0] // blk
    n_pairs = N_HEADS // 2
    pairs_per_group = n_pairs // N_KV_HEADS
    lane = lax.broadcasted_iota(I32, (1, LANES), 1)
    first = lane < hd
    assert WINDOW == blk
    from_prev = (lax.broadcasted_iota(I32, (blk, blk), 1) > lax.broadcasted_iota(I32, (blk, blk), 0))
    bd_row = lax.broadcasted_iota(I32, (4 * blk, LANES), 0)
    bd_lane = lax.broadcasted_iota(I32, (4 * blk, LANES), 1)
    ones_bd = ((bd_row < 2 * blk) == (bd_lane < hd)).astype(BF16)
    zero = jnp.zeros((), BF16)

    def block_diag(t):
        same, swapped = t[:, :LANES], t[:, LANES:]
        out = []
        for g in range(N_KV_HEADS):
            top = jnp.where(first, same if g == 0 else swapped, zero)
            bot = jnp.where(first, zero, swapped if g == 0 else same)
            out.append(jnp.concatenate([top, bot], axis=0))
        return out

    for j in range(nqb):
        rows = slice(j * blk, (j + 1) * blk)
        if j == 0:
            k_prev, v_prev = kp_ref[...], vp_ref[...]
        else:
            prev_rows = slice((j - 1) * blk, j * blk)
            k_prev, v_prev = k_ref[prev_rows, :], v_ref[prev_rows, :]
        k_bd = block_diag(jnp.concatenate([k_prev, k_ref[rows, :]], axis=0))
        v_bd = block_diag(jnp.concatenate([v_prev, v_ref[rows, :]], axis=0))
        outs = []
        for p in range(n_pairs):
            g = p // pairs_per_group
            qp = q_ref[rows, p * LANES:(p + 1) * LANES]
            s = lax.dot_general(qp, k_bd[g], (((1,), (1,)), ((), ())), preferred_element_type=F32)
            es, ms = [], []
            for a in range(2):
                s_prev = s[:, 2 * a * blk:(2 * a + 1) * blk]
                s_cur = s[:, (2 * a + 1) * blk:(2 * a + 2) * blk]
                if j == 0:
                    s_prev = jnp.where(n > 0, s_prev, NEG_INF)
                sa = jnp.where(from_prev, s_prev, s_cur)
                m = jnp.maximum(jnp.max(sa, axis=-1, keepdims=True), sinks_ref[2 * p + a])
                ea = jnp.exp(sa - m)
                es += [jnp.where(from_prev, ea, 0.0), jnp.where(from_prev, 0.0, ea)]
                ms.append(m)
            e = jnp.concatenate(es, axis=1).astype(BF16)
            od = jnp.dot(e, jnp.concatenate([v_bd[g], ones_bd], axis=1), preferred_element_type=F32)
            sink_term = jnp.where(first, jnp.exp(sinks_ref[2 * p] - ms[0]),
                                  jnp.exp(sinks_ref[2 * p + 1] - ms[1]))
            outs.append(od[:, :LANES] / (od[:, LANES:] + sink_term))
        attn = jnp.concatenate(outs, axis=1)
        o_ref[rows, :] = _rms(attn, g_ref[...]).astype(BF16)


def _attention(q, k, v, sinks, g_attn, B, S):
    T, attn_w = q.shape
    kv_w = k.shape[1]
    blk = WINDOW
    tq = TQ_ATTN
    nt = S // tq
    cur = lambda b, n: (b * nt + n, 0)
    prev = lambda b, n: (b * (S // blk) + jnp.maximum(n * (tq // blk) - 1, 0), 0)
    return pl.pallas_call(
        _attn_kernel,
        grid=(B, nt),
        in_specs=[pl.BlockSpec(memory_space=pltpu.SMEM),
                  pl.BlockSpec((tq, attn_w), cur),
                  pl.BlockSpec((tq, kv_w), cur), pl.BlockSpec((tq, kv_w), cur),
                  pl.BlockSpec((blk, kv_w), prev), pl.BlockSpec((blk, kv_w), prev),
                  pl.BlockSpec((1, attn_w), lambda b, n: (0, 0))],
        out_specs=pl.BlockSpec((tq, attn_w), cur),
        out_shape=jax.ShapeDtypeStruct((T, attn_w), BF16),
        compiler_params=_cparams(("arbitrary", "arbitrary")),
        name="attn",
    )(sinks, q, k, v, k, v, g_attn)


def _neg_expm1(x, exp_x):
    series = -x * (1.0 + x * (0.5 + x * (1.0 / 6.0 + x * (1.0 / 24.0))))
    return jnp.where(x > -0.02, series, 1.0 - exp_x)


def _gelu_tanh(x):
    c = math.sqrt(2.0 / math.pi)
    return x * (0.5 + 0.5 * jnp.tanh(x * (c + (c * 0.044715) * (x * x))))


def _lru_kernel(lx_ref, lg_ref, cw_ref, cb_ref, wa_ref, ba_ref, wx_ref, bx_ref, lam_ref, g_ref,
                o_ref, xpad, a_scr, b_scr, h_scr, h_carry):
    i = pl.program_id(0)
    B, tl, C = lx_ref.shape
    pad = SUBLANES

    @pl.when(i == 0)
    def _():
        xpad[:, 0:pad, :] = jnp.zeros((B, pad, C), F32)
        h_carry[...] = jnp.zeros_like(h_carry)

    @pl.when(i > 0)
    def _():
        xpad[:, 0:pad, :] = xpad[:, tl:tl + pad, :]

    xpad[:, pad:, :] = lx_ref[...]
    groups = (B * tl // SUBLANES, SUBLANES, C)
    sub = lax.broadcasted_iota(I32, groups, 1)
    x_now = lx_ref[...].reshape(groups)
    x_before = xpad[:, pl.ds(0, tl), :].reshape(groups)
    y = x_now * cw_ref[CONV_WIDTH - 1:CONV_WIDTH, :] + cb_ref[...]
    for s in range(1, CONV_WIDTH):
        shifted = pltpu.roll(jnp.where(sub >= SUBLANES - s, x_before, x_now), s, 1)
        y = y + shifted * cw_ref[CONV_WIDTH - 1 - s:CONV_WIDTH - s, :]
    y2 = y.reshape(B * tl, C)
    yb = y2.astype(BF16)
    gate_a = jnp.dot(yb, wa_ref[...], preferred_element_type=F32) + ba_ref[...]
    gate_x = jnp.dot(yb, wx_ref[...], preferred_element_type=F32) + bx_ref[...]
    r = jax.nn.sigmoid(gate_a)
    ig = jax.nn.sigmoid(gate_x)
    nl = -lam_ref[...]
    softplus = jnp.maximum(nl, 0.0) + jnp.log1p(jnp.exp(-jnp.abs(nl)))
    log_a = (-LRU_C) * r * softplus
    a_all = jnp.exp(log_a)
    one_minus_a2 = _neg_expm1(2.0 * log_a, a_all * a_all)
    b_all = one_minus_a2 * lax.rsqrt(jnp.maximum(one_minus_a2, 1e-30)) * ig * y2
    groups = (B * tl // SUBLANES, SUBLANES, C)
    sub = lax.broadcasted_iota(I32, groups, 1)
    a_cum, b_loc = a_all.reshape(groups), b_all.reshape(groups)
    d = 1
    while d < SUBLANES:
        keep = sub >= d
        a_prev = pltpu.roll(a_cum, d, 1)
        b_prev = pltpu.roll(b_loc, d, 1)
        b_loc = b_loc + jnp.where(keep, a_cum * b_prev, 0.0)
        a_cum = jnp.where(keep, a_cum * a_prev, a_cum)
        d *= 2
    a_scr[...] = a_cum.reshape(B, tl, C)
    b_scr[...] = b_loc.reshape(B, tl, C)

    def carry(g, h_prev):
        rows = pl.ds(pl.multiple_of(g * SUBLANES, SUBLANES), SUBLANES)
        h = b_scr[:, rows, :] + a_scr[:, rows, :] * h_prev
        h_scr[:, rows, :] = h
        return h[:, SUBLANES - 1:SUBLANES, :]

    h_carry[...] = lax.fori_loop(0, tl // SUBLANES, carry, h_carry[...], unroll=4)
    lru = h_scr[...].reshape(B * tl, C) * _gelu_tanh(lg_ref[...].reshape(B * tl, C))
    o_ref[...] = _rms(lru, g_ref[...]).astype(BF16).reshape(B, tl, C)


def _lru(lx3, lg3, conv_w, conv_b, wa_bd, ba, wx_bd, bx, lam, g_lru):
    B, S, C = lx3.shape
    tl = TL_LRU
    blk = lambda i: (0, i, 0)
    const = lambda i: (0, 0)
    vec = pl.BlockSpec((1, C), const)
    return pl.pallas_call(
        _lru_kernel,
        grid=(S // tl,),
        in_specs=[pl.BlockSpec((B, tl, C), blk), pl.BlockSpec((B, tl, C), blk),
                  pl.BlockSpec((CONV_WIDTH, C), const), vec,
                  pl.BlockSpec((C, C), const), vec, pl.BlockSpec((C, C), const), vec, vec, vec],
        out_specs=pl.BlockSpec((B, tl, C), blk),
        out_shape=jax.ShapeDtypeStruct((B, S, C), BF16),
        scratch_shapes=[pltpu.VMEM((B, tl + SUBLANES, C), F32),
                        pltpu.VMEM((B, tl, C), F32), pltpu.VMEM((B, tl, C), F32),
                        pltpu.VMEM((B, tl, C), F32), pltpu.VMEM((B, 1, C), F32)],
        compiler_params=_cparams(("arbitrary",)),
        name="lru",
    )(lx3, lg3, conv_w, conv_b, wa_bd, ba, wx_bd, bx, lam, g_lru)


def _outproj_kernel(x_ref, ma_ref, mb_ref, wo_ref, g_ref, wr_ref, br_ref,
                    x1_ref, h2_ref, route_ref, wo_b):
    @pl.when(pl.program_id(0) == 0)
    def _():
        wo_b[...] = wo_ref[...].astype(BF16)

    half = ma_ref.shape[1]
    x1 = (x_ref[...]
          + jnp.dot(ma_ref[...], wo_b[0:half, :], preferred_element_type=F32)
          + jnp.dot(mb_ref[...], wo_b[half:, :], preferred_element_type=F32))
    x1_ref[...] = x1
    h2 = _rms(x1, g_ref[...])
    h2_ref[...] = h2.astype(BF16).reshape(h2_ref.shape)
    h_hi = h2.astype(BF16)
    h_lo = (h2 - h_hi.astype(F32)).astype(BF16)
    hw = jnp.dot(h_hi, wr_ref[...], preferred_element_type=F32)
    logits = (hw[:, :LANES] + hw[:, LANES:]
              + jnp.dot(h_lo, wr_ref[:, :LANES], preferred_element_type=F32) + br_ref[...])
    tm = logits.shape[0]
    lane = lax.broadcasted_iota(I32, (tm, LANES), 1)
    lane_f = lane.astype(F32)
    big = float(LANES)

    def first_argmax(vals):
        m = jnp.max(vals, axis=-1, keepdims=True)
        idx = jnp.min(jnp.where(vals == m, lane_f, big), axis=-1, keepdims=True)
        return m, idx.astype(I32)

    gl = jnp.where(lane < N_GROUPS, logits, NEG_INF)
    gmax, gidx = first_argmax(gl)
    gsum = jnp.sum(jnp.where(lane < N_GROUPS, jnp.exp(gl - gmax), 0.0), axis=-1, keepdims=True)
    g_top_p = 1.0 / gsum
    lo = N_GROUPS + EXPERTS_PER_GROUP * gidx
    el = jnp.where((lane >= lo) & (lane < lo + EXPERTS_PER_GROUP), logits, NEG_INF)
    m1, i1 = first_argmax(el)
    el2 = jnp.where(lane == i1, NEG_INF, el)
    m2, i2 = first_argmax(el2)
    ratio = jnp.exp(m2 - m1)
    w1 = g_top_p / (1.0 + ratio)
    w2 = g_top_p * ratio / (1.0 + ratio)
    ids = jnp.where(lane == 0, i1 - N_GROUPS, jnp.where(lane == 1, i2 - N_GROUPS, 0)).astype(F32)
    route_ref[...] = jnp.where(lane == 2, w1, jnp.where(lane == 3, w2, ids))


def _outproj(x2, mix_a, mix_b, w_out, g_ffn, w_router, b_router):
    T, D = x2.shape
    tm = TM_PROJ
    half = mix_a.shape[1]
    row = lambda i: (i, 0)
    const = lambda i: (0, 0)
    return pl.pallas_call(
        _outproj_kernel,
        grid=(T // tm,),
        in_specs=[pl.BlockSpec((tm, D), row), pl.BlockSpec((tm, half), row),
                  pl.BlockSpec((tm, half), row), pl.BlockSpec(w_out.shape, const, **_FETCH_ONCE),
                  pl.BlockSpec((1, D), const), pl.BlockSpec((D, 2 * LANES), const),
                  pl.BlockSpec((1, LANES), const)],
        out_specs=[pl.BlockSpec((tm, D), row),
                   pl.BlockSpec((tm, D // LANES, LANES), lambda i: (i, 0, 0)),
                   pl.BlockSpec((tm, LANES), row)],
        out_shape=[jax.ShapeDtypeStruct((T, D), F32),
                   jax.ShapeDtypeStruct((T, D // LANES, LANES), BF16),
                   jax.ShapeDtypeStruct((T, LANES), F32)],
        scratch_shapes=[pltpu.VMEM(w_out.shape, BF16)],
        compiler_params=_cparams(("arbitrary",)),
        name="outproj",
    )(x2, mix_a, mix_b, w_out, g_ffn, w_router, b_router)


def _lane_cumsum(x):
    lane = lax.broadcasted_iota(I32, x.shape, 1)
    shift = 1
    while shift < LANES:
        x = x + jnp.where(lane >= shift, pltpu.roll(x, shift, 1), 0)
        shift *= 2
    return x


def _plan_kernel(eid_ref, lrow_d_ref, lrow_c_ref, te_ref, nv_ref, meta_ref, tcnt_ref, tstart_ref,
                 cnt_scr, tcnt_scr, tpre_scr):
    i = pl.program_id(0)
    tm = eid_ref.shape[0]
    tc = TM_COMBINE
    group = TM_DISPATCH // TM_COMBINE
    nsub = tm // tc
    lane = lax.broadcasted_iota(I32, (tm, LANES), 1)
    eid = eid_ref[:, 0:2].astype(I32)
    sel1 = lane == eid[:, 0:1]
    sel2 = lane == eid[:, 1:2]

    @pl.when(i == 0)
    def _():
        cnt_scr[...] = jnp.zeros_like(cnt_scr)

    onehot = (sel1 | sel2).astype(BF16)
    r = lax.broadcasted_iota(I32, (tm, tm), 0)
    c = lax.broadcasted_iota(I32, (tm, tm), 1)
    strict_lower = (r > c).astype(BF16)
    within = jnp.dot(strict_lower, onehot, preferred_element_type=F32)
    base = cnt_scr[0:1, :]
    onehot_f = onehot.astype(F32)
    row = lax.broadcasted_iota(I32, (tm, LANES), 0)
    srow = lax.broadcasted_iota(I32, (2 * SUBLANES, LANES), 0)
    counts = [jnp.sum(onehot_f[s * tc:(s + 1) * tc], axis=0, keepdims=True) for s in range(nsub)]
    cnt_mat = jnp.zeros((2 * SUBLANES, LANES), F32)
    seen = []
    total = jnp.zeros((1, LANES), F32)
    for s in range(nsub):
        seen.append(total)
        tile = i * nsub + s
        tcnt_scr[pl.ds(tile, 1), :] = counts[s]
        tpre_scr[pl.ds(tile, 1), :] = base + total
        cnt_mat = jnp.where(srow == s, counts[s], cnt_mat)
        cnt_mat = jnp.where(srow == SUBLANES + s // group, cnt_mat + counts[s], cnt_mat)
        total = total + counts[s]
    cnt_i = cnt_mat.astype(I32)
    run_start = (_lane_cumsum(cnt_i) - cnt_i).astype(F32)

    def local_rows(first_tile_of, table_row_of):
        local = None
        for s in range(nsub):
            val = within - seen[first_tile_of(s)] + run_start[table_row_of(s):table_row_of(s) + 1, :]
            local = val if local is None else jnp.where(row >= s * tc, val, local)
        l1 = jnp.sum(jnp.where(sel1, local, 0.0), axis=-1, keepdims=True)
        l2 = jnp.sum(jnp.where(sel2, local, 0.0), axis=-1, keepdims=True)
        return jnp.where(lane == 0, l1, jnp.where(lane == 1, l2, 0.0)).astype(I32)

    lrow_c_ref[...] = local_rows(lambda s: s, lambda s: s)
    lrow_d = local_rows(lambda s: (s // group) * group, lambda s: SUBLANES + s // group)
    lrow_d_ref[...] = lrow_d.astype(F32).T[0:SUBLANES, :].astype(I32)
    cnt_scr[...] = cnt_scr[...] + total

    @pl.when(i == pl.num_programs(0) - 1)
    def _():
        shift = int(math.log2(TM_EXPERT))
        cnt = cnt_scr[...].astype(I32)
        padded = ((cnt + (TM_EXPERT - 1)) >> shift) << shift
        ends = _lane_cumsum(padded)
        nt = te_ref.shape[0]
        end_tile = (ends >> shift)[0:1, :]
        tl_lane = lax.broadcasted_iota(I32, (nt, LANES), 1)
        tile = lax.broadcasted_iota(I32, (nt, LANES), 0)
        done = jnp.where((tl_lane < N_EXPERTS) & (end_tile <= tile), 1.0, 0.0)
        te = jnp.minimum(jnp.sum(done, axis=-1, keepdims=True).astype(I32), N_EXPERTS - 1)
        n_valid = jnp.sum(jnp.where(tl_lane[0:1, :] == N_EXPERTS - 1, end_tile, 0).astype(F32),
                          axis=-1, keepdims=True).astype(I32)
        nv_ref[...] = jnp.broadcast_to(n_valid, nv_ref.shape)
        seg_end = jnp.sum(jnp.where(tl_lane == te, end_tile, 0).astype(F32), axis=-1,
                          keepdims=True).astype(I32)
        nxt = jnp.sum(jnp.where((tl_lane < N_EXPERTS) & (end_tile <= seg_end), 1.0, 0.0), axis=-1,
                      keepdims=True).astype(I32)
        nxt = jnp.where(seg_end < n_valid, jnp.minimum(nxt, N_EXPERTS - 1), -1)
        own = tl_lane == te
        seg_cnt = jnp.sum(jnp.where(own, cnt[0:1, :], 0).astype(F32), axis=-1,
                          keepdims=True).astype(I32)
        seg_tiles = jnp.sum(jnp.where(own, padded[0:1, :] >> shift, 0).astype(F32), axis=-1,
                            keepdims=True).astype(I32)
        tile_in_seg = tile[:, 0:1] - (seg_end - seg_tiles)
        rows_used = jnp.clip(seg_cnt - (tile_in_seg << shift), 0, TM_EXPERT)
        te_ref[...] = jnp.where(tl_lane == 1, nxt, jnp.where(tl_lane == 2, rows_used, te))
        row = lax.broadcasted_iota(I32, cnt.shape, 0)
        meta_ref[...] = jnp.where(row == 0, ends - padded + cnt, jnp.where(row == 1, padded - cnt, 0))
        tcnt_ref[...] = tcnt_scr[...].astype(I32)
        tstart_ref[...] = (tpre_scr[...] + (ends - padded).astype(F32)[0:1, :]).astype(I32)


def _plan(eid, n_tiles):
    T = eid.shape[0]
    tm = TM_PLAN
    steps = T // tm
    n_ctiles = T // TM_COMBINE
    assert TM_DISPATCH % TM_COMBINE == 0 and tm % TM_DISPATCH == 0 and tm // TM_COMBINE <= SUBLANES
    row = lambda i: (i, 0)
    const = lambda i: (0, 0)
    return pl.pallas_call(
        _plan_kernel,
        grid=(steps,),
        in_specs=[pl.BlockSpec((tm, LANES), row)],
        out_specs=[pl.BlockSpec((SUBLANES, tm), lambda i: (0, i)), pl.BlockSpec((tm, LANES), row),
                   pl.BlockSpec((n_tiles, LANES), const), pl.BlockSpec((SUBLANES, LANES), const),
                   pl.BlockSpec((SUBLANES, LANES), const),
                   pl.BlockSpec((n_ctiles, LANES), const), pl.BlockSpec((n_ctiles, LANES), const)],
        out_shape=[jax.ShapeDtypeStruct((SUBLANES, T), I32), jax.ShapeDtypeStruct((T, LANES), I32),
                   jax.ShapeDtypeStruct((n_tiles, LANES), I32),
                   jax.ShapeDtypeStruct((SUBLANES, LANES), I32),
                   jax.ShapeDtypeStruct((SUBLANES, LANES), I32),
                   jax.ShapeDtypeStruct((n_ctiles, LANES), I32),
                   jax.ShapeDtypeStruct((n_ctiles, LANES), I32)],
        scratch_shapes=[pltpu.VMEM((SUBLANES, LANES), F32),
                        pltpu.VMEM((n_ctiles, LANES), F32), pltpu.VMEM((n_ctiles, LANES), F32)],
        compiler_params=_cparams(("arbitrary",)),
        name="plan",
    )(eid)


_PAD_BITS = tuple(1 << b for b in reversed(range(int(math.log2(TM_EXPERT)))))


def _run_bits(max_rows):
    return tuple(1 << b for b in reversed(range(int(math.log2(max_rows)) + 1)))


def _dispatch_kernel(tcnt_ref, tstart_ref, meta_ref, nv_ref, lrow_ref, h_ref,
                     xs_ref, lbuf_0, lbuf_1, lbuf_2, zeros, sem, zsem):
    i = pl.program_id(0)
    n = pl.num_programs(0)
    td = h_ref.shape[0]
    nr = 2 * td
    bufs = (lbuf_0, lbuf_1, lbuf_2)

    def wait_buf(s):
        pltpu.make_async_copy(bufs[s], xs_ref.at[pl.ds(0, nr)], sem.at[s]).wait()

    def step(cur_s):
        prev_s = (cur_s - 1) % N_DISPATCH_BUFS
        cur, prev = bufs[cur_s], bufs[prev_s]

        @pl.when(i >= N_DISPATCH_BUFS)
        def _():
            wait_buf(cur_s)

        local_row = lrow_ref[...].astype(F32)
        r = lax.broadcasted_iota(I32, (nr, td), 0).astype(F32)
        onehot = ((r == local_row[0:1, :]) | (r == local_row[1:2, :])).astype(BF16)
        sorted_rows = jnp.dot(onehot, h_ref[...].reshape(td, -1), preferred_element_type=F32)
        cur[...] = sorted_rows.astype(BF16).reshape(cur.shape)

        group = TM_DISPATCH // TM_COMBINE
        first = jnp.maximum(i - 1, 0) * group
        lo = 0
        for e in range(N_EXPERTS):
            cnt = sum(tcnt_ref[first + u, e] for u in range(group))
            cnt = jnp.where(i >= 1, cnt, 0)
            dst = tstart_ref[first, e]
            for bit in _run_bits(td):
                off = cnt & ~(2 * bit - 1)

                @pl.when((cnt & bit) != 0)
                def _():
                    pltpu.make_async_copy(prev.at[pl.ds(lo + off, bit)],
                                          xs_ref.at[pl.ds(dst + off, bit)],
                                          sem.at[prev_s]).start(priority=e % 2)
            lo = lo + cnt

        @pl.when(i == n - 1)
        def _():
            wait_buf(prev_s)
            if N_DISPATCH_BUFS > 2:
                @pl.when(n >= 3)
                def _():
                    wait_buf((cur_s - 2) % N_DISPATCH_BUFS)

    n_tiles = xs_ref.shape[0] // TM_EXPERT
    first_tail = n_tiles - N_EXPERTS

    def pad_copy(e, bit):
        cnt = meta_ref[1, e]
        dst = meta_ref[0, e] + (cnt & ~(2 * bit - 1))
        return (cnt & bit) != 0, pltpu.make_async_copy(zeros.at[pl.ds(0, bit)],
                                                       xs_ref.at[pl.ds(dst, bit)], zsem)

    def tail_copy(j):
        return j >= nv_ref[0, 0], pltpu.make_async_copy(
            zeros, xs_ref.at[pl.ds(j * TM_EXPERT, TM_EXPERT)], zsem)

    def for_each_fill(act):
        def per_expert(e, c):
            for bit in _PAD_BITS:
                pred, cp = pad_copy(e, bit)
                pl.when(pred)(functools.partial(act, cp))
            return c

        lax.fori_loop(0, N_EXPERTS, per_expert, 0)

        def per_tile(j, c):
            pred, cp = tail_copy(j)
            pl.when(pred)(functools.partial(act, cp))
            return c

        lax.fori_loop(first_tail, n_tiles, per_tile, 0)

    @pl.when(i == 0)
    def _():
        zeros[...] = jnp.zeros_like(zeros)
        for_each_fill(lambda cp: cp.start())

    phase = lax.rem(i, N_DISPATCH_BUFS)
    for s in range(N_DISPATCH_BUFS):
        pl.when(phase == s)(functools.partial(step, s))

    @pl.when(i == n - 1)
    def _():
        for_each_fill(lambda cp: cp.wait())


def _dispatch(tile_cnt, tile_start, meta, n_valid, lrow, h2, n_rows):
    T = h2.shape[0]
    slab = h2.shape[1:]
    td = TM_DISPATCH
    nt = T // td
    smem = pl.BlockSpec(memory_space=pltpu.SMEM)
    return pl.pallas_call(
        _dispatch_kernel,
        grid=(nt + 1,),
        in_specs=[smem, smem, smem, smem,
                  pl.BlockSpec((SUBLANES, td), lambda i: (0, jnp.minimum(i, nt - 1))),
                  pl.BlockSpec((td,) + slab, lambda i: (jnp.minimum(i, nt - 1), 0, 0))],
        out_specs=pl.BlockSpec(memory_space=pl.ANY),
        out_shape=jax.ShapeDtypeStruct((n_rows,) + slab, h2.dtype),
        scratch_shapes=[pltpu.VMEM((2 * td,) + slab, h2.dtype)] * N_DISPATCH_BUFS + [
                        pltpu.VMEM((TM_EXPERT,) + slab, h2.dtype),
                        pltpu.SemaphoreType.DMA((N_DISPATCH_BUFS,)), pltpu.SemaphoreType.DMA(())],
        compiler_params=_cparams(("arbitrary",)),
        name="dispatch",
    )(tile_cnt, tile_start, meta, n_valid, lrow, h2)


def _experts_kernel(te_ref, nv_ref, xs_ref, wg_ref, wu_ref, wd_ref, ys_ref,
                    wg_f, wu_f, wd_f, wg_b, wu_b, wd_b, slot_ref, sem):
    j = pl.program_id(0)
    valid = j < nv_ref[0, 0]
    new_expert = valid & ((j == 0) | (te_ref[j, 0] != te_ref[jnp.maximum(j - 1, 0), 0]))

    def fetch(e, s):
        return (pltpu.make_async_copy(wg_ref.at[e], wg_f.at[s], sem.at[s]),
                pltpu.make_async_copy(wu_ref.at[e], wu_f.at[s], sem.at[s]),
                pltpu.make_async_copy(wd_ref.at[e], wd_f.at[s], sem.at[s]))

    @pl.when(j == 0)
    def _():
        slot_ref[0] = 0
        for cp in fetch(te_ref[0, 0], 0):
            cp.start(priority=WEIGHT_DMA_PRIORITY)

    @pl.when(new_expert)
    def _():
        s = slot_ref[0]
        for cp in fetch(te_ref[j, 0], s):
            cp.wait()
        nxt = te_ref[j, 1]

        @pl.when(nxt >= 0)
        def _():
            for cp in fetch(nxt, 1 - s):
                cp.start(priority=WEIGHT_DMA_PRIORITY)

        wg_b[...] = wg_f[s].astype(BF16)
        wu_b[...] = wu_f[s].astype(BF16)
        wd_b[...] = wd_f[s].astype(BF16)
        slot_ref[0] = 1 - s

    def mlp(rows):
        x = xs_ref[0:rows].reshape(rows, -1)
        y = None
        for lo in range(0, wg_b.shape[1], EXPERT_FF_BLOCK):
            cols = slice(lo, lo + EXPERT_FF_BLOCK)
            a = jnp.dot(x, wg_b[:, cols], preferred_element_type=F32)
            u = jnp.dot(x, wu_b[:, cols], preferred_element_type=F32)
            hid = (a * jax.nn.sigmoid(a) * u).astype(BF16)
            part = jnp.dot(hid, wd_b[cols, :], preferred_element_type=F32)
            y = part if y is None else y + part
        ys_ref[0:rows] = y.astype(BF16).reshape((rows,) + ys_ref.shape[1:])
        if rows < tm:
            ys_ref[rows:] = jnp.zeros((tm - rows,) + ys_ref.shape[1:], ys_ref.dtype)

    tm = xs_ref.shape[0]
    blocks_used = (te_ref[j, 2] + (EXPERT_ROW_BLOCK - 1)) // EXPERT_ROW_BLOCK
    for nblk in range(1, tm // EXPERT_ROW_BLOCK + 1):
        pl.when(valid & (blocks_used == nblk))(functools.partial(mlp, nblk * EXPERT_ROW_BLOCK))

    @pl.when(jnp.logical_not(valid) | (blocks_used == 0))
    def _():
        ys_ref[...] = jnp.zeros_like(ys_ref)


def _experts(tile_table, n_valid, xs, wg, wu, wd):
    R = xs.shape[0]
    slab = xs.shape[1:]
    E, D, Fh = wg.shape
    tm = TM_EXPERT
    n_tiles = R // tm
    rows = lambda j, te, nv: (j, 0, 0)
    rows_in = lambda j, te, nv: (jnp.minimum(j, nv[0, 0] - 1), 0, 0)
    hbm = pl.BlockSpec(memory_space=pl.ANY)
    return pl.pallas_call(
        _experts_kernel,
        grid_spec=pltpu.PrefetchScalarGridSpec(
            num_scalar_prefetch=2,
            grid=(n_tiles,),
            in_specs=[pl.BlockSpec((tm,) + slab, rows_in), hbm, hbm, hbm],
            out_specs=pl.BlockSpec((tm,) + slab, rows),
            scratch_shapes=[pltpu.VMEM((2, D, Fh), F32), pltpu.VMEM((2, D, Fh), F32),
                            pltpu.VMEM((2, Fh, D), F32),
                            pltpu.VMEM((D, Fh), BF16), pltpu.VMEM((D, Fh), BF16),
                            pltpu.VMEM((Fh, D), BF16),
                            pltpu.SMEM((1,), I32), pltpu.SemaphoreType.DMA((2,))]),
        out_shape=jax.ShapeDtypeStruct(xs.shape, xs.dtype),
        compiler_params=_cparams(("arbitrary",)),
        name="experts",
    )(tile_table, n_valid, xs, wg, wu, wd)


def _combine_kernel(tcnt_ref, tstart_ref, lrow_ref, x1_ref, wt_ref, p_ref, gp_ref, wpg_ref, wpp_ref,
                    gf_ref, ys_ref, o_ref, *scratch):
    g = COMBINE_TILES
    nb = N_COMBINE_BUFS
    bufs = tuple(scratch[k * g:(k + 1) * g] for k in range(nb))
    wpg_b, wpp_b, sem = scratch[nb * g:]
    i = pl.program_id(0)
    n = pl.num_programs(0)
    tc = x1_ref.shape[0] // g
    nr = 2 * tc

    def request(tile, live, buf, s):
        lo = 0
        for e in range(N_EXPERTS):
            cnt = jnp.where(live, tcnt_ref[tile, e], 0)
            src = tstart_ref[tile, e]
            for bit in _run_bits(tc):
                off = cnt & ~(2 * bit - 1)

                @pl.when((cnt & bit) != 0)
                def _():
                    pltpu.make_async_copy(ys_ref.at[pl.ds(src + off, bit)],
                                          buf.at[pl.ds(lo + off, bit)],
                                          s).start(priority=e % 2)
            lo = lo + cnt

    ahead = nb - 1

    @pl.when(i == 0)
    def _():
        for k in range(ahead):
            for u in range(g):
                request(jnp.minimum(k, n - 1) * g + u, k < n, bufs[k][u], sem.at[k, u])
        wpg_b[...] = wpg_ref[...].astype(BF16)
        wpp_b[...] = wpp_ref[...].astype(BF16)

    def step(slot):
        cur, far = bufs[slot], bufs[(slot + ahead) % nb]
        for u in range(g):
            pltpu.make_async_copy(ys_ref.at[pl.ds(0, nr)], cur[u], sem.at[slot, u]).wait()
        for u in range(g):
            request(jnp.minimum(i + ahead, n - 1) * g + u, i + ahead < n, far[u],
                    sem.at[(slot + ahead) % nb, u])
        r = lax.broadcasted_iota(I32, (tc, nr), 1)
        for u in range(g):
            tok = slice(u * tc, (u + 1) * tc)
            rows = cur[u][...].reshape(nr, -1)
            local_row = lrow_ref[tok, :]
            y0 = jnp.dot((r == local_row[:, 0:1]).astype(BF16), rows, preferred_element_type=F32)
            y1 = jnp.dot((r == local_row[:, 1:2]).astype(BF16), rows, preferred_element_type=F32)
            wt = wt_ref[tok, :]
            x2 = x1_ref[tok, :] + wt[:, 2:3] * y0 + wt[:, 3:4] * y1
            hp = _rms(x2, gp_ref[...]).astype(BF16)
            gate = jax.nn.sigmoid(jnp.dot(hp, wpg_b[...], preferred_element_type=F32))
            proj = jnp.dot(p_ref[tok, :].astype(BF16), wpp_b[...], preferred_element_type=F32)
            x3 = x2 + gate * proj
            o_ref[tok, :] = _rms(x3, gf_ref[...])

    phase = lax.rem(i, nb)
    for slot in range(nb):
        pl.when(phase == slot)(functools.partial(step, slot))


def _combine(tile_cnt, tile_start, lrow, x1, wts, p2, g_ple, w_ple_gate, w_ple_proj, g_final, ys):
    T, D = x1.shape
    P = p2.shape[1]
    g = COMBINE_TILES
    tm = g * TM_COMBINE
    n = T // tm
    row = lambda i: (i, 0)
    const = lambda i: (0, 0)
    smem = pl.BlockSpec(memory_space=pltpu.SMEM)
    run_buf = pltpu.VMEM((2 * TM_COMBINE,) + ys.shape[1:], ys.dtype)
    return pl.pallas_call(
        _combine_kernel,
        grid=(n,),
        in_specs=[smem, smem, pl.BlockSpec((tm, LANES), row),
                  pl.BlockSpec((tm, D), row), pl.BlockSpec((tm, LANES), row),
                  pl.BlockSpec((tm, P), row), pl.BlockSpec((1, D), const),
                  pl.BlockSpec((D, D), const, **_FETCH_ONCE), pl.BlockSpec((P, D), const, **_FETCH_ONCE),
                  pl.BlockSpec((1, D), const), pl.BlockSpec(memory_space=pl.ANY)],
        out_specs=pl.BlockSpec((tm, D), row),
        out_shape=jax.ShapeDtypeStruct((T, D), F32),
        scratch_shapes=[run_buf] * (N_COMBINE_BUFS * g) + [
                        pltpu.VMEM((D, D), BF16), pltpu.VMEM((P, D), BF16),
                        pltpu.SemaphoreType.DMA((N_COMBINE_BUFS, g))],
        compiler_params=_cparams(("arbitrary",)),
        name="combine",
    )(tile_cnt, tile_start, lrow, x1, wts, p2, g_ple, w_ple_gate, w_ple_proj, g_final, ys)


def _block_diag(w):
    nb, d, _ = w.shape
    eye = jnp.eye(nb, dtype=w.dtype)
    return (eye[:, None, :, None] * w[:, :, None, :]).reshape(nb * d, nb * d)


def kernel(x, p, positions, g_mix, w_in, sinks, conv_w, conv_b, lru_wa, lru_ba, lru_wx, lru_bx,
           lru_lambda, g_attn_out, g_lru_out, w_out, g_ffn, w_router_group, b_router_group,
           w_router_expert, b_router_expert, w_expert_gate, w_expert_up, w_expert_down, g_ple,
           w_ple_gate, w_ple_proj, g_final):
    B, S, D = x.shape
    assert w_in.shape[0] == 1, "single-layer block only"
    T = B * S
    lru_w = conv_w.shape[-1]
    n_rows = 2 * T + N_EXPERTS * TM_EXPERT
    n_tiles = n_rows // TM_EXPERT
    pos3 = positions.reshape(T // TM_INPROJ, TM_INPROJ // LANES, LANES).astype(I32)
    x2 = x.reshape(T, D)
    q, k, v, lx, lg = _inproj(x2, pos3, g_mix[0][None], w_in[0], lru_w)
    mix_a = _attention(q, k, v, sinks[0], g_attn_out[0][None], B, S)
    mix_b = _lru(lx.reshape(B, S, lru_w), lg.reshape(B, S, lru_w), conv_w[0], conv_b[0][None],
                 _block_diag(lru_wa[0]).astype(BF16), lru_ba[0][None],
                 _block_diag(lru_wx[0]).astype(BF16), lru_bx[0][None],
                 lru_lambda[0][None], g_lru_out[0][None]).reshape(T, lru_w)
    n_router = N_GROUPS + N_EXPERTS
    w_router = jnp.pad(jnp.concatenate([w_router_group[0], w_router_expert[0]], axis=1),
                       ((0, 0), (0, LANES - n_router)))
    b_router = jnp.pad(jnp.concatenate([b_router_group[0], b_router_expert[0]]),
                       (0, LANES - n_router))[None]
    w_router_hi = w_router.astype(BF16)
    w_router_lo = (w_router - w_router_hi.astype(F32)).astype(BF16)
    x1, h2, route = _outproj(x2, mix_a, mix_b, w_out[0], g_ffn[0][None],
                             jnp.concatenate([w_router_hi, w_router_lo], axis=1), b_router)
    lrow_d, lrow_c, tile_table, n_valid, meta, tile_cnt, tile_start = _plan(route, n_tiles)
    xs = _dispatch(tile_cnt, tile_start, meta, n_valid, lrow_d, h2, n_rows)
    Fh = w_expert_gate.shape[-1]
    ys = _experts(tile_table, n_valid, xs,
                  w_expert_gate[0].reshape(N_EXPERTS, D, Fh),
                  w_expert_up[0].reshape(N_EXPERTS, D, Fh),
                  w_expert_down[0].reshape(N_EXPERTS, Fh, D))
    out = _combine(tile_cnt, tile_start, lrow_c, x1, route,
                   p[0].reshape(T, -1), g_ple[0][None], w_ple_gate[0], w_ple_proj[0], g_final[None], ys)
    return out.reshape(B, S, D)
```
